```python
import jax, jax.numpy as jnp
from jax import lax
import numpy as np

D_MODEL = 1024
BATCH = 8
SEQ = 4096
DEPTH = 4

PLE_DIM = 256
D_FF = 2816
POOL_WINDOWS = (2, 4, 8, 16)
POOL_GROUP = D_MODEL // 8
POOL_WIDTH = len(POOL_WINDOWS) * POOL_GROUP
SGU_HEADS = 4
SGU_HEAD_DIM = D_MODEL // 8
SGU_WIDTH = SGU_HEADS * SGU_HEAD_DIM
CHUNK = 128
CONV_WIDTH = D_MODEL // 2
CONV_KERNEL = 31
N_BRANCH = 3
OFF_POOL = 0
OFF_U = OFF_POOL + POOL_WIDTH
OFF_V = OFF_U + SGU_WIDTH
OFF_GLU_A = OFF_V + SGU_WIDTH
OFF_GLU_B = OFF_GLU_A + CONV_WIDTH
OFF_GATES = OFF_GLU_B + CONV_WIDTH
IN_COLS = OFF_GATES + N_BRANCH * D_MODEL
EPS = 1e-6

kernel_name = "hybrid_pool_sgu_conformer_gated_trunk"


def rms_norm(x, g):
    xf = x.astype(jnp.float32)
    y = xf * lax.rsqrt(jnp.mean(xf * xf, axis=-1, keepdims=True) + EPS)
    return (y * g.astype(jnp.float32)).astype(x.dtype)


def layer_norm(x, g, b):
    xf = x.astype(jnp.float32)
    mu = jnp.mean(xf, axis=-1, keepdims=True)
    xc = xf - mu
    y = xc * lax.rsqrt(jnp.mean(xc * xc, axis=-1, keepdims=True) + EPS)
    return (y * g.astype(jnp.float32) + b.astype(jnp.float32)).astype(x.dtype)


def swiglu(x, w_gate, w_up, w_down):
    return (jax.nn.silu(x @ w_gate) * (x @ w_up)) @ w_down


def pool_mixer(xa, w_grp, scale):
    b_, s_, _ = xa.shape
    xf = xa.astype(jnp.float32).reshape(b_, s_, len(POOL_WINDOWS), POOL_GROUP)
    csum = jnp.cumsum(xf, axis=1)
    pos = jnp.arange(s_)
    outs = []
    for gi, w in enumerate(POOL_WINDOWS):
        cg = csum[:, :, gi]
        shifted = jnp.pad(cg, ((0, 0), (w, 0), (0, 0)))[:, :s_]
        count = jnp.minimum(pos + 1, w).astype(jnp.float32)[None, :, None]
        outs.append((cg - shifted) / count - xf[:, :, gi])
    pooled = jnp.stack(outs, axis=2).astype(xa.dtype)
    y = jnp.einsum('bsgi,gio->bsgo', pooled, w_grp)
    return y.reshape(b_, s_, POOL_WIDTH) * scale


def sgu_mixer(u, v, ln_g, ln_b, w_s, b_s):
    b_, s_, _ = u.shape
    u = jax.nn.gelu(u, approximate=False)
    v = layer_norm(jax.nn.gelu(v, approximate=False), ln_g, ln_b)
    vc = v.reshape(b_, s_ // CHUNK, CHUNK, SGU_HEADS, SGU_HEAD_DIM)
    causal = jnp.tril(jnp.ones((CHUNK, CHUNK), dtype=bool))
    w = jnp.where(causal[None], w_s, jnp.zeros_like(w_s))
    s = jnp.einsum('hts,bcshd->bcthd', w, vc) + b_s.T[None, None, :, :, None]
    return u * s.reshape(b_, s_, SGU_WIDTH)


def conv_mixer(a, gate, dw_k, dw_b, ln_g, ln_b):
    xg = a * jax.nn.sigmoid(gate)
    y = lax.conv_general_dilated(
        xg, dw_k, window_strides=(1,), padding=[(CONV_KERNEL - 1, 0)],
        dimension_numbers=('NWC', 'WIO', 'NWC'), feature_group_count=CONV_WIDTH) + dw_b
    return jax.nn.silu(layer_norm(y, ln_g, ln_b))


def _fwd_setup_inputs(seed: int = 0) -> dict:
    key = jax.random.key(seed)
    ks = iter(jax.random.split(key, 40))
    L = DEPTH

    def w(shape, fan_in):
        return jax.random.normal(next(ks), shape, jnp.float32) * (fan_in ** -0.5)

    def gain(shape):
        return 1.0 + 0.05 * jax.random.normal(next(ks), shape, jnp.float32)

    def bias(shape):
        return 0.02 * jax.random.normal(next(ks), shape, jnp.float32)

    return {
        "x": jax.random.normal(next(ks), (BATCH, SEQ, D_MODEL), jnp.float32),
        "p": jax.random.normal(next(ks), (DEPTH, BATCH, SEQ, PLE_DIM), jnp.float32),
        "ffn1_pre_g": gain((L, D_MODEL)),
        "ffn1_w_gate": w((L, D_MODEL, D_FF), D_MODEL),
        "ffn1_w_up": w((L, D_MODEL, D_FF), D_MODEL),
        "ffn1_w_down": w((L, D_FF, D_MODEL), D_FF),
        "ffn1_post_g": gain((L, D_MODEL)),
        "mix_pre_g": gain((L, D_MODEL)),
        "w_in": w((L, D_MODEL, IN_COLS), D_MODEL),
        "pool_w": w((L, len(POOL_WINDOWS), POOL_GROUP, POOL_GROUP), POOL_GROUP),
        "pool_scale": gain((L, POOL_WIDTH)),
        "w_pool_out": w((L, POOL_WIDTH, D_MODEL), POOL_WIDTH),
        "sgu_ln_g": gain((L, SGU_WIDTH)),
        "sgu_ln_b": bias((L, SGU_WIDTH)),
        "sgu_w_s": w((L, SGU_HEADS, CHUNK, CHUNK), CHUNK),
        "sgu_b_s": 1.0 + 0.1 * jax.random.normal(next(ks), (L, SGU_HEADS, CHUNK), jnp.float32),
        "w_sgu_out": w((L, SGU_WIDTH, D_MODEL), SGU_WIDTH),
        "conv_dw_k": w((L, CONV_KERNEL, 1, CONV_WIDTH), CONV_KERNEL),
        "conv_dw_b": bias((L, CONV_WIDTH)),
        "conv_ln_g": gain((L, CONV_WIDTH)),
        "conv_ln_b": bias((L, CONV_WIDTH)),
        "w_conv_out": w((L, CONV_WIDTH, D_MODEL), CONV_WIDTH),
        "w_out": w((L, D_MODEL, D_MODEL), D_MODEL),
        "mix_post_g": gain((L, D_MODEL)),
        "ffn2_pre_g": gain((L, D_MODEL)),
        "ffn2_w_gate": w((L, D_MODEL, D_FF), D_MODEL),
        "ffn2_w_up": w((L, D_MODEL, D_FF), D_MODEL),
        "ffn2_w_down": w((L, D_FF, D_MODEL), D_FF),
        "ffn2_post_g": gain((L, D_MODEL)),
        "ple_w_proj": w((L, PLE_DIM, D_MODEL), PLE_DIM),
        "ple_pre_g": gain((L, D_MODEL)),
        "ple_w_gate": w((L, D_MODEL, D_MODEL), D_MODEL),
        "ple_post_g": gain((L, D_MODEL)),
    }


def _fwd_reference(x, p, ffn1_pre_g, ffn1_w_gate, ffn1_w_up, ffn1_w_down, ffn1_post_g,
              mix_pre_g, w_in, pool_w, pool_scale, w_pool_out,
              sgu_ln_g, sgu_ln_b, sgu_w_s, sgu_b_s, w_sgu_out,
              conv_dw_k, conv_dw_b, conv_ln_g, conv_ln_b, w_conv_out,
              w_out, mix_post_g,
              ffn2_pre_g, ffn2_w_gate, ffn2_w_up, ffn2_w_down, ffn2_post_g,
              ple_w_proj, ple_pre_g, ple_w_gate, ple_post_g):
    h = x
    b_, s_, _ = x.shape
    for i in range(DEPTH):
        f = swiglu(rms_norm(h, ffn1_pre_g[i]), ffn1_w_gate[i], ffn1_w_up[i], ffn1_w_down[i])
        h = h + 0.5 * rms_norm(f, ffn1_post_g[i])

        n = rms_norm(h, mix_pre_g[i])
        z = n @ w_in[i]
        z_pool = z[..., OFF_POOL:OFF_U]
        z_u = z[..., OFF_U:OFF_V]
        z_v = z[..., OFF_V:OFF_GLU_A]
        z_a = z[..., OFF_GLU_A:OFF_GLU_B]
        z_b = z[..., OFF_GLU_B:OFF_GATES]
        gates = jax.nn.sigmoid(z[..., OFF_GATES:]).reshape(b_, s_, N_BRANCH, D_MODEL)

        y_pool = pool_mixer(z_pool, pool_w[i], pool_scale[i]) @ w_pool_out[i]
        y_sgu = sgu_mixer(z_u, z_v, sgu_ln_g[i], sgu_ln_b[i], sgu_w_s[i], sgu_b_s[i]) @ w_sgu_out[i]
        y_conv = conv_mixer(z_a, z_b, conv_dw_k[i], conv_dw_b[i],
                            conv_ln_g[i], conv_ln_b[i]) @ w_conv_out[i]
        merged = gates[:, :, 0] * y_pool + gates[:, :, 1] * y_sgu + gates[:, :, 2] * y_conv
        h = h + rms_norm(merged @ w_out[i], mix_post_g[i])

        f = swiglu(rms_norm(h, ffn2_pre_g[i]), ffn2_w_gate[i], ffn2_w_up[i], ffn2_w_down[i])
        h = h + 0.5 * rms_norm(f, ffn2_post_g[i])

        e = p[i] @ ple_w_proj[i]
        g = jax.nn.sigmoid(rms_norm(h, ple_pre_g[i]) @ ple_w_gate[i])
        h = h + rms_norm(g * e, ple_post_g[i])
    return h


import jax as _jax
import jax.numpy as _jnp

TWIN_FORMAT = 'train_step'
FWD_PARAMS = ['x', 'p', 'ffn1_pre_g', 'ffn1_w_gate', 'ffn1_w_up', 'ffn1_w_down', 'ffn1_post_g', 'mix_pre_g', 'w_in', 'pool_w', 'pool_scale', 'w_pool_out', 'sgu_ln_g', 'sgu_ln_b', 'sgu_w_s', 'sgu_b_s', 'w_sgu_out', 'conv_dw_k', 'conv_dw_b', 'conv_ln_g', 'conv_ln_b', 'w_conv_out', 'w_out', 'mix_post_g', 'ffn2_pre_g', 'ffn2_w_gate', 'ffn2_w_up', 'ffn2_w_down', 'ffn2_post_g', 'ple_w_proj', 'ple_pre_g', 'ple_w_gate', 'ple_post_g']
TWIN_WEIGHTS = ['ffn1_pre_g', 'ffn1_w_gate', 'ffn1_w_up', 'ffn1_w_down', 'ffn1_post_g', 'mix_pre_g', 'w_in', 'pool_w', 'pool_scale', 'w_pool_out', 'sgu_ln_g', 'sgu_ln_b', 'sgu_w_s', 'sgu_b_s', 'w_sgu_out', 'conv_dw_k', 'conv_dw_b', 'conv_ln_g', 'conv_ln_b', 'w_conv_out', 'w_out', 'mix_post_g', 'ffn2_pre_g', 'ffn2_w_gate', 'ffn2_w_up', 'ffn2_w_down', 'ffn2_post_g', 'ple_w_proj', 'ple_pre_g', 'ple_w_gate', 'ple_post_g']
TWIN_DIFF_INPUT = 'x'
TWIN_INPUTS = ['x', 'p', 'ffn1_pre_g', 'ffn1_w_gate', 'ffn1_w_up', 'ffn1_w_down', 'ffn1_post_g', 'mix_pre_g', 'w_in', 'pool_w', 'pool_scale', 'w_pool_out', 'sgu_ln_g', 'sgu_ln_b', 'sgu_w_s', 'sgu_b_s', 'w_sgu_out', 'conv_dw_k', 'conv_dw_b', 'conv_ln_g', 'conv_ln_b', 'w_conv_out', 'w_out', 'mix_post_g', 'ffn2_pre_g', 'ffn2_w_gate', 'ffn2_w_up', 'ffn2_w_down', 'ffn2_post_g', 'ple_w_proj', 'ple_pre_g', 'ple_w_gate', 'ple_post_g', 'loss_target', 'm_ffn1_pre_g', 'm_ffn1_w_gate', 'm_ffn1_w_up', 'm_ffn1_w_down', 'm_ffn1_post_g', 'm_mix_pre_g', 'm_w_in', 'm_pool_w', 'm_pool_scale', 'm_w_pool_out', 'm_sgu_ln_g', 'm_sgu_ln_b', 'm_sgu_w_s', 'm_sgu_b_s', 'm_w_sgu_out', 'm_conv_dw_k', 'm_conv_dw_b', 'm_conv_ln_g', 'm_conv_ln_b', 'm_w_conv_out', 'm_w_out', 'm_mix_post_g', 'm_ffn2_pre_g', 'm_ffn2_w_gate', 'm_ffn2_w_up', 'm_ffn2_w_down', 'm_ffn2_post_g', 'm_ple_w_proj', 'm_ple_pre_g', 'm_ple_w_gate', 'm_ple_post_g', 'v_ffn1_pre_g', 'v_ffn1_w_gate', 'v_ffn1_w_up', 'v_ffn1_w_down', 'v_ffn1_post_g', 'v_mix_pre_g', 'v_w_in', 'v_pool_w', 'v_pool_scale', 'v_w_pool_out', 'v_sgu_ln_g', 'v_sgu_ln_b', 'v_sgu_w_s', 'v_sgu_b_s', 'v_w_sgu_out', 'v_conv_dw_k', 'v_conv_dw_b', 'v_conv_ln_g', 'v_conv_ln_b', 'v_w_conv_out', 'v_w_out', 'v_mix_post_g', 'v_ffn2_pre_g', 'v_ffn2_w_gate', 'v_ffn2_w_up', 'v_ffn2_w_down', 'v_ffn2_post_g', 'v_ple_w_proj', 'v_ple_pre_g', 'v_ple_w_gate', 'v_ple_post_g']
TWIN_OUTPUTS = ['loss', 'grad_x', 'grad_ffn1_pre_g', 'grad_ffn1_w_gate', 'grad_ffn1_w_up', 'grad_ffn1_w_down', 'grad_ffn1_post_g', 'grad_mix_pre_g', 'grad_w_in', 'grad_pool_w', 'grad_pool_scale', 'grad_w_pool_out', 'grad_sgu_ln_g', 'grad_sgu_ln_b', 'grad_sgu_w_s', 'grad_sgu_b_s', 'grad_w_sgu_out', 'grad_conv_dw_k', 'grad_conv_dw_b', 'grad_conv_ln_g', 'grad_conv_ln_b', 'grad_w_conv_out', 'grad_w_out', 'grad_mix_post_g', 'grad_ffn2_pre_g', 'grad_ffn2_w_gate', 'grad_ffn2_w_up', 'grad_ffn2_w_down', 'grad_ffn2_post_g', 'grad_ple_w_proj', 'grad_ple_pre_g', 'grad_ple_w_gate', 'grad_ple_post_g', 'delta_ffn1_pre_g', 'delta_ffn1_w_gate', 'delta_ffn1_w_up', 'delta_ffn1_w_down', 'delta_ffn1_post_g', 'delta_mix_pre_g', 'delta_w_in', 'delta_pool_w', 'delta_pool_scale', 'delta_w_pool_out', 'delta_sgu_ln_g', 'delta_sgu_ln_b', 'delta_sgu_w_s', 'delta_sgu_b_s', 'delta_w_sgu_out', 'delta_conv_dw_k', 'delta_conv_dw_b', 'delta_conv_ln_g', 'delta_conv_ln_b', 'delta_w_conv_out', 'delta_w_out', 'delta_mix_post_g', 'delta_ffn2_pre_g', 'delta_ffn2_w_gate', 'delta_ffn2_w_up', 'delta_ffn2_w_down', 'delta_ffn2_post_g', 'delta_ple_w_proj', 'delta_ple_pre_g', 'delta_ple_w_gate', 'delta_ple_post_g', 'new_m_ffn1_pre_g', 'new_m_ffn1_w_gate', 'new_m_ffn1_w_up', 'new_m_ffn1_w_down', 'new_m_ffn1_post_g', 'new_m_mix_pre_g', 'new_m_w_in', 'new_m_pool_w', 'new_m_pool_scale', 'new_m_w_pool_out', 'new_m_sgu_ln_g', 'new_m_sgu_ln_b', 'new_m_sgu_w_s', 'new_m_sgu_b_s', 'new_m_w_sgu_out', 'new_m_conv_dw_k', 'new_m_conv_dw_b', 'new_m_conv_ln_g', 'new_m_conv_ln_b', 'new_m_w_conv_out', 'new_m_w_out', 'new_m_mix_post_g', 'new_m_ffn2_pre_g', 'new_m_ffn2_w_gate', 'new_m_ffn2_w_up', 'new_m_ffn2_w_down', 'new_m_ffn2_post_g', 'new_m_ple_w_proj', 'new_m_ple_pre_g', 'new_m_ple_w_gate', 'new_m_ple_post_g', 'new_v_ffn1_pre_g', 'new_v_ffn1_w_gate', 'new_v_ffn1_w_up', 'new_v_ffn1_w_down', 'new_v_ffn1_post_g', 'new_v_mix_pre_g', 'new_v_w_in', 'new_v_pool_w', 'new_v_pool_scale', 'new_v_w_pool_out', 'new_v_sgu_ln_g', 'new_v_sgu_ln_b', 'new_v_sgu_w_s', 'new_v_sgu_b_s', 'new_v_w_sgu_out', 'new_v_conv_dw_k', 'new_v_conv_dw_b', 'new_v_conv_ln_g', 'new_v_conv_ln_b', 'new_v_w_conv_out', 'new_v_w_out', 'new_v_mix_post_g', 'new_v_ffn2_pre_g', 'new_v_ffn2_w_gate', 'new_v_ffn2_w_up', 'new_v_ffn2_w_down', 'new_v_ffn2_post_g', 'new_v_ple_w_proj', 'new_v_ple_pre_g', 'new_v_ple_w_gate', 'new_v_ple_post_g']
TWIN_LEAF_KINDS = {'loss': 'loss', 'grad_x': 'grad_x', 'grad_ffn1_pre_g': 'grad_w', 'grad_ffn1_w_gate': 'grad_w', 'grad_ffn1_w_up': 'grad_w', 'grad_ffn1_w_down': 'grad_w', 'grad_ffn1_post_g': 'grad_w', 'grad_mix_pre_g': 'grad_w', 'grad_w_in': 'grad_w', 'grad_pool_w': 'grad_w', 'grad_pool_scale': 'grad_w', 'grad_w_pool_out': 'grad_w', 'grad_sgu_ln_g': 'grad_w', 'grad_sgu_ln_b': 'grad_w', 'grad_sgu_w_s': 'grad_w', 'grad_sgu_b_s': 'grad_w', 'grad_w_sgu_out': 'grad_w', 'grad_conv_dw_k': 'grad_w', 'grad_conv_dw_b': 'grad_w', 'grad_conv_ln_g': 'grad_w', 'grad_conv_ln_b': 'grad_w', 'grad_w_conv_out': 'grad_w', 'grad_w_out': 'grad_w', 'grad_mix_post_g': 'grad_w', 'grad_ffn2_pre_g': 'grad_w', 'grad_ffn2_w_gate': 'grad_w', 'grad_ffn2_w_up': 'grad_w', 'grad_ffn2_w_down': 'grad_w', 'grad_ffn2_post_g': 'grad_w', 'grad_ple_w_proj': 'grad_w', 'grad_ple_pre_g': 'grad_w', 'grad_ple_w_gate': 'grad_w', 'grad_ple_post_g': 'grad_w', 'delta_ffn1_pre_g': 'delta_w', 'delta_ffn1_w_gate': 'delta_w', 'delta_ffn1_w_up': 'delta_w', 'delta_ffn1_w_down': 'delta_w', 'delta_ffn1_post_g': 'delta_w', 'delta_mix_pre_g': 'delta_w', 'delta_w_in': 'delta_w', 'delta_pool_w': 'delta_w', 'delta_pool_scale': 'delta_w', 'delta_w_pool_out': 'delta_w', 'delta_sgu_ln_g': 'delta_w', 'delta_sgu_ln_b': 'delta_w', 'delta_sgu_w_s': 'delta_w', 'delta_sgu_b_s': 'delta_w', 'delta_w_sgu_out': 'delta_w', 'delta_conv_dw_k': 'delta_w', 'delta_conv_dw_b': 'delta_w', 'delta_conv_ln_g': 'delta_w', 'delta_conv_ln_b': 'delta_w', 'delta_w_conv_out': 'delta_w', 'delta_w_out': 'delta_w', 'delta_mix_post_g': 'delta_w', 'delta_ffn2_pre_g': 'delta_w', 'delta_ffn2_w_gate': 'delta_w', 'delta_ffn2_w_up': 'delta_w', 'delta_ffn2_w_down': 'delta_w', 'delta_ffn2_post_g': 'delta_w', 'delta_ple_w_proj': 'delta_w', 'delta_ple_pre_g': 'delta_w', 'delta_ple_w_gate': 'delta_w', 'delta_ple_post_g': 'delta_w', 'new_m_ffn1_pre_g': 'new_m', 'new_m_ffn1_w_gate': 'new_m', 'new_m_ffn1_w_up': 'new_m', 'new_m_ffn1_w_down': 'new_m', 'new_m_ffn1_post_g': 'new_m', 'new_m_mix_pre_g': 'new_m', 'new_m_w_in': 'new_m', 'new_m_pool_w': 'new_m', 'new_m_pool_scale': 'new_m', 'new_m_w_pool_out': 'new_m', 'new_m_sgu_ln_g': 'new_m', 'new_m_sgu_ln_b': 'new_m', 'new_m_sgu_w_s': 'new_m', 'new_m_sgu_b_s': 'new_m', 'new_m_w_sgu_out': 'new_m', 'new_m_conv_dw_k': 'new_m', 'new_m_conv_dw_b': 'new_m', 'new_m_conv_ln_g': 'new_m', 'new_m_conv_ln_b': 'new_m', 'new_m_w_conv_out': 'new_m', 'new_m_w_out': 'new_m', 'new_m_mix_post_g': 'new_m', 'new_m_ffn2_pre_g': 'new_m', 'new_m_ffn2_w_gate': 'new_m', 'new_m_ffn2_w_up': 'new_m', 'new_m_ffn2_w_down': 'new_m', 'new_m_ffn2_post_g': 'new_m', 'new_m_ple_w_proj': 'new_m', 'new_m_ple_pre_g': 'new_m', 'new_m_ple_w_gate': 'new_m', 'new_m_ple_post_g': 'new_m', 'new_v_ffn1_pre_g': 'new_v', 'new_v_ffn1_w_gate': 'new_v', 'new_v_ffn1_w_up': 'new_v', 'new_v_ffn1_w_down': 'new_v', 'new_v_ffn1_post_g': 'new_v', 'new_v_mix_pre_g': 'new_v', 'new_v_w_in': 'new_v', 'new_v_pool_w': 'new_v', 'new_v_pool_scale': 'new_v', 'new_v_w_pool_out': 'new_v', 'new_v_sgu_ln_g': 'new_v', 'new_v_sgu_ln_b': 'new_v', 'new_v_sgu_w_s': 'new_v', 'new_v_sgu_b_s': 'new_v', 'new_v_w_sgu_out': 'new_v', 'new_v_conv_dw_k': 'new_v', 'new_v_conv_dw_b': 'new_v', 'new_v_conv_ln_g': 'new_v', 'new_v_conv_ln_b': 'new_v', 'new_v_w_conv_out': 'new_v', 'new_v_w_out': 'new_v', 'new_v_mix_post_g': 'new_v', 'new_v_ffn2_pre_g': 'new_v', 'new_v_ffn2_w_gate': 'new_v', 'new_v_ffn2_w_up': 'new_v', 'new_v_ffn2_w_down': 'new_v', 'new_v_ffn2_post_g': 'new_v', 'new_v_ple_w_proj': 'new_v', 'new_v_ple_pre_g': 'new_v', 'new_v_ple_w_gate': 'new_v', 'new_v_ple_post_g': 'new_v'}


def _forward(args):
    return _fwd_reference(*[args[k] for k in FWD_PARAMS])


def _output_shape():
    out = _jax.eval_shape(lambda: _forward(_fwd_setup_inputs(0)))
    return out.shape, out.dtype

N_MICROBATCH = 1
ADAM_LR = 0.001
ADAM_B1 = 0.9
ADAM_B2 = 0.999
ADAM_EPS = 1e-08
ADAM_WD = 0.01
ADAM_STEP = 10
PER_EXAMPLE_BATCH_AXIS = {'x': 0, 'p': 1, 'loss_target': 0}
SHARED_INPUTS = []
_WEIGHT_DTYPES = {'ffn1_pre_g': _jnp.float32, 'ffn1_w_gate': _jnp.float32, 'ffn1_w_up': _jnp.float32, 'ffn1_w_down': _jnp.float32, 'ffn1_post_g': _jnp.float32, 'mix_pre_g': _jnp.float32, 'w_in': _jnp.float32, 'pool_w': _jnp.float32, 'pool_scale': _jnp.float32, 'w_pool_out': _jnp.float32, 'sgu_ln_g': _jnp.float32, 'sgu_ln_b': _jnp.float32, 'sgu_w_s': _jnp.float32, 'sgu_b_s': _jnp.float32, 'w_sgu_out': _jnp.float32, 'conv_dw_k': _jnp.float32, 'conv_dw_b': _jnp.float32, 'conv_ln_g': _jnp.float32, 'conv_ln_b': _jnp.float32, 'w_conv_out': _jnp.float32, 'w_out': _jnp.float32, 'mix_post_g': _jnp.float32, 'ffn2_pre_g': _jnp.float32, 'ffn2_w_gate': _jnp.float32, 'ffn2_w_up': _jnp.float32, 'ffn2_w_down': _jnp.float32, 'ffn2_post_g': _jnp.float32, 'ple_w_proj': _jnp.float32, 'ple_pre_g': _jnp.float32, 'ple_w_gate': _jnp.float32, 'ple_post_g': _jnp.float32}
MOMENT_SCALE = {'ffn1_pre_g': 1.000113e+00, 'ffn1_w_gate': 3.894843e-01, 'ffn1_w_up': 4.328422e-01, 'ffn1_w_down': 7.201189e-01, 'ffn1_post_g': 7.439551e+00, 'mix_pre_g': 1.681049e+00, 'w_in': 6.964353e-01, 'pool_w': 1.660367e+00, 'pool_scale': 1.780892e+00, 'w_pool_out': 1.184940e+00, 'sgu_ln_g': 4.545538e-01, 'sgu_ln_b': 4.899680e-01, 'sgu_w_s': 4.326459e-01, 'sgu_b_s': 6.355974e-01, 'w_sgu_out': 2.620896e+00, 'conv_dw_k': 9.383483e-01, 'conv_dw_b': 1.142113e+01, 'conv_ln_g': 4.453951e+00, 'conv_ln_b': 7.110368e+00, 'w_conv_out': 1.870824e+00, 'w_out': 3.292028e+00, 'mix_post_g': 3.225027e+01, 'ffn2_pre_g': 8.525554e-01, 'ffn2_w_gate': 3.069533e-01, 'ffn2_w_up': 3.985483e-01, 'ffn2_w_down': 6.645511e-01, 'ffn2_post_g': 7.906830e+00, 'ple_w_proj': 9.657320e-01, 'ple_pre_g': 3.320879e-01, 'ple_w_gate': 3.352148e-01, 'ple_post_g': 3.199753e+01}


def _to_microbatches(a, axis):
    t = _jnp.moveaxis(a, axis, 0)
    t = t.reshape((N_MICROBATCH, t.shape[0] // N_MICROBATCH) + t.shape[1:])
    return _jnp.moveaxis(t, 1, axis + 1)


def setup_inputs(seed: int = 0) -> dict:
    inp = _fwd_setup_inputs(seed)
    key = _jax.random.fold_in(_jax.random.key(seed), 7919)
    shape, _ = _output_shape()
    out = dict(inp)
    out["loss_target"] = _jax.random.normal(_jax.random.fold_in(key, 0), shape, _jnp.float32)
    for i, name in enumerate(TWIN_WEIGHTS):
        w = inp[name].astype(_jnp.float32)
        if MOMENT_SCALE is None:
            s = _jnp.sqrt(_jnp.mean(_jnp.square(w)) + 1e-30)
        else:
            s = MOMENT_SCALE[name]
        km, kv = _jax.random.split(_jax.random.fold_in(key, i + 1))
        out[name] = w
        out["m_" + name] = s * _jax.random.normal(km, w.shape, _jnp.float32)
        out["v_" + name] = (s * s) * _jax.random.uniform(kv, w.shape, _jnp.float32, 0.5, 1.5)
    if N_MICROBATCH > 1:
        for name, axis in PER_EXAMPLE_BATCH_AXIS.items():
            out[name] = _to_microbatches(out[name], axis)
    return {'x': out['x'], 'p': out['p'], 'ffn1_pre_g': out['ffn1_pre_g'], 'ffn1_w_gate': out['ffn1_w_gate'], 'ffn1_w_up': out['ffn1_w_up'], 'ffn1_w_down': out['ffn1_w_down'], 'ffn1_post_g': out['ffn1_post_g'], 'mix_pre_g': out['mix_pre_g'], 'w_in': out['w_in'], 'pool_w': out['pool_w'], 'pool_scale': out['pool_scale'], 'w_pool_out': out['w_pool_out'], 'sgu_ln_g': out['sgu_ln_g'], 'sgu_ln_b': out['sgu_ln_b'], 'sgu_w_s': out['sgu_w_s'], 'sgu_b_s': out['sgu_b_s'], 'w_sgu_out': out['w_sgu_out'], 'conv_dw_k': out['conv_dw_k'], 'conv_dw_b': out['conv_dw_b'], 'conv_ln_g': out['conv_ln_g'], 'conv_ln_b': out['conv_ln_b'], 'w_conv_out': out['w_conv_out'], 'w_out': out['w_out'], 'mix_post_g': out['mix_post_g'], 'ffn2_pre_g': out['ffn2_pre_g'], 'ffn2_w_gate': out['ffn2_w_gate'], 'ffn2_w_up': out['ffn2_w_up'], 'ffn2_w_down': out['ffn2_w_down'], 'ffn2_post_g': out['ffn2_post_g'], 'ple_w_proj': out['ple_w_proj'], 'ple_pre_g': out['ple_pre_g'], 'ple_w_gate': out['ple_w_gate'], 'ple_post_g': out['ple_post_g'], 'loss_target': out['loss_target'], 'm_ffn1_pre_g': out['m_ffn1_pre_g'], 'm_ffn1_w_gate': out['m_ffn1_w_gate'], 'm_ffn1_w_up': out['m_ffn1_w_up'], 'm_ffn1_w_down': out['m_ffn1_w_down'], 'm_ffn1_post_g': out['m_ffn1_post_g'], 'm_mix_pre_g': out['m_mix_pre_g'], 'm_w_in': out['m_w_in'], 'm_pool_w': out['m_pool_w'], 'm_pool_scale': out['m_pool_scale'], 'm_w_pool_out': out['m_w_pool_out'], 'm_sgu_ln_g': out['m_sgu_ln_g'], 'm_sgu_ln_b': out['m_sgu_ln_b'], 'm_sgu_w_s': out['m_sgu_w_s'], 'm_sgu_b_s': out['m_sgu_b_s'], 'm_w_sgu_out': out['m_w_sgu_out'], 'm_conv_dw_k': out['m_conv_dw_k'], 'm_conv_dw_b': out['m_conv_dw_b'], 'm_conv_ln_g': out['m_conv_ln_g'], 'm_conv_ln_b': out['m_conv_ln_b'], 'm_w_conv_out': out['m_w_conv_out'], 'm_w_out': out['m_w_out'], 'm_mix_post_g': out['m_mix_post_g'], 'm_ffn2_pre_g': out['m_ffn2_pre_g'], 'm_ffn2_w_gate': out['m_ffn2_w_gate'], 'm_ffn2_w_up': out['m_ffn2_w_up'], 'm_ffn2_w_down': out['m_ffn2_w_down'], 'm_ffn2_post_g': out['m_ffn2_post_g'], 'm_ple_w_proj': out['m_ple_w_proj'], 'm_ple_pre_g': out['m_ple_pre_g'], 'm_ple_w_gate': out['m_ple_w_gate'], 'm_ple_post_g': out['m_ple_post_g'], 'v_ffn1_pre_g': out['v_ffn1_pre_g'], 'v_ffn1_w_gate': out['v_ffn1_w_gate'], 'v_ffn1_w_up': out['v_ffn1_w_up'], 'v_ffn1_w_down': out['v_ffn1_w_down'], 'v_ffn1_post_g': out['v_ffn1_post_g'], 'v_mix_pre_g': out['v_mix_pre_g'], 'v_w_in': out['v_w_in'], 'v_pool_w': out['v_pool_w'], 'v_pool_scale': out['v_pool_scale'], 'v_w_pool_out': out['v_w_pool_out'], 'v_sgu_ln_g': out['v_sgu_ln_g'], 'v_sgu_ln_b': out['v_sgu_ln_b'], 'v_sgu_w_s': out['v_sgu_w_s'], 'v_sgu_b_s': out['v_sgu_b_s'], 'v_w_sgu_out': out['v_w_sgu_out'], 'v_conv_dw_k': out['v_conv_dw_k'], 'v_conv_dw_b': out['v_conv_dw_b'], 'v_conv_ln_g': out['v_conv_ln_g'], 'v_conv_ln_b': out['v_conv_ln_b'], 'v_w_conv_out': out['v_w_conv_out'], 'v_w_out': out['v_w_out'], 'v_mix_post_g': out['v_mix_post_g'], 'v_ffn2_pre_g': out['v_ffn2_pre_g'], 'v_ffn2_w_gate': out['v_ffn2_w_gate'], 'v_ffn2_w_up': out['v_ffn2_w_up'], 'v_ffn2_w_down': out['v_ffn2_w_down'], 'v_ffn2_post_g': out['v_ffn2_post_g'], 'v_ple_w_proj': out['v_ple_w_proj'], 'v_ple_pre_g': out['v_ple_pre_g'], 'v_ple_w_gate': out['v_ple_w_gate'], 'v_ple_post_g': out['v_ple_post_g']}


def _loss(weights, diff, rest, loss_target):
    with _jax.named_scope("forward"):
        args = {**rest, TWIN_DIFF_INPUT: diff, **{k: w.astype(_WEIGHT_DTYPES[k]) for k, w in weights.items()}}
        y = _forward(args)
    with _jax.named_scope("loss_head"):
        err = _jnp.square(y.astype(_jnp.float32) - loss_target)
        return 0.5 * _jnp.sum(_jnp.mean(err, axis=-1)) if err.ndim else 0.5 * err


def _adamw(w, g, m, v):
    m = ADAM_B1 * m + (1.0 - ADAM_B1) * g
    v = ADAM_B2 * v + (1.0 - ADAM_B2) * _jnp.square(g)
    m_hat = m / (1.0 - ADAM_B1 ** ADAM_STEP)
    v_hat = v / (1.0 - ADAM_B2 ** ADAM_STEP)
    delta = -ADAM_LR * (m_hat / (_jnp.sqrt(v_hat) + ADAM_EPS) + ADAM_WD * w)
    return delta, m, v


def reference(x, p, ffn1_pre_g, ffn1_w_gate, ffn1_w_up, ffn1_w_down, ffn1_post_g, mix_pre_g, w_in, pool_w, pool_scale, w_pool_out, sgu_ln_g, sgu_ln_b, sgu_w_s, sgu_b_s, w_sgu_out, conv_dw_k, conv_dw_b, conv_ln_g, conv_ln_b, w_conv_out, w_out, mix_post_g, ffn2_pre_g, ffn2_w_gate, ffn2_w_up, ffn2_w_down, ffn2_post_g, ple_w_proj, ple_pre_g, ple_w_gate, ple_post_g, loss_target, m_ffn1_pre_g, m_ffn1_w_gate, m_ffn1_w_up, m_ffn1_w_down, m_ffn1_post_g, m_mix_pre_g, m_w_in, m_pool_w, m_pool_scale, m_w_pool_out, m_sgu_ln_g, m_sgu_ln_b, m_sgu_w_s, m_sgu_b_s, m_w_sgu_out, m_conv_dw_k, m_conv_dw_b, m_conv_ln_g, m_conv_ln_b, m_w_conv_out, m_w_out, m_mix_post_g, m_ffn2_pre_g, m_ffn2_w_gate, m_ffn2_w_up, m_ffn2_w_down, m_ffn2_post_g, m_ple_w_proj, m_ple_pre_g, m_ple_w_gate, m_ple_post_g, v_ffn1_pre_g, v_ffn1_w_gate, v_ffn1_w_up, v_ffn1_w_down, v_ffn1_post_g, v_mix_pre_g, v_w_in, v_pool_w, v_pool_scale, v_w_pool_out, v_sgu_ln_g, v_sgu_ln_b, v_sgu_w_s, v_sgu_b_s, v_w_sgu_out, v_conv_dw_k, v_conv_dw_b, v_conv_ln_g, v_conv_ln_b, v_w_conv_out, v_w_out, v_mix_post_g, v_ffn2_pre_g, v_ffn2_w_gate, v_ffn2_w_up, v_ffn2_w_down, v_ffn2_post_g, v_ple_w_proj, v_ple_pre_g, v_ple_w_gate, v_ple_post_g):
    given = dict(x=x, p=p, ffn1_pre_g=ffn1_pre_g, ffn1_w_gate=ffn1_w_gate, ffn1_w_up=ffn1_w_up, ffn1_w_down=ffn1_w_down, ffn1_post_g=ffn1_post_g, mix_pre_g=mix_pre_g, w_in=w_in, pool_w=pool_w, pool_scale=pool_scale, w_pool_out=w_pool_out, sgu_ln_g=sgu_ln_g, sgu_ln_b=sgu_ln_b, sgu_w_s=sgu_w_s, sgu_b_s=sgu_b_s, w_sgu_out=w_sgu_out, conv_dw_k=conv_dw_k, conv_dw_b=conv_dw_b, conv_ln_g=conv_ln_g, conv_ln_b=conv_ln_b, w_conv_out=w_conv_out, w_out=w_out, mix_post_g=mix_post_g, ffn2_pre_g=ffn2_pre_g, ffn2_w_gate=ffn2_w_gate, ffn2_w_up=ffn2_w_up, ffn2_w_down=ffn2_w_down, ffn2_post_g=ffn2_post_g, ple_w_proj=ple_w_proj, ple_pre_g=ple_pre_g, ple_w_gate=ple_w_gate, ple_post_g=ple_post_g, loss_target=loss_target, m_ffn1_pre_g=m_ffn1_pre_g, m_ffn1_w_gate=m_ffn1_w_gate, m_ffn1_w_up=m_ffn1_w_up, m_ffn1_w_down=m_ffn1_w_down, m_ffn1_post_g=m_ffn1_post_g, m_mix_pre_g=m_mix_pre_g, m_w_in=m_w_in, m_pool_w=m_pool_w, m_pool_scale=m_pool_scale, m_w_pool_out=m_w_pool_out, m_sgu_ln_g=m_sgu_ln_g, m_sgu_ln_b=m_sgu_ln_b, m_sgu_w_s=m_sgu_w_s, m_sgu_b_s=m_sgu_b_s, m_w_sgu_out=m_w_sgu_out, m_conv_dw_k=m_conv_dw_k, m_conv_dw_b=m_conv_dw_b, m_conv_ln_g=m_conv_ln_g, m_conv_ln_b=m_conv_ln_b, m_w_conv_out=m_w_conv_out, m_w_out=m_w_out, m_mix_post_g=m_mix_post_g, m_ffn2_pre_g=m_ffn2_pre_g, m_ffn2_w_gate=m_ffn2_w_gate, m_ffn2_w_up=m_ffn2_w_up, m_ffn2_w_down=m_ffn2_w_down, m_ffn2_post_g=m_ffn2_post_g, m_ple_w_proj=m_ple_w_proj, m_ple_pre_g=m_ple_pre_g, m_ple_w_gate=m_ple_w_gate, m_ple_post_g=m_ple_post_g, v_ffn1_pre_g=v_ffn1_pre_g, v_ffn1_w_gate=v_ffn1_w_gate, v_ffn1_w_up=v_ffn1_w_up, v_ffn1_w_down=v_ffn1_w_down, v_ffn1_post_g=v_ffn1_post_g, v_mix_pre_g=v_mix_pre_g, v_w_in=v_w_in, v_pool_w=v_pool_w, v_pool_scale=v_pool_scale, v_w_pool_out=v_w_pool_out, v_sgu_ln_g=v_sgu_ln_g, v_sgu_ln_b=v_sgu_ln_b, v_sgu_w_s=v_sgu_w_s, v_sgu_b_s=v_sgu_b_s, v_w_sgu_out=v_w_sgu_out, v_conv_dw_k=v_conv_dw_k, v_conv_dw_b=v_conv_dw_b, v_conv_ln_g=v_conv_ln_g, v_conv_ln_b=v_conv_ln_b, v_w_conv_out=v_w_conv_out, v_w_out=v_w_out, v_mix_post_g=v_mix_post_g, v_ffn2_pre_g=v_ffn2_pre_g, v_ffn2_w_gate=v_ffn2_w_gate, v_ffn2_w_up=v_ffn2_w_up, v_ffn2_w_down=v_ffn2_w_down, v_ffn2_post_g=v_ffn2_post_g, v_ple_w_proj=v_ple_w_proj, v_ple_pre_g=v_ple_pre_g, v_ple_w_gate=v_ple_w_gate, v_ple_post_g=v_ple_post_g)
    weights = {n: given[n] for n in TWIN_WEIGHTS}
    shared = {n: given[n] for n in SHARED_INPUTS}
    per_example = {n: given[n] for n in ['x', 'p']}
    grad_fn = _jax.value_and_grad(_loss, argnums=(0, 1))

    def one_microbatch(ex, loss_target):
        ex = dict(ex)
        diff = ex.pop(TWIN_DIFF_INPUT)
        return grad_fn(weights, diff, {**shared, **ex}, loss_target)

    if N_MICROBATCH == 1:
        loss, (grad_w, grad_x) = one_microbatch(per_example, given["loss_target"])
    else:
        def body(carry, xs):
            loss_sum, grad_sum = carry
            l_k, (gw_k, gx_k) = one_microbatch(xs[0], xs[1])
            with _jax.named_scope("update"):
                return (loss_sum + l_k, _jax.tree.map(_jnp.add, grad_sum, gw_k)), gx_k

        init = (_jnp.zeros((), _jnp.float32), _jax.tree.map(_jnp.zeros_like, weights))
        (loss, grad_w), grad_x = _jax.lax.scan(body, init, (per_example, given["loss_target"]))
    with _jax.named_scope("update"):
        delta_w, new_m, new_v = {}, {}, {}
        for n in TWIN_WEIGHTS:
            delta_w[n], new_m[n], new_v[n] = _adamw(weights[n], grad_w[n], given["m_" + n], given["v_" + n])
    return (loss, grad_x, *[grad_w[n] for n in TWIN_WEIGHTS], *[delta_w[n] for n in TWIN_WEIGHTS],
            *[new_m[n] for n in TWIN_WEIGHTS], *[new_v[n] for n in TWIN_WEIGHTS])
```

```python
import math

import jax
import jax.numpy as jnp
from jax import lax
from jax.experimental import pallas as pl
from jax.experimental.pallas import tpu as pltpu

BF = jnp.bfloat16
F32 = jnp.float32
EPS = 1e-6
D_MODEL = 1024
LANES = 128
N_CHIPS = 4
FFN_SHARD = 704
FFN_SHARD_PAD = 768
POOL_WINDOWS = (2, 4, 8, 16)
SGU_HEADS = 4
CHUNK = 128
CONV_TAPS = 31
CONV_PAD = 32
ROW_TILE = 512
VMEM_LIMIT_BYTES = 56 * 1024 * 1024
ADAM_LR, ADAM_B1, ADAM_B2, ADAM_EPS, ADAM_WD, ADAM_STEP = 0.001, 0.9, 0.999, 1e-08, 0.01, 10
MESH = pl.DeviceIdType.MESH
ANY = pl.BlockSpec(memory_space=pl.ANY)

ZB_POOL, ZB_U, ZB_V, ZB_A, ZB_B, ZB_GATES = 0, 1, 2, 3, 4, 5

BIG = (
    ("ffn1_w_gate", "col", 1024, FFN_SHARD, 1024, FFN_SHARD_PAD),
    ("ffn1_w_up", "col", 1024, FFN_SHARD, 1024, FFN_SHARD_PAD),
    ("ffn1_w_down", "row", FFN_SHARD, 1024, FFN_SHARD_PAD, 1024),
    ("w_in", "col", 1024, 1408, 1024, 1408),
    ("w_pool_out", "col", 512, 256, 512, 256),
    ("w_sgu_out", "col", 512, 256, 512, 256),
    ("w_conv_out", "col", 512, 256, 512, 256),
    ("w_out", "row", 256, 1024, 256, 1024),
    ("ffn2_w_gate", "col", 1024, FFN_SHARD, 1024, FFN_SHARD_PAD),
    ("ffn2_w_up", "col", 1024, FFN_SHARD, 1024, FFN_SHARD_PAD),
    ("ffn2_w_down", "row", FFN_SHARD, 1024, FFN_SHARD_PAD, 1024),
    ("ple_w_proj", "col", 256, 256, 256, 256),
    ("ple_w_gate", "row", 256, 1024, 256, 1024),
)
SMALL = ("ffn1_pre_g", "ffn1_post_g", "mix_pre_g", "pool_w", "pool_scale", "sgu_ln_g", "sgu_ln_b", "sgu_w_s",
         "sgu_b_s", "conv_dw_b", "conv_ln_g", "conv_ln_b", "mix_post_g", "ffn2_pre_g", "ffn2_post_g",
         "ple_pre_g", "ple_post_g")
WEIGHTS = ("ffn1_pre_g", "ffn1_w_gate", "ffn1_w_up", "ffn1_w_down", "ffn1_post_g", "mix_pre_g", "w_in", "pool_w",
           "pool_scale", "w_pool_out", "sgu_ln_g", "sgu_ln_b", "sgu_w_s", "sgu_b_s", "w_sgu_out", "conv_dw_k",
           "conv_dw_b", "conv_ln_g", "conv_ln_b", "w_conv_out", "w_out", "mix_post_g", "ffn2_pre_g", "ffn2_w_gate",
           "ffn2_w_up", "ffn2_w_down", "ffn2_post_g", "ple_w_proj", "ple_pre_g", "ple_w_gate", "ple_post_g")


def _params(n_grid):
    return pltpu.CompilerParams(dimension_semantics=("arbitrary",) * n_grid, vmem_limit_bytes=VMEM_LIMIT_BYTES)


def _tile(n, cap=ROW_TILE):
    for t in range(min(cap, n) - min(cap, n) % 16, 0, -16):
        if n % t == 0:
            return t
    return n


def _sigmoid(x):
    return 1.0 / (1.0 + jnp.exp(-x))


def _silu_and_grad(x):
    s = _sigmoid(x)
    return x * s, s * (1.0 + x * (1.0 - s))


def _gelu_and_grad(x):
    cdf = 0.5 * (1.0 + lax.erf(x * (1.0 / math.sqrt(2.0))))
    pdf = jnp.exp(-0.5 * x * x) * (1.0 / math.sqrt(2.0 * math.pi))
    return x * cdf, cdf + x * pdf


def _rms_fwd(x, g):
    return x * lax.rsqrt(jnp.mean(x * x, axis=-1, keepdims=True) + EPS) * g


def _rms_bwd(x, g, dy):
    r = lax.rsqrt(jnp.mean(x * x, axis=-1, keepdims=True) + EPS)
    xh = x * r
    dxh = dy * g
    dx = r * (dxh - xh * jnp.mean(dxh * xh, axis=-1, keepdims=True))
    return dx, jnp.sum(dy * xh, axis=0, keepdims=True)


def _ln_stats(x):
    xc = x - jnp.mean(x, axis=-1, keepdims=True)
    r = lax.rsqrt(jnp.mean(xc * xc, axis=-1, keepdims=True) + EPS)
    return xc * r, r


def _ln_bwd(xh, r, g, dy):
    dxh = dy * g
    dx = r * (dxh - jnp.mean(dxh, axis=-1, keepdims=True) - xh * jnp.mean(dxh * xh, axis=-1, keepdims=True))
    return dx, jnp.sum(dy * xh, axis=0, keepdims=True), jnp.sum(dy, axis=0, keepdims=True)


def _rowwise(name, fn, rows, consts, outs, accs=(), tm=ROW_TILE):
    T = rows[0][0].shape[-2]
    tm = _tile(T, tm)
    n_in, n_o = len(rows) + len(consts), len(outs)

    def body(*refs):
        res = fn(*[r[...] for r in refs[:n_in]])
        for ref, val in zip(refs[n_in:n_in + n_o], res[:n_o]):
            ref[...] = val.astype(ref.dtype)
        acc_refs = refs[n_in + n_o:]
        if acc_refs:
            @pl.when(pl.program_id(0) == 0)
            def _():
                for ref, val in zip(acc_refs, res[n_o:]):
                    ref[...] = val

            @pl.when(pl.program_id(0) != 0)
            def _():
                for ref, val in zip(acc_refs, res[n_o:]):
                    ref[...] += val

    in_specs = []
    for row in rows:
        w, cb = row[1], row[2]
        if len(row) == 4:
            in_specs.append(pl.BlockSpec((None, tm, w), lambda i, cb=cb, ld=row[3]: (ld, i, cb)))
        else:
            in_specs.append(pl.BlockSpec((tm, w), lambda i, cb=cb: (i, cb)))
    in_specs += [pl.BlockSpec(c.shape, lambda i: (0, 0)) for c in consts]
    out_specs = [pl.BlockSpec((tm, w), lambda i: (i, 0)) for w, _ in outs]
    out_specs += [pl.BlockSpec(s, lambda i: (0, 0)) for s in accs]
    out_shape = [jax.ShapeDtypeStruct((T, w), dt) for w, dt in outs]
    out_shape += [jax.ShapeDtypeStruct(s, F32) for s in accs]
    return pl.pallas_call(body, grid=(T // tm,), in_specs=in_specs, out_specs=out_specs, out_shape=out_shape,
                          name=name, compiler_params=_params(1))(*[r[0] for r in rows], *consts)


def _cast_pad(name, w, rp, cp, dtype):
    L, r, c = w.shape

    def body(w_ref, o_ref):
        if (rp, cp) != (r, c):
            o_ref[...] = jnp.zeros_like(o_ref)
            o_ref[pl.ds(0, r), pl.ds(0, c)] = w_ref[...].astype(dtype)
        else:
            o_ref[...] = w_ref[...].astype(dtype)

    return pl.pallas_call(body, grid=(L,), in_specs=[pl.BlockSpec((None, r, c), lambda l: (l, 0, 0))],
                          out_specs=pl.BlockSpec((None, rp, cp), lambda l: (l, 0, 0)),
                          out_shape=jax.ShapeDtypeStruct((L, rp, cp), dtype), name=name,
                          compiler_params=_params(1))(w)


_NN = (((1,), (0,)), ((), ()))
_NT = (((1,), (1,)), ((), ()))
_TN = (((0,), (0,)), ((), ()))


def _mm_blockout(name, x, w4, l, trans_w, tm=ROW_TILE):
    M, kx = x.shape
    nb, _, r, cc = w4.shape
    bo = r if trans_w else cc
    tm = _tile(M, tm)

    def body(x_ref, w_ref, o_ref):
        o_ref[...] = lax.dot_general(x_ref[...].astype(BF), w_ref[...].astype(BF), _NT if trans_w else _NN,
                                     preferred_element_type=F32).astype(o_ref.dtype)

    return pl.pallas_call(
        body, grid=(nb, M // tm),
        in_specs=[pl.BlockSpec((tm, kx), lambda b, i: (i, 0)),
                  pl.BlockSpec((None, None, r, cc), lambda b, i: (b, l, 0, 0))],
        out_specs=pl.BlockSpec((tm, bo), lambda b, i: (i, b)),
        out_shape=jax.ShapeDtypeStruct((M, nb * bo), BF), name=name, compiler_params=_params(2))(x, w4)


def _mm_blocksum(name, pairs, l, trans_w, tm=2 * ROW_TILE):
    x0, w0 = pairs[0]
    M = x0.shape[0]
    nb, _, r, cc = w0.shape
    bw, n_out = (cc, r) if trans_w else (r, cc)
    tm = _tile(M, tm)
    n_p = len(pairs)

    def body(*refs):
        o_ref, acc = refs[2 * n_p], refs[2 * n_p + 1]
        b = pl.program_id(1)

        @pl.when(b == 0)
        def _():
            acc[...] = jnp.zeros_like(acc)

        t = acc[...]
        for x_ref, w_ref in zip(refs[:n_p], refs[n_p:2 * n_p]):
            t = t + lax.dot_general(x_ref[...].astype(BF), w_ref[...].astype(BF), _NT if trans_w else _NN,
                                    preferred_element_type=F32)
        acc[...] = t

        @pl.when(b == nb - 1)
        def _():
            o_ref[...] = acc[...].astype(o_ref.dtype)

    in_specs = [pl.BlockSpec((tm, bw), lambda i, b: (i, b)) for _ in pairs]
    in_specs += [pl.BlockSpec((None, None, r, cc), lambda i, b: (b, l, 0, 0)) for _ in pairs]
    return pl.pallas_call(
        body, grid=(M // tm, nb), in_specs=in_specs, out_specs=pl.BlockSpec((tm, n_out), lambda i, b: (i, 0)),
        out_shape=jax.ShapeDtypeStruct((M, n_out), BF), scratch_shapes=[pltpu.VMEM((tm, n_out), F32)],
        name=name, compiler_params=_params(2))(*[p[0] for p in pairs], *[p[1] for p in pairs])


def _mm_tn(name, a, dy, buf, l, a_blocked, tk=ROW_TILE):
    T = a.shape[0]
    _, nb, R, C = buf.shape

    def body(a_ref, dy_ref, _, o_ref):
        o_ref[...] = lax.dot_general(a_ref[...].astype(BF), dy_ref[...].astype(BF), _TN,
                                     preferred_element_type=F32).astype(o_ref.dtype)

    if a_blocked:
        grid = (nb,)
        in_specs = [pl.BlockSpec((T, R), lambda b: (0, b)), pl.BlockSpec((T, C), lambda b: (0, 0)), ANY]
        out_specs = pl.BlockSpec((None, None, R, C), lambda b: (l, b, 0, 0))
    else:
        tk = min(tk, R)
        grid = (nb, R // tk)
        in_specs = [pl.BlockSpec((T, tk), lambda b, k: (0, k)), pl.BlockSpec((T, C), lambda b, k: (0, b)), ANY]
        out_specs = pl.BlockSpec((None, None, tk, C), lambda b, k: (l, b, k, 0))
    return pl.pallas_call(body, grid=grid, in_specs=in_specs, out_specs=out_specs,
                          out_shape=jax.ShapeDtypeStruct(buf.shape, buf.dtype), input_output_aliases={2: 0},
                          name=name, compiler_params=_params(len(grid)))(a, dy, buf)


def _pool_apply(x, win, row):
    s, k = x, 1
    while k < win:
        s = s + jnp.where(row >= k, pltpu.roll(s, k, 0), 0.0)
        k *= 2
    return s / jnp.minimum(row + 1, win).astype(F32) - x


def _pool_apply_t(dp, win, row):
    T = dp.shape[0]
    s, k = dp / jnp.minimum(row + 1, win).astype(F32), 1
    while k < win:
        s = s + jnp.where(row < T - k, pltpu.roll(s, T - k, 0), 0.0)
        k *= 2
    return s - dp


def _pool_fwd(z, w, scale):
    T = z.shape[0]

    def body(z_ref, w_ref, s_ref, o_ref):
        row = lax.broadcasted_iota(jnp.int32, (T, LANES), 0)
        for gi, win in enumerate(POOL_WINDOWS):
            cols = pl.ds(gi * LANES, LANES)
            pooled = _pool_apply(z_ref[:, cols].astype(F32), win, row)
            y = jnp.dot(pooled.astype(BF), w_ref[gi].astype(BF), preferred_element_type=F32)
            o_ref[:, cols] = (y * s_ref[:, cols]).astype(o_ref.dtype)

    return pl.pallas_call(
        body, grid=(1,),
        in_specs=[pl.BlockSpec((T, 512), lambda i: (0, ZB_POOL)), pl.BlockSpec(w.shape, lambda i: (0, 0, 0)),
                  pl.BlockSpec(scale.shape, lambda i: (0, 0))],
        out_specs=pl.BlockSpec((T, 512), lambda i: (0, 0)), out_shape=jax.ShapeDtypeStruct((T, 512), BF),
        name="pool_fwd", compiler_params=_params(1))(z, w, scale)


def _pool_bwd(dr, z, w, scale):
    T = z.shape[0]

    def body(dr_ref, z_ref, w_ref, s_ref, dz_ref, dw_ref, ds_ref):
        row = lax.broadcasted_iota(jnp.int32, (T, LANES), 0)
        for gi, win in enumerate(POOL_WINDOWS):
            cols = pl.ds(gi * LANES, LANES)
            pooled = _pool_apply(z_ref[:, cols].astype(F32), win, row).astype(BF)
            wg = w_ref[gi].astype(BF)
            y = jnp.dot(pooled, wg, preferred_element_type=F32)
            d = dr_ref[:, cols].astype(F32)
            ds_ref[:, cols] = jnp.sum(d * y, axis=0, keepdims=True)
            dy = (d * s_ref[:, cols]).astype(BF)
            dw_ref[gi] = lax.dot_general(pooled, dy, _TN, preferred_element_type=F32)
            dpooled = lax.dot_general(dy, wg, _NT, preferred_element_type=F32)
            dz_ref[:, cols] = _pool_apply_t(dpooled, win, row).astype(dz_ref.dtype)

    return pl.pallas_call(
        body, grid=(1,),
        in_specs=[pl.BlockSpec((T, 512), lambda i: (0, 0)), pl.BlockSpec((T, 512), lambda i: (0, ZB_POOL)),
                  pl.BlockSpec(w.shape, lambda i: (0, 0, 0)), pl.BlockSpec(scale.shape, lambda i: (0, 0))],
        out_specs=[pl.BlockSpec((T, 512), lambda i: (0, 0)), pl.BlockSpec(w.shape, lambda i: (0, 0, 0)),
                   pl.BlockSpec(scale.shape, lambda i: (0, 0))],
        out_shape=[jax.ShapeDtypeStruct((T, 512), BF), jax.ShapeDtypeStruct(w.shape, F32),
                   jax.ShapeDtypeStruct(scale.shape, F32)],
        name="pool_bwd", compiler_params=_params(1))(dr, z, w, scale)


def _tril(transposed=False):
    r = lax.broadcasted_iota(jnp.int32, (CHUNK, CHUNK), 0)
    c = lax.broadcasted_iota(jnp.int32, (CHUNK, CHUNK), 1)
    return c >= r if transposed else r >= c


def _sgu_fwd(z, ln_g, ln_b, w_s, bias):
    T = z.shape[0]
    tm = _tile(T)

    def body(zu_ref, zv_ref, g_ref, b_ref, w_ref, bias_ref, o_ref):
        gu, _ = _gelu_and_grad(zu_ref[...].astype(F32))
        gv, _ = _gelu_and_grad(zv_ref[...].astype(F32))
        xh, _ = _ln_stats(gv)
        v16 = (xh * g_ref[...] + b_ref[...]).astype(BF)
        tri = _tril()
        for h in range(SGU_HEADS):
            cols = slice(h * LANES, (h + 1) * LANES)
            wh = jnp.where(tri, w_ref[h], 0.0).astype(BF)
            for c in range(tm // CHUNK):
                rows = slice(c * CHUNK, (c + 1) * CHUNK)
                s = jnp.dot(wh, v16[rows, cols], preferred_element_type=F32) + bias_ref[:, cols]
                o_ref[rows, cols] = (gu[rows, cols] * s).astype(o_ref.dtype)

    small = [pl.BlockSpec(a.shape, lambda i, n=a.ndim: (0,) * n) for a in (ln_g, ln_b, w_s, bias)]
    return pl.pallas_call(
        body, grid=(T // tm,),
        in_specs=[pl.BlockSpec((tm, 512), lambda i: (i, ZB_U)), pl.BlockSpec((tm, 512), lambda i: (i, ZB_V))] + small,
        out_specs=pl.BlockSpec((tm, 512), lambda i: (i, 0)), out_shape=jax.ShapeDtypeStruct((T, 512), BF),
        name="sgu_fwd", compiler_params=_params(1))(z, z, ln_g, ln_b, w_s, bias)


def _sgu_bwd(dr, z, ln_g, ln_b, w_s, w_st, bias):
    T = z.shape[0]
    tm = _tile(T)
    n_steps = T // tm

    def body(dr_ref, zu_ref, zv_ref, g_ref, b_ref, w_ref, wt_ref, bias_ref,
             dzu_ref, dzv_ref, dg_ref, db_ref, dw_ref, dbias_ref, dgu_s, dv_s):
        i = pl.program_id(0)

        @pl.when(i == 0)
        def _():
            dg_ref[...] = jnp.zeros_like(dg_ref)
            db_ref[...] = jnp.zeros_like(db_ref)
            dw_ref[...] = jnp.zeros_like(dw_ref)
            dbias_ref[...] = jnp.zeros_like(dbias_ref)

        zu = zu_ref[...].astype(F32)
        zv = zv_ref[...].astype(F32)
        gu, gu_grad = _gelu_and_grad(zu)
        gv, gv_grad = _gelu_and_grad(zv)
        xh, r = _ln_stats(gv)
        v16 = (xh * g_ref[...] + b_ref[...]).astype(BF)
        dr = dr_ref[...].astype(F32)
        tri = _tril()
        for h in range(SGU_HEADS):
            cols = slice(h * LANES, (h + 1) * LANES)
            wh = jnp.where(tri, w_ref[h], 0.0).astype(BF)
            wht = jnp.where(_tril(transposed=True), wt_ref[h], 0.0).astype(BF)
            for c in range(tm // CHUNK):
                rows = slice(c * CHUNK, (c + 1) * CHUNK)
                v_blk = v16[rows, cols]
                s = jnp.dot(wh, v_blk, preferred_element_type=F32) + bias_ref[:, cols]
                ds = dr[rows, cols] * gu[rows, cols]
                dgu_s[rows, cols] = dr[rows, cols] * s
                ds16 = ds.astype(BF)
                dw_ref[h] += jnp.where(tri, lax.dot_general(ds16, v_blk, _NT, preferred_element_type=F32), 0.0)
                dv_s[rows, cols] = jnp.dot(wht, ds16, preferred_element_type=F32)
                dbias_ref[:, cols] += ds
        dzu_ref[...] = (dgu_s[...] * gu_grad).astype(dzu_ref.dtype)
        dgv, dg, db = _ln_bwd(xh, r, g_ref[...], dv_s[...])
        dzv_ref[...] = (dgv * gv_grad).astype(dzv_ref.dtype)
        dg_ref[...] += dg
        db_ref[...] += db

        @pl.when(i == n_steps - 1)
        def _():
            for h in range(SGU_HEADS):
                cols = slice(h * LANES, (h + 1) * LANES)
                tot = jnp.sum(dbias_ref[:, cols], axis=1, keepdims=True)
                dbias_ref[:, cols] = jnp.broadcast_to(tot, (CHUNK, LANES))

    small = (ln_g, ln_b, w_s, w_st, bias)
    small_specs = [pl.BlockSpec(a.shape, lambda i, n=a.ndim: (0,) * n) for a in small]
    return pl.pallas_call(
        body, grid=(n_steps,),
        in_specs=[pl.BlockSpec((tm, 512), lambda i: (i, 0)), pl.BlockSpec((tm, 512), lambda i: (i, ZB_U)),
                  pl.BlockSpec((tm, 512), lambda i: (i, ZB_V))] + small_specs,
        out_specs=[pl.BlockSpec((tm, 512), lambda i: (i, 0)), pl.BlockSpec((tm, 512), lambda i: (i, 0)),
                   pl.BlockSpec((1, 512), lambda i: (0, 0)), pl.BlockSpec((1, 512), lambda i: (0, 0)),
                   pl.BlockSpec(w_s.shape, lambda i: (0, 0, 0)), pl.BlockSpec(bias.shape, lambda i: (0, 0))],
        out_shape=[jax.ShapeDtypeStruct((T, 512), BF), jax.ShapeDtypeStruct((T, 512), BF),
                   jax.ShapeDtypeStruct((1, 512), F32), jax.ShapeDtypeStruct((1, 512), F32),
                   jax.ShapeDtypeStruct(w_s.shape, F32), jax.ShapeDtypeStruct(bias.shape, F32)],
        scratch_shapes=[pltpu.VMEM((tm, 512), F32), pltpu.VMEM((tm, 512), F32)],
        name="sgu_bwd", compiler_params=_params(1))(dr, z, z, ln_g, ln_b, w_s, w_st, bias)


def _conv_fwd(z, convk, l, bias):
    T = z.shape[0]

    def body(za_ref, zb_ref, k_ref, b_ref, o_ref):
        xg = za_ref[...].astype(F32) * _sigmoid(zb_ref[...].astype(F32))
        xp = jnp.concatenate([jnp.zeros((CONV_PAD, LANES), F32), xg], axis=0)
        kw = k_ref[...]
        acc = jnp.broadcast_to(b_ref[...], (T, LANES))
        for k in range(CONV_TAPS):
            sh = CONV_TAPS - 1 - k
            tap = xp if sh == 0 else pltpu.roll(xp, sh, 0)
            acc = acc + kw[k:k + 1, :] * tap[CONV_PAD:, :]
        o_ref[...] = acc.astype(o_ref.dtype)

    return pl.pallas_call(
        body, grid=(4,),
        in_specs=[pl.BlockSpec((T, LANES), lambda g: (0, 4 * ZB_A + g)),
                  pl.BlockSpec((T, LANES), lambda g: (0, 4 * ZB_B + g)),
                  pl.BlockSpec((None, None, CONV_PAD, LANES), lambda g: (g, l, 0, 0)),
                  pl.BlockSpec((1, LANES), lambda g: (0, g))],
        out_specs=pl.BlockSpec((T, LANES), lambda g: (0, g)), out_shape=jax.ShapeDtypeStruct((T, 512), BF),
        name="conv_fwd", compiler_params=_params(1))(z, z, convk, bias)


def _conv_bwd(dy, z, convk, l):
    T = z.shape[0]

    def body(dy_ref, za_ref, zb_ref, k_ref, dza_ref, dzb_ref, dk_ref, db_ref):
        a = za_ref[...].astype(F32)
        sg = _sigmoid(zb_ref[...].astype(F32))
        d = dy_ref[...].astype(F32)
        kw = k_ref[...]
        xp = jnp.concatenate([jnp.zeros((CONV_PAD, LANES), F32), a * sg], axis=0)
        dp = jnp.concatenate([d, jnp.zeros((CONV_PAD, LANES), F32)], axis=0)
        dxg = jnp.zeros((T, LANES), F32)
        dk_ref[...] = jnp.zeros_like(dk_ref)
        for k in range(CONV_TAPS):
            sh = CONV_TAPS - 1 - k
            tap = xp if sh == 0 else pltpu.roll(xp, sh, 0)
            dk_ref[k:k + 1, :] = jnp.sum(d * tap[CONV_PAD:, :], axis=0, keepdims=True)
            dtap = dp if sh == 0 else pltpu.roll(dp, T + CONV_PAD - sh, 0)
            dxg = dxg + kw[k:k + 1, :] * dtap[:T, :]
        db_ref[...] = jnp.sum(d, axis=0, keepdims=True)
        dza_ref[...] = (dxg * sg).astype(dza_ref.dtype)
        dzb_ref[...] = (dxg * a * sg * (1.0 - sg)).astype(dzb_ref.dtype)

    col = pl.BlockSpec((T, LANES), lambda g: (0, g))
    return pl.pallas_call(
        body, grid=(4,),
        in_specs=[col, pl.BlockSpec((T, LANES), lambda g: (0, 4 * ZB_A + g)),
                  pl.BlockSpec((T, LANES), lambda g: (0, 4 * ZB_B + g)),
                  pl.BlockSpec((None, None, CONV_PAD, LANES), lambda g: (g, l, 0, 0))],
        out_specs=[col, col, pl.BlockSpec((CONV_PAD, LANES), lambda g: (0, g)),
                   pl.BlockSpec((1, LANES), lambda g: (0, g))],
        out_shape=[jax.ShapeDtypeStruct((T, 512), BF), jax.ShapeDtypeStruct((T, 512), BF),
                   jax.ShapeDtypeStruct((CONV_PAD, 512), F32), jax.ShapeDtypeStruct((1, 512), F32)],
        name="conv_bwd", compiler_params=_params(1))(dy, z, z, convk)


D = D_MODEL


def _rms_call(name, h, g):
    return _rowwise(name, lambda x, gg: (_rms_fwd(x.astype(F32), gg),), [(h, D, 0)], [g], [(D, BF)])[0]


def _res_norm_call(name, h, f, g, coef):
    return _rowwise(name, lambda x, ff, gg: (x + coef * _rms_fwd(ff.astype(F32), gg),),
                    [(h, D, 0), (f, D, 0)], [g], [(D, F32)])[0]


def _res_norm_bwd_call(name, dh, f, g, coef):
    return _rowwise(name, lambda d, ff, gg: _rms_bwd(ff.astype(F32), gg, coef * d),
                    [(dh, D, 0), (f, D, 0)], [g], [(D, BF)], [(1, D)])


def _pre_norm_bwd_call(name, dh, h, dn, g):
    def fn(d, x, dnn, gg):
        dx, dg = _rms_bwd(x, gg, dnn.astype(F32))
        return d + dx, dg

    return _rowwise(name, fn, [(dh, D, 0), (h, D, 0), (dn, D, 0)], [g], [(D, F32)], [(1, D)])


def _ffn_fwd(l, h, S, W, pre):
    n = _rms_call("ffn_norm", h, S[pre + "_pre_g"])
    gp = _mm_blockout("ffn_gate", n, W[pre + "_w_gate"], l, False)
    u = _mm_blockout("ffn_up", n, W[pre + "_w_up"], l, False)
    fw = gp.shape[1]
    a = _rowwise("ffn_act", lambda g, uu: (_silu_and_grad(g.astype(F32))[0] * uu.astype(F32),),
                 [(gp, fw, 0), (u, fw, 0)], [], [(fw, BF)])[0]
    f = _mm_blocksum("ffn_down", [(a, W[pre + "_w_down"])], l, False)
    out = _res_norm_call("ffn_res", h, f, S[pre + "_post_g"], 0.5)
    return out, dict(h=h, n=n, gp=gp, u=u, a=a, f=f)


def _ffn_bwd(l, dh, sv, S, W, G, SG, pre):
    df, SG[pre + "_post_g"] = _res_norm_bwd_call("ffn_res_bwd", dh, sv["f"], S[pre + "_post_g"], 0.5)
    da = _mm_blockout("ffn_dact", df, W[pre + "_w_down"], l, True)
    fw = da.shape[1]

    def act_bwd(d, g, uu):
        act, grad = _silu_and_grad(g.astype(F32))
        d = d.astype(F32)
        return d * uu.astype(F32) * grad, d * act

    dgp, du = _rowwise("ffn_act_bwd", act_bwd, [(da, fw, 0), (sv["gp"], fw, 0), (sv["u"], fw, 0)], [],
                       [(fw, BF), (fw, BF)])
    G[pre + "_w_down"] = _mm_tn("ffn_dw_down", sv["a"], df, G[pre + "_w_down"], l, True)
    G[pre + "_w_gate"] = _mm_tn("ffn_dw_gate", sv["n"], dgp, G[pre + "_w_gate"], l, False)
    G[pre + "_w_up"] = _mm_tn("ffn_dw_up", sv["n"], du, G[pre + "_w_up"], l, False)
    dn = _mm_blocksum("ffn_dn", [(dgp, W[pre + "_w_gate"]), (du, W[pre + "_w_up"])], l, True)
    dh_in, SG[pre + "_pre_g"] = _pre_norm_bwd_call("pre_norm_bwd", dh, sv["h"], dn, S[pre + "_pre_g"])
    return dh_in


def _gate_rows(z):
    return [(z, 512, ZB_GATES + j) for j in range(6)]


def _gates(zg):
    return [_sigmoid(jnp.concatenate([zg[2 * k].astype(F32), zg[2 * k + 1].astype(F32)], axis=1)) for k in range(3)]


def _mix_fwd(l, h, S, W):
    n = _rms_call("mix_norm", h, S["mix_pre_g"])
    z = _mm_blockout("mix_in", n, W["w_in"], l, False)
    r_pool = _pool_fwd(z, S["pool_w"], S["pool_scale"])
    r_sgu = _sgu_fwd(z, S["sgu_ln_g"], S["sgu_ln_b"], S["sgu_w_s"], S["sgu_bias"])
    yc = _conv_fwd(z, W["conv_dw_k"], l, S["conv_dw_b"])

    def ln_silu(y, g, b):
        xh, _ = _ln_stats(y.astype(F32))
        return (_silu_and_grad(xh * g + b)[0],)

    r_conv = _rowwise("conv_ln", ln_silu, [(yc, 512, 0)], [S["conv_ln_g"], S["conv_ln_b"]], [(512, BF)])[0]
    y_pool = _mm_blockout("branch_out", r_pool, W["w_pool_out"], l, False)
    y_sgu = _mm_blockout("branch_out", r_sgu, W["w_sgu_out"], l, False)
    y_conv = _mm_blockout("branch_out", r_conv, W["w_conv_out"], l, False)

    def merge(*t):
        g = _gates(t[:6])
        return (g[0] * t[6].astype(F32) + g[1] * t[7].astype(F32) + g[2] * t[8].astype(F32),)

    merged = _rowwise("mix_merge", merge, _gate_rows(z) + [(y_pool, D, 0), (y_sgu, D, 0), (y_conv, D, 0)], [],
                      [(D, BF)])[0]
    o = _mm_blocksum("mix_out", [(merged, W["w_out"])], l, False)
    out = _res_norm_call("mix_res", h, o, S["mix_post_g"], 1.0)
    return out, dict(h=h, n=n, z=z, r_pool=r_pool, r_sgu=r_sgu, yc=yc, r_conv=r_conv, y_pool=y_pool, y_sgu=y_sgu,
                     y_conv=y_conv, merged=merged, o=o)


def _mix_bwd(l, dh, sv, S, W, G, SG):
    z = sv["z"]
    do, SG["mix_post_g"] = _res_norm_bwd_call("mix_res_bwd", dh, sv["o"], S["mix_post_g"], 1.0)
    dmerged = _mm_blockout("mix_dmerged", do, W["w_out"], l, True)
    G["w_out"] = _mm_tn("mix_dw_out", sv["merged"], do, G["w_out"], l, True)

    def merge_bwd(dm, *t):
        g = _gates(t[:6])
        dm = dm.astype(F32)
        dzg = [dm * t[6 + k].astype(F32) * g[k] * (1.0 - g[k]) for k in range(3)]
        return dm * g[0], dm * g[1], dm * g[2], jnp.concatenate(dzg, axis=1)

    rows = [(dmerged, D, 0)] + _gate_rows(z) + [(sv["y_pool"], D, 0), (sv["y_sgu"], D, 0), (sv["y_conv"], D, 0)]
    dy_pool, dy_sgu, dy_conv, dzg = _rowwise("mix_merge_bwd", merge_bwd, rows, [], [(D, BF)] * 3 + [(3 * D, BF)],
                                             tm=ROW_TILE // 2)
    dr = {}
    for br, dy in (("pool", dy_pool), ("sgu", dy_sgu), ("conv", dy_conv)):
        wn = "w_%s_out" % br
        G[wn] = _mm_tn("branch_dw", sv["r_" + br], dy, G[wn], l, False)
        dr[br] = _mm_blocksum("branch_dr", [(dy, W[wn])], l, True)
    dz_pool, SG["pool_w"], SG["pool_scale"] = _pool_bwd(dr["pool"], z, S["pool_w"], S["pool_scale"])
    dzu, dzv, SG["sgu_ln_g"], SG["sgu_ln_b"], SG["sgu_w_s"], dbias = _sgu_bwd(
        dr["sgu"], z, S["sgu_ln_g"], S["sgu_ln_b"], S["sgu_w_s"], S["sgu_w_st"], S["sgu_bias"])
    SG["sgu_b_s"] = dbias[:, ::LANES].T

    def ln_silu_bwd(d, y, g, b):
        xh, r = _ln_stats(y.astype(F32))
        _, grad = _silu_and_grad(xh * g + b)
        return _ln_bwd(xh, r, g, d.astype(F32) * grad)

    dyc, SG["conv_ln_g"], SG["conv_ln_b"] = _rowwise(
        "conv_ln_bwd", ln_silu_bwd, [(dr["conv"], 512, 0), (sv["yc"], 512, 0)], [S["conv_ln_g"], S["conv_ln_b"]],
        [(512, BF)], [(1, 512), (1, 512)])
    dza, dzb, SG["conv_dw_k"], SG["conv_dw_b"] = _conv_bwd(dyc, z, W["conv_dw_k"], l)
    dz = jnp.concatenate([dz_pool, dzu, dzv, dza, dzb, dzg], axis=1)
    G["w_in"] = _mm_tn("mix_dw_in", sv["n"], dz, G["w_in"], l, False)
    dn = _mm_blocksum("mix_dn", [(dz, W["w_in"])], l, True)
    dh_in, SG["mix_pre_g"] = _pre_norm_bwd_call("pre_norm_bwd", dh, sv["h"], dn, S["mix_pre_g"])
    return dh_in


def _ple_fwd(l, h, p_l, S, W):
    n = _rms_call("ple_norm", h, S["ple_pre_g"])
    e = _mm_blockout("ple_proj", p_l, W["ple_w_proj"], l, False)
    gp = _mm_blocksum("ple_gate", [(n, W["ple_w_gate"])], l, False)

    def res(x, ee, g, gg):
        return (x + _rms_fwd(_sigmoid(g.astype(F32)) * ee.astype(F32), gg),)

    out = _rowwise("ple_res", res, [(h, D, 0), (e, D, 0), (gp, D, 0)], [S["ple_post_g"]], [(D, F32)])[0]
    return out, dict(h=h, n=n, e=e, gp=gp, p=p_l)


def _ple_bwd(l, dh, sv, S, W, G, SG):
    def res_bwd(d, ee, g, gg):
        sg = _sigmoid(g.astype(F32))
        ee = ee.astype(F32)
        dq, dg = _rms_bwd(sg * ee, gg, d)
        return dq * sg, dq * ee * sg * (1.0 - sg), dg

    de, dgp, SG["ple_post_g"] = _rowwise("ple_res_bwd", res_bwd, [(dh, D, 0), (sv["e"], D, 0), (sv["gp"], D, 0)],
                                         [S["ple_post_g"]], [(D, BF), (D, BF)], [(1, D)])
    G["ple_w_proj"] = _mm_tn("ple_dw_proj", sv["p"], de, G["ple_w_proj"], l, False)
    G["ple_w_gate"] = _mm_tn("ple_dw_gate", sv["n"], dgp, G["ple_w_gate"], l, True)
    dn = _mm_blockout("ple_dn", dgp, W["ple_w_gate"], l, True)
    dh_in, SG["ple_pre_g"] = _pre_norm_bwd_call("pre_norm_bwd", dh, sv["h"], dn, S["ple_pre_g"])
    return dh_in


def _layer_small(a, l):
    S = {}
    for name in SMALL:
        v = a[name][l]
        S[name] = v.reshape(1, -1) if v.ndim == 1 else v
    S["sgu_w_st"] = jnp.swapaxes(S["sgu_w_s"], 1, 2)
    S["sgu_bias"] = jnp.repeat(S["sgu_b_s"].T, LANES, axis=1)
    return S


def _layer_fwd(l, h, p_l, S, W):
    h, sv1 = _ffn_fwd(l, h, S, W, "ffn1")
    h, sv2 = _mix_fwd(l, h, S, W)
    h, sv3 = _ffn_fwd(l, h, S, W, "ffn2")
    h, sv4 = _ple_fwd(l, h, p_l, S, W)
    return h, (sv1, sv2, sv3, sv4)


def _layer_bwd(l, dh, sv, S, W, G):
    SG = {}
    dh = _ple_bwd(l, dh, sv[3], S, W, G, SG)
    dh = _ffn_bwd(l, dh, sv[2], S, W, G, SG, "ffn2")
    dh = _mix_bwd(l, dh, sv[1], S, W, G, SG)
    dh = _ffn_bwd(l, dh, sv[0], S, W, G, SG, "ffn1")
    return dh, SG


def _local_fwd_bwd(a, W, x, target):
    L = a["ffn1_pre_g"].shape[0]
    small = [_layer_small(a, l) for l in range(L)]
    h, saved = x, []
    for l in range(L):
        h, sv = _layer_fwd(l, h, a["p"][l, 0], small[l], W)
        saved.append(sv)

    def loss_fn(y, t):
        e = y - t
        return e * (1.0 / D), jnp.sum(e * e, axis=0, keepdims=True)

    dh, lsum = _rowwise("loss", loss_fn, [(h, D, 0), (target, D, 0)], [], [(D, F32)], [(1, D)])
    loss = 0.5 * jnp.sum(lsum) / D
    G = {name: lax.empty((L, N_CHIPS, rp, cp), BF) for name, _, _, _, rp, cp in BIG}
    small_grads = [None] * L
    for l in reversed(range(L)):
        dh, small_grads[l] = _layer_bwd(l, dh, saved[l], small[l], W, G)
    return loss, dh, G, small_grads


def _place():
    x, y, c = lax.axis_index("x"), lax.axis_index("y"), lax.axis_index("c")
    chips = [(1 - x, y), (x, 1 - y), (1 - x, 1 - y)]
    return x, y, c, chips


def _remote(src, dst, send_sem, recv_sem, to):
    return pltpu.make_async_remote_copy(src_ref=src, dst_ref=dst, send_sem=send_sem, recv_sem=recv_sem,
                                        device_id=to, device_id_type=MESH)


def _exchange_call(name, body, ins, out_shapes, n_remote, n_local):
    scratch = [pltpu.SemaphoreType.DMA((n_remote,)), pltpu.SemaphoreType.DMA((n_remote,)),
               pltpu.SemaphoreType.DMA((n_local,))]
    return pl.pallas_call(body, in_specs=[ANY] * len(ins), out_specs=[ANY] * len(out_shapes), out_shape=out_shapes,
                          scratch_shapes=scratch, name=name)(*ins)


def _gather_weights(shards):
    n = len(shards)
    lh = shards[0].shape[0] // 2

    def body(*refs):
        ins, outs = refs[:n], refs[n:2 * n]
        send, recv, local = refs[2 * n:]
        x, y, c, chips = _place()
        me = 2 * x + y
        mine, other = pl.ds(c * lh, lh), pl.ds((1 - c) * lh, lh)
        own = [pltpu.make_async_copy(ins[a], outs[a].at[me], local.at[a]) for a in range(n)]
        for cp in own:
            cp.start()
        sent = []
        for a in range(n):
            for j, (cx, cy) in enumerate(chips):
                k = a * 6 + j
                sent.append(_remote(ins[a].at[mine], outs[a].at[me, mine], send.at[k], recv.at[k], (cx, cy, c)))
                sent[-1].start()
        for a in range(n):
            for j, (cx, cy) in enumerate(chips):
                k = a * 6 + j
                piece = outs[a].at[2 * cx + cy, mine]
                _remote(piece, piece, send.at[k], recv.at[k], (cx, cy, c)).wait_recv()
                sent.append(_remote(piece, piece, send.at[k + 3], recv.at[k + 3], (x, y, 1 - c)))
                sent[-1].start()
        for a in range(n):
            for j, (cx, cy) in enumerate(chips):
                k = a * 6 + j + 3
                piece = outs[a].at[2 * cx + cy, other]
                _remote(piece, piece, send.at[k], recv.at[k], (x, y, 1 - c)).wait_recv()
        for cp in sent:
            cp.wait_send()
        for cp in own:
            cp.wait()

    shapes = [jax.ShapeDtypeStruct((N_CHIPS,) + s.shape, s.dtype) for s in shards]
    return _exchange_call("gather_weights", body, shards, shapes, 6 * n, n)


def _rs_pair(grads):
    n = len(grads)

    def body(*refs):
        ins, keep, got = refs[:n], refs[n:2 * n], refs[2 * n:3 * n]
        send, recv, local = refs[3 * n:]
        x, y, c, _ = _place()
        cps = []
        for a in range(n):
            rh = ins[a].shape[2] // 2
            cps.append(pltpu.make_async_copy(ins[a].at[:, :, pl.ds(c * rh, rh)], keep[a], local.at[a]))
            cps.append(_remote(ins[a].at[:, :, pl.ds((1 - c) * rh, rh)], got[a], send.at[a], recv.at[a],
                               (x, y, 1 - c)))
        for cp in cps:
            cp.start()
        for cp in cps:
            cp.wait()

    half = [jax.ShapeDtypeStruct(g.shape[:2] + (g.shape[2] // 2, g.shape[3]), g.dtype) for g in grads]
    res = _exchange_call("rs_pair", body, grads, half + half, n, n)
    return res[:n], res[n:]


def _rs_cross(parts):
    n = len(parts)

    def body(*refs):
        ins, outs = refs[:n], refs[n:2 * n]
        send, recv, local = refs[2 * n:]
        x, y, c, chips = _place()
        me = 2 * x + y
        cps = []
        for a in range(n):
            cps.append(pltpu.make_async_copy(ins[a].at[:, me], outs[a].at[me], local.at[a]))
            cps[-1].start()
            for j, (cx, cy) in enumerate(chips):
                k = a * 3 + j
                cps.append(_remote(ins[a].at[:, 2 * cx + cy], outs[a].at[me], send.at[k], recv.at[k], (cx, cy, c)))
                cps[-1].start()
        for a in range(n):
            for j, (cx, cy) in enumerate(chips):
                k = a * 3 + j
                slot = outs[a].at[2 * cx + cy]
                _remote(slot, slot, send.at[k], recv.at[k], (cx, cy, c)).wait_recv()
        for a in range(n):
            cps[a * 4].wait()
            for j in range(3):
                cps[a * 4 + 1 + j].wait_send()

    shapes = [jax.ShapeDtypeStruct((N_CHIPS, q.shape[0]) + q.shape[2:], q.dtype) for q in parts]
    return _exchange_call("rs_cross", body, parts, shapes, 3 * n, n)


def _rs_share(halves):
    n = len(halves)

    def body(*refs):
        ins, outs = refs[:n], refs[n:2 * n]
        send, recv, local = refs[2 * n:]
        x, y, c, _ = _place()
        cps = []
        for a in range(n):
            rh = ins[a].shape[1]
            mine = outs[a].at[:, pl.ds(c * rh, rh)]
            cps.append(pltpu.make_async_copy(ins[a], mine, local.at[a]))
            cps[-1].start()
            cps.append(_remote(ins[a], mine, send.at[a], recv.at[a], (x, y, 1 - c)))
            cps[-1].start()
        for a in range(n):
            rh = ins[a].shape[1]
            theirs = outs[a].at[:, pl.ds((1 - c) * rh, rh)]
            _remote(theirs, theirs, send.at[a], recv.at[a], (x, y, 1 - c)).wait_recv()
            cps[2 * a].wait()
            cps[2 * a + 1].wait_send()

    shapes = [jax.ShapeDtypeStruct((h.shape[0], 2 * h.shape[1], h.shape[2]), h.dtype) for h in halves]
    return _exchange_call("rs_share", body, halves, shapes, n, n)


def _gather_small(v):
    def body(v_ref, o_ref, send, recv, local):
        x, y, c, _ = _place()
        me = 4 * x + 2 * y + c
        own = pltpu.make_async_copy(v_ref, o_ref.at[me], local.at[0])
        own.start()
        peers = []
        for m in range(1, 8):
            px = 1 - x if m & 4 else x
            py = 1 - y if m & 2 else y
            pc = 1 - c if m & 1 else c
            peers.append((px, py, pc))
            _remote(v_ref, o_ref.at[me], send.at[m - 1], recv.at[m - 1], (px, py, pc)).start()
        for m, (px, py, pc) in enumerate(peers):
            slot = o_ref.at[4 * px + 2 * py + pc]
            cp = _remote(slot, slot, send.at[m], recv.at[m], (px, py, pc))
            cp.wait_recv()
            cp.wait_send()
        own.wait()

    return _exchange_call("gather_small", body, [v], [jax.ShapeDtypeStruct((8,) + v.shape, v.dtype)], 7, 1)[0]


def _reduce_scatter_grads(G):
    names = [b[0] for b in BIG]
    keep, got = _rs_pair([G[nm] for nm in names])
    parts = []
    for k, g in zip(keep, got):
        c = k.shape[-1]
        t = _rowwise("rs_add_pair", lambda p, q: (p.astype(F32) + q.astype(F32),),
                     [(k.reshape(-1, c), c, 0), (g.reshape(-1, c), c, 0)], [], [(c, BF)])[0]
        parts.append(t.reshape(k.shape))
    slots = _rs_cross(parts)
    halves = []
    for s in slots:
        _, L, rh, c = s.shape
        s3 = s.reshape(N_CHIPS, L * rh, c)
        t = _rowwise("rs_add_chips", lambda p0, p1, p2, p3: (((p0.astype(F32) + p1.astype(F32)) + p2.astype(F32))
                                                            + p3.astype(F32),),
                     [(s3, c, 0, k) for k in range(N_CHIPS)], [], [(c, F32)])[0]
        halves.append(t.reshape(L, rh, c))
    return dict(zip(names, _rs_share(halves)))


def _adamw_math(w, g, m, v):
    m = ADAM_B1 * m + (1.0 - ADAM_B1) * g
    v = ADAM_B2 * v + (1.0 - ADAM_B2) * (g * g)
    m_hat = m / (1.0 - ADAM_B1 ** ADAM_STEP)
    v_hat = v / (1.0 - ADAM_B2 ** ADAM_STEP)
    return -ADAM_LR * (m_hat / (jnp.sqrt(v_hat) + ADAM_EPS) + ADAM_WD * w), m, v


def _adamw(w, g, m, v):
    L, R, C = w.shape
    cp = g.shape[2]
    tr = _tile(R, 256)

    def body(w_ref, g_ref, m_ref, v_ref, go_ref, d_ref, mo_ref, vo_ref):
        gv = g_ref[:, pl.ds(0, C)] if cp != C else g_ref[...]
        d, mn, vn = _adamw_math(w_ref[...], gv, m_ref[...], v_ref[...])
        go_ref[...] = gv
        d_ref[...] = d
        mo_ref[...] = mn
        vo_ref[...] = vn

    spec = pl.BlockSpec((None, tr, C), lambda l, i: (l, i, 0))
    out = jax.ShapeDtypeStruct(w.shape, F32)
    return pl.pallas_call(body, grid=(L, R // tr),
                          in_specs=[spec, pl.BlockSpec((None, tr, cp), lambda l, i: (l, i, 0)), spec, spec],
                          out_specs=[spec] * 4, out_shape=[out] * 4, name="adamw", compiler_params=_params(2))(w, g, m, v)


def _pack(parts):
    flat = jnp.concatenate([q.reshape(-1, LANES) for q in parts], axis=0)
    return jnp.pad(flat, ((0, -flat.shape[0] % ROW_TILE), (0, 0)))


def _unpack(flat, like):
    out, r = [], 0
    for q in like:
        n = q.size // LANES
        out.append(flat[r:r + n].reshape(q.shape))
        r += n
    return out


def _train_step(a):
    L = a["ffn1_pre_g"].shape[0]
    x, y, _, _ = _place()
    chip = 2 * x + y

    shards = [_cast_pad("cast_weight", a[name], rp, cp, BF) for name, _, _, _, rp, cp in BIG]
    shards.append(_cast_pad("pad_conv_taps", a["conv_dw_k"].reshape(L, CONV_TAPS, LANES), CONV_PAD, LANES, F32))
    full = _gather_weights(shards)
    W = dict(zip([b[0] for b in BIG] + ["conv_dw_k"], full))

    loss, grad_x, G, small_grads = _local_fwd_bwd(a, W, a["x"][0], a["loss_target"][0])
    loss = lax.psum(loss, ("x", "y", "c"))

    big_grads = _reduce_scatter_grads(G)

    small_names = SMALL + ("conv_dw_k",)
    stacked = [jnp.stack([small_grads[l][name] for l in range(L)]) for name in small_names]
    gathered = _gather_small(_pack(stacked))
    rows = gathered.shape[1]
    total = _rowwise("sum_small", lambda *t: (((((((t[0] + t[1]) + t[2]) + t[3]) + t[4]) + t[5]) + t[6]) + t[7],),
                     [(gathered, LANES, 0, k) for k in range(8)], [], [(LANES, F32)])[0]
    reduced = dict(zip(small_names, _unpack(total, stacked)))

    grads, deltas, new_m, new_v = {}, {}, {}, {}
    for name, _, _, _, _, _ in BIG:
        grads[name], deltas[name], new_m[name], new_v[name] = _adamw(
            a[name], big_grads[name], a["m_" + name], a["v_" + name])
    taps = lax.dynamic_slice_in_dim(reduced["conv_dw_k"], chip * LANES, LANES, axis=2)[:, :CONV_TAPS]
    grads["conv_dw_k"] = taps.reshape(a["conv_dw_k"].shape)
    for name in SMALL:
        grads[name] = reduced[name].reshape(a[name].shape)
    tiny = SMALL + ("conv_dw_k",)
    shapes = [a[name] for name in tiny]
    res = _adamw(*[_pack([a[pre + name] if pre != "g" else grads[name] for name in tiny])[None]
                   for pre in ("", "g", "m_", "v_")])
    for dst, flat in zip((deltas, new_m, new_v), res[1:]):
        for name, val in zip(tiny, _unpack(flat[0], shapes)):
            dst[name] = val

    return (loss, grad_x[None], *[grads[n] for n in WEIGHTS], *[deltas[n] for n in WEIGHTS],
            *[new_m[n] for n in WEIGHTS], *[new_v[n] for n in WEIGHTS])


def kernel(x, p, ffn1_pre_g, ffn1_w_gate, ffn1_w_up, ffn1_w_down, ffn1_post_g, mix_pre_g, w_in, pool_w, pool_scale, w_pool_out, sgu_ln_g, sgu_ln_b, sgu_w_s, sgu_b_s, w_sgu_out, conv_dw_k, conv_dw_b, conv_ln_g, conv_ln_b, w_conv_out, w_out, mix_post_g, ffn2_pre_g, ffn2_w_gate, ffn2_w_up, ffn2_w_down, ffn2_post_g, ple_w_proj, ple_pre_g, ple_w_gate, ple_post_g, loss_target, m_ffn1_pre_g, m_ffn1_w_gate, m_ffn1_w_up, m_ffn1_w_down, m_ffn1_post_g, m_mix_pre_g, m_w_in, m_pool_w, m_pool_scale, m_w_pool_out, m_sgu_ln_g, m_sgu_ln_b, m_sgu_w_s, m_sgu_b_s, m_w_sgu_out, m_conv_dw_k, m_conv_dw_b, m_conv_ln_g, m_conv_ln_b, m_w_conv_out, m_w_out, m_mix_post_g, m_ffn2_pre_g, m_ffn2_w_gate, m_ffn2_w_up, m_ffn2_w_down, m_ffn2_post_g, m_ple_w_proj, m_ple_pre_g, m_ple_w_gate, m_ple_post_g, v_ffn1_pre_g, v_ffn1_w_gate, v_ffn1_w_up, v_ffn1_w_down, v_ffn1_post_g, v_mix_pre_g, v_w_in, v_pool_w, v_pool_scale, v_w_pool_out, v_sgu_ln_g, v_sgu_ln_b, v_sgu_w_s, v_sgu_b_s, v_w_sgu_out, v_conv_dw_k, v_conv_dw_b, v_conv_ln_g, v_conv_ln_b, v_w_conv_out, v_w_out, v_mix_post_g, v_ffn2_pre_g, v_ffn2_w_gate, v_ffn2_w_up, v_ffn2_w_down, v_ffn2_post_g, v_ple_w_proj, v_ple_pre_g, v_ple_w_gate, v_ple_post_g):
    return _train_step(dict(locals()))
```

```python
import math

import jax
import jax.numpy as jnp
from jax import lax
from jax.experimental import pallas as pl
from jax.experimental.pallas import tpu as pltpu

BF = jnp.bfloat16
F32 = jnp.float32
EPS = 1e-6
D_MODEL = 1024
LANES = 128
N_CHIPS = 4
FFN_SHARD = 704
FFN_SHARD_PAD = 768
POOL_WINDOWS = (2, 4, 8, 16)
SGU_HEADS = 4
CHUNK = 128
CONV_TAPS = 31
CONV_PAD = 32
ROW_TILE = 512
VMEM_LIMIT_BYTES = 56 * 1024 * 1024
ADAM_LR, ADAM_B1, ADAM_B2, ADAM_EPS, ADAM_WD, ADAM_STEP = 0.001, 0.9, 0.999, 1e-08, 0.01, 10
MESH = pl.DeviceIdType.MESH
ANY = pl.BlockSpec(memory_space=pl.ANY)

ZB_POOL, ZB_U, ZB_V, ZB_A, ZB_B, ZB_GATES = 0, 1, 2, 3, 4, 5

TRANSPOSED = ("ffn1_w_gate", "ffn1_w_up", "ffn2_w_gate", "ffn2_w_up")
BIG = (
    ("ffn1_w_gate", "row", FFN_SHARD, 1024, FFN_SHARD_PAD, 1024),
    ("ffn1_w_up", "row", FFN_SHARD, 1024, FFN_SHARD_PAD, 1024),
    ("ffn1_w_down", "row", FFN_SHARD, 1024, FFN_SHARD_PAD, 1024),
    ("w_in", "col", 1024, 1408, 1024, 1408),
    ("w_pool_out", "col", 512, 256, 512, 256),
    ("w_sgu_out", "col", 512, 256, 512, 256),
    ("w_conv_out", "col", 512, 256, 512, 256),
    ("w_out", "row", 256, 1024, 256, 1024),
    ("ffn2_w_gate", "row", FFN_SHARD, 1024, FFN_SHARD_PAD, 1024),
    ("ffn2_w_up", "row", FFN_SHARD, 1024, FFN_SHARD_PAD, 1024),
    ("ffn2_w_down", "row", FFN_SHARD, 1024, FFN_SHARD_PAD, 1024),
    ("ple_w_proj", "col", 256, 256, 256, 256),
    ("ple_w_gate", "row", 256, 1024, 256, 1024),
)
SMALL = ("ffn1_pre_g", "ffn1_post_g", "mix_pre_g", "pool_w", "pool_scale", "sgu_ln_g", "sgu_ln_b", "sgu_w_s",
         "sgu_b_s", "conv_dw_b", "conv_ln_g", "conv_ln_b", "mix_post_g", "ffn2_pre_g", "ffn2_post_g",
         "ple_pre_g", "ple_post_g")
WEIGHTS = ("ffn1_pre_g", "ffn1_w_gate", "ffn1_w_up", "ffn1_w_down", "ffn1_post_g", "mix_pre_g", "w_in", "pool_w",
           "pool_scale", "w_pool_out", "sgu_ln_g", "sgu_ln_b", "sgu_w_s", "sgu_b_s", "w_sgu_out", "conv_dw_k",
           "conv_dw_b", "conv_ln_g", "conv_ln_b", "w_conv_out", "w_out", "mix_post_g", "ffn2_pre_g", "ffn2_w_gate",
           "ffn2_w_up", "ffn2_w_down", "ffn2_post_g", "ple_w_proj", "ple_pre_g", "ple_w_gate", "ple_post_g")


def _params(n_grid):
    return pltpu.CompilerParams(dimension_semantics=("arbitrary",) * n_grid, vmem_limit_bytes=VMEM_LIMIT_BYTES)


def _tile(n, cap=ROW_TILE):
    for t in range(min(cap, n) - min(cap, n) % 16, 0, -16):
        if n % t == 0:
            return t
    return n


def _sigmoid(x):
    return 1.0 / (1.0 + jnp.exp(-x))


def _silu_and_grad(x):
    s = _sigmoid(x)
    return x * s, s * (1.0 + x * (1.0 - s))


def _gelu_and_grad(x):
    cdf = 0.5 * (1.0 + lax.erf(x * (1.0 / math.sqrt(2.0))))
    pdf = jnp.exp(-0.5 * x * x) * (1.0 / math.sqrt(2.0 * math.pi))
    return x * cdf, cdf + x * pdf


def _rms_fwd(x, g):
    return x * lax.rsqrt(jnp.mean(x * x, axis=-1, keepdims=True) + EPS) * g


def _rms_bwd(x, g, dy):
    r = lax.rsqrt(jnp.mean(x * x, axis=-1, keepdims=True) + EPS)
    xh = x * r
    dxh = dy * g
    dx = r * (dxh - xh * jnp.mean(dxh * xh, axis=-1, keepdims=True))
    return dx, jnp.sum(dy * xh, axis=0, keepdims=True)


def _ln_stats(x):
    xc = x - jnp.mean(x, axis=-1, keepdims=True)
    r = lax.rsqrt(jnp.mean(xc * xc, axis=-1, keepdims=True) + EPS)
    return xc * r, r


def _ln_bwd(xh, r, g, dy):
    dxh = dy * g
    dx = r * (dxh - jnp.mean(dxh, axis=-1, keepdims=True) - xh * jnp.mean(dxh * xh, axis=-1, keepdims=True))
    return dx, jnp.sum(dy * xh, axis=0, keepdims=True), jnp.sum(dy, axis=0, keepdims=True)


def _rowwise(name, fn, rows, consts, outs, accs=(), tm=ROW_TILE):
    T = rows[0][0].shape[-2]
    tm = _tile(T, tm)
    n_in, n_o = len(rows) + len(consts), len(outs)

    def body(*refs):
        res = fn(*[r[...] for r in refs[:n_in]])
        for ref, val in zip(refs[n_in:n_in + n_o], res[:n_o]):
            ref[...] = val.astype(ref.dtype)
        acc_refs = refs[n_in + n_o:]
        if acc_refs:
            @pl.when(pl.program_id(0) == 0)
            def _():
                for ref, val in zip(acc_refs, res[n_o:]):
                    ref[...] = val

            @pl.when(pl.program_id(0) != 0)
            def _():
                for ref, val in zip(acc_refs, res[n_o:]):
                    ref[...] += val

    in_specs = []
    for row in rows:
        w, cb = row[1], row[2]
        if len(row) == 4:
            in_specs.append(pl.BlockSpec((None, tm, w), lambda i, cb=cb, ld=row[3]: (ld, i, cb)))
        else:
            in_specs.append(pl.BlockSpec((tm, w), lambda i, cb=cb: (i, cb)))
    in_specs += [pl.BlockSpec(c.shape, lambda i: (0, 0)) for c in consts]
    out_specs = [pl.BlockSpec((tm, w), lambda i: (i, 0)) for w, _ in outs]
    out_specs += [pl.BlockSpec(s, lambda i: (0, 0)) for s in accs]
    out_shape = [jax.ShapeDtypeStruct((T, w), dt) for w, dt in outs]
    out_shape += [jax.ShapeDtypeStruct(s, F32) for s in accs]
    return pl.pallas_call(body, grid=(T // tm,), in_specs=in_specs, out_specs=out_specs, out_shape=out_shape,
                          name=name, compiler_params=_params(1))(*[r[0] for r in rows], *consts)


def _tiled(name, fn, grid, pos, ins, outs):
    n_in = len(ins)

    def body(_, *refs):
        res = fn(*[r[...] for r in refs[:n_in]])
        for ref, val in zip(refs[n_in:], res):
            ref[...] = val.astype(ref.dtype)

    spec = pltpu.PrefetchScalarGridSpec(
        num_scalar_prefetch=1, grid=grid, in_specs=[pl.BlockSpec(bs, im) for _, bs, im in ins],
        out_specs=[pl.BlockSpec(bs, im) for _, _, bs, im in outs])
    return pl.pallas_call(body, grid_spec=spec, out_shape=[jax.ShapeDtypeStruct(s, d) for s, d, _, _ in outs],
                          name=name, compiler_params=_params(len(grid)))(pos, *[a for a, _, _ in ins])


def _cast_into(name, w, rp, cp, dtype, pos):
    L, r, c = w.shape

    def body(_, w_ref, o_ref):
        if (rp, cp) != (r, c):
            o_ref[...] = jnp.zeros_like(o_ref)
            o_ref[pl.ds(0, r), pl.ds(0, c)] = w_ref[...].astype(dtype)
        else:
            o_ref[...] = w_ref[...].astype(dtype)

    spec = pltpu.PrefetchScalarGridSpec(
        num_scalar_prefetch=1, grid=(L,), in_specs=[pl.BlockSpec((None, r, c), lambda l, p: (l, 0, 0))],
        out_specs=pl.BlockSpec((None, None, rp, cp), lambda l, p: (p[0], l, 0, 0)))
    return pl.pallas_call(body, grid_spec=spec, out_shape=jax.ShapeDtypeStruct((N_CHIPS, L, rp, cp), dtype),
                          name=name, compiler_params=_params(1))(pos, w)


_NN = (((1,), (0,)), ((), ()))
_NT = (((1,), (1,)), ((), ()))
_TN = (((0,), (0,)), ((), ()))


def _mm_blockout(name, x, w4, l, trans_w, tm=ROW_TILE):
    M, kx = x.shape
    nb, _, r, cc = w4.shape
    bo = r if trans_w else cc
    tm = _tile(M, tm)

    def body(x_ref, w_ref, o_ref):
        o_ref[...] = lax.dot_general(x_ref[...].astype(BF), w_ref[...].astype(BF), _NT if trans_w else _NN,
                                     preferred_element_type=F32).astype(o_ref.dtype)

    return pl.pallas_call(
        body, grid=(nb, M // tm),
        in_specs=[pl.BlockSpec((tm, kx), lambda b, i: (i, 0)),
                  pl.BlockSpec((None, None, r, cc), lambda b, i: (b, l, 0, 0))],
        out_specs=pl.BlockSpec((tm, bo), lambda b, i: (i, b)),
        out_shape=jax.ShapeDtypeStruct((M, nb * bo), BF), name=name, compiler_params=_params(2))(x, w4)


def _mm_blocksum(name, pairs, l, trans_w, tm=2 * ROW_TILE):
    x0, w0 = pairs[0]
    M = x0.shape[0]
    nb, _, r, cc = w0.shape
    bw, n_out = (cc, r) if trans_w else (r, cc)
    tm = _tile(M, tm)
    n_p = len(pairs)

    def body(*refs):
        o_ref, acc = refs[2 * n_p], refs[2 * n_p + 1]
        b = pl.program_id(1)

        @pl.when(b == 0)
        def _():
            acc[...] = jnp.zeros_like(acc)

        t = acc[...]
        for x_ref, w_ref in zip(refs[:n_p], refs[n_p:2 * n_p]):
            t = t + lax.dot_general(x_ref[...].astype(BF), w_ref[...].astype(BF), _NT if trans_w else _NN,
                                    preferred_element_type=F32)
        acc[...] = t

        @pl.when(b == nb - 1)
        def _():
            o_ref[...] = acc[...].astype(o_ref.dtype)

    in_specs = [pl.BlockSpec((tm, bw), lambda i, b: (i, b)) for _ in pairs]
    in_specs += [pl.BlockSpec((None, None, r, cc), lambda i, b: (b, l, 0, 0)) for _ in pairs]
    return pl.pallas_call(
        body, grid=(M // tm, nb), in_specs=in_specs, out_specs=pl.BlockSpec((tm, n_out), lambda i, b: (i, 0)),
        out_shape=jax.ShapeDtypeStruct((M, n_out), BF), scratch_shapes=[pltpu.VMEM((tm, n_out), F32)],
        name=name, compiler_params=_params(2))(*[p[0] for p in pairs], *[p[1] for p in pairs])


def _mm_tn(name, a, dy, buf, l, a_blocked, tk=ROW_TILE):
    T = a.shape[0]
    _, nb, R, C = buf.shape

    def body(a_ref, dy_ref, _, o_ref):
        o_ref[...] = lax.dot_general(a_ref[...].astype(BF), dy_ref[...].astype(BF), _TN,
                                     preferred_element_type=F32).astype(o_ref.dtype)

    if a_blocked:
        grid = (nb,)
        in_specs = [pl.BlockSpec((T, R), lambda b: (0, b)), pl.BlockSpec((T, C), lambda b: (0, 0)), ANY]
        out_specs = pl.BlockSpec((None, None, R, C), lambda b: (l, b, 0, 0))
    else:
        tk = min(tk, R)
        grid = (nb, R // tk)
        in_specs = [pl.BlockSpec((T, tk), lambda b, k: (0, k)), pl.BlockSpec((T, C), lambda b, k: (0, b)), ANY]
        out_specs = pl.BlockSpec((None, None, tk, C), lambda b, k: (l, b, k, 0))
    return pl.pallas_call(body, grid=grid, in_specs=in_specs, out_specs=out_specs,
                          out_shape=jax.ShapeDtypeStruct(buf.shape, buf.dtype), input_output_aliases={2: 0},
                          name=name, compiler_params=_params(len(grid)))(a, dy, buf)


def _pool_apply(x, win, row):
    s, k = x, 1
    while k < win:
        s = s + jnp.where(row >= k, pltpu.roll(s, k, 0), 0.0)
        k *= 2
    return s / jnp.minimum(row + 1, win).astype(F32) - x


def _pool_apply_t(dp, win, row):
    T = dp.shape[0]
    s, k = dp / jnp.minimum(row + 1, win).astype(F32), 1
    while k < win:
        s = s + jnp.where(row < T - k, pltpu.roll(s, T - k, 0), 0.0)
        k *= 2
    return s - dp


def _pool_fwd(z, w, scale):
    T = z.shape[0]

    def body(z_ref, w_ref, s_ref, o_ref):
        row = lax.broadcasted_iota(jnp.int32, (T, LANES), 0)
        for gi, win in enumerate(POOL_WINDOWS):
            cols = pl.ds(gi * LANES, LANES)
            pooled = _pool_apply(z_ref[:, cols].astype(F32), win, row)
            y = jnp.dot(pooled.astype(BF), w_ref[gi].astype(BF), preferred_element_type=F32)
            o_ref[:, cols] = (y * s_ref[:, cols]).astype(o_ref.dtype)

    return pl.pallas_call(
        body, grid=(1,),
        in_specs=[pl.BlockSpec((T, 512), lambda i: (0, ZB_POOL)), pl.BlockSpec(w.shape, lambda i: (0, 0, 0)),
                  pl.BlockSpec(scale.shape, lambda i: (0, 0))],
        out_specs=pl.BlockSpec((T, 512), lambda i: (0, 0)), out_shape=jax.ShapeDtypeStruct((T, 512), BF),
        name="pool_fwd", compiler_params=_params(1))(z, w, scale)


def _pool_bwd(dr, z, w, scale):
    T = z.shape[0]

    def body(dr_ref, z_ref, w_ref, s_ref, dz_ref, dw_ref, ds_ref):
        row = lax.broadcasted_iota(jnp.int32, (T, LANES), 0)
        for gi, win in enumerate(POOL_WINDOWS):
            cols = pl.ds(gi * LANES, LANES)
            pooled = _pool_apply(z_ref[:, cols].astype(F32), win, row).astype(BF)
            wg = w_ref[gi].astype(BF)
            y = jnp.dot(pooled, wg, preferred_element_type=F32)
            d = dr_ref[:, cols].astype(F32)
            ds_ref[:, cols] = jnp.sum(d * y, axis=0, keepdims=True)
            dy = (d * s_ref[:, cols]).astype(BF)
            dw_ref[gi] = lax.dot_general(pooled, dy, _TN, preferred_element_type=F32)
            dpooled = lax.dot_general(dy, wg, _NT, preferred_element_type=F32)
            dz_ref[:, cols] = _pool_apply_t(dpooled, win, row).astype(dz_ref.dtype)

    return pl.pallas_call(
        body, grid=(1,),
        in_specs=[pl.BlockSpec((T, 512), lambda i: (0, 0)), pl.BlockSpec((T, 512), lambda i: (0, ZB_POOL)),
                  pl.BlockSpec(w.shape, lambda i: (0, 0, 0)), pl.BlockSpec(scale.shape, lambda i: (0, 0))],
        out_specs=[pl.BlockSpec((T, 512), lambda i: (0, 0)), pl.BlockSpec(w.shape, lambda i: (0, 0, 0)),
                   pl.BlockSpec(scale.shape, lambda i: (0, 0))],
        out_shape=[jax.ShapeDtypeStruct((T, 512), BF), jax.ShapeDtypeStruct(w.shape, F32),
                   jax.ShapeDtypeStruct(scale.shape, F32)],
        name="pool_bwd", compiler_params=_params(1))(dr, z, w, scale)


def _tril(transposed=False):
    r = lax.broadcasted_iota(jnp.int32, (CHUNK, CHUNK), 0)
    c = lax.broadcasted_iota(jnp.int32, (CHUNK, CHUNK), 1)
    return c >= r if transposed else r >= c


def _sgu_fwd(z, ln_g, ln_b, w_s, bias):
    T = z.shape[0]
    tm = _tile(T)

    def body(zu_ref, zv_ref, g_ref, b_ref, w_ref, bias_ref, o_ref):
        gu, _ = _gelu_and_grad(zu_ref[...].astype(F32))
        gv, _ = _gelu_and_grad(zv_ref[...].astype(F32))
        xh, _ = _ln_stats(gv)
        v16 = (xh * g_ref[...] + b_ref[...]).astype(BF)
        tri = _tril()
        for h in range(SGU_HEADS):
            cols = slice(h * LANES, (h + 1) * LANES)
            wh = jnp.where(tri, w_ref[h], 0.0).astype(BF)
            for c in range(tm // CHUNK):
                rows = slice(c * CHUNK, (c + 1) * CHUNK)
                s = jnp.dot(wh, v16[rows, cols], preferred_element_type=F32) + bias_ref[:, cols]
                o_ref[rows, cols] = (gu[rows, cols] * s).astype(o_ref.dtype)

    small = [pl.BlockSpec(a.shape, lambda i, n=a.ndim: (0,) * n) for a in (ln_g, ln_b, w_s, bias)]
    return pl.pallas_call(
        body, grid=(T // tm,),
        in_specs=[pl.BlockSpec((tm, 512), lambda i: (i, ZB_U)), pl.BlockSpec((tm, 512), lambda i: (i, ZB_V))] + small,
        out_specs=pl.BlockSpec((tm, 512), lambda i: (i, 0)), out_shape=jax.ShapeDtypeStruct((T, 512), BF),
        name="sgu_fwd", compiler_params=_params(1))(z, z, ln_g, ln_b, w_s, bias)


def _sgu_bwd(dr, z, ln_g, ln_b, w_s, w_st, bias):
    T = z.shape[0]
    tm = _tile(T)
    n_steps = T // tm

    def body(dr_ref, zu_ref, zv_ref, g_ref, b_ref, w_ref, wt_ref, bias_ref,
             dzu_ref, dzv_ref, dg_ref, db_ref, dw_ref, dbias_ref, dgu_s, dv_s):
        i = pl.program_id(0)

        @pl.when(i == 0)
        def _():
            dg_ref[...] = jnp.zeros_like(dg_ref)
            db_ref[...] = jnp.zeros_like(db_ref)
            dw_ref[...] = jnp.zeros_like(dw_ref)
            dbias_ref[...] = jnp.zeros_like(dbias_ref)

        zu = zu_ref[...].astype(F32)
        zv = zv_ref[...].astype(F32)
        gu, gu_grad = _gelu_and_grad(zu)
        gv, gv_grad = _gelu_and_grad(zv)
        xh, r = _ln_stats(gv)
        v16 = (xh * g_ref[...] + b_ref[...]).astype(BF)
        dr = dr_ref[...].astype(F32)
        tri = _tril()
        for h in range(SGU_HEADS):
            cols = slice(h * LANES, (h + 1) * LANES)
            wh = jnp.where(tri, w_ref[h], 0.0).astype(BF)
            wht = jnp.where(_tril(transposed=True), wt_ref[h], 0.0).astype(BF)
            for c in range(tm // CHUNK):
                rows = slice(c * CHUNK, (c + 1) * CHUNK)
                v_blk = v16[rows, cols]
                s = jnp.dot(wh, v_blk, preferred_element_type=F32) + bias_ref[:, cols]
                ds = dr[rows, cols] * gu[rows, cols]
                dgu_s[rows, cols] = dr[rows, cols] * s
                ds16 = ds.astype(BF)
                dw_ref[h] += jnp.where(tri, lax.dot_general(ds16, v_blk, _NT, preferred_element_type=F32), 0.0)
                dv_s[rows, cols] = jnp.dot(wht, ds16, preferred_element_type=F32)
                dbias_ref[:, cols] += ds
        dzu_ref[...] = (dgu_s[...] * gu_grad).astype(dzu_ref.dtype)
        dgv, dg, db = _ln_bwd(xh, r, g_ref[...], dv_s[...])
        dzv_ref[...] = (dgv * gv_grad).astype(dzv_ref.dtype)
        dg_ref[...] += dg
        db_ref[...] += db

        @pl.when(i == n_steps - 1)
        def _():
            for h in range(SGU_HEADS):
                cols = slice(h * LANES, (h + 1) * LANES)
                tot = jnp.sum(dbias_ref[:, cols], axis=1, keepdims=True)
                dbias_ref[:, cols] = jnp.broadcast_to(tot, (CHUNK, LANES))

    small = (ln_g, ln_b, w_s, w_st, bias)
    small_specs = [pl.BlockSpec(a.shape, lambda i, n=a.ndim: (0,) * n) for a in small]
    return pl.pallas_call(
        body, grid=(n_steps,),
        in_specs=[pl.BlockSpec((tm, 512), lambda i: (i, 0)), pl.BlockSpec((tm, 512), lambda i: (i, ZB_U)),
                  pl.BlockSpec((tm, 512), lambda i: (i, ZB_V))] + small_specs,
        out_specs=[pl.BlockSpec((tm, 512), lambda i: (i, 0)), pl.BlockSpec((tm, 512), lambda i: (i, 0)),
                   pl.BlockSpec((1, 512), lambda i: (0, 0)), pl.BlockSpec((1, 512), lambda i: (0, 0)),
                   pl.BlockSpec(w_s.shape, lambda i: (0, 0, 0)), pl.BlockSpec(bias.shape, lambda i: (0, 0))],
        out_shape=[jax.ShapeDtypeStruct((T, 512), BF), jax.ShapeDtypeStruct((T, 512), BF),
                   jax.ShapeDtypeStruct((1, 512), F32), jax.ShapeDtypeStruct((1, 512), F32),
                   jax.ShapeDtypeStruct(w_s.shape, F32), jax.ShapeDtypeStruct(bias.shape, F32)],
        scratch_shapes=[pltpu.VMEM((tm, 512), F32), pltpu.VMEM((tm, 512), F32)],
        name="sgu_bwd", compiler_params=_params(1))(dr, z, z, ln_g, ln_b, w_s, w_st, bias)


def _conv_fwd(z, convk, l, bias):
    T = z.shape[0]

    def body(za_ref, zb_ref, k_ref, b_ref, o_ref):
        xg = za_ref[...].astype(F32) * _sigmoid(zb_ref[...].astype(F32))
        xp = jnp.concatenate([jnp.zeros((CONV_PAD, LANES), F32), xg], axis=0)
        kw = k_ref[...]
        acc = jnp.broadcast_to(b_ref[...], (T, LANES))
        for k in range(CONV_TAPS):
            sh = CONV_TAPS - 1 - k
            tap = xp if sh == 0 else pltpu.roll(xp, sh, 0)
            acc = acc + kw[k:k + 1, :] * tap[CONV_PAD:, :]
        o_ref[...] = acc.astype(o_ref.dtype)

    return pl.pallas_call(
        body, grid=(4,),
        in_specs=[pl.BlockSpec((T, LANES), lambda g: (0, 4 * ZB_A + g)),
                  pl.BlockSpec((T, LANES), lambda g: (0, 4 * ZB_B + g)),
                  pl.BlockSpec((None, None, CONV_PAD, LANES), lambda g: (g, l, 0, 0)),
                  pl.BlockSpec((1, LANES), lambda g: (0, g))],
        out_specs=pl.BlockSpec((T, LANES), lambda g: (0, g)), out_shape=jax.ShapeDtypeStruct((T, 512), BF),
        name="conv_fwd", compiler_params=_params(1))(z, z, convk, bias)


def _conv_bwd(dy, z, convk, l):
    T = z.shape[0]

    def body(dy_ref, za_ref, zb_ref, k_ref, dza_ref, dzb_ref, dk_ref, db_ref):
        a = za_ref[...].astype(F32)
        sg = _sigmoid(zb_ref[...].astype(F32))
        d = dy_ref[...].astype(F32)
        kw = k_ref[...]
        xp = jnp.concatenate([jnp.zeros((CONV_PAD, LANES), F32), a * sg], axis=0)
        dp = jnp.concatenate([d, jnp.zeros((CONV_PAD, LANES), F32)], axis=0)
        dxg = jnp.zeros((T, LANES), F32)
        dk_ref[...] = jnp.zeros_like(dk_ref)
        for k in range(CONV_TAPS):
            sh = CONV_TAPS - 1 - k
            tap = xp if sh == 0 else pltpu.roll(xp, sh, 0)
            dk_ref[k:k + 1, :] = jnp.sum(d * tap[CONV_PAD:, :], axis=0, keepdims=True)
            dtap = dp if sh == 0 else pltpu.roll(dp, T + CONV_PAD - sh, 0)
            dxg = dxg + kw[k:k + 1, :] * dtap[:T, :]
        db_ref[...] = jnp.sum(d, axis=0, keepdims=True)
        dza_ref[...] = (dxg * sg).astype(dza_ref.dtype)
        dzb_ref[...] = (dxg * a * sg * (1.0 - sg)).astype(dzb_ref.dtype)

    col = pl.BlockSpec((T, LANES), lambda g: (0, g))
    return pl.pallas_call(
        body, grid=(4,),
        in_specs=[col, pl.BlockSpec((T, LANES), lambda g: (0, 4 * ZB_A + g)),
                  pl.BlockSpec((T, LANES), lambda g: (0, 4 * ZB_B + g)),
                  pl.BlockSpec((None, None, CONV_PAD, LANES), lambda g: (g, l, 0, 0))],
        out_specs=[col, col, pl.BlockSpec((CONV_PAD, LANES), lambda g: (0, g)),
                   pl.BlockSpec((1, LANES), lambda g: (0, g))],
        out_shape=[jax.ShapeDtypeStruct((T, 512), BF), jax.ShapeDtypeStruct((T, 512), BF),
                   jax.ShapeDtypeStruct((CONV_PAD, 512), F32), jax.ShapeDtypeStruct((1, 512), F32)],
        name="conv_bwd", compiler_params=_params(1))(dy, z, z, convk)


D = D_MODEL


def _rms_call(name, h, g):
    return _rowwise(name, lambda x, gg: (_rms_fwd(x.astype(F32), gg),), [(h, D, 0)], [g], [(D, BF)])[0]


def _res_norm_call(name, h, f, g, coef):
    return _rowwise(name, lambda x, ff, gg: (x + coef * _rms_fwd(ff.astype(F32), gg),),
                    [(h, D, 0), (f, D, 0)], [g], [(D, F32)])[0]


def _res_norm_bwd_call(name, dh, f, g, coef):
    return _rowwise(name, lambda d, ff, gg: _rms_bwd(ff.astype(F32), gg, coef * d),
                    [(dh, D, 0), (f, D, 0)], [g], [(D, BF)], [(1, D)])


def _pre_norm_bwd_call(name, dh, h, dn, g):
    def fn(d, x, dnn, gg):
        dx, dg = _rms_bwd(x, gg, dnn.astype(F32))
        return d + dx, dg

    return _rowwise(name, fn, [(dh, D, 0), (h, D, 0), (dn, D, 0)], [g], [(D, F32)], [(1, D)])


def _ffn_fwd(l, h, S, W, pre):
    n = _rms_call("ffn_norm", h, S[pre + "_pre_g"])
    gp = _mm_blockout("ffn_gate", n, W[pre + "_w_gate"], l, True)
    u = _mm_blockout("ffn_up", n, W[pre + "_w_up"], l, True)
    fw = gp.shape[1]
    a = _rowwise("ffn_act", lambda g, uu: (_silu_and_grad(g.astype(F32))[0] * uu.astype(F32),),
                 [(gp, fw, 0), (u, fw, 0)], [], [(fw, BF)])[0]
    f = _mm_blocksum("ffn_down", [(a, W[pre + "_w_down"])], l, False)
    out = _res_norm_call("ffn_res", h, f, S[pre + "_post_g"], 0.5)
    return out, dict(h=h, n=n, gp=gp, u=u, a=a, f=f)


def _ffn_bwd(l, dh, sv, S, W, G, SG, pre):
    df, SG[pre + "_post_g"] = _res_norm_bwd_call("ffn_res_bwd", dh, sv["f"], S[pre + "_post_g"], 0.5)
    da = _mm_blockout("ffn_dact", df, W[pre + "_w_down"], l, True)
    fw = da.shape[1]

    def act_bwd(d, g, uu):
        act, grad = _silu_and_grad(g.astype(F32))
        d = d.astype(F32)
        return d * uu.astype(F32) * grad, d * act

    dgp, du = _rowwise("ffn_act_bwd", act_bwd, [(da, fw, 0), (sv["gp"], fw, 0), (sv["u"], fw, 0)], [],
                       [(fw, BF), (fw, BF)])
    G[pre + "_w_down"] = _mm_tn("ffn_dw_down", sv["a"], df, G[pre + "_w_down"], l, True)
    G[pre + "_w_gate"] = _mm_tn("ffn_dw_gate", dgp, sv["n"], G[pre + "_w_gate"], l, True)
    G[pre + "_w_up"] = _mm_tn("ffn_dw_up", du, sv["n"], G[pre + "_w_up"], l, True)
    dn = _mm_blocksum("ffn_dn", [(dgp, W[pre + "_w_gate"]), (du, W[pre + "_w_up"])], l, False)
    dh_in, SG[pre + "_pre_g"] = _pre_norm_bwd_call("pre_norm_bwd", dh, sv["h"], dn, S[pre + "_pre_g"])
    return dh_in


def _gate_rows(z):
    return [(z, 512, ZB_GATES + j) for j in range(6)]


def _gates(zg):
    return [_sigmoid(jnp.concatenate([zg[2 * k].astype(F32), zg[2 * k + 1].astype(F32)], axis=1)) for k in range(3)]


def _mix_fwd(l, h, S, W):
    n = _rms_call("mix_norm", h, S["mix_pre_g"])
    z = _mm_blockout("mix_in", n, W["w_in"], l, False)
    r_pool = _pool_fwd(z, S["pool_w"], S["pool_scale"])
    r_sgu = _sgu_fwd(z, S["sgu_ln_g"], S["sgu_ln_b"], S["sgu_w_s"], S["sgu_bias"])
    yc = _conv_fwd(z, W["conv_dw_k"], l, S["conv_dw_b"])

    def ln_silu(y, g, b):
        xh, _ = _ln_stats(y.astype(F32))
        return (_silu_and_grad(xh * g + b)[0],)

    r_conv = _rowwise("conv_ln", ln_silu, [(yc, 512, 0)], [S["conv_ln_g"], S["conv_ln_b"]], [(512, BF)])[0]
    y_pool = _mm_blockout("branch_out", r_pool, W["w_pool_out"], l, False)
    y_sgu = _mm_blockout("branch_out", r_sgu, W["w_sgu_out"], l, False)
    y_conv = _mm_blockout("branch_out", r_conv, W["w_conv_out"], l, False)

    def merge(*t):
        g = _gates(t[:6])
        return (g[0] * t[6].astype(F32) + g[1] * t[7].astype(F32) + g[2] * t[8].astype(F32),)

    merged = _rowwise("mix_merge", merge, _gate_rows(z) + [(y_pool, D, 0), (y_sgu, D, 0), (y_conv, D, 0)], [],
                      [(D, BF)])[0]
    o = _mm_blocksum("mix_out", [(merged, W["w_out"])], l, False)
    out = _res_norm_call("mix_res", h, o, S["mix_post_g"], 1.0)
    return out, dict(h=h, n=n, z=z, r_pool=r_pool, r_sgu=r_sgu, yc=yc, r_conv=r_conv, y_pool=y_pool, y_sgu=y_sgu,
                     y_conv=y_conv, merged=merged, o=o)


def _mix_bwd(l, dh, sv, S, W, G, SG):
    z = sv["z"]
    do, SG["mix_post_g"] = _res_norm_bwd_call("mix_res_bwd", dh, sv["o"], S["mix_post_g"], 1.0)
    dmerged = _mm_blockout("mix_dmerged", do, W["w_out"], l, True)
    G["w_out"] = _mm_tn("mix_dw_out", sv["merged"], do, G["w_out"], l, True)

    def merge_bwd(dm, *t):
        g = _gates(t[:6])
        dm = dm.astype(F32)
        dzg = [dm * t[6 + k].astype(F32) * g[k] * (1.0 - g[k]) for k in range(3)]
        return dm * g[0], dm * g[1], dm * g[2], jnp.concatenate(dzg, axis=1)

    rows = [(dmerged, D, 0)] + _gate_rows(z) + [(sv["y_pool"], D, 0), (sv["y_sgu"], D, 0), (sv["y_conv"], D, 0)]
    dy_pool, dy_sgu, dy_conv, dzg = _rowwise("mix_merge_bwd", merge_bwd, rows, [], [(D, BF)] * 3 + [(3 * D, BF)],
                                             tm=ROW_TILE // 2)
    dr = {}
    for br, dy in (("pool", dy_pool), ("sgu", dy_sgu), ("conv", dy_conv)):
        wn = "w_%s_out" % br
        G[wn] = _mm_tn("branch_dw", sv["r_" + br], dy, G[wn], l, False)
        dr[br] = _mm_blocksum("branch_dr", [(dy, W[wn])], l, True)
    dz_pool, SG["pool_w"], SG["pool_scale"] = _pool_bwd(dr["pool"], z, S["pool_w"], S["pool_scale"])
    dzu, dzv, SG["sgu_ln_g"], SG["sgu_ln_b"], SG["sgu_w_s"], dbias = _sgu_bwd(
        dr["sgu"], z, S["sgu_ln_g"], S["sgu_ln_b"], S["sgu_w_s"], S["sgu_w_st"], S["sgu_bias"])
    SG["sgu_b_s"] = dbias[:, ::LANES].T

    def ln_silu_bwd(d, y, g, b):
        xh, r = _ln_stats(y.astype(F32))
        _, grad = _silu_and_grad(xh * g + b)
        return _ln_bwd(xh, r, g, d.astype(F32) * grad)

    dyc, SG["conv_ln_g"], SG["conv_ln_b"] = _rowwise(
        "conv_ln_bwd", ln_silu_bwd, [(dr["conv"], 512, 0), (sv["yc"], 512, 0)], [S["conv_ln_g"], S["conv_ln_b"]],
        [(512, BF)], [(1, 512), (1, 512)])
    dza, dzb, SG["conv_dw_k"], SG["conv_dw_b"] = _conv_bwd(dyc, z, W["conv_dw_k"], l)
    dz = jnp.concatenate([dz_pool, dzu, dzv, dza, dzb, dzg], axis=1)
    G["w_in"] = _mm_tn("mix_dw_in", sv["n"], dz, G["w_in"], l, False)
    dn = _mm_blocksum("mix_dn", [(dz, W["w_in"])], l, True)
    dh_in, SG["mix_pre_g"] = _pre_norm_bwd_call("pre_norm_bwd", dh, sv["h"], dn, S["mix_pre_g"])
    return dh_in


def _ple_fwd(l, h, p_l, S, W):
    n = _rms_call("ple_norm", h, S["ple_pre_g"])
    e = _mm_blockout("ple_proj", p_l, W["ple_w_proj"], l, False)
    gp = _mm_blocksum("ple_gate", [(n, W["ple_w_gate"])], l, False)

    def res(x, ee, g, gg):
        return (x + _rms_fwd(_sigmoid(g.astype(F32)) * ee.astype(F32), gg),)

    out = _rowwise("ple_res", res, [(h, D, 0), (e, D, 0), (gp, D, 0)], [S["ple_post_g"]], [(D, F32)])[0]
    return out, dict(h=h, n=n, e=e, gp=gp, p=p_l)


def _ple_bwd(l, dh, sv, S, W, G, SG):
    def res_bwd(d, ee, g, gg):
        sg = _sigmoid(g.astype(F32))
        ee = ee.astype(F32)
        dq, dg = _rms_bwd(sg * ee, gg, d)
        return dq * sg, dq * ee * sg * (1.0 - sg), dg

    de, dgp, SG["ple_post_g"] = _rowwise("ple_res_bwd", res_bwd, [(dh, D, 0), (sv["e"], D, 0), (sv["gp"], D, 0)],
                                         [S["ple_post_g"]], [(D, BF), (D, BF)], [(1, D)])
    G["ple_w_proj"] = _mm_tn("ple_dw_proj", sv["p"], de, G["ple_w_proj"], l, False)
    G["ple_w_gate"] = _mm_tn("ple_dw_gate", sv["n"], dgp, G["ple_w_gate"], l, True)
    dn = _mm_blockout("ple_dn", dgp, W["ple_w_gate"], l, True)
    dh_in, SG["ple_pre_g"] = _pre_norm_bwd_call("pre_norm_bwd", dh, sv["h"], dn, S["ple_pre_g"])
    return dh_in


def _layer_small(a, l):
    S = {}
    for name in SMALL:
        v = a[name][l]
        S[name] = v.reshape(1, -1) if v.ndim == 1 else v
    S["sgu_w_st"] = jnp.swapaxes(S["sgu_w_s"], 1, 2)
    S["sgu_bias"] = jnp.repeat(S["sgu_b_s"].T, LANES, axis=1)
    return S


def _layer_fwd(l, h, p_l, S, W):
    h, sv1 = _ffn_fwd(l, h, S, W, "ffn1")
    h, sv2 = _mix_fwd(l, h, S, W)
    h, sv3 = _ffn_fwd(l, h, S, W, "ffn2")
    h, sv4 = _ple_fwd(l, h, p_l, S, W)
    return h, (sv1, sv2, sv3, sv4)


def _layer_bwd(l, dh, sv, S, W, G):
    SG = {}
    dh = _ple_bwd(l, dh, sv[3], S, W, G, SG)
    dh = _ffn_bwd(l, dh, sv[2], S, W, G, SG, "ffn2")
    dh = _mix_bwd(l, dh, sv[1], S, W, G, SG)
    dh = _ffn_bwd(l, dh, sv[0], S, W, G, SG, "ffn1")
    return dh, SG


def _local_fwd_bwd(a, W, x, target):
    L = a["ffn1_pre_g"].shape[0]
    small = [_layer_small(a, l) for l in range(L)]
    h, saved = x, []
    for l in range(L):
        h, sv = _layer_fwd(l, h, a["p"][l, 0], small[l], W)
        saved.append(sv)

    def loss_fn(y, t):
        e = y - t
        return e * (1.0 / D), jnp.sum(e * e, axis=0, keepdims=True)

    dh, lsum = _rowwise("loss", loss_fn, [(h, D, 0), (target, D, 0)], [], [(D, F32)], [(1, D)])
    loss = 0.5 * jnp.sum(lsum) / D
    G = {name: lax.empty((L, N_CHIPS, rp, cp), BF) for name, _, _, _, rp, cp in BIG}
    small_grads = [None] * L
    for l in reversed(range(L)):
        dh, small_grads[l] = _layer_bwd(l, dh, saved[l], small[l], W, G)
    return loss, dh, G, small_grads


def _place():
    x, y, c = lax.axis_index("x"), lax.axis_index("y"), lax.axis_index("c")
    chips = [(1 - x, y), (x, 1 - y), (1 - x, 1 - y)]
    return x, y, c, chips


def _remote(src, dst, send_sem, recv_sem, to):
    return pltpu.make_async_remote_copy(src_ref=src, dst_ref=dst, send_sem=send_sem, recv_sem=recv_sem,
                                        device_id=to, device_id_type=MESH)


def _exchange_call(name, body, ins, out_shapes, n_remote, in_place=False):
    scratch = [pltpu.SemaphoreType.DMA((n_remote,)), pltpu.SemaphoreType.DMA((n_remote,))]
    aliases = {i: i for i in range(len(ins))} if in_place else {}
    return pl.pallas_call(body, in_specs=[ANY] * len(ins), out_specs=[ANY] * len(out_shapes), out_shape=out_shapes,
                          scratch_shapes=scratch, input_output_aliases=aliases, name=name)(*ins)


def _gather_weights(bufs):
    n = len(bufs)
    lh = bufs[0].shape[1] // 2

    def body(*refs):
        outs = refs[n:2 * n]
        send, recv = refs[2 * n:]
        x, y, c, chips = _place()
        me = 2 * x + y
        mine, other = pl.ds(c * lh, lh), pl.ds((1 - c) * lh, lh)
        sent = []
        for a in range(n):
            for j, (cx, cy) in enumerate(chips):
                k = a * 6 + j
                piece = outs[a].at[me, mine]
                sent.append(_remote(piece, piece, send.at[k], recv.at[k], (cx, cy, c)))
                sent[-1].start()
        for a in range(n):
            for j, (cx, cy) in enumerate(chips):
                k = a * 6 + j
                piece = outs[a].at[2 * cx + cy, mine]
                _remote(piece, piece, send.at[k], recv.at[k], (cx, cy, c)).wait_recv()
                sent.append(_remote(piece, piece, send.at[k + 3], recv.at[k + 3], (x, y, 1 - c)))
                sent[-1].start()
        for a in range(n):
            for j, (cx, cy) in enumerate(chips):
                k = a * 6 + j + 3
                piece = outs[a].at[2 * cx + cy, other]
                _remote(piece, piece, send.at[k], recv.at[k], (x, y, 1 - c)).wait_recv()
        for cp in sent:
            cp.wait_send()

    shapes = [jax.ShapeDtypeStruct(b.shape, b.dtype) for b in bufs]
    return _exchange_call("gather_weights", body, bufs, shapes, 6 * n, in_place=True)


def _rs_pair(grads):
    n = len(grads)

    def body(*refs):
        ins, got = refs[:n], refs[n:2 * n]
        send, recv = refs[2 * n:]
        x, y, c, _ = _place()
        cps = []
        for a in range(n):
            rh = ins[a].shape[2] // 2
            cps.append(_remote(ins[a].at[:, :, pl.ds((1 - c) * rh, rh)], got[a], send.at[a], recv.at[a],
                               (x, y, 1 - c)))
            cps[-1].start()
        for cp in cps:
            cp.wait()

    half = [jax.ShapeDtypeStruct(g.shape[:2] + (g.shape[2] // 2, g.shape[3]), g.dtype) for g in grads]
    return _exchange_call("rs_pair", body, grads, half, n)


def _rs_cross(parts):
    n = len(parts)

    def body(*refs):
        ins, outs = refs[:n], refs[n:2 * n]
        send, recv = refs[2 * n:]
        x, y, c, chips = _place()
        cps = []
        for a in range(n):
            for j, (cx, cy) in enumerate(chips):
                k = a * 3 + j
                cps.append(_remote(ins[a].at[:, 2 * cx + cy], outs[a].at[j], send.at[k], recv.at[k], (cx, cy, c)))
                cps[-1].start()
        for cp in cps:
            cp.wait()

    shapes = [jax.ShapeDtypeStruct((3, q.shape[0]) + q.shape[2:], q.dtype) for q in parts]
    return _exchange_call("rs_cross", body, parts, shapes, 3 * n)


def _rs_share(bufs):
    n = len(bufs)

    def body(*refs):
        outs = refs[n:2 * n]
        send, recv = refs[2 * n:]
        x, y, c, _ = _place()
        cps = []
        for a in range(n):
            rh = outs[a].shape[1] // 2
            mine = outs[a].at[:, pl.ds(c * rh, rh)]
            cps.append(_remote(mine, mine, send.at[a], recv.at[a], (x, y, 1 - c)))
            cps[-1].start()
        for a in range(n):
            rh = outs[a].shape[1] // 2
            theirs = outs[a].at[:, pl.ds((1 - c) * rh, rh)]
            _remote(theirs, theirs, send.at[a], recv.at[a], (x, y, 1 - c)).wait_recv()
            cps[a].wait_send()

    shapes = [jax.ShapeDtypeStruct(b.shape, b.dtype) for b in bufs]
    return _exchange_call("rs_share", body, bufs, shapes, n, in_place=True)


def _peers(x, y, c):
    return [(1 - x if m & 4 else x, 1 - y if m & 2 else y, 1 - c if m & 1 else c) for m in range(1, 8)]


def _scatter_small(v3):
    def body(v_ref, o_ref, send, recv):
        x, y, c, _ = _place()
        cps = []
        for m, (px, py, pc) in enumerate(_peers(x, y, c)):
            cps.append(_remote(v_ref.at[4 * px + 2 * py + pc], o_ref.at[m], send.at[m], recv.at[m], (px, py, pc)))
            cps[-1].start()
        for cp in cps:
            cp.wait()

    return _exchange_call("scatter_small", body, [v3], [jax.ShapeDtypeStruct((7,) + v3.shape[1:], v3.dtype)], 7)[0]


def _gather_small(buf):
    def body(_, o_ref, send, recv):
        x, y, c, _ = _place()
        mine = o_ref.at[4 * x + 2 * y + c]
        peers = _peers(x, y, c)
        cps = []
        for m, to in enumerate(peers):
            cps.append(_remote(mine, mine, send.at[m], recv.at[m], to))
            cps[-1].start()
        for m, (px, py, pc) in enumerate(peers):
            slab = o_ref.at[4 * px + 2 * py + pc]
            _remote(slab, slab, send.at[m], recv.at[m], (px, py, pc)).wait_recv()
            cps[m].wait_send()

    return _exchange_call("gather_small", body, [buf], [jax.ShapeDtypeStruct(buf.shape, buf.dtype)], 7,
                          in_place=True)[0]


def _reduce_scatter_grads(G, pos):
    names = [b[0] for b in BIG]
    grads = [G[nm] for nm in names]
    parts = []
    for g, q in zip(grads, _rs_pair(grads)):
        L, nb, R, C = g.shape
        rh = R // 2
        tm = _tile(rh)
        nh = rh // tm
        t = _tiled("rs_add_pair", lambda u, w: (u.astype(F32) + w.astype(F32),), (L * nb, nh), pos,
                   [(g.reshape(L * nb * R, C), (tm, C), lambda b, i, p: (b * 2 * nh + p[1] * nh + i, 0)),
                    (q.reshape(L * nb * rh, C), (tm, C), lambda b, i, p: (b * nh + i, 0))],
                   [((L * nb * rh, C), BF, (tm, C), lambda b, i, p: (b * nh + i, 0))])[0]
        parts.append(t.reshape(L, nb, rh, C))
    bufs = []
    for t, s in zip(parts, _rs_cross(parts)):
        L, nb, rh, C = t.shape
        tm = _tile(rh)
        nh = rh // tm

        def add(own, s0, s1, s2):
            return (((own.astype(F32) + s0.astype(F32)) + s1.astype(F32)) + s2.astype(F32),)

        slots = s.reshape(3, L * rh, C)
        ins = [(t.reshape(L * nb * rh, C), (tm, C), lambda l, i, p: ((l * nb + p[0]) * nh + i, 0))]
        ins += [(slots, (None, tm, C), lambda l, i, p, j=j: (j, l * nh + i, 0)) for j in range(3)]
        buf = _tiled("rs_add_chips", add, (L, nh), pos, ins,
                     [((L * 2 * rh, C), F32, (tm, C), lambda l, i, p: (l * 2 * nh + p[1] * nh + i, 0))])[0]
        bufs.append(buf.reshape(L, 2 * rh, C))
    return dict(zip(names, _rs_share(bufs)))


def _allreduce_small(v, pos):
    rows = v.shape[0]
    rs = rows // 8
    v3 = v.reshape(8, rs, LANES)
    got = _scatter_small(v3)
    tm = _tile(rs)
    ins = [(v3, (None, tm, LANES), lambda i, p: (p[2], i, 0))]
    ins += [(got, (None, tm, LANES), lambda i, p, m=m: (m, i, 0)) for m in range(7)]
    buf = _tiled("sum_small", lambda *t: (((((((t[0] + t[1]) + t[2]) + t[3]) + t[4]) + t[5]) + t[6]) + t[7],),
                 (rs // tm,), pos, ins, [((8, rs, LANES), F32, (None, tm, LANES), lambda i, p: (p[2], i, 0))])[0]
    return _gather_small(buf).reshape(rows, LANES)


def _adamw_math(w, g, m, v):
    m = ADAM_B1 * m + (1.0 - ADAM_B1) * g
    v = ADAM_B2 * v + (1.0 - ADAM_B2) * (g * g)
    m_hat = m / (1.0 - ADAM_B1 ** ADAM_STEP)
    v_hat = v / (1.0 - ADAM_B2 ** ADAM_STEP)
    return -ADAM_LR * (m_hat / (jnp.sqrt(v_hat) + ADAM_EPS) + ADAM_WD * w), m, v


def _adamw(w, g, m, v):
    L, R, C = w.shape
    tr = _tile(R, 256)

    def body(w_ref, g_ref, m_ref, v_ref, go_ref, d_ref, mo_ref, vo_ref):
        gv = g_ref[...]
        d, mn, vn = _adamw_math(w_ref[...], gv, m_ref[...], v_ref[...])
        go_ref[...] = gv
        d_ref[...] = d
        mo_ref[...] = mn
        vo_ref[...] = vn

    spec = pl.BlockSpec((None, tr, C), lambda l, i: (l, i, 0))
    out = jax.ShapeDtypeStruct(w.shape, F32)
    return pl.pallas_call(body, grid=(L, R // tr), in_specs=[spec] * 4, out_specs=[spec] * 4, out_shape=[out] * 4,
                          name="adamw", compiler_params=_params(2))(w, g, m, v)


def _pack(parts):
    flat = jnp.concatenate([q.reshape(-1, LANES) for q in parts], axis=0)
    return jnp.pad(flat, ((0, -flat.shape[0] % ROW_TILE), (0, 0)))


def _unpack(flat, like):
    out, r = [], 0
    for q in like:
        n = q.size // LANES
        out.append(flat[r:r + n].reshape(q.shape))
        r += n
    return out


def _train_step(a):
    a = dict(a)
    L = a["ffn1_pre_g"].shape[0]
    x, y, c, _ = _place()
    chip = 2 * x + y
    pos = jnp.stack([chip, c, 2 * chip + c]).astype(jnp.int32)
    for name in TRANSPOSED:
        for pre in ("", "m_", "v_"):
            a[pre + name] = jnp.swapaxes(a[pre + name], 1, 2)

    bufs = [_cast_into("cast_weight", a[name], rp, cp, BF, pos) for name, _, _, _, rp, cp in BIG]
    bufs.append(_cast_into("pad_conv_taps", a["conv_dw_k"].reshape(L, CONV_TAPS, LANES), CONV_PAD, LANES, F32, pos))
    W = dict(zip([b[0] for b in BIG] + ["conv_dw_k"], _gather_weights(bufs)))

    loss, grad_x, G, small_grads = _local_fwd_bwd(a, W, a["x"][0], a["loss_target"][0])
    loss = lax.psum(loss, ("x", "y", "c"))

    big_grads = _reduce_scatter_grads(G, pos)

    small_names = SMALL + ("conv_dw_k",)
    stacked = [jnp.stack([small_grads[l][name] for l in range(L)]) for name in small_names]
    reduced = dict(zip(small_names, _unpack(_allreduce_small(_pack(stacked), pos), stacked)))

    grads, deltas, new_m, new_v = {}, {}, {}, {}
    for name, _, _, _, _, _ in BIG:
        res = _adamw(a[name], big_grads[name], a["m_" + name], a["v_" + name])
        if name in TRANSPOSED:
            res = [jnp.swapaxes(r, 1, 2) for r in res]
        grads[name], deltas[name], new_m[name], new_v[name] = res
    taps = lax.dynamic_slice_in_dim(reduced["conv_dw_k"], chip * LANES, LANES, axis=2)[:, :CONV_TAPS]
    grads["conv_dw_k"] = taps.reshape(a["conv_dw_k"].shape)
    for name in SMALL:
        grads[name] = reduced[name].reshape(a[name].shape)
    tiny = SMALL + ("conv_dw_k",)
    shapes = [a[name] for name in tiny]
    res = _adamw(*[_pack([a[pre + name] if pre != "g" else grads[name] for name in tiny])[None]
                   for pre in ("", "g", "m_", "v_")])
    for dst, flat in zip((deltas, new_m, new_v), res[1:]):
        for name, val in zip(tiny, _unpack(flat[0], shapes)):
            dst[name] = val

    return (loss, grad_x[None], *[grads[n] for n in WEIGHTS], *[deltas[n] for n in WEIGHTS],
            *[new_m[n] for n in WEIGHTS], *[new_v[n] for n in WEIGHTS])


def kernel(x, p, ffn1_pre_g, ffn1_w_gate, ffn1_w_up, ffn1_w_down, ffn1_post_g, mix_pre_g, w_in, pool_w, pool_scale, w_pool_out, sgu_ln_g, sgu_ln_b, sgu_w_s, sgu_b_s, w_sgu_out, conv_dw_k, conv_dw_b, conv_ln_g, conv_ln_b, w_conv_out, w_out, mix_post_g, ffn2_pre_g, ffn2_w_gate, ffn2_w_up, ffn2_w_down, ffn2_post_g, ple_w_proj, ple_pre_g, ple_w_gate, ple_post_g, loss_target, m_ffn1_pre_g, m_ffn1_w_gate, m_ffn1_w_up, m_ffn1_w_down, m_ffn1_post_g, m_mix_pre_g, m_w_in, m_pool_w, m_pool_scale, m_w_pool_out, m_sgu_ln_g, m_sgu_ln_b, m_sgu_w_s, m_sgu_b_s, m_w_sgu_out, m_conv_dw_k, m_conv_dw_b, m_conv_ln_g, m_conv_ln_b, m_w_conv_out, m_w_out, m_mix_post_g, m_ffn2_pre_g, m_ffn2_w_gate, m_ffn2_w_up, m_ffn2_w_down, m_ffn2_post_g, m_ple_w_proj, m_ple_pre_g, m_ple_w_gate, m_ple_post_g, v_ffn1_pre_g, v_ffn1_w_gate, v_ffn1_w_up, v_ffn1_w_down, v_ffn1_post_g, v_mix_pre_g, v_w_in, v_pool_w, v_pool_scale, v_w_pool_out, v_sgu_ln_g, v_sgu_ln_b, v_sgu_w_s, v_sgu_b_s, v_w_sgu_out, v_conv_dw_k, v_conv_dw_b, v_conv_ln_g, v_conv_ln_b, v_w_conv_out, v_w_out, v_mix_post_g, v_ffn2_pre_g, v_ffn2_w_gate, v_ffn2_w_up, v_ffn2_w_down, v_ffn2_post_g, v_ple_w_proj, v_ple_pre_g, v_ple_w_gate, v_ple_post_g):
    return _train_step(dict(locals()))
```

```python
import math

import jax
import jax.numpy as jnp
from jax import lax
from jax.experimental import pallas as pl
from jax.experimental.pallas import tpu as pltpu

BF = jnp.bfloat16
F32 = jnp.float32
EPS = 1e-6
D_MODEL = 1024
LANES = 128
N_CHIPS = 4
FFN_SHARD = 704
FFN_SHARD_PAD = 768
POOL_WINDOWS = (2, 4, 8, 16)
SGU_HEADS = 4
CHUNK = 128
CONV_TAPS = 31
CONV_PAD = 32
ROW_TILE = 512
VMEM_LIMIT_BYTES = 56 * 1024 * 1024
ADAM_LR, ADAM_B1, ADAM_B2, ADAM_EPS, ADAM_WD, ADAM_STEP = 0.001, 0.9, 0.999, 1e-08, 0.01, 10
MESH = pl.DeviceIdType.MESH
ANY = pl.BlockSpec(memory_space=pl.ANY)

ZB_POOL, ZB_U, ZB_V, ZB_A, ZB_B, ZB_GATES = 0, 1, 2, 3, 4, 5

TRANSPOSED = ("ffn1_w_gate", "ffn1_w_up", "ffn2_w_gate", "ffn2_w_up")
BIG = (
    ("ffn1_w_gate", "row", FFN_SHARD, 1024, FFN_SHARD_PAD, 1024),
    ("ffn1_w_up", "row", FFN_SHARD, 1024, FFN_SHARD_PAD, 1024),
    ("ffn1_w_down", "row", FFN_SHARD, 1024, FFN_SHARD_PAD, 1024),
    ("w_in", "col", 1024, 1408, 1024, 1408),
    ("w_pool_out", "col", 512, 256, 512, 256),
    ("w_sgu_out", "col", 512, 256, 512, 256),
    ("w_conv_out", "col", 512, 256, 512, 256),
    ("w_out", "row", 256, 1024, 256, 1024),
    ("ffn2_w_gate", "row", FFN_SHARD, 1024, FFN_SHARD_PAD, 1024),
    ("ffn2_w_up", "row", FFN_SHARD, 1024, FFN_SHARD_PAD, 1024),
    ("ffn2_w_down", "row", FFN_SHARD, 1024, FFN_SHARD_PAD, 1024),
    ("ple_w_proj", "col", 256, 256, 256, 256),
    ("ple_w_gate", "row", 256, 1024, 256, 1024),
)
SMALL = ("ffn1_pre_g", "ffn1_post_g", "mix_pre_g", "pool_w", "pool_scale", "sgu_ln_g", "sgu_ln_b", "sgu_w_s",
         "sgu_b_s", "conv_dw_b", "conv_ln_g", "conv_ln_b", "mix_post_g", "ffn2_pre_g", "ffn2_post_g",
         "ple_pre_g", "ple_post_g")
WEIGHTS = ("ffn1_pre_g", "ffn1_w_gate", "ffn1_w_up", "ffn1_w_down", "ffn1_post_g", "mix_pre_g", "w_in", "pool_w",
           "pool_scale", "w_pool_out", "sgu_ln_g", "sgu_ln_b", "sgu_w_s", "sgu_b_s", "w_sgu_out", "conv_dw_k",
           "conv_dw_b", "conv_ln_g", "conv_ln_b", "w_conv_out", "w_out", "mix_post_g", "ffn2_pre_g", "ffn2_w_gate",
           "ffn2_w_up", "ffn2_w_down", "ffn2_post_g", "ple_w_proj", "ple_pre_g", "ple_w_gate", "ple_post_g")


def _params(n_grid):
    return pltpu.CompilerParams(dimension_semantics=("arbitrary",) * n_grid, vmem_limit_bytes=VMEM_LIMIT_BYTES)


def _tile(n, cap=ROW_TILE):
    for t in range(min(cap, n) - min(cap, n) % 16, 0, -16):
        if n % t == 0:
            return t
    return n


def _sigmoid(x):
    return 1.0 / (1.0 + jnp.exp(-x))


def _silu_and_grad(x):
    s = _sigmoid(x)
    return x * s, s * (1.0 + x * (1.0 - s))


def _gelu_and_grad(x):
    cdf = 0.5 * (1.0 + lax.erf(x * (1.0 / math.sqrt(2.0))))
    pdf = jnp.exp(-0.5 * x * x) * (1.0 / math.sqrt(2.0 * math.pi))
    return x * cdf, cdf + x * pdf


def _rms_fwd(x, g):
    return x * lax.rsqrt(jnp.mean(x * x, axis=-1, keepdims=True) + EPS) * g


def _rms_bwd(x, g, dy):
    r = lax.rsqrt(jnp.mean(x * x, axis=-1, keepdims=True) + EPS)
    xh = x * r
    dxh = dy * g
    dx = r * (dxh - xh * jnp.mean(dxh * xh, axis=-1, keepdims=True))
    return dx, jnp.sum(dy * xh, axis=0, keepdims=True)


def _ln_stats(x):
    xc = x - jnp.mean(x, axis=-1, keepdims=True)
    r = lax.rsqrt(jnp.mean(xc * xc, axis=-1, keepdims=True) + EPS)
    return xc * r, r


def _ln_bwd(xh, r, g, dy):
    dxh = dy * g
    dx = r * (dxh - jnp.mean(dxh, axis=-1, keepdims=True) - xh * jnp.mean(dxh * xh, axis=-1, keepdims=True))
    return dx, jnp.sum(dy * xh, axis=0, keepdims=True), jnp.sum(dy, axis=0, keepdims=True)


def _rowwise(name, fn, rows, consts, outs, accs=(), tm=ROW_TILE, deps=()):
    T = rows[0][0].shape[-2]
    tm = _tile(T, tm)
    n_in, n_o, n_dep = len(rows) + len(consts), len(outs), len(deps)

    def body(*refs):
        refs = refs[n_dep:]
        res = fn(*[r[...] for r in refs[:n_in]])
        for ref, val in zip(refs[n_in:n_in + n_o], res[:n_o]):
            ref[...] = val.astype(ref.dtype)
        acc_refs = refs[n_in + n_o:]
        if acc_refs:
            @pl.when(pl.program_id(0) == 0)
            def _():
                for ref, val in zip(acc_refs, res[n_o:]):
                    ref[...] = val

            @pl.when(pl.program_id(0) != 0)
            def _():
                for ref, val in zip(acc_refs, res[n_o:]):
                    ref[...] += val

    in_specs = [ANY] * n_dep
    for row in rows:
        w, cb = row[1], row[2]
        if len(row) == 4:
            in_specs.append(pl.BlockSpec((None, tm, w), lambda i, cb=cb, ld=row[3]: (ld, i, cb)))
        else:
            in_specs.append(pl.BlockSpec((tm, w), lambda i, cb=cb: (i, cb)))
    in_specs += [pl.BlockSpec(c.shape, lambda i: (0, 0)) for c in consts]
    out_specs = [pl.BlockSpec((tm, w), lambda i: (i, 0)) for w, _ in outs]
    out_specs += [pl.BlockSpec(s, lambda i: (0, 0)) for s in accs]
    out_shape = [jax.ShapeDtypeStruct((T, w), dt) for w, dt in outs]
    out_shape += [jax.ShapeDtypeStruct(s, F32) for s in accs]
    return pl.pallas_call(body, grid=(T // tm,), in_specs=in_specs, out_specs=out_specs, out_shape=out_shape,
                          name=name, compiler_params=_params(1))(*deps, *[r[0] for r in rows], *consts)


def _tiled(name, fn, grid, pos, ins, outs, into=None):
    n_in = len(ins)
    extra = [] if into is None else [into]

    def body(_, *refs):
        res = fn(*[r[...] for r in refs[:n_in]])
        for ref, val in zip(refs[n_in + len(extra):], res):
            ref[...] = val.astype(ref.dtype)

    spec = pltpu.PrefetchScalarGridSpec(
        num_scalar_prefetch=1, grid=grid, in_specs=[pl.BlockSpec(bs, im) for _, bs, im in ins] + [ANY] * len(extra),
        out_specs=[pl.BlockSpec(bs, im) for _, _, bs, im in outs])
    return pl.pallas_call(body, grid_spec=spec, out_shape=[jax.ShapeDtypeStruct(s, d) for s, d, _, _ in outs],
                          input_output_aliases={1 + n_in: 0} if extra else {}, name=name,
                          compiler_params=_params(len(grid)))(pos, *[a for a, _, _ in ins], *extra)


def _cast_layers(name, w, rp, cp, dtype, pos):
    L, r, c = w.shape

    def body(_, w_ref, *o_refs):
        for k, o_ref in enumerate(o_refs):
            @pl.when(pl.program_id(0) == k)
            def _(o_ref=o_ref):
                if (rp, cp) != (r, c):
                    o_ref[...] = jnp.zeros_like(o_ref)
                    o_ref[pl.ds(0, r), pl.ds(0, c)] = w_ref[...].astype(dtype)
                else:
                    o_ref[...] = w_ref[...].astype(dtype)

    spec = pltpu.PrefetchScalarGridSpec(
        num_scalar_prefetch=1, grid=(L,), in_specs=[pl.BlockSpec((None, r, c), lambda l, p: (l, 0, 0))],
        out_specs=[pl.BlockSpec((None, rp, cp), lambda l, p: (p[0], 0, 0))] * L)
    return pl.pallas_call(body, grid_spec=spec, out_shape=[jax.ShapeDtypeStruct((N_CHIPS, rp, cp), dtype)] * L,
                          name=name, compiler_params=_params(1))(pos, w)


_NN = (((1,), (0,)), ((), ()))
_NT = (((1,), (1,)), ((), ()))
_TN = (((0,), (0,)), ((), ()))


def _mm_blockout(name, x, w4, l, trans_w, tm=ROW_TILE):
    M, kx = x.shape
    w3 = w4[l]
    nb, r, cc = w3.shape
    bo = r if trans_w else cc
    tm = _tile(M, tm)

    def body(x_ref, w_ref, o_ref):
        o_ref[...] = lax.dot_general(x_ref[...].astype(BF), w_ref[...].astype(BF), _NT if trans_w else _NN,
                                     preferred_element_type=F32).astype(o_ref.dtype)

    return pl.pallas_call(
        body, grid=(nb, M // tm),
        in_specs=[pl.BlockSpec((tm, kx), lambda b, i: (i, 0)),
                  pl.BlockSpec((None, r, cc), lambda b, i: (b, 0, 0))],
        out_specs=pl.BlockSpec((tm, bo), lambda b, i: (i, b)),
        out_shape=jax.ShapeDtypeStruct((M, nb * bo), BF), name=name, compiler_params=_params(2))(x, w3)


def _mm_blocksum(name, pairs, l, trans_w, tm=2 * ROW_TILE):
    pairs = [(x, w4[l]) for x, w4 in pairs]
    x0, w0 = pairs[0]
    M = x0.shape[0]
    nb, r, cc = w0.shape
    bw, n_out = (cc, r) if trans_w else (r, cc)
    tm = _tile(M, tm)
    n_p = len(pairs)

    def body(*refs):
        o_ref, acc = refs[2 * n_p], refs[2 * n_p + 1]
        b = pl.program_id(1)

        @pl.when(b == 0)
        def _():
            acc[...] = jnp.zeros_like(acc)

        t = acc[...]
        for x_ref, w_ref in zip(refs[:n_p], refs[n_p:2 * n_p]):
            t = t + lax.dot_general(x_ref[...].astype(BF), w_ref[...].astype(BF), _NT if trans_w else _NN,
                                    preferred_element_type=F32)
        acc[...] = t

        @pl.when(b == nb - 1)
        def _():
            o_ref[...] = acc[...].astype(o_ref.dtype)

    in_specs = [pl.BlockSpec((tm, bw), lambda i, b: (i, b)) for _ in pairs]
    in_specs += [pl.BlockSpec((None, r, cc), lambda i, b: (b, 0, 0)) for _ in pairs]
    return pl.pallas_call(
        body, grid=(M // tm, nb), in_specs=in_specs, out_specs=pl.BlockSpec((tm, n_out), lambda i, b: (i, 0)),
        out_shape=jax.ShapeDtypeStruct((M, n_out), BF), scratch_shapes=[pltpu.VMEM((tm, n_out), F32)],
        name=name, compiler_params=_params(2))(*[p[0] for p in pairs], *[p[1] for p in pairs])


def _mm_tn(name, a, dy, buf, l, a_blocked, tk=ROW_TILE):
    T = a.shape[0]
    nb, R, C = buf[l].shape

    def body(a_ref, dy_ref, o_ref):
        o_ref[...] = lax.dot_general(a_ref[...].astype(BF), dy_ref[...].astype(BF), _TN,
                                     preferred_element_type=F32).astype(o_ref.dtype)

    if a_blocked:
        grid = (nb,)
        in_specs = [pl.BlockSpec((T, R), lambda b: (0, b)), pl.BlockSpec((T, C), lambda b: (0, 0))]
        out_specs = pl.BlockSpec((None, R, C), lambda b: (b, 0, 0))
    else:
        tk = min(tk, R)
        grid = (nb, R // tk)
        in_specs = [pl.BlockSpec((T, tk), lambda b, k: (0, k)), pl.BlockSpec((T, C), lambda b, k: (0, b))]
        out_specs = pl.BlockSpec((None, tk, C), lambda b, k: (b, k, 0))
    buf = list(buf)
    buf[l] = pl.pallas_call(body, grid=grid, in_specs=in_specs, out_specs=out_specs,
                            out_shape=jax.ShapeDtypeStruct((nb, R, C), BF), name=name,
                            compiler_params=_params(len(grid)))(a, dy)
    return buf


def _pool_apply(x, win, row):
    s, k = x, 1
    while k < win:
        s = s + jnp.where(row >= k, pltpu.roll(s, k, 0), 0.0)
        k *= 2
    return s / jnp.minimum(row + 1, win).astype(F32) - x


def _pool_apply_t(dp, win, row):
    T = dp.shape[0]
    s, k = dp / jnp.minimum(row + 1, win).astype(F32), 1
    while k < win:
        s = s + jnp.where(row < T - k, pltpu.roll(s, T - k, 0), 0.0)
        k *= 2
    return s - dp


def _pool_fwd(z, w, scale):
    T = z.shape[0]

    def body(z_ref, w_ref, s_ref, o_ref):
        row = lax.broadcasted_iota(jnp.int32, (T, LANES), 0)
        for gi, win in enumerate(POOL_WINDOWS):
            cols = pl.ds(gi * LANES, LANES)
            pooled = _pool_apply(z_ref[:, cols].astype(F32), win, row)
            y = jnp.dot(pooled.astype(BF), w_ref[gi].astype(BF), preferred_element_type=F32)
            o_ref[:, cols] = (y * s_ref[:, cols]).astype(o_ref.dtype)

    return pl.pallas_call(
        body, grid=(1,),
        in_specs=[pl.BlockSpec((T, 512), lambda i: (0, ZB_POOL)), pl.BlockSpec(w.shape, lambda i: (0, 0, 0)),
                  pl.BlockSpec(scale.shape, lambda i: (0, 0))],
        out_specs=pl.BlockSpec((T, 512), lambda i: (0, 0)), out_shape=jax.ShapeDtypeStruct((T, 512), BF),
        name="pool_fwd", compiler_params=_params(1))(z, w, scale)


def _pool_bwd(dr, z, w, scale):
    T = z.shape[0]

    def body(dr_ref, z_ref, w_ref, s_ref, dz_ref, dw_ref, ds_ref):
        row = lax.broadcasted_iota(jnp.int32, (T, LANES), 0)
        for gi, win in enumerate(POOL_WINDOWS):
            cols = pl.ds(gi * LANES, LANES)
            pooled = _pool_apply(z_ref[:, cols].astype(F32), win, row).astype(BF)
            wg = w_ref[gi].astype(BF)
            y = jnp.dot(pooled, wg, preferred_element_type=F32)
            d = dr_ref[:, cols].astype(F32)
            ds_ref[:, cols] = jnp.sum(d * y, axis=0, keepdims=True)
            dy = (d * s_ref[:, cols]).astype(BF)
            dw_ref[gi] = lax.dot_general(pooled, dy, _TN, preferred_element_type=F32)
            dpooled = lax.dot_general(dy, wg, _NT, preferred_element_type=F32)
            dz_ref[:, cols] = _pool_apply_t(dpooled, win, row).astype(dz_ref.dtype)

    return pl.pallas_call(
        body, grid=(1,),
        in_specs=[pl.BlockSpec((T, 512), lambda i: (0, 0)), pl.BlockSpec((T, 512), lambda i: (0, ZB_POOL)),
                  pl.BlockSpec(w.shape, lambda i: (0, 0, 0)), pl.BlockSpec(scale.shape, lambda i: (0, 0))],
        out_specs=[pl.BlockSpec((T, 512), lambda i: (0, 0)), pl.BlockSpec(w.shape, lambda i: (0, 0, 0)),
                   pl.BlockSpec(scale.shape, lambda i: (0, 0))],
        out_shape=[jax.ShapeDtypeStruct((T, 512), BF), jax.ShapeDtypeStruct(w.shape, F32),
                   jax.ShapeDtypeStruct(scale.shape, F32)],
        name="pool_bwd", compiler_params=_params(1))(dr, z, w, scale)


def _tril(transposed=False):
    r = lax.broadcasted_iota(jnp.int32, (CHUNK, CHUNK), 0)
    c = lax.broadcasted_iota(jnp.int32, (CHUNK, CHUNK), 1)
    return c >= r if transposed else r >= c


def _sgu_fwd(z, ln_g, ln_b, w_s, bias):
    T = z.shape[0]
    tm = _tile(T)

    def body(zu_ref, zv_ref, g_ref, b_ref, w_ref, bias_ref, o_ref):
        gu, _ = _gelu_and_grad(zu_ref[...].astype(F32))
        gv, _ = _gelu_and_grad(zv_ref[...].astype(F32))
        xh, _ = _ln_stats(gv)
        v16 = (xh * g_ref[...] + b_ref[...]).astype(BF)
        tri = _tril()
        for h in range(SGU_HEADS):
            cols = slice(h * LANES, (h + 1) * LANES)
            wh = jnp.where(tri, w_ref[h], 0.0).astype(BF)
            for c in range(tm // CHUNK):
                rows = slice(c * CHUNK, (c + 1) * CHUNK)
                s = jnp.dot(wh, v16[rows, cols], preferred_element_type=F32) + bias_ref[:, cols]
                o_ref[rows, cols] = (gu[rows, cols] * s).astype(o_ref.dtype)

    small = [pl.BlockSpec(a.shape, lambda i, n=a.ndim: (0,) * n) for a in (ln_g, ln_b, w_s, bias)]
    return pl.pallas_call(
        body, grid=(T // tm,),
        in_specs=[pl.BlockSpec((tm, 512), lambda i: (i, ZB_U)), pl.BlockSpec((tm, 512), lambda i: (i, ZB_V))] + small,
        out_specs=pl.BlockSpec((tm, 512), lambda i: (i, 0)), out_shape=jax.ShapeDtypeStruct((T, 512), BF),
        name="sgu_fwd", compiler_params=_params(1))(z, z, ln_g, ln_b, w_s, bias)


def _sgu_bwd(dr, z, ln_g, ln_b, w_s, w_st, bias):
    T = z.shape[0]
    tm = _tile(T)
    n_steps = T // tm

    def body(dr_ref, zu_ref, zv_ref, g_ref, b_ref, w_ref, wt_ref, bias_ref,
             dzu_ref, dzv_ref, dg_ref, db_ref, dw_ref, dbias_ref, dgu_s, dv_s):
        i = pl.program_id(0)

        @pl.when(i == 0)
        def _():
            dg_ref[...] = jnp.zeros_like(dg_ref)
            db_ref[...] = jnp.zeros_like(db_ref)
            dw_ref[...] = jnp.zeros_like(dw_ref)
            dbias_ref[...] = jnp.zeros_like(dbias_ref)

        zu = zu_ref[...].astype(F32)
        zv = zv_ref[...].astype(F32)
        gu, gu_grad = _gelu_and_grad(zu)
        gv, gv_grad = _gelu_and_grad(zv)
        xh, r = _ln_stats(gv)
        v16 = (xh * g_ref[...] + b_ref[...]).astype(BF)
        dr = dr_ref[...].astype(F32)
        tri = _tril()
        for h in range(SGU_HEADS):
            cols = slice(h * LANES, (h + 1) * LANES)
            wh = jnp.where(tri, w_ref[h], 0.0).astype(BF)
            wht = jnp.where(_tril(transposed=True), wt_ref[h], 0.0).astype(BF)
            for c in range(tm // CHUNK):
                rows = slice(c * CHUNK, (c + 1) * CHUNK)
                v_blk = v16[rows, cols]
                s = jnp.dot(wh, v_blk, preferred_element_type=F32) + bias_ref[:, cols]
                ds = dr[rows, cols] * gu[rows, cols]
                dgu_s[rows, cols] = dr[rows, cols] * s
                ds16 = ds.astype(BF)
                dw_ref[h] += jnp.where(tri, lax.dot_general(ds16, v_blk, _NT, preferred_element_type=F32), 0.0)
                dv_s[rows, cols] = jnp.dot(wht, ds16, preferred_element_type=F32)
                dbias_ref[:, cols] += ds
        dzu_ref[...] = (dgu_s[...] * gu_grad).astype(dzu_ref.dtype)
        dgv, dg, db = _ln_bwd(xh, r, g_ref[...], dv_s[...])
        dzv_ref[...] = (dgv * gv_grad).astype(dzv_ref.dtype)
        dg_ref[...] += dg
        db_ref[...] += db

        @pl.when(i == n_steps - 1)
        def _():
            for h in range(SGU_HEADS):
                cols = slice(h * LANES, (h + 1) * LANES)
                tot = jnp.sum(dbias_ref[:, cols], axis=1, keepdims=True)
                dbias_ref[:, cols] = jnp.broadcast_to(tot, (CHUNK, LANES))

    small = (ln_g, ln_b, w_s, w_st, bias)
    small_specs = [pl.BlockSpec(a.shape, lambda i, n=a.ndim: (0,) * n) for a in small]
    return pl.pallas_call(
        body, grid=(n_steps,),
        in_specs=[pl.BlockSpec((tm, 512), lambda i: (i, 0)), pl.BlockSpec((tm, 512), lambda i: (i, ZB_U)),
                  pl.BlockSpec((tm, 512), lambda i: (i, ZB_V))] + small_specs,
        out_specs=[pl.BlockSpec((tm, 512), lambda i: (i, 0)), pl.BlockSpec((tm, 512), lambda i: (i, 0)),
                   pl.BlockSpec((1, 512), lambda i: (0, 0)), pl.BlockSpec((1, 512), lambda i: (0, 0)),
                   pl.BlockSpec(w_s.shape, lambda i: (0, 0, 0)), pl.BlockSpec(bias.shape, lambda i: (0, 0))],
        out_shape=[jax.ShapeDtypeStruct((T, 512), BF), jax.ShapeDtypeStruct((T, 512), BF),
                   jax.ShapeDtypeStruct((1, 512), F32), jax.ShapeDtypeStruct((1, 512), F32),
                   jax.ShapeDtypeStruct(w_s.shape, F32), jax.ShapeDtypeStruct(bias.shape, F32)],
        scratch_shapes=[pltpu.VMEM((tm, 512), F32), pltpu.VMEM((tm, 512), F32)],
        name="sgu_bwd", compiler_params=_params(1))(dr, z, z, ln_g, ln_b, w_s, w_st, bias)


def _conv_fwd(z, convk, l, bias):
    T = z.shape[0]

    def body(za_ref, zb_ref, k_ref, b_ref, o_ref):
        xg = za_ref[...].astype(F32) * _sigmoid(zb_ref[...].astype(F32))
        xp = jnp.concatenate([jnp.zeros((CONV_PAD, LANES), F32), xg], axis=0)
        kw = k_ref[...]
        acc = jnp.broadcast_to(b_ref[...], (T, LANES))
        for k in range(CONV_TAPS):
            sh = CONV_TAPS - 1 - k
            tap = xp if sh == 0 else pltpu.roll(xp, sh, 0)
            acc = acc + kw[k:k + 1, :] * tap[CONV_PAD:, :]
        o_ref[...] = acc.astype(o_ref.dtype)

    return pl.pallas_call(
        body, grid=(4,),
        in_specs=[pl.BlockSpec((T, LANES), lambda g: (0, 4 * ZB_A + g)),
                  pl.BlockSpec((T, LANES), lambda g: (0, 4 * ZB_B + g)),
                  pl.BlockSpec((None, CONV_PAD, LANES), lambda g: (g, 0, 0)),
                  pl.BlockSpec((1, LANES), lambda g: (0, g))],
        out_specs=pl.BlockSpec((T, LANES), lambda g: (0, g)), out_shape=jax.ShapeDtypeStruct((T, 512), BF),
        name="conv_fwd", compiler_params=_params(1))(z, z, convk[l], bias)


def _conv_bwd(dy, z, convk, l):
    T = z.shape[0]

    def body(dy_ref, za_ref, zb_ref, k_ref, dza_ref, dzb_ref, dk_ref, db_ref):
        a = za_ref[...].astype(F32)
        sg = _sigmoid(zb_ref[...].astype(F32))
        d = dy_ref[...].astype(F32)
        kw = k_ref[...]
        xp = jnp.concatenate([jnp.zeros((CONV_PAD, LANES), F32), a * sg], axis=0)
        dp = jnp.concatenate([d, jnp.zeros((CONV_PAD, LANES), F32)], axis=0)
        dxg = jnp.zeros((T, LANES), F32)
        dk_ref[...] = jnp.zeros_like(dk_ref)
        for k in range(CONV_TAPS):
            sh = CONV_TAPS - 1 - k
            tap = xp if sh == 0 else pltpu.roll(xp, sh, 0)
            dk_ref[k:k + 1, :] = jnp.sum(d * tap[CONV_PAD:, :], axis=0, keepdims=True)
            dtap = dp if sh == 0 else pltpu.roll(dp, T + CONV_PAD - sh, 0)
            dxg = dxg + kw[k:k + 1, :] * dtap[:T, :]
        db_ref[...] = jnp.sum(d, axis=0, keepdims=True)
        dza_ref[...] = (dxg * sg).astype(dza_ref.dtype)
        dzb_ref[...] = (dxg * a * sg * (1.0 - sg)).astype(dzb_ref.dtype)

    col = pl.BlockSpec((T, LANES), lambda g: (0, g))
    return pl.pallas_call(
        body, grid=(4,),
        in_specs=[col, pl.BlockSpec((T, LANES), lambda g: (0, 4 * ZB_A + g)),
                  pl.BlockSpec((T, LANES), lambda g: (0, 4 * ZB_B + g)),
                  pl.BlockSpec((None, CONV_PAD, LANES), lambda g: (g, 0, 0))],
        out_specs=[col, col, pl.BlockSpec((CONV_PAD, LANES), lambda g: (0, g)),
                   pl.BlockSpec((1, LANES), lambda g: (0, g))],
        out_shape=[jax.ShapeDtypeStruct((T, 512), BF), jax.ShapeDtypeStruct((T, 512), BF),
                   jax.ShapeDtypeStruct((CONV_PAD, 512), F32), jax.ShapeDtypeStruct((1, 512), F32)],
        name="conv_bwd", compiler_params=_params(1))(dy, z, z, convk[l])


D = D_MODEL


def _rms_call(name, h, g, deps=()):
    return _rowwise(name, lambda x, gg: (_rms_fwd(x.astype(F32), gg),), [(h, D, 0)], [g], [(D, BF)], deps=deps)[0]


def _res_norm_call(name, h, f, g, coef):
    return _rowwise(name, lambda x, ff, gg: (x + coef * _rms_fwd(ff.astype(F32), gg),),
                    [(h, D, 0), (f, D, 0)], [g], [(D, F32)])[0]


def _res_norm_bwd_call(name, dh, f, g, coef, deps=()):
    return _rowwise(name, lambda d, ff, gg: _rms_bwd(ff.astype(F32), gg, coef * d),
                    [(dh, D, 0), (f, D, 0)], [g], [(D, BF)], [(1, D)], deps=deps)


def _pre_norm_bwd_call(name, dh, h, dn, g):
    def fn(d, x, dnn, gg):
        dx, dg = _rms_bwd(x, gg, dnn.astype(F32))
        return d + dx, dg

    return _rowwise(name, fn, [(dh, D, 0), (h, D, 0), (dn, D, 0)], [g], [(D, F32)], [(1, D)])


def _ffn_fwd(l, h, S, W, pre, deps=()):
    n = _rms_call("ffn_norm", h, S[pre + "_pre_g"], deps)
    gp = _mm_blockout("ffn_gate", n, W[pre + "_w_gate"], l, True)
    u = _mm_blockout("ffn_up", n, W[pre + "_w_up"], l, True)
    fw = gp.shape[1]
    a = _rowwise("ffn_act", lambda g, uu: (_silu_and_grad(g.astype(F32))[0] * uu.astype(F32),),
                 [(gp, fw, 0), (u, fw, 0)], [], [(fw, BF)])[0]
    f = _mm_blocksum("ffn_down", [(a, W[pre + "_w_down"])], l, False)
    out = _res_norm_call("ffn_res", h, f, S[pre + "_post_g"], 0.5)
    return out, dict(h=h, n=n, gp=gp, u=u, a=a, f=f)


def _ffn_bwd(l, dh, sv, S, W, G, SG, pre):
    df, SG[pre + "_post_g"] = _res_norm_bwd_call("ffn_res_bwd", dh, sv["f"], S[pre + "_post_g"], 0.5)
    da = _mm_blockout("ffn_dact", df, W[pre + "_w_down"], l, True)
    fw = da.shape[1]

    def act_bwd(d, g, uu):
        act, grad = _silu_and_grad(g.astype(F32))
        d = d.astype(F32)
        return d * uu.astype(F32) * grad, d * act

    dgp, du = _rowwise("ffn_act_bwd", act_bwd, [(da, fw, 0), (sv["gp"], fw, 0), (sv["u"], fw, 0)], [],
                       [(fw, BF), (fw, BF)])
    G[pre + "_w_down"] = _mm_tn("ffn_dw_down", sv["a"], df, G[pre + "_w_down"], l, True)
    G[pre + "_w_gate"] = _mm_tn("ffn_dw_gate", dgp, sv["n"], G[pre + "_w_gate"], l, True)
    G[pre + "_w_up"] = _mm_tn("ffn_dw_up", du, sv["n"], G[pre + "_w_up"], l, True)
    dn = _mm_blocksum("ffn_dn", [(dgp, W[pre + "_w_gate"]), (du, W[pre + "_w_up"])], l, False)
    dh_in, SG[pre + "_pre_g"] = _pre_norm_bwd_call("pre_norm_bwd", dh, sv["h"], dn, S[pre + "_pre_g"])
    return dh_in


def _gate_rows(z):
    return [(z, 512, ZB_GATES + j) for j in range(6)]


def _gates(zg):
    return [_sigmoid(jnp.concatenate([zg[2 * k].astype(F32), zg[2 * k + 1].astype(F32)], axis=1)) for k in range(3)]


def _mix_fwd(l, h, S, W):
    n = _rms_call("mix_norm", h, S["mix_pre_g"])
    z = _mm_blockout("mix_in", n, W["w_in"], l, False)
    r_pool = _pool_fwd(z, S["pool_w"], S["pool_scale"])
    r_sgu = _sgu_fwd(z, S["sgu_ln_g"], S["sgu_ln_b"], S["sgu_w_s"], S["sgu_bias"])
    yc = _conv_fwd(z, W["conv_dw_k"], l, S["conv_dw_b"])

    def ln_silu(y, g, b):
        xh, _ = _ln_stats(y.astype(F32))
        return (_silu_and_grad(xh * g + b)[0],)

    r_conv = _rowwise("conv_ln", ln_silu, [(yc, 512, 0)], [S["conv_ln_g"], S["conv_ln_b"]], [(512, BF)])[0]
    y_pool = _mm_blockout("branch_out", r_pool, W["w_pool_out"], l, False)
    y_sgu = _mm_blockout("branch_out", r_sgu, W["w_sgu_out"], l, False)
    y_conv = _mm_blockout("branch_out", r_conv, W["w_conv_out"], l, False)

    def merge(*t):
        g = _gates(t[:6])
        return (g[0] * t[6].astype(F32) + g[1] * t[7].astype(F32) + g[2] * t[8].astype(F32),)

    merged = _rowwise("mix_merge", merge, _gate_rows(z) + [(y_pool, D, 0), (y_sgu, D, 0), (y_conv, D, 0)], [],
                      [(D, BF)])[0]
    o = _mm_blocksum("mix_out", [(merged, W["w_out"])], l, False)
    out = _res_norm_call("mix_res", h, o, S["mix_post_g"], 1.0)
    return out, dict(h=h, n=n, z=z, r_pool=r_pool, r_sgu=r_sgu, yc=yc, r_conv=r_conv, y_pool=y_pool, y_sgu=y_sgu,
                     y_conv=y_conv, merged=merged, o=o)


def _mix_bwd(l, dh, sv, S, W, G, SG, deps=()):
    z = sv["z"]
    do, SG["mix_post_g"] = _res_norm_bwd_call("mix_res_bwd", dh, sv["o"], S["mix_post_g"], 1.0, deps)
    dmerged = _mm_blockout("mix_dmerged", do, W["w_out"], l, True)
    G["w_out"] = _mm_tn("mix_dw_out", sv["merged"], do, G["w_out"], l, True)

    def merge_bwd(dm, *t):
        g = _gates(t[:6])
        dm = dm.astype(F32)
        dzg = [dm * t[6 + k].astype(F32) * g[k] * (1.0 - g[k]) for k in range(3)]
        return dm * g[0], dm * g[1], dm * g[2], jnp.concatenate(dzg, axis=1)

    rows = [(dmerged, D, 0)] + _gate_rows(z) + [(sv["y_pool"], D, 0), (sv["y_sgu"], D, 0), (sv["y_conv"], D, 0)]
    dy_pool, dy_sgu, dy_conv, dzg = _rowwise("mix_merge_bwd", merge_bwd, rows, [], [(D, BF)] * 3 + [(3 * D, BF)],
                                             tm=ROW_TILE // 2)
    dr = {}
    for br, dy in (("pool", dy_pool), ("sgu", dy_sgu), ("conv", dy_conv)):
        wn = "w_%s_out" % br
        G[wn] = _mm_tn("branch_dw", sv["r_" + br], dy, G[wn], l, False)
        dr[br] = _mm_blocksum("branch_dr", [(dy, W[wn])], l, True)
    dz_pool, SG["pool_w"], SG["pool_scale"] = _pool_bwd(dr["pool"], z, S["pool_w"], S["pool_scale"])
    dzu, dzv, SG["sgu_ln_g"], SG["sgu_ln_b"], SG["sgu_w_s"], dbias = _sgu_bwd(
        dr["sgu"], z, S["sgu_ln_g"], S["sgu_ln_b"], S["sgu_w_s"], S["sgu_w_st"], S["sgu_bias"])
    SG["sgu_b_s"] = dbias[:, ::LANES].T

    def ln_silu_bwd(d, y, g, b):
        xh, r = _ln_stats(y.astype(F32))
        _, grad = _silu_and_grad(xh * g + b)
        return _ln_bwd(xh, r, g, d.astype(F32) * grad)

    dyc, SG["conv_ln_g"], SG["conv_ln_b"] = _rowwise(
        "conv_ln_bwd", ln_silu_bwd, [(dr["conv"], 512, 0), (sv["yc"], 512, 0)], [S["conv_ln_g"], S["conv_ln_b"]],
        [(512, BF)], [(1, 512), (1, 512)])
    dza, dzb, SG["conv_dw_k"], SG["conv_dw_b"] = _conv_bwd(dyc, z, W["conv_dw_k"], l)
    dz = jnp.concatenate([dz_pool, dzu, dzv, dza, dzb, dzg], axis=1)
    G["w_in"] = _mm_tn("mix_dw_in", sv["n"], dz, G["w_in"], l, False)
    dn = _mm_blocksum("mix_dn", [(dz, W["w_in"])], l, True)
    dh_in, SG["mix_pre_g"] = _pre_norm_bwd_call("pre_norm_bwd", dh, sv["h"], dn, S["mix_pre_g"])
    return dh_in


def _ple_fwd(l, h, p_l, S, W):
    n = _rms_call("ple_norm", h, S["ple_pre_g"])
    e = _mm_blockout("ple_proj", p_l, W["ple_w_proj"], l, False)
    gp = _mm_blocksum("ple_gate", [(n, W["ple_w_gate"])], l, False)

    def res(x, ee, g, gg):
        return (x + _rms_fwd(_sigmoid(g.astype(F32)) * ee.astype(F32), gg),)

    out = _rowwise("ple_res", res, [(h, D, 0), (e, D, 0), (gp, D, 0)], [S["ple_post_g"]], [(D, F32)])[0]
    return out, dict(h=h, n=n, e=e, gp=gp, p=p_l)


def _ple_bwd(l, dh, sv, S, W, G, SG, deps=()):
    def res_bwd(d, ee, g, gg):
        sg = _sigmoid(g.astype(F32))
        ee = ee.astype(F32)
        dq, dg = _rms_bwd(sg * ee, gg, d)
        return dq * sg, dq * ee * sg * (1.0 - sg), dg

    de, dgp, SG["ple_post_g"] = _rowwise("ple_res_bwd", res_bwd, [(dh, D, 0), (sv["e"], D, 0), (sv["gp"], D, 0)],
                                         [S["ple_post_g"]], [(D, BF), (D, BF)], [(1, D)], deps=deps)
    G["ple_w_proj"] = _mm_tn("ple_dw_proj", sv["p"], de, G["ple_w_proj"], l, False)
    G["ple_w_gate"] = _mm_tn("ple_dw_gate", sv["n"], dgp, G["ple_w_gate"], l, True)
    dn = _mm_blockout("ple_dn", dgp, W["ple_w_gate"], l, True)
    dh_in, SG["ple_pre_g"] = _pre_norm_bwd_call("pre_norm_bwd", dh, sv["h"], dn, S["ple_pre_g"])
    return dh_in


def _layer_small(a, l):
    S = {}
    for name in SMALL:
        v = a[name][l]
        S[name] = v.reshape(1, -1) if v.ndim == 1 else v
    S["sgu_w_st"] = jnp.swapaxes(S["sgu_w_s"], 1, 2)
    S["sgu_bias"] = jnp.repeat(S["sgu_b_s"].T, LANES, axis=1)
    return S


def _layer_fwd(l, h, p_l, S, W, deps=(), mid=None):
    h, sv1 = _ffn_fwd(l, h, S, W, "ffn1", deps)
    h, sv2 = _mix_fwd(l, h, S, W)
    h, sv3 = _ffn_fwd(l, h, S, W, "ffn2", mid(h) if mid else ())
    h, sv4 = _ple_fwd(l, h, p_l, S, W)
    return h, (sv1, sv2, sv3, sv4)


def _layer_bwd(l, dh, sv, S, W, G, deps=(), mid=None):
    SG = {}
    dh = _ple_bwd(l, dh, sv[3], S, W, G, SG, deps)
    dh = _ffn_bwd(l, dh, sv[2], S, W, G, SG, "ffn2")
    dh = _mix_bwd(l, dh, sv[1], S, W, G, SG, mid(dh) if mid else ())
    dh = _ffn_bwd(l, dh, sv[0], S, W, G, SG, "ffn1")
    return dh, SG


HBM = pl.BlockSpec(memory_space=pltpu.HBM)
SEM = pl.BlockSpec(memory_space=pltpu.SEMAPHORE)
SIDE_EFFECT = pltpu.SideEffectType.DATAFLOW_SIDE_EFFECTING


def _place():
    x, y, c = lax.axis_index("x"), lax.axis_index("y"), lax.axis_index("c")
    chips = [(1 - x, y), (x, 1 - y), (1 - x, 1 - y)]
    return x, y, c, chips


def _remote(src, dst, send_sem, recv_sem, to):
    return pltpu.make_async_remote_copy(src_ref=src, dst_ref=dst, send_sem=send_sem, recv_sem=recv_sem,
                                        device_id=to, device_id_type=MESH)


def _split_start(name, plan, bufs, deps):
    count, fn = plan
    n, nd = len(bufs), len(deps)

    def body(*refs):
        send, recv = refs[nd + n], refs[nd + n + 1]
        x, y, c, chips = _place()
        for k, (src, dst, _, to) in enumerate(fn(refs[nd:nd + n], x, y, c, chips)):
            _remote(src, dst, send.at[k], recv.at[k], to).start()
        refs[-1][...] = jnp.zeros_like(refs[-1])

    res = pl.pallas_call(
        body, in_specs=[ANY] * nd + [HBM] * n,
        out_specs=[SEM, SEM] + [HBM] * n + [pl.BlockSpec(memory_space=pltpu.VMEM)],
        out_shape=[pltpu.SemaphoreType.DMA((count,)), pltpu.SemaphoreType.DMA((count,))]
        + [pltpu.HBM(b.shape, b.dtype) for b in bufs] + [jax.ShapeDtypeStruct((8, LANES), F32)],
        input_output_aliases={nd + i: 2 + i for i in range(n)}, name=name,
        compiler_params=pltpu.CompilerParams(has_side_effects=SIDE_EFFECT),
    )(*deps, *[pltpu.with_memory_space_constraint(b, pltpu.HBM) for b in bufs])
    return (res[0], res[1]), list(res[2:2 + n]), res[-1]


def _split_wait(name, plan, sems, bufs, after):
    _, fn = plan
    n = len(bufs)

    def body(*refs):
        send, recv = refs[n], refs[n + 1]
        x, y, c, chips = _place()
        for k, (src, _, land, to) in enumerate(fn(refs[:n], x, y, c, chips)):
            cp = _remote(src, land, send.at[k], recv.at[k], to)
            cp.wait_send()
            cp.wait_recv()

    res = pl.pallas_call(
        body, in_specs=[HBM] * n + [SEM, SEM] + [ANY] * len(after), out_specs=[HBM] * n,
        out_shape=[pltpu.HBM(b.shape, b.dtype) for b in bufs], input_output_aliases={i: i for i in range(n)},
        name=name, compiler_params=pltpu.CompilerParams(has_side_effects=SIDE_EFFECT))(*bufs, *sems, *after)
    return list(res)


def _gather_plans(n):
    def across(b, x, y, c, chips):
        me, out = 2 * x + y, []
        for a in range(n):
            rh = b[a].shape[1] // 2
            mine = b[a].at[me, pl.ds(c * rh, rh)]
            for cx, cy in chips:
                out.append((mine, mine, b[a].at[2 * cx + cy, pl.ds(c * rh, rh)], (cx, cy, c)))
        return out

    def to_sibling(b, x, y, c, chips):
        out = []
        for a in range(n):
            rh = b[a].shape[1] // 2
            for cx, cy in chips:
                piece = b[a].at[2 * cx + cy, pl.ds(c * rh, rh)]
                out.append((piece, piece, b[a].at[2 * cx + cy, pl.ds((1 - c) * rh, rh)], (x, y, 1 - c)))
        return out

    return (3 * n, across), (3 * n, to_sibling)


def _pair_plan(n):
    def fn(b, x, y, c, chips):
        out = []
        for a in range(n):
            rh = b[a].shape[1] // 2
            out.append((b[a].at[:, pl.ds((1 - c) * rh, rh)], b[n + a], b[n + a], (x, y, 1 - c)))
        return out

    return n, fn


def _cross_plan(n):
    def fn(b, x, y, c, chips):
        out = []
        for a in range(n):
            for j, (cx, cy) in enumerate(chips):
                out.append((b[a].at[2 * cx + cy], b[n + a].at[j], b[n + a].at[j], (cx, cy, c)))
        return out

    return 3 * n, fn


def _share_plan(n, l):
    def fn(b, x, y, c, chips):
        out = []
        for a in range(n):
            rh = b[a].shape[1] // 2
            mine = b[a].at[l, pl.ds(c * rh, rh)]
            out.append((mine, mine, b[a].at[l, pl.ds((1 - c) * rh, rh)], (x, y, 1 - c)))
        return out

    return n, fn


def _exchange_call(name, body, ins, out_shapes, n_remote, in_place=False):
    scratch = [pltpu.SemaphoreType.DMA((n_remote,)), pltpu.SemaphoreType.DMA((n_remote,))]
    aliases = {i: i for i in range(len(ins))} if in_place else {}
    return pl.pallas_call(body, in_specs=[ANY] * len(ins), out_specs=[ANY] * len(out_shapes), out_shape=out_shapes,
                          scratch_shapes=scratch, input_output_aliases=aliases, name=name)(*ins)


def _peers(x, y, c):
    return [(1 - x if m & 4 else x, 1 - y if m & 2 else y, 1 - c if m & 1 else c) for m in range(1, 8)]


def _scatter_small(v3):
    def body(v_ref, o_ref, send, recv):
        x, y, c, _ = _place()
        cps = []
        for m, (px, py, pc) in enumerate(_peers(x, y, c)):
            cps.append(_remote(v_ref.at[4 * px + 2 * py + pc], o_ref.at[m], send.at[m], recv.at[m], (px, py, pc)))
            cps[-1].start()
        for cp in cps:
            cp.wait()

    return _exchange_call("scatter_small", body, [v3], [jax.ShapeDtypeStruct((7,) + v3.shape[1:], v3.dtype)], 7)[0]


def _gather_small(buf):
    def body(_, o_ref, send, recv):
        x, y, c, _ = _place()
        mine = o_ref.at[4 * x + 2 * y + c]
        peers = _peers(x, y, c)
        cps = []
        for m, to in enumerate(peers):
            cps.append(_remote(mine, mine, send.at[m], recv.at[m], to))
            cps[-1].start()
        for m, (px, py, pc) in enumerate(peers):
            slab = o_ref.at[4 * px + 2 * py + pc]
            _remote(slab, slab, send.at[m], recv.at[m], (px, py, pc)).wait_recv()
            cps[m].wait_send()

    return _exchange_call("gather_small", body, [buf], [jax.ShapeDtypeStruct(buf.shape, buf.dtype)], 7,
                          in_place=True)[0]


def _allreduce_small(v, pos):
    rows = v.shape[0]
    rs = rows // 8
    v3 = v.reshape(8, rs, LANES)
    got = _scatter_small(v3)
    tm = _tile(rs)
    ins = [(v3, (None, tm, LANES), lambda i, p: (p[2], i, 0))]
    ins += [(got, (None, tm, LANES), lambda i, p, m=m: (m, i, 0)) for m in range(7)]
    buf = _tiled("sum_small", lambda *t: (((((((t[0] + t[1]) + t[2]) + t[3]) + t[4]) + t[5]) + t[6]) + t[7],),
                 (rs // tm,), pos, ins, [((8, rs, LANES), F32, (None, tm, LANES), lambda i, p: (p[2], i, 0))])[0]
    return _gather_small(buf).reshape(rows, LANES)


def _add_pair(grads, got, pos):
    out = []
    for g, q in zip(grads, got):
        nb, R, C = g.shape
        rh = R // 2
        tm = _tile(rh)
        nh = rh // tm
        t = _tiled("rs_add_pair", lambda u, w: (u.astype(F32) + w.astype(F32),), (nb, nh), pos,
                   [(g.reshape(nb * R, C), (tm, C), lambda b, i, p: (b * 2 * nh + p[1] * nh + i, 0)),
                    (q.reshape(nb * rh, C), (tm, C), lambda b, i, p: (b * nh + i, 0))],
                   [((nb * rh, C), BF, (tm, C), lambda b, i, p: (b * nh + i, 0))])[0]
        out.append(t.reshape(nb, rh, C))
    return out


def _add_chips(parts, slots, reduced, l, pos):
    out = []
    for t, s, red in zip(parts, slots, reduced):
        nb, rh, C = t.shape
        L = red.shape[0]
        tm = _tile(rh)
        nh = rh // tm

        def add(own, s0, s1, s2):
            return (((own.astype(F32) + s0.astype(F32)) + s1.astype(F32)) + s2.astype(F32),)

        ins = [(t.reshape(nb * rh, C), (tm, C), lambda i, p: (p[0] * nh + i, 0))]
        ins += [(s, (None, tm, C), lambda i, p, j=j: (j, i, 0)) for j in range(3)]
        buf = _tiled("rs_add_chips", add, (nh,), pos, ins,
                     [((L * 2 * rh, C), F32, (tm, C), lambda i, p: (l * 2 * nh + p[1] * nh + i, 0))],
                     into=red.reshape(L * 2 * rh, C))[0]
        out.append(buf.reshape(L, 2 * rh, C))
    return out


def _adamw_math(w, g, m, v):
    m = ADAM_B1 * m + (1.0 - ADAM_B1) * g
    v = ADAM_B2 * v + (1.0 - ADAM_B2) * (g * g)
    m_hat = m / (1.0 - ADAM_B1 ** ADAM_STEP)
    v_hat = v / (1.0 - ADAM_B2 ** ADAM_STEP)
    return -ADAM_LR * (m_hat / (jnp.sqrt(v_hat) + ADAM_EPS) + ADAM_WD * w), m, v


def _adamw(w, g, m, v):
    L, R, C = w.shape
    tr = _tile(R, 256)

    def body(w_ref, g_ref, m_ref, v_ref, go_ref, d_ref, mo_ref, vo_ref):
        gv = g_ref[...]
        d, mn, vn = _adamw_math(w_ref[...], gv, m_ref[...], v_ref[...])
        go_ref[...] = gv
        d_ref[...] = d
        mo_ref[...] = mn
        vo_ref[...] = vn

    spec = pl.BlockSpec((None, tr, C), lambda l, i: (l, i, 0))
    out = jax.ShapeDtypeStruct(w.shape, F32)
    return pl.pallas_call(body, grid=(L, R // tr), in_specs=[spec] * 4, out_specs=[spec] * 4, out_shape=[out] * 4,
                          name="adamw", compiler_params=_params(2))(w, g, m, v)


def _pack(parts):
    flat = jnp.concatenate([q.reshape(-1, LANES) for q in parts], axis=0)
    return jnp.pad(flat, ((0, -flat.shape[0] % ROW_TILE), (0, 0)))


def _unpack(flat, like):
    out, r = [], 0
    for q in like:
        n = q.size // LANES
        out.append(flat[r:r + n].reshape(q.shape))
        r += n
    return out


def _train_step(a):
    a = dict(a)
    L = a["ffn1_pre_g"].shape[0]
    x, y, c, _ = _place()
    chip = 2 * x + y
    pos = jnp.stack([chip, c, 2 * chip + c]).astype(jnp.int32)
    for name in TRANSPOSED:
        for pre in ("", "m_", "v_"):
            a[pre + name] = jnp.swapaxes(a[pre + name], 1, 2)
    big = [b[0] for b in BIG]
    gathered = big + ["conv_dw_k"]
    n_w, n_g = len(gathered), len(big)

    own = [_cast_layers("cast_weight", a[name], rp, cp, BF, pos) for name, _, _, _, rp, cp in BIG]
    own.append(_cast_layers("pad_conv_taps", a["conv_dw_k"].reshape(L, CONV_TAPS, LANES), CONV_PAD, LANES, F32, pos))
    across, to_sibling = _gather_plans(n_w)
    W = {name: [None] * L for name in gathered}

    def gather_first(l, deps):
        sems, bufs, token = _split_start("gather_a%d" % l, across, [own[i][l] for i in range(n_w)], deps)
        return sems, bufs, token

    def gather_second(l, state, after):
        bufs = _split_wait("gather_a%d_done" % l, across, state[0], state[1], after)
        return _split_start("gather_b%d" % l, to_sibling, bufs, [])

    def gather_done(l, state, after):
        for name, buf in zip(gathered, _split_wait("gather_b%d_done" % l, to_sibling, state[0], state[1], after)):
            W[name][l] = buf

    state = gather_first(0, [])
    state = gather_second(0, state, [])
    gather_done(0, state, [])

    small = [_layer_small(a, l) for l in range(L)]
    h, saved = a["x"][0], []
    for l in range(L):
        deps, mid, box = (), None, {}
        if l + 1 < L:
            box["state"] = gather_first(l + 1, [h])
            deps = (box["state"][2],)

            def mid(hm, l=l, box=box):
                box["state"] = gather_second(l + 1, box["state"], [hm])
                return (box["state"][2],)

        h, sv = _layer_fwd(l, h, a["p"][l, 0], small[l], W, deps, mid)
        saved.append(sv)
        if l + 1 < L:
            gather_done(l + 1, box["state"], [h])

    def loss_fn(yv, t):
        e = yv - t
        return e * (1.0 / D), jnp.sum(e * e, axis=0, keepdims=True)

    dh, lsum = _rowwise("loss", loss_fn, [(h, D, 0), (a["loss_target"][0], D, 0)], [], [(D, F32)], [(1, D)])
    loss = lax.psum(0.5 * jnp.sum(lsum) / D, ("x", "y", "c"))

    G = {name: [jax.ShapeDtypeStruct((N_CHIPS, rp, cp), BF)] * L for name, _, _, _, rp, cp in BIG}
    reduced = [lax.empty((L, rp, cp), F32) for _, _, _, _, rp, cp in BIG]
    pair, cross = _pair_plan(n_g), _cross_plan(n_g)
    small_grads = [None] * L

    def pair_start(l, deps):
        grads = [G[name][l] for name in big]
        lands = [lax.empty((N_CHIPS, g.shape[1] // 2, g.shape[2]), BF) for g in grads]
        return _split_start("rs_pair%d" % l, pair, grads + lands, deps)

    def cross_start(l, state, after):
        bufs = _split_wait("rs_pair%d_done" % l, pair, state[0], state[1], after)
        parts = _add_pair(bufs[:n_g], bufs[n_g:], pos)
        lands = [lax.empty((3,) + t.shape[1:], BF) for t in parts]
        return _split_start("rs_cross%d" % l, cross, parts + lands, [])

    def share_start(l, state, after, reduced):
        bufs = _split_wait("rs_cross%d_done" % l, cross, state[0], state[1], after)
        reduced = _add_chips(bufs[:n_g], bufs[n_g:], reduced, l, pos)
        return _split_start("rs_share%d" % l, _share_plan(n_g, l), reduced, [])

    def share_done(l, state, after):
        return _split_wait("rs_share%d_done" % l, _share_plan(n_g, l), state[0], state[1], after)

    st_pair = st_cross = st_share = None
    for l in reversed(range(L)):
        deps = tuple(s[2] for s in (st_pair, st_share) if s is not None)
        box = {"cross": None}

        def mid(dm, l=l, box=box, st_pair=st_pair, st_share=st_share):
            out = []
            if st_share is not None:
                box["reduced"] = share_done(l + 2, st_share, [dm])
            if st_pair is not None:
                box["cross"] = cross_start(l + 1, st_pair, [dm])
                out.append(box["cross"][2])
            return tuple(out)

        dh, small_grads[l] = _layer_bwd(l, dh, saved[l], small[l], W, G, deps, mid)
        if st_share is not None:
            reduced = box["reduced"]
        st_share = share_start(l + 1, box["cross"], [dh], reduced) if box["cross"] is not None else None
        st_pair = pair_start(l, [dh])
    grad_x = dh
    if st_share is not None:
        reduced = share_done(1, st_share, [])
    st_cross = cross_start(0, st_pair, [])
    st_share = share_start(0, st_cross, [], reduced)
    reduced = share_done(0, st_share, [])
    big_grads = dict(zip(big, reduced))

    small_names = SMALL + ("conv_dw_k",)
    stacked = [jnp.stack([small_grads[l][name] for l in range(L)]) for name in small_names]
    summed = dict(zip(small_names, _unpack(_allreduce_small(_pack(stacked), pos), stacked)))

    grads, deltas, new_m, new_v = {}, {}, {}, {}
    for name in big:
        res = _adamw(a[name], big_grads[name], a["m_" + name], a["v_" + name])
        if name in TRANSPOSED:
            res = [jnp.swapaxes(r, 1, 2) for r in res]
        grads[name], deltas[name], new_m[name], new_v[name] = res
    taps = lax.dynamic_slice_in_dim(summed["conv_dw_k"], chip * LANES, LANES, axis=2)[:, :CONV_TAPS]
    grads["conv_dw_k"] = taps.reshape(a["conv_dw_k"].shape)
    for name in SMALL:
        grads[name] = summed[name].reshape(a[name].shape)
    shapes = [a[name] for name in small_names]
    res = _adamw(*[_pack([a[pre + name] if pre != "g" else grads[name] for name in small_names])[None]
                   for pre in ("", "g", "m_", "v_")])
    for dst, flat in zip((deltas, new_m, new_v), res[1:]):
        for name, val in zip(small_names, _unpack(flat[0], shapes)):
            dst[name] = val

    return (loss, grad_x[None], *[grads[n] for n in WEIGHTS], *[deltas[n] for n in WEIGHTS],
            *[new_m[n] for n in WEIGHTS], *[new_v[n] for n in WEIGHTS])


def kernel(x, p, ffn1_pre_g, ffn1_w_gate, ffn1_w_up, ffn1_w_down, ffn1_post_g, mix_pre_g, w_in, pool_w, pool_scale, w_pool_out, sgu_ln_g, sgu_ln_b, sgu_w_s, sgu_b_s, w_sgu_out, conv_dw_k, conv_dw_b, conv_ln_g, conv_ln_b, w_conv_out, w_out, mix_post_g, ffn2_pre_g, ffn2_w_gate, ffn2_w_up, ffn2_w_down, ffn2_post_g, ple_w_proj, ple_pre_g, ple_w_gate, ple_post_g, loss_target, m_ffn1_pre_g, m_ffn1_w_gate, m_ffn1_w_up, m_ffn1_w_down, m_ffn1_post_g, m_mix_pre_g, m_w_in, m_pool_w, m_pool_scale, m_w_pool_out, m_sgu_ln_g, m_sgu_ln_b, m_sgu_w_s, m_sgu_b_s, m_w_sgu_out, m_conv_dw_k, m_conv_dw_b, m_conv_ln_g, m_conv_ln_b, m_w_conv_out, m_w_out, m_mix_post_g, m_ffn2_pre_g, m_ffn2_w_gate, m_ffn2_w_up, m_ffn2_w_down, m_ffn2_post_g, m_ple_w_proj, m_ple_pre_g, m_ple_w_gate, m_ple_post_g, v_ffn1_pre_g, v_ffn1_w_gate, v_ffn1_w_up, v_ffn1_w_down, v_ffn1_post_g, v_mix_pre_g, v_w_in, v_pool_w, v_pool_scale, v_w_pool_out, v_sgu_ln_g, v_sgu_ln_b, v_sgu_w_s, v_sgu_b_s, v_w_sgu_out, v_conv_dw_k, v_conv_dw_b, v_conv_ln_g, v_conv_ln_b, v_w_conv_out, v_w_out, v_mix_post_g, v_ffn2_pre_g, v_ffn2_w_gate, v_ffn2_w_up, v_ffn2_w_down, v_ffn2_post_g, v_ple_w_proj, v_ple_pre_g, v_ple_w_gate, v_ple_post_g):
    return _train_step(dict(locals()))
```

```python
import math

import jax
import jax.numpy as jnp
from jax import lax
from jax.experimental import pallas as pl
from jax.experimental.pallas import tpu as pltpu

BF = jnp.bfloat16
F32 = jnp.float32
EPS = 1e-6
D_MODEL = 1024
LANES = 128
N_CHIPS = 4
FFN_SHARD = 704
FFN_SHARD_PAD = 768
POOL_WINDOWS = (2, 4, 8, 16)
SGU_HEADS = 4
CHUNK = 128
CONV_TAPS = 31
CONV_PAD = 32
ROW_TILE = 512
VMEM_LIMIT_BYTES = 56 * 1024 * 1024
ADAM_LR, ADAM_B1, ADAM_B2, ADAM_EPS, ADAM_WD, ADAM_STEP = 0.001, 0.9, 0.999, 1e-08, 0.01, 10
MESH = pl.DeviceIdType.MESH
ANY = pl.BlockSpec(memory_space=pl.ANY)

ZB_POOL, ZB_U, ZB_V, ZB_A, ZB_B, ZB_GATES = 0, 1, 2, 3, 4, 5

TRANSPOSED = ("ffn1_w_gate", "ffn1_w_up", "ffn2_w_gate", "ffn2_w_up")
BIG = (
    ("ffn1_w_gate", "row", FFN_SHARD, 1024, FFN_SHARD_PAD, 1024),
    ("ffn1_w_up", "row", FFN_SHARD, 1024, FFN_SHARD_PAD, 1024),
    ("ffn1_w_down", "row", FFN_SHARD, 1024, FFN_SHARD_PAD, 1024),
    ("w_in", "col", 1024, 1408, 1024, 1408),
    ("w_pool_out", "col", 512, 256, 512, 256),
    ("w_sgu_out", "col", 512, 256, 512, 256),
    ("w_conv_out", "col", 512, 256, 512, 256),
    ("w_out", "row", 256, 1024, 256, 1024),
    ("ffn2_w_gate", "row", FFN_SHARD, 1024, FFN_SHARD_PAD, 1024),
    ("ffn2_w_up", "row", FFN_SHARD, 1024, FFN_SHARD_PAD, 1024),
    ("ffn2_w_down", "row", FFN_SHARD, 1024, FFN_SHARD_PAD, 1024),
    ("ple_w_proj", "col", 256, 256, 256, 256),
    ("ple_w_gate", "row", 256, 1024, 256, 1024),
)
SMALL = ("ffn1_pre_g", "ffn1_post_g", "mix_pre_g", "pool_w", "pool_scale", "sgu_ln_g", "sgu_ln_b", "sgu_w_s",
         "sgu_b_s", "conv_dw_b", "conv_ln_g", "conv_ln_b", "mix_post_g", "ffn2_pre_g", "ffn2_post_g",
         "ple_pre_g", "ple_post_g")
WEIGHTS = ("ffn1_pre_g", "ffn1_w_gate", "ffn1_w_up", "ffn1_w_down", "ffn1_post_g", "mix_pre_g", "w_in", "pool_w",
           "pool_scale", "w_pool_out", "sgu_ln_g", "sgu_ln_b", "sgu_w_s", "sgu_b_s", "w_sgu_out", "conv_dw_k",
           "conv_dw_b", "conv_ln_g", "conv_ln_b", "w_conv_out", "w_out", "mix_post_g", "ffn2_pre_g", "ffn2_w_gate",
           "ffn2_w_up", "ffn2_w_down", "ffn2_post_g", "ple_w_proj", "ple_pre_g", "ple_w_gate", "ple_post_g")


def _params(n_grid):
    return pltpu.CompilerParams(dimension_semantics=("arbitrary",) * n_grid, vmem_limit_bytes=VMEM_LIMIT_BYTES)


def _tile(n, cap=ROW_TILE):
    for t in range(min(cap, n) - min(cap, n) % 16, 0, -16):
        if n % t == 0:
            return t
    return n


def _sigmoid(x):
    return 1.0 / (1.0 + jnp.exp(-x))


def _silu_and_grad(x):
    s = _sigmoid(x)
    return x * s, s * (1.0 + x * (1.0 - s))


def _gelu_and_grad(x):
    cdf = 0.5 * (1.0 + lax.erf(x * (1.0 / math.sqrt(2.0))))
    pdf = jnp.exp(-0.5 * x * x) * (1.0 / math.sqrt(2.0 * math.pi))
    return x * cdf, cdf + x * pdf


def _rms_fwd(x, g):
    return x * lax.rsqrt(jnp.mean(x * x, axis=-1, keepdims=True) + EPS) * g


def _rms_bwd(x, g, dy):
    r = lax.rsqrt(jnp.mean(x * x, axis=-1, keepdims=True) + EPS)
    xh = x * r
    dxh = dy * g
    dx = r * (dxh - xh * jnp.mean(dxh * xh, axis=-1, keepdims=True))
    return dx, jnp.sum(dy * xh, axis=0, keepdims=True)


def _ln_stats(x):
    xc = x - jnp.mean(x, axis=-1, keepdims=True)
    r = lax.rsqrt(jnp.mean(xc * xc, axis=-1, keepdims=True) + EPS)
    return xc * r, r


def _ln_bwd(xh, r, g, dy):
    dxh = dy * g
    dx = r * (dxh - jnp.mean(dxh, axis=-1, keepdims=True) - xh * jnp.mean(dxh * xh, axis=-1, keepdims=True))
    return dx, jnp.sum(dy * xh, axis=0, keepdims=True), jnp.sum(dy, axis=0, keepdims=True)


def _rowwise(name, fn, rows, consts, outs, accs=(), tm=ROW_TILE, deps=()):
    T = rows[0][0].shape[-2]
    tm = _tile(T, tm)
    n_in, n_o, n_dep = len(rows) + len(consts), len(outs), len(deps)

    def body(*refs):
        refs = refs[n_dep:]
        res = fn(*[r[...] for r in refs[:n_in]])
        for ref, val in zip(refs[n_in:n_in + n_o], res[:n_o]):
            ref[...] = val.astype(ref.dtype)
        acc_refs = refs[n_in + n_o:]
        if acc_refs:
            @pl.when(pl.program_id(0) == 0)
            def _():
                for ref, val in zip(acc_refs, res[n_o:]):
                    ref[...] = val

            @pl.when(pl.program_id(0) != 0)
            def _():
                for ref, val in zip(acc_refs, res[n_o:]):
                    ref[...] += val

    in_specs = [ANY] * n_dep
    for row in rows:
        w, cb = row[1], row[2]
        if len(row) == 4:
            in_specs.append(pl.BlockSpec((None, tm, w), lambda i, cb=cb, ld=row[3]: (ld, i, cb)))
        else:
            in_specs.append(pl.BlockSpec((tm, w), lambda i, cb=cb: (i, cb)))
    in_specs += [pl.BlockSpec(c.shape, lambda i: (0, 0)) for c in consts]
    out_specs = [pl.BlockSpec((tm, w), lambda i: (i, 0)) for w, _ in outs]
    out_specs += [pl.BlockSpec(s, lambda i: (0, 0)) for s in accs]
    out_shape = [jax.ShapeDtypeStruct((T, w), dt) for w, dt in outs]
    out_shape += [jax.ShapeDtypeStruct(s, F32) for s in accs]
    return pl.pallas_call(body, grid=(T // tm,), in_specs=in_specs, out_specs=out_specs, out_shape=out_shape,
                          name=name, compiler_params=_params(1))(*deps, *[r[0] for r in rows], *consts)


def _tiled(name, fn, grid, pos, ins, outs, into=None):
    n_in = len(ins)
    extra = [] if into is None else [into]

    def body(_, *refs):
        res = fn(*[r[...] for r in refs[:n_in]])
        for ref, val in zip(refs[n_in + len(extra):], res):
            ref[...] = val.astype(ref.dtype)

    spec = pltpu.PrefetchScalarGridSpec(
        num_scalar_prefetch=1, grid=grid, in_specs=[pl.BlockSpec(bs, im) for _, bs, im in ins] + [ANY] * len(extra),
        out_specs=[pl.BlockSpec(bs, im) for _, _, bs, im in outs])
    return pl.pallas_call(body, grid_spec=spec, out_shape=[jax.ShapeDtypeStruct(s, d) for s, d, _, _ in outs],
                          input_output_aliases={1 + n_in: 0} if extra else {}, name=name,
                          compiler_params=_params(len(grid)))(pos, *[a for a, _, _ in ins], *extra)


def _cast_layers(name, w, rp, cp, dtype, pos):
    L, r, c = w.shape

    def body(_, w_ref, *o_refs):
        for k, o_ref in enumerate(o_refs):
            @pl.when(pl.program_id(0) == k)
            def _(o_ref=o_ref):
                if (rp, cp) != (r, c):
                    o_ref[...] = jnp.zeros_like(o_ref)
                    o_ref[pl.ds(0, r), pl.ds(0, c)] = w_ref[...].astype(dtype)
                else:
                    o_ref[...] = w_ref[...].astype(dtype)

    spec = pltpu.PrefetchScalarGridSpec(
        num_scalar_prefetch=1, grid=(L,), in_specs=[pl.BlockSpec((None, r, c), lambda l, p: (l, 0, 0))],
        out_specs=[pl.BlockSpec((None, rp, cp), lambda l, p: (p[0], 0, 0))] * L)
    return pl.pallas_call(body, grid_spec=spec, out_shape=[jax.ShapeDtypeStruct((N_CHIPS, rp, cp), dtype)] * L,
                          name=name, compiler_params=_params(1))(pos, w)


_NN = (((1,), (0,)), ((), ()))
_NT = (((1,), (1,)), ((), ()))
_TN = (((0,), (0,)), ((), ()))


def _mm_blockout(name, x, w4, l, trans_w, tm=ROW_TILE):
    M, kx = x.shape
    w3 = w4[l]
    nb, r, cc = w3.shape
    bo = r if trans_w else cc
    tm = _tile(M, tm)

    def body(x_ref, w_ref, o_ref):
        o_ref[...] = lax.dot_general(x_ref[...].astype(BF), w_ref[...].astype(BF), _NT if trans_w else _NN,
                                     preferred_element_type=F32).astype(o_ref.dtype)

    return pl.pallas_call(
        body, grid=(nb, M // tm),
        in_specs=[pl.BlockSpec((tm, kx), lambda b, i: (i, 0)),
                  pl.BlockSpec((None, r, cc), lambda b, i: (b, 0, 0))],
        out_specs=pl.BlockSpec((tm, bo), lambda b, i: (i, b)),
        out_shape=jax.ShapeDtypeStruct((M, nb * bo), BF), name=name, compiler_params=_params(2))(x, w3)


def _mm_blocksum(name, pairs, l, trans_w, tm=2 * ROW_TILE):
    pairs = [(x, w4[l]) for x, w4 in pairs]
    x0, w0 = pairs[0]
    M = x0.shape[0]
    nb, r, cc = w0.shape
    bw, n_out = (cc, r) if trans_w else (r, cc)
    tm = _tile(M, tm)
    n_p = len(pairs)

    def body(*refs):
        o_ref, acc = refs[2 * n_p], refs[2 * n_p + 1]
        b = pl.program_id(1)

        @pl.when(b == 0)
        def _():
            acc[...] = jnp.zeros_like(acc)

        t = acc[...]
        for x_ref, w_ref in zip(refs[:n_p], refs[n_p:2 * n_p]):
            t = t + lax.dot_general(x_ref[...].astype(BF), w_ref[...].astype(BF), _NT if trans_w else _NN,
                                    preferred_element_type=F32)
        acc[...] = t

        @pl.when(b == nb - 1)
        def _():
            o_ref[...] = acc[...].astype(o_ref.dtype)

    in_specs = [pl.BlockSpec((tm, bw), lambda i, b: (i, b)) for _ in pairs]
    in_specs += [pl.BlockSpec((None, r, cc), lambda i, b: (b, 0, 0)) for _ in pairs]
    return pl.pallas_call(
        body, grid=(M // tm, nb), in_specs=in_specs, out_specs=pl.BlockSpec((tm, n_out), lambda i, b: (i, 0)),
        out_shape=jax.ShapeDtypeStruct((M, n_out), BF), scratch_shapes=[pltpu.VMEM((tm, n_out), F32)],
        name=name, compiler_params=_params(2))(*[p[0] for p in pairs], *[p[1] for p in pairs])


def _mm_tn(name, a, dy, buf, l, a_blocked, tk=ROW_TILE):
    T = a.shape[0]
    nb, R, C = buf[l].shape

    def body(a_ref, dy_ref, o_ref):
        o_ref[...] = lax.dot_general(a_ref[...].astype(BF), dy_ref[...].astype(BF), _TN,
                                     preferred_element_type=F32).astype(o_ref.dtype)

    if a_blocked:
        grid = (nb,)
        in_specs = [pl.BlockSpec((T, R), lambda b: (0, b)), pl.BlockSpec((T, C), lambda b: (0, 0))]
        out_specs = pl.BlockSpec((None, R, C), lambda b: (b, 0, 0))
    else:
        tk = min(tk, R)
        grid = (nb, R // tk)
        in_specs = [pl.BlockSpec((T, tk), lambda b, k: (0, k)), pl.BlockSpec((T, C), lambda b, k: (0, b))]
        out_specs = pl.BlockSpec((None, tk, C), lambda b, k: (b, k, 0))
    buf = list(buf)
    buf[l] = pl.pallas_call(body, grid=grid, in_specs=in_specs, out_specs=out_specs,
                            out_shape=jax.ShapeDtypeStruct((nb, R, C), BF), name=name,
                            compiler_params=_params(len(grid)))(a, dy)
    return buf


def _acc_rows(ref, val, first):
    @pl.when(first)
    def _():
        ref[...] = val

    @pl.when(jnp.logical_not(first))
    def _():
        ref[...] += val


def _norm_mm(name, h, g, ws, trans_w, act=False, deps=(), tm=2 * ROW_TILE):
    T = h.shape[0]
    nb, r, cc = ws[0].shape
    bo = r if trans_w else cc
    tm = _tile(T, tm)
    n_w, n_dep = len(ws), len(deps)

    def body(*refs):
        refs = refs[n_dep:]
        h_ref, g_ref, w_refs = refs[0], refs[1], refs[2:2 + n_w]
        n_ref, o_refs, n_s = refs[2 + n_w], refs[3 + n_w:3 + 2 * n_w], refs[-1]

        @pl.when(pl.program_id(1) == 0)
        def _():
            n = _rms_fwd(h_ref[...].astype(F32), g_ref[...]).astype(BF)
            n_s[...] = n
            n_ref[...] = n

        n = n_s[...]
        prods = []
        for w_ref, o_ref in zip(w_refs, o_refs):
            prods.append(lax.dot_general(n, w_ref[...], _NT if trans_w else _NN,
                                         preferred_element_type=F32).astype(BF))
            o_ref[...] = prods[-1]
        if act:
            refs[3 + 2 * n_w][...] = (_silu_and_grad(prods[0].astype(F32))[0] * prods[1].astype(F32)).astype(BF)

    wide = pl.BlockSpec((tm, bo), lambda i, b: (i, b))
    n_out = n_w + (1 if act else 0)
    return pl.pallas_call(
        body, grid=(T // tm, nb),
        in_specs=[ANY] * n_dep + [pl.BlockSpec((tm, D_MODEL), lambda i, b: (i, 0)),
                                  pl.BlockSpec(g.shape, lambda i, b: (0, 0))]
        + [pl.BlockSpec((None, r, cc), lambda i, b: (b, 0, 0))] * n_w,
        out_specs=[pl.BlockSpec((tm, D_MODEL), lambda i, b: (i, 0))] + [wide] * n_out,
        out_shape=[jax.ShapeDtypeStruct((T, D_MODEL), BF)] + [jax.ShapeDtypeStruct((T, nb * bo), BF)] * n_out,
        scratch_shapes=[pltpu.VMEM((tm, D_MODEL), BF)], name=name, compiler_params=_params(2))(*deps, h, g, *ws)


def _mm_res(name, x, w3, h, g, coef, tm=ROW_TILE):
    T, kx = x.shape
    w2 = w3.reshape(kx, D_MODEL)
    tm = _tile(T, tm)

    def body(x_ref, w_ref, h_ref, g_ref, f_ref, o_ref):
        f = jnp.dot(x_ref[...], w_ref[...], preferred_element_type=F32).astype(BF)
        f_ref[...] = f
        o_ref[...] = h_ref[...] + coef * _rms_fwd(f.astype(F32), g_ref[...])

    row = pl.BlockSpec((tm, D_MODEL), lambda i: (i, 0))
    return pl.pallas_call(
        body, grid=(T // tm,),
        in_specs=[pl.BlockSpec((tm, kx), lambda i: (i, 0)), pl.BlockSpec(w2.shape, lambda i: (0, 0)), row,
                  pl.BlockSpec(g.shape, lambda i: (0, 0))],
        out_specs=[row, row],
        out_shape=[jax.ShapeDtypeStruct((T, D_MODEL), BF), jax.ShapeDtypeStruct((T, D_MODEL), F32)],
        name=name, compiler_params=_params(1))(x, w2, h, g)


def _resbwd_mm(name, dh, f, g, coef, w3, trans_w, act=None, deps=(), tm=ROW_TILE):
    T = dh.shape[0]
    nb, r, cc = w3.shape
    bo = r if trans_w else cc
    tm = _tile(T, tm)
    n_dep, n_act = len(deps), 2 if act else 0

    def body(*refs):
        refs = refs[n_dep:]
        dh_ref, f_ref, g_ref, w_ref = refs[:4]
        df_ref, dg_ref = refs[4 + n_act], refs[5 + n_act]
        df_s = refs[-1]
        i, b = pl.program_id(0), pl.program_id(1)

        @pl.when(b == 0)
        def _():
            dx, dg = _rms_bwd(f_ref[...].astype(F32), g_ref[...], coef * dh_ref[...])
            df_s[...] = dx.astype(BF)
            df_ref[...] = dx.astype(BF)
            _acc_rows(dg_ref, dg, i == 0)

        prod =lax.dot_general(df_s[...], w_ref[...], _NT if trans_w else _NN, preferred_element_type=F32)
        if act:
            gate, up = refs[4][...].astype(F32), refs[5][...].astype(F32)
            val, grad = _silu_and_grad(gate)
            refs[6 + n_act][...] = (prod * up * grad).astype(BF)
            refs[7 + n_act][...] = (prod * val).astype(BF)
        else:
            refs[6][...] = prod.astype(BF)

    row = pl.BlockSpec((tm, D_MODEL), lambda i, b: (i, 0))
    wide = pl.BlockSpec((tm, bo), lambda i, b: (i, b))
    vec = pl.BlockSpec((1, D_MODEL), lambda i, b: (0, 0))
    n_prod = 2 if act else 1
    return pl.pallas_call(
        body, grid=(T // tm, nb),
        in_specs=[ANY] * n_dep + [row, row, vec, pl.BlockSpec((None, r, cc), lambda i, b: (b, 0, 0))] + [wide] * n_act,
        out_specs=[row, vec] + [wide] * n_prod,
        out_shape=[jax.ShapeDtypeStruct((T, D_MODEL), BF), jax.ShapeDtypeStruct((1, D_MODEL), F32)]
        + [jax.ShapeDtypeStruct((T, nb * bo), BF)] * n_prod,
        scratch_shapes=[pltpu.VMEM((tm, D_MODEL), BF)], name=name,
        compiler_params=_params(2))(*deps, dh, f, g, w3, *(act or ()))


def _dn_prenorm(name, pairs, trans_w, dh, h, g, tm=ROW_TILE):
    T = dh.shape[0]
    nb, r, cc = pairs[0][1].shape
    bw = cc if trans_w else r
    tm = _tile(T, tm)
    n_p = len(pairs)

    def body(*refs):
        x_refs, w_refs = refs[:n_p], refs[n_p:2 * n_p]
        dh_ref, h_ref, g_ref, o_ref, dg_ref, acc = refs[2 * n_p:]
        i, b = pl.program_id(0), pl.program_id(1)
        t = None
        for x_ref, w_ref in zip(x_refs, w_refs):
            p = lax.dot_general(x_ref[...], w_ref[...], _NT if trans_w else _NN, preferred_element_type=F32)
            t = p if t is None else t + p

        @pl.when(b == 0)
        def _():
            acc[...] = t

        @pl.when(b != 0)
        def _():
            acc[...] += t

        @pl.when(b == nb - 1)
        def _():
            dx, dg = _rms_bwd(h_ref[...], g_ref[...], acc[...])
            o_ref[...] = dh_ref[...] + dx
            _acc_rows(dg_ref, dg, i == 0)

    row = pl.BlockSpec((tm, D_MODEL), lambda i, b: (i, 0))
    vec = pl.BlockSpec((1, D_MODEL), lambda i, b: (0, 0))
    return pl.pallas_call(
        body, grid=(T // tm, nb),
        in_specs=[pl.BlockSpec((tm, bw), lambda i, b: (i, b))] * n_p
        + [pl.BlockSpec((None, r, cc), lambda i, b: (b, 0, 0))] * n_p + [row, row, vec],
        out_specs=[row, vec],
        out_shape=[jax.ShapeDtypeStruct((T, D_MODEL), F32), jax.ShapeDtypeStruct((1, D_MODEL), F32)],
        scratch_shapes=[pltpu.VMEM((tm, D_MODEL), F32)], name=name,
        compiler_params=_params(2))(*[p[0] for p in pairs], *[p[1] for p in pairs], dh, h, g)


def _pool_apply(x, win, row):
    s, k = x, 1
    while k < win:
        s = s + jnp.where(row >= k, pltpu.roll(s, k, 0), 0.0)
        k *= 2
    return s / jnp.minimum(row + 1, win).astype(F32) - x


def _pool_apply_t(dp, win, row):
    T = dp.shape[0]
    s, k = dp / jnp.minimum(row + 1, win).astype(F32), 1
    while k < win:
        s = s + jnp.where(row < T - k, pltpu.roll(s, T - k, 0), 0.0)
        k *= 2
    return s - dp


def _pool_fwd(z, w, scale):
    T = z.shape[0]

    def body(z_ref, w_ref, s_ref, o_ref):
        row = lax.broadcasted_iota(jnp.int32, (T, LANES), 0)
        for gi, win in enumerate(POOL_WINDOWS):
            cols = pl.ds(gi * LANES, LANES)
            pooled = _pool_apply(z_ref[:, cols].astype(F32), win, row)
            y = jnp.dot(pooled.astype(BF), w_ref[gi].astype(BF), preferred_element_type=F32)
            o_ref[:, cols] = (y * s_ref[:, cols]).astype(o_ref.dtype)

    return pl.pallas_call(
        body, grid=(1,),
        in_specs=[pl.BlockSpec((T, 512), lambda i: (0, ZB_POOL)), pl.BlockSpec(w.shape, lambda i: (0, 0, 0)),
                  pl.BlockSpec(scale.shape, lambda i: (0, 0))],
        out_specs=pl.BlockSpec((T, 512), lambda i: (0, 0)), out_shape=jax.ShapeDtypeStruct((T, 512), BF),
        name="pool_fwd", compiler_params=_params(1))(z, w, scale)


def _pool_bwd(dr, z, w, scale):
    T = z.shape[0]

    def body(dr_ref, z_ref, w_ref, s_ref, dz_ref, dw_ref, ds_ref):
        row = lax.broadcasted_iota(jnp.int32, (T, LANES), 0)
        for gi, win in enumerate(POOL_WINDOWS):
            cols = pl.ds(gi * LANES, LANES)
            pooled = _pool_apply(z_ref[:, cols].astype(F32), win, row).astype(BF)
            wg = w_ref[gi].astype(BF)
            y = jnp.dot(pooled, wg, preferred_element_type=F32)
            d = dr_ref[:, cols].astype(F32)
            ds_ref[:, cols] = jnp.sum(d * y, axis=0, keepdims=True)
            dy = (d * s_ref[:, cols]).astype(BF)
            dw_ref[gi] = lax.dot_general(pooled, dy, _TN, preferred_element_type=F32)
            dpooled = lax.dot_general(dy, wg, _NT, preferred_element_type=F32)
            dz_ref[:, cols] = _pool_apply_t(dpooled, win, row).astype(dz_ref.dtype)

    return pl.pallas_call(
        body, grid=(1,),
        in_specs=[pl.BlockSpec((T, 512), lambda i: (0, 0)), pl.BlockSpec((T, 512), lambda i: (0, ZB_POOL)),
                  pl.BlockSpec(w.shape, lambda i: (0, 0, 0)), pl.BlockSpec(scale.shape, lambda i: (0, 0))],
        out_specs=[pl.BlockSpec((T, 512), lambda i: (0, 0)), pl.BlockSpec(w.shape, lambda i: (0, 0, 0)),
                   pl.BlockSpec(scale.shape, lambda i: (0, 0))],
        out_shape=[jax.ShapeDtypeStruct((T, 512), BF), jax.ShapeDtypeStruct(w.shape, F32),
                   jax.ShapeDtypeStruct(scale.shape, F32)],
        name="pool_bwd", compiler_params=_params(1))(dr, z, w, scale)


def _tril(transposed=False):
    r = lax.broadcasted_iota(jnp.int32, (CHUNK, CHUNK), 0)
    c = lax.broadcasted_iota(jnp.int32, (CHUNK, CHUNK), 1)
    return c >= r if transposed else r >= c


def _sgu_fwd(z, ln_g, ln_b, w_s, bias):
    T = z.shape[0]
    tm = _tile(T)

    def body(zu_ref, zv_ref, g_ref, b_ref, w_ref, bias_ref, o_ref):
        gu, _ = _gelu_and_grad(zu_ref[...].astype(F32))
        gv, _ = _gelu_and_grad(zv_ref[...].astype(F32))
        xh, _ = _ln_stats(gv)
        v16 = (xh * g_ref[...] + b_ref[...]).astype(BF)
        tri = _tril()
        for h in range(SGU_HEADS):
            cols = slice(h * LANES, (h + 1) * LANES)
            wh = jnp.where(tri, w_ref[h], 0.0).astype(BF)
            for c in range(tm // CHUNK):
                rows = slice(c * CHUNK, (c + 1) * CHUNK)
                s = jnp.dot(wh, v16[rows, cols], preferred_element_type=F32) + bias_ref[:, cols]
                o_ref[rows, cols] = (gu[rows, cols] * s).astype(o_ref.dtype)

    small = [pl.BlockSpec(a.shape, lambda i, n=a.ndim: (0,) * n) for a in (ln_g, ln_b, w_s, bias)]
    return pl.pallas_call(
        body, grid=(T // tm,),
        in_specs=[pl.BlockSpec((tm, 512), lambda i: (i, ZB_U)), pl.BlockSpec((tm, 512), lambda i: (i, ZB_V))] + small,
        out_specs=pl.BlockSpec((tm, 512), lambda i: (i, 0)), out_shape=jax.ShapeDtypeStruct((T, 512), BF),
        name="sgu_fwd", compiler_params=_params(1))(z, z, ln_g, ln_b, w_s, bias)


def _sgu_bwd(dr, z, ln_g, ln_b, w_s, w_st, bias):
    T = z.shape[0]
    tm = _tile(T)
    n_steps = T // tm

    def body(dr_ref, zu_ref, zv_ref, g_ref, b_ref, w_ref, wt_ref, bias_ref,
             dzu_ref, dzv_ref, dg_ref, db_ref, dw_ref, dbias_ref, dgu_s, dv_s):
        i = pl.program_id(0)

        @pl.when(i == 0)
        def _():
            dg_ref[...] = jnp.zeros_like(dg_ref)
            db_ref[...] = jnp.zeros_like(db_ref)
            dw_ref[...] = jnp.zeros_like(dw_ref)
            dbias_ref[...] = jnp.zeros_like(dbias_ref)

        zu = zu_ref[...].astype(F32)
        zv = zv_ref[...].astype(F32)
        gu, gu_grad = _gelu_and_grad(zu)
        gv, gv_grad = _gelu_and_grad(zv)
        xh, r = _ln_stats(gv)
        v16 = (xh * g_ref[...] + b_ref[...]).astype(BF)
        dr = dr_ref[...].astype(F32)
        tri = _tril()
        for h in range(SGU_HEADS):
            cols = slice(h * LANES, (h + 1) * LANES)
            wh = jnp.where(tri, w_ref[h], 0.0).astype(BF)
            wht = jnp.where(_tril(transposed=True), wt_ref[h], 0.0).astype(BF)
            for c in range(tm // CHUNK):
                rows = slice(c * CHUNK, (c + 1) * CHUNK)
                v_blk = v16[rows, cols]
                s = jnp.dot(wh, v_blk, preferred_element_type=F32) + bias_ref[:, cols]
                ds = dr[rows, cols] * gu[rows, cols]
                dgu_s[rows, cols] = dr[rows, cols] * s
                ds16 = ds.astype(BF)
                dw_ref[h] += jnp.where(tri, lax.dot_general(ds16, v_blk, _NT, preferred_element_type=F32), 0.0)
                dv_s[rows, cols] = jnp.dot(wht, ds16, preferred_element_type=F32)
                dbias_ref[:, cols] += ds
        dzu_ref[...] = (dgu_s[...] * gu_grad).astype(dzu_ref.dtype)
        dgv, dg, db = _ln_bwd(xh, r, g_ref[...], dv_s[...])
        dzv_ref[...] = (dgv * gv_grad).astype(dzv_ref.dtype)
        dg_ref[...] += dg
        db_ref[...] += db

        @pl.when(i == n_steps - 1)
        def _():
            for h in range(SGU_HEADS):
                cols = slice(h * LANES, (h + 1) * LANES)
                tot = jnp.sum(dbias_ref[:, cols], axis=1, keepdims=True)
                dbias_ref[:, cols] = jnp.broadcast_to(tot, (CHUNK, LANES))

    small = (ln_g, ln_b, w_s, w_st, bias)
    small_specs = [pl.BlockSpec(a.shape, lambda i, n=a.ndim: (0,) * n) for a in small]
    return pl.pallas_call(
        body, grid=(n_steps,),
        in_specs=[pl.BlockSpec((tm, 512), lambda i: (i, 0)), pl.BlockSpec((tm, 512), lambda i: (i, ZB_U)),
                  pl.BlockSpec((tm, 512), lambda i: (i, ZB_V))] + small_specs,
        out_specs=[pl.BlockSpec((tm, 512), lambda i: (i, 0)), pl.BlockSpec((tm, 512), lambda i: (i, 0)),
                   pl.BlockSpec((1, 512), lambda i: (0, 0)), pl.BlockSpec((1, 512), lambda i: (0, 0)),
                   pl.BlockSpec(w_s.shape, lambda i: (0, 0, 0)), pl.BlockSpec(bias.shape, lambda i: (0, 0))],
        out_shape=[jax.ShapeDtypeStruct((T, 512), BF), jax.ShapeDtypeStruct((T, 512), BF),
                   jax.ShapeDtypeStruct((1, 512), F32), jax.ShapeDtypeStruct((1, 512), F32),
                   jax.ShapeDtypeStruct(w_s.shape, F32), jax.ShapeDtypeStruct(bias.shape, F32)],
        scratch_shapes=[pltpu.VMEM((tm, 512), F32), pltpu.VMEM((tm, 512), F32)],
        name="sgu_bwd", compiler_params=_params(1))(dr, z, z, ln_g, ln_b, w_s, w_st, bias)


def _conv_fwd(z, convk, l, bias):
    T = z.shape[0]

    def body(za_ref, zb_ref, k_ref, b_ref, o_ref):
        xg = za_ref[...].astype(F32) * _sigmoid(zb_ref[...].astype(F32))
        xp = jnp.concatenate([jnp.zeros((CONV_PAD, LANES), F32), xg], axis=0)
        kw = k_ref[...]
        acc = jnp.broadcast_to(b_ref[...], (T, LANES))
        for k in range(CONV_TAPS):
            sh = CONV_TAPS - 1 - k
            tap = xp if sh == 0 else pltpu.roll(xp, sh, 0)
            acc = acc + kw[k:k + 1, :] * tap[CONV_PAD:, :]
        o_ref[...] = acc.astype(o_ref.dtype)

    return pl.pallas_call(
        body, grid=(4,),
        in_specs=[pl.BlockSpec((T, LANES), lambda g: (0, 4 * ZB_A + g)),
                  pl.BlockSpec((T, LANES), lambda g: (0, 4 * ZB_B + g)),
                  pl.BlockSpec((None, CONV_PAD, LANES), lambda g: (g, 0, 0)),
                  pl.BlockSpec((1, LANES), lambda g: (0, g))],
        out_specs=pl.BlockSpec((T, LANES), lambda g: (0, g)), out_shape=jax.ShapeDtypeStruct((T, 512), BF),
        name="conv_fwd", compiler_params=_params(1))(z, z, convk[l], bias)


def _conv_bwd(dy, z, convk, l):
    T = z.shape[0]

    def body(dy_ref, za_ref, zb_ref, k_ref, dza_ref, dzb_ref, dk_ref, db_ref):
        a = za_ref[...].astype(F32)
        sg = _sigmoid(zb_ref[...].astype(F32))
        d = dy_ref[...].astype(F32)
        kw = k_ref[...]
        xp = jnp.concatenate([jnp.zeros((CONV_PAD, LANES), F32), a * sg], axis=0)
        dp = jnp.concatenate([d, jnp.zeros((CONV_PAD, LANES), F32)], axis=0)
        dxg = jnp.zeros((T, LANES), F32)
        dk_ref[...] = jnp.zeros_like(dk_ref)
        for k in range(CONV_TAPS):
            sh = CONV_TAPS - 1 - k
            tap = xp if sh == 0 else pltpu.roll(xp, sh, 0)
            dk_ref[k:k + 1, :] = jnp.sum(d * tap[CONV_PAD:, :], axis=0, keepdims=True)
            dtap = dp if sh == 0 else pltpu.roll(dp, T + CONV_PAD - sh, 0)
            dxg = dxg + kw[k:k + 1, :] * dtap[:T, :]
        db_ref[...] = jnp.sum(d, axis=0, keepdims=True)
        dza_ref[...] = (dxg * sg).astype(dza_ref.dtype)
        dzb_ref[...] = (dxg * a * sg * (1.0 - sg)).astype(dzb_ref.dtype)

    col = pl.BlockSpec((T, LANES), lambda g: (0, g))
    return pl.pallas_call(
        body, grid=(4,),
        in_specs=[col, pl.BlockSpec((T, LANES), lambda g: (0, 4 * ZB_A + g)),
                  pl.BlockSpec((T, LANES), lambda g: (0, 4 * ZB_B + g)),
                  pl.BlockSpec((None, CONV_PAD, LANES), lambda g: (g, 0, 0))],
        out_specs=[col, col, pl.BlockSpec((CONV_PAD, LANES), lambda g: (0, g)),
                   pl.BlockSpec((1, LANES), lambda g: (0, g))],
        out_shape=[jax.ShapeDtypeStruct((T, 512), BF), jax.ShapeDtypeStruct((T, 512), BF),
                   jax.ShapeDtypeStruct((CONV_PAD, 512), F32), jax.ShapeDtypeStruct((1, 512), F32)],
        name="conv_bwd", compiler_params=_params(1))(dy, z, z, convk[l])


D = D_MODEL


def _ffn_fwd(l, h, S, W, pre, deps=()):
    n, gp, u, a = _norm_mm("ffn_in", h, S[pre + "_pre_g"], [W[pre + "_w_gate"][l], W[pre + "_w_up"][l]], True,
                           act=True, deps=deps)
    f, out = _mm_res("ffn_out", a, W[pre + "_w_down"][l], h, S[pre + "_post_g"], 0.5)
    return out, dict(h=h, n=n, gp=gp, u=u, a=a, f=f)


def _ffn_bwd(l, dh, sv, S, W, G, SG, pre, deps=()):
    df, SG[pre + "_post_g"], dgp, du = _resbwd_mm("ffn_bwd_act", dh, sv["f"], S[pre + "_post_g"], 0.5,
                                                  W[pre + "_w_down"][l], True, act=(sv["gp"], sv["u"]), deps=deps)
    G[pre + "_w_down"] = _mm_tn("ffn_dw_down", sv["a"], df, G[pre + "_w_down"], l, True)
    G[pre + "_w_gate"] = _mm_tn("ffn_dw_gate", dgp, sv["n"], G[pre + "_w_gate"], l, True)
    G[pre + "_w_up"] = _mm_tn("ffn_dw_up", du, sv["n"], G[pre + "_w_up"], l, True)
    dh_in, SG[pre + "_pre_g"] = _dn_prenorm("ffn_bwd_in", [(dgp, W[pre + "_w_gate"][l]), (du, W[pre + "_w_up"][l])],
                                            False, dh, sv["h"], S[pre + "_pre_g"])
    return dh_in


def _gate_rows(z):
    return [(z, 512, ZB_GATES + j) for j in range(6)]


def _gates(zg):
    return [_sigmoid(jnp.concatenate([zg[2 * k].astype(F32), zg[2 * k + 1].astype(F32)], axis=1)) for k in range(3)]


def _mix_fwd(l, h, S, W):
    n, z = _norm_mm("mix_in", h, S["mix_pre_g"], [W["w_in"][l]], False)
    r_pool = _pool_fwd(z, S["pool_w"], S["pool_scale"])
    r_sgu = _sgu_fwd(z, S["sgu_ln_g"], S["sgu_ln_b"], S["sgu_w_s"], S["sgu_bias"])
    yc = _conv_fwd(z, W["conv_dw_k"], l, S["conv_dw_b"])

    def ln_silu(y, g, b):
        xh, _ = _ln_stats(y.astype(F32))
        return (_silu_and_grad(xh * g + b)[0],)

    r_conv = _rowwise("conv_ln", ln_silu, [(yc, 512, 0)], [S["conv_ln_g"], S["conv_ln_b"]], [(512, BF)])[0]
    y_pool = _mm_blockout("branch_out", r_pool, W["w_pool_out"], l, False)
    y_sgu = _mm_blockout("branch_out", r_sgu, W["w_sgu_out"], l, False)
    y_conv = _mm_blockout("branch_out", r_conv, W["w_conv_out"], l, False)

    def merge(*t):
        g = _gates(t[:6])
        return (g[0] * t[6].astype(F32) + g[1] * t[7].astype(F32) + g[2] * t[8].astype(F32),)

    merged = _rowwise("mix_merge", merge, _gate_rows(z) + [(y_pool, D, 0), (y_sgu, D, 0), (y_conv, D, 0)], [],
                      [(D, BF)])[0]
    o, out = _mm_res("mix_out", merged, W["w_out"][l], h, S["mix_post_g"], 1.0)
    return out, dict(h=h, n=n, z=z, r_pool=r_pool, r_sgu=r_sgu, yc=yc, r_conv=r_conv, y_pool=y_pool, y_sgu=y_sgu,
                     y_conv=y_conv, merged=merged, o=o)


def _mix_bwd(l, dh, sv, S, W, G, SG, deps=()):
    z = sv["z"]
    do, SG["mix_post_g"], dmerged = _resbwd_mm("mix_bwd_out", dh, sv["o"], S["mix_post_g"], 1.0,
                                               W["w_out"][l].reshape(1, D, D), True, deps=deps)
    G["w_out"] = _mm_tn("mix_dw_out", sv["merged"], do, G["w_out"], l, True)

    def merge_bwd(dm, *t):
        g = _gates(t[:6])
        dm = dm.astype(F32)
        dzg = [dm * t[6 + k].astype(F32) * g[k] * (1.0 - g[k]) for k in range(3)]
        return dm * g[0], dm * g[1], dm * g[2], jnp.concatenate(dzg, axis=1)

    rows = [(dmerged, D, 0)] + _gate_rows(z) + [(sv["y_pool"], D, 0), (sv["y_sgu"], D, 0), (sv["y_conv"], D, 0)]
    dy_pool, dy_sgu, dy_conv, dzg = _rowwise("mix_merge_bwd", merge_bwd, rows, [], [(D, BF)] * 3 + [(3 * D, BF)],
                                             tm=ROW_TILE // 2)
    dr = {}
    for br, dy in (("pool", dy_pool), ("sgu", dy_sgu), ("conv", dy_conv)):
        wn = "w_%s_out" % br
        G[wn] = _mm_tn("branch_dw", sv["r_" + br], dy, G[wn], l, False)
        dr[br] = _mm_blocksum("branch_dr", [(dy, W[wn])], l, True)
    dz_pool, SG["pool_w"], SG["pool_scale"] = _pool_bwd(dr["pool"], z, S["pool_w"], S["pool_scale"])
    dzu, dzv, SG["sgu_ln_g"], SG["sgu_ln_b"], SG["sgu_w_s"], dbias = _sgu_bwd(
        dr["sgu"], z, S["sgu_ln_g"], S["sgu_ln_b"], S["sgu_w_s"], S["sgu_w_st"], S["sgu_bias"])
    SG["sgu_b_s"] = dbias[:, ::LANES].T

    def ln_silu_bwd(d, y, g, b):
        xh, r = _ln_stats(y.astype(F32))
        _, grad = _silu_and_grad(xh * g + b)
        return _ln_bwd(xh, r, g, d.astype(F32) * grad)

    dyc, SG["conv_ln_g"], SG["conv_ln_b"] = _rowwise(
        "conv_ln_bwd", ln_silu_bwd, [(dr["conv"], 512, 0), (sv["yc"], 512, 0)], [S["conv_ln_g"], S["conv_ln_b"]],
        [(512, BF)], [(1, 512), (1, 512)])
    dza, dzb, SG["conv_dw_k"], SG["conv_dw_b"] = _conv_bwd(dyc, z, W["conv_dw_k"], l)
    dz = jnp.concatenate([dz_pool, dzu, dzv, dza, dzb, dzg], axis=1)
    G["w_in"] = _mm_tn("mix_dw_in", sv["n"], dz, G["w_in"], l, False)
    dh_in, SG["mix_pre_g"] = _dn_prenorm("mix_bwd_in", [(dz, W["w_in"][l])], True, dh, sv["h"], S["mix_pre_g"])
    return dh_in


def _ple_fwd(l, h, p_l, S, W):
    n, gp = _norm_mm("ple_in", h, S["ple_pre_g"], [W["ple_w_gate"][l].reshape(1, D, D)], False)
    e = _mm_blockout("ple_proj", p_l, W["ple_w_proj"], l, False)

    def res(x, ee, g, gg):
        return (x + _rms_fwd(_sigmoid(g.astype(F32)) * ee.astype(F32), gg),)

    out = _rowwise("ple_res", res, [(h, D, 0), (e, D, 0), (gp, D, 0)], [S["ple_post_g"]], [(D, F32)])[0]
    return out, dict(h=h, n=n, e=e, gp=gp, p=p_l)


def _ple_bwd(l, dh, sv, S, W, G, SG, deps=()):
    def res_bwd(d, ee, g, gg):
        sg = _sigmoid(g.astype(F32))
        ee = ee.astype(F32)
        dq, dg = _rms_bwd(sg * ee, gg, d)
        return dq * sg, dq * ee * sg * (1.0 - sg), dg

    de, dgp, SG["ple_post_g"] = _rowwise("ple_res_bwd", res_bwd, [(dh, D, 0), (sv["e"], D, 0), (sv["gp"], D, 0)],
                                         [S["ple_post_g"]], [(D, BF), (D, BF)], [(1, D)], deps=deps)
    G["ple_w_proj"] = _mm_tn("ple_dw_proj", sv["p"], de, G["ple_w_proj"], l, False)
    G["ple_w_gate"] = _mm_tn("ple_dw_gate", sv["n"], dgp, G["ple_w_gate"], l, True)
    dh_in, SG["ple_pre_g"] = _dn_prenorm("ple_bwd_in", [(dgp, W["ple_w_gate"][l].reshape(1, D, D))], True, dh,
                                         sv["h"], S["ple_pre_g"])
    return dh_in


def _layer_small(a, l):
    S = {}
    for name in SMALL:
        v = a[name][l]
        S[name] = v.reshape(1, -1) if v.ndim == 1 else v
    S["sgu_w_st"] = jnp.swapaxes(S["sgu_w_s"], 1, 2)
    S["sgu_bias"] = jnp.repeat(S["sgu_b_s"].T, LANES, axis=1)
    return S


def _layer_fwd(l, h, p_l, S, W, deps=(), mid=None):
    h, sv1 = _ffn_fwd(l, h, S, W, "ffn1", deps)
    h, sv2 = _mix_fwd(l, h, S, W)
    h, sv3 = _ffn_fwd(l, h, S, W, "ffn2", mid(h) if mid else ())
    h, sv4 = _ple_fwd(l, h, p_l, S, W)
    return h, (sv1, sv2, sv3, sv4)


def _layer_bwd(l, dh, sv, S, W, G, deps=(), mid=None):
    SG = {}
    dh = _ple_bwd(l, dh, sv[3], S, W, G, SG, deps)
    dh = _ffn_bwd(l, dh, sv[2], S, W, G, SG, "ffn2")
    dh = _mix_bwd(l, dh, sv[1], S, W, G, SG, mid(dh) if mid else ())
    dh = _ffn_bwd(l, dh, sv[0], S, W, G, SG, "ffn1")
    return dh, SG


HBM = pl.BlockSpec(memory_space=pltpu.HBM)
SEM = pl.BlockSpec(memory_space=pltpu.SEMAPHORE)
SIDE_EFFECT = pltpu.SideEffectType.DATAFLOW_SIDE_EFFECTING


def _place():
    x, y, c = lax.axis_index("x"), lax.axis_index("y"), lax.axis_index("c")
    chips = [(1 - x, y), (x, 1 - y), (1 - x, 1 - y)]
    return x, y, c, chips


def _remote(src, dst, send_sem, recv_sem, to):
    return pltpu.make_async_remote_copy(src_ref=src, dst_ref=dst, send_sem=send_sem, recv_sem=recv_sem,
                                        device_id=to, device_id_type=MESH)


def _split_start(name, plan, bufs, deps):
    count, fn = plan
    n, nd = len(bufs), len(deps)

    def body(*refs):
        send, recv = refs[nd + n], refs[nd + n + 1]
        x, y, c, chips = _place()
        for k, (src, dst, _, to) in enumerate(fn(refs[nd:nd + n], x, y, c, chips)):
            _remote(src, dst, send.at[k], recv.at[k], to).start()
        refs[-1][...] = jnp.zeros_like(refs[-1])

    res = pl.pallas_call(
        body, in_specs=[ANY] * nd + [HBM] * n,
        out_specs=[SEM, SEM] + [HBM] * n + [pl.BlockSpec(memory_space=pltpu.VMEM)],
        out_shape=[pltpu.SemaphoreType.DMA((count,)), pltpu.SemaphoreType.DMA((count,))]
        + [pltpu.HBM(b.shape, b.dtype) for b in bufs] + [jax.ShapeDtypeStruct((8, LANES), F32)],
        input_output_aliases={nd + i: 2 + i for i in range(n)}, name=name,
        compiler_params=pltpu.CompilerParams(has_side_effects=SIDE_EFFECT),
    )(*deps, *[pltpu.with_memory_space_constraint(b, pltpu.HBM) for b in bufs])
    return (res[0], res[1]), list(res[2:2 + n]), res[-1]


def _split_wait(name, plan, sems, bufs, after):
    _, fn = plan
    n = len(bufs)

    def body(*refs):
        send, recv = refs[n], refs[n + 1]
        x, y, c, chips = _place()
        for k, (src, _, land, to) in enumerate(fn(refs[:n], x, y, c, chips)):
            cp = _remote(src, land, send.at[k], recv.at[k], to)
            cp.wait_send()
            cp.wait_recv()

    res = pl.pallas_call(
        body, in_specs=[HBM] * n + [SEM, SEM] + [ANY] * len(after), out_specs=[HBM] * n,
        out_shape=[pltpu.HBM(b.shape, b.dtype) for b in bufs], input_output_aliases={i: i for i in range(n)},
        name=name, compiler_params=pltpu.CompilerParams(has_side_effects=SIDE_EFFECT))(*bufs, *sems, *after)
    return list(res)


def _gather_plans(n):
    def across(b, x, y, c, chips):
        me, out = 2 * x + y, []
        for a in range(n):
            rh = b[a].shape[1] // 2
            mine = b[a].at[me, pl.ds(c * rh, rh)]
            for cx, cy in chips:
                out.append((mine, mine, b[a].at[2 * cx + cy, pl.ds(c * rh, rh)], (cx, cy, c)))
        return out

    def to_sibling(b, x, y, c, chips):
        out = []
        for a in range(n):
            rh = b[a].shape[1] // 2
            for cx, cy in chips:
                piece = b[a].at[2 * cx + cy, pl.ds(c * rh, rh)]
                out.append((piece, piece, b[a].at[2 * cx + cy, pl.ds((1 - c) * rh, rh)], (x, y, 1 - c)))
        return out

    return (3 * n, across), (3 * n, to_sibling)


def _pair_plan(n):
    def fn(b, x, y, c, chips):
        out = []
        for a in range(n):
            rh = b[a].shape[1] // 2
            out.append((b[a].at[:, pl.ds((1 - c) * rh, rh)], b[n + a], b[n + a], (x, y, 1 - c)))
        return out

    return n, fn


def _cross_plan(n):
    def fn(b, x, y, c, chips):
        out = []
        for a in range(n):
            for j, (cx, cy) in enumerate(chips):
                out.append((b[a].at[2 * cx + cy], b[n + a].at[j], b[n + a].at[j], (cx, cy, c)))
        return out

    return 3 * n, fn


def _share_plan(n, l):
    def fn(b, x, y, c, chips):
        out = []
        for a in range(n):
            rh = b[a].shape[1] // 2
            mine = b[a].at[l, pl.ds(c * rh, rh)]
            out.append((mine, mine, b[a].at[l, pl.ds((1 - c) * rh, rh)], (x, y, 1 - c)))
        return out

    return n, fn


def _exchange_call(name, body, ins, out_shapes, n_remote, in_place=False):
    scratch = [pltpu.SemaphoreType.DMA((n_remote,)), pltpu.SemaphoreType.DMA((n_remote,))]
    aliases = {i: i for i in range(len(ins))} if in_place else {}
    return pl.pallas_call(body, in_specs=[ANY] * len(ins), out_specs=[ANY] * len(out_shapes), out_shape=out_shapes,
                          scratch_shapes=scratch, input_output_aliases=aliases, name=name)(*ins)


def _peers(x, y, c):
    return [(1 - x if m & 4 else x, 1 - y if m & 2 else y, 1 - c if m & 1 else c) for m in range(1, 8)]


def _scatter_small(v3):
    def body(v_ref, o_ref, send, recv):
        x, y, c, _ = _place()
        cps = []
        for m, (px, py, pc) in enumerate(_peers(x, y, c)):
            cps.append(_remote(v_ref.at[4 * px + 2 * py + pc], o_ref.at[m], send.at[m], recv.at[m], (px, py, pc)))
            cps[-1].start()
        for cp in cps:
            cp.wait()

    return _exchange_call("scatter_small", body, [v3], [jax.ShapeDtypeStruct((7,) + v3.shape[1:], v3.dtype)], 7)[0]


def _gather_small(buf):
    def body(_, o_ref, send, recv):
        x, y, c, _ = _place()
        mine = o_ref.at[4 * x + 2 * y + c]
        peers = _peers(x, y, c)
        cps = []
        for m, to in enumerate(peers):
            cps.append(_remote(mine, mine, send.at[m], recv.at[m], to))
            cps[-1].start()
        for m, (px, py, pc) in enumerate(peers):
            slab = o_ref.at[4 * px + 2 * py + pc]
            _remote(slab, slab, send.at[m], recv.at[m], (px, py, pc)).wait_recv()
            cps[m].wait_send()

    return _exchange_call("gather_small", body, [buf], [jax.ShapeDtypeStruct(buf.shape, buf.dtype)], 7,
                          in_place=True)[0]


def _allreduce_small(v, pos):
    rows = v.shape[0]
    rs = rows // 8
    v3 = v.reshape(8, rs, LANES)
    got = _scatter_small(v3)
    tm = _tile(rs)
    ins = [(v3, (None, tm, LANES), lambda i, p: (p[2], i, 0))]
    ins += [(got, (None, tm, LANES), lambda i, p, m=m: (m, i, 0)) for m in range(7)]
    buf = _tiled("sum_small", lambda *t: (((((((t[0] + t[1]) + t[2]) + t[3]) + t[4]) + t[5]) + t[6]) + t[7],),
                 (rs // tm,), pos, ins, [((8, rs, LANES), F32, (None, tm, LANES), lambda i, p: (p[2], i, 0))])[0]
    return _gather_small(buf).reshape(rows, LANES)


def _add_pair(grads, got, pos):
    out = []
    for g, q in zip(grads, got):
        nb, R, C = g.shape
        rh = R // 2
        tm = _tile(rh)
        nh = rh // tm
        t = _tiled("rs_add_pair", lambda u, w: (u.astype(F32) + w.astype(F32),), (nb, nh), pos,
                   [(g.reshape(nb * R, C), (tm, C), lambda b, i, p: (b * 2 * nh + p[1] * nh + i, 0)),
                    (q.reshape(nb * rh, C), (tm, C), lambda b, i, p: (b * nh + i, 0))],
                   [((nb * rh, C), BF, (tm, C), lambda b, i, p: (b * nh + i, 0))])[0]
        out.append(t.reshape(nb, rh, C))
    return out


def _add_chips(parts, slots, reduced, l, pos):
    out = []
    for t, s, red in zip(parts, slots, reduced):
        nb, rh, C = t.shape
        L = red.shape[0]
        tm = _tile(rh)
        nh = rh // tm

        def add(own, s0, s1, s2):
            return (((own.astype(F32) + s0.astype(F32)) + s1.astype(F32)) + s2.astype(F32),)

        ins = [(t.reshape(nb * rh, C), (tm, C), lambda i, p: (p[0] * nh + i, 0))]
        ins += [(s, (None, tm, C), lambda i, p, j=j: (j, i, 0)) for j in range(3)]
        buf = _tiled("rs_add_chips", add, (nh,), pos, ins,
                     [((L * 2 * rh, C), F32, (tm, C), lambda i, p: (l * 2 * nh + p[1] * nh + i, 0))],
                     into=red.reshape(L * 2 * rh, C))[0]
        out.append(buf.reshape(L, 2 * rh, C))
    return out


def _adamw_math(w, g, m, v):
    m = ADAM_B1 * m + (1.0 - ADAM_B1) * g
    v = ADAM_B2 * v + (1.0 - ADAM_B2) * (g * g)
    m_hat = m / (1.0 - ADAM_B1 ** ADAM_STEP)
    v_hat = v / (1.0 - ADAM_B2 ** ADAM_STEP)
    return -ADAM_LR * (m_hat / (jnp.sqrt(v_hat) + ADAM_EPS) + ADAM_WD * w), m, v


def _adamw(w, g, m, v, lo=0, hi=None, into=None):
    L, R, C = w.shape
    hi = L if hi is None else hi
    tr = _tile(R, 256)
    extra = list(into) if into else []

    def body(w_ref, g_ref, m_ref, v_ref, *rest):
        go_ref, d_ref, mo_ref, vo_ref = rest[len(extra):]
        gv = g_ref[...]
        d, mn, vn = _adamw_math(w_ref[...], gv, m_ref[...], v_ref[...])
        go_ref[...] = gv
        d_ref[...] = d
        mo_ref[...] = mn
        vo_ref[...] = vn

    spec = pl.BlockSpec((None, tr, C), lambda l, i: (l + lo, i, 0))
    out = jax.ShapeDtypeStruct(w.shape, F32)
    return pl.pallas_call(body, grid=(hi - lo, R // tr), in_specs=[spec] * 4 + [ANY] * len(extra),
                          out_specs=[spec] * 4, out_shape=[out] * 4,
                          input_output_aliases={4 + k: k for k in range(len(extra))}, name="adamw",
                          compiler_params=_params(2))(w, g, m, v, *extra)


def _pack(parts):
    flat = jnp.concatenate([q.reshape(-1, LANES) for q in parts], axis=0)
    return jnp.pad(flat, ((0, -flat.shape[0] % ROW_TILE), (0, 0)))


def _unpack(flat, like):
    out, r = [], 0
    for q in like:
        n = q.size // LANES
        out.append(flat[r:r + n].reshape(q.shape))
        r += n
    return out


def _train_step(a):
    a = dict(a)
    L = a["ffn1_pre_g"].shape[0]
    x, y, c, _ = _place()
    chip = 2 * x + y
    pos = jnp.stack([chip, c, 2 * chip + c]).astype(jnp.int32)
    for name in TRANSPOSED:
        for pre in ("", "m_", "v_"):
            a[pre + name] = jnp.swapaxes(a[pre + name], 1, 2)
    big = [b[0] for b in BIG]
    gathered = big + ["conv_dw_k"]
    n_w, n_g = len(gathered), len(big)

    own = [_cast_layers("cast_weight", a[name], rp, cp, BF, pos) for name, _, _, _, rp, cp in BIG]
    own.append(_cast_layers("pad_conv_taps", a["conv_dw_k"].reshape(L, CONV_TAPS, LANES), CONV_PAD, LANES, F32, pos))
    across, to_sibling = _gather_plans(n_w)
    W = {name: [None] * L for name in gathered}

    def gather_first(l, deps):
        sems, bufs, token = _split_start("gather_a%d" % l, across, [own[i][l] for i in range(n_w)], deps)
        return sems, bufs, token

    def gather_second(l, state, after):
        bufs = _split_wait("gather_a%d_done" % l, across, state[0], state[1], after)
        return _split_start("gather_b%d" % l, to_sibling, bufs, [])

    def gather_done(l, state, after):
        for name, buf in zip(gathered, _split_wait("gather_b%d_done" % l, to_sibling, state[0], state[1], after)):
            W[name][l] = buf

    state = gather_first(0, [])
    state = gather_second(0, state, [])
    gather_done(0, state, [])

    small = [_layer_small(a, l) for l in range(L)]
    h, saved = a["x"][0], []
    for l in range(L):
        deps, mid, box = (), None, {}
        if l + 1 < L:
            box["state"] = gather_first(l + 1, [h])
            deps = (box["state"][2],)

            def mid(hm, l=l, box=box):
                box["state"] = gather_second(l + 1, box["state"], [hm])
                return (box["state"][2],)

        h, sv = _layer_fwd(l, h, a["p"][l, 0], small[l], W, deps, mid)
        saved.append(sv)
        if l + 1 < L:
            gather_done(l + 1, box["state"], [h])

    def loss_fn(yv, t):
        e = yv - t
        return e * (1.0 / D), jnp.sum(e * e, axis=0, keepdims=True)

    dh, lsum = _rowwise("loss", loss_fn, [(h, D, 0), (a["loss_target"][0], D, 0)], [], [(D, F32)], [(1, D)])
    loss = lax.psum(0.5 * jnp.sum(lsum) / D, ("x", "y", "c"))

    G = {name: [jax.ShapeDtypeStruct((N_CHIPS, rp, cp), BF)] * L for name, _, _, _, rp, cp in BIG}
    reduced = [lax.empty((L, rp, cp), F32) for _, _, _, _, rp, cp in BIG]
    pair, cross = _pair_plan(n_g), _cross_plan(n_g)
    small_grads = [None] * L

    def pair_start(l, deps):
        grads = [G[name][l] for name in big]
        lands = [lax.empty((N_CHIPS, g.shape[1] // 2, g.shape[2]), BF) for g in grads]
        return _split_start("rs_pair%d" % l, pair, grads + lands, deps)

    def cross_start(l, state, after):
        bufs = _split_wait("rs_pair%d_done" % l, pair, state[0], state[1], after)
        parts = _add_pair(bufs[:n_g], bufs[n_g:], pos)
        lands = [lax.empty((3,) + t.shape[1:], BF) for t in parts]
        return _split_start("rs_cross%d" % l, cross, parts + lands, [])

    def share_start(l, state, after, reduced):
        bufs = _split_wait("rs_cross%d_done" % l, cross, state[0], state[1], after)
        reduced = _add_chips(bufs[:n_g], bufs[n_g:], reduced, l, pos)
        return _split_start("rs_share%d" % l, _share_plan(n_g, l), reduced, [])

    def share_done(l, state, after):
        return _split_wait("rs_share%d_done" % l, _share_plan(n_g, l), state[0], state[1], after)

    st_pair = st_cross = st_share = None
    for l in reversed(range(L)):
        deps = tuple(s[2] for s in (st_pair, st_share) if s is not None)
        box = {"cross": None}

        def mid(dm, l=l, box=box, st_pair=st_pair, st_share=st_share):
            out = []
            if st_share is not None:
                box["reduced"] = share_done(l + 2, st_share, [dm])
            if st_pair is not None:
                box["cross"] = cross_start(l + 1, st_pair, [dm])
                out.append(box["cross"][2])
            return tuple(out)

        dh, small_grads[l] = _layer_bwd(l, dh, saved[l], small[l], W, G, deps, mid)
        if st_share is not None:
            reduced = box["reduced"]
        st_share = share_start(l + 1, box["cross"], [dh], reduced) if box["cross"] is not None else None
        st_pair = pair_start(l, [dh])
    grad_x = dh
    if st_share is not None:
        reduced = share_done(1, st_share, [])
    st_cross = cross_start(0, st_pair, [])
    small_names = SMALL + ("conv_dw_k",)
    stacked = [jnp.stack([small_grads[l][name] for l in range(L)]) for name in small_names]
    summed = dict(zip(small_names, _unpack(_allreduce_small(_pack(stacked), pos), stacked)))
    upper = {}
    if L > 1:
        for name, red in zip(big, reduced):
            upper[name] = _adamw(a[name], red, a["m_" + name], a["v_" + name], 1, L)
    st_share = share_start(0, st_cross, [r[1] for r in upper.values()] + [summed[small_names[0]]], reduced)
    reduced = share_done(0, st_share, [])
    big_grads = dict(zip(big, reduced))

    grads, deltas, new_m, new_v = {}, {}, {}, {}
    for name in big:
        res = _adamw(a[name], big_grads[name], a["m_" + name], a["v_" + name], 0, 1, upper.get(name))
        if name in TRANSPOSED:
            res = [jnp.swapaxes(r, 1, 2) for r in res]
        grads[name], deltas[name], new_m[name], new_v[name] = res
    taps = lax.dynamic_slice_in_dim(summed["conv_dw_k"], chip * LANES, LANES, axis=2)[:, :CONV_TAPS]
    grads["conv_dw_k"] = taps.reshape(a["conv_dw_k"].shape)
    for name in SMALL:
        grads[name] = summed[name].reshape(a[name].shape)
    shapes = [a[name] for name in small_names]
    res = _adamw(*[_pack([a[pre + name] if pre != "g" else grads[name] for name in small_names])[None]
                   for pre in ("", "g", "m_", "v_")])
    for dst, flat in zip((deltas, new_m, new_v), res[1:]):
        for name, val in zip(small_names, _unpack(flat[0], shapes)):
            dst[name] = val

    return (loss, grad_x[None], *[grads[n] for n in WEIGHTS], *[deltas[n] for n in WEIGHTS],
            *[new_m[n] for n in WEIGHTS], *[new_v[n] for n in WEIGHTS])


def kernel(x, p, ffn1_pre_g, ffn1_w_gate, ffn1_w_up, ffn1_w_down, ffn1_post_g, mix_pre_g, w_in, pool_w, pool_scale, w_pool_out, sgu_ln_g, sgu_ln_b, sgu_w_s, sgu_b_s, w_sgu_out, conv_dw_k, conv_dw_b, conv_ln_g, conv_ln_b, w_conv_out, w_out, mix_post_g, ffn2_pre_g, ffn2_w_gate, ffn2_w_up, ffn2_w_down, ffn2_post_g, ple_w_proj, ple_pre_g, ple_w_gate, ple_post_g, loss_target, m_ffn1_pre_g, m_ffn1_w_gate, m_ffn1_w_up, m_ffn1_w_down, m_ffn1_post_g, m_mix_pre_g, m_w_in, m_pool_w, m_pool_scale, m_w_pool_out, m_sgu_ln_g, m_sgu_ln_b, m_sgu_w_s, m_sgu_b_s, m_w_sgu_out, m_conv_dw_k, m_conv_dw_b, m_conv_ln_g, m_conv_ln_b, m_w_conv_out, m_w_out, m_mix_post_g, m_ffn2_pre_g, m_ffn2_w_gate, m_ffn2_w_up, m_ffn2_w_down, m_ffn2_post_g, m_ple_w_proj, m_ple_pre_g, m_ple_w_gate, m_ple_post_g, v_ffn1_pre_g, v_ffn1_w_gate, v_ffn1_w_up, v_ffn1_w_down, v_ffn1_post_g, v_mix_pre_g, v_w_in, v_pool_w, v_pool_scale, v_w_pool_out, v_sgu_ln_g, v_sgu_ln_b, v_sgu_w_s, v_sgu_b_s, v_w_sgu_out, v_conv_dw_k, v_conv_dw_b, v_conv_ln_g, v_conv_ln_b, v_w_conv_out, v_w_out, v_mix_post_g, v_ffn2_pre_g, v_ffn2_w_gate, v_ffn2_w_up, v_ffn2_w_down, v_ffn2_post_g, v_ple_w_proj, v_ple_pre_g, v_ple_w_gate, v_ple_post_g):
    return _train_step(dict(locals()))
```

```python
import math

import jax
import jax.numpy as jnp
from jax import lax
from jax.experimental import pallas as pl
from jax.experimental.pallas import tpu as pltpu

BF = jnp.bfloat16
F32 = jnp.float32
EPS = 1e-6
D_MODEL = 1024
LANES = 128
N_CHIPS = 4
FFN_SHARD = 704
FFN_SHARD_PAD = 768
POOL_WINDOWS = (2, 4, 8, 16)
SGU_HEADS = 4
CHUNK = 128
CONV_TAPS = 31
CONV_PAD = 32
ROW_TILE = 512
VMEM_LIMIT_BYTES = 56 * 1024 * 1024
ADAM_LR, ADAM_B1, ADAM_B2, ADAM_EPS, ADAM_WD, ADAM_STEP = 0.001, 0.9, 0.999, 1e-08, 0.01, 10
MESH = pl.DeviceIdType.MESH
ANY = pl.BlockSpec(memory_space=pl.ANY)

ZB_POOL, ZB_U, ZB_V, ZB_A, ZB_B, ZB_GATES = 0, 1, 2, 3, 4, 5

TRANSPOSED = ("ffn1_w_gate", "ffn1_w_up", "ffn2_w_gate", "ffn2_w_up")
BIG = (
    ("ffn1_w_gate", "row", FFN_SHARD, 1024, FFN_SHARD_PAD, 1024),
    ("ffn1_w_up", "row", FFN_SHARD, 1024, FFN_SHARD_PAD, 1024),
    ("ffn1_w_down", "row", FFN_SHARD, 1024, FFN_SHARD_PAD, 1024),
    ("w_in", "col", 1024, 1408, 1024, 1408),
    ("w_pool_out", "col", 512, 256, 512, 256),
    ("w_sgu_out", "col", 512, 256, 512, 256),
    ("w_conv_out", "col", 512, 256, 512, 256),
    ("w_out", "row", 256, 1024, 256, 1024),
    ("ffn2_w_gate", "row", FFN_SHARD, 1024, FFN_SHARD_PAD, 1024),
    ("ffn2_w_up", "row", FFN_SHARD, 1024, FFN_SHARD_PAD, 1024),
    ("ffn2_w_down", "row", FFN_SHARD, 1024, FFN_SHARD_PAD, 1024),
    ("ple_w_proj", "col", 256, 256, 256, 256),
    ("ple_w_gate", "row", 256, 1024, 256, 1024),
)
SMALL = ("ffn1_pre_g", "ffn1_post_g", "mix_pre_g", "pool_w", "pool_scale", "sgu_ln_g", "sgu_ln_b", "sgu_w_s",
         "sgu_b_s", "conv_dw_b", "conv_ln_g", "conv_ln_b", "mix_post_g", "ffn2_pre_g", "ffn2_post_g",
         "ple_pre_g", "ple_post_g")
WEIGHTS = ("ffn1_pre_g", "ffn1_w_gate", "ffn1_w_up", "ffn1_w_down", "ffn1_post_g", "mix_pre_g", "w_in", "pool_w",
           "pool_scale", "w_pool_out", "sgu_ln_g", "sgu_ln_b", "sgu_w_s", "sgu_b_s", "w_sgu_out", "conv_dw_k",
           "conv_dw_b", "conv_ln_g", "conv_ln_b", "w_conv_out", "w_out", "mix_post_g", "ffn2_pre_g", "ffn2_w_gate",
           "ffn2_w_up", "ffn2_w_down", "ffn2_post_g", "ple_w_proj", "ple_pre_g", "ple_w_gate", "ple_post_g")


def _params(n_grid):
    return pltpu.CompilerParams(dimension_semantics=("arbitrary",) * n_grid, vmem_limit_bytes=VMEM_LIMIT_BYTES)


def _tile(n, cap=ROW_TILE):
    for t in range(min(cap, n) - min(cap, n) % 16, 0, -16):
        if n % t == 0:
            return t
    return n


def _sigmoid(x):
    return 1.0 / (1.0 + jnp.exp(-x))


def _silu_and_grad(x):
    s = _sigmoid(x)
    return x * s, s * (1.0 + x * (1.0 - s))


def _gelu_and_grad(x):
    cdf = 0.5 * (1.0 + lax.erf(x * (1.0 / math.sqrt(2.0))))
    pdf = jnp.exp(-0.5 * x * x) * (1.0 / math.sqrt(2.0 * math.pi))
    return x * cdf, cdf + x * pdf


def _rms_fwd(x, g):
    return x * lax.rsqrt(jnp.mean(x * x, axis=-1, keepdims=True) + EPS) * g


def _rms_bwd(x, g, dy):
    r = lax.rsqrt(jnp.mean(x * x, axis=-1, keepdims=True) + EPS)
    xh = x * r
    dxh = dy * g
    dx = r * (dxh - xh * jnp.mean(dxh * xh, axis=-1, keepdims=True))
    return dx, jnp.sum(dy * xh, axis=0, keepdims=True)


def _ln_stats(x):
    xc = x - jnp.mean(x, axis=-1, keepdims=True)
    r = lax.rsqrt(jnp.mean(xc * xc, axis=-1, keepdims=True) + EPS)
    return xc * r, r


def _ln_bwd(xh, r, g, dy):
    dxh = dy * g
    dx = r * (dxh - jnp.mean(dxh, axis=-1, keepdims=True) - xh * jnp.mean(dxh * xh, axis=-1, keepdims=True))
    return dx, jnp.sum(dy * xh, axis=0, keepdims=True), jnp.sum(dy, axis=0, keepdims=True)


def _rowwise(name, fn, rows, consts, outs, accs=(), tm=ROW_TILE, deps=()):
    T = rows[0][0].shape[-2]
    tm = _tile(T, tm)
    n_in, n_o, n_dep = len(rows) + len(consts), len(outs), len(deps)

    def body(*refs):
        refs = refs[n_dep:]
        res = fn(*[r[...] for r in refs[:n_in]])
        for ref, val in zip(refs[n_in:n_in + n_o], res[:n_o]):
            ref[...] = val.astype(ref.dtype)
        acc_refs = refs[n_in + n_o:]
        if acc_refs:
            @pl.when(pl.program_id(0) == 0)
            def _():
                for ref, val in zip(acc_refs, res[n_o:]):
                    ref[...] = val

            @pl.when(pl.program_id(0) != 0)
            def _():
                for ref, val in zip(acc_refs, res[n_o:]):
                    ref[...] += val

    in_specs = [ANY] * n_dep
    for row in rows:
        w, cb = row[1], row[2]
        if len(row) == 4:
            in_specs.append(pl.BlockSpec((None, tm, w), lambda i, cb=cb, ld=row[3]: (ld, i, cb)))
        else:
            in_specs.append(pl.BlockSpec((tm, w), lambda i, cb=cb: (i, cb)))
    in_specs += [pl.BlockSpec(c.shape, lambda i: (0, 0)) for c in consts]
    out_specs = [pl.BlockSpec((tm, w), lambda i: (i, 0)) for w, _ in outs]
    out_specs += [pl.BlockSpec(s, lambda i: (0, 0)) for s in accs]
    out_shape = [jax.ShapeDtypeStruct((T, w), dt) for w, dt in outs]
    out_shape += [jax.ShapeDtypeStruct(s, F32) for s in accs]
    return pl.pallas_call(body, grid=(T // tm,), in_specs=in_specs, out_specs=out_specs, out_shape=out_shape,
                          name=name, compiler_params=_params(1))(*deps, *[r[0] for r in rows], *consts)


def _tiled(name, fn, grid, pos, ins, outs):
    n_in = len(ins)

    def body(_, *refs):
        res = fn(*[r[...] for r in refs[:n_in]])
        for ref, val in zip(refs[n_in:], res):
            ref[...] = val.astype(ref.dtype)

    spec = pltpu.PrefetchScalarGridSpec(
        num_scalar_prefetch=1, grid=grid, in_specs=[pl.BlockSpec(bs, im) for _, bs, im in ins],
        out_specs=[pl.BlockSpec(bs, im) for _, _, bs, im in outs])
    return pl.pallas_call(body, grid_spec=spec, out_shape=[jax.ShapeDtypeStruct(s, d) for s, d, _, _ in outs],
                          name=name, compiler_params=_params(len(grid)))(pos, *[a for a, _, _ in ins])


def _cast_layers(name, w, rp, cp, dtype, pos):
    L, r, c = w.shape

    def body(_, w_ref, *o_refs):
        for k, o_ref in enumerate(o_refs):
            @pl.when(pl.program_id(0) == k)
            def _(o_ref=o_ref):
                if (rp, cp) != (r, c):
                    o_ref[...] = jnp.zeros_like(o_ref)
                    o_ref[pl.ds(0, r), pl.ds(0, c)] = w_ref[...].astype(dtype)
                else:
                    o_ref[...] = w_ref[...].astype(dtype)

    spec = pltpu.PrefetchScalarGridSpec(
        num_scalar_prefetch=1, grid=(L,), in_specs=[pl.BlockSpec((None, r, c), lambda l, p: (l, 0, 0))],
        out_specs=[pl.BlockSpec((None, rp, cp), lambda l, p: (p[0], 0, 0))] * L)
    return pl.pallas_call(body, grid_spec=spec, out_shape=[jax.ShapeDtypeStruct((N_CHIPS, rp, cp), dtype)] * L,
                          name=name, compiler_params=_params(1))(pos, w)


_NN = (((1,), (0,)), ((), ()))
_NT = (((1,), (1,)), ((), ()))
_TN = (((0,), (0,)), ((), ()))


def _mm_blockout(name, x, w4, l, trans_w, tm=ROW_TILE):
    M, kx = x.shape
    w3 = w4[l]
    nb, r, cc = w3.shape
    bo = r if trans_w else cc
    tm = _tile(M, tm)

    def body(x_ref, w_ref, o_ref):
        o_ref[...] = lax.dot_general(x_ref[...].astype(BF), w_ref[...].astype(BF), _NT if trans_w else _NN,
                                     preferred_element_type=F32).astype(o_ref.dtype)

    return pl.pallas_call(
        body, grid=(nb, M // tm),
        in_specs=[pl.BlockSpec((tm, kx), lambda b, i: (i, 0)),
                  pl.BlockSpec((None, r, cc), lambda b, i: (b, 0, 0))],
        out_specs=pl.BlockSpec((tm, bo), lambda b, i: (i, b)),
        out_shape=jax.ShapeDtypeStruct((M, nb * bo), BF), name=name, compiler_params=_params(2))(x, w3)


def _mm_tn(name, a, dy, buf, l, a_blocked, tk=ROW_TILE):
    T = a.shape[0]
    nb, R, C = buf[l].shape

    def body(a_ref, dy_ref, o_ref):
        o_ref[...] = lax.dot_general(a_ref[...].astype(BF), dy_ref[...].astype(BF), _TN,
                                     preferred_element_type=F32).astype(o_ref.dtype)

    if a_blocked:
        grid = (nb,)
        in_specs = [pl.BlockSpec((T, R), lambda b: (0, b)), pl.BlockSpec((T, C), lambda b: (0, 0))]
        out_specs = pl.BlockSpec((None, R, C), lambda b: (b, 0, 0))
    else:
        tk = min(tk, R)
        grid = (nb, R // tk)
        in_specs = [pl.BlockSpec((T, tk), lambda b, k: (0, k)), pl.BlockSpec((T, C), lambda b, k: (0, b))]
        out_specs = pl.BlockSpec((None, tk, C), lambda b, k: (b, k, 0))
    buf = list(buf)
    buf[l] = pl.pallas_call(body, grid=grid, in_specs=in_specs, out_specs=out_specs,
                            out_shape=jax.ShapeDtypeStruct((nb, R, C), BF), name=name,
                            compiler_params=_params(len(grid)))(a, dy)
    return buf


def _acc_rows(ref, val, first):
    @pl.when(first)
    def _():
        ref[...] = val

    @pl.when(jnp.logical_not(first))
    def _():
        ref[...] += val


def _norm_mm(name, h, g, ws, trans_w, act=False, deps=(), tm=2 * ROW_TILE):
    T = h.shape[0]
    nb, r, cc = ws[0].shape
    bo = r if trans_w else cc
    tm = _tile(T, tm)
    n_w, n_dep = len(ws), len(deps)

    def body(*refs):
        refs = refs[n_dep:]
        h_ref, g_ref, w_refs = refs[0], refs[1], refs[2:2 + n_w]
        n_ref, o_refs, n_s = refs[2 + n_w], refs[3 + n_w:3 + 2 * n_w], refs[-1]

        @pl.when(pl.program_id(1) == 0)
        def _():
            n = _rms_fwd(h_ref[...].astype(F32), g_ref[...]).astype(BF)
            n_s[...] = n
            n_ref[...] = n

        n = n_s[...]
        prods = []
        for w_ref, o_ref in zip(w_refs, o_refs):
            prods.append(lax.dot_general(n, w_ref[...], _NT if trans_w else _NN,
                                         preferred_element_type=F32).astype(BF))
            o_ref[...] = prods[-1]
        if act:
            refs[3 + 2 * n_w][...] = (_silu_and_grad(prods[0].astype(F32))[0] * prods[1].astype(F32)).astype(BF)

    wide = pl.BlockSpec((tm, bo), lambda i, b: (i, b))
    n_out = n_w + (1 if act else 0)
    return pl.pallas_call(
        body, grid=(T // tm, nb),
        in_specs=[ANY] * n_dep + [pl.BlockSpec((tm, D_MODEL), lambda i, b: (i, 0)),
                                  pl.BlockSpec(g.shape, lambda i, b: (0, 0))]
        + [pl.BlockSpec((None, r, cc), lambda i, b: (b, 0, 0))] * n_w,
        out_specs=[pl.BlockSpec((tm, D_MODEL), lambda i, b: (i, 0))] + [wide] * n_out,
        out_shape=[jax.ShapeDtypeStruct((T, D_MODEL), BF)] + [jax.ShapeDtypeStruct((T, nb * bo), BF)] * n_out,
        scratch_shapes=[pltpu.VMEM((tm, D_MODEL), BF)], name=name, compiler_params=_params(2))(*deps, h, g, *ws)


def _mm_res(name, x, w3, h, g, coef, tm=ROW_TILE):
    T, kx = x.shape
    w2 = w3.reshape(kx, D_MODEL)
    tm = _tile(T, tm)

    def body(x_ref, w_ref, h_ref, g_ref, f_ref, o_ref):
        f = jnp.dot(x_ref[...], w_ref[...], preferred_element_type=F32).astype(BF)
        f_ref[...] = f
        o_ref[...] = h_ref[...] + coef * _rms_fwd(f.astype(F32), g_ref[...])

    row = pl.BlockSpec((tm, D_MODEL), lambda i: (i, 0))
    return pl.pallas_call(
        body, grid=(T // tm,),
        in_specs=[pl.BlockSpec((tm, kx), lambda i: (i, 0)), pl.BlockSpec(w2.shape, lambda i: (0, 0)), row,
                  pl.BlockSpec(g.shape, lambda i: (0, 0))],
        out_specs=[row, row],
        out_shape=[jax.ShapeDtypeStruct((T, D_MODEL), BF), jax.ShapeDtypeStruct((T, D_MODEL), F32)],
        name=name, compiler_params=_params(1))(x, w2, h, g)


def _resbwd_mm(name, dh, f, g, coef, w3, trans_w, act=None, deps=(), tm=ROW_TILE):
    T = dh.shape[0]
    nb, r, cc = w3.shape
    bo = r if trans_w else cc
    tm = _tile(T, tm)
    n_dep, n_act = len(deps), 2 if act else 0

    def body(*refs):
        refs = refs[n_dep:]
        dh_ref, f_ref, g_ref, w_ref = refs[:4]
        df_ref, dg_ref = refs[4 + n_act], refs[5 + n_act]
        df_s = refs[-1]
        i, b = pl.program_id(0), pl.program_id(1)

        @pl.when(b == 0)
        def _():
            dx, dg = _rms_bwd(f_ref[...].astype(F32), g_ref[...], coef * dh_ref[...])
            df_s[...] = dx.astype(BF)
            df_ref[...] = dx.astype(BF)
            _acc_rows(dg_ref, dg, i == 0)

        prod =lax.dot_general(df_s[...], w_ref[...], _NT if trans_w else _NN, preferred_element_type=F32)
        if act:
            gate, up = refs[4][...].astype(F32), refs[5][...].astype(F32)
            val, grad = _silu_and_grad(gate)
            refs[6 + n_act][...] = (prod * up * grad).astype(BF)
            refs[7 + n_act][...] = (prod * val).astype(BF)
        else:
            refs[6][...] = prod.astype(BF)

    row = pl.BlockSpec((tm, D_MODEL), lambda i, b: (i, 0))
    wide = pl.BlockSpec((tm, bo), lambda i, b: (i, b))
    vec = pl.BlockSpec((1, D_MODEL), lambda i, b: (0, 0))
    n_prod = 2 if act else 1
    return pl.pallas_call(
        body, grid=(T // tm, nb),
        in_specs=[ANY] * n_dep + [row, row, vec, pl.BlockSpec((None, r, cc), lambda i, b: (b, 0, 0))] + [wide] * n_act,
        out_specs=[row, vec] + [wide] * n_prod,
        out_shape=[jax.ShapeDtypeStruct((T, D_MODEL), BF), jax.ShapeDtypeStruct((1, D_MODEL), F32)]
        + [jax.ShapeDtypeStruct((T, nb * bo), BF)] * n_prod,
        scratch_shapes=[pltpu.VMEM((tm, D_MODEL), BF)], name=name,
        compiler_params=_params(2))(*deps, dh, f, g, w3, *(act or ()))


def _dn_prenorm(name, pairs, trans_w, dh, h, g, tm=ROW_TILE):
    T = dh.shape[0]
    nb, r, cc = pairs[0][1].shape
    bw = cc if trans_w else r
    tm = _tile(T, tm)
    n_p = len(pairs)

    def body(*refs):
        x_refs, w_refs = refs[:n_p], refs[n_p:2 * n_p]
        dh_ref, h_ref, g_ref, o_ref, dg_ref, acc = refs[2 * n_p:]
        i, b = pl.program_id(0), pl.program_id(1)
        t = None
        for x_ref, w_ref in zip(x_refs, w_refs):
            p = lax.dot_general(x_ref[...], w_ref[...], _NT if trans_w else _NN, preferred_element_type=F32)
            t = p if t is None else t + p

        @pl.when(b == 0)
        def _():
            acc[...] = t

        @pl.when(b != 0)
        def _():
            acc[...] += t

        @pl.when(b == nb - 1)
        def _():
            dx, dg = _rms_bwd(h_ref[...], g_ref[...], acc[...])
            o_ref[...] = dh_ref[...] + dx
            _acc_rows(dg_ref, dg, i == 0)

    row = pl.BlockSpec((tm, D_MODEL), lambda i, b: (i, 0))
    vec = pl.BlockSpec((1, D_MODEL), lambda i, b: (0, 0))
    return pl.pallas_call(
        body, grid=(T // tm, nb),
        in_specs=[pl.BlockSpec((tm, bw), lambda i, b: (i, b))] * n_p
        + [pl.BlockSpec((None, r, cc), lambda i, b: (b, 0, 0))] * n_p + [row, row, vec],
        out_specs=[row, vec],
        out_shape=[jax.ShapeDtypeStruct((T, D_MODEL), F32), jax.ShapeDtypeStruct((1, D_MODEL), F32)],
        scratch_shapes=[pltpu.VMEM((tm, D_MODEL), F32)], name=name,
        compiler_params=_params(2))(*[p[0] for p in pairs], *[p[1] for p in pairs], dh, h, g)


def _pool_apply(x, win, row):
    s, k = x, 1
    while k < win:
        s = s + jnp.where(row >= k, pltpu.roll(s, k, 0), 0.0)
        k *= 2
    return s / jnp.minimum(row + 1, win).astype(F32) - x


def _pool_apply_t(dp, win, row):
    T = dp.shape[0]
    s, k = dp / jnp.minimum(row + 1, win).astype(F32), 1
    while k < win:
        s = s + jnp.where(row < T - k, pltpu.roll(s, T - k, 0), 0.0)
        k *= 2
    return s - dp


def _pool_fwd(z, w, scale):
    T = z.shape[0]

    def body(z_ref, w_ref, s_ref, o_ref):
        row = lax.broadcasted_iota(jnp.int32, (T, LANES), 0)
        for gi, win in enumerate(POOL_WINDOWS):
            cols = pl.ds(gi * LANES, LANES)
            pooled = _pool_apply(z_ref[:, cols].astype(F32), win, row)
            y = jnp.dot(pooled.astype(BF), w_ref[gi].astype(BF), preferred_element_type=F32)
            o_ref[:, cols] = (y * s_ref[:, cols]).astype(o_ref.dtype)

    return pl.pallas_call(
        body, grid=(1,),
        in_specs=[pl.BlockSpec((T, 512), lambda i: (0, ZB_POOL)), pl.BlockSpec(w.shape, lambda i: (0, 0, 0)),
                  pl.BlockSpec(scale.shape, lambda i: (0, 0))],
        out_specs=pl.BlockSpec((T, 512), lambda i: (0, 0)), out_shape=jax.ShapeDtypeStruct((T, 512), BF),
        name="pool_fwd", compiler_params=_params(1))(z, w, scale)


def _pool_bwd(dr, z, w, scale):
    T = z.shape[0]

    def body(dr_ref, z_ref, w_ref, s_ref, dz_ref, dw_ref, ds_ref):
        row = lax.broadcasted_iota(jnp.int32, (T, LANES), 0)
        for gi, win in enumerate(POOL_WINDOWS):
            cols = pl.ds(gi * LANES, LANES)
            pooled = _pool_apply(z_ref[:, cols].astype(F32), win, row).astype(BF)
            wg = w_ref[gi].astype(BF)
            y = jnp.dot(pooled, wg, preferred_element_type=F32)
            d = dr_ref[:, cols].astype(F32)
            ds_ref[:, cols] = jnp.sum(d * y, axis=0, keepdims=True)
            dy = (d * s_ref[:, cols]).astype(BF)
            dw_ref[gi] = lax.dot_general(pooled, dy, _TN, preferred_element_type=F32)
            dpooled = lax.dot_general(dy, wg, _NT, preferred_element_type=F32)
            dz_ref[:, cols] = _pool_apply_t(dpooled, win, row).astype(dz_ref.dtype)

    return pl.pallas_call(
        body, grid=(1,),
        in_specs=[pl.BlockSpec((T, 512), lambda i: (0, 0)), pl.BlockSpec((T, 512), lambda i: (0, ZB_POOL)),
                  pl.BlockSpec(w.shape, lambda i: (0, 0, 0)), pl.BlockSpec(scale.shape, lambda i: (0, 0))],
        out_specs=[pl.BlockSpec((T, 512), lambda i: (0, 0)), pl.BlockSpec(w.shape, lambda i: (0, 0, 0)),
                   pl.BlockSpec(scale.shape, lambda i: (0, 0))],
        out_shape=[jax.ShapeDtypeStruct((T, 512), BF), jax.ShapeDtypeStruct(w.shape, F32),
                   jax.ShapeDtypeStruct(scale.shape, F32)],
        name="pool_bwd", compiler_params=_params(1))(dr, z, w, scale)


def _tril(transposed=False):
    r = lax.broadcasted_iota(jnp.int32, (CHUNK, CHUNK), 0)
    c = lax.broadcasted_iota(jnp.int32, (CHUNK, CHUNK), 1)
    return c >= r if transposed else r >= c


def _sgu_fwd(z, ln_g, ln_b, w_s, bias):
    T = z.shape[0]
    tm = _tile(T)

    def body(zu_ref, zv_ref, g_ref, b_ref, w_ref, bias_ref, o_ref):
        gu, _ = _gelu_and_grad(zu_ref[...].astype(F32))
        gv, _ = _gelu_and_grad(zv_ref[...].astype(F32))
        xh, _ = _ln_stats(gv)
        v16 = (xh * g_ref[...] + b_ref[...]).astype(BF)
        tri = _tril()
        for h in range(SGU_HEADS):
            cols = slice(h * LANES, (h + 1) * LANES)
            wh = jnp.where(tri, w_ref[h], 0.0).astype(BF)
            for c in range(tm // CHUNK):
                rows = slice(c * CHUNK, (c + 1) * CHUNK)
                s = jnp.dot(wh, v16[rows, cols], preferred_element_type=F32) + bias_ref[:, cols]
                o_ref[rows, cols] = (gu[rows, cols] * s).astype(o_ref.dtype)

    small = [pl.BlockSpec(a.shape, lambda i, n=a.ndim: (0,) * n) for a in (ln_g, ln_b, w_s, bias)]
    return pl.pallas_call(
        body, grid=(T // tm,),
        in_specs=[pl.BlockSpec((tm, 512), lambda i: (i, ZB_U)), pl.BlockSpec((tm, 512), lambda i: (i, ZB_V))] + small,
        out_specs=pl.BlockSpec((tm, 512), lambda i: (i, 0)), out_shape=jax.ShapeDtypeStruct((T, 512), BF),
        name="sgu_fwd", compiler_params=_params(1))(z, z, ln_g, ln_b, w_s, bias)


def _sgu_bwd(dr, z, ln_g, ln_b, w_s, w_st, bias):
    T = z.shape[0]
    tm = _tile(T)
    n_steps = T // tm

    def body(dr_ref, zu_ref, zv_ref, g_ref, b_ref, w_ref, wt_ref, bias_ref,
             dzu_ref, dzv_ref, dg_ref, db_ref, dw_ref, dbias_ref, dgu_s, dv_s):
        i = pl.program_id(0)

        @pl.when(i == 0)
        def _():
            dg_ref[...] = jnp.zeros_like(dg_ref)
            db_ref[...] = jnp.zeros_like(db_ref)
            dw_ref[...] = jnp.zeros_like(dw_ref)
            dbias_ref[...] = jnp.zeros_like(dbias_ref)

        zu = zu_ref[...].astype(F32)
        zv = zv_ref[...].astype(F32)
        gu, gu_grad = _gelu_and_grad(zu)
        gv, gv_grad = _gelu_and_grad(zv)
        xh, r = _ln_stats(gv)
        v16 = (xh * g_ref[...] + b_ref[...]).astype(BF)
        dr = dr_ref[...].astype(F32)
        tri = _tril()
        for h in range(SGU_HEADS):
            cols = slice(h * LANES, (h + 1) * LANES)
            wh = jnp.where(tri, w_ref[h], 0.0).astype(BF)
            wht = jnp.where(_tril(transposed=True), wt_ref[h], 0.0).astype(BF)
            for c in range(tm // CHUNK):
                rows = slice(c * CHUNK, (c + 1) * CHUNK)
                v_blk = v16[rows, cols]
                s = jnp.dot(wh, v_blk, preferred_element_type=F32) + bias_ref[:, cols]
                ds = dr[rows, cols] * gu[rows, cols]
                dgu_s[rows, cols] = dr[rows, cols] * s
                ds16 = ds.astype(BF)
                dw_ref[h] += jnp.where(tri, lax.dot_general(ds16, v_blk, _NT, preferred_element_type=F32), 0.0)
                dv_s[rows, cols] = jnp.dot(wht, ds16, preferred_element_type=F32)
                dbias_ref[:, cols] += ds
        dzu_ref[...] = (dgu_s[...] * gu_grad).astype(dzu_ref.dtype)
        dgv, dg, db = _ln_bwd(xh, r, g_ref[...], dv_s[...])
        dzv_ref[...] = (dgv * gv_grad).astype(dzv_ref.dtype)
        dg_ref[...] += dg
        db_ref[...] += db

        @pl.when(i == n_steps - 1)
        def _():
            for h in range(SGU_HEADS):
                cols = slice(h * LANES, (h + 1) * LANES)
                tot = jnp.sum(dbias_ref[:, cols], axis=1, keepdims=True)
                dbias_ref[:, cols] = jnp.broadcast_to(tot, (CHUNK, LANES))

    small = (ln_g, ln_b, w_s, w_st, bias)
    small_specs = [pl.BlockSpec(a.shape, lambda i, n=a.ndim: (0,) * n) for a in small]
    return pl.pallas_call(
        body, grid=(n_steps,),
        in_specs=[pl.BlockSpec((tm, 512), lambda i: (i, 0)), pl.BlockSpec((tm, 512), lambda i: (i, ZB_U)),
                  pl.BlockSpec((tm, 512), lambda i: (i, ZB_V))] + small_specs,
        out_specs=[pl.BlockSpec((tm, 512), lambda i: (i, 0)), pl.BlockSpec((tm, 512), lambda i: (i, 0)),
                   pl.BlockSpec((1, 512), lambda i: (0, 0)), pl.BlockSpec((1, 512), lambda i: (0, 0)),
                   pl.BlockSpec(w_s.shape, lambda i: (0, 0, 0)), pl.BlockSpec(bias.shape, lambda i: (0, 0))],
        out_shape=[jax.ShapeDtypeStruct((T, 512), BF), jax.ShapeDtypeStruct((T, 512), BF),
                   jax.ShapeDtypeStruct((1, 512), F32), jax.ShapeDtypeStruct((1, 512), F32),
                   jax.ShapeDtypeStruct(w_s.shape, F32), jax.ShapeDtypeStruct(bias.shape, F32)],
        scratch_shapes=[pltpu.VMEM((tm, 512), F32), pltpu.VMEM((tm, 512), F32)],
        name="sgu_bwd", compiler_params=_params(1))(dr, z, z, ln_g, ln_b, w_s, w_st, bias)


def _conv_fwd(z, convk, l, bias):
    T = z.shape[0]

    def body(za_ref, zb_ref, k_ref, b_ref, o_ref):
        xg = za_ref[...].astype(F32) * _sigmoid(zb_ref[...].astype(F32))
        xp = jnp.concatenate([jnp.zeros((CONV_PAD, LANES), F32), xg], axis=0)
        kw = k_ref[...]
        acc = jnp.broadcast_to(b_ref[...], (T, LANES))
        for k in range(CONV_TAPS):
            sh = CONV_TAPS - 1 - k
            tap = xp if sh == 0 else pltpu.roll(xp, sh, 0)
            acc = acc + kw[k:k + 1, :] * tap[CONV_PAD:, :]
        o_ref[...] = acc.astype(o_ref.dtype)

    return pl.pallas_call(
        body, grid=(4,),
        in_specs=[pl.BlockSpec((T, LANES), lambda g: (0, 4 * ZB_A + g)),
                  pl.BlockSpec((T, LANES), lambda g: (0, 4 * ZB_B + g)),
                  pl.BlockSpec((None, CONV_PAD, LANES), lambda g: (g, 0, 0)),
                  pl.BlockSpec((1, LANES), lambda g: (0, g))],
        out_specs=pl.BlockSpec((T, LANES), lambda g: (0, g)), out_shape=jax.ShapeDtypeStruct((T, 512), BF),
        name="conv_fwd", compiler_params=_params(1))(z, z, convk[l], bias)


def _conv_bwd(dy, z, convk, l):
    T = z.shape[0]

    def body(dy_ref, za_ref, zb_ref, k_ref, dza_ref, dzb_ref, dk_ref, db_ref):
        a = za_ref[...].astype(F32)
        sg = _sigmoid(zb_ref[...].astype(F32))
        d = dy_ref[...].astype(F32)
        kw = k_ref[...]
        xp = jnp.concatenate([jnp.zeros((CONV_PAD, LANES), F32), a * sg], axis=0)
        dp = jnp.concatenate([d, jnp.zeros((CONV_PAD, LANES), F32)], axis=0)
        dxg = jnp.zeros((T, LANES), F32)
        dk_ref[...] = jnp.zeros_like(dk_ref)
        for k in range(CONV_TAPS):
            sh = CONV_TAPS - 1 - k
            tap = xp if sh == 0 else pltpu.roll(xp, sh, 0)
            dk_ref[k:k + 1, :] = jnp.sum(d * tap[CONV_PAD:, :], axis=0, keepdims=True)
            dtap = dp if sh == 0 else pltpu.roll(dp, T + CONV_PAD - sh, 0)
            dxg = dxg + kw[k:k + 1, :] * dtap[:T, :]
        db_ref[...] = jnp.sum(d, axis=0, keepdims=True)
        dza_ref[...] = (dxg * sg).astype(dza_ref.dtype)
        dzb_ref[...] = (dxg * a * sg * (1.0 - sg)).astype(dzb_ref.dtype)

    col = pl.BlockSpec((T, LANES), lambda g: (0, g))
    return pl.pallas_call(
        body, grid=(4,),
        in_specs=[col, pl.BlockSpec((T, LANES), lambda g: (0, 4 * ZB_A + g)),
                  pl.BlockSpec((T, LANES), lambda g: (0, 4 * ZB_B + g)),
                  pl.BlockSpec((None, CONV_PAD, LANES), lambda g: (g, 0, 0))],
        out_specs=[col, col, pl.BlockSpec((CONV_PAD, LANES), lambda g: (0, g)),
                   pl.BlockSpec((1, LANES), lambda g: (0, g))],
        out_shape=[jax.ShapeDtypeStruct((T, 512), BF), jax.ShapeDtypeStruct((T, 512), BF),
                   jax.ShapeDtypeStruct((CONV_PAD, 512), F32), jax.ShapeDtypeStruct((1, 512), F32)],
        name="conv_bwd", compiler_params=_params(1))(dy, z, z, convk[l])


D = D_MODEL


def _ffn_fwd(l, h, S, W, pre, deps=()):
    n, gp, u, a = _norm_mm("ffn_in", h, S[pre + "_pre_g"], [W[pre + "_w_gate"][l], W[pre + "_w_up"][l]], True,
                           act=True, deps=deps)
    f, out = _mm_res("ffn_out", a, W[pre + "_w_down"][l], h, S[pre + "_post_g"], 0.5)
    return out, dict(h=h, n=n, gp=gp, u=u, a=a, f=f)


def _ffn_bwd(l, dh, sv, S, W, G, SG, pre, deps=()):
    df, SG[pre + "_post_g"], dgp, du = _resbwd_mm("ffn_bwd_act", dh, sv["f"], S[pre + "_post_g"], 0.5,
                                                  W[pre + "_w_down"][l], True, act=(sv["gp"], sv["u"]), deps=deps)
    G[pre + "_w_down"] = _mm_tn("ffn_dw_down", sv["a"], df, G[pre + "_w_down"], l, True)
    G[pre + "_w_gate"] = _mm_tn("ffn_dw_gate", dgp, sv["n"], G[pre + "_w_gate"], l, True)
    G[pre + "_w_up"] = _mm_tn("ffn_dw_up", du, sv["n"], G[pre + "_w_up"], l, True)
    dh_in, SG[pre + "_pre_g"] = _dn_prenorm("ffn_bwd_in", [(dgp, W[pre + "_w_gate"][l]), (du, W[pre + "_w_up"][l])],
                                            False, dh, sv["h"], S[pre + "_pre_g"])
    return dh_in


def _gates(zg):
    return [_sigmoid(jnp.concatenate([zg[2 * k].astype(F32), zg[2 * k + 1].astype(F32)], axis=1)) for k in range(3)]


def _merge_fwd(z, rs, ws, tm=ROW_TILE):
    T = z.shape[0]
    tm = _tile(T, tm)
    nb, kk, bw = ws[0].shape

    def body(*refs):
        r_refs, g_refs, w_refs, y_refs, m_ref = refs[:3], refs[3:9], refs[9:12], refs[12:15], refs[15]
        for r_ref, w_ref, y_ref in zip(r_refs, w_refs, y_refs):
            for b in range(nb):
                y_ref[:, b * bw:(b + 1) * bw] = jnp.dot(r_ref[...], w_ref[b],
                                                        preferred_element_type=F32).astype(y_ref.dtype)
        g = _gates([q[...] for q in g_refs])
        m_ref[...] = (g[0] * y_refs[0][...].astype(F32) + g[1] * y_refs[1][...].astype(F32)
                      + g[2] * y_refs[2][...].astype(F32)).astype(m_ref.dtype)

    row = pl.BlockSpec((tm, D_MODEL), lambda i: (i, 0))
    return pl.pallas_call(
        body, grid=(T // tm,),
        in_specs=[pl.BlockSpec((tm, kk), lambda i: (i, 0))] * 3
        + [pl.BlockSpec((tm, 512), lambda i, j=j: (i, ZB_GATES + j)) for j in range(6)]
        + [pl.BlockSpec(ws[0].shape, lambda i: (0, 0, 0))] * 3,
        out_specs=[row] * 4, out_shape=[jax.ShapeDtypeStruct((T, D_MODEL), BF)] * 4,
        name="mix_merge", compiler_params=_params(1))(*rs, *[z] * 6, *ws)


def _merge_bwd(dmerged, z, ys, ws, tm=ROW_TILE // 2):
    T = z.shape[0]
    tm = _tile(T, tm)
    nb, kk, bw = ws[0].shape

    def body(*refs):
        dm_ref, g_refs, y_refs, w_refs = refs[0], refs[1:7], refs[7:10], refs[10:13]
        dy_refs, dzg_ref, dr_refs = refs[13:16], refs[16], refs[17:20]
        dm = dm_ref[...].astype(F32)
        g = _gates([q[...] for q in g_refs])
        for k in range(3):
            dy_refs[k][...] = (dm * g[k]).astype(BF)
            dzg_ref[:, k * D_MODEL:(k + 1) * D_MODEL] = (dm * y_refs[k][...].astype(F32) * g[k]
                                                         * (1.0 - g[k])).astype(BF)
            dr = None
            for b in range(nb):
                p = lax.dot_general(dy_refs[k][:, b * bw:(b + 1) * bw], w_refs[k][b], _NT,
                                    preferred_element_type=F32)
                dr = p if dr is None else dr + p
            dr_refs[k][...] = dr.astype(BF)

    row = pl.BlockSpec((tm, D_MODEL), lambda i: (i, 0))
    return pl.pallas_call(
        body, grid=(T // tm,),
        in_specs=[row] + [pl.BlockSpec((tm, 512), lambda i, j=j: (i, ZB_GATES + j)) for j in range(6)] + [row] * 3
        + [pl.BlockSpec(ws[0].shape, lambda i: (0, 0, 0))] * 3,
        out_specs=[row] * 3 + [pl.BlockSpec((tm, 3 * D_MODEL), lambda i: (i, 0))]
        + [pl.BlockSpec((tm, kk), lambda i: (i, 0))] * 3,
        out_shape=[jax.ShapeDtypeStruct((T, D_MODEL), BF)] * 3 + [jax.ShapeDtypeStruct((T, 3 * D_MODEL), BF)]
        + [jax.ShapeDtypeStruct((T, kk), BF)] * 3,
        name="mix_merge_bwd", compiler_params=_params(1))(dmerged, *[z] * 6, *ys, *ws)


def _mix_fwd(l, h, S, W):
    n, z = _norm_mm("mix_in", h, S["mix_pre_g"], [W["w_in"][l]], False)
    r_pool = _pool_fwd(z, S["pool_w"], S["pool_scale"])
    r_sgu = _sgu_fwd(z, S["sgu_ln_g"], S["sgu_ln_b"], S["sgu_w_s"], S["sgu_bias"])
    yc = _conv_fwd(z, W["conv_dw_k"], l, S["conv_dw_b"])

    def ln_silu(y, g, b):
        xh, _ = _ln_stats(y.astype(F32))
        return (_silu_and_grad(xh * g + b)[0],)

    r_conv = _rowwise("conv_ln", ln_silu, [(yc, 512, 0)], [S["conv_ln_g"], S["conv_ln_b"]], [(512, BF)])[0]
    y_pool, y_sgu, y_conv, merged = _merge_fwd(z, (r_pool, r_sgu, r_conv),
                                               [W["w_%s_out" % br][l] for br in ("pool", "sgu", "conv")])
    o, out = _mm_res("mix_out", merged, W["w_out"][l], h, S["mix_post_g"], 1.0)
    return out, dict(h=h, n=n, z=z, r_pool=r_pool, r_sgu=r_sgu, yc=yc, r_conv=r_conv, y_pool=y_pool, y_sgu=y_sgu,
                     y_conv=y_conv, merged=merged, o=o)


def _mix_bwd(l, dh, sv, S, W, G, SG, deps=()):
    z = sv["z"]
    do, SG["mix_post_g"], dmerged = _resbwd_mm("mix_bwd_out", dh, sv["o"], S["mix_post_g"], 1.0,
                                               W["w_out"][l].reshape(1, D, D), True, deps=deps)
    G["w_out"] = _mm_tn("mix_dw_out", sv["merged"], do, G["w_out"], l, True)

    branches = ("pool", "sgu", "conv")
    res = _merge_bwd(dmerged, z, [sv["y_" + br] for br in branches], [W["w_%s_out" % br][l] for br in branches])
    dzg, dr = res[3], dict(zip(branches, res[4:]))
    for br, dy in zip(branches, res[:3]):
        wn = "w_%s_out" % br
        G[wn] = _mm_tn("branch_dw", sv["r_" + br], dy, G[wn], l, False)
    dz_pool, SG["pool_w"], SG["pool_scale"] = _pool_bwd(dr["pool"], z, S["pool_w"], S["pool_scale"])
    dzu, dzv, SG["sgu_ln_g"], SG["sgu_ln_b"], SG["sgu_w_s"], dbias = _sgu_bwd(
        dr["sgu"], z, S["sgu_ln_g"], S["sgu_ln_b"], S["sgu_w_s"], S["sgu_w_st"], S["sgu_bias"])
    SG["sgu_b_s"] = dbias[:, ::LANES].T

    def ln_silu_bwd(d, y, g, b):
        xh, r = _ln_stats(y.astype(F32))
        _, grad = _silu_and_grad(xh * g + b)
        return _ln_bwd(xh, r, g, d.astype(F32) * grad)

    dyc, SG["conv_ln_g"], SG["conv_ln_b"] = _rowwise(
        "conv_ln_bwd", ln_silu_bwd, [(dr["conv"], 512, 0), (sv["yc"], 512, 0)], [S["conv_ln_g"], S["conv_ln_b"]],
        [(512, BF)], [(1, 512), (1, 512)])
    dza, dzb, SG["conv_dw_k"], SG["conv_dw_b"] = _conv_bwd(dyc, z, W["conv_dw_k"], l)
    dz = jnp.concatenate([dz_pool, dzu, dzv, dza, dzb, dzg], axis=1)
    G["w_in"] = _mm_tn("mix_dw_in", sv["n"], dz, G["w_in"], l, False)
    dh_in, SG["mix_pre_g"] = _dn_prenorm("mix_bwd_in", [(dz, W["w_in"][l])], True, dh, sv["h"], S["mix_pre_g"])
    return dh_in


def _ple_fwd(l, h, p_l, S, W):
    n, gp = _norm_mm("ple_in", h, S["ple_pre_g"], [W["ple_w_gate"][l].reshape(1, D, D)], False)
    e = _mm_blockout("ple_proj", p_l, W["ple_w_proj"], l, False)

    def res(x, ee, g, gg):
        return (x + _rms_fwd(_sigmoid(g.astype(F32)) * ee.astype(F32), gg),)

    out = _rowwise("ple_res", res, [(h, D, 0), (e, D, 0), (gp, D, 0)], [S["ple_post_g"]], [(D, F32)])[0]
    return out, dict(h=h, n=n, e=e, gp=gp, p=p_l)


def _ple_bwd(l, dh, sv, S, W, G, SG, deps=()):
    def res_bwd(d, ee, g, gg):
        sg = _sigmoid(g.astype(F32))
        ee = ee.astype(F32)
        dq, dg = _rms_bwd(sg * ee, gg, d)
        return dq * sg, dq * ee * sg * (1.0 - sg), dg

    de, dgp, SG["ple_post_g"] = _rowwise("ple_res_bwd", res_bwd, [(dh, D, 0), (sv["e"], D, 0), (sv["gp"], D, 0)],
                                         [S["ple_post_g"]], [(D, BF), (D, BF)], [(1, D)], deps=deps)
    G["ple_w_proj"] = _mm_tn("ple_dw_proj", sv["p"], de, G["ple_w_proj"], l, False)
    G["ple_w_gate"] = _mm_tn("ple_dw_gate", sv["n"], dgp, G["ple_w_gate"], l, True)
    dh_in, SG["ple_pre_g"] = _dn_prenorm("ple_bwd_in", [(dgp, W["ple_w_gate"][l].reshape(1, D, D))], True, dh,
                                         sv["h"], S["ple_pre_g"])
    return dh_in


def _layer_small(a, l):
    S = {}
    for name in SMALL:
        v = a[name][l]
        S[name] = v.reshape(1, -1) if v.ndim == 1 else v
    S["sgu_w_st"] = jnp.swapaxes(S["sgu_w_s"], 1, 2)
    S["sgu_bias"] = jnp.repeat(S["sgu_b_s"].T, LANES, axis=1)
    return S


def _layer_fwd(l, h, p_l, S, W, deps=(), mid=None):
    h, sv1 = _ffn_fwd(l, h, S, W, "ffn1", deps)
    h, sv2 = _mix_fwd(l, h, S, W)
    h, sv3 = _ffn_fwd(l, h, S, W, "ffn2", mid(h) if mid else ())
    h, sv4 = _ple_fwd(l, h, p_l, S, W)
    return h, (sv1, sv2, sv3, sv4)


def _layer_bwd(l, dh, sv, S, W, G, deps=(), mid=None):
    SG = {}
    dh = _ple_bwd(l, dh, sv[3], S, W, G, SG, deps)
    dh = _ffn_bwd(l, dh, sv[2], S, W, G, SG, "ffn2")
    dh = _mix_bwd(l, dh, sv[1], S, W, G, SG, mid(dh) if mid else ())
    dh = _ffn_bwd(l, dh, sv[0], S, W, G, SG, "ffn1")
    return dh, SG


HBM = pl.BlockSpec(memory_space=pltpu.HBM)
SEM = pl.BlockSpec(memory_space=pltpu.SEMAPHORE)
SIDE_EFFECT = pltpu.SideEffectType.DATAFLOW_SIDE_EFFECTING


def _place():
    x, y, c = lax.axis_index("x"), lax.axis_index("y"), lax.axis_index("c")
    chips = [(1 - x, y), (x, 1 - y), (1 - x, 1 - y)]
    return x, y, c, chips


def _remote(src, dst, send_sem, recv_sem, to):
    return pltpu.make_async_remote_copy(src_ref=src, dst_ref=dst, send_sem=send_sem, recv_sem=recv_sem,
                                        device_id=to, device_id_type=MESH)


def _split_start(name, plan, bufs, deps):
    count, fn = plan
    n, nd = len(bufs), len(deps)

    def body(*refs):
        send, recv = refs[nd + n], refs[nd + n + 1]
        x, y, c, chips = _place()
        for k, (src, dst, _, to) in enumerate(fn(refs[nd:nd + n], x, y, c, chips)):
            _remote(src, dst, send.at[k], recv.at[k], to).start()
        refs[-1][...] = jnp.zeros_like(refs[-1])

    res = pl.pallas_call(
        body, in_specs=[ANY] * nd + [HBM] * n,
        out_specs=[SEM, SEM] + [HBM] * n + [pl.BlockSpec(memory_space=pltpu.VMEM)],
        out_shape=[pltpu.SemaphoreType.DMA((count,)), pltpu.SemaphoreType.DMA((count,))]
        + [pltpu.HBM(b.shape, b.dtype) for b in bufs] + [jax.ShapeDtypeStruct((8, LANES), F32)],
        input_output_aliases={nd + i: 2 + i for i in range(n)}, name=name,
        compiler_params=pltpu.CompilerParams(has_side_effects=SIDE_EFFECT),
    )(*deps, *[pltpu.with_memory_space_constraint(b, pltpu.HBM) for b in bufs])
    return (res[0], res[1]), list(res[2:2 + n]), res[-1]


def _split_wait(name, plan, sems, bufs, after):
    _, fn = plan
    n = len(bufs)

    def body(*refs):
        send, recv = refs[n], refs[n + 1]
        x, y, c, chips = _place()
        for k, (src, _, land, to) in enumerate(fn(refs[:n], x, y, c, chips)):
            cp = _remote(src, land, send.at[k], recv.at[k], to)
            cp.wait_send()
            cp.wait_recv()

    res = pl.pallas_call(
        body, in_specs=[HBM] * n + [SEM, SEM] + [ANY] * len(after), out_specs=[HBM] * n,
        out_shape=[pltpu.HBM(b.shape, b.dtype) for b in bufs], input_output_aliases={i: i for i in range(n)},
        name=name, compiler_params=pltpu.CompilerParams(has_side_effects=SIDE_EFFECT))(*bufs, *sems, *after)
    return list(res)


def _gather_plans(n):
    def across(b, x, y, c, chips):
        me, out = 2 * x + y, []
        for a in range(n):
            rh = b[a].shape[1] // 2
            mine = b[a].at[me, pl.ds(c * rh, rh)]
            for cx, cy in chips:
                out.append((mine, mine, b[a].at[2 * cx + cy, pl.ds(c * rh, rh)], (cx, cy, c)))
        return out

    def to_sibling(b, x, y, c, chips):
        out = []
        for a in range(n):
            rh = b[a].shape[1] // 2
            for cx, cy in chips:
                piece = b[a].at[2 * cx + cy, pl.ds(c * rh, rh)]
                out.append((piece, piece, b[a].at[2 * cx + cy, pl.ds((1 - c) * rh, rh)], (x, y, 1 - c)))
        return out

    return (3 * n, across), (3 * n, to_sibling)


def _pair_plan(n):
    def fn(b, x, y, c, chips):
        out = []
        for a in range(n):
            rh = b[a].shape[1] // 2
            out.append((b[a].at[:, pl.ds((1 - c) * rh, rh)], b[n + a], b[n + a], (x, y, 1 - c)))
        return out

    return n, fn


def _cross_plan(n):
    def fn(b, x, y, c, chips):
        out = []
        for a in range(n):
            for j, (cx, cy) in enumerate(chips):
                out.append((b[a].at[2 * cx + cy], b[n + a].at[j], b[n + a].at[j], (cx, cy, c)))
        return out

    return 3 * n, fn


def _share_plan(n, l):
    def fn(b, x, y, c, chips):
        out = []
        for a in range(n):
            rh = b[a].shape[1] // 2
            mine = b[a].at[l, pl.ds(c * rh, rh)]
            out.append((mine, mine, b[a].at[l, pl.ds((1 - c) * rh, rh)], (x, y, 1 - c)))
        return out

    return n, fn


def _exchange_call(name, body, ins, out_shapes, n_remote, in_place=False):
    scratch = [pltpu.SemaphoreType.DMA((n_remote,)), pltpu.SemaphoreType.DMA((n_remote,))]
    aliases = {i: i for i in range(len(ins))} if in_place else {}
    return pl.pallas_call(body, in_specs=[ANY] * len(ins), out_specs=[ANY] * len(out_shapes), out_shape=out_shapes,
                          scratch_shapes=scratch, input_output_aliases=aliases, name=name)(*ins)


def _peers(x, y, c):
    return [(1 - x if m & 4 else x, 1 - y if m & 2 else y, 1 - c if m & 1 else c) for m in range(1, 8)]


def _scatter_small(v3):
    def body(v_ref, o_ref, send, recv):
        x, y, c, _ = _place()
        cps = []
        for m, (px, py, pc) in enumerate(_peers(x, y, c)):
            cps.append(_remote(v_ref.at[4 * px + 2 * py + pc], o_ref.at[m], send.at[m], recv.at[m], (px, py, pc)))
            cps[-1].start()
        for cp in cps:
            cp.wait()

    return _exchange_call("scatter_small", body, [v3], [jax.ShapeDtypeStruct((7,) + v3.shape[1:], v3.dtype)], 7)[0]


def _gather_small(buf):
    def body(_, o_ref, send, recv):
        x, y, c, _ = _place()
        mine = o_ref.at[4 * x + 2 * y + c]
        peers = _peers(x, y, c)
        cps = []
        for m, to in enumerate(peers):
            cps.append(_remote(mine, mine, send.at[m], recv.at[m], to))
            cps[-1].start()
        for m, (px, py, pc) in enumerate(peers):
            slab = o_ref.at[4 * px + 2 * py + pc]
            _remote(slab, slab, send.at[m], recv.at[m], (px, py, pc)).wait_recv()
            cps[m].wait_send()

    return _exchange_call("gather_small", body, [buf], [jax.ShapeDtypeStruct(buf.shape, buf.dtype)], 7,
                          in_place=True)[0]


def _allreduce_small(v, pos):
    rows = v.shape[0]
    rs = rows // 8
    v3 = v.reshape(8, rs, LANES)
    got = _scatter_small(v3)
    tm = _tile(rs)
    ins = [(v3, (None, tm, LANES), lambda i, p: (p[2], i, 0))]
    ins += [(got, (None, tm, LANES), lambda i, p, m=m: (m, i, 0)) for m in range(7)]
    buf = _tiled("sum_small", lambda *t: (((((((t[0] + t[1]) + t[2]) + t[3]) + t[4]) + t[5]) + t[6]) + t[7],),
                 (rs // tm,), pos, ins, [((8, rs, LANES), F32, (None, tm, LANES), lambda i, p: (p[2], i, 0))])[0]
    return _gather_small(buf).reshape(rows, LANES)


ADD_ROWS = 128


def _multi_tiled(name, fn, pos, groups, in_place=False):
    steps = max(g[1] for g in groups)
    flat_in, in_specs, out_specs, out_shape, counts, dests = [], [], [], [], [], []
    for ins, n_t, (shape, dtype, oidx, dest) in groups:
        for arr, idx in ins:
            flat_in.append(arr)
            in_specs.append(pl.BlockSpec((ADD_ROWS, arr.shape[1]),
                                         lambda i, p, idx=idx, n_t=n_t: (idx(jnp.minimum(i, n_t - 1), p), 0)))
        out_specs.append(pl.BlockSpec((ADD_ROWS, shape[1]),
                                      lambda i, p, oidx=oidx, n_t=n_t: (oidx(jnp.minimum(i, n_t - 1), p), 0)))
        out_shape.append(jax.ShapeDtypeStruct(shape, dtype))
        counts.append((len(ins), n_t))
        dests.append(dest)
    n_in = len(flat_in)
    extra = dests if in_place else []

    def body(_, *refs):
        outs = refs[n_in + len(extra):]
        k = 0
        for (n_a, n_t), o_ref in zip(counts, outs):
            tiles = refs[k:k + n_a]
            k += n_a

            @pl.when(pl.program_id(0) < n_t)
            def _(tiles=tiles, o_ref=o_ref):
                o_ref[...] = fn(*[t[...] for t in tiles]).astype(o_ref.dtype)

    spec = pltpu.PrefetchScalarGridSpec(num_scalar_prefetch=1, grid=(steps,),
                                        in_specs=in_specs + [ANY] * len(extra), out_specs=out_specs)
    return pl.pallas_call(body, grid_spec=spec, out_shape=out_shape,
                          input_output_aliases={1 + n_in + k: k for k in range(len(extra))}, name=name,
                          compiler_params=_params(1))(pos, *flat_in, *extra)


def _add_pair(grads, got, pos):
    groups = []
    for g, q in zip(grads, got):
        nb, R, C = g.shape
        rh = R // 2
        nh = rh // ADD_ROWS
        groups.append(([(g.reshape(nb * R, C), lambda t, p, nh=nh: (t // nh) * 2 * nh + p[1] * nh + t % nh),
                        (q.reshape(nb * rh, C), lambda t, p: t)], nb * nh,
                       ((nb * rh, C), BF, lambda t, p: t, None)))
    res = _multi_tiled("rs_add_pair", lambda u, w: u.astype(F32) + w.astype(F32), pos, groups)
    return [t.reshape(q.shape) for t, q in zip(res, got)]


def _add_chips(parts, slots, reduced, l, pos):
    def add(own, s0, s1, s2):
        return ((own.astype(F32) + s0.astype(F32)) + s1.astype(F32)) + s2.astype(F32)

    groups = []
    for t, s, red in zip(parts, slots, reduced):
        nb, rh, C = t.shape
        L = red.shape[0]
        nh = rh // ADD_ROWS
        ins = [(t.reshape(nb * rh, C), lambda i, p, nh=nh: p[0] * nh + i)]
        ins += [(s.reshape(3 * rh, C), lambda i, p, j=j, nh=nh: j * nh + i) for j in range(3)]
        groups.append((ins, nh, ((L * 2 * rh, C), F32, lambda i, p, nh=nh: l * 2 * nh + p[1] * nh + i,
                                 red.reshape(L * 2 * rh, C))))
    res = _multi_tiled("rs_add_chips", add, pos, groups, in_place=True)
    return [buf.reshape(red.shape) for buf, red in zip(res, reduced)]


def _adamw_math(w, g, m, v):
    m = ADAM_B1 * m + (1.0 - ADAM_B1) * g
    v = ADAM_B2 * v + (1.0 - ADAM_B2) * (g * g)
    m_hat = m / (1.0 - ADAM_B1 ** ADAM_STEP)
    v_hat = v / (1.0 - ADAM_B2 ** ADAM_STEP)
    return -ADAM_LR * (m_hat / (jnp.sqrt(v_hat) + ADAM_EPS) + ADAM_WD * w), m, v


def _adamw(w, g, m, v, lo=0, hi=None, into=None, deps=()):
    L, R, C = w.shape
    hi = L if hi is None else hi
    tr = _tile(R, 256)
    extra = (list(into) if into else []) + list(deps)
    n_alias = 4 if into else 0

    def body(w_ref, g_ref, m_ref, v_ref, *rest):
        go_ref, d_ref, mo_ref, vo_ref = rest[len(extra):]
        gv = g_ref[...]
        d, mn, vn = _adamw_math(w_ref[...], gv, m_ref[...], v_ref[...])
        go_ref[...] = gv
        d_ref[...] = d
        mo_ref[...] = mn
        vo_ref[...] = vn

    spec = pl.BlockSpec((None, tr, C), lambda l, i: (l + lo, i, 0))
    out = jax.ShapeDtypeStruct(w.shape, F32)
    return pl.pallas_call(body, grid=(hi - lo, R // tr), in_specs=[spec] * 4 + [ANY] * len(extra),
                          out_specs=[spec] * 4, out_shape=[out] * 4,
                          input_output_aliases={4 + k: k for k in range(n_alias)}, name="adamw",
                          compiler_params=_params(2))(w, g, m, v, *extra)


def _pack(parts):
    flat = jnp.concatenate([q.reshape(-1, LANES) for q in parts], axis=0)
    return jnp.pad(flat, ((0, -flat.shape[0] % ROW_TILE), (0, 0)))


def _unpack(flat, like):
    out, r = [], 0
    for q in like:
        n = q.size // LANES
        out.append(flat[r:r + n].reshape(q.shape))
        r += n
    return out


def _train_step(a):
    a = dict(a)
    L = a["ffn1_pre_g"].shape[0]
    x, y, c, _ = _place()
    chip = 2 * x + y
    pos = jnp.stack([chip, c, 2 * chip + c]).astype(jnp.int32)
    for name in TRANSPOSED:
        for pre in ("", "m_", "v_"):
            a[pre + name] = jnp.swapaxes(a[pre + name], 1, 2)
    big = [b[0] for b in BIG]
    gathered = big + ["conv_dw_k"]
    n_w, n_g = len(gathered), len(big)

    own = [_cast_layers("cast_weight", a[name], rp, cp, BF, pos) for name, _, _, _, rp, cp in BIG]
    own.append(_cast_layers("pad_conv_taps", a["conv_dw_k"].reshape(L, CONV_TAPS, LANES), CONV_PAD, LANES, F32, pos))
    across, to_sibling = _gather_plans(n_w)
    W = {name: [None] * L for name in gathered}

    def gather_first(l, deps):
        sems, bufs, token = _split_start("gather_a%d" % l, across, [own[i][l] for i in range(n_w)], deps)
        return sems, bufs, token

    def gather_second(l, state, after):
        bufs = _split_wait("gather_a%d_done" % l, across, state[0], state[1], after)
        return _split_start("gather_b%d" % l, to_sibling, bufs, [])

    def gather_done(l, state, after):
        for name, buf in zip(gathered, _split_wait("gather_b%d_done" % l, to_sibling, state[0], state[1], after)):
            W[name][l] = buf

    state = gather_first(0, [])
    state = gather_second(0, state, [])
    gather_done(0, state, [])

    small = [_layer_small(a, l) for l in range(L)]
    h, saved = a["x"][0], []
    for l in range(L):
        deps, mid, box = (), None, {}
        if l + 1 < L:
            box["state"] = gather_first(l + 1, [h])
            deps = (box["state"][2],)

            def mid(hm, l=l, box=box):
                box["state"] = gather_second(l + 1, box["state"], [hm])
                return (box["state"][2],)

        h, sv = _layer_fwd(l, h, a["p"][l, 0], small[l], W, deps, mid)
        saved.append(sv)
        if l + 1 < L:
            gather_done(l + 1, box["state"], [h])

    def loss_fn(yv, t):
        e = yv - t
        return e * (1.0 / D), jnp.sum(e * e, axis=0, keepdims=True)

    dh, lsum = _rowwise("loss", loss_fn, [(h, D, 0), (a["loss_target"][0], D, 0)], [], [(D, F32)], [(1, D)])
    loss = lax.psum(0.5 * jnp.sum(lsum) / D, ("x", "y", "c"))

    G = {name: [jax.ShapeDtypeStruct((N_CHIPS, rp, cp), BF)] * L for name, _, _, _, rp, cp in BIG}
    reduced = [lax.empty((L, rp, cp), F32) for _, _, _, _, rp, cp in BIG]
    pair, cross = _pair_plan(n_g), _cross_plan(n_g)
    small_grads = [None] * L

    def pair_start(l, deps):
        grads = [G[name][l] for name in big]
        lands = [lax.empty((N_CHIPS, g.shape[1] // 2, g.shape[2]), BF) for g in grads]
        return _split_start("rs_pair%d" % l, pair, grads + lands, deps)

    def cross_start(l, state, after):
        bufs = _split_wait("rs_pair%d_done" % l, pair, state[0], state[1], after)
        parts = _add_pair(bufs[:n_g], bufs[n_g:], pos)
        lands = [lax.empty((3,) + t.shape[1:], BF) for t in parts]
        return _split_start("rs_cross%d" % l, cross, parts + lands, [])

    def share_start(l, state, after, reduced):
        bufs = _split_wait("rs_cross%d_done" % l, cross, state[0], state[1], after)
        reduced = _add_chips(bufs[:n_g], bufs[n_g:], reduced, l, pos)
        return _split_start("rs_share%d" % l, _share_plan(n_g, l), reduced, [])

    def share_done(l, state, after):
        return _split_wait("rs_share%d_done" % l, _share_plan(n_g, l), state[0], state[1], after)

    st_pair = st_cross = st_share = None
    for l in reversed(range(L)):
        deps = tuple(s[2] for s in (st_pair, st_share) if s is not None)
        box = {"cross": None}

        def mid(dm, l=l, box=box, st_pair=st_pair, st_share=st_share):
            out = []
            if st_share is not None:
                box["reduced"] = share_done(l + 2, st_share, [dm])
            if st_pair is not None:
                box["cross"] = cross_start(l + 1, st_pair, [dm])
                out.append(box["cross"][2])
            return tuple(out)

        dh, small_grads[l] = _layer_bwd(l, dh, saved[l], small[l], W, G, deps, mid)
        if st_share is not None:
            reduced = box["reduced"]
        st_share = share_start(l + 1, box["cross"], [dh], reduced) if box["cross"] is not None else None
        st_pair = pair_start(l, [dh])
    grad_x = dh
    if st_share is not None:
        reduced = share_done(1, st_share, [])
    st_cross = cross_start(0, st_pair, [])
    small_names = SMALL + ("conv_dw_k",)
    stacked = [jnp.stack([small_grads[l][name] for l in range(L)]) for name in small_names]
    summed = dict(zip(small_names, _unpack(_allreduce_small(_pack(stacked), pos), stacked)))
    upper = {}
    if L > 1:
        for name, red in zip(big, reduced):
            upper[name] = _adamw(a[name], red, a["m_" + name], a["v_" + name], 1, L, deps=[st_cross[2]])
    st_share = share_start(0, st_cross, [r[1] for r in upper.values()] + [summed[small_names[0]]], reduced)
    reduced = share_done(0, st_share, [])
    big_grads = dict(zip(big, reduced))

    grads, deltas, new_m, new_v = {}, {}, {}, {}
    for name in big:
        res = _adamw(a[name], big_grads[name], a["m_" + name], a["v_" + name], 0, 1, upper.get(name))
        if name in TRANSPOSED:
            res = [jnp.swapaxes(r, 1, 2) for r in res]
        grads[name], deltas[name], new_m[name], new_v[name] = res
    taps = lax.dynamic_slice_in_dim(summed["conv_dw_k"], chip * LANES, LANES, axis=2)[:, :CONV_TAPS]
    grads["conv_dw_k"] = taps.reshape(a["conv_dw_k"].shape)
    for name in SMALL:
        grads[name] = summed[name].reshape(a[name].shape)
    shapes = [a[name] for name in small_names]
    res = _adamw(*[_pack([a[pre + name] if pre != "g" else grads[name] for name in small_names])[None]
                   for pre in ("", "g", "m_", "v_")])
    for dst, flat in zip((deltas, new_m, new_v), res[1:]):
        for name, val in zip(small_names, _unpack(flat[0], shapes)):
            dst[name] = val

    return (loss, grad_x[None], *[grads[n] for n in WEIGHTS], *[deltas[n] for n in WEIGHTS],
            *[new_m[n] for n in WEIGHTS], *[new_v[n] for n in WEIGHTS])


def kernel(x, p, ffn1_pre_g, ffn1_w_gate, ffn1_w_up, ffn1_w_down, ffn1_post_g, mix_pre_g, w_in, pool_w, pool_scale, w_pool_out, sgu_ln_g, sgu_ln_b, sgu_w_s, sgu_b_s, w_sgu_out, conv_dw_k, conv_dw_b, conv_ln_g, conv_ln_b, w_conv_out, w_out, mix_post_g, ffn2_pre_g, ffn2_w_gate, ffn2_w_up, ffn2_w_down, ffn2_post_g, ple_w_proj, ple_pre_g, ple_w_gate, ple_post_g, loss_target, m_ffn1_pre_g, m_ffn1_w_gate, m_ffn1_w_up, m_ffn1_w_down, m_ffn1_post_g, m_mix_pre_g, m_w_in, m_pool_w, m_pool_scale, m_w_pool_out, m_sgu_ln_g, m_sgu_ln_b, m_sgu_w_s, m_sgu_b_s, m_w_sgu_out, m_conv_dw_k, m_conv_dw_b, m_conv_ln_g, m_conv_ln_b, m_w_conv_out, m_w_out, m_mix_post_g, m_ffn2_pre_g, m_ffn2_w_gate, m_ffn2_w_up, m_ffn2_w_down, m_ffn2_post_g, m_ple_w_proj, m_ple_pre_g, m_ple_w_gate, m_ple_post_g, v_ffn1_pre_g, v_ffn1_w_gate, v_ffn1_w_up, v_ffn1_w_down, v_ffn1_post_g, v_mix_pre_g, v_w_in, v_pool_w, v_pool_scale, v_w_pool_out, v_sgu_ln_g, v_sgu_ln_b, v_sgu_w_s, v_sgu_b_s, v_w_sgu_out, v_conv_dw_k, v_conv_dw_b, v_conv_ln_g, v_conv_ln_b, v_w_conv_out, v_w_out, v_mix_post_g, v_ffn2_pre_g, v_ffn2_w_gate, v_ffn2_w_up, v_ffn2_w_down, v_ffn2_post_g, v_ple_w_proj, v_ple_pre_g, v_ple_w_gate, v_ple_post_g):
    return _train_step(dict(locals()))
```

```python
import math

import jax
import jax.numpy as jnp
from jax import lax
from jax.experimental import pallas as pl
from jax.experimental.pallas import tpu as pltpu

BF = jnp.bfloat16
F32 = jnp.float32
EPS = 1e-6
D_MODEL = 1024
LANES = 128
SUBLANES = 8
MXU_TILE = 256
N_CHIPS = 4
FFN_SHARD = 704
FFN_SHARD_PAD = 768
POOL_WINDOWS = (2, 4, 8, 16)
SGU_HEADS = 4
CHUNK = 128
CONV_TAPS = 31
CONV_PAD = 32
ROW_TILE = 512
EPI_ROWS = 256
VMEM_LIMIT_BYTES = 56 * 1024 * 1024
ADAM_LR, ADAM_B1, ADAM_B2, ADAM_EPS, ADAM_WD, ADAM_STEP = 0.001, 0.9, 0.999, 1e-08, 0.01, 10
MESH = pl.DeviceIdType.MESH
ANY = pl.BlockSpec(memory_space=pl.ANY)

ZB_POOL, ZB_U, ZB_V, ZB_A, ZB_B, ZB_GATES = 0, 1, 2, 3, 4, 5

TRANSPOSED = ("ffn1_w_gate", "ffn1_w_up", "ffn2_w_gate", "ffn2_w_up")
BIG = (
    ("ffn1_w_gate", "row", FFN_SHARD, 1024, FFN_SHARD_PAD, 1024),
    ("ffn1_w_up", "row", FFN_SHARD, 1024, FFN_SHARD_PAD, 1024),
    ("ffn1_w_down", "row", FFN_SHARD, 1024, FFN_SHARD_PAD, 1024),
    ("w_in", "col", 1024, 1408, 1024, 1408),
    ("w_pool_out", "col", 512, 256, 512, 256),
    ("w_sgu_out", "col", 512, 256, 512, 256),
    ("w_conv_out", "col", 512, 256, 512, 256),
    ("w_out", "row", 256, 1024, 256, 1024),
    ("ffn2_w_gate", "row", FFN_SHARD, 1024, FFN_SHARD_PAD, 1024),
    ("ffn2_w_up", "row", FFN_SHARD, 1024, FFN_SHARD_PAD, 1024),
    ("ffn2_w_down", "row", FFN_SHARD, 1024, FFN_SHARD_PAD, 1024),
    ("ple_w_proj", "col", 256, 256, 256, 256),
    ("ple_w_gate", "row", 256, 1024, 256, 1024),
)
SMALL = ("ffn1_pre_g", "ffn1_post_g", "mix_pre_g", "pool_w", "pool_scale", "sgu_ln_g", "sgu_ln_b", "sgu_w_s",
         "sgu_b_s", "conv_dw_b", "conv_ln_g", "conv_ln_b", "mix_post_g", "ffn2_pre_g", "ffn2_post_g",
         "ple_pre_g", "ple_post_g")
WEIGHTS = ("ffn1_pre_g", "ffn1_w_gate", "ffn1_w_up", "ffn1_w_down", "ffn1_post_g", "mix_pre_g", "w_in", "pool_w",
           "pool_scale", "w_pool_out", "sgu_ln_g", "sgu_ln_b", "sgu_w_s", "sgu_b_s", "w_sgu_out", "conv_dw_k",
           "conv_dw_b", "conv_ln_g", "conv_ln_b", "w_conv_out", "w_out", "mix_post_g", "ffn2_pre_g", "ffn2_w_gate",
           "ffn2_w_up", "ffn2_w_down", "ffn2_post_g", "ple_w_proj", "ple_pre_g", "ple_w_gate", "ple_post_g")


def _params(n_grid):
    return pltpu.CompilerParams(dimension_semantics=("arbitrary",) * n_grid, vmem_limit_bytes=VMEM_LIMIT_BYTES)


def _tile(n, cap=ROW_TILE):
    for t in range(min(cap, n) - min(cap, n) % 16, 0, -16):
        if n % t == 0:
            return t
    return n


def _sigmoid(x):
    return 0.5 * jnp.tanh(0.5 * x) + 0.5


def _silu_and_grad(x):
    s = _sigmoid(x)
    return x * s, s * (1.0 + x * (1.0 - s))


def _gelu_and_grad(x):
    cdf = 0.5 * (1.0 + lax.erf(x * (1.0 / math.sqrt(2.0))))
    pdf = jnp.exp(-0.5 * x * x) * (1.0 / math.sqrt(2.0 * math.pi))
    return x * cdf, cdf + x * pdf


def _rms_fwd(x, g):
    return x * lax.rsqrt(jnp.mean(x * x, axis=-1, keepdims=True) + EPS) * g


def _rms_bwd(x, g, dy):
    r = lax.rsqrt(jnp.mean(x * x, axis=-1, keepdims=True) + EPS)
    xh = x * r
    dxh = dy * g
    dx = r * (dxh - xh * jnp.mean(dxh * xh, axis=-1, keepdims=True))
    return dx, jnp.sum(dy * xh, axis=0, keepdims=True)


def _ln_stats(x):
    xc = x - jnp.mean(x, axis=-1, keepdims=True)
    r = lax.rsqrt(jnp.mean(xc * xc, axis=-1, keepdims=True) + EPS)
    return xc * r, r


def _ln_bwd(xh, r, g, dy):
    dxh = dy * g
    dx = r * (dxh - jnp.mean(dxh, axis=-1, keepdims=True) - xh * jnp.mean(dxh * xh, axis=-1, keepdims=True))
    return dx, jnp.sum(dy * xh, axis=0, keepdims=True), jnp.sum(dy, axis=0, keepdims=True)


def _rowwise(name, fn, rows, consts, outs, accs=(), tm=ROW_TILE, deps=()):
    T = rows[0][0].shape[-2]
    tm = _tile(T, tm)
    n_in, n_o, n_dep = len(rows) + len(consts), len(outs), len(deps)

    def body(*refs):
        refs = refs[n_dep:]
        res = fn(*[r[...] for r in refs[:n_in]])
        for ref, val in zip(refs[n_in:n_in + n_o], res[:n_o]):
            ref[...] = val.astype(ref.dtype)
        acc_refs = refs[n_in + n_o:]
        if acc_refs:
            @pl.when(pl.program_id(0) == 0)
            def _():
                for ref, val in zip(acc_refs, res[n_o:]):
                    ref[...] = val

            @pl.when(pl.program_id(0) != 0)
            def _():
                for ref, val in zip(acc_refs, res[n_o:]):
                    ref[...] += val

    in_specs = [ANY] * n_dep
    for row in rows:
        w, cb = row[1], row[2]
        if len(row) == 4:
            in_specs.append(pl.BlockSpec((None, tm, w), lambda i, cb=cb, ld=row[3]: (ld, i, cb)))
        else:
            in_specs.append(pl.BlockSpec((tm, w), lambda i, cb=cb: (i, cb)))
    in_specs += [pl.BlockSpec(c.shape, lambda i: (0, 0)) for c in consts]
    out_specs = [pl.BlockSpec((tm, w), lambda i: (i, 0)) for w, _ in outs]
    out_specs += [pl.BlockSpec(s, lambda i: (0, 0)) for s in accs]
    out_shape = [jax.ShapeDtypeStruct((T, w), dt) for w, dt in outs]
    out_shape += [jax.ShapeDtypeStruct(s, F32) for s in accs]
    return pl.pallas_call(body, grid=(T // tm,), in_specs=in_specs, out_specs=out_specs, out_shape=out_shape,
                          name=name, compiler_params=_params(1))(*deps, *[r[0] for r in rows], *consts)


def _tiled(name, fn, grid, pos, ins, outs):
    n_in = len(ins)

    def body(_, *refs):
        res = fn(*[r[...] for r in refs[:n_in]])
        for ref, val in zip(refs[n_in:], res):
            ref[...] = val.astype(ref.dtype)

    spec = pltpu.PrefetchScalarGridSpec(
        num_scalar_prefetch=1, grid=grid, in_specs=[pl.BlockSpec(bs, im) for _, bs, im in ins],
        out_specs=[pl.BlockSpec(bs, im) for _, _, bs, im in outs])
    return pl.pallas_call(body, grid_spec=spec, out_shape=[jax.ShapeDtypeStruct(s, d) for s, d, _, _ in outs],
                          name=name, compiler_params=_params(len(grid)))(pos, *[a for a, _, _ in ins])


def _cast_layers(name, w, rp, cp, dtype, pos):
    L, r, c = w.shape

    def body(_, w_ref, *o_refs):
        for k, o_ref in enumerate(o_refs):
            @pl.when(pl.program_id(0) == k)
            def _(o_ref=o_ref):
                if (rp, cp) != (r, c):
                    o_ref[...] = jnp.zeros_like(o_ref)
                    o_ref[pl.ds(0, r), pl.ds(0, c)] = w_ref[...].astype(dtype)
                else:
                    o_ref[...] = w_ref[...].astype(dtype)

    spec = pltpu.PrefetchScalarGridSpec(
        num_scalar_prefetch=1, grid=(L,), in_specs=[pl.BlockSpec((None, r, c), lambda l, p: (l, 0, 0))],
        out_specs=[pl.BlockSpec((None, rp, cp), lambda l, p: (p[0], 0, 0))] * L)
    return pl.pallas_call(body, grid_spec=spec, out_shape=[jax.ShapeDtypeStruct((N_CHIPS, rp, cp), dtype)] * L,
                          name=name, compiler_params=_params(1))(pos, w)


_NN = (((1,), (0,)), ((), ()))
_NT = (((1,), (1,)), ((), ()))
_TN = (((0,), (0,)), ((), ()))


def _mm_tn(name, a, dy, buf, l, a_blocked, tk=ROW_TILE):
    T = a.shape[0]
    nb, R, C = buf[l].shape

    def body(a_ref, dy_ref, o_ref):
        o_ref[...] = lax.dot_general(a_ref[...].astype(BF), dy_ref[...].astype(BF), _TN,
                                     preferred_element_type=F32).astype(o_ref.dtype)

    if a_blocked:
        grid = (nb,)
        in_specs = [pl.BlockSpec((T, R), lambda b: (0, b)), pl.BlockSpec((T, C), lambda b: (0, 0))]
        out_specs = pl.BlockSpec((None, R, C), lambda b: (b, 0, 0))
    else:
        tk = min(tk, R)
        grid = (nb, R // tk)
        in_specs = [pl.BlockSpec((T, tk), lambda b, k: (0, k)), pl.BlockSpec((T, C), lambda b, k: (0, b))]
        out_specs = pl.BlockSpec((None, tk, C), lambda b, k: (b, k, 0))
    buf = list(buf)
    buf[l] = pl.pallas_call(body, grid=grid, in_specs=in_specs, out_specs=out_specs,
                            out_shape=jax.ShapeDtypeStruct((nb, R, C), BF), name=name,
                            compiler_params=_params(len(grid)))(a, dy)
    return buf


def _acc_rows(ref, val, first):
    @pl.when(first)
    def _():
        ref[...] = val

    @pl.when(jnp.logical_not(first))
    def _():
        ref[...] += val


def _norm_mm(name, h, g, ws, trans_w, act=False, deps=(), tm=2 * ROW_TILE):
    T = h.shape[0]
    nb, r, cc = ws[0].shape
    bo = r if trans_w else cc
    tm = _tile(T, tm)
    n_w, n_dep = len(ws), len(deps)

    def body(*refs):
        refs = refs[n_dep:]
        h_ref, g_ref, w_refs = refs[0], refs[1], refs[2:2 + n_w]
        n_ref, o_refs, n_s = refs[2 + n_w], refs[3 + n_w:3 + 2 * n_w], refs[-1]

        @pl.when(pl.program_id(1) == 0)
        def _():
            n = _rms_fwd(h_ref[...].astype(F32), g_ref[...]).astype(BF)
            n_s[...] = n
            n_ref[...] = n

        n = n_s[...]
        if act:
            for j in range(bo // MXU_TILE):
                cols = slice(j * MXU_TILE, (j + 1) * MXU_TILE)
                prods = [lax.dot_general(n, w_ref[cols, :], _NT, preferred_element_type=F32).astype(BF)
                         for w_ref in w_refs]
                for o_ref, p in zip(o_refs, prods):
                    o_ref[:, cols] = p
                refs[3 + 2 * n_w][:, cols] = (_silu_and_grad(prods[0].astype(F32))[0]
                                              * prods[1].astype(F32)).astype(BF)
        else:
            for w_ref, o_ref in zip(w_refs, o_refs):
                o_ref[...] = lax.dot_general(n, w_ref[...], _NT if trans_w else _NN,
                                             preferred_element_type=F32).astype(BF)

    wide = pl.BlockSpec((tm, bo), lambda i, b: (i, b))
    n_out = n_w + (1 if act else 0)
    return pl.pallas_call(
        body, grid=(T // tm, nb),
        in_specs=[ANY] * n_dep + [pl.BlockSpec((tm, D_MODEL), lambda i, b: (i, 0)),
                                  pl.BlockSpec(g.shape, lambda i, b: (0, 0))]
        + [pl.BlockSpec((None, r, cc), lambda i, b: (b, 0, 0))] * n_w,
        out_specs=[pl.BlockSpec((tm, D_MODEL), lambda i, b: (i, 0))] + [wide] * n_out,
        out_shape=[jax.ShapeDtypeStruct((T, D_MODEL), BF)] + [jax.ShapeDtypeStruct((T, nb * bo), BF)] * n_out,
        scratch_shapes=[pltpu.VMEM((tm, D_MODEL), BF)], name=name, compiler_params=_params(2))(*deps, h, g, *ws)


def _mm_res(name, x, w3, h, g, coef, tm=ROW_TILE):
    T, kx = x.shape
    w2 = w3.reshape(kx, D_MODEL)
    tm = _tile(T, tm)

    def body(x_ref, w_ref, h_ref, g_ref, f_ref, o_ref):
        f = jnp.dot(x_ref[...], w_ref[...], preferred_element_type=F32).astype(BF)
        f_ref[...] = f
        o_ref[...] = h_ref[...] + coef * _rms_fwd(f.astype(F32), g_ref[...])

    row = pl.BlockSpec((tm, D_MODEL), lambda i: (i, 0))
    return pl.pallas_call(
        body, grid=(T // tm,),
        in_specs=[pl.BlockSpec((tm, kx), lambda i: (i, 0)), pl.BlockSpec(w2.shape, lambda i: (0, 0)), row,
                  pl.BlockSpec(g.shape, lambda i: (0, 0))],
        out_specs=[row, row],
        out_shape=[jax.ShapeDtypeStruct((T, D_MODEL), BF), jax.ShapeDtypeStruct((T, D_MODEL), F32)],
        name=name, compiler_params=_params(1))(x, w2, h, g)


def _resbwd_mm(name, dh, f, g, coef, w3, trans_w, act=None, deps=(), tm=2 * ROW_TILE):
    T = dh.shape[0]
    nb, r, cc = w3.shape
    bo = r if trans_w else cc
    tm = _tile(T, tm)
    n_dep, n_act = len(deps), 2 if act else 0

    def body(*refs):
        refs = refs[n_dep:]
        dh_ref, f_ref, g_ref, w_ref = refs[:4]
        df_ref, dg_ref = refs[4 + n_act], refs[5 + n_act]
        df_s = refs[-1]
        i, b = pl.program_id(0), pl.program_id(1)

        @pl.when(b == 0)
        def _():
            dg = jnp.zeros((1, D_MODEL), F32)
            for c in range(tm // EPI_ROWS):
                rows = slice(c * EPI_ROWS, (c + 1) * EPI_ROWS)
                dx, dg_c = _rms_bwd(f_ref[rows, :].astype(F32), g_ref[...], coef * dh_ref[rows, :])
                df_s[rows, :] = dx.astype(BF)
                df_ref[rows, :] = dx.astype(BF)
                dg = dg + dg_c
            _acc_rows(dg_ref, dg, i == 0)

        if act:
            for j in range(bo // MXU_TILE):
                cols = slice(j * MXU_TILE, (j + 1) * MXU_TILE)
                prod = lax.dot_general(df_s[...], w_ref[cols, :], _NT, preferred_element_type=F32)
                val, grad = _silu_and_grad(refs[4][:, cols].astype(F32))
                refs[6 + n_act][:, cols] = (prod * refs[5][:, cols].astype(F32) * grad).astype(BF)
                refs[7 + n_act][:, cols] = (prod * val).astype(BF)
        else:
            refs[6][...] = lax.dot_general(df_s[...], w_ref[...], _NT if trans_w else _NN,
                                           preferred_element_type=F32).astype(BF)

    row = pl.BlockSpec((tm, D_MODEL), lambda i, b: (i, 0))
    wide = pl.BlockSpec((tm, bo), lambda i, b: (i, b))
    vec = pl.BlockSpec((1, D_MODEL), lambda i, b: (0, 0))
    n_prod = 2 if act else 1
    return pl.pallas_call(
        body, grid=(T // tm, nb),
        in_specs=[ANY] * n_dep + [row, row, vec, pl.BlockSpec((None, r, cc), lambda i, b: (b, 0, 0))] + [wide] * n_act,
        out_specs=[row, vec] + [wide] * n_prod,
        out_shape=[jax.ShapeDtypeStruct((T, D_MODEL), BF), jax.ShapeDtypeStruct((1, D_MODEL), F32)]
        + [jax.ShapeDtypeStruct((T, nb * bo), BF)] * n_prod,
        scratch_shapes=[pltpu.VMEM((tm, D_MODEL), BF)], name=name,
        compiler_params=_params(2))(*deps, dh, f, g, w3, *(act or ()))


def _dn_prenorm(name, pairs, trans_w, dh, h, g, tm=2 * ROW_TILE):
    T = dh.shape[0]
    nb, r, cc = pairs[0][1].shape
    bw = cc if trans_w else r
    tm = _tile(T, tm)
    n_p = len(pairs)

    def body(*refs):
        x_refs, w_refs = refs[:n_p], refs[n_p:2 * n_p]
        dh_ref, h_ref, g_ref, o_ref, dg_ref, acc = refs[2 * n_p:]
        i, b = pl.program_id(0), pl.program_id(1)

        @pl.when(b == 0)
        def _():
            acc[...] = jnp.zeros_like(acc)

        for x_ref, w_ref in zip(x_refs, w_refs):
            acc[...] += lax.dot_general(x_ref[...], w_ref[...], _NT if trans_w else _NN, preferred_element_type=F32)

        @pl.when(b == nb - 1)
        def _():
            dg = jnp.zeros((1, D_MODEL), F32)
            for c in range(tm // EPI_ROWS):
                rows = slice(c * EPI_ROWS, (c + 1) * EPI_ROWS)
                dx, dg_c = _rms_bwd(h_ref[rows, :], g_ref[...], acc[rows, :])
                o_ref[rows, :] = dh_ref[rows, :] + dx
                dg = dg + dg_c
            _acc_rows(dg_ref, dg, i == 0)

    row = pl.BlockSpec((tm, D_MODEL), lambda i, b: (i, 0))
    vec = pl.BlockSpec((1, D_MODEL), lambda i, b: (0, 0))
    return pl.pallas_call(
        body, grid=(T // tm, nb),
        in_specs=[pl.BlockSpec((tm, bw), lambda i, b: (i, b))] * n_p
        + [pl.BlockSpec((None, r, cc), lambda i, b: (b, 0, 0))] * n_p + [row, row, vec],
        out_specs=[row, vec],
        out_shape=[jax.ShapeDtypeStruct((T, D_MODEL), F32), jax.ShapeDtypeStruct((1, D_MODEL), F32)],
        scratch_shapes=[pltpu.VMEM((tm, D_MODEL), F32)], name=name,
        compiler_params=_params(2))(*[p[0] for p in pairs], *[p[1] for p in pairs], dh, h, g)


def _pool_apply(x, win, row):
    s, k = x, 1
    while k < win:
        s = s + jnp.where(row >= k, pltpu.roll(s, k, 0), 0.0)
        k *= 2
    return s / jnp.minimum(row + 1, win).astype(F32) - x


def _pool_apply_t(dp, win, row):
    T = dp.shape[0]
    s, k = dp / jnp.minimum(row + 1, win).astype(F32), 1
    while k < win:
        s = s + jnp.where(row < T - k, pltpu.roll(s, T - k, 0), 0.0)
        k *= 2
    return s - dp


def _pool_fwd(z, w, scale):
    T = z.shape[0]

    def body(z_ref, w_ref, s_ref, o_ref):
        row = lax.broadcasted_iota(jnp.int32, (T, LANES), 0)
        for gi, win in enumerate(POOL_WINDOWS):
            cols = pl.ds(gi * LANES, LANES)
            pooled = _pool_apply(z_ref[:, cols].astype(F32), win, row)
            y = jnp.dot(pooled.astype(BF), w_ref[gi].astype(BF), preferred_element_type=F32)
            o_ref[:, cols] = (y * s_ref[:, cols]).astype(o_ref.dtype)

    return pl.pallas_call(
        body, grid=(1,),
        in_specs=[pl.BlockSpec((T, 512), lambda i: (0, ZB_POOL)), pl.BlockSpec(w.shape, lambda i: (0, 0, 0)),
                  pl.BlockSpec(scale.shape, lambda i: (0, 0))],
        out_specs=pl.BlockSpec((T, 512), lambda i: (0, 0)), out_shape=jax.ShapeDtypeStruct((T, 512), BF),
        name="pool_fwd", compiler_params=_params(1))(z, w, scale)


def _pool_bwd(dr, z, w, scale):
    T = z.shape[0]

    def body(dr_ref, z_ref, w_ref, s_ref, dz_ref, dw_ref, ds_ref):
        row = lax.broadcasted_iota(jnp.int32, (T, LANES), 0)
        for gi, win in enumerate(POOL_WINDOWS):
            cols = pl.ds(gi * LANES, LANES)
            pooled = _pool_apply(z_ref[:, cols].astype(F32), win, row).astype(BF)
            wg = w_ref[gi].astype(BF)
            y = jnp.dot(pooled, wg, preferred_element_type=F32)
            d = dr_ref[:, cols].astype(F32)
            ds_ref[:, cols] = jnp.sum(d * y, axis=0, keepdims=True)
            dy = (d * s_ref[:, cols]).astype(BF)
            dw_ref[gi] = lax.dot_general(pooled, dy, _TN, preferred_element_type=F32)
            dpooled = lax.dot_general(dy, wg, _NT, preferred_element_type=F32)
            dz_ref[:, cols] = _pool_apply_t(dpooled, win, row).astype(dz_ref.dtype)

    return pl.pallas_call(
        body, grid=(1,),
        in_specs=[pl.BlockSpec((T, 512), lambda i: (0, 0)), pl.BlockSpec((T, 512), lambda i: (0, ZB_POOL)),
                  pl.BlockSpec(w.shape, lambda i: (0, 0, 0)), pl.BlockSpec(scale.shape, lambda i: (0, 0))],
        out_specs=[pl.BlockSpec((T, 512), lambda i: (0, 0)), pl.BlockSpec(w.shape, lambda i: (0, 0, 0)),
                   pl.BlockSpec(scale.shape, lambda i: (0, 0))],
        out_shape=[jax.ShapeDtypeStruct((T, 512), BF), jax.ShapeDtypeStruct(w.shape, F32),
                   jax.ShapeDtypeStruct(scale.shape, F32)],
        name="pool_bwd", compiler_params=_params(1))(dr, z, w, scale)


def _tril(transposed=False):
    r = lax.broadcasted_iota(jnp.int32, (CHUNK, CHUNK), 0)
    c = lax.broadcasted_iota(jnp.int32, (CHUNK, CHUNK), 1)
    return c >= r if transposed else r >= c


def _sgu_fwd(z, ln_g, ln_b, w_s, bias):
    T = z.shape[0]
    tm = _tile(T)

    def body(zu_ref, zv_ref, g_ref, b_ref, w_ref, bias_ref, o_ref):
        gu, _ = _gelu_and_grad(zu_ref[...].astype(F32))
        gv, _ = _gelu_and_grad(zv_ref[...].astype(F32))
        xh, _ = _ln_stats(gv)
        v16 = (xh * g_ref[...] + b_ref[...]).astype(BF)
        tri = _tril()
        for h in range(SGU_HEADS):
            cols = slice(h * LANES, (h + 1) * LANES)
            wh = jnp.where(tri, w_ref[h], 0.0).astype(BF)
            for c in range(tm // CHUNK):
                rows = slice(c * CHUNK, (c + 1) * CHUNK)
                s = jnp.dot(wh, v16[rows, cols], preferred_element_type=F32) + bias_ref[:, cols]
                o_ref[rows, cols] = (gu[rows, cols] * s).astype(o_ref.dtype)

    small = [pl.BlockSpec(a.shape, lambda i, n=a.ndim: (0,) * n) for a in (ln_g, ln_b, w_s, bias)]
    return pl.pallas_call(
        body, grid=(T // tm,),
        in_specs=[pl.BlockSpec((tm, 512), lambda i: (i, ZB_U)), pl.BlockSpec((tm, 512), lambda i: (i, ZB_V))] + small,
        out_specs=pl.BlockSpec((tm, 512), lambda i: (i, 0)), out_shape=jax.ShapeDtypeStruct((T, 512), BF),
        name="sgu_fwd", compiler_params=_params(1))(z, z, ln_g, ln_b, w_s, bias)


def _sgu_bwd(dr, z, ln_g, ln_b, w_s, w_st, bias):
    T = z.shape[0]
    tm = _tile(T)
    n_steps = T // tm

    def body(dr_ref, zu_ref, zv_ref, g_ref, b_ref, w_ref, wt_ref, bias_ref,
             dzu_ref, dzv_ref, dg_ref, db_ref, dw_ref, dbias_ref, dgu_s, dv_s):
        i = pl.program_id(0)

        @pl.when(i == 0)
        def _():
            dg_ref[...] = jnp.zeros_like(dg_ref)
            db_ref[...] = jnp.zeros_like(db_ref)
            dw_ref[...] = jnp.zeros_like(dw_ref)
            dbias_ref[...] = jnp.zeros_like(dbias_ref)

        zu = zu_ref[...].astype(F32)
        zv = zv_ref[...].astype(F32)
        gu, gu_grad = _gelu_and_grad(zu)
        gv, gv_grad = _gelu_and_grad(zv)
        xh, r = _ln_stats(gv)
        v16 = (xh * g_ref[...] + b_ref[...]).astype(BF)
        dr = dr_ref[...].astype(F32)
        tri = _tril()
        for h in range(SGU_HEADS):
            cols = slice(h * LANES, (h + 1) * LANES)
            wh = jnp.where(tri, w_ref[h], 0.0).astype(BF)
            wht = jnp.where(_tril(transposed=True), wt_ref[h], 0.0).astype(BF)
            for c in range(tm // CHUNK):
                rows = slice(c * CHUNK, (c + 1) * CHUNK)
                v_blk = v16[rows, cols]
                s = jnp.dot(wh, v_blk, preferred_element_type=F32) + bias_ref[:, cols]
                ds = dr[rows, cols] * gu[rows, cols]
                dgu_s[rows, cols] = dr[rows, cols] * s
                ds16 = ds.astype(BF)
                dw_ref[h] += jnp.where(tri, lax.dot_general(ds16, v_blk, _NT, preferred_element_type=F32), 0.0)
                dv_s[rows, cols] = jnp.dot(wht, ds16, preferred_element_type=F32)
                dbias_ref[:, cols] += ds
        dzu_ref[...] = (dgu_s[...] * gu_grad).astype(dzu_ref.dtype)
        dgv, dg, db = _ln_bwd(xh, r, g_ref[...], dv_s[...])
        dzv_ref[...] = (dgv * gv_grad).astype(dzv_ref.dtype)
        dg_ref[...] += dg
        db_ref[...] += db

        @pl.when(i == n_steps - 1)
        def _():
            for h in range(SGU_HEADS):
                cols = slice(h * LANES, (h + 1) * LANES)
                tot = jnp.sum(dbias_ref[:, cols], axis=1, keepdims=True)
                dbias_ref[:, cols] = jnp.broadcast_to(tot, (CHUNK, LANES))

    small = (ln_g, ln_b, w_s, w_st, bias)
    small_specs = [pl.BlockSpec(a.shape, lambda i, n=a.ndim: (0,) * n) for a in small]
    return pl.pallas_call(
        body, grid=(n_steps,),
        in_specs=[pl.BlockSpec((tm, 512), lambda i: (i, 0)), pl.BlockSpec((tm, 512), lambda i: (i, ZB_U)),
                  pl.BlockSpec((tm, 512), lambda i: (i, ZB_V))] + small_specs,
        out_specs=[pl.BlockSpec((tm, 512), lambda i: (i, 0)), pl.BlockSpec((tm, 512), lambda i: (i, 0)),
                   pl.BlockSpec((1, 512), lambda i: (0, 0)), pl.BlockSpec((1, 512), lambda i: (0, 0)),
                   pl.BlockSpec(w_s.shape, lambda i: (0, 0, 0)), pl.BlockSpec(bias.shape, lambda i: (0, 0))],
        out_shape=[jax.ShapeDtypeStruct((T, 512), BF), jax.ShapeDtypeStruct((T, 512), BF),
                   jax.ShapeDtypeStruct((1, 512), F32), jax.ShapeDtypeStruct((1, 512), F32),
                   jax.ShapeDtypeStruct(w_s.shape, F32), jax.ShapeDtypeStruct(bias.shape, F32)],
        scratch_shapes=[pltpu.VMEM((tm, 512), F32), pltpu.VMEM((tm, 512), F32)],
        name="sgu_bwd", compiler_params=_params(1))(dr, z, z, ln_g, ln_b, w_s, w_st, bias)


def _conv_fwd(z, convk, l, bias):
    T = z.shape[0]

    def body(za_ref, zb_ref, k_ref, b_ref, o_ref):
        xg = za_ref[...].astype(F32) * _sigmoid(zb_ref[...].astype(F32))
        xp = jnp.concatenate([jnp.zeros((CONV_PAD, LANES), F32), xg], axis=0)
        kw = k_ref[...]
        acc = jnp.broadcast_to(b_ref[...], (T, LANES))
        for s in range(SUBLANES):
            xs = xp if s == 0 else pltpu.roll(xp, s, 0)
            for q in range(CONV_PAD // SUBLANES):
                k = CONV_TAPS - 1 - (SUBLANES * q + s)
                if k >= 0:
                    lo = CONV_PAD - SUBLANES * q
                    acc = acc + kw[k:k + 1, :] * xs[lo:lo + T, :]
        o_ref[...] = acc.astype(o_ref.dtype)

    return pl.pallas_call(
        body, grid=(4,),
        in_specs=[pl.BlockSpec((T, LANES), lambda g: (0, 4 * ZB_A + g)),
                  pl.BlockSpec((T, LANES), lambda g: (0, 4 * ZB_B + g)),
                  pl.BlockSpec((None, CONV_PAD, LANES), lambda g: (g, 0, 0)),
                  pl.BlockSpec((1, LANES), lambda g: (0, g))],
        out_specs=pl.BlockSpec((T, LANES), lambda g: (0, g)), out_shape=jax.ShapeDtypeStruct((T, 512), BF),
        name="conv_fwd", compiler_params=_params(1))(z, z, convk[l], bias)


def _conv_bwd(dy, z, convk, l):
    T = z.shape[0]

    def body(dy_ref, za_ref, zb_ref, k_ref, dza_ref, dzb_ref, dk_ref, db_ref):
        a = za_ref[...].astype(F32)
        sg = _sigmoid(zb_ref[...].astype(F32))
        d = dy_ref[...].astype(F32)
        kw = k_ref[...]
        xp = jnp.concatenate([jnp.zeros((CONV_PAD, LANES), F32), a * sg], axis=0)
        dp = jnp.concatenate([d, jnp.zeros((CONV_PAD, LANES), F32)], axis=0)
        dxg = jnp.zeros((T, LANES), F32)
        dk_ref[...] = jnp.zeros_like(dk_ref)
        for s in range(SUBLANES):
            xs = xp if s == 0 else pltpu.roll(xp, s, 0)
            ds = dp if s == 0 else pltpu.roll(dp, T + CONV_PAD - s, 0)
            for q in range(CONV_PAD // SUBLANES):
                k = CONV_TAPS - 1 - (SUBLANES * q + s)
                if k >= 0:
                    lo = CONV_PAD - SUBLANES * q
                    dk_ref[k:k + 1, :] = jnp.sum(d * xs[lo:lo + T, :], axis=0, keepdims=True)
                    dxg = dxg + kw[k:k + 1, :] * ds[SUBLANES * q:SUBLANES * q + T, :]
        db_ref[...] = jnp.sum(d, axis=0, keepdims=True)
        dza_ref[...] = (dxg * sg).astype(dza_ref.dtype)
        dzb_ref[...] = (dxg * a * sg * (1.0 - sg)).astype(dzb_ref.dtype)

    col = pl.BlockSpec((T, LANES), lambda g: (0, g))
    return pl.pallas_call(
        body, grid=(4,),
        in_specs=[col, pl.BlockSpec((T, LANES), lambda g: (0, 4 * ZB_A + g)),
                  pl.BlockSpec((T, LANES), lambda g: (0, 4 * ZB_B + g)),
                  pl.BlockSpec((None, CONV_PAD, LANES), lambda g: (g, 0, 0))],
        out_specs=[col, col, pl.BlockSpec((CONV_PAD, LANES), lambda g: (0, g)),
                   pl.BlockSpec((1, LANES), lambda g: (0, g))],
        out_shape=[jax.ShapeDtypeStruct((T, 512), BF), jax.ShapeDtypeStruct((T, 512), BF),
                   jax.ShapeDtypeStruct((CONV_PAD, 512), F32), jax.ShapeDtypeStruct((1, 512), F32)],
        name="conv_bwd", compiler_params=_params(1))(dy, z, z, convk[l])


D = D_MODEL


def _ffn_fwd(l, h, S, W, pre, deps=()):
    n, gp, u, a = _norm_mm("ffn_in", h, S[pre + "_pre_g"], [W[pre + "_w_gate"][l], W[pre + "_w_up"][l]], True,
                           act=True, deps=deps)
    f, out = _mm_res("ffn_out", a, W[pre + "_w_down"][l], h, S[pre + "_post_g"], 0.5)
    return out, dict(h=h, n=n, gp=gp, u=u, a=a, f=f)


def _ffn_bwd(l, dh, sv, S, W, G, SG, pre, deps=()):
    df, SG[pre + "_post_g"], dgp, du = _resbwd_mm("ffn_bwd_act", dh, sv["f"], S[pre + "_post_g"], 0.5,
                                                  W[pre + "_w_down"][l], True, act=(sv["gp"], sv["u"]), deps=deps)
    G[pre + "_w_down"] = _mm_tn("ffn_dw_down", sv["a"], df, G[pre + "_w_down"], l, True)
    G[pre + "_w_gate"] = _mm_tn("ffn_dw_gate", dgp, sv["n"], G[pre + "_w_gate"], l, True)
    G[pre + "_w_up"] = _mm_tn("ffn_dw_up", du, sv["n"], G[pre + "_w_up"], l, True)
    dh_in, SG[pre + "_pre_g"] = _dn_prenorm("ffn_bwd_in", [(dgp, W[pre + "_w_gate"][l]), (du, W[pre + "_w_up"][l])],
                                            False, dh, sv["h"], S[pre + "_pre_g"])
    return dh_in


def _gates(zg):
    return [_sigmoid(jnp.concatenate([zg[2 * k].astype(F32), zg[2 * k + 1].astype(F32)], axis=1)) for k in range(3)]


def _merge_fwd(z, rs, ws, tm=ROW_TILE):
    T = z.shape[0]
    tm = _tile(T, tm)
    nb, kk, bw = ws[0].shape

    def body(*refs):
        r_refs, g_refs, w_refs, y_refs, m_ref = refs[:3], refs[3:9], refs[9:12], refs[12:15], refs[15]
        for r_ref, w_ref, y_ref in zip(r_refs, w_refs, y_refs):
            for b in range(nb):
                y_ref[:, b * bw:(b + 1) * bw] = jnp.dot(r_ref[...], w_ref[b],
                                                        preferred_element_type=F32).astype(y_ref.dtype)
        g = _gates([q[...] for q in g_refs])
        m_ref[...] = (g[0] * y_refs[0][...].astype(F32) + g[1] * y_refs[1][...].astype(F32)
                      + g[2] * y_refs[2][...].astype(F32)).astype(m_ref.dtype)

    row = pl.BlockSpec((tm, D_MODEL), lambda i: (i, 0))
    return pl.pallas_call(
        body, grid=(T // tm,),
        in_specs=[pl.BlockSpec((tm, kk), lambda i: (i, 0))] * 3
        + [pl.BlockSpec((tm, 512), lambda i, j=j: (i, ZB_GATES + j)) for j in range(6)]
        + [pl.BlockSpec(ws[0].shape, lambda i: (0, 0, 0))] * 3,
        out_specs=[row] * 4, out_shape=[jax.ShapeDtypeStruct((T, D_MODEL), BF)] * 4,
        name="mix_merge", compiler_params=_params(1))(*rs, *[z] * 6, *ws)


def _merge_bwd(dmerged, z, ys, ws, tm=ROW_TILE // 2):
    T = z.shape[0]
    tm = _tile(T, tm)
    nb, kk, bw = ws[0].shape

    def body(*refs):
        dm_ref, g_refs, y_refs, w_refs = refs[0], refs[1:7], refs[7:10], refs[10:13]
        dy_refs, dzg_ref, dr_refs = refs[13:16], refs[16], refs[17:20]
        dm = dm_ref[...].astype(F32)
        g = _gates([q[...] for q in g_refs])
        for k in range(3):
            dy_refs[k][...] = (dm * g[k]).astype(BF)
            dzg_ref[:, k * D_MODEL:(k + 1) * D_MODEL] = (dm * y_refs[k][...].astype(F32) * g[k]
                                                         * (1.0 - g[k])).astype(BF)
            dr = None
            for b in range(nb):
                p = lax.dot_general(dy_refs[k][:, b * bw:(b + 1) * bw], w_refs[k][b], _NT,
                                    preferred_element_type=F32)
                dr = p if dr is None else dr + p
            dr_refs[k][...] = dr.astype(BF)

    row = pl.BlockSpec((tm, D_MODEL), lambda i: (i, 0))
    return pl.pallas_call(
        body, grid=(T // tm,),
        in_specs=[row] + [pl.BlockSpec((tm, 512), lambda i, j=j: (i, ZB_GATES + j)) for j in range(6)] + [row] * 3
        + [pl.BlockSpec(ws[0].shape, lambda i: (0, 0, 0))] * 3,
        out_specs=[row] * 3 + [pl.BlockSpec((tm, 3 * D_MODEL), lambda i: (i, 0))]
        + [pl.BlockSpec((tm, kk), lambda i: (i, 0))] * 3,
        out_shape=[jax.ShapeDtypeStruct((T, D_MODEL), BF)] * 3 + [jax.ShapeDtypeStruct((T, 3 * D_MODEL), BF)]
        + [jax.ShapeDtypeStruct((T, kk), BF)] * 3,
        name="mix_merge_bwd", compiler_params=_params(1))(dmerged, *[z] * 6, *ys, *ws)


def _mix_fwd(l, h, S, W, deps=()):
    n, z = _norm_mm("mix_in", h, S["mix_pre_g"], [W["w_in"][l]], False, deps=deps)
    r_pool = _pool_fwd(z, S["pool_w"], S["pool_scale"])
    r_sgu = _sgu_fwd(z, S["sgu_ln_g"], S["sgu_ln_b"], S["sgu_w_s"], S["sgu_bias"])
    yc = _conv_fwd(z, W["conv_dw_k"], l, S["conv_dw_b"])

    def ln_silu(y, g, b):
        xh, _ = _ln_stats(y.astype(F32))
        return (_silu_and_grad(xh * g + b)[0],)

    r_conv = _rowwise("conv_ln", ln_silu, [(yc, 512, 0)], [S["conv_ln_g"], S["conv_ln_b"]], [(512, BF)])[0]
    y_pool, y_sgu, y_conv, merged = _merge_fwd(z, (r_pool, r_sgu, r_conv),
                                               [W["w_%s_out" % br][l] for br in ("pool", "sgu", "conv")])
    o, out = _mm_res("mix_out", merged, W["w_out"][l], h, S["mix_post_g"], 1.0)
    return out, dict(h=h, n=n, z=z, r_pool=r_pool, r_sgu=r_sgu, yc=yc, r_conv=r_conv, y_pool=y_pool, y_sgu=y_sgu,
                     y_conv=y_conv, merged=merged, o=o)


def _mix_bwd(l, dh, sv, S, W, G, SG, deps=()):
    z = sv["z"]
    do, SG["mix_post_g"], dmerged = _resbwd_mm("mix_bwd_out", dh, sv["o"], S["mix_post_g"], 1.0,
                                               W["w_out"][l].reshape(1, D, D), True, deps=deps)
    G["w_out"] = _mm_tn("mix_dw_out", sv["merged"], do, G["w_out"], l, True)

    branches = ("pool", "sgu", "conv")
    res = _merge_bwd(dmerged, z, [sv["y_" + br] for br in branches], [W["w_%s_out" % br][l] for br in branches])
    dzg, dr = res[3], dict(zip(branches, res[4:]))
    for br, dy in zip(branches, res[:3]):
        wn = "w_%s_out" % br
        G[wn] = _mm_tn("branch_dw", sv["r_" + br], dy, G[wn], l, False)
    dz_pool, SG["pool_w"], SG["pool_scale"] = _pool_bwd(dr["pool"], z, S["pool_w"], S["pool_scale"])
    dzu, dzv, SG["sgu_ln_g"], SG["sgu_ln_b"], SG["sgu_w_s"], dbias = _sgu_bwd(
        dr["sgu"], z, S["sgu_ln_g"], S["sgu_ln_b"], S["sgu_w_s"], S["sgu_w_st"], S["sgu_bias"])
    SG["sgu_b_s"] = dbias[:, ::LANES].T

    def ln_silu_bwd(d, y, g, b):
        xh, r = _ln_stats(y.astype(F32))
        _, grad = _silu_and_grad(xh * g + b)
        return _ln_bwd(xh, r, g, d.astype(F32) * grad)

    dyc, SG["conv_ln_g"], SG["conv_ln_b"] = _rowwise(
        "conv_ln_bwd", ln_silu_bwd, [(dr["conv"], 512, 0), (sv["yc"], 512, 0)], [S["conv_ln_g"], S["conv_ln_b"]],
        [(512, BF)], [(1, 512), (1, 512)])
    dza, dzb, SG["conv_dw_k"], SG["conv_dw_b"] = _conv_bwd(dyc, z, W["conv_dw_k"], l)
    dz = jnp.concatenate([dz_pool, dzu, dzv, dza, dzb, dzg], axis=1)
    G["w_in"] = _mm_tn("mix_dw_in", sv["n"], dz, G["w_in"], l, False)
    dh_in, SG["mix_pre_g"] = _dn_prenorm("mix_bwd_in", [(dz, W["w_in"][l])], True, dh, sv["h"], S["mix_pre_g"])
    return dh_in


def _ple_out(h, p, gp, w3, g, tm=ROW_TILE):
    T, kp = p.shape
    nb, _, bw = w3.shape
    tm = _tile(T, tm)

    def body(h_ref, p_ref, gp_ref, w_ref, g_ref, e_ref, o_ref):
        p16 = p_ref[...].astype(BF)
        for b in range(nb):
            e_ref[:, b * bw:(b + 1) * bw] = jnp.dot(p16, w_ref[b], preferred_element_type=F32).astype(BF)
        q = _sigmoid(gp_ref[...].astype(F32)) * e_ref[...].astype(F32)
        o_ref[...] = h_ref[...] + _rms_fwd(q, g_ref[...])

    row = pl.BlockSpec((tm, D_MODEL), lambda i: (i, 0))
    return pl.pallas_call(
        body, grid=(T // tm,),
        in_specs=[row, pl.BlockSpec((tm, kp), lambda i: (i, 0)), row, pl.BlockSpec(w3.shape, lambda i: (0, 0, 0)),
                  pl.BlockSpec(g.shape, lambda i: (0, 0))],
        out_specs=[row, row],
        out_shape=[jax.ShapeDtypeStruct((T, D_MODEL), BF), jax.ShapeDtypeStruct((T, D_MODEL), F32)],
        name="ple_out", compiler_params=_params(1))(h, p, gp, w3, g)


def _ple_fwd(l, h, p_l, S, W, deps=()):
    n, gp = _norm_mm("ple_in", h, S["ple_pre_g"], [W["ple_w_gate"][l].reshape(1, D, D)], False, deps=deps)
    e, out = _ple_out(h, p_l, gp, W["ple_w_proj"][l], S["ple_post_g"])
    return out, dict(h=h, n=n, e=e, gp=gp, p=p_l)


def _ple_bwd(l, dh, sv, S, W, G, SG, deps=()):
    def res_bwd(d, ee, g, gg):
        sg = _sigmoid(g.astype(F32))
        ee = ee.astype(F32)
        dq, dg = _rms_bwd(sg * ee, gg, d)
        return dq * sg, dq * ee * sg * (1.0 - sg), dg

    de, dgp, SG["ple_post_g"] = _rowwise("ple_res_bwd", res_bwd, [(dh, D, 0), (sv["e"], D, 0), (sv["gp"], D, 0)],
                                         [S["ple_post_g"]], [(D, BF), (D, BF)], [(1, D)], deps=deps)
    G["ple_w_proj"] = _mm_tn("ple_dw_proj", sv["p"], de, G["ple_w_proj"], l, False)
    G["ple_w_gate"] = _mm_tn("ple_dw_gate", sv["n"], dgp, G["ple_w_gate"], l, True)
    dh_in, SG["ple_pre_g"] = _dn_prenorm("ple_bwd_in", [(dgp, W["ple_w_gate"][l].reshape(1, D, D))], True, dh,
                                         sv["h"], S["ple_pre_g"])
    return dh_in


def _layer_small(a, l):
    S = {}
    for name in SMALL:
        v = a[name][l]
        S[name] = v.reshape(1, -1) if v.ndim == 1 else v
    S["sgu_w_st"] = jnp.swapaxes(S["sgu_w_s"], 1, 2)
    S["sgu_bias"] = jnp.repeat(S["sgu_b_s"].T, LANES, axis=1)
    return S


def _layer_fwd(l, h, p_l, S, W, deps=(), hooks=None):
    hooks = hooks or {}

    def after(part, hv):
        return hooks[part](hv) if part in hooks else ()

    h, sv1 = _ffn_fwd(l, h, S, W, "ffn1", deps)
    h, sv2 = _mix_fwd(l, h, S, W, after("ffn1", h))
    h, sv3 = _ffn_fwd(l, h, S, W, "ffn2", after("mix", h))
    h, sv4 = _ple_fwd(l, h, p_l, S, W, after("ffn2", h))
    return h, (sv1, sv2, sv3, sv4)


def _layer_bwd(l, dh, sv, S, W, G, deps=(), mid=None):
    SG = {}
    dh = _ple_bwd(l, dh, sv[3], S, W, G, SG, deps)
    dh = _ffn_bwd(l, dh, sv[2], S, W, G, SG, "ffn2")
    dh = _mix_bwd(l, dh, sv[1], S, W, G, SG, mid(dh) if mid else ())
    dh = _ffn_bwd(l, dh, sv[0], S, W, G, SG, "ffn1")
    return dh, SG


HBM = pl.BlockSpec(memory_space=pltpu.HBM)
SEM = pl.BlockSpec(memory_space=pltpu.SEMAPHORE)
SIDE_EFFECT = pltpu.SideEffectType.DATAFLOW_SIDE_EFFECTING


def _place():
    x, y, c = lax.axis_index("x"), lax.axis_index("y"), lax.axis_index("c")
    chips = [(1 - x, y), (x, 1 - y), (1 - x, 1 - y)]
    return x, y, c, chips


def _remote(src, dst, send_sem, recv_sem, to):
    return pltpu.make_async_remote_copy(src_ref=src, dst_ref=dst, send_sem=send_sem, recv_sem=recv_sem,
                                        device_id=to, device_id_type=MESH)


def _split_start(name, plan, bufs, deps):
    count, fn = plan
    n, nd = len(bufs), len(deps)

    def body(*refs):
        send, recv = refs[nd + n], refs[nd + n + 1]
        x, y, c, chips = _place()
        for k, (src, dst, _, to) in enumerate(fn(refs[nd:nd + n], x, y, c, chips)):
            _remote(src, dst, send.at[k], recv.at[k], to).start()
        refs[-1][...] = jnp.zeros_like(refs[-1])

    res = pl.pallas_call(
        body, in_specs=[ANY] * nd + [HBM] * n,
        out_specs=[SEM, SEM] + [HBM] * n + [pl.BlockSpec(memory_space=pltpu.VMEM)],
        out_shape=[pltpu.SemaphoreType.DMA((count,)), pltpu.SemaphoreType.DMA((count,))]
        + [pltpu.HBM(b.shape, b.dtype) for b in bufs] + [jax.ShapeDtypeStruct((8, LANES), F32)],
        input_output_aliases={nd + i: 2 + i for i in range(n)}, name=name,
        compiler_params=pltpu.CompilerParams(has_side_effects=SIDE_EFFECT),
    )(*deps, *[pltpu.with_memory_space_constraint(b, pltpu.HBM) for b in bufs])
    return (res[0], res[1]), list(res[2:2 + n]), res[-1]


def _split_wait(name, plan, sems, bufs, after):
    _, fn = plan
    n = len(bufs)

    def body(*refs):
        send, recv = refs[n], refs[n + 1]
        x, y, c, chips = _place()
        for k, (src, _, land, to) in enumerate(fn(refs[:n], x, y, c, chips)):
            cp = _remote(src, land, send.at[k], recv.at[k], to)
            cp.wait_send()
            cp.wait_recv()

    res = pl.pallas_call(
        body, in_specs=[HBM] * n + [SEM, SEM] + [ANY] * len(after), out_specs=[HBM] * n,
        out_shape=[pltpu.HBM(b.shape, b.dtype) for b in bufs], input_output_aliases={i: i for i in range(n)},
        name=name, compiler_params=pltpu.CompilerParams(has_side_effects=SIDE_EFFECT))(*bufs, *sems, *after)
    return list(res)


def _gather_plans(n):
    def across(b, x, y, c, chips):
        me, out = 2 * x + y, []
        for a in range(n):
            rh = b[a].shape[1] // 2
            mine = b[a].at[me, pl.ds(c * rh, rh)]
            for cx, cy in chips:
                out.append((mine, mine, b[a].at[2 * cx + cy, pl.ds(c * rh, rh)], (cx, cy, c)))
        return out

    def to_sibling(b, x, y, c, chips):
        out = []
        for a in range(n):
            rh = b[a].shape[1] // 2
            for cx, cy in chips:
                piece = b[a].at[2 * cx + cy, pl.ds(c * rh, rh)]
                out.append((piece, piece, b[a].at[2 * cx + cy, pl.ds((1 - c) * rh, rh)], (x, y, 1 - c)))
        return out

    return (3 * n, across), (3 * n, to_sibling)


def _pair_plan(n):
    def fn(b, x, y, c, chips):
        out = []
        for a in range(n):
            rh = b[a].shape[1] // 2
            out.append((b[a].at[:, pl.ds((1 - c) * rh, rh)], b[n + a], b[n + a], (x, y, 1 - c)))
        return out

    return n, fn


def _cross_plan(n):
    def fn(b, x, y, c, chips):
        out = []
        for a in range(n):
            for j, (cx, cy) in enumerate(chips):
                out.append((b[a].at[2 * cx + cy], b[n + a].at[j], b[n + a].at[j], (cx, cy, c)))
        return out

    return 3 * n, fn


def _share_plan(n, l):
    def fn(b, x, y, c, chips):
        out = []
        for a in range(n):
            rh = b[a].shape[1] // 2
            mine = b[a].at[l, pl.ds(c * rh, rh)]
            out.append((mine, mine, b[a].at[l, pl.ds((1 - c) * rh, rh)], (x, y, 1 - c)))
        return out

    return n, fn


def _exchange_call(name, body, ins, out_shapes, n_remote, in_place=False):
    scratch = [pltpu.SemaphoreType.DMA((n_remote,)), pltpu.SemaphoreType.DMA((n_remote,))]
    aliases = {i: i for i in range(len(ins))} if in_place else {}
    return pl.pallas_call(body, in_specs=[ANY] * len(ins), out_specs=[ANY] * len(out_shapes), out_shape=out_shapes,
                          scratch_shapes=scratch, input_output_aliases=aliases, name=name)(*ins)


def _peers(x, y, c):
    return [(1 - x if m & 4 else x, 1 - y if m & 2 else y, 1 - c if m & 1 else c) for m in range(1, 8)]


def _scatter_small(v3):
    def body(v_ref, o_ref, send, recv):
        x, y, c, _ = _place()
        cps = []
        for m, (px, py, pc) in enumerate(_peers(x, y, c)):
            cps.append(_remote(v_ref.at[4 * px + 2 * py + pc], o_ref.at[m], send.at[m], recv.at[m], (px, py, pc)))
            cps[-1].start()
        for cp in cps:
            cp.wait()

    return _exchange_call("scatter_small", body, [v3], [jax.ShapeDtypeStruct((7,) + v3.shape[1:], v3.dtype)], 7)[0]


def _gather_small(buf):
    def body(_, o_ref, send, recv):
        x, y, c, _ = _place()
        mine = o_ref.at[4 * x + 2 * y + c]
        peers = _peers(x, y, c)
        cps = []
        for m, to in enumerate(peers):
            cps.append(_remote(mine, mine, send.at[m], recv.at[m], to))
            cps[-1].start()
        for m, (px, py, pc) in enumerate(peers):
            slab = o_ref.at[4 * px + 2 * py + pc]
            _remote(slab, slab, send.at[m], recv.at[m], (px, py, pc)).wait_recv()
            cps[m].wait_send()

    return _exchange_call("gather_small", body, [buf], [jax.ShapeDtypeStruct(buf.shape, buf.dtype)], 7,
                          in_place=True)[0]


def _allreduce_small(v, pos):
    rows = v.shape[0]
    rs = rows // 8
    v3 = v.reshape(8, rs, LANES)
    got = _scatter_small(v3)
    tm = _tile(rs)
    ins = [(v3, (None, tm, LANES), lambda i, p: (p[2], i, 0))]
    ins += [(got, (None, tm, LANES), lambda i, p, m=m: (m, i, 0)) for m in range(7)]
    buf = _tiled("sum_small", lambda *t: (((((((t[0] + t[1]) + t[2]) + t[3]) + t[4]) + t[5]) + t[6]) + t[7],),
                 (rs // tm,), pos, ins, [((8, rs, LANES), F32, (None, tm, LANES), lambda i, p: (p[2], i, 0))])[0]
    return _gather_small(buf).reshape(rows, LANES)


ADD_ROWS = 128


def _multi_tiled(name, fn, pos, groups, in_place=False):
    steps = max(g[1] for g in groups)
    flat_in, in_specs, out_specs, out_shape, counts, dests = [], [], [], [], [], []
    for ins, n_t, (shape, dtype, oidx, dest) in groups:
        for arr, idx in ins:
            flat_in.append(arr)
            in_specs.append(pl.BlockSpec((ADD_ROWS, arr.shape[1]),
                                         lambda i, p, idx=idx, n_t=n_t: (idx(jnp.minimum(i, n_t - 1), p), 0)))
        out_specs.append(pl.BlockSpec((ADD_ROWS, shape[1]),
                                      lambda i, p, oidx=oidx, n_t=n_t: (oidx(jnp.minimum(i, n_t - 1), p), 0)))
        out_shape.append(jax.ShapeDtypeStruct(shape, dtype))
        counts.append((len(ins), n_t))
        dests.append(dest)
    n_in = len(flat_in)
    extra = dests if in_place else []

    def body(_, *refs):
        outs = refs[n_in + len(extra):]
        k = 0
        for (n_a, n_t), o_ref in zip(counts, outs):
            tiles = refs[k:k + n_a]
            k += n_a

            @pl.when(pl.program_id(0) < n_t)
            def _(tiles=tiles, o_ref=o_ref):
                o_ref[...] = fn(*[t[...] for t in tiles]).astype(o_ref.dtype)

    spec = pltpu.PrefetchScalarGridSpec(num_scalar_prefetch=1, grid=(steps,),
                                        in_specs=in_specs + [ANY] * len(extra), out_specs=out_specs)
    return pl.pallas_call(body, grid_spec=spec, out_shape=out_shape,
                          input_output_aliases={1 + n_in + k: k for k in range(len(extra))}, name=name,
                          compiler_params=_params(1))(pos, *flat_in, *extra)


def _add_pair(grads, got, pos):
    groups = []
    for g, q in zip(grads, got):
        nb, R, C = g.shape
        rh = R // 2
        nh = rh // ADD_ROWS
        groups.append(([(g.reshape(nb * R, C), lambda t, p, nh=nh: (t // nh) * 2 * nh + p[1] * nh + t % nh),
                        (q.reshape(nb * rh, C), lambda t, p: t)], nb * nh,
                       ((nb * rh, C), BF, lambda t, p: t, None)))
    res = _multi_tiled("rs_add_pair", lambda u, w: u.astype(F32) + w.astype(F32), pos, groups)
    return [t.reshape(q.shape) for t, q in zip(res, got)]


def _add_chips(parts, slots, reduced, l, pos):
    def add(own, s0, s1, s2):
        return ((own.astype(F32) + s0.astype(F32)) + s1.astype(F32)) + s2.astype(F32)

    groups = []
    for t, s, red in zip(parts, slots, reduced):
        nb, rh, C = t.shape
        L = red.shape[0]
        nh = rh // ADD_ROWS
        ins = [(t.reshape(nb * rh, C), lambda i, p, nh=nh: p[0] * nh + i)]
        ins += [(s.reshape(3 * rh, C), lambda i, p, j=j, nh=nh: j * nh + i) for j in range(3)]
        groups.append((ins, nh, ((L * 2 * rh, C), F32, lambda i, p, nh=nh: l * 2 * nh + p[1] * nh + i,
                                 red.reshape(L * 2 * rh, C))))
    res = _multi_tiled("rs_add_chips", add, pos, groups, in_place=True)
    return [buf.reshape(red.shape) for buf, red in zip(res, reduced)]


def _adamw_math(w, g, m, v):
    m = ADAM_B1 * m + (1.0 - ADAM_B1) * g
    v = ADAM_B2 * v + (1.0 - ADAM_B2) * (g * g)
    m_hat = m / (1.0 - ADAM_B1 ** ADAM_STEP)
    v_hat = v / (1.0 - ADAM_B2 ** ADAM_STEP)
    return -ADAM_LR * (m_hat / (jnp.sqrt(v_hat) + ADAM_EPS) + ADAM_WD * w), m, v


def _adamw(w, g, m, v, lo=0, hi=None, into=None, deps=()):
    L, R, C = w.shape
    hi = L if hi is None else hi
    tr = _tile(R, 256)
    extra = (list(into) if into else []) + list(deps)
    n_alias = 4 if into else 0

    def body(w_ref, g_ref, m_ref, v_ref, *rest):
        go_ref, d_ref, mo_ref, vo_ref = rest[len(extra):]
        gv = g_ref[...]
        d, mn, vn = _adamw_math(w_ref[...], gv, m_ref[...], v_ref[...])
        go_ref[...] = gv
        d_ref[...] = d
        mo_ref[...] = mn
        vo_ref[...] = vn

    spec = pl.BlockSpec((None, tr, C), lambda l, i: (l + lo, i, 0))
    out = jax.ShapeDtypeStruct(w.shape, F32)
    return pl.pallas_call(body, grid=(hi - lo, R // tr), in_specs=[spec] * 4 + [ANY] * len(extra),
                          out_specs=[spec] * 4, out_shape=[out] * 4,
                          input_output_aliases={4 + k: k for k in range(n_alias)}, name="adamw",
                          compiler_params=_params(2))(w, g, m, v, *extra)


def _pack(parts):
    flat = jnp.concatenate([q.reshape(-1, LANES) for q in parts], axis=0)
    return jnp.pad(flat, ((0, -flat.shape[0] % ROW_TILE), (0, 0)))


def _unpack(flat, like):
    out, r = [], 0
    for q in like:
        n = q.size // LANES
        out.append(flat[r:r + n].reshape(q.shape))
        r += n
    return out


def _train_step(a):
    a = dict(a)
    L = a["ffn1_pre_g"].shape[0]
    x, y, c, _ = _place()
    chip = 2 * x + y
    pos = jnp.stack([chip, c, 2 * chip + c]).astype(jnp.int32)
    for name in TRANSPOSED:
        for pre in ("", "m_", "v_"):
            a[pre + name] = jnp.swapaxes(a[pre + name], 1, 2)
    big = [b[0] for b in BIG]
    gathered = big + ["conv_dw_k"]
    n_w, n_g = len(gathered), len(big)

    own = [_cast_layers("cast_weight", a[name], rp, cp, BF, pos) for name, _, _, _, rp, cp in BIG]
    own.append(_cast_layers("pad_conv_taps", a["conv_dw_k"].reshape(L, CONV_TAPS, LANES), CONV_PAD, LANES, F32, pos))
    W = {name: [None] * L for name in gathered}
    every = list(range(n_w))
    first = every[:3]
    rest = every[3:]

    def gather_first(l, ids, tag, deps):
        return _split_start("gather_a%d%s" % (l, tag), _gather_plans(len(ids))[0], [own[i][l] for i in ids], deps)

    def gather_second(l, ids, tag, state, after):
        across, to_sibling = _gather_plans(len(ids))
        bufs = _split_wait("gather_a%d%s_done" % (l, tag), across, state[0], state[1], after)
        return _split_start("gather_b%d%s" % (l, tag), to_sibling, bufs, [])

    def gather_done(l, ids, tag, state, after):
        to_sibling = _gather_plans(len(ids))[1]
        bufs = _split_wait("gather_b%d%s_done" % (l, tag), to_sibling, state[0], state[1], after)
        for i, buf in zip(ids, bufs):
            W[gathered[i]][l] = buf

    state = gather_second(0, first, "f", gather_first(0, first, "f", []), [])
    gather_done(0, first, "f", state, [])
    box = {"rest": gather_first(0, rest, "r", [])}

    small = [_layer_small(a, l) for l in range(L)]
    h, saved = a["x"][0], []
    for l in range(L):
        hooks = {}
        if l == 0:
            deps = (box["rest"][2],)

            def after_ffn1(hv, box=box):
                gather_done(0, rest, "r", gather_second(0, rest, "r", box["rest"], [hv]), [])
                if L > 1:
                    box["next"] = gather_first(1, every, "", [hv])
                    return (box["next"][2],)
                return ()

            hooks["ffn1"] = after_ffn1
            switch = "ffn2"
        else:
            deps = ()
            if l + 1 < L:
                box["next"] = gather_first(l + 1, every, "", [h])
                deps = (box["next"][2],)
            switch = "mix"
        if l + 1 < L:
            def second(hv, l=l, box=box):
                box["next"] = gather_second(l + 1, every, "", box["next"], [hv])
                return (box["next"][2],)

            hooks[switch] = second
        h, sv = _layer_fwd(l, h, a["p"][l, 0], small[l], W, deps, hooks)
        saved.append(sv)
        if l + 1 < L:
            gather_done(l + 1, every, "", box["next"], [h])

    def loss_fn(yv, t):
        e = yv - t
        return e * (1.0 / D), jnp.sum(e * e, axis=0, keepdims=True)

    dh, lsum = _rowwise("loss", loss_fn, [(h, D, 0), (a["loss_target"][0], D, 0)], [], [(D, F32)], [(1, D)])
    loss = lax.psum(0.5 * jnp.sum(lsum) / D, ("x", "y", "c"))

    G = {name: [jax.ShapeDtypeStruct((N_CHIPS, rp, cp), BF)] * L for name, _, _, _, rp, cp in BIG}
    reduced = [lax.empty((L, rp, cp), F32) for _, _, _, _, rp, cp in BIG]
    pair, cross = _pair_plan(n_g), _cross_plan(n_g)
    small_grads = [None] * L

    def pair_start(l, deps):
        grads = [G[name][l] for name in big]
        lands = [lax.empty((N_CHIPS, g.shape[1] // 2, g.shape[2]), BF) for g in grads]
        return _split_start("rs_pair%d" % l, pair, grads + lands, deps)

    def cross_start(l, state, after):
        bufs = _split_wait("rs_pair%d_done" % l, pair, state[0], state[1], after)
        parts = _add_pair(bufs[:n_g], bufs[n_g:], pos)
        lands = [lax.empty((3,) + t.shape[1:], BF) for t in parts]
        return _split_start("rs_cross%d" % l, cross, parts + lands, [])

    def share_start(l, state, after, reduced):
        bufs = _split_wait("rs_cross%d_done" % l, cross, state[0], state[1], after)
        reduced = _add_chips(bufs[:n_g], bufs[n_g:], reduced, l, pos)
        return _split_start("rs_share%d" % l, _share_plan(n_g, l), reduced, [])

    def share_done(l, state, after):
        return _split_wait("rs_share%d_done" % l, _share_plan(n_g, l), state[0], state[1], after)

    st_pair = st_cross = st_share = None
    for l in reversed(range(L)):
        deps = tuple(s[2] for s in (st_pair, st_share) if s is not None)
        box = {"cross": None}

        def mid(dm, l=l, box=box, st_pair=st_pair, st_share=st_share):
            out = []
            if st_share is not None:
                box["reduced"] = share_done(l + 2, st_share, [dm])
            if st_pair is not None:
                box["cross"] = cross_start(l + 1, st_pair, [dm])
                out.append(box["cross"][2])
            return tuple(out)

        dh, small_grads[l] = _layer_bwd(l, dh, saved[l], small[l], W, G, deps, mid)
        if st_share is not None:
            reduced = box["reduced"]
        st_share = share_start(l + 1, box["cross"], [dh], reduced) if box["cross"] is not None else None
        st_pair = pair_start(l, [dh])
    grad_x = dh
    if st_share is not None:
        reduced = share_done(1, st_share, [])
    st_cross = cross_start(0, st_pair, [])
    small_names = SMALL + ("conv_dw_k",)
    stacked = [jnp.stack([small_grads[l][name] for l in range(L)]) for name in small_names]
    summed = dict(zip(small_names, _unpack(_allreduce_small(_pack(stacked), pos), stacked)))
    upper = {}
    if L > 1:
        for name, red in zip(big, reduced):
            upper[name] = _adamw(a[name], red, a["m_" + name], a["v_" + name], 1, L, deps=[st_cross[2]])
    st_share = share_start(0, st_cross, [r[1] for r in upper.values()] + [summed[small_names[0]]], reduced)
    reduced = share_done(0, st_share, [])
    big_grads = dict(zip(big, reduced))

    grads, deltas, new_m, new_v = {}, {}, {}, {}
    for name in big:
        res = _adamw(a[name], big_grads[name], a["m_" + name], a["v_" + name], 0, 1, upper.get(name))
        if name in TRANSPOSED:
            res = [jnp.swapaxes(r, 1, 2) for r in res]
        grads[name], deltas[name], new_m[name], new_v[name] = res
    taps = lax.dynamic_slice_in_dim(summed["conv_dw_k"], chip * LANES, LANES, axis=2)[:, :CONV_TAPS]
    grads["conv_dw_k"] = taps.reshape(a["conv_dw_k"].shape)
    for name in SMALL:
        grads[name] = summed[name].reshape(a[name].shape)
    shapes = [a[name] for name in small_names]
    res = _adamw(*[_pack([a[pre + name] if pre != "g" else grads[name] for name in small_names])[None]
                   for pre in ("", "g", "m_", "v_")])
    for dst, flat in zip((deltas, new_m, new_v), res[1:]):
        for name, val in zip(small_names, _unpack(flat[0], shapes)):
            dst[name] = val

    return (loss, grad_x[None], *[grads[n] for n in WEIGHTS], *[deltas[n] for n in WEIGHTS],
            *[new_m[n] for n in WEIGHTS], *[new_v[n] for n in WEIGHTS])


def kernel(x, p, ffn1_pre_g, ffn1_w_gate, ffn1_w_up, ffn1_w_down, ffn1_post_g, mix_pre_g, w_in, pool_w, pool_scale, w_pool_out, sgu_ln_g, sgu_ln_b, sgu_w_s, sgu_b_s, w_sgu_out, conv_dw_k, conv_dw_b, conv_ln_g, conv_ln_b, w_conv_out, w_out, mix_post_g, ffn2_pre_g, ffn2_w_gate, ffn2_w_up, ffn2_w_down, ffn2_post_g, ple_w_proj, ple_pre_g, ple_w_gate, ple_post_g, loss_target, m_ffn1_pre_g, m_ffn1_w_gate, m_ffn1_w_up, m_ffn1_w_down, m_ffn1_post_g, m_mix_pre_g, m_w_in, m_pool_w, m_pool_scale, m_w_pool_out, m_sgu_ln_g, m_sgu_ln_b, m_sgu_w_s, m_sgu_b_s, m_w_sgu_out, m_conv_dw_k, m_conv_dw_b, m_conv_ln_g, m_conv_ln_b, m_w_conv_out, m_w_out, m_mix_post_g, m_ffn2_pre_g, m_ffn2_w_gate, m_ffn2_w_up, m_ffn2_w_down, m_ffn2_post_g, m_ple_w_proj, m_ple_pre_g, m_ple_w_gate, m_ple_post_g, v_ffn1_pre_g, v_ffn1_w_gate, v_ffn1_w_up, v_ffn1_w_down, v_ffn1_post_g, v_mix_pre_g, v_w_in, v_pool_w, v_pool_scale, v_w_pool_out, v_sgu_ln_g, v_sgu_ln_b, v_sgu_w_s, v_sgu_b_s, v_w_sgu_out, v_conv_dw_k, v_conv_dw_b, v_conv_ln_g, v_conv_ln_b, v_w_conv_out, v_w_out, v_mix_post_g, v_ffn2_pre_g, v_ffn2_w_gate, v_ffn2_w_up, v_ffn2_w_down, v_ffn2_post_g, v_ple_w_proj, v_ple_pre_g, v_ple_w_gate, v_ple_post_g):
    return _train_step(dict(locals()))
```

```python
import math

import jax
import jax.numpy as jnp
from jax import lax
from jax.experimental import pallas as pl
from jax.experimental.pallas import tpu as pltpu

BF = jnp.bfloat16
F32 = jnp.float32
EPS = 1e-6
D_MODEL = 1024
LANES = 128
SUBLANES = 8
MXU_TILE = 256
N_CHIPS = 4
FFN_SHARD = 704
FFN_SHARD_PAD = 768
POOL_WINDOWS = (2, 4, 8, 16)
SGU_HEADS = 4
CHUNK = 128
CONV_TAPS = 31
CONV_PAD = 32
ROW_TILE = 512
EPI_ROWS = 256
VMEM_LIMIT_BYTES = 56 * 1024 * 1024
ADAM_TILE_ELEMS = 3 * 128 * 1024
ADAM_LR, ADAM_B1, ADAM_B2, ADAM_EPS, ADAM_WD, ADAM_STEP =0.001, 0.9, 0.999, 1e-08, 0.01, 10
MESH = pl.DeviceIdType.MESH
ANY = pl.BlockSpec(memory_space=pl.ANY)

ZB_POOL, ZB_U, ZB_V, ZB_A, ZB_B, ZB_GATES = 0, 1, 2, 3, 4, 5
DZ_HALF = 2816

TRANSPOSED = ("ffn1_w_gate", "ffn1_w_up", "ffn2_w_gate", "ffn2_w_up")
BIG = (
    ("ffn1_w_gate", "row", FFN_SHARD, 1024, FFN_SHARD_PAD, 1024),
    ("ffn1_w_up", "row", FFN_SHARD, 1024, FFN_SHARD_PAD, 1024),
    ("ffn1_w_down", "row", FFN_SHARD, 1024, FFN_SHARD_PAD, 1024),
    ("w_in", "col", 1024, 1408, 1024, 1408),
    ("w_pool_out", "col", 512, 256, 512, 256),
    ("w_sgu_out", "col", 512, 256, 512, 256),
    ("w_conv_out", "col", 512, 256, 512, 256),
    ("w_out", "row", 256, 1024, 256, 1024),
    ("ffn2_w_gate", "row", FFN_SHARD, 1024, FFN_SHARD_PAD, 1024),
    ("ffn2_w_up", "row", FFN_SHARD, 1024, FFN_SHARD_PAD, 1024),
    ("ffn2_w_down", "row", FFN_SHARD, 1024, FFN_SHARD_PAD, 1024),
    ("ple_w_proj", "col", 256, 256, 256, 256),
    ("ple_w_gate", "row", 256, 1024, 256, 1024),
)
SMALL = ("ffn1_pre_g", "ffn1_post_g", "mix_pre_g", "pool_w", "pool_scale", "sgu_ln_g", "sgu_ln_b", "sgu_w_s",
         "sgu_b_s", "conv_dw_b", "conv_ln_g", "conv_ln_b", "mix_post_g", "ffn2_pre_g", "ffn2_post_g",
         "ple_pre_g", "ple_post_g")
WEIGHTS = ("ffn1_pre_g", "ffn1_w_gate", "ffn1_w_up", "ffn1_w_down", "ffn1_post_g", "mix_pre_g", "w_in", "pool_w",
           "pool_scale", "w_pool_out", "sgu_ln_g", "sgu_ln_b", "sgu_w_s", "sgu_b_s", "w_sgu_out", "conv_dw_k",
           "conv_dw_b", "conv_ln_g", "conv_ln_b", "w_conv_out", "w_out", "mix_post_g", "ffn2_pre_g", "ffn2_w_gate",
           "ffn2_w_up", "ffn2_w_down", "ffn2_post_g", "ple_w_proj", "ple_pre_g", "ple_w_gate", "ple_post_g")


def _params(n_grid):
    return pltpu.CompilerParams(dimension_semantics=("arbitrary",) * n_grid, vmem_limit_bytes=VMEM_LIMIT_BYTES)


def _tile(n, cap=ROW_TILE):
    for t in range(min(cap, n) - min(cap, n) % 16, 0, -16):
        if n % t == 0:
            return t
    return n


def _sigmoid(x):
    return 0.5 * jnp.tanh(0.5 * x) + 0.5


def _silu_and_grad(x):
    s = _sigmoid(x)
    return x * s, s * (1.0 + x * (1.0 - s))


def _gelu_and_grad(x):
    cdf = 0.5 * (1.0 + lax.erf(x * (1.0 / math.sqrt(2.0))))
    pdf = jnp.exp(-0.5 * x * x) * (1.0 / math.sqrt(2.0 * math.pi))
    return x * cdf, cdf + x * pdf


def _rms_fwd(x, g):
    return x * lax.rsqrt(jnp.mean(x * x, axis=-1, keepdims=True) + EPS) * g


def _rms_bwd(x, g, dy):
    r = lax.rsqrt(jnp.mean(x * x, axis=-1, keepdims=True) + EPS)
    xh = x * r
    dxh = dy * g
    dx = r * (dxh - xh * jnp.mean(dxh * xh, axis=-1, keepdims=True))
    return dx, jnp.sum(dy * xh, axis=0, keepdims=True)


def _ln_stats(x):
    xc = x - jnp.mean(x, axis=-1, keepdims=True)
    r = lax.rsqrt(jnp.mean(xc * xc, axis=-1, keepdims=True) + EPS)
    return xc * r, r


def _ln_bwd(xh, r, g, dy):
    dxh = dy * g
    dx = r * (dxh - jnp.mean(dxh, axis=-1, keepdims=True) - xh * jnp.mean(dxh * xh, axis=-1, keepdims=True))
    return dx, jnp.sum(dy * xh, axis=0, keepdims=True), jnp.sum(dy, axis=0, keepdims=True)


def _rowwise(name, fn, rows, consts, outs, accs=(), tm=ROW_TILE, deps=()):
    T = rows[0][0].shape[-2]
    tm = _tile(T, tm)
    n_in, n_o, n_dep = len(rows) + len(consts), len(outs), len(deps)

    def body(*refs):
        refs = refs[n_dep:]
        res = fn(*[r[...] for r in refs[:n_in]])
        for ref, val in zip(refs[n_in:n_in + n_o], res[:n_o]):
            ref[...] = val.astype(ref.dtype)
        acc_refs = refs[n_in + n_o:]
        if acc_refs:
            @pl.when(pl.program_id(0) == 0)
            def _():
                for ref, val in zip(acc_refs, res[n_o:]):
                    ref[...] = val

            @pl.when(pl.program_id(0) != 0)
            def _():
                for ref, val in zip(acc_refs, res[n_o:]):
                    ref[...] += val

    in_specs = [ANY] * n_dep
    for row in rows:
        w, cb = row[1], row[2]
        if len(row) == 4:
            in_specs.append(pl.BlockSpec((None, tm, w), lambda i, cb=cb, ld=row[3]: (ld, i, cb)))
        else:
            in_specs.append(pl.BlockSpec((tm, w), lambda i, cb=cb: (i, cb)))
    in_specs += [pl.BlockSpec(c.shape, lambda i: (0, 0)) for c in consts]
    out_specs = [pl.BlockSpec((tm, w), lambda i: (i, 0)) for w, _ in outs]
    out_specs += [pl.BlockSpec(s, lambda i: (0, 0)) for s in accs]
    out_shape = [jax.ShapeDtypeStruct((T, w), dt) for w, dt in outs]
    out_shape += [jax.ShapeDtypeStruct(s, F32) for s in accs]
    return pl.pallas_call(body, grid=(T // tm,), in_specs=in_specs, out_specs=out_specs, out_shape=out_shape,
                          name=name, compiler_params=_params(1))(*deps, *[r[0] for r in rows], *consts)


def _tiled(name, fn, grid, pos, ins, outs):
    n_in = len(ins)

    def body(_, *refs):
        res = fn(*[r[...] for r in refs[:n_in]])
        for ref, val in zip(refs[n_in:], res):
            ref[...] = val.astype(ref.dtype)

    spec = pltpu.PrefetchScalarGridSpec(
        num_scalar_prefetch=1, grid=grid, in_specs=[pl.BlockSpec(bs, im) for _, bs, im in ins],
        out_specs=[pl.BlockSpec(bs, im) for _, _, bs, im in outs])
    return pl.pallas_call(body, grid_spec=spec, out_shape=[jax.ShapeDtypeStruct(s, d) for s, d, _, _ in outs],
                          name=name, compiler_params=_params(len(grid)))(pos, *[a for a, _, _ in ins])


def _cast_layers(name, w, rp, cp, dtype, pos, deps=()):
    L, r, c = w.shape

    def body(_, w_ref, *rest):
        for k, o_ref in enumerate(rest[len(deps):]):
            @pl.when(pl.program_id(0) == k)
            def _(o_ref=o_ref):
                if (rp, cp) != (r, c):
                    o_ref[...] = jnp.zeros_like(o_ref)
                    o_ref[pl.ds(0, r), pl.ds(0, c)] = w_ref[...].astype(dtype)
                else:
                    o_ref[...] = w_ref[...].astype(dtype)

    spec = pltpu.PrefetchScalarGridSpec(
        num_scalar_prefetch=1, grid=(L,),
        in_specs=[pl.BlockSpec((None, r, c), lambda l, p: (l, 0, 0))] + [ANY] * len(deps),
        out_specs=[pl.BlockSpec((None, rp, cp), lambda l, p: (p[0], 0, 0))] * L)
    return pl.pallas_call(body, grid_spec=spec, out_shape=[jax.ShapeDtypeStruct((N_CHIPS, rp, cp), dtype)] * L,
                          name=name, compiler_params=_params(1))(pos, w, *deps)


_NN = (((1,), (0,)), ((), ()))
_NT = (((1,), (1,)), ((), ()))
_TN = (((0,), (0,)), ((), ()))


def _mm_tn(name, a, dy, buf, l, a_blocked, tk=ROW_TILE, first=0):
    T = a.shape[0]
    nb, R, C = buf[l].shape
    extra = [buf[l]] if first else []

    def body(a_ref, dy_ref, *rest):
        rest[-1][...] = lax.dot_general(a_ref[...].astype(BF), dy_ref[...].astype(BF), _TN,
                                        preferred_element_type=F32).astype(BF)

    if a_blocked:
        grid = (nb,)
        in_specs = [pl.BlockSpec((T, R), lambda b: (0, b)), pl.BlockSpec((T, C), lambda b: (0, 0))]
        out_specs = pl.BlockSpec((None, R, C), lambda b: (b, 0, 0))
    else:
        tk = min(tk, R)
        grid = (dy.shape[1] // C, R // tk)
        in_specs = [pl.BlockSpec((T, tk), lambda b, k: (0, k)), pl.BlockSpec((T, C), lambda b, k: (0, b))]
        out_specs = pl.BlockSpec((None, tk, C), lambda b, k: (b + first, k, 0))
    buf = list(buf)
    buf[l] = pl.pallas_call(body, grid=grid, in_specs=in_specs + [ANY] * len(extra), out_specs=out_specs,
                            out_shape=jax.ShapeDtypeStruct((nb, R, C), BF),
                            input_output_aliases={2: 0} if extra else {}, name=name,
                            compiler_params=_params(len(grid)))(a, dy, *extra)
    return buf


def _acc_rows(ref, val, first):
    @pl.when(first)
    def _():
        ref[...] = val

    @pl.when(jnp.logical_not(first))
    def _():
        ref[...] += val


def _norm_mm(name, h, g, ws, trans_w, act=False, deps=(), tm=2 * ROW_TILE):
    T = h.shape[0]
    nb, r, cc = ws[0].shape
    bo = r if trans_w else cc
    tm = _tile(T, tm)
    n_w, n_dep = len(ws), len(deps)

    def body(*refs):
        refs = refs[n_dep:]
        h_ref, g_ref, w_refs = refs[0], refs[1], refs[2:2 + n_w]
        n_ref, o_refs, n_s = refs[2 + n_w], refs[3 + n_w:3 + 2 * n_w], refs[-1]

        @pl.when(pl.program_id(1) == 0)
        def _():
            n = _rms_fwd(h_ref[...].astype(F32), g_ref[...]).astype(BF)
            n_s[...] = n
            n_ref[...] = n

        n = n_s[...]
        prods = []
        for w_ref, o_ref in zip(w_refs, o_refs):
            prods.append(lax.dot_general(n, w_ref[...], _NT if trans_w else _NN,
                                         preferred_element_type=F32).astype(BF))
            o_ref[...] = prods[-1]
        if act:
            refs[3 + 2 * n_w][...] = (_silu_and_grad(prods[0].astype(F32))[0] * prods[1].astype(F32)).astype(BF)

    wide = pl.BlockSpec((tm, bo), lambda i, b: (i, b))
    n_out = n_w + (1 if act else 0)
    return pl.pallas_call(
        body, grid=(T // tm, nb),
        in_specs=[ANY] * n_dep + [pl.BlockSpec((tm, D_MODEL), lambda i, b: (i, 0)),
                                  pl.BlockSpec(g.shape, lambda i, b: (0, 0))]
        + [pl.BlockSpec((None, r, cc), lambda i, b: (b, 0, 0))] * n_w,
        out_specs=[pl.BlockSpec((tm, D_MODEL), lambda i, b: (i, 0))] + [wide] * n_out,
        out_shape=[jax.ShapeDtypeStruct((T, D_MODEL), BF)] + [jax.ShapeDtypeStruct((T, nb * bo), BF)] * n_out,
        scratch_shapes=[pltpu.VMEM((tm, D_MODEL), BF)], name=name, compiler_params=_params(2))(*deps, h, g, *ws)


def _mm_res(name, x, w3, h, g, coef, tm=ROW_TILE):
    T, kx = x.shape
    w2 = w3.reshape(kx, D_MODEL)
    tm = _tile(T, tm)

    def body(x_ref, w_ref, h_ref, g_ref, f_ref, o_ref):
        f = jnp.dot(x_ref[...], w_ref[...], preferred_element_type=F32).astype(BF)
        f_ref[...] = f
        o_ref[...] = h_ref[...] + coef * _rms_fwd(f.astype(F32), g_ref[...])

    row = pl.BlockSpec((tm, D_MODEL), lambda i: (i, 0))
    return pl.pallas_call(
        body, grid=(T // tm,),
        in_specs=[pl.BlockSpec((tm, kx), lambda i: (i, 0)), pl.BlockSpec(w2.shape, lambda i: (0, 0)), row,
                  pl.BlockSpec(g.shape, lambda i: (0, 0))],
        out_specs=[row, row],
        out_shape=[jax.ShapeDtypeStruct((T, D_MODEL), BF), jax.ShapeDtypeStruct((T, D_MODEL), F32)],
        name=name, compiler_params=_params(1))(x, w2, h, g)


def _resbwd_mm(name, dh, f, g, coef, w3, trans_w, act=None, deps=(), tm=2 * ROW_TILE):
    T = dh.shape[0]
    nb, r, cc = w3.shape
    bo = r if trans_w else cc
    tm = _tile(T, tm)
    n_dep, n_act = len(deps), 2 if act else 0

    def body(*refs):
        refs = refs[n_dep:]
        dh_ref, f_ref, g_ref, w_ref = refs[:4]
        df_ref, dg_ref = refs[4 + n_act], refs[5 + n_act]
        df_s = refs[-1]
        i, b = pl.program_id(0), pl.program_id(1)

        @pl.when(b == 0)
        def _():
            dg = jnp.zeros((1, D_MODEL), F32)
            for c in range(tm // EPI_ROWS):
                rows = slice(c * EPI_ROWS, (c + 1) * EPI_ROWS)
                dx, dg_c = _rms_bwd(f_ref[rows, :].astype(F32), g_ref[...], coef * dh_ref[rows, :])
                df_s[rows, :] = dx.astype(BF)
                df_ref[rows, :] = dx.astype(BF)
                dg = dg + dg_c
            _acc_rows(dg_ref, dg, i == 0)

        if act:
            for j in range(bo // MXU_TILE):
                cols = slice(j * MXU_TILE, (j + 1) * MXU_TILE)
                prod = lax.dot_general(df_s[...], w_ref[cols, :], _NT, preferred_element_type=F32)
                val, grad = _silu_and_grad(refs[4][:, cols].astype(F32))
                refs[6 + n_act][:, cols] = (prod * refs[5][:, cols].astype(F32) * grad).astype(BF)
                refs[7 + n_act][:, cols] = (prod * val).astype(BF)
        else:
            refs[6][...] = lax.dot_general(df_s[...], w_ref[...], _NT if trans_w else _NN,
                                           preferred_element_type=F32).astype(BF)

    row = pl.BlockSpec((tm, D_MODEL), lambda i, b: (i, 0))
    wide = pl.BlockSpec((tm, bo), lambda i, b: (i, b))
    vec = pl.BlockSpec((1, D_MODEL), lambda i, b: (0, 0))
    n_prod = 2 if act else 1
    return pl.pallas_call(
        body, grid=(T // tm, nb),
        in_specs=[ANY] * n_dep + [row, row, vec, pl.BlockSpec((None, r, cc), lambda i, b: (b, 0, 0))] + [wide] * n_act,
        out_specs=[row, vec] + [wide] * n_prod,
        out_shape=[jax.ShapeDtypeStruct((T, D_MODEL), BF), jax.ShapeDtypeStruct((1, D_MODEL), F32)]
        + [jax.ShapeDtypeStruct((T, nb * bo), BF)] * n_prod,
        scratch_shapes=[pltpu.VMEM((tm, D_MODEL), BF)], name=name,
        compiler_params=_params(2))(*deps, dh, f, g, w3, *(act or ()))


def _dn_prenorm(name, pairs, trans_w, dh, h, g, tm=2 * ROW_TILE):
    T = dh.shape[0]
    _, r, cc = pairs[0][1].shape
    bw = cc if trans_w else r
    nb = pairs[0][0].shape[1] // bw
    tm = _tile(T, tm)
    n_p = len(pairs)

    def body(*refs):
        x_refs, w_refs = refs[:n_p], refs[n_p:2 * n_p]
        dh_ref, h_ref, g_ref, o_ref, dg_ref, acc = refs[2 * n_p:]
        i, b = pl.program_id(0), pl.program_id(1)

        @pl.when(b == 0)
        def _():
            acc[...] = jnp.zeros_like(acc)

        for x_ref, w_ref in zip(x_refs, w_refs):
            acc[...] += lax.dot_general(x_ref[...], w_ref[...], _NT if trans_w else _NN, preferred_element_type=F32)

        @pl.when(b == nb - 1)
        def _():
            dg = jnp.zeros((1, D_MODEL), F32)
            for c in range(tm // EPI_ROWS):
                rows = slice(c * EPI_ROWS, (c + 1) * EPI_ROWS)
                dx, dg_c = _rms_bwd(h_ref[rows, :], g_ref[...], acc[rows, :])
                o_ref[rows, :] = dh_ref[rows, :] + dx
                dg = dg + dg_c
            _acc_rows(dg_ref, dg, i == 0)

    row = pl.BlockSpec((tm, D_MODEL), lambda i, b: (i, 0))
    once = pl.BlockSpec((tm, D_MODEL), lambda i, b: (i, 0), pipeline_mode=pl.Buffered(1))
    vec = pl.BlockSpec((1, D_MODEL), lambda i, b: (0, 0))
    return pl.pallas_call(
        body, grid=(T // tm, nb),
        in_specs=[pl.BlockSpec((tm, bw), lambda i, b: (i, b))] * n_p
        + [pl.BlockSpec((None, r, cc), lambda i, b, off=p[2]: (b + off, 0, 0)) for p in pairs] + [once, once, vec],
        out_specs=[row, vec],
        out_shape=[jax.ShapeDtypeStruct((T, D_MODEL), F32), jax.ShapeDtypeStruct((1, D_MODEL), F32)],
        scratch_shapes=[pltpu.VMEM((tm, D_MODEL), F32)], name=name,
        compiler_params=_params(2))(*[p[0] for p in pairs], *[p[1] for p in pairs], dh, h, g)


def _pool_apply(x, win, row):
    s, k = x, 1
    while k < win:
        s = s + jnp.where(row >= k, pltpu.roll(s, k, 0), 0.0)
        k *= 2
    return s / jnp.minimum(row + 1, win).astype(F32) - x


def _pool_apply_t(dp, win, row):
    T = dp.shape[0]
    s, k = dp / jnp.minimum(row + 1, win).astype(F32), 1
    while k < win:
        s = s + jnp.where(row < T - k, pltpu.roll(s, T - k, 0), 0.0)
        k *= 2
    return s - dp


def _pool_fwd(z, w, scale):
    T = z.shape[0]

    def body(z_ref, w_ref, s_ref, o_ref):
        row = lax.broadcasted_iota(jnp.int32, (T, LANES), 0)
        for gi, win in enumerate(POOL_WINDOWS):
            cols = pl.ds(gi * LANES, LANES)
            pooled = _pool_apply(z_ref[:, cols].astype(F32), win, row)
            y = jnp.dot(pooled.astype(BF), w_ref[gi].astype(BF), preferred_element_type=F32)
            o_ref[:, cols] = (y * s_ref[:, cols]).astype(o_ref.dtype)

    return pl.pallas_call(
        body, grid=(1,),
        in_specs=[pl.BlockSpec((T, 512), lambda i: (0, ZB_POOL)), pl.BlockSpec(w.shape, lambda i: (0, 0, 0)),
                  pl.BlockSpec(scale.shape, lambda i: (0, 0))],
        out_specs=pl.BlockSpec((T, 512), lambda i: (0, 0)), out_shape=jax.ShapeDtypeStruct((T, 512), BF),
        name="pool_fwd", compiler_params=_params(1))(z, w, scale)


def _pool_bwd(dr, z, w, scale):
    T = z.shape[0]

    def body(dr_ref, z_ref, w_ref, s_ref, dz_ref, dw_ref, ds_ref):
        row = lax.broadcasted_iota(jnp.int32, (T, LANES), 0)
        for gi, win in enumerate(POOL_WINDOWS):
            cols = pl.ds(gi * LANES, LANES)
            pooled = _pool_apply(z_ref[:, cols].astype(F32), win, row).astype(BF)
            wg = w_ref[gi].astype(BF)
            y = jnp.dot(pooled, wg, preferred_element_type=F32)
            d = dr_ref[:, cols].astype(F32)
            ds_ref[:, cols] = jnp.sum(d * y, axis=0, keepdims=True)
            dy = (d * s_ref[:, cols]).astype(BF)
            dw_ref[gi] = lax.dot_general(pooled, dy, _TN, preferred_element_type=F32)
            dpooled = lax.dot_general(dy, wg, _NT, preferred_element_type=F32)
            dz_ref[:, cols] = _pool_apply_t(dpooled, win, row).astype(dz_ref.dtype)

    return pl.pallas_call(
        body, grid=(1,),
        in_specs=[pl.BlockSpec((T, 512), lambda i: (0, 0)), pl.BlockSpec((T, 512), lambda i: (0, ZB_POOL)),
                  pl.BlockSpec(w.shape, lambda i: (0, 0, 0)), pl.BlockSpec(scale.shape, lambda i: (0, 0))],
        out_specs=[pl.BlockSpec((T, 512), lambda i: (0, 0)), pl.BlockSpec(w.shape, lambda i: (0, 0, 0)),
                   pl.BlockSpec(scale.shape, lambda i: (0, 0))],
        out_shape=[jax.ShapeDtypeStruct((T, 512), BF), jax.ShapeDtypeStruct(w.shape, F32),
                   jax.ShapeDtypeStruct(scale.shape, F32)],
        name="pool_bwd", compiler_params=_params(1))(dr, z, w, scale)


def _tril(transposed=False):
    r = lax.broadcasted_iota(jnp.int32, (CHUNK, CHUNK), 0)
    c = lax.broadcasted_iota(jnp.int32, (CHUNK, CHUNK), 1)
    return c >= r if transposed else r >= c


def _sgu_fwd(z, ln_g, ln_b, w_s, bias):
    T = z.shape[0]
    tm = _tile(T)

    def body(zu_ref, zv_ref, g_ref, b_ref, w_ref, bias_ref, o_ref):
        gu, _ = _gelu_and_grad(zu_ref[...].astype(F32))
        gv, _ = _gelu_and_grad(zv_ref[...].astype(F32))
        xh, _ = _ln_stats(gv)
        v16 = (xh * g_ref[...] + b_ref[...]).astype(BF)
        tri = _tril()
        for h in range(SGU_HEADS):
            cols = slice(h * LANES, (h + 1) * LANES)
            wh = jnp.where(tri, w_ref[h], 0.0).astype(BF)
            for c in range(tm // CHUNK):
                rows = slice(c * CHUNK, (c + 1) * CHUNK)
                s = jnp.dot(wh, v16[rows, cols], preferred_element_type=F32) + bias_ref[:, cols]
                o_ref[rows, cols] = (gu[rows, cols] * s).astype(o_ref.dtype)

    small = [pl.BlockSpec(a.shape, lambda i, n=a.ndim: (0,) * n) for a in (ln_g, ln_b, w_s, bias)]
    return pl.pallas_call(
        body, grid=(T // tm,),
        in_specs=[pl.BlockSpec((tm, 512), lambda i: (i, ZB_U)), pl.BlockSpec((tm, 512), lambda i: (i, ZB_V))] + small,
        out_specs=pl.BlockSpec((tm, 512), lambda i: (i, 0)), out_shape=jax.ShapeDtypeStruct((T, 512), BF),
        name="sgu_fwd", compiler_params=_params(1))(z, z, ln_g, ln_b, w_s, bias)


def _sgu_bwd(dr, z, ln_g, ln_b, w_s, w_st, bias):
    T = z.shape[0]
    tm = _tile(T)
    n_steps = T // tm

    def body(dr_ref, zu_ref, zv_ref, g_ref, b_ref, w_ref, wt_ref, bias_ref,
             dzu_ref, dzv_ref, dg_ref, db_ref, dw_ref, dbias_ref, dgu_s, dv_s):
        i = pl.program_id(0)

        @pl.when(i == 0)
        def _():
            dg_ref[...] = jnp.zeros_like(dg_ref)
            db_ref[...] = jnp.zeros_like(db_ref)
            dw_ref[...] = jnp.zeros_like(dw_ref)
            dbias_ref[...] = jnp.zeros_like(dbias_ref)

        zu = zu_ref[...].astype(F32)
        zv = zv_ref[...].astype(F32)
        gu, gu_grad = _gelu_and_grad(zu)
        gv, gv_grad = _gelu_and_grad(zv)
        xh, r = _ln_stats(gv)
        v16 = (xh * g_ref[...] + b_ref[...]).astype(BF)
        dr = dr_ref[...].astype(F32)
        tri = _tril()
        for h in range(SGU_HEADS):
            cols = slice(h * LANES, (h + 1) * LANES)
            wh = jnp.where(tri, w_ref[h], 0.0).astype(BF)
            wht = jnp.where(_tril(transposed=True), wt_ref[h], 0.0).astype(BF)
            for c in range(tm // CHUNK):
                rows = slice(c * CHUNK, (c + 1) * CHUNK)
                v_blk = v16[rows, cols]
                s = jnp.dot(wh, v_blk, preferred_element_type=F32) + bias_ref[:, cols]
                ds = dr[rows, cols] * gu[rows, cols]
                dgu_s[rows, cols] = dr[rows, cols] * s
                ds16 = ds.astype(BF)
                dw_ref[h] += jnp.where(tri, lax.dot_general(ds16, v_blk, _NT, preferred_element_type=F32), 0.0)
                dv_s[rows, cols] = jnp.dot(wht, ds16, preferred_element_type=F32)
                dbias_ref[:, cols] += ds
        dzu_ref[...] = (dgu_s[...] * gu_grad).astype(dzu_ref.dtype)
        dgv, dg, db = _ln_bwd(xh, r, g_ref[...], dv_s[...])
        dzv_ref[...] = (dgv * gv_grad).astype(dzv_ref.dtype)
        dg_ref[...] += dg
        db_ref[...] += db

        @pl.when(i == n_steps - 1)
        def _():
            for h in range(SGU_HEADS):
                cols = slice(h * LANES, (h + 1) * LANES)
                tot = jnp.sum(dbias_ref[:, cols], axis=1, keepdims=True)
                dbias_ref[:, cols] = jnp.broadcast_to(tot, (CHUNK, LANES))

    small = (ln_g, ln_b, w_s, w_st, bias)
    small_specs = [pl.BlockSpec(a.shape, lambda i, n=a.ndim: (0,) * n) for a in small]
    return pl.pallas_call(
        body, grid=(n_steps,),
        in_specs=[pl.BlockSpec((tm, 512), lambda i: (i, 0)), pl.BlockSpec((tm, 512), lambda i: (i, ZB_U)),
                  pl.BlockSpec((tm, 512), lambda i: (i, ZB_V))] + small_specs,
        out_specs=[pl.BlockSpec((tm, 512), lambda i: (i, 0)), pl.BlockSpec((tm, 512), lambda i: (i, 0)),
                   pl.BlockSpec((1, 512), lambda i: (0, 0)), pl.BlockSpec((1, 512), lambda i: (0, 0)),
                   pl.BlockSpec(w_s.shape, lambda i: (0, 0, 0)), pl.BlockSpec(bias.shape, lambda i: (0, 0))],
        out_shape=[jax.ShapeDtypeStruct((T, 512), BF), jax.ShapeDtypeStruct((T, 512), BF),
                   jax.ShapeDtypeStruct((1, 512), F32), jax.ShapeDtypeStruct((1, 512), F32),
                   jax.ShapeDtypeStruct(w_s.shape, F32), jax.ShapeDtypeStruct(bias.shape, F32)],
        scratch_shapes=[pltpu.VMEM((tm, 512), F32), pltpu.VMEM((tm, 512), F32)],
        name="sgu_bwd", compiler_params=_params(1))(dr, z, z, ln_g, ln_b, w_s, w_st, bias)


def _conv_fwd(z, convk, l, bias):
    T = z.shape[0]

    def body(za_ref, zb_ref, k_ref, b_ref, o_ref):
        xg = za_ref[...].astype(F32) * _sigmoid(zb_ref[...].astype(F32))
        xp = jnp.concatenate([jnp.zeros((CONV_PAD, LANES), F32), xg], axis=0)
        kw = k_ref[...]
        acc = jnp.broadcast_to(b_ref[...], (T, LANES))
        for s in range(SUBLANES):
            xs = xp if s == 0 else pltpu.roll(xp, s, 0)
            for q in range(CONV_PAD // SUBLANES):
                k = CONV_TAPS - 1 - (SUBLANES * q + s)
                if k >= 0:
                    lo = CONV_PAD - SUBLANES * q
                    acc = acc + kw[k:k + 1, :] * xs[lo:lo + T, :]
        o_ref[...] = acc.astype(o_ref.dtype)

    return pl.pallas_call(
        body, grid=(4,),
        in_specs=[pl.BlockSpec((T, LANES), lambda g: (0, 4 * ZB_A + g)),
                  pl.BlockSpec((T, LANES), lambda g: (0, 4 * ZB_B + g)),
                  pl.BlockSpec((None, CONV_PAD, LANES), lambda g: (g, 0, 0)),
                  pl.BlockSpec((1, LANES), lambda g: (0, g))],
        out_specs=pl.BlockSpec((T, LANES), lambda g: (0, g)), out_shape=jax.ShapeDtypeStruct((T, 512), BF),
        name="conv_fwd", compiler_params=_params(1))(z, z, convk[l], bias)


def _conv_bwd(dy, z, convk, l):
    T = z.shape[0]

    def body(dy_ref, za_ref, zb_ref, k_ref, dza_ref, dzb_ref, dk_ref, db_ref):
        a = za_ref[...].astype(F32)
        sg = _sigmoid(zb_ref[...].astype(F32))
        d = dy_ref[...].astype(F32)
        kw = k_ref[...]
        xp = jnp.concatenate([jnp.zeros((CONV_PAD, LANES), F32), a * sg], axis=0)
        dp = jnp.concatenate([d, jnp.zeros((CONV_PAD, LANES), F32)], axis=0)
        dxg = jnp.zeros((T, LANES), F32)
        dk_ref[...] = jnp.zeros_like(dk_ref)
        for s in range(SUBLANES):
            xs = xp if s == 0 else pltpu.roll(xp, s, 0)
            ds = dp if s == 0 else pltpu.roll(dp, T + CONV_PAD - s, 0)
            for q in range(CONV_PAD // SUBLANES):
                k = CONV_TAPS - 1 - (SUBLANES * q + s)
                if k >= 0:
                    lo = CONV_PAD - SUBLANES * q
                    dk_ref[k:k + 1, :] = jnp.sum(d * xs[lo:lo + T, :], axis=0, keepdims=True)
                    dxg = dxg + kw[k:k + 1, :] * ds[SUBLANES * q:SUBLANES * q + T, :]
        db_ref[...] = jnp.sum(d, axis=0, keepdims=True)
        dza_ref[...] = (dxg * sg).astype(dza_ref.dtype)
        dzb_ref[...] = (dxg * a * sg * (1.0 - sg)).astype(dzb_ref.dtype)

    col = pl.BlockSpec((T, LANES), lambda g: (0, g))
    return pl.pallas_call(
        body, grid=(4,),
        in_specs=[col, pl.BlockSpec((T, LANES), lambda g: (0, 4 * ZB_A + g)),
                  pl.BlockSpec((T, LANES), lambda g: (0, 4 * ZB_B + g)),
                  pl.BlockSpec((None, CONV_PAD, LANES), lambda g: (g, 0, 0))],
        out_specs=[col, col, pl.BlockSpec((CONV_PAD, LANES), lambda g: (0, g)),
                   pl.BlockSpec((1, LANES), lambda g: (0, g))],
        out_shape=[jax.ShapeDtypeStruct((T, 512), BF), jax.ShapeDtypeStruct((T, 512), BF),
                   jax.ShapeDtypeStruct((CONV_PAD, 512), F32), jax.ShapeDtypeStruct((1, 512), F32)],
        name="conv_bwd", compiler_params=_params(1))(dy, z, z, convk[l])


D = D_MODEL


def _ffn_fwd(l, h, S, W, pre, deps=()):
    n, gp, u, a = _norm_mm("ffn_in", h, S[pre + "_pre_g"], [W[pre + "_w_gate"][l], W[pre + "_w_up"][l]], True,
                           act=True, deps=deps)
    f, out = _mm_res("ffn_out", a, W[pre + "_w_down"][l], h, S[pre + "_post_g"], 0.5)
    return out, dict(h=h, n=n, gp=gp, u=u, a=a, f=f)


def _ffn_bwd(l, dh, sv, S, W, G, SG, pre, deps=()):
    df, SG[pre + "_post_g"], dgp, du = _resbwd_mm("ffn_bwd_act", dh, sv["f"], S[pre + "_post_g"], 0.5,
                                                  W[pre + "_w_down"][l], True, act=(sv["gp"], sv["u"]), deps=deps)
    G[pre + "_w_down"] = _mm_tn("ffn_dw_down", sv["a"], df, G[pre + "_w_down"], l, True)
    G[pre + "_w_gate"] = _mm_tn("ffn_dw_gate", dgp, sv["n"], G[pre + "_w_gate"], l, True)
    G[pre + "_w_up"] = _mm_tn("ffn_dw_up", du, sv["n"], G[pre + "_w_up"], l, True)
    dh_in, SG[pre + "_pre_g"] = _dn_prenorm("ffn_bwd_in", [(dgp, W[pre + "_w_gate"][l], 0), (du, W[pre + "_w_up"][l], 0)],
                                            False, dh, sv["h"], S[pre + "_pre_g"])
    return dh_in


def _gates(zg):
    return [_sigmoid(jnp.concatenate([zg[2 * k].astype(F32), zg[2 * k + 1].astype(F32)], axis=1)) for k in range(3)]


def _merge_fwd(z, rs, ws, tm=ROW_TILE):
    T = z.shape[0]
    tm = _tile(T, tm)
    nb, kk, bw = ws[0].shape

    def body(*refs):
        r_refs, g_refs, w_refs, y_refs, m_ref = refs[:3], refs[3:9], refs[9:12], refs[12:15], refs[15]
        for r_ref, w_ref, y_ref in zip(r_refs, w_refs, y_refs):
            for b in range(nb):
                y_ref[:, b * bw:(b + 1) * bw] = jnp.dot(r_ref[...], w_ref[b],
                                                        preferred_element_type=F32).astype(y_ref.dtype)
        g = _gates([q[...] for q in g_refs])
        m_ref[...] = (g[0] * y_refs[0][...].astype(F32) + g[1] * y_refs[1][...].astype(F32)
                      + g[2] * y_refs[2][...].astype(F32)).astype(m_ref.dtype)

    row = pl.BlockSpec((tm, D_MODEL), lambda i: (i, 0))
    return pl.pallas_call(
        body, grid=(T // tm,),
        in_specs=[pl.BlockSpec((tm, kk), lambda i: (i, 0))] * 3
        + [pl.BlockSpec((tm, 512), lambda i, j=j: (i, ZB_GATES + j)) for j in range(6)]
        + [pl.BlockSpec(ws[0].shape, lambda i: (0, 0, 0))] * 3,
        out_specs=[row] * 4, out_shape=[jax.ShapeDtypeStruct((T, D_MODEL), BF)] * 4,
        name="mix_merge", compiler_params=_params(1))(*rs, *[z] * 6, *ws)


def _merge_bwd(dmerged, z, ys, ws, tm=ROW_TILE // 2):
    T = z.shape[0]
    tm = _tile(T, tm)
    nb, kk, bw = ws[0].shape

    def body(*refs):
        dm_ref, g_refs, y_refs, w_refs = refs[0], refs[1:7], refs[7:10], refs[10:13]
        dy_refs, lo_ref, hi_ref, dr_refs = refs[13:16], refs[16], refs[17], refs[18:21]
        cut = DZ_HALF - ZB_GATES * 512
        dm = dm_ref[...].astype(F32)
        g = _gates([q[...] for q in g_refs])
        for k in range(3):
            dy_refs[k][...] = (dm * g[k]).astype(BF)
            dzg = (dm * y_refs[k][...].astype(F32) * g[k] * (1.0 - g[k])).astype(BF)
            if k == 0:
                lo_ref[...] = dzg[:, :cut]
                hi_ref[:, :D_MODEL - cut] = dzg[:, cut:]
            else:
                hi_ref[:, k * D_MODEL - cut:(k + 1) * D_MODEL - cut] = dzg
            dr = None
            for b in range(nb):
                p = lax.dot_general(dy_refs[k][:, b * bw:(b + 1) * bw], w_refs[k][b], _NT,
                                    preferred_element_type=F32)
                dr = p if dr is None else dr + p
            dr_refs[k][...] = dr.astype(BF)

    row = pl.BlockSpec((tm, D_MODEL), lambda i: (i, 0))
    return pl.pallas_call(
        body, grid=(T // tm,),
        in_specs=[row] + [pl.BlockSpec((tm, 512), lambda i, j=j: (i, ZB_GATES + j)) for j in range(6)] + [row] * 3
        + [pl.BlockSpec(ws[0].shape, lambda i: (0, 0, 0))] * 3,
        out_specs=[row] * 3 + [pl.BlockSpec((tm, DZ_HALF - ZB_GATES * 512), lambda i: (i, 0)),
                               pl.BlockSpec((tm, DZ_HALF), lambda i: (i, 0))]
        + [pl.BlockSpec((tm, kk), lambda i: (i, 0))] * 3,
        out_shape=[jax.ShapeDtypeStruct((T, D_MODEL), BF)] * 3
        + [jax.ShapeDtypeStruct((T, DZ_HALF - ZB_GATES * 512), BF), jax.ShapeDtypeStruct((T, DZ_HALF), BF)]
        + [jax.ShapeDtypeStruct((T, kk), BF)] * 3,
        name="mix_merge_bwd", compiler_params=_params(1))(dmerged, *[z] * 6, *ys, *ws)


def _mix_fwd(l, h, S, W, deps=()):
    n, z = _norm_mm("mix_in", h, S["mix_pre_g"], [W["w_in"][l]], False, deps=deps)
    r_pool = _pool_fwd(z, S["pool_w"], S["pool_scale"])
    r_sgu = _sgu_fwd(z, S["sgu_ln_g"], S["sgu_ln_b"], S["sgu_w_s"], S["sgu_bias"])
    yc = _conv_fwd(z, W["conv_dw_k"], l, S["conv_dw_b"])

    def ln_silu(y, g, b):
        xh, _ = _ln_stats(y.astype(F32))
        return (_silu_and_grad(xh * g + b)[0],)

    r_conv = _rowwise("conv_ln", ln_silu, [(yc, 512, 0)], [S["conv_ln_g"], S["conv_ln_b"]], [(512, BF)])[0]
    y_pool, y_sgu, y_conv, merged = _merge_fwd(z, (r_pool, r_sgu, r_conv),
                                               [W["w_%s_out" % br][l] for br in ("pool", "sgu", "conv")])
    o, out = _mm_res("mix_out", merged, W["w_out"][l], h, S["mix_post_g"], 1.0)
    return out, dict(h=h, n=n, z=z, r_pool=r_pool, r_sgu=r_sgu, yc=yc, r_conv=r_conv, y_pool=y_pool, y_sgu=y_sgu,
                     y_conv=y_conv, merged=merged, o=o)


def _mix_bwd(l, dh, sv, S, W, G, SG, deps=()):
    z = sv["z"]
    do, SG["mix_post_g"], dmerged = _resbwd_mm("mix_bwd_out", dh, sv["o"], S["mix_post_g"], 1.0,
                                               W["w_out"][l].reshape(1, D, D), True, deps=deps)
    G["w_out"] = _mm_tn("mix_dw_out", sv["merged"], do, G["w_out"], l, True)

    branches = ("pool", "sgu", "conv")
    res = _merge_bwd(dmerged, z, [sv["y_" + br] for br in branches], [W["w_%s_out" % br][l] for br in branches])
    dz_gate_lo, dz_hi, dr = res[3], res[4], dict(zip(branches, res[5:]))
    for br, dy in zip(branches, res[:3]):
        wn = "w_%s_out" % br
        G[wn] = _mm_tn("branch_dw", sv["r_" + br], dy, G[wn], l, False)
    dz_pool, SG["pool_w"], SG["pool_scale"] = _pool_bwd(dr["pool"], z, S["pool_w"], S["pool_scale"])
    dzu, dzv, SG["sgu_ln_g"], SG["sgu_ln_b"], SG["sgu_w_s"], dbias = _sgu_bwd(
        dr["sgu"], z, S["sgu_ln_g"], S["sgu_ln_b"], S["sgu_w_s"], S["sgu_w_st"], S["sgu_bias"])
    SG["sgu_b_s"] = dbias[:, ::LANES].T

    def ln_silu_bwd(d, y, g, b):
        xh, r = _ln_stats(y.astype(F32))
        _, grad = _silu_and_grad(xh * g + b)
        return _ln_bwd(xh, r, g, d.astype(F32) * grad)

    dyc, SG["conv_ln_g"], SG["conv_ln_b"] = _rowwise(
        "conv_ln_bwd", ln_silu_bwd, [(dr["conv"], 512, 0), (sv["yc"], 512, 0)], [S["conv_ln_g"], S["conv_ln_b"]],
        [(512, BF)], [(1, 512), (1, 512)])
    dza, dzb, SG["conv_dw_k"], SG["conv_dw_b"] = _conv_bwd(dyc, z, W["conv_dw_k"], l)
    dz_lo = jnp.concatenate([dz_pool, dzu, dzv, dza, dzb, dz_gate_lo], axis=1)
    G["w_in"] = _mm_tn("mix_dw_in", sv["n"], dz_lo, G["w_in"], l, False)
    G["w_in"] = _mm_tn("mix_dw_in", sv["n"], dz_hi, G["w_in"], l, False, first=2)
    dh_in, SG["mix_pre_g"] = _dn_prenorm("mix_bwd_in", [(dz_lo, W["w_in"][l], 0), (dz_hi, W["w_in"][l], 2)], True, dh,
                                         sv["h"], S["mix_pre_g"])
    return dh_in


def _ple_out(h, p, gp, w3, g, tm=ROW_TILE):
    T, kp = p.shape
    nb, _, bw = w3.shape
    tm = _tile(T, tm)

    def body(h_ref, p_ref, gp_ref, w_ref, g_ref, e_ref, o_ref):
        p16 = p_ref[...].astype(BF)
        for b in range(nb):
            e_ref[:, b * bw:(b + 1) * bw] = jnp.dot(p16, w_ref[b], preferred_element_type=F32).astype(BF)
        q = _sigmoid(gp_ref[...].astype(F32)) * e_ref[...].astype(F32)
        o_ref[...] = h_ref[...] + _rms_fwd(q, g_ref[...])

    row = pl.BlockSpec((tm, D_MODEL), lambda i: (i, 0))
    return pl.pallas_call(
        body, grid=(T // tm,),
        in_specs=[row, pl.BlockSpec((tm, kp), lambda i: (i, 0)), row, pl.BlockSpec(w3.shape, lambda i: (0, 0, 0)),
                  pl.BlockSpec(g.shape, lambda i: (0, 0))],
        out_specs=[row, row],
        out_shape=[jax.ShapeDtypeStruct((T, D_MODEL), BF), jax.ShapeDtypeStruct((T, D_MODEL), F32)],
        name="ple_out", compiler_params=_params(1))(h, p, gp, w3, g)


def _ple_fwd(l, h, p_l, S, W, deps=()):
    n, gp = _norm_mm("ple_in", h, S["ple_pre_g"], [W["ple_w_gate"][l].reshape(1, D, D)], False, deps=deps)
    e, out = _ple_out(h, p_l, gp, W["ple_w_proj"][l], S["ple_post_g"])
    return out, dict(h=h, n=n, e=e, gp=gp, p=p_l)


def _ple_bwd_rows(dh, e, gp, g_post, w3, h, g_pre, deps=(), tm=ROW_TILE):
    T = dh.shape[0]
    w2 = w3.reshape(D_MODEL, D_MODEL)
    tm = _tile(T, tm)
    n_dep = len(deps)

    def body(*refs):
        dh_ref, e_ref, gp_ref, gpost_ref, w_ref, h_ref, gpre_ref, de_ref, dgp_ref, o_ref, dpost_ref, dpre_ref = \
            refs[n_dep:]
        first = pl.program_id(0) == 0
        d = dh_ref[...]
        sg = _sigmoid(gp_ref[...].astype(F32))
        ee = e_ref[...].astype(F32)
        dq, dpost = _rms_bwd(sg * ee, gpost_ref[...], d)
        de_ref[...] = (dq * sg).astype(BF)
        dgp = (dq * ee * sg * (1.0 - sg)).astype(BF)
        dgp_ref[...] = dgp
        dn = lax.dot_general(dgp, w_ref[...], _NT, preferred_element_type=F32)
        dx, dpre = _rms_bwd(h_ref[...], gpre_ref[...], dn)
        o_ref[...] = d + dx
        _acc_rows(dpost_ref, dpost, first)
        _acc_rows(dpre_ref, dpre, first)

    row = pl.BlockSpec((tm, D_MODEL), lambda i: (i, 0))
    vec = pl.BlockSpec((1, D_MODEL), lambda i: (0, 0))
    return pl.pallas_call(
        body, grid=(T // tm,),
        in_specs=[ANY] * n_dep + [row, row, row, vec, pl.BlockSpec(w2.shape, lambda i: (0, 0)), row, vec],
        out_specs=[row, row, row, vec, vec],
        out_shape=[jax.ShapeDtypeStruct((T, D_MODEL), BF)] * 2 + [jax.ShapeDtypeStruct((T, D_MODEL), F32)]
        + [jax.ShapeDtypeStruct((1, D_MODEL), F32)] * 2,
        name="ple_bwd", compiler_params=_params(1))(*deps, dh, e, gp, g_post, w2, h, g_pre)


def _ple_bwd(l, dh, sv, S, W, G, SG, deps=()):
    de, dgp, dh_in, SG["ple_post_g"], SG["ple_pre_g"] = _ple_bwd_rows(
        dh, sv["e"], sv["gp"], S["ple_post_g"], W["ple_w_gate"][l], sv["h"], S["ple_pre_g"], deps)
    G["ple_w_proj"] = _mm_tn("ple_dw_proj", sv["p"], de, G["ple_w_proj"], l, False)
    G["ple_w_gate"] = _mm_tn("ple_dw_gate", sv["n"], dgp, G["ple_w_gate"], l, True)
    return dh_in


def _layer_small(a, l):
    S = {}
    for name in SMALL:
        v = a[name][l]
        S[name] = v.reshape(1, -1) if v.ndim == 1 else v
    S["sgu_w_st"] = jnp.swapaxes(S["sgu_w_s"], 1, 2)
    S["sgu_bias"] = jnp.repeat(S["sgu_b_s"].T, LANES, axis=1)
    return S


def _layer_fwd(l, h, p_l, S, W, deps=(), hooks=None):
    hooks = hooks or {}

    def after(part, hv):
        return hooks[part](hv) if part in hooks else ()

    h, sv1 = _ffn_fwd(l, h, S, W, "ffn1", deps)
    h, sv2 = _mix_fwd(l, h, S, W, after("ffn1", h))
    h, sv3 = _ffn_fwd(l, h, S, W, "ffn2", after("mix", h))
    h, sv4 = _ple_fwd(l, h, p_l, S, W, after("ffn2", h))
    return h, (sv1, sv2, sv3, sv4)


def _layer_bwd(l, dh, sv, S, W, G, deps=(), mid=None):
    SG = {}
    dh = _ple_bwd(l, dh, sv[3], S, W, G, SG, deps)
    dh = _ffn_bwd(l, dh, sv[2], S, W, G, SG, "ffn2")
    dh = _mix_bwd(l, dh, sv[1], S, W, G, SG, mid(dh) if mid else ())
    dh = _ffn_bwd(l, dh, sv[0], S, W, G, SG, "ffn1")
    return dh, SG


HBM = pl.BlockSpec(memory_space=pltpu.HBM)
SEM = pl.BlockSpec(memory_space=pltpu.SEMAPHORE)
SIDE_EFFECT = pltpu.SideEffectType.DATAFLOW_SIDE_EFFECTING


def _place():
    x, y, c = lax.axis_index("x"), lax.axis_index("y"), lax.axis_index("c")
    chips = [(1 - x, y), (x, 1 - y), (1 - x, 1 - y)]
    return x, y, c, chips


def _remote(src, dst, send_sem, recv_sem, to):
    return pltpu.make_async_remote_copy(src_ref=src, dst_ref=dst, send_sem=send_sem, recv_sem=recv_sem,
                                        device_id=to, device_id_type=MESH)


def _split_start(name, plan, bufs, deps):
    count, fn = plan
    n, nd = len(bufs), len(deps)

    def body(*refs):
        send, recv = refs[nd + n], refs[nd + n + 1]
        x, y, c, chips = _place()
        for k, (src, dst, _, to) in enumerate(fn(refs[nd:nd + n], x, y, c, chips)):
            _remote(src, dst, send.at[k], recv.at[k], to).start()
        refs[-1][...] = jnp.zeros_like(refs[-1])

    res = pl.pallas_call(
        body, in_specs=[ANY] * nd + [HBM] * n,
        out_specs=[SEM, SEM] + [HBM] * n + [pl.BlockSpec(memory_space=pltpu.VMEM)],
        out_shape=[pltpu.SemaphoreType.DMA((count,)), pltpu.SemaphoreType.DMA((count,))]
        + [pltpu.HBM(b.shape, b.dtype) for b in bufs] + [jax.ShapeDtypeStruct((8, LANES), F32)],
        input_output_aliases={nd + i: 2 + i for i in range(n)}, name=name,
        compiler_params=pltpu.CompilerParams(has_side_effects=SIDE_EFFECT),
    )(*deps, *[pltpu.with_memory_space_constraint(b, pltpu.HBM) for b in bufs])
    return (res[0], res[1]), list(res[2:2 + n]), res[-1]


def _split_wait(name, plan, sems, bufs, after):
    _, fn = plan
    n = len(bufs)

    def body(*refs):
        send, recv = refs[n], refs[n + 1]
        x, y, c, chips = _place()
        for k, (src, _, land, to) in enumerate(fn(refs[:n], x, y, c, chips)):
            cp = _remote(src, land, send.at[k], recv.at[k], to)
            cp.wait_send()
            cp.wait_recv()

    res = pl.pallas_call(
        body, in_specs=[HBM] * n + [SEM, SEM] + [ANY] * len(after), out_specs=[HBM] * n,
        out_shape=[pltpu.HBM(b.shape, b.dtype) for b in bufs], input_output_aliases={i: i for i in range(n)},
        name=name, compiler_params=pltpu.CompilerParams(has_side_effects=SIDE_EFFECT))(*bufs, *sems, *after)
    return list(res)


def _gather_plans(n):
    def across(b, x, y, c, chips):
        me, out = 2 * x + y, []
        for a in range(n):
            rh = b[a].shape[1] // 2
            mine = b[a].at[me, pl.ds(c * rh, rh)]
            for cx, cy in chips:
                out.append((mine, mine, b[a].at[2 * cx + cy, pl.ds(c * rh, rh)], (cx, cy, c)))
        return out

    def to_sibling(b, x, y, c, chips):
        out = []
        for a in range(n):
            rh = b[a].shape[1] // 2
            for cx, cy in chips:
                piece = b[a].at[2 * cx + cy, pl.ds(c * rh, rh)]
                out.append((piece, piece, b[a].at[2 * cx + cy, pl.ds((1 - c) * rh, rh)], (x, y, 1 - c)))
        return out

    return (3 * n, across), (3 * n, to_sibling)


def _pair_plan(n):
    def fn(b, x, y, c, chips):
        out = []
        for a in range(n):
            rh = b[a].shape[1] // 2
            out.append((b[a].at[:, pl.ds((1 - c) * rh, rh)], b[n + a], b[n + a], (x, y, 1 - c)))
        return out

    return n, fn


def _cross_plan(n):
    def fn(b, x, y, c, chips):
        out = []
        for a in range(n):
            for j, (cx, cy) in enumerate(chips):
                out.append((b[a].at[2 * cx + cy], b[n + a].at[j], b[n + a].at[j], (cx, cy, c)))
        return out

    return 3 * n, fn


def _share_plan(n, l):
    def fn(b, x, y, c, chips):
        out = []
        for a in range(n):
            rh = b[a].shape[1] // 2
            mine = b[a].at[l, pl.ds(c * rh, rh)]
            out.append((mine, mine, b[a].at[l, pl.ds((1 - c) * rh, rh)], (x, y, 1 - c)))
        return out

    return n, fn


def _exchange_call(name, body, ins, out_shapes, n_remote, in_place=False):
    scratch = [pltpu.SemaphoreType.DMA((n_remote,)), pltpu.SemaphoreType.DMA((n_remote,))]
    aliases = {i: i for i in range(len(ins))} if in_place else {}
    return pl.pallas_call(body, in_specs=[ANY] * len(ins), out_specs=[ANY] * len(out_shapes), out_shape=out_shapes,
                          scratch_shapes=scratch, input_output_aliases=aliases, name=name)(*ins)


def _peers(x, y, c):
    return [(1 - x if m & 4 else x, 1 - y if m & 2 else y, 1 - c if m & 1 else c) for m in range(1, 8)]


def _scatter_small(v3):
    def body(v_ref, o_ref, send, recv):
        x, y, c, _ = _place()
        cps = []
        for m, (px, py, pc) in enumerate(_peers(x, y, c)):
            cps.append(_remote(v_ref.at[4 * px + 2 * py + pc], o_ref.at[m], send.at[m], recv.at[m], (px, py, pc)))
            cps[-1].start()
        for cp in cps:
            cp.wait()

    return _exchange_call("scatter_small", body, [v3], [jax.ShapeDtypeStruct((7,) + v3.shape[1:], v3.dtype)], 7)[0]


def _gather_small(buf):
    def body(_, o_ref, send, recv):
        x, y, c, _ = _place()
        mine = o_ref.at[4 * x + 2 * y + c]
        peers = _peers(x, y, c)
        cps = []
        for m, to in enumerate(peers):
            cps.append(_remote(mine, mine, send.at[m], recv.at[m], to))
            cps[-1].start()
        for m, (px, py, pc) in enumerate(peers):
            slab = o_ref.at[4 * px + 2 * py + pc]
            _remote(slab, slab, send.at[m], recv.at[m], (px, py, pc)).wait_recv()
            cps[m].wait_send()

    return _exchange_call("gather_small", body, [buf], [jax.ShapeDtypeStruct(buf.shape, buf.dtype)], 7,
                          in_place=True)[0]


def _allreduce_small(v, pos):
    rows = v.shape[0]
    rs = rows // 8
    v3 = v.reshape(8, rs, LANES)
    got = _scatter_small(v3)
    tm = _tile(rs)
    ins = [(v3, (None, tm, LANES), lambda i, p: (p[2], i, 0))]
    ins += [(got, (None, tm, LANES), lambda i, p, m=m: (m, i, 0)) for m in range(7)]
    buf = _tiled("sum_small", lambda *t: (((((((t[0] + t[1]) + t[2]) + t[3]) + t[4]) + t[5]) + t[6]) + t[7],),
                 (rs // tm,), pos, ins, [((8, rs, LANES), F32, (None, tm, LANES), lambda i, p: (p[2], i, 0))])[0]
    return _gather_small(buf).reshape(rows, LANES)


ADD_ROWS = 128


def _multi_tiled(name, fn, pos, groups, in_place=False):
    steps = max(g[1] for g in groups)
    flat_in, in_specs, out_specs, out_shape, counts, dests = [], [], [], [], [], []
    for ins, n_t, (shape, dtype, oidx, dest) in groups:
        for arr, idx in ins:
            flat_in.append(arr)
            in_specs.append(pl.BlockSpec((ADD_ROWS, arr.shape[1]),
                                         lambda i, p, idx=idx, n_t=n_t: (idx(jnp.minimum(i, n_t - 1), p), 0)))
        out_specs.append(pl.BlockSpec((ADD_ROWS, shape[1]),
                                      lambda i, p, oidx=oidx, n_t=n_t: (oidx(jnp.minimum(i, n_t - 1), p), 0)))
        out_shape.append(jax.ShapeDtypeStruct(shape, dtype))
        counts.append((len(ins), n_t))
        dests.append(dest)
    n_in = len(flat_in)
    extra = dests if in_place else []

    def body(_, *refs):
        outs = refs[n_in + len(extra):]
        k = 0
        for (n_a, n_t), o_ref in zip(counts, outs):
            tiles = refs[k:k + n_a]
            k += n_a

            @pl.when(pl.program_id(0) < n_t)
            def _(tiles=tiles, o_ref=o_ref):
                o_ref[...] = fn(*[t[...] for t in tiles]).astype(o_ref.dtype)

    spec = pltpu.PrefetchScalarGridSpec(num_scalar_prefetch=1, grid=(steps,),
                                        in_specs=in_specs + [ANY] * len(extra), out_specs=out_specs)
    return pl.pallas_call(body, grid_spec=spec, out_shape=out_shape,
                          input_output_aliases={1 + n_in + k: k for k in range(len(extra))}, name=name,
                          compiler_params=_params(1))(pos, *flat_in, *extra)


def _add_pair(grads, got, pos):
    groups = []
    for g, q in zip(grads, got):
        nb, R, C = g.shape
        rh = R // 2
        nh = rh // ADD_ROWS
        groups.append(([(g.reshape(nb * R, C), lambda t, p, nh=nh: (t // nh) * 2 * nh + p[1] * nh + t % nh),
                        (q.reshape(nb * rh, C), lambda t, p: t)], nb * nh,
                       ((nb * rh, C), BF, lambda t, p: t, None)))
    res = _multi_tiled("rs_add_pair", lambda u, w: u.astype(F32) + w.astype(F32), pos, groups)
    return [t.reshape(q.shape) for t, q in zip(res, got)]


def _add_chips(parts, slots, reduced, l, pos):
    def add(own, s0, s1, s2):
        return ((own.astype(F32) + s0.astype(F32)) + s1.astype(F32)) + s2.astype(F32)

    groups = []
    for t, s, red in zip(parts, slots, reduced):
        nb, rh, C = t.shape
        L = red.shape[0]
        nh = rh // ADD_ROWS
        ins = [(t.reshape(nb * rh, C), lambda i, p, nh=nh: p[0] * nh + i)]
        ins += [(s.reshape(3 * rh, C), lambda i, p, j=j, nh=nh: j * nh + i) for j in range(3)]
        groups.append((ins, nh, ((L * 2 * rh, C), F32, lambda i, p, nh=nh: l * 2 * nh + p[1] * nh + i,
                                 red.reshape(L * 2 * rh, C))))
    res = _multi_tiled("rs_add_chips", add, pos, groups, in_place=True)
    return [buf.reshape(red.shape) for buf, red in zip(res, reduced)]


def _adamw_math(w, g, m, v):
    m = ADAM_B1 * m + (1.0 - ADAM_B1) * g
    v = ADAM_B2 * v + (1.0 - ADAM_B2) * (g * g)
    m_hat = m / (1.0 - ADAM_B1 ** ADAM_STEP)
    v_hat = v / (1.0 - ADAM_B2 ** ADAM_STEP)
    return -ADAM_LR * (m_hat / (jnp.sqrt(v_hat) + ADAM_EPS) + ADAM_WD * w), m, v


def _adamw(w, g, m, v, lo=0, hi=None, into=None, deps=()):
    L, R, C = w.shape
    hi = L if hi is None else hi
    tr = _tile(R, max(16, ADAM_TILE_ELEMS // C))
    extra = (list(into) if into else []) + list(deps)
    n_alias = 4 if into else 0

    def body(w_ref, g_ref, m_ref, v_ref, *rest):
        go_ref, d_ref, mo_ref, vo_ref = rest[len(extra):]
        gv = g_ref[...]
        d, mn, vn = _adamw_math(w_ref[...], gv, m_ref[...], v_ref[...])
        go_ref[...] = gv
        d_ref[...] = d
        mo_ref[...] = mn
        vo_ref[...] = vn

    spec = pl.BlockSpec((None, tr, C), lambda l, i: (l + lo, i, 0))
    out = jax.ShapeDtypeStruct(w.shape, F32)
    return pl.pallas_call(body, grid=(hi - lo, R // tr), in_specs=[spec] * 4 + [ANY] * len(extra),
                          out_specs=[spec] * 4, out_shape=[out] * 4,
                          input_output_aliases={4 + k: k for k in range(n_alias)}, name="adamw",
                          compiler_params=_params(2))(w, g, m, v, *extra)


def _pack(parts):
    flat = jnp.concatenate([q.reshape(-1, LANES) for q in parts], axis=0)
    return jnp.pad(flat, ((0, -flat.shape[0] % ROW_TILE), (0, 0)))


def _unpack(flat, like):
    out, r = [], 0
    for q in like:
        n = q.size // LANES
        out.append(flat[r:r + n].reshape(q.shape))
        r += n
    return out


def _train_step(a):
    a = dict(a)
    L = a["ffn1_pre_g"].shape[0]
    x, y, c, _ = _place()
    chip = 2 * x + y
    pos = jnp.stack([chip, c, 2 * chip + c]).astype(jnp.int32)
    for name in TRANSPOSED:
        for pre in ("", "m_", "v_"):
            a[pre + name] = jnp.swapaxes(a[pre + name], 1, 2)
    big = [b[0] for b in BIG]
    gathered = big + ["conv_dw_k"]
    n_w, n_g = len(gathered), len(big)

    own = [None] * n_w
    W = {name: [None] * L for name in gathered}
    every = list(range(n_w))
    first = every[:3]
    rest = every[3:]

    def cast(i, deps):
        if i == n_g:
            taps = a["conv_dw_k"].reshape(L, CONV_TAPS, LANES)
            return _cast_layers("pad_conv_taps", taps, CONV_PAD, LANES, F32, pos, deps)
        name, _, _, _, rp, cp = BIG[i]
        return _cast_layers("cast_weight", a[name], rp, cp, BF, pos, deps)

    def gather_first(l, ids, tag, deps):
        return _split_start("gather_a%d%s" % (l, tag), _gather_plans(len(ids))[0], [own[i][l] for i in ids], deps)

    def gather_second(l, ids, tag, state, after):
        across, to_sibling = _gather_plans(len(ids))
        bufs = _split_wait("gather_a%d%s_done" % (l, tag), across, state[0], state[1], after)
        return _split_start("gather_b%d%s" % (l, tag), to_sibling, bufs, [])

    def gather_done(l, ids, tag, state, after):
        to_sibling = _gather_plans(len(ids))[1]
        bufs = _split_wait("gather_b%d%s_done" % (l, tag), to_sibling, state[0], state[1], after)
        for i, buf in zip(ids, bufs):
            W[gathered[i]][l] = buf

    for i in first:
        own[i] = cast(i, ())
    state = gather_first(0, first, "f", [])
    for i in rest:
        own[i] = cast(i, (state[2],))
    state = gather_second(0, first, "f", state, [own[i][0] for i in rest])
    gather_done(0, first, "f", state, [])
    box = {"rest": gather_first(0, rest, "r", [])}

    small = [_layer_small(a, l) for l in range(L)]
    h, saved = a["x"][0], []
    for l in range(L):
        hooks = {}
        if l == 0:
            deps = (box["rest"][2],)

            def after_ffn1(hv, box=box):
                gather_done(0, rest, "r", gather_second(0, rest, "r", box["rest"], [hv]), [])
                if L > 1:
                    box["next"] = gather_first(1, every, "", [hv])
                    return (box["next"][2],)
                return ()

            hooks["ffn1"] = after_ffn1
            switch = "ffn2"
        else:
            deps = ()
            if l + 1 < L:
                box["next"] = gather_first(l + 1, every, "", [h])
                deps = (box["next"][2],)
            switch = "mix"
        if l + 1 < L:
            def second(hv, l=l, box=box):
                box["next"] = gather_second(l + 1, every, "", box["next"], [hv])
                return (box["next"][2],)

            hooks[switch] = second
        h, sv = _layer_fwd(l, h, a["p"][l, 0], small[l], W, deps, hooks)
        saved.append(sv)
        if l + 1 < L:
            gather_done(l + 1, every, "", box["next"], [h])

    def loss_fn(yv, t):
        e = yv - t
        return e * (1.0 / D), jnp.sum(e * e, axis=0, keepdims=True)

    dh, lsum = _rowwise("loss", loss_fn, [(h, D, 0), (a["loss_target"][0], D, 0)], [], [(D, F32)], [(1, D)])
    loss = lax.psum(0.5 * jnp.sum(lsum) / D, ("x", "y", "c"))

    G = {name: [jax.ShapeDtypeStruct((N_CHIPS, rp, cp), BF)] * L for name, _, _, _, rp, cp in BIG}
    reduced = [lax.empty((L, rp, cp), F32) for _, _, _, _, rp, cp in BIG]
    pair, cross = _pair_plan(n_g), _cross_plan(n_g)
    small_grads = [None] * L

    def pair_start(l, deps):
        grads = [G[name][l] for name in big]
        lands = [lax.empty((N_CHIPS, g.shape[1] // 2, g.shape[2]), BF) for g in grads]
        return _split_start("rs_pair%d" % l, pair, grads + lands, deps)

    def cross_start(l, state, after):
        bufs = _split_wait("rs_pair%d_done" % l, pair, state[0], state[1], after)
        parts = _add_pair(bufs[:n_g], bufs[n_g:], pos)
        lands = [lax.empty((3,) + t.shape[1:], BF) for t in parts]
        return _split_start("rs_cross%d" % l, cross, parts + lands, [])

    def share_start(l, state, after, reduced):
        bufs = _split_wait("rs_cross%d_done" % l, cross, state[0], state[1], after)
        reduced = _add_chips(bufs[:n_g], bufs[n_g:], reduced, l, pos)
        return _split_start("rs_share%d" % l, _share_plan(n_g, l), reduced, [])

    def share_done(l, state, after):
        return _split_wait("rs_share%d_done" % l, _share_plan(n_g, l), state[0], state[1], after)

    st_pair = st_cross = st_share = None
    for l in reversed(range(L)):
        deps = tuple(s[2] for s in (st_pair, st_share) if s is not None)
        box = {"cross": None}

        def mid(dm, l=l, box=box, st_pair=st_pair, st_share=st_share):
            out = []
            if st_share is not None:
                box["reduced"] = share_done(l + 2, st_share, [dm])
            if st_pair is not None:
                box["cross"] = cross_start(l + 1, st_pair, [dm])
                out.append(box["cross"][2])
            return tuple(out)

        dh, small_grads[l] = _layer_bwd(l, dh, saved[l], small[l], W, G, deps, mid)
        if st_share is not None:
            reduced = box["reduced"]
        st_share = share_start(l + 1, box["cross"], [dh], reduced) if box["cross"] is not None else None
        st_pair = pair_start(l, [dh])
    grad_x = dh
    if st_share is not None:
        reduced = share_done(1, st_share, [])
    st_cross = cross_start(0, st_pair, [])
    small_names = SMALL + ("conv_dw_k",)
    stacked = [jnp.stack([small_grads[l][name] for l in range(L)]) for name in small_names]
    summed = dict(zip(small_names, _unpack(_allreduce_small(_pack(stacked), pos), stacked)))
    upper = {}
    if L > 1:
        for name, red in zip(big, reduced):
            upper[name] = _adamw(a[name], red, a["m_" + name], a["v_" + name], 1, L, deps=[st_cross[2]])
    st_share = share_start(0, st_cross, [r[1] for r in upper.values()] + [summed[small_names[0]]], reduced)
    reduced = share_done(0, st_share, [])
    big_grads = dict(zip(big, reduced))

    grads, deltas, new_m, new_v = {}, {}, {}, {}
    for name in big:
        res = _adamw(a[name], big_grads[name], a["m_" + name], a["v_" + name], 0, 1, upper.get(name))
        if name in TRANSPOSED:
            res = [jnp.swapaxes(r, 1, 2) for r in res]
        grads[name], deltas[name], new_m[name], new_v[name] = res
    taps = lax.dynamic_slice_in_dim(summed["conv_dw_k"], chip * LANES, LANES, axis=2)[:, :CONV_TAPS]
    grads["conv_dw_k"] = taps.reshape(a["conv_dw_k"].shape)
    for name in SMALL:
        grads[name] = summed[name].reshape(a[name].shape)
    shapes = [a[name] for name in small_names]
    res = _adamw(*[_pack([a[pre + name] if pre != "g" else grads[name] for name in small_names])[None]
                   for pre in ("", "g", "m_", "v_")])
    for dst, flat in zip((deltas, new_m, new_v), res[1:]):
        for name, val in zip(small_names, _unpack(flat[0], shapes)):
            dst[name] = val

    return (loss, grad_x[None], *[grads[n] for n in WEIGHTS], *[deltas[n] for n in WEIGHTS],
            *[new_m[n] for n in WEIGHTS], *[new_v[n] for n in WEIGHTS])


def kernel(x, p, ffn1_pre_g, ffn1_w_gate, ffn1_w_up, ffn1_w_down, ffn1_post_g, mix_pre_g, w_in, pool_w, pool_scale, w_pool_out, sgu_ln_g, sgu_ln_b, sgu_w_s, sgu_b_s, w_sgu_out, conv_dw_k, conv_dw_b, conv_ln_g, conv_ln_b, w_conv_out, w_out, mix_post_g, ffn2_pre_g, ffn2_w_gate, ffn2_w_up, ffn2_w_down, ffn2_post_g, ple_w_proj, ple_pre_g, ple_w_gate, ple_post_g, loss_target, m_ffn1_pre_g, m_ffn1_w_gate, m_ffn1_w_up, m_ffn1_w_down, m_ffn1_post_g, m_mix_pre_g, m_w_in, m_pool_w, m_pool_scale, m_w_pool_out, m_sgu_ln_g, m_sgu_ln_b, m_sgu_w_s, m_sgu_b_s, m_w_sgu_out, m_conv_dw_k, m_conv_dw_b, m_conv_ln_g, m_conv_ln_b, m_w_conv_out, m_w_out, m_mix_post_g, m_ffn2_pre_g, m_ffn2_w_gate, m_ffn2_w_up, m_ffn2_w_down, m_ffn2_post_g, m_ple_w_proj, m_ple_pre_g, m_ple_w_gate, m_ple_post_g, v_ffn1_pre_g, v_ffn1_w_gate, v_ffn1_w_up, v_ffn1_w_down, v_ffn1_post_g, v_mix_pre_g, v_w_in, v_pool_w, v_pool_scale, v_w_pool_out, v_sgu_ln_g, v_sgu_ln_b, v_sgu_w_s, v_sgu_b_s, v_w_sgu_out, v_conv_dw_k, v_conv_dw_b, v_conv_ln_g, v_conv_ln_b, v_w_conv_out, v_w_out, v_mix_post_g, v_ffn2_pre_g, v_ffn2_w_gate, v_ffn2_w_up, v_ffn2_w_down, v_ffn2_post_g, v_ple_w_proj, v_ple_pre_g, v_ple_w_gate, v_ple_post_g):
    return _train_step(dict(locals()))
```

```python
import math

import jax
import jax.numpy as jnp
from jax import lax
from jax.experimental import pallas as pl
from jax.experimental.pallas import tpu as pltpu

BF = jnp.bfloat16
F32 = jnp.float32
EPS = 1e-6
D_MODEL = 1024
LANES = 128
SUBLANES = 8
MXU_TILE = 256
N_CHIPS = 4
FFN_SHARD = 704
FFN_SHARD_PAD = 768
POOL_WINDOWS = (2, 4, 8, 16)
SGU_HEADS = 4
CHUNK = 128
CONV_TAPS = 31
CONV_PAD = 32
ROW_TILE = 512
EPI_ROWS = 256
VMEM_LIMIT_BYTES = 56 * 1024 * 1024
ADAM_TILE_ELEMS = 3 * 128 * 1024
ADAM_LR, ADAM_B1, ADAM_B2, ADAM_EPS, ADAM_WD, ADAM_STEP =0.001, 0.9, 0.999, 1e-08, 0.01, 10
MESH = pl.DeviceIdType.MESH
ANY = pl.BlockSpec(memory_space=pl.ANY)

ZB_POOL, ZB_U, ZB_V, ZB_A, ZB_B, ZB_GATES = 0, 1, 2, 3, 4, 5
DZ_HALF = 2816

TRANSPOSED = ("ffn1_w_gate", "ffn1_w_up", "ffn2_w_gate", "ffn2_w_up")
BIG = (
    ("ffn1_w_gate", "row", FFN_SHARD, 1024, FFN_SHARD_PAD, 1024),
    ("ffn1_w_up", "row", FFN_SHARD, 1024, FFN_SHARD_PAD, 1024),
    ("ffn1_w_down", "row", FFN_SHARD, 1024, FFN_SHARD_PAD, 1024),
    ("w_in", "col", 1024, 1408, 1024, 1408),
    ("w_pool_out", "col", 512, 256, 512, 256),
    ("w_sgu_out", "col", 512, 256, 512, 256),
    ("w_conv_out", "col", 512, 256, 512, 256),
    ("w_out", "row", 256, 1024, 256, 1024),
    ("ffn2_w_gate", "row", FFN_SHARD, 1024, FFN_SHARD_PAD, 1024),
    ("ffn2_w_up", "row", FFN_SHARD, 1024, FFN_SHARD_PAD, 1024),
    ("ffn2_w_down", "row", FFN_SHARD, 1024, FFN_SHARD_PAD, 1024),
    ("ple_w_proj", "col", 256, 256, 256, 256),
    ("ple_w_gate", "row", 256, 1024, 256, 1024),
)
SMALL = ("ffn1_pre_g", "ffn1_post_g", "mix_pre_g", "pool_w", "pool_scale", "sgu_ln_g", "sgu_ln_b", "sgu_w_s",
         "sgu_b_s", "conv_dw_b", "conv_ln_g", "conv_ln_b", "mix_post_g", "ffn2_pre_g", "ffn2_post_g",
         "ple_pre_g", "ple_post_g")
WEIGHTS = ("ffn1_pre_g", "ffn1_w_gate", "ffn1_w_up", "ffn1_w_down", "ffn1_post_g", "mix_pre_g", "w_in", "pool_w",
           "pool_scale", "w_pool_out", "sgu_ln_g", "sgu_ln_b", "sgu_w_s", "sgu_b_s", "w_sgu_out", "conv_dw_k",
           "conv_dw_b", "conv_ln_g", "conv_ln_b", "w_conv_out", "w_out", "mix_post_g", "ffn2_pre_g", "ffn2_w_gate",
           "ffn2_w_up", "ffn2_w_down", "ffn2_post_g", "ple_w_proj", "ple_pre_g", "ple_w_gate", "ple_post_g")


def _params(n_grid):
    return pltpu.CompilerParams(dimension_semantics=("arbitrary",) * n_grid, vmem_limit_bytes=VMEM_LIMIT_BYTES)


def _tile(n, cap=ROW_TILE):
    for t in range(min(cap, n) - min(cap, n) % 16, 0, -16):
        if n % t == 0:
            return t
    return n


def _sigmoid(x):
    return 0.5 * jnp.tanh(0.5 * x) + 0.5


def _silu_and_grad(x):
    s = _sigmoid(x)
    return x * s, s * (1.0 + x * (1.0 - s))


def _gelu_and_grad(x):
    cdf = 0.5 * (1.0 + lax.erf(x * (1.0 / math.sqrt(2.0))))
    pdf = jnp.exp(-0.5 * x * x) * (1.0 / math.sqrt(2.0 * math.pi))
    return x * cdf, cdf + x * pdf


def _rms_fwd(x, g):
    return x * lax.rsqrt(jnp.mean(x * x, axis=-1, keepdims=True) + EPS) * g


def _rms_bwd(x, g, dy):
    r = lax.rsqrt(jnp.mean(x * x, axis=-1, keepdims=True) + EPS)
    xh = x * r
    dxh = dy * g
    dx = r * (dxh - xh * jnp.mean(dxh * xh, axis=-1, keepdims=True))
    return dx, jnp.sum(dy * xh, axis=0, keepdims=True)


def _ln_stats(x):
    xc = x - jnp.mean(x, axis=-1, keepdims=True)
    r = lax.rsqrt(jnp.mean(xc * xc, axis=-1, keepdims=True) + EPS)
    return xc * r, r


def _ln_bwd(xh, r, g, dy):
    dxh = dy * g
    dx = r * (dxh - jnp.mean(dxh, axis=-1, keepdims=True) - xh * jnp.mean(dxh * xh, axis=-1, keepdims=True))
    return dx, jnp.sum(dy * xh, axis=0, keepdims=True), jnp.sum(dy, axis=0, keepdims=True)


def _rowwise(name, fn, rows, consts, outs, accs=(), tm=ROW_TILE, deps=()):
    T = rows[0][0].shape[-2]
    tm = _tile(T, tm)
    n_in, n_o, n_dep = len(rows) + len(consts), len(outs), len(deps)

    def body(*refs):
        refs = refs[n_dep:]
        res = fn(*[r[...] for r in refs[:n_in]])
        for ref, val in zip(refs[n_in:n_in + n_o], res[:n_o]):
            ref[...] = val.astype(ref.dtype)
        acc_refs = refs[n_in + n_o:]
        if acc_refs:
            @pl.when(pl.program_id(0) == 0)
            def _():
                for ref, val in zip(acc_refs, res[n_o:]):
                    ref[...] = val

            @pl.when(pl.program_id(0) != 0)
            def _():
                for ref, val in zip(acc_refs, res[n_o:]):
                    ref[...] += val

    in_specs = [ANY] * n_dep
    for row in rows:
        w, cb = row[1], row[2]
        if len(row) == 4:
            in_specs.append(pl.BlockSpec((None, tm, w), lambda i, cb=cb, ld=row[3]: (ld, i, cb)))
        else:
            in_specs.append(pl.BlockSpec((tm, w), lambda i, cb=cb: (i, cb)))
    in_specs += [pl.BlockSpec(c.shape, lambda i: (0, 0)) for c in consts]
    out_specs = [pl.BlockSpec((tm, w), lambda i: (i, 0)) for w, _ in outs]
    out_specs += [pl.BlockSpec(s, lambda i: (0, 0)) for s in accs]
    out_shape = [jax.ShapeDtypeStruct((T, w), dt) for w, dt in outs]
    out_shape += [jax.ShapeDtypeStruct(s, F32) for s in accs]
    return pl.pallas_call(body, grid=(T // tm,), in_specs=in_specs, out_specs=out_specs, out_shape=out_shape,
                          name=name, compiler_params=_params(1))(*deps, *[r[0] for r in rows], *consts)


def _tiled(name, fn, grid, pos, ins, outs):
    n_in = len(ins)

    def body(_, *refs):
        res = fn(*[r[...] for r in refs[:n_in]])
        for ref, val in zip(refs[n_in:], res):
            ref[...] = val.astype(ref.dtype)

    spec = pltpu.PrefetchScalarGridSpec(
        num_scalar_prefetch=1, grid=grid, in_specs=[pl.BlockSpec(bs, im) for _, bs, im in ins],
        out_specs=[pl.BlockSpec(bs, im) for _, _, bs, im in outs])
    return pl.pallas_call(body, grid_spec=spec, out_shape=[jax.ShapeDtypeStruct(s, d) for s, d, _, _ in outs],
                          name=name, compiler_params=_params(len(grid)))(pos, *[a for a, _, _ in ins])


def _cast_layers(name, w, rp, cp, dtype, pos, deps=()):
    L, r, c = w.shape

    def body(_, w_ref, *rest):
        for k, o_ref in enumerate(rest[len(deps):]):
            @pl.when(pl.program_id(0) == k)
            def _(o_ref=o_ref):
                if (rp, cp) != (r, c):
                    o_ref[...] = jnp.zeros_like(o_ref)
                    o_ref[pl.ds(0, r), pl.ds(0, c)] = w_ref[...].astype(dtype)
                else:
                    o_ref[...] = w_ref[...].astype(dtype)

    spec = pltpu.PrefetchScalarGridSpec(
        num_scalar_prefetch=1, grid=(L,),
        in_specs=[pl.BlockSpec((None, r, c), lambda l, p: (l, 0, 0))] + [ANY] * len(deps),
        out_specs=[pl.BlockSpec((None, rp, cp), lambda l, p: (p[0], 0, 0))] * L)
    return pl.pallas_call(body, grid_spec=spec, out_shape=[jax.ShapeDtypeStruct((N_CHIPS, rp, cp), dtype)] * L,
                          name=name, compiler_params=_params(1))(pos, w, *deps)


_NN = (((1,), (0,)), ((), ()))
_NT = (((1,), (1,)), ((), ()))
_TN = (((0,), (0,)), ((), ()))


def _mm_tn(name, a, dy, buf, l, a_blocked, tk=ROW_TILE, first=0):
    T = a.shape[0]
    nb, R, C = buf[l].shape
    extra = [buf[l]] if first else []

    def body(a_ref, dy_ref, *rest):
        rest[-1][...] = lax.dot_general(a_ref[...].astype(BF), dy_ref[...].astype(BF), _TN,
                                        preferred_element_type=F32).astype(BF)

    if a_blocked:
        grid = (nb,)
        in_specs = [pl.BlockSpec((T, R), lambda b: (0, b)), pl.BlockSpec((T, C), lambda b: (0, 0))]
        out_specs = pl.BlockSpec((None, R, C), lambda b: (b, 0, 0))
    else:
        tk = min(tk, R)
        grid = (dy.shape[1] // C, R // tk)
        in_specs = [pl.BlockSpec((T, tk), lambda b, k: (0, k)), pl.BlockSpec((T, C), lambda b, k: (0, b))]
        out_specs = pl.BlockSpec((None, tk, C), lambda b, k: (b + first, k, 0))
    buf = list(buf)
    buf[l] = pl.pallas_call(body, grid=grid, in_specs=in_specs + [ANY] * len(extra), out_specs=out_specs,
                            out_shape=jax.ShapeDtypeStruct((nb, R, C), BF),
                            input_output_aliases={2: 0} if extra else {}, name=name,
                            compiler_params=_params(len(grid)))(a, dy, *extra)
    return buf


def _acc_rows(ref, val, first):
    @pl.when(first)
    def _():
        ref[...] = val

    @pl.when(jnp.logical_not(first))
    def _():
        ref[...] += val


def _norm_mm(name, h, g, ws, trans_w, act=False, deps=(), tm=2 * ROW_TILE):
    T = h.shape[0]
    nb, r, cc = ws[0].shape
    bo = r if trans_w else cc
    tm = _tile(T, tm)
    n_w, n_dep = len(ws), len(deps)

    def body(*refs):
        refs = refs[n_dep:]
        h_ref, g_ref, w_refs = refs[0], refs[1], refs[2:2 + n_w]
        n_ref, o_refs, n_s = refs[2 + n_w], refs[3 + n_w:3 + 2 * n_w], refs[-1]

        @pl.when(pl.program_id(1) == 0)
        def _():
            n = _rms_fwd(h_ref[...].astype(F32), g_ref[...]).astype(BF)
            n_s[...] = n
            n_ref[...] = n

        n = n_s[...]
        prods = []
        for w_ref, o_ref in zip(w_refs, o_refs):
            prods.append(lax.dot_general(n, w_ref[...], _NT if trans_w else _NN,
                                         preferred_element_type=F32).astype(BF))
            o_ref[...] = prods[-1]
        if act:
            refs[3 + 2 * n_w][...] = (_silu_and_grad(prods[0].astype(F32))[0] * prods[1].astype(F32)).astype(BF)

    wide = pl.BlockSpec((tm, bo), lambda i, b: (i, b))
    n_out = n_w + (1 if act else 0)
    return pl.pallas_call(
        body, grid=(T // tm, nb),
        in_specs=[ANY] * n_dep + [pl.BlockSpec((tm, D_MODEL), lambda i, b: (i, 0)),
                                  pl.BlockSpec(g.shape, lambda i, b: (0, 0))]
        + [pl.BlockSpec((None, r, cc), lambda i, b: (b, 0, 0))] * n_w,
        out_specs=[pl.BlockSpec((tm, D_MODEL), lambda i, b: (i, 0))] + [wide] * n_out,
        out_shape=[jax.ShapeDtypeStruct((T, D_MODEL), BF)] + [jax.ShapeDtypeStruct((T, nb * bo), BF)] * n_out,
        scratch_shapes=[pltpu.VMEM((tm, D_MODEL), BF)], name=name, compiler_params=_params(2))(*deps, h, g, *ws)


def _mm_res(name, x, w3, h, g, coef, tm=ROW_TILE):
    T, kx = x.shape
    w2 = w3.reshape(kx, D_MODEL)
    tm = _tile(T, tm)

    def body(x_ref, w_ref, h_ref, g_ref, f_ref, o_ref):
        f = jnp.dot(x_ref[...], w_ref[...], preferred_element_type=F32).astype(BF)
        f_ref[...] = f
        o_ref[...] = h_ref[...] + coef * _rms_fwd(f.astype(F32), g_ref[...])

    row = pl.BlockSpec((tm, D_MODEL), lambda i: (i, 0))
    return pl.pallas_call(
        body, grid=(T // tm,),
        in_specs=[pl.BlockSpec((tm, kx), lambda i: (i, 0)), pl.BlockSpec(w2.shape, lambda i: (0, 0)), row,
                  pl.BlockSpec(g.shape, lambda i: (0, 0))],
        out_specs=[row, row],
        out_shape=[jax.ShapeDtypeStruct((T, D_MODEL), BF), jax.ShapeDtypeStruct((T, D_MODEL), F32)],
        name=name, compiler_params=_params(1))(x, w2, h, g)


def _resbwd_mm(name, dh, f, g, coef, w3, trans_w, act=None, deps=(), tm=2 * ROW_TILE):
    T = dh.shape[0]
    nb, r, cc = w3.shape
    bo = r if trans_w else cc
    tm = _tile(T, tm)
    n_dep, n_act = len(deps), 3 if act else 0
    n_i = T // tm

    def body(*refs):
        refs = refs[n_dep:]
        dh_ref, f_ref, g_ref, w_ref = refs[:4]
        df_ref, dg_ref = refs[4 + n_act], refs[5 + n_act]
        df_s = refs[-2] if act else refs[-1]
        i, b = pl.program_id(0), pl.program_id(1)

        @pl.when(b == 0)
        def _():
            dg = jnp.zeros((1, D_MODEL), F32)
            for c in range(tm // EPI_ROWS):
                rows = slice(c * EPI_ROWS, (c + 1) * EPI_ROWS)
                dx, dg_c = _rms_bwd(f_ref[rows, :].astype(F32), g_ref[...], coef * dh_ref[rows, :])
                df_s[rows, :] = dx.astype(BF)
                df_ref[rows, :] = dx.astype(BF)
                dg = dg + dg_c
            _acc_rows(dg_ref, dg, i == 0)

        if act:
            for j in range(bo // MXU_TILE):
                cols = slice(j * MXU_TILE, (j + 1) * MXU_TILE)
                prod = lax.dot_general(df_s[...], w_ref[cols, :], _NT, preferred_element_type=F32)
                val, grad = _silu_and_grad(refs[4][:, cols].astype(F32))
                refs[6 + n_act][:, cols] = (prod * refs[5][:, cols].astype(F32) * grad).astype(BF)
                refs[7 + n_act][:, cols] = (prod * val).astype(BF)
            acc = refs[-1]
            part = lax.dot_general(refs[6][...], df_s[...], _TN, preferred_element_type=F32)

            @pl.when(i == 0)
            def _():
                acc[b] = part

            @pl.when(i != 0)
            def _():
                acc[b] += part

            @pl.when(i == n_i - 1)
            def _():
                refs[8 + n_act][...] = acc[b].astype(BF)
        else:
            refs[6][...] = lax.dot_general(df_s[...], w_ref[...], _NT if trans_w else _NN,
                                           preferred_element_type=F32).astype(BF)

    row = pl.BlockSpec((tm, D_MODEL), lambda i, b: (i, 0))
    wide = pl.BlockSpec((tm, bo), lambda i, b: (i, b))
    vec = pl.BlockSpec((1, D_MODEL), lambda i, b: (0, 0))
    out_specs = [row, vec] + [wide] * (2 if act else 1)
    out_shape = [jax.ShapeDtypeStruct((T, D_MODEL), BF), jax.ShapeDtypeStruct((1, D_MODEL), F32)]
    out_shape += [jax.ShapeDtypeStruct((T, nb * bo), BF)] * (2 if act else 1)
    scratch = [pltpu.VMEM((tm, D_MODEL), BF)]
    if act:
        out_specs.append(pl.BlockSpec((None, r, cc), lambda i, b: (jnp.where(i == n_i - 1, b, 0), 0, 0)))
        out_shape.append(jax.ShapeDtypeStruct((nb, r, cc), BF))
        scratch.append(pltpu.VMEM((nb, r, cc), F32))
    return pl.pallas_call(
        body, grid=(n_i, nb),
        in_specs=[ANY] * n_dep + [row, row, vec, pl.BlockSpec((None, r, cc), lambda i, b: (b, 0, 0))] + [wide] * n_act,
        out_specs=out_specs, out_shape=out_shape, scratch_shapes=scratch, name=name,
        compiler_params=_params(2))(*deps, dh, f, g, w3, *(act or ()))


def _dn_prenorm(name, xs, ws, trans_w, dh, h, g, tm=2 * ROW_TILE):
    T = dh.shape[0]
    chained = not isinstance(ws, (list, tuple))
    ws = [ws] if chained else list(ws)
    _, r, cc = ws[0].shape
    bw = cc if trans_w else r
    per_x = xs[0].shape[1] // bw
    nb = per_x * len(xs) if chained else per_x
    tm = _tile(T, tm)
    n_x, n_w = len(xs), len(ws)

    def body(*refs):
        x_refs, w_refs = refs[:n_x], refs[n_x:n_x + n_w]
        dh_ref, h_ref, g_ref, o_ref, dg_ref, acc = refs[n_x + n_w:]
        i, b = pl.program_id(0), pl.program_id(1)

        @pl.when(b == 0)
        def _():
            acc[...] = jnp.zeros_like(acc)

        def add(x_ref, w_ref):
            acc[...] += lax.dot_general(x_ref[...], w_ref[...], _NT if trans_w else _NN, preferred_element_type=F32)

        if chained:
            for k, x_ref in enumerate(x_refs):
                pl.when(b // per_x == k)(lambda x_ref=x_ref: add(x_ref, w_refs[0]))
        else:
            for x_ref, w_ref in zip(x_refs, w_refs):
                add(x_ref, w_ref)

        @pl.when(b == nb - 1)
        def _():
            dg = jnp.zeros((1, D_MODEL), F32)
            for c in range(tm // EPI_ROWS):
                rows = slice(c * EPI_ROWS, (c + 1) * EPI_ROWS)
                dx, dg_c = _rms_bwd(h_ref[rows, :], g_ref[...], acc[rows, :])
                o_ref[rows, :] = dh_ref[rows, :] + dx
                dg = dg + dg_c
            _acc_rows(dg_ref, dg, i == 0)

    row = pl.BlockSpec((tm, D_MODEL), lambda i, b: (i, 0))
    vec = pl.BlockSpec((1, D_MODEL), lambda i, b: (0, 0))
    if chained:
        x_specs = [pl.BlockSpec((tm, bw), lambda i, b, k=k: (i, jnp.clip(b - k * per_x, 0, per_x - 1)))
                   for k in range(n_x)]
    else:
        x_specs = [pl.BlockSpec((tm, bw), lambda i, b: (i, b))] * n_x
    return pl.pallas_call(
        body, grid=(T // tm, nb),
        in_specs=x_specs + [pl.BlockSpec((None, r, cc), lambda i, b: (b, 0, 0))] * n_w + [row, row, vec],
        out_specs=[row, vec],
        out_shape=[jax.ShapeDtypeStruct((T, D_MODEL), F32), jax.ShapeDtypeStruct((1, D_MODEL), F32)],
        scratch_shapes=[pltpu.VMEM((tm, D_MODEL), F32)], name=name,
        compiler_params=_params(2))(*xs, *ws, dh, h, g)


def _pool_apply(x, win, row):
    s, k = x, 1
    while k < win:
        s = s + jnp.where(row >= k, pltpu.roll(s, k, 0), 0.0)
        k *= 2
    return s / jnp.minimum(row + 1, win).astype(F32) - x


def _pool_apply_t(dp, win, row):
    T = dp.shape[0]
    s, k = dp / jnp.minimum(row + 1, win).astype(F32), 1
    while k < win:
        s = s + jnp.where(row < T - k, pltpu.roll(s, T - k, 0), 0.0)
        k *= 2
    return s - dp


def _pool_fwd(z, w, scale):
    T = z.shape[0]

    def body(z_ref, w_ref, s_ref, o_ref):
        row = lax.broadcasted_iota(jnp.int32, (T, LANES), 0)
        for gi, win in enumerate(POOL_WINDOWS):
            cols = pl.ds(gi * LANES, LANES)
            pooled = _pool_apply(z_ref[:, cols].astype(F32), win, row)
            y = jnp.dot(pooled.astype(BF), w_ref[gi].astype(BF), preferred_element_type=F32)
            o_ref[:, cols] = (y * s_ref[:, cols]).astype(o_ref.dtype)

    return pl.pallas_call(
        body, grid=(1,),
        in_specs=[pl.BlockSpec((T, 512), lambda i: (0, ZB_POOL)), pl.BlockSpec(w.shape, lambda i: (0, 0, 0)),
                  pl.BlockSpec(scale.shape, lambda i: (0, 0))],
        out_specs=pl.BlockSpec((T, 512), lambda i: (0, 0)), out_shape=jax.ShapeDtypeStruct((T, 512), BF),
        name="pool_fwd", compiler_params=_params(1))(z, w, scale)


def _pool_bwd(dr, z, w, scale):
    T = z.shape[0]

    def body(dr_ref, z_ref, w_ref, s_ref, dz_ref, dw_ref, ds_ref):
        row = lax.broadcasted_iota(jnp.int32, (T, LANES), 0)
        for gi, win in enumerate(POOL_WINDOWS):
            cols = pl.ds(gi * LANES, LANES)
            pooled = _pool_apply(z_ref[:, cols].astype(F32), win, row).astype(BF)
            wg = w_ref[gi].astype(BF)
            y = jnp.dot(pooled, wg, preferred_element_type=F32)
            d = dr_ref[:, cols].astype(F32)
            ds_ref[:, cols] = jnp.sum(d * y, axis=0, keepdims=True)
            dy = (d * s_ref[:, cols]).astype(BF)
            dw_ref[gi] = lax.dot_general(pooled, dy, _TN, preferred_element_type=F32)
            dpooled = lax.dot_general(dy, wg, _NT, preferred_element_type=F32)
            dz_ref[:, cols] = _pool_apply_t(dpooled, win, row).astype(dz_ref.dtype)

    return pl.pallas_call(
        body, grid=(1,),
        in_specs=[pl.BlockSpec((T, 512), lambda i: (0, 0)), pl.BlockSpec((T, 512), lambda i: (0, ZB_POOL)),
                  pl.BlockSpec(w.shape, lambda i: (0, 0, 0)), pl.BlockSpec(scale.shape, lambda i: (0, 0))],
        out_specs=[pl.BlockSpec((T, 512), lambda i: (0, 0)), pl.BlockSpec(w.shape, lambda i: (0, 0, 0)),
                   pl.BlockSpec(scale.shape, lambda i: (0, 0))],
        out_shape=[jax.ShapeDtypeStruct((T, 512), BF), jax.ShapeDtypeStruct(w.shape, F32),
                   jax.ShapeDtypeStruct(scale.shape, F32)],
        name="pool_bwd", compiler_params=_params(1))(dr, z, w, scale)


def _tril(transposed=False):
    r = lax.broadcasted_iota(jnp.int32, (CHUNK, CHUNK), 0)
    c = lax.broadcasted_iota(jnp.int32, (CHUNK, CHUNK), 1)
    return c >= r if transposed else r >= c


def _sgu_fwd(z, ln_g, ln_b, w_s, bias):
    T = z.shape[0]
    tm = _tile(T)

    def body(zu_ref, zv_ref, g_ref, b_ref, w_ref, bias_ref, o_ref):
        gu, _ = _gelu_and_grad(zu_ref[...].astype(F32))
        gv, _ = _gelu_and_grad(zv_ref[...].astype(F32))
        xh, _ = _ln_stats(gv)
        v16 = (xh * g_ref[...] + b_ref[...]).astype(BF)
        tri = _tril()
        for h in range(SGU_HEADS):
            cols = slice(h * LANES, (h + 1) * LANES)
            wh = jnp.where(tri, w_ref[h], 0.0).astype(BF)
            for c in range(tm // CHUNK):
                rows = slice(c * CHUNK, (c + 1) * CHUNK)
                s = jnp.dot(wh, v16[rows, cols], preferred_element_type=F32) + bias_ref[:, cols]
                o_ref[rows, cols] = (gu[rows, cols] * s).astype(o_ref.dtype)

    small = [pl.BlockSpec(a.shape, lambda i, n=a.ndim: (0,) * n) for a in (ln_g, ln_b, w_s, bias)]
    return pl.pallas_call(
        body, grid=(T // tm,),
        in_specs=[pl.BlockSpec((tm, 512), lambda i: (i, ZB_U)), pl.BlockSpec((tm, 512), lambda i: (i, ZB_V))] + small,
        out_specs=pl.BlockSpec((tm, 512), lambda i: (i, 0)), out_shape=jax.ShapeDtypeStruct((T, 512), BF),
        name="sgu_fwd", compiler_params=_params(1))(z, z, ln_g, ln_b, w_s, bias)


def _sgu_bwd(dr, z, ln_g, ln_b, w_s, w_st, bias):
    T = z.shape[0]
    tm = _tile(T)
    n_steps = T // tm

    def body(dr_ref, zu_ref, zv_ref, g_ref, b_ref, w_ref, wt_ref, bias_ref,
             dzu_ref, dzv_ref, dg_ref, db_ref, dw_ref, dbias_ref, dgu_s, dv_s):
        i = pl.program_id(0)

        @pl.when(i == 0)
        def _():
            dg_ref[...] = jnp.zeros_like(dg_ref)
            db_ref[...] = jnp.zeros_like(db_ref)
            dw_ref[...] = jnp.zeros_like(dw_ref)
            dbias_ref[...] = jnp.zeros_like(dbias_ref)

        zu = zu_ref[...].astype(F32)
        zv = zv_ref[...].astype(F32)
        gu, gu_grad = _gelu_and_grad(zu)
        gv, gv_grad = _gelu_and_grad(zv)
        xh, r = _ln_stats(gv)
        v16 = (xh * g_ref[...] + b_ref[...]).astype(BF)
        dr = dr_ref[...].astype(F32)
        tri = _tril()
        for h in range(SGU_HEADS):
            cols = slice(h * LANES, (h + 1) * LANES)
            wh = jnp.where(tri, w_ref[h], 0.0).astype(BF)
            wht = jnp.where(_tril(transposed=True), wt_ref[h], 0.0).astype(BF)
            for c in range(tm // CHUNK):
                rows = slice(c * CHUNK, (c + 1) * CHUNK)
                v_blk = v16[rows, cols]
                s = jnp.dot(wh, v_blk, preferred_element_type=F32) + bias_ref[:, cols]
                ds = dr[rows, cols] * gu[rows, cols]
                dgu_s[rows, cols] = dr[rows, cols] * s
                ds16 = ds.astype(BF)
                dw_ref[h] += jnp.where(tri, lax.dot_general(ds16, v_blk, _NT, preferred_element_type=F32), 0.0)
                dv_s[rows, cols] = jnp.dot(wht, ds16, preferred_element_type=F32)
                dbias_ref[:, cols] += ds
        dzu_ref[...] = (dgu_s[...] * gu_grad).astype(dzu_ref.dtype)
        dgv, dg, db = _ln_bwd(xh, r, g_ref[...], dv_s[...])
        dzv_ref[...] = (dgv * gv_grad).astype(dzv_ref.dtype)
        dg_ref[...] += dg
        db_ref[...] += db

        @pl.when(i == n_steps - 1)
        def _():
            for h in range(SGU_HEADS):
                cols = slice(h * LANES, (h + 1) * LANES)
                tot = jnp.sum(dbias_ref[:, cols], axis=1, keepdims=True)
                dbias_ref[:, cols] = jnp.broadcast_to(tot, (CHUNK, LANES))

    small = (ln_g, ln_b, w_s, w_st, bias)
    small_specs = [pl.BlockSpec(a.shape, lambda i, n=a.ndim: (0,) * n) for a in small]
    return pl.pallas_call(
        body, grid=(n_steps,),
        in_specs=[pl.BlockSpec((tm, 512), lambda i: (i, 0)), pl.BlockSpec((tm, 512), lambda i: (i, ZB_U)),
                  pl.BlockSpec((tm, 512), lambda i: (i, ZB_V))] + small_specs,
        out_specs=[pl.BlockSpec((tm, 512), lambda i: (i, 0)), pl.BlockSpec((tm, 512), lambda i: (i, 0)),
                   pl.BlockSpec((1, 512), lambda i: (0, 0)), pl.BlockSpec((1, 512), lambda i: (0, 0)),
                   pl.BlockSpec(w_s.shape, lambda i: (0, 0, 0)), pl.BlockSpec(bias.shape, lambda i: (0, 0))],
        out_shape=[jax.ShapeDtypeStruct((T, 512), BF), jax.ShapeDtypeStruct((T, 512), BF),
                   jax.ShapeDtypeStruct((1, 512), F32), jax.ShapeDtypeStruct((1, 512), F32),
                   jax.ShapeDtypeStruct(w_s.shape, F32), jax.ShapeDtypeStruct(bias.shape, F32)],
        scratch_shapes=[pltpu.VMEM((tm, 512), F32), pltpu.VMEM((tm, 512), F32)],
        name="sgu_bwd", compiler_params=_params(1))(dr, z, z, ln_g, ln_b, w_s, w_st, bias)


def _conv_fwd(z, convk, l, bias):
    T = z.shape[0]

    def body(za_ref, zb_ref, k_ref, b_ref, o_ref):
        xg = za_ref[...].astype(F32) * _sigmoid(zb_ref[...].astype(F32))
        xp = jnp.concatenate([jnp.zeros((CONV_PAD, LANES), F32), xg], axis=0)
        kw = k_ref[...]
        acc = jnp.broadcast_to(b_ref[...], (T, LANES))
        for s in range(SUBLANES):
            xs = xp if s == 0 else pltpu.roll(xp, s, 0)
            for q in range(CONV_PAD // SUBLANES):
                k = CONV_TAPS - 1 - (SUBLANES * q + s)
                if k >= 0:
                    lo = CONV_PAD - SUBLANES * q
                    acc = acc + kw[k:k + 1, :] * xs[lo:lo + T, :]
        o_ref[...] = acc.astype(o_ref.dtype)

    return pl.pallas_call(
        body, grid=(4,),
        in_specs=[pl.BlockSpec((T, LANES), lambda g: (0, 4 * ZB_A + g)),
                  pl.BlockSpec((T, LANES), lambda g: (0, 4 * ZB_B + g)),
                  pl.BlockSpec((None, CONV_PAD, LANES), lambda g: (g, 0, 0)),
                  pl.BlockSpec((1, LANES), lambda g: (0, g))],
        out_specs=pl.BlockSpec((T, LANES), lambda g: (0, g)), out_shape=jax.ShapeDtypeStruct((T, 512), BF),
        name="conv_fwd", compiler_params=_params(1))(z, z, convk[l], bias)


def _conv_bwd(dy, z, convk, l):
    T = z.shape[0]

    def body(dy_ref, za_ref, zb_ref, k_ref, dza_ref, dzb_ref, dk_ref, db_ref):
        a = za_ref[...].astype(F32)
        sg = _sigmoid(zb_ref[...].astype(F32))
        d = dy_ref[...].astype(F32)
        kw = k_ref[...]
        xp = jnp.concatenate([jnp.zeros((CONV_PAD, LANES), F32), a * sg], axis=0)
        dp = jnp.concatenate([d, jnp.zeros((CONV_PAD, LANES), F32)], axis=0)
        dxg = jnp.zeros((T, LANES), F32)
        dk_ref[...] = jnp.zeros_like(dk_ref)
        for s in range(SUBLANES):
            xs = xp if s == 0 else pltpu.roll(xp, s, 0)
            ds = dp if s == 0 else pltpu.roll(dp, T + CONV_PAD - s, 0)
            for q in range(CONV_PAD // SUBLANES):
                k = CONV_TAPS - 1 - (SUBLANES * q + s)
                if k >= 0:
                    lo = CONV_PAD - SUBLANES * q
                    dk_ref[k:k + 1, :] = jnp.sum(d * xs[lo:lo + T, :], axis=0, keepdims=True)
                    dxg = dxg + kw[k:k + 1, :] * ds[SUBLANES * q:SUBLANES * q + T, :]
        db_ref[...] = jnp.sum(d, axis=0, keepdims=True)
        dza_ref[...] = (dxg * sg).astype(dza_ref.dtype)
        dzb_ref[...] = (dxg * a * sg * (1.0 - sg)).astype(dzb_ref.dtype)

    col = pl.BlockSpec((T, LANES), lambda g: (0, g))
    return pl.pallas_call(
        body, grid=(4,),
        in_specs=[col, pl.BlockSpec((T, LANES), lambda g: (0, 4 * ZB_A + g)),
                  pl.BlockSpec((T, LANES), lambda g: (0, 4 * ZB_B + g)),
                  pl.BlockSpec((None, CONV_PAD, LANES), lambda g: (g, 0, 0))],
        out_specs=[col, col, pl.BlockSpec((CONV_PAD, LANES), lambda g: (0, g)),
                   pl.BlockSpec((1, LANES), lambda g: (0, g))],
        out_shape=[jax.ShapeDtypeStruct((T, 512), BF), jax.ShapeDtypeStruct((T, 512), BF),
                   jax.ShapeDtypeStruct((CONV_PAD, 512), F32), jax.ShapeDtypeStruct((1, 512), F32)],
        name="conv_bwd", compiler_params=_params(1))(dy, z, z, convk[l])


D = D_MODEL


def _ffn_fwd(l, h, S, W, pre, deps=()):
    n, gp, u, a = _norm_mm("ffn_in", h, S[pre + "_pre_g"], [W[pre + "_w_gate"][l], W[pre + "_w_up"][l]], True,
                           act=True, deps=deps)
    f, out = _mm_res("ffn_out", a, W[pre + "_w_down"][l], h, S[pre + "_post_g"], 0.5)
    return out, dict(h=h, n=n, gp=gp, u=u, a=a, f=f)


def _ffn_bwd(l, dh, sv, S, W, G, SG, pre, deps=()):
    df, SG[pre + "_post_g"], dgp, du, dwd = _resbwd_mm(
        "ffn_bwd_act", dh, sv["f"], S[pre + "_post_g"], 0.5, W[pre + "_w_down"][l], True,
        act=(sv["gp"], sv["u"], sv["a"]), deps=deps, tm=ROW_TILE)
    G[pre + "_w_down"] = G[pre + "_w_down"][:l] + [dwd] + G[pre + "_w_down"][l + 1:]
    G[pre + "_w_gate"] = _mm_tn("ffn_dw_gate", dgp, sv["n"], G[pre + "_w_gate"], l, True)
    G[pre + "_w_up"] = _mm_tn("ffn_dw_up", du, sv["n"], G[pre + "_w_up"], l, True)
    dh_in, SG[pre + "_pre_g"] = _dn_prenorm("ffn_bwd_in", [dgp, du], [W[pre + "_w_gate"][l], W[pre + "_w_up"][l]],
                                            False, dh, sv["h"], S[pre + "_pre_g"])
    return dh_in


def _gates(zg):
    return [_sigmoid(jnp.concatenate([zg[2 * k].astype(F32), zg[2 * k + 1].astype(F32)], axis=1)) for k in range(3)]


def _merge_fwd(z, rs, ws, tm=ROW_TILE):
    T = z.shape[0]
    tm = _tile(T, tm)
    nb, kk, bw = ws[0].shape

    def body(*refs):
        r_refs, g_refs, w_refs, y_refs, m_ref = refs[:3], refs[3:9], refs[9:12], refs[12:15], refs[15]
        for r_ref, w_ref, y_ref in zip(r_refs, w_refs, y_refs):
            for b in range(nb):
                y_ref[:, b * bw:(b + 1) * bw] = jnp.dot(r_ref[...], w_ref[b],
                                                        preferred_element_type=F32).astype(y_ref.dtype)
        g = _gates([q[...] for q in g_refs])
        m_ref[...] = (g[0] * y_refs[0][...].astype(F32) + g[1] * y_refs[1][...].astype(F32)
                      + g[2] * y_refs[2][...].astype(F32)).astype(m_ref.dtype)

    row = pl.BlockSpec((tm, D_MODEL), lambda i: (i, 0))
    return pl.pallas_call(
        body, grid=(T // tm,),
        in_specs=[pl.BlockSpec((tm, kk), lambda i: (i, 0))] * 3
        + [pl.BlockSpec((tm, 512), lambda i, j=j: (i, ZB_GATES + j)) for j in range(6)]
        + [pl.BlockSpec(ws[0].shape, lambda i: (0, 0, 0))] * 3,
        out_specs=[row] * 4, out_shape=[jax.ShapeDtypeStruct((T, D_MODEL), BF)] * 4,
        name="mix_merge", compiler_params=_params(1))(*rs, *[z] * 6, *ws)


def _merge_bwd(dmerged, z, ys, ws, tm=ROW_TILE // 2):
    T = z.shape[0]
    tm = _tile(T, tm)
    nb, kk, bw = ws[0].shape

    def body(*refs):
        dm_ref, g_refs, y_refs, w_refs = refs[0], refs[1:7], refs[7:10], refs[10:13]
        dy_refs, lo_ref, hi_ref, dr_refs = refs[13:16], refs[16], refs[17], refs[18:21]
        cut = DZ_HALF - ZB_GATES * 512
        dm = dm_ref[...].astype(F32)
        g = _gates([q[...] for q in g_refs])
        for k in range(3):
            dy_refs[k][...] = (dm * g[k]).astype(BF)
            dzg = (dm * y_refs[k][...].astype(F32) * g[k] * (1.0 - g[k])).astype(BF)
            if k == 0:
                lo_ref[...] = dzg[:, :cut]
                hi_ref[:, :D_MODEL - cut] = dzg[:, cut:]
            else:
                hi_ref[:, k * D_MODEL - cut:(k + 1) * D_MODEL - cut] = dzg
            dr = None
            for b in range(nb):
                p = lax.dot_general(dy_refs[k][:, b * bw:(b + 1) * bw], w_refs[k][b], _NT,
                                    preferred_element_type=F32)
                dr = p if dr is None else dr + p
            dr_refs[k][...] = dr.astype(BF)

    row = pl.BlockSpec((tm, D_MODEL), lambda i: (i, 0))
    return pl.pallas_call(
        body, grid=(T // tm,),
        in_specs=[row] + [pl.BlockSpec((tm, 512), lambda i, j=j: (i, ZB_GATES + j)) for j in range(6)] + [row] * 3
        + [pl.BlockSpec(ws[0].shape, lambda i: (0, 0, 0))] * 3,
        out_specs=[row] * 3 + [pl.BlockSpec((tm, DZ_HALF - ZB_GATES * 512), lambda i: (i, 0)),
                               pl.BlockSpec((tm, DZ_HALF), lambda i: (i, 0))]
        + [pl.BlockSpec((tm, kk), lambda i: (i, 0))] * 3,
        out_shape=[jax.ShapeDtypeStruct((T, D_MODEL), BF)] * 3
        + [jax.ShapeDtypeStruct((T, DZ_HALF - ZB_GATES * 512), BF), jax.ShapeDtypeStruct((T, DZ_HALF), BF)]
        + [jax.ShapeDtypeStruct((T, kk), BF)] * 3,
        name="mix_merge_bwd", compiler_params=_params(1))(dmerged, *[z] * 6, *ys, *ws)


def _mix_fwd(l, h, S, W, deps=()):
    n, z = _norm_mm("mix_in", h, S["mix_pre_g"], [W["w_in"][l]], False, deps=deps)
    r_pool = _pool_fwd(z, S["pool_w"], S["pool_scale"])
    r_sgu = _sgu_fwd(z, S["sgu_ln_g"], S["sgu_ln_b"], S["sgu_w_s"], S["sgu_bias"])
    yc = _conv_fwd(z, W["conv_dw_k"], l, S["conv_dw_b"])

    def ln_silu(y, g, b):
        xh, _ = _ln_stats(y.astype(F32))
        return (_silu_and_grad(xh * g + b)[0],)

    r_conv = _rowwise("conv_ln", ln_silu, [(yc, 512, 0)], [S["conv_ln_g"], S["conv_ln_b"]], [(512, BF)])[0]
    y_pool, y_sgu, y_conv, merged = _merge_fwd(z, (r_pool, r_sgu, r_conv),
                                               [W["w_%s_out" % br][l] for br in ("pool", "sgu", "conv")])
    o, out = _mm_res("mix_out", merged, W["w_out"][l], h, S["mix_post_g"], 1.0)
    return out, dict(h=h, n=n, z=z, r_pool=r_pool, r_sgu=r_sgu, yc=yc, r_conv=r_conv, y_pool=y_pool, y_sgu=y_sgu,
                     y_conv=y_conv, merged=merged, o=o)


def _mix_bwd(l, dh, sv, S, W, G, SG, deps=()):
    z = sv["z"]
    do, SG["mix_post_g"], dmerged = _resbwd_mm("mix_bwd_out", dh, sv["o"], S["mix_post_g"], 1.0,
                                               W["w_out"][l].reshape(1, D, D), True, deps=deps)
    G["w_out"] = _mm_tn("mix_dw_out", sv["merged"], do, G["w_out"], l, True)

    branches = ("pool", "sgu", "conv")
    res = _merge_bwd(dmerged, z, [sv["y_" + br] for br in branches], [W["w_%s_out" % br][l] for br in branches])
    dz_gate_lo, dz_hi, dr = res[3], res[4], dict(zip(branches, res[5:]))
    for br, dy in zip(branches, res[:3]):
        wn = "w_%s_out" % br
        G[wn] = _mm_tn("branch_dw", sv["r_" + br], dy, G[wn], l, False)
    dz_pool, SG["pool_w"], SG["pool_scale"] = _pool_bwd(dr["pool"], z, S["pool_w"], S["pool_scale"])
    dzu, dzv, SG["sgu_ln_g"], SG["sgu_ln_b"], SG["sgu_w_s"], dbias = _sgu_bwd(
        dr["sgu"], z, S["sgu_ln_g"], S["sgu_ln_b"], S["sgu_w_s"], S["sgu_w_st"], S["sgu_bias"])
    SG["sgu_b_s"] = dbias[:, ::LANES].T

    def ln_silu_bwd(d, y, g, b):
        xh, r = _ln_stats(y.astype(F32))
        _, grad = _silu_and_grad(xh * g + b)
        return _ln_bwd(xh, r, g, d.astype(F32) * grad)

    dyc, SG["conv_ln_g"], SG["conv_ln_b"] = _rowwise(
        "conv_ln_bwd", ln_silu_bwd, [(dr["conv"], 512, 0), (sv["yc"], 512, 0)], [S["conv_ln_g"], S["conv_ln_b"]],
        [(512, BF)], [(1, 512), (1, 512)])
    dza, dzb, SG["conv_dw_k"], SG["conv_dw_b"] = _conv_bwd(dyc, z, W["conv_dw_k"], l)
    dz_lo = jnp.concatenate([dz_pool, dzu, dzv, dza, dzb, dz_gate_lo], axis=1)
    G["w_in"] = _mm_tn("mix_dw_in", sv["n"], dz_lo, G["w_in"], l, False)
    G["w_in"] = _mm_tn("mix_dw_in", sv["n"], dz_hi, G["w_in"], l, False, first=2)
    dh_in, SG["mix_pre_g"] = _dn_prenorm("mix_bwd_in", [dz_lo, dz_hi], W["w_in"][l], True, dh, sv["h"],
                                         S["mix_pre_g"])
    return dh_in


def _ple_out(h, p, gp, w3, g, tm=ROW_TILE):
    T, kp = p.shape
    nb, _, bw = w3.shape
    tm = _tile(T, tm)

    def body(h_ref, p_ref, gp_ref, w_ref, g_ref, e_ref, o_ref):
        p16 = p_ref[...].astype(BF)
        for b in range(nb):
            e_ref[:, b * bw:(b + 1) * bw] = jnp.dot(p16, w_ref[b], preferred_element_type=F32).astype(BF)
        q = _sigmoid(gp_ref[...].astype(F32)) * e_ref[...].astype(F32)
        o_ref[...] = h_ref[...] + _rms_fwd(q, g_ref[...])

    row = pl.BlockSpec((tm, D_MODEL), lambda i: (i, 0))
    return pl.pallas_call(
        body, grid=(T // tm,),
        in_specs=[row, pl.BlockSpec((tm, kp), lambda i: (i, 0)), row, pl.BlockSpec(w3.shape, lambda i: (0, 0, 0)),
                  pl.BlockSpec(g.shape, lambda i: (0, 0))],
        out_specs=[row, row],
        out_shape=[jax.ShapeDtypeStruct((T, D_MODEL), BF), jax.ShapeDtypeStruct((T, D_MODEL), F32)],
        name="ple_out", compiler_params=_params(1))(h, p, gp, w3, g)


def _ple_fwd(l, h, p_l, S, W, deps=()):
    n, gp = _norm_mm("ple_in", h, S["ple_pre_g"], [W["ple_w_gate"][l].reshape(1, D, D)], False, deps=deps)
    e, out = _ple_out(h, p_l, gp, W["ple_w_proj"][l], S["ple_post_g"])
    return out, dict(h=h, n=n, e=e, gp=gp, p=p_l)


def _ple_bwd_rows(dh, e, gp, g_post, w3, h, g_pre, deps=(), tm=ROW_TILE):
    T = dh.shape[0]
    w2 = w3.reshape(D_MODEL, D_MODEL)
    tm = _tile(T, tm)
    n_dep = len(deps)

    def body(*refs):
        dh_ref, e_ref, gp_ref, gpost_ref, w_ref, h_ref, gpre_ref, de_ref, dgp_ref, o_ref, dpost_ref, dpre_ref = \
            refs[n_dep:]
        first = pl.program_id(0) == 0
        d = dh_ref[...]
        sg = _sigmoid(gp_ref[...].astype(F32))
        ee = e_ref[...].astype(F32)
        dq, dpost = _rms_bwd(sg * ee, gpost_ref[...], d)
        de_ref[...] = (dq * sg).astype(BF)
        dgp = (dq * ee * sg * (1.0 - sg)).astype(BF)
        dgp_ref[...] = dgp
        dn = lax.dot_general(dgp, w_ref[...], _NT, preferred_element_type=F32)
        dx, dpre = _rms_bwd(h_ref[...], gpre_ref[...], dn)
        o_ref[...] = d + dx
        _acc_rows(dpost_ref, dpost, first)
        _acc_rows(dpre_ref, dpre, first)

    row = pl.BlockSpec((tm, D_MODEL), lambda i: (i, 0))
    vec = pl.BlockSpec((1, D_MODEL), lambda i: (0, 0))
    return pl.pallas_call(
        body, grid=(T // tm,),
        in_specs=[ANY] * n_dep + [row, row, row, vec, pl.BlockSpec(w2.shape, lambda i: (0, 0)), row, vec],
        out_specs=[row, row, row, vec, vec],
        out_shape=[jax.ShapeDtypeStruct((T, D_MODEL), BF)] * 2 + [jax.ShapeDtypeStruct((T, D_MODEL), F32)]
        + [jax.ShapeDtypeStruct((1, D_MODEL), F32)] * 2,
        name="ple_bwd", compiler_params=_params(1))(*deps, dh, e, gp, g_post, w2, h, g_pre)


def _ple_bwd(l, dh, sv, S, W, G, SG, deps=()):
    de, dgp, dh_in, SG["ple_post_g"], SG["ple_pre_g"] = _ple_bwd_rows(
        dh, sv["e"], sv["gp"], S["ple_post_g"], W["ple_w_gate"][l], sv["h"], S["ple_pre_g"], deps)
    G["ple_w_proj"] = _mm_tn("ple_dw_proj", sv["p"], de, G["ple_w_proj"], l, False)
    G["ple_w_gate"] = _mm_tn("ple_dw_gate", sv["n"], dgp, G["ple_w_gate"], l, True)
    return dh_in


def _layer_small(a, l):
    S = {}
    for name in SMALL:
        v = a[name][l]
        S[name] = v.reshape(1, -1) if v.ndim == 1 else v
    S["sgu_w_st"] = jnp.swapaxes(S["sgu_w_s"], 1, 2)
    S["sgu_bias"] = jnp.repeat(S["sgu_b_s"].T, LANES, axis=1)
    return S


def _layer_fwd(l, h, p_l, S, W, deps=(), hooks=None):
    hooks = hooks or {}

    def after(part, hv):
        return hooks[part](hv) if part in hooks else ()

    h, sv1 = _ffn_fwd(l, h, S, W, "ffn1", deps)
    h, sv2 = _mix_fwd(l, h, S, W, after("ffn1", h))
    h, sv3 = _ffn_fwd(l, h, S, W, "ffn2", after("mix", h))
    h, sv4 = _ple_fwd(l, h, p_l, S, W, after("ffn2", h))
    return h, (sv1, sv2, sv3, sv4)


def _layer_bwd(l, dh, sv, S, W, G, deps=(), mid=None):
    SG = {}
    dh = _ple_bwd(l, dh, sv[3], S, W, G, SG, deps)
    dh = _ffn_bwd(l, dh, sv[2], S, W, G, SG, "ffn2")
    dh = _mix_bwd(l, dh, sv[1], S, W, G, SG, mid(dh) if mid else ())
    dh = _ffn_bwd(l, dh, sv[0], S, W, G, SG, "ffn1")
    return dh, SG


HBM = pl.BlockSpec(memory_space=pltpu.HBM)
SEM = pl.BlockSpec(memory_space=pltpu.SEMAPHORE)
SIDE_EFFECT = pltpu.SideEffectType.DATAFLOW_SIDE_EFFECTING


def _place():
    x, y, c = lax.axis_index("x"), lax.axis_index("y"), lax.axis_index("c")
    chips = [(1 - x, y), (x, 1 - y), (1 - x, 1 - y)]
    return x, y, c, chips


def _remote(src, dst, send_sem, recv_sem, to):
    return pltpu.make_async_remote_copy(src_ref=src, dst_ref=dst, send_sem=send_sem, recv_sem=recv_sem,
                                        device_id=to, device_id_type=MESH)


def _split_start(name, plan, bufs, deps):
    count, fn = plan
    n, nd = len(bufs), len(deps)

    def body(*refs):
        send, recv = refs[nd + n], refs[nd + n + 1]
        x, y, c, chips = _place()
        for k, (src, dst, _, to) in enumerate(fn(refs[nd:nd + n], x, y, c, chips)):
            _remote(src, dst, send.at[k], recv.at[k], to).start()
        refs[-1][...] = jnp.zeros_like(refs[-1])

    res = pl.pallas_call(
        body, in_specs=[ANY] * nd + [HBM] * n,
        out_specs=[SEM, SEM] + [HBM] * n + [pl.BlockSpec(memory_space=pltpu.VMEM)],
        out_shape=[pltpu.SemaphoreType.DMA((count,)), pltpu.SemaphoreType.DMA((count,))]
        + [pltpu.HBM(b.shape, b.dtype) for b in bufs] + [jax.ShapeDtypeStruct((8, LANES), F32)],
        input_output_aliases={nd + i: 2 + i for i in range(n)}, name=name,
        compiler_params=pltpu.CompilerParams(has_side_effects=SIDE_EFFECT),
    )(*deps, *[pltpu.with_memory_space_constraint(b, pltpu.HBM) for b in bufs])
    return (res[0], res[1]), list(res[2:2 + n]), res[-1]


def _split_wait(name, plan, sems, bufs, after):
    _, fn = plan
    n = len(bufs)

    def body(*refs):
        send, recv = refs[n], refs[n + 1]
        x, y, c, chips = _place()
        for k, (src, _, land, to) in enumerate(fn(refs[:n], x, y, c, chips)):
            cp = _remote(src, land, send.at[k], recv.at[k], to)
            cp.wait_send()
            cp.wait_recv()

    res = pl.pallas_call(
        body, in_specs=[HBM] * n + [SEM, SEM] + [ANY] * len(after), out_specs=[HBM] * n,
        out_shape=[pltpu.HBM(b.shape, b.dtype) for b in bufs], input_output_aliases={i: i for i in range(n)},
        name=name, compiler_params=pltpu.CompilerParams(has_side_effects=SIDE_EFFECT))(*bufs, *sems, *after)
    return list(res)


def _gather_plans(n):
    def across(b, x, y, c, chips):
        me, out = 2 * x + y, []
        for a in range(n):
            rh = b[a].shape[1] // 2
            mine = b[a].at[me, pl.ds(c * rh, rh)]
            for cx, cy in chips:
                out.append((mine, mine, b[a].at[2 * cx + cy, pl.ds(c * rh, rh)], (cx, cy, c)))
        return out

    def to_sibling(b, x, y, c, chips):
        out = []
        for a in range(n):
            rh = b[a].shape[1] // 2
            for cx, cy in chips:
                piece = b[a].at[2 * cx + cy, pl.ds(c * rh, rh)]
                out.append((piece, piece, b[a].at[2 * cx + cy, pl.ds((1 - c) * rh, rh)], (x, y, 1 - c)))
        return out

    return (3 * n, across), (3 * n, to_sibling)


def _pair_plan(n):
    def fn(b, x, y, c, chips):
        out = []
        for a in range(n):
            rh = b[a].shape[1] // 2
            out.append((b[a].at[:, pl.ds((1 - c) * rh, rh)], b[n + a], b[n + a], (x, y, 1 - c)))
        return out

    return n, fn


def _cross_plan(n):
    def fn(b, x, y, c, chips):
        out = []
        for a in range(n):
            for j, (cx, cy) in enumerate(chips):
                out.append((b[a].at[2 * cx + cy], b[n + a].at[j], b[n + a].at[j], (cx, cy, c)))
        return out

    return 3 * n, fn


def _share_plan(n, l):
    def fn(b, x, y, c, chips):
        out = []
        for a in range(n):
            rh = b[a].shape[1] // 2
            mine = b[a].at[l, pl.ds(c * rh, rh)]
            out.append((mine, mine, b[a].at[l, pl.ds((1 - c) * rh, rh)], (x, y, 1 - c)))
        return out

    return n, fn


def _exchange_call(name, body, ins, out_shapes, n_remote, in_place=False):
    scratch = [pltpu.SemaphoreType.DMA((n_remote,)), pltpu.SemaphoreType.DMA((n_remote,))]
    aliases = {i: i for i in range(len(ins))} if in_place else {}
    return pl.pallas_call(body, in_specs=[ANY] * len(ins), out_specs=[ANY] * len(out_shapes), out_shape=out_shapes,
                          scratch_shapes=scratch, input_output_aliases=aliases, name=name)(*ins)


def _peers(x, y, c):
    return [(1 - x if m & 4 else x, 1 - y if m & 2 else y, 1 - c if m & 1 else c) for m in range(1, 8)]


def _scatter_small(v3):
    def body(v_ref, o_ref, send, recv):
        x, y, c, _ = _place()
        cps = []
        for m, (px, py, pc) in enumerate(_peers(x, y, c)):
            cps.append(_remote(v_ref.at[4 * px + 2 * py + pc], o_ref.at[m], send.at[m], recv.at[m], (px, py, pc)))
            cps[-1].start()
        for cp in cps:
            cp.wait()

    return _exchange_call("scatter_small", body, [v3], [jax.ShapeDtypeStruct((7,) + v3.shape[1:], v3.dtype)], 7)[0]


def _gather_small(buf):
    def body(_, o_ref, send, recv):
        x, y, c, _ = _place()
        mine = o_ref.at[4 * x + 2 * y + c]
        peers = _peers(x, y, c)
        cps = []
        for m, to in enumerate(peers):
            cps.append(_remote(mine, mine, send.at[m], recv.at[m], to))
            cps[-1].start()
        for m, (px, py, pc) in enumerate(peers):
            slab = o_ref.at[4 * px + 2 * py + pc]
            _remote(slab, slab, send.at[m], recv.at[m], (px, py, pc)).wait_recv()
            cps[m].wait_send()

    return _exchange_call("gather_small", body, [buf], [jax.ShapeDtypeStruct(buf.shape, buf.dtype)], 7,
                          in_place=True)[0]


def _allreduce_small(v, pos):
    rows = v.shape[0]
    rs = rows // 8
    v3 = v.reshape(8, rs, LANES)
    got = _scatter_small(v3)
    tm = _tile(rs)
    ins = [(v3, (None, tm, LANES), lambda i, p: (p[2], i, 0))]
    ins += [(got, (None, tm, LANES), lambda i, p, m=m: (m, i, 0)) for m in range(7)]
    buf = _tiled("sum_small", lambda *t: (((((((t[0] + t[1]) + t[2]) + t[3]) + t[4]) + t[5]) + t[6]) + t[7],),
                 (rs // tm,), pos, ins, [((8, rs, LANES), F32, (None, tm, LANES), lambda i, p: (p[2], i, 0))])[0]
    return _gather_small(buf).reshape(rows, LANES)


ADD_ROWS = 128


def _multi_tiled(name, fn, pos, groups, in_place=False):
    steps = max(g[1] for g in groups)
    flat_in, in_specs, out_specs, out_shape, counts, dests = [], [], [], [], [], []
    for ins, n_t, (shape, dtype, oidx, dest) in groups:
        for arr, idx in ins:
            flat_in.append(arr)
            in_specs.append(pl.BlockSpec((ADD_ROWS, arr.shape[1]),
                                         lambda i, p, idx=idx, n_t=n_t: (idx(jnp.minimum(i, n_t - 1), p), 0)))
        out_specs.append(pl.BlockSpec((ADD_ROWS, shape[1]),
                                      lambda i, p, oidx=oidx, n_t=n_t: (oidx(jnp.minimum(i, n_t - 1), p), 0)))
        out_shape.append(jax.ShapeDtypeStruct(shape, dtype))
        counts.append((len(ins), n_t))
        dests.append(dest)
    n_in = len(flat_in)
    extra = dests if in_place else []

    def body(_, *refs):
        outs = refs[n_in + len(extra):]
        k = 0
        for (n_a, n_t), o_ref in zip(counts, outs):
            tiles = refs[k:k + n_a]
            k += n_a

            @pl.when(pl.program_id(0) < n_t)
            def _(tiles=tiles, o_ref=o_ref):
                o_ref[...] = fn(*[t[...] for t in tiles]).astype(o_ref.dtype)

    spec = pltpu.PrefetchScalarGridSpec(num_scalar_prefetch=1, grid=(steps,),
                                        in_specs=in_specs + [ANY] * len(extra), out_specs=out_specs)
    return pl.pallas_call(body, grid_spec=spec, out_shape=out_shape,
                          input_output_aliases={1 + n_in + k: k for k in range(len(extra))}, name=name,
                          compiler_params=_params(1))(pos, *flat_in, *extra)


def _add_pair(grads, got, pos):
    groups = []
    for g, q in zip(grads, got):
        nb, R, C = g.shape
        rh = R // 2
        nh = rh // ADD_ROWS
        groups.append(([(g.reshape(nb * R, C), lambda t, p, nh=nh: (t // nh) * 2 * nh + p[1] * nh + t % nh),
                        (q.reshape(nb * rh, C), lambda t, p: t)], nb * nh,
                       ((nb * rh, C), BF, lambda t, p: t, None)))
    res = _multi_tiled("rs_add_pair", lambda u, w: u.astype(F32) + w.astype(F32), pos, groups)
    return [t.reshape(q.shape) for t, q in zip(res, got)]


def _add_chips(parts, slots, reduced, l, pos):
    def add(own, s0, s1, s2):
        return ((own.astype(F32) + s0.astype(F32)) + s1.astype(F32)) + s2.astype(F32)

    groups = []
    for t, s, red in zip(parts, slots, reduced):
        nb, rh, C = t.shape
        L = red.shape[0]
        nh = rh // ADD_ROWS
        ins = [(t.reshape(nb * rh, C), lambda i, p, nh=nh: p[0] * nh + i)]
        ins += [(s.reshape(3 * rh, C), lambda i, p, j=j, nh=nh: j * nh + i) for j in range(3)]
        groups.append((ins, nh, ((L * 2 * rh, C), F32, lambda i, p, nh=nh: l * 2 * nh + p[1] * nh + i,
                                 red.reshape(L * 2 * rh, C))))
    res = _multi_tiled("rs_add_chips", add, pos, groups, in_place=True)
    return [buf.reshape(red.shape) for buf, red in zip(res, reduced)]


def _adamw_math(w, g, m, v):
    m = ADAM_B1 * m + (1.0 - ADAM_B1) * g
    v = ADAM_B2 * v + (1.0 - ADAM_B2) * (g * g)
    m_hat = m / (1.0 - ADAM_B1 ** ADAM_STEP)
    v_hat = v / (1.0 - ADAM_B2 ** ADAM_STEP)
    return -ADAM_LR * (m_hat / (jnp.sqrt(v_hat) + ADAM_EPS) + ADAM_WD * w), m, v


def _adamw(w, g, m, v, lo=0, hi=None, into=None, deps=()):
    L, R, C = w.shape
    hi = L if hi is None else hi
    tr = _tile(R, max(16, ADAM_TILE_ELEMS // C))
    extra = (list(into) if into else []) + list(deps)
    n_alias = 4 if into else 0

    def body(w_ref, g_ref, m_ref, v_ref, *rest):
        go_ref, d_ref, mo_ref, vo_ref = rest[len(extra):]
        gv = g_ref[...]
        d, mn, vn = _adamw_math(w_ref[...], gv, m_ref[...], v_ref[...])
        go_ref[...] = gv
        d_ref[...] = d
        mo_ref[...] = mn
        vo_ref[...] = vn

    spec = pl.BlockSpec((None, tr, C), lambda l, i: (l + lo, i, 0))
    out = jax.ShapeDtypeStruct(w.shape, F32)
    return pl.pallas_call(body, grid=(hi - lo, R // tr), in_specs=[spec] * 4 + [ANY] * len(extra),
                          out_specs=[spec] * 4, out_shape=[out] * 4,
                          input_output_aliases={4 + k: k for k in range(n_alias)}, name="adamw",
                          compiler_params=_params(2))(w, g, m, v, *extra)


def _pack(parts):
    flat = jnp.concatenate([q.reshape(-1, LANES) for q in parts], axis=0)
    return jnp.pad(flat, ((0, -flat.shape[0] % ROW_TILE), (0, 0)))


def _unpack(flat, like):
    out, r = [], 0
    for q in like:
        n = q.size // LANES
        out.append(flat[r:r + n].reshape(q.shape))
        r += n
    return out


def _train_step(a):
    a = dict(a)
    L = a["ffn1_pre_g"].shape[0]
    x, y, c, _ = _place()
    chip = 2 * x + y
    pos = jnp.stack([chip, c, 2 * chip + c]).astype(jnp.int32)
    for name in TRANSPOSED:
        for pre in ("", "m_", "v_"):
            a[pre + name] = jnp.swapaxes(a[pre + name], 1, 2)
    big = [b[0] for b in BIG]
    gathered = big + ["conv_dw_k"]
    n_w, n_g = len(gathered), len(big)

    own = [None] * n_w
    W = {name: [None] * L for name in gathered}
    every = list(range(n_w))
    first = every[:3]
    rest = every[3:]

    def cast(i, deps):
        if i == n_g:
            taps = a["conv_dw_k"].reshape(L, CONV_TAPS, LANES)
            return _cast_layers("pad_conv_taps", taps, CONV_PAD, LANES, F32, pos, deps)
        name, _, _, _, rp, cp = BIG[i]
        return _cast_layers("cast_weight", a[name], rp, cp, BF, pos, deps)

    def gather_first(l, ids, tag, deps):
        return _split_start("gather_a%d%s" % (l, tag), _gather_plans(len(ids))[0], [own[i][l] for i in ids], deps)

    def gather_second(l, ids, tag, state, after):
        across, to_sibling = _gather_plans(len(ids))
        bufs = _split_wait("gather_a%d%s_done" % (l, tag), across, state[0], state[1], after)
        return _split_start("gather_b%d%s" % (l, tag), to_sibling, bufs, [])

    def gather_done(l, ids, tag, state, after):
        to_sibling = _gather_plans(len(ids))[1]
        bufs = _split_wait("gather_b%d%s_done" % (l, tag), to_sibling, state[0], state[1], after)
        for i, buf in zip(ids, bufs):
            W[gathered[i]][l] = buf

    for i in first:
        own[i] = cast(i, ())
    state = gather_first(0, first, "f", [])
    for i in rest:
        own[i] = cast(i, (state[2],))
    state = gather_second(0, first, "f", state, [own[i][0] for i in rest])
    gather_done(0, first, "f", state, [])
    box = {"rest": gather_first(0, rest, "r", [])}

    small = [_layer_small(a, l) for l in range(L)]
    h, saved = a["x"][0], []
    for l in range(L):
        hooks = {}
        if l == 0:
            deps = (box["rest"][2],)

            def after_ffn1(hv, box=box):
                gather_done(0, rest, "r", gather_second(0, rest, "r", box["rest"], [hv]), [])
                if L > 1:
                    box["next"] = gather_first(1, every, "", [hv])
                    return (box["next"][2],)
                return ()

            hooks["ffn1"] = after_ffn1
            switch = "ffn2"
        else:
            deps = ()
            if l + 1 < L:
                box["next"] = gather_first(l + 1, every, "", [h])
                deps = (box["next"][2],)
            switch = "mix"
        if l + 1 < L:
            def second(hv, l=l, box=box):
                box["next"] = gather_second(l + 1, every, "", box["next"], [hv])
                return (box["next"][2],)

            hooks[switch] = second
        h, sv = _layer_fwd(l, h, a["p"][l, 0], small[l], W, deps, hooks)
        saved.append(sv)
        if l + 1 < L:
            gather_done(l + 1, every, "", box["next"], [h])

    def loss_fn(yv, t):
        e = yv - t
        return e * (1.0 / D), jnp.sum(e * e, axis=0, keepdims=True)

    dh, lsum = _rowwise("loss", loss_fn, [(h, D, 0), (a["loss_target"][0], D, 0)], [], [(D, F32)], [(1, D)])
    loss = lax.psum(0.5 * jnp.sum(lsum) / D, ("x", "y", "c"))

    G = {name: [jax.ShapeDtypeStruct((N_CHIPS, rp, cp), BF)] * L for name, _, _, _, rp, cp in BIG}
    reduced = [lax.empty((L, rp, cp), F32) for _, _, _, _, rp, cp in BIG]
    pair, cross = _pair_plan(n_g), _cross_plan(n_g)
    small_grads = [None] * L

    def pair_start(l, deps):
        grads = [G[name][l] for name in big]
        lands = [lax.empty((N_CHIPS, g.shape[1] // 2, g.shape[2]), BF) for g in grads]
        return _split_start("rs_pair%d" % l, pair, grads + lands, deps)

    def cross_start(l, state, after):
        bufs = _split_wait("rs_pair%d_done" % l, pair, state[0], state[1], after)
        parts = _add_pair(bufs[:n_g], bufs[n_g:], pos)
        lands = [lax.empty((3,) + t.shape[1:], BF) for t in parts]
        return _split_start("rs_cross%d" % l, cross, parts + lands, [])

    def share_start(l, state, after, reduced):
        bufs = _split_wait("rs_cross%d_done" % l, cross, state[0], state[1], after)
        reduced = _add_chips(bufs[:n_g], bufs[n_g:], reduced, l, pos)
        return _split_start("rs_share%d" % l, _share_plan(n_g, l), reduced, [])

    def share_done(l, state, after):
        return _split_wait("rs_share%d_done" % l, _share_plan(n_g, l), state[0], state[1], after)

    st_pair = st_cross = st_share = None
    for l in reversed(range(L)):
        deps = tuple(s[2] for s in (st_pair, st_share) if s is not None)
        box = {"cross": None}

        def mid(dm, l=l, box=box, st_pair=st_pair, st_share=st_share):
            out = []
            if st_share is not None:
                box["reduced"] = share_done(l + 2, st_share, [dm])
            if st_pair is not None:
                box["cross"] = cross_start(l + 1, st_pair, [dm])
                out.append(box["cross"][2])
            return tuple(out)

        dh, small_grads[l] = _layer_bwd(l, dh, saved[l], small[l], W, G, deps, mid)
        if st_share is not None:
            reduced = box["reduced"]
        st_share = share_start(l + 1, box["cross"], [dh], reduced) if box["cross"] is not None else None
        st_pair = pair_start(l, [dh])
    grad_x = dh
    if st_share is not None:
        reduced = share_done(1, st_share, [])
    st_cross = cross_start(0, st_pair, [])
    small_names = SMALL + ("conv_dw_k",)
    stacked = [jnp.stack([small_grads[l][name] for l in range(L)]) for name in small_names]
    summed = dict(zip(small_names, _unpack(_allreduce_small(_pack(stacked), pos), stacked)))
    upper = {}
    if L > 1:
        for name, red in zip(big, reduced):
            upper[name] = _adamw(a[name], red, a["m_" + name], a["v_" + name], 1, L, deps=[st_cross[2]])
    st_share = share_start(0, st_cross, [r[1] for r in upper.values()] + [summed[small_names[0]]], reduced)
    reduced = share_done(0, st_share, [])
    big_grads = dict(zip(big, reduced))

    grads, deltas, new_m, new_v = {}, {}, {}, {}
    for name in big:
        res = _adamw(a[name], big_grads[name], a["m_" + name], a["v_" + name], 0, 1, upper.get(name))
        if name in TRANSPOSED:
            res = [jnp.swapaxes(r, 1, 2) for r in res]
        grads[name], deltas[name], new_m[name], new_v[name] = res
    taps = lax.dynamic_slice_in_dim(summed["conv_dw_k"], chip * LANES, LANES, axis=2)[:, :CONV_TAPS]
    grads["conv_dw_k"] = taps.reshape(a["conv_dw_k"].shape)
    for name in SMALL:
        grads[name] = summed[name].reshape(a[name].shape)
    shapes = [a[name] for name in small_names]
    res = _adamw(*[_pack([a[pre + name] if pre != "g" else grads[name] for name in small_names])[None]
                   for pre in ("", "g", "m_", "v_")])
    for dst, flat in zip((deltas, new_m, new_v), res[1:]):
        for name, val in zip(small_names, _unpack(flat[0], shapes)):
            dst[name] = val

    return (loss, grad_x[None], *[grads[n] for n in WEIGHTS], *[deltas[n] for n in WEIGHTS],
            *[new_m[n] for n in WEIGHTS], *[new_v[n] for n in WEIGHTS])


def kernel(x, p, ffn1_pre_g, ffn1_w_gate, ffn1_w_up, ffn1_w_down, ffn1_post_g, mix_pre_g, w_in, pool_w, pool_scale, w_pool_out, sgu_ln_g, sgu_ln_b, sgu_w_s, sgu_b_s, w_sgu_out, conv_dw_k, conv_dw_b, conv_ln_g, conv_ln_b, w_conv_out, w_out, mix_post_g, ffn2_pre_g, ffn2_w_gate, ffn2_w_up, ffn2_w_down, ffn2_post_g, ple_w_proj, ple_pre_g, ple_w_gate, ple_post_g, loss_target, m_ffn1_pre_g, m_ffn1_w_gate, m_ffn1_w_up, m_ffn1_w_down, m_ffn1_post_g, m_mix_pre_g, m_w_in, m_pool_w, m_pool_scale, m_w_pool_out, m_sgu_ln_g, m_sgu_ln_b, m_sgu_w_s, m_sgu_b_s, m_w_sgu_out, m_conv_dw_k, m_conv_dw_b, m_conv_ln_g, m_conv_ln_b, m_w_conv_out, m_w_out, m_mix_post_g, m_ffn2_pre_g, m_ffn2_w_gate, m_ffn2_w_up, m_ffn2_w_down, m_ffn2_post_g, m_ple_w_proj, m_ple_pre_g, m_ple_w_gate, m_ple_post_g, v_ffn1_pre_g, v_ffn1_w_gate, v_ffn1_w_up, v_ffn1_w_down, v_ffn1_post_g, v_mix_pre_g, v_w_in, v_pool_w, v_pool_scale, v_w_pool_out, v_sgu_ln_g, v_sgu_ln_b, v_sgu_w_s, v_sgu_b_s, v_w_sgu_out, v_conv_dw_k, v_conv_dw_b, v_conv_ln_g, v_conv_ln_b, v_w_conv_out, v_w_out, v_mix_post_g, v_ffn2_pre_g, v_ffn2_w_gate, v_ffn2_w_up, v_ffn2_w_down, v_ffn2_post_g, v_ple_w_proj, v_ple_pre_g, v_ple_w_gate, v_ple_post_g):
    return _train_step(dict(locals()))
```

```python
import math

import jax
import jax.numpy as jnp
from jax import lax
from jax.experimental import pallas as pl
from jax.experimental.pallas import tpu as pltpu

BF = jnp.bfloat16
F32 = jnp.float32
EPS = 1e-6
D_MODEL = 1024
LANES = 128
SUBLANES = 8
MXU_TILE = 256
N_CHIPS = 4
FFN_SHARD = 704
FFN_SHARD_PAD = 768
POOL_WINDOWS = (2, 4, 8, 16)
SGU_HEADS = 4
CHUNK = 128
CONV_TAPS = 31
CONV_PAD = 32
ROW_TILE = 512
EPI_ROWS = 256
VMEM_LIMIT_BYTES = 56 * 1024 * 1024
ADAM_TILE_ELEMS = 3 * 128 * 1024
ADAM_LR, ADAM_B1, ADAM_B2, ADAM_EPS, ADAM_WD, ADAM_STEP =0.001, 0.9, 0.999, 1e-08, 0.01, 10
MESH = pl.DeviceIdType.MESH
ANY = pl.BlockSpec(memory_space=pl.ANY)

ZB_POOL, ZB_U, ZB_V, ZB_A, ZB_B, ZB_GATES = 0, 1, 2, 3, 4, 5
DZ_HALF = 2816

TRANSPOSED = ("ffn1_w_gate", "ffn1_w_up", "ffn2_w_gate", "ffn2_w_up")
BIG = (
    ("ffn1_w_gate", "row", FFN_SHARD, 1024, FFN_SHARD_PAD, 1024),
    ("ffn1_w_up", "row", FFN_SHARD, 1024, FFN_SHARD_PAD, 1024),
    ("ffn1_w_down", "row", FFN_SHARD, 1024, FFN_SHARD_PAD, 1024),
    ("w_in", "col", 1024, 1408, 1024, 1408),
    ("w_pool_out", "col", 512, 256, 512, 256),
    ("w_sgu_out", "col", 512, 256, 512, 256),
    ("w_conv_out", "col", 512, 256, 512, 256),
    ("w_out", "row", 256, 1024, 256, 1024),
    ("ffn2_w_gate", "row", FFN_SHARD, 1024, FFN_SHARD_PAD, 1024),
    ("ffn2_w_up", "row", FFN_SHARD, 1024, FFN_SHARD_PAD, 1024),
    ("ffn2_w_down", "row", FFN_SHARD, 1024, FFN_SHARD_PAD, 1024),
    ("ple_w_proj", "col", 256, 256, 256, 256),
    ("ple_w_gate", "row", 256, 1024, 256, 1024),
)
SMALL = ("ffn1_pre_g", "ffn1_post_g", "mix_pre_g", "pool_w", "pool_scale", "sgu_ln_g", "sgu_ln_b", "sgu_w_s",
         "sgu_b_s", "conv_dw_b", "conv_ln_g", "conv_ln_b", "mix_post_g", "ffn2_pre_g", "ffn2_post_g",
         "ple_pre_g", "ple_post_g")
WEIGHTS = ("ffn1_pre_g", "ffn1_w_gate", "ffn1_w_up", "ffn1_w_down", "ffn1_post_g", "mix_pre_g", "w_in", "pool_w",
           "pool_scale", "w_pool_out", "sgu_ln_g", "sgu_ln_b", "sgu_w_s", "sgu_b_s", "w_sgu_out", "conv_dw_k",
           "conv_dw_b", "conv_ln_g", "conv_ln_b", "w_conv_out", "w_out", "mix_post_g", "ffn2_pre_g", "ffn2_w_gate",
           "ffn2_w_up", "ffn2_w_down", "ffn2_post_g", "ple_w_proj", "ple_pre_g", "ple_w_gate", "ple_post_g")


def _params(n_grid):
    return pltpu.CompilerParams(dimension_semantics=("arbitrary",) * n_grid, vmem_limit_bytes=VMEM_LIMIT_BYTES)


def _tile(n, cap=ROW_TILE):
    for t in range(min(cap, n) - min(cap, n) % 16, 0, -16):
        if n % t == 0:
            return t
    return n


def _sigmoid(x):
    return 0.5 * jnp.tanh(0.5 * x) + 0.5


def _silu_and_grad(x):
    s = _sigmoid(x)
    return x * s, s * (1.0 + x * (1.0 - s))


def _gelu_and_grad(x):
    cdf = 0.5 * (1.0 + lax.erf(x * (1.0 / math.sqrt(2.0))))
    pdf = jnp.exp(-0.5 * x * x) * (1.0 / math.sqrt(2.0 * math.pi))
    return x * cdf, cdf + x * pdf


def _rms_fwd(x, g):
    return x * lax.rsqrt(jnp.mean(x * x, axis=-1, keepdims=True) + EPS) * g


def _rms_bwd(x, g, dy):
    r = lax.rsqrt(jnp.mean(x * x, axis=-1, keepdims=True) + EPS)
    xh = x * r
    dxh = dy * g
    dx = r * (dxh - xh * jnp.mean(dxh * xh, axis=-1, keepdims=True))
    return dx, jnp.sum(dy * xh, axis=0, keepdims=True)


def _ln_stats(x):
    xc = x - jnp.mean(x, axis=-1, keepdims=True)
    r = lax.rsqrt(jnp.mean(xc * xc, axis=-1, keepdims=True) + EPS)
    return xc * r, r


def _ln_bwd(xh, r, g, dy):
    dxh = dy * g
    dx = r * (dxh - jnp.mean(dxh, axis=-1, keepdims=True) - xh * jnp.mean(dxh * xh, axis=-1, keepdims=True))
    return dx, jnp.sum(dy * xh, axis=0, keepdims=True), jnp.sum(dy, axis=0, keepdims=True)


def _rowwise(name, fn, rows, consts, outs, accs=(), tm=ROW_TILE, deps=()):
    T = rows[0][0].shape[-2]
    tm = _tile(T, tm)
    n_in, n_o, n_dep = len(rows) + len(consts), len(outs), len(deps)

    def body(*refs):
        refs = refs[n_dep:]
        res = fn(*[r[...] for r in refs[:n_in]])
        for ref, val in zip(refs[n_in:n_in + n_o], res[:n_o]):
            ref[...] = val.astype(ref.dtype)
        acc_refs = refs[n_in + n_o:]
        if acc_refs:
            @pl.when(pl.program_id(0) == 0)
            def _():
                for ref, val in zip(acc_refs, res[n_o:]):
                    ref[...] = val

            @pl.when(pl.program_id(0) != 0)
            def _():
                for ref, val in zip(acc_refs, res[n_o:]):
                    ref[...] += val

    in_specs = [ANY] * n_dep
    for row in rows:
        w, cb = row[1], row[2]
        if len(row) == 4:
            in_specs.append(pl.BlockSpec((None, tm, w), lambda i, cb=cb, ld=row[3]: (ld, i, cb)))
        else:
            in_specs.append(pl.BlockSpec((tm, w), lambda i, cb=cb: (i, cb)))
    in_specs += [pl.BlockSpec(c.shape, lambda i: (0, 0)) for c in consts]
    out_specs = [pl.BlockSpec((tm, w), lambda i: (i, 0)) for w, _ in outs]
    out_specs += [pl.BlockSpec(s, lambda i: (0, 0)) for s in accs]
    out_shape = [jax.ShapeDtypeStruct((T, w), dt) for w, dt in outs]
    out_shape += [jax.ShapeDtypeStruct(s, F32) for s in accs]
    return pl.pallas_call(body, grid=(T // tm,), in_specs=in_specs, out_specs=out_specs, out_shape=out_shape,
                          name=name, compiler_params=_params(1))(*deps, *[r[0] for r in rows], *consts)


def _tiled(name, fn, grid, pos, ins, outs):
    n_in = len(ins)

    def body(_, *refs):
        res = fn(*[r[...] for r in refs[:n_in]])
        for ref, val in zip(refs[n_in:], res):
            ref[...] = val.astype(ref.dtype)

    spec = pltpu.PrefetchScalarGridSpec(
        num_scalar_prefetch=1, grid=grid, in_specs=[pl.BlockSpec(bs, im) for _, bs, im in ins],
        out_specs=[pl.BlockSpec(bs, im) for _, _, bs, im in outs])
    return pl.pallas_call(body, grid_spec=spec, out_shape=[jax.ShapeDtypeStruct(s, d) for s, d, _, _ in outs],
                          name=name, compiler_params=_params(len(grid)))(pos, *[a for a, _, _ in ins])


def _cast_layers(name, w, rp, cp, dtype, pos, deps=()):
    L, r, c = w.shape

    def body(_, w_ref, *rest):
        for k, o_ref in enumerate(rest[len(deps):]):
            @pl.when(pl.program_id(0) == k)
            def _(o_ref=o_ref):
                if (rp, cp) != (r, c):
                    o_ref[...] = jnp.zeros_like(o_ref)
                    o_ref[pl.ds(0, r), pl.ds(0, c)] = w_ref[...].astype(dtype)
                else:
                    o_ref[...] = w_ref[...].astype(dtype)

    spec = pltpu.PrefetchScalarGridSpec(
        num_scalar_prefetch=1, grid=(L,),
        in_specs=[pl.BlockSpec((None, r, c), lambda l, p: (l, 0, 0))] + [ANY] * len(deps),
        out_specs=[pl.BlockSpec((None, rp, cp), lambda l, p: (p[0], 0, 0))] * L)
    return pl.pallas_call(body, grid_spec=spec, out_shape=[jax.ShapeDtypeStruct((N_CHIPS, rp, cp), dtype)] * L,
                          name=name, compiler_params=_params(1))(pos, w, *deps)


_NN = (((1,), (0,)), ((), ()))
_NT = (((1,), (1,)), ((), ()))
_TN = (((0,), (0,)), ((), ()))


def _mm_tn(name, a, dy, buf, l, a_blocked, tk=ROW_TILE, first=0):
    T = a.shape[0]
    nb, R, C = buf[l].shape
    extra = [buf[l]] if first else []

    def body(a_ref, dy_ref, *rest):
        rest[-1][...] = lax.dot_general(a_ref[...].astype(BF), dy_ref[...].astype(BF), _TN,
                                        preferred_element_type=F32).astype(BF)

    if a_blocked:
        grid = (nb,)
        in_specs = [pl.BlockSpec((T, R), lambda b: (0, b)), pl.BlockSpec((T, C), lambda b: (0, 0))]
        out_specs = pl.BlockSpec((None, R, C), lambda b: (b, 0, 0))
    else:
        tk = min(tk, R)
        grid = (dy.shape[1] // C, R // tk)
        in_specs = [pl.BlockSpec((T, tk), lambda b, k: (0, k)), pl.BlockSpec((T, C), lambda b, k: (0, b))]
        out_specs = pl.BlockSpec((None, tk, C), lambda b, k: (b + first, k, 0))
    buf = list(buf)
    buf[l] = pl.pallas_call(body, grid=grid, in_specs=in_specs + [ANY] * len(extra), out_specs=out_specs,
                            out_shape=jax.ShapeDtypeStruct((nb, R, C), BF),
                            input_output_aliases={2: 0} if extra else {}, name=name,
                            compiler_params=_params(len(grid)))(a, dy, *extra)
    return buf


def _acc_rows(ref, val, first):
    @pl.when(first)
    def _():
        ref[...] = val

    @pl.when(jnp.logical_not(first))
    def _():
        ref[...] += val


def _norm_mm(name, h, g, ws, trans_w, act=False, deps=(), tm=2 * ROW_TILE):
    T = h.shape[0]
    nb, r, cc = ws[0].shape
    bo = r if trans_w else cc
    tm = _tile(T, tm)
    n_w, n_dep = len(ws), len(deps)

    def body(*refs):
        refs = refs[n_dep:]
        h_ref, g_ref, w_refs = refs[0], refs[1], refs[2:2 + n_w]
        n_ref, o_refs, n_s = refs[2 + n_w], refs[3 + n_w:3 + 2 * n_w], refs[-1]

        @pl.when(pl.program_id(1) == 0)
        def _():
            n = _rms_fwd(h_ref[...].astype(F32), g_ref[...]).astype(BF)
            n_s[...] = n
            n_ref[...] = n

        n = n_s[...]
        prods = []
        for w_ref, o_ref in zip(w_refs, o_refs):
            prods.append(lax.dot_general(n, w_ref[...], _NT if trans_w else _NN,
                                         preferred_element_type=F32).astype(BF))
            o_ref[...] = prods[-1]
        if act:
            refs[3 + 2 * n_w][...] = (_silu_and_grad(prods[0].astype(F32))[0] * prods[1].astype(F32)).astype(BF)

    wide = pl.BlockSpec((tm, bo), lambda i, b: (i, b))
    n_out = n_w + (1 if act else 0)
    return pl.pallas_call(
        body, grid=(T // tm, nb),
        in_specs=[ANY] * n_dep + [pl.BlockSpec((tm, D_MODEL), lambda i, b: (i, 0)),
                                  pl.BlockSpec(g.shape, lambda i, b: (0, 0))]
        + [pl.BlockSpec((None, r, cc), lambda i, b: (b, 0, 0))] * n_w,
        out_specs=[pl.BlockSpec((tm, D_MODEL), lambda i, b: (i, 0))] + [wide] * n_out,
        out_shape=[jax.ShapeDtypeStruct((T, D_MODEL), BF)] + [jax.ShapeDtypeStruct((T, nb * bo), BF)] * n_out,
        scratch_shapes=[pltpu.VMEM((tm, D_MODEL), BF)], name=name, compiler_params=_params(2))(*deps, h, g, *ws)


def _mm_res(name, x, w3, h, g, coef, tm=ROW_TILE):
    T, kx = x.shape
    w2 = w3.reshape(kx, D_MODEL)
    tm = _tile(T, tm)

    def body(x_ref, w_ref, h_ref, g_ref, f_ref, o_ref):
        f = jnp.dot(x_ref[...], w_ref[...], preferred_element_type=F32).astype(BF)
        f_ref[...] = f
        o_ref[...] = h_ref[...] + coef * _rms_fwd(f.astype(F32), g_ref[...])

    row = pl.BlockSpec((tm, D_MODEL), lambda i: (i, 0))
    return pl.pallas_call(
        body, grid=(T // tm,),
        in_specs=[pl.BlockSpec((tm, kx), lambda i: (i, 0)), pl.BlockSpec(w2.shape, lambda i: (0, 0)), row,
                  pl.BlockSpec(g.shape, lambda i: (0, 0))],
        out_specs=[row, row],
        out_shape=[jax.ShapeDtypeStruct((T, D_MODEL), BF), jax.ShapeDtypeStruct((T, D_MODEL), F32)],
        name=name, compiler_params=_params(1))(x, w2, h, g)


def _resbwd_mm(name, dh, f, g, coef, w3, trans_w, act=None, deps=(), tm=2 * ROW_TILE):
    T = dh.shape[0]
    nb, r, cc = w3.shape
    bo = r if trans_w else cc
    tm = _tile(T, tm)
    n_dep, n_act = len(deps), 3 if act else 0
    n_i = T // tm

    def body(*refs):
        refs = refs[n_dep:]
        dh_ref, f_ref, g_ref, w_ref = refs[:4]
        df_ref, dg_ref = refs[4 + n_act], refs[5 + n_act]
        df_s = refs[-2] if act else refs[-1]
        i, b = pl.program_id(0), pl.program_id(1)

        @pl.when(b == 0)
        def _():
            dg = jnp.zeros((1, D_MODEL), F32)
            for c in range(tm // EPI_ROWS):
                rows = slice(c * EPI_ROWS, (c + 1) * EPI_ROWS)
                dx, dg_c = _rms_bwd(f_ref[rows, :].astype(F32), g_ref[...], coef * dh_ref[rows, :])
                df_s[rows, :] = dx.astype(BF)
                df_ref[rows, :] = dx.astype(BF)
                dg = dg + dg_c
            _acc_rows(dg_ref, dg, i == 0)

        if act:
            for j in range(bo // MXU_TILE):
                cols = slice(j * MXU_TILE, (j + 1) * MXU_TILE)
                prod = lax.dot_general(df_s[...], w_ref[cols, :], _NT, preferred_element_type=F32)
                val, grad = _silu_and_grad(refs[4][:, cols].astype(F32))
                refs[6 + n_act][:, cols] = (prod * refs[5][:, cols].astype(F32) * grad).astype(BF)
                refs[7 + n_act][:, cols] = (prod * val).astype(BF)
            acc = refs[-1]
            part = lax.dot_general(refs[6][...], df_s[...], _TN, preferred_element_type=F32)

            @pl.when(i == 0)
            def _():
                acc[b] = part

            @pl.when(i != 0)
            def _():
                acc[b] += part

            @pl.when(i == n_i - 1)
            def _():
                refs[8 + n_act][...] = acc[b].astype(BF)
        else:
            refs[6][...] = lax.dot_general(df_s[...], w_ref[...], _NT if trans_w else _NN,
                                           preferred_element_type=F32).astype(BF)

    row = pl.BlockSpec((tm, D_MODEL), lambda i, b: (i, 0))
    wide = pl.BlockSpec((tm, bo), lambda i, b: (i, b))
    vec = pl.BlockSpec((1, D_MODEL), lambda i, b: (0, 0))
    out_specs = [row, vec] + [wide] * (2 if act else 1)
    out_shape = [jax.ShapeDtypeStruct((T, D_MODEL), BF), jax.ShapeDtypeStruct((1, D_MODEL), F32)]
    out_shape += [jax.ShapeDtypeStruct((T, nb * bo), BF)] * (2 if act else 1)
    scratch = [pltpu.VMEM((tm, D_MODEL), BF)]
    if act:
        out_specs.append(pl.BlockSpec((None, r, cc), lambda i, b: (jnp.where(i == n_i - 1, b, 0), 0, 0)))
        out_shape.append(jax.ShapeDtypeStruct((nb, r, cc), BF))
        scratch.append(pltpu.VMEM((nb, r, cc), F32))
    return pl.pallas_call(
        body, grid=(n_i, nb),
        in_specs=[ANY] * n_dep + [row, row, vec, pl.BlockSpec((None, r, cc), lambda i, b: (b, 0, 0))] + [wide] * n_act,
        out_specs=out_specs, out_shape=out_shape, scratch_shapes=scratch, name=name,
        compiler_params=_params(2))(*deps, dh, f, g, w3, *(act or ()))


def _dn_prenorm(name, xs, ws, trans_w, dh, h, g, tm=2 * ROW_TILE):
    T = dh.shape[0]
    chained = not isinstance(ws, (list, tuple))
    ws = [ws] if chained else list(ws)
    _, r, cc = ws[0].shape
    bw = cc if trans_w else r
    per_x = xs[0].shape[1] // bw
    nb = per_x * len(xs) if chained else per_x
    tm = _tile(T, tm)
    n_x, n_w = len(xs), len(ws)

    def body(*refs):
        x_refs, w_refs = refs[:n_x], refs[n_x:n_x + n_w]
        dh_ref, h_ref, g_ref, o_ref, dg_ref, acc = refs[n_x + n_w:]
        i, b = pl.program_id(0), pl.program_id(1)

        @pl.when(b == 0)
        def _():
            acc[...] = jnp.zeros_like(acc)

        def add(x_ref, w_ref):
            acc[...] += lax.dot_general(x_ref[...], w_ref[...], _NT if trans_w else _NN, preferred_element_type=F32)

        if chained:
            for k, x_ref in enumerate(x_refs):
                pl.when(b // per_x == k)(lambda x_ref=x_ref: add(x_ref, w_refs[0]))
        else:
            for x_ref, w_ref in zip(x_refs, w_refs):
                add(x_ref, w_ref)

        @pl.when(b == nb - 1)
        def _():
            dg = jnp.zeros((1, D_MODEL), F32)
            for c in range(tm // EPI_ROWS):
                rows = slice(c * EPI_ROWS, (c + 1) * EPI_ROWS)
                dx, dg_c = _rms_bwd(h_ref[rows, :], g_ref[...], acc[rows, :])
                o_ref[rows, :] = dh_ref[rows, :] + dx
                dg = dg + dg_c
            _acc_rows(dg_ref, dg, i == 0)

    row = pl.BlockSpec((tm, D_MODEL), lambda i, b: (i, 0))
    vec = pl.BlockSpec((1, D_MODEL), lambda i, b: (0, 0))
    if chained:
        x_specs = [pl.BlockSpec((tm, bw), lambda i, b, k=k: (i, jnp.clip(b - k * per_x, 0, per_x - 1)))
                   for k in range(n_x)]
    else:
        x_specs = [pl.BlockSpec((tm, bw), lambda i, b: (i, b))] * n_x
    return pl.pallas_call(
        body, grid=(T // tm, nb),
        in_specs=x_specs + [pl.BlockSpec((None, r, cc), lambda i, b: (b, 0, 0))] * n_w + [row, row, vec],
        out_specs=[row, vec],
        out_shape=[jax.ShapeDtypeStruct((T, D_MODEL), F32), jax.ShapeDtypeStruct((1, D_MODEL), F32)],
        scratch_shapes=[pltpu.VMEM((tm, D_MODEL), F32)], name=name,
        compiler_params=_params(2))(*xs, *ws, dh, h, g)


def _pool_apply(x, win, row):
    s, k = x, 1
    while k < win:
        s = s + jnp.where(row >= k, pltpu.roll(s, k, 0), 0.0)
        k *= 2
    return s / jnp.minimum(row + 1, win).astype(F32) - x


def _pool_apply_t(dp, win, row):
    T = dp.shape[0]
    s, k = dp / jnp.minimum(row + 1, win).astype(F32), 1
    while k < win:
        s = s + jnp.where(row < T - k, pltpu.roll(s, T - k, 0), 0.0)
        k *= 2
    return s - dp


def _pool_fwd(z, w, scale):
    T = z.shape[0]

    def body(z_ref, w_ref, s_ref, o_ref):
        row = lax.broadcasted_iota(jnp.int32, (T, LANES), 0)
        for gi, win in enumerate(POOL_WINDOWS):
            cols = pl.ds(gi * LANES, LANES)
            pooled = _pool_apply(z_ref[:, cols].astype(F32), win, row)
            y = jnp.dot(pooled.astype(BF), w_ref[gi].astype(BF), preferred_element_type=F32)
            o_ref[:, cols] = (y * s_ref[:, cols]).astype(o_ref.dtype)

    return pl.pallas_call(
        body, grid=(1,),
        in_specs=[pl.BlockSpec((T, 512), lambda i: (0, ZB_POOL)), pl.BlockSpec(w.shape, lambda i: (0, 0, 0)),
                  pl.BlockSpec(scale.shape, lambda i: (0, 0))],
        out_specs=pl.BlockSpec((T, 512), lambda i: (0, 0)), out_shape=jax.ShapeDtypeStruct((T, 512), BF),
        name="pool_fwd", compiler_params=_params(1))(z, w, scale)


def _pool_bwd(dr, z, w, scale):
    T = z.shape[0]

    def body(dr_ref, z_ref, w_ref, s_ref, dz_ref, dw_ref, ds_ref):
        row = lax.broadcasted_iota(jnp.int32, (T, LANES), 0)
        for gi, win in enumerate(POOL_WINDOWS):
            cols = pl.ds(gi * LANES, LANES)
            pooled = _pool_apply(z_ref[:, cols].astype(F32), win, row).astype(BF)
            wg = w_ref[gi].astype(BF)
            y = jnp.dot(pooled, wg, preferred_element_type=F32)
            d = dr_ref[:, cols].astype(F32)
            ds_ref[:, cols] = jnp.sum(d * y, axis=0, keepdims=True)
            dy = (d * s_ref[:, cols]).astype(BF)
            dw_ref[gi] = lax.dot_general(pooled, dy, _TN, preferred_element_type=F32)
            dpooled = lax.dot_general(dy, wg, _NT, preferred_element_type=F32)
            dz_ref[:, cols] = _pool_apply_t(dpooled, win, row).astype(dz_ref.dtype)

    return pl.pallas_call(
        body, grid=(1,),
        in_specs=[pl.BlockSpec((T, 512), lambda i: (0, 0)), pl.BlockSpec((T, 512), lambda i: (0, ZB_POOL)),
                  pl.BlockSpec(w.shape, lambda i: (0, 0, 0)), pl.BlockSpec(scale.shape, lambda i: (0, 0))],
        out_specs=[pl.BlockSpec((T, 512), lambda i: (0, 0)), pl.BlockSpec(w.shape, lambda i: (0, 0, 0)),
                   pl.BlockSpec(scale.shape, lambda i: (0, 0))],
        out_shape=[jax.ShapeDtypeStruct((T, 512), BF), jax.ShapeDtypeStruct(w.shape, F32),
                   jax.ShapeDtypeStruct(scale.shape, F32)],
        name="pool_bwd", compiler_params=_params(1))(dr, z, w, scale)


def _tril(transposed=False):
    r = lax.broadcasted_iota(jnp.int32, (CHUNK, CHUNK), 0)
    c = lax.broadcasted_iota(jnp.int32, (CHUNK, CHUNK), 1)
    return c >= r if transposed else r >= c


def _sgu_fwd(z, ln_g, ln_b, w_s, bias):
    T = z.shape[0]
    tm = _tile(T)

    def body(zu_ref, zv_ref, g_ref, b_ref, w_ref, bias_ref, o_ref):
        gu, _ = _gelu_and_grad(zu_ref[...].astype(F32))
        gv, _ = _gelu_and_grad(zv_ref[...].astype(F32))
        xh, _ = _ln_stats(gv)
        v16 = (xh * g_ref[...] + b_ref[...]).astype(BF)
        tri = _tril()
        for h in range(SGU_HEADS):
            cols = slice(h * LANES, (h + 1) * LANES)
            wh = jnp.where(tri, w_ref[h], 0.0).astype(BF)
            for c in range(tm // CHUNK):
                rows = slice(c * CHUNK, (c + 1) * CHUNK)
                s = jnp.dot(wh, v16[rows, cols], preferred_element_type=F32) + bias_ref[:, cols]
                o_ref[rows, cols] = (gu[rows, cols] * s).astype(o_ref.dtype)

    small = [pl.BlockSpec(a.shape, lambda i, n=a.ndim: (0,) * n) for a in (ln_g, ln_b, w_s, bias)]
    return pl.pallas_call(
        body, grid=(T // tm,),
        in_specs=[pl.BlockSpec((tm, 512), lambda i: (i, ZB_U)), pl.BlockSpec((tm, 512), lambda i: (i, ZB_V))] + small,
        out_specs=pl.BlockSpec((tm, 512), lambda i: (i, 0)), out_shape=jax.ShapeDtypeStruct((T, 512), BF),
        name="sgu_fwd", compiler_params=_params(1))(z, z, ln_g, ln_b, w_s, bias)


def _sgu_bwd(dr, z, ln_g, ln_b, w_s, w_st, bias):
    T = z.shape[0]
    tm = _tile(T)
    n_steps = T // tm

    def body(dr_ref, zu_ref, zv_ref, g_ref, b_ref, w_ref, wt_ref, bias_ref,
             dzu_ref, dzv_ref, dg_ref, db_ref, dw_ref, dbias_ref, dgu_s, dv_s):
        i = pl.program_id(0)

        @pl.when(i == 0)
        def _():
            dg_ref[...] = jnp.zeros_like(dg_ref)
            db_ref[...] = jnp.zeros_like(db_ref)
            dw_ref[...] = jnp.zeros_like(dw_ref)
            dbias_ref[...] = jnp.zeros_like(dbias_ref)

        zu = zu_ref[...].astype(F32)
        zv = zv_ref[...].astype(F32)
        gu, gu_grad = _gelu_and_grad(zu)
        gv, gv_grad = _gelu_and_grad(zv)
        xh, r = _ln_stats(gv)
        v16 = (xh * g_ref[...] + b_ref[...]).astype(BF)
        dr = dr_ref[...].astype(F32)
        tri = _tril()
        for h in range(SGU_HEADS):
            cols = slice(h * LANES, (h + 1) * LANES)
            wh = jnp.where(tri, w_ref[h], 0.0).astype(BF)
            wht = jnp.where(_tril(transposed=True), wt_ref[h], 0.0).astype(BF)
            for c in range(tm // CHUNK):
                rows = slice(c * CHUNK, (c + 1) * CHUNK)
                v_blk = v16[rows, cols]
                s = jnp.dot(wh, v_blk, preferred_element_type=F32) + bias_ref[:, cols]
                ds = dr[rows, cols] * gu[rows, cols]
                dgu_s[rows, cols] = dr[rows, cols] * s
                ds16 = ds.astype(BF)
                dw_ref[h] += jnp.where(tri, lax.dot_general(ds16, v_blk, _NT, preferred_element_type=F32), 0.0)
                dv_s[rows, cols] = jnp.dot(wht, ds16, preferred_element_type=F32)
                dbias_ref[:, cols] += ds
        dzu_ref[...] = (dgu_s[...] * gu_grad).astype(dzu_ref.dtype)
        dgv, dg, db = _ln_bwd(xh, r, g_ref[...], dv_s[...])
        dzv_ref[...] = (dgv * gv_grad).astype(dzv_ref.dtype)
        dg_ref[...] += dg
        db_ref[...] += db

        @pl.when(i == n_steps - 1)
        def _():
            for h in range(SGU_HEADS):
                cols = slice(h * LANES, (h + 1) * LANES)
                tot = jnp.sum(dbias_ref[:, cols], axis=1, keepdims=True)
                dbias_ref[:, cols] = jnp.broadcast_to(tot, (CHUNK, LANES))

    small = (ln_g, ln_b, w_s, w_st, bias)
    small_specs = [pl.BlockSpec(a.shape, lambda i, n=a.ndim: (0,) * n) for a in small]
    return pl.pallas_call(
        body, grid=(n_steps,),
        in_specs=[pl.BlockSpec((tm, 512), lambda i: (i, 0)), pl.BlockSpec((tm, 512), lambda i: (i, ZB_U)),
                  pl.BlockSpec((tm, 512), lambda i: (i, ZB_V))] + small_specs,
        out_specs=[pl.BlockSpec((tm, 512), lambda i: (i, 0)), pl.BlockSpec((tm, 512), lambda i: (i, 0)),
                   pl.BlockSpec((1, 512), lambda i: (0, 0)), pl.BlockSpec((1, 512), lambda i: (0, 0)),
                   pl.BlockSpec(w_s.shape, lambda i: (0, 0, 0)), pl.BlockSpec(bias.shape, lambda i: (0, 0))],
        out_shape=[jax.ShapeDtypeStruct((T, 512), BF), jax.ShapeDtypeStruct((T, 512), BF),
                   jax.ShapeDtypeStruct((1, 512), F32), jax.ShapeDtypeStruct((1, 512), F32),
                   jax.ShapeDtypeStruct(w_s.shape, F32), jax.ShapeDtypeStruct(bias.shape, F32)],
        scratch_shapes=[pltpu.VMEM((tm, 512), F32), pltpu.VMEM((tm, 512), F32)],
        name="sgu_bwd", compiler_params=_params(1))(dr, z, z, ln_g, ln_b, w_s, w_st, bias)


def _conv_fwd(z, convk, l, bias):
    T = z.shape[0]

    def body(za_ref, zb_ref, k_ref, b_ref, o_ref):
        xg = za_ref[...].astype(F32) * _sigmoid(zb_ref[...].astype(F32))
        xp = jnp.concatenate([jnp.zeros((CONV_PAD, LANES), F32), xg], axis=0)
        kw = k_ref[...]
        acc = jnp.broadcast_to(b_ref[...], (T, LANES))
        for s in range(SUBLANES):
            xs = xp if s == 0 else pltpu.roll(xp, s, 0)
            for q in range(CONV_PAD // SUBLANES):
                k = CONV_TAPS - 1 - (SUBLANES * q + s)
                if k >= 0:
                    lo = CONV_PAD - SUBLANES * q
                    acc = acc + kw[k:k + 1, :] * xs[lo:lo + T, :]
        o_ref[...] = acc.astype(o_ref.dtype)

    return pl.pallas_call(
        body, grid=(4,),
        in_specs=[pl.BlockSpec((T, LANES), lambda g: (0, 4 * ZB_A + g)),
                  pl.BlockSpec((T, LANES), lambda g: (0, 4 * ZB_B + g)),
                  pl.BlockSpec((None, CONV_PAD, LANES), lambda g: (g, 0, 0)),
                  pl.BlockSpec((1, LANES), lambda g: (0, g))],
        out_specs=pl.BlockSpec((T, LANES), lambda g: (0, g)), out_shape=jax.ShapeDtypeStruct((T, 512), BF),
        name="conv_fwd", compiler_params=_params(1))(z, z, convk[l], bias)


def _conv_bwd(dy, z, convk, l):
    T = z.shape[0]

    def body(dy_ref, za_ref, zb_ref, k_ref, dza_ref, dzb_ref, dk_ref, db_ref):
        a = za_ref[...].astype(F32)
        sg = _sigmoid(zb_ref[...].astype(F32))
        d = dy_ref[...].astype(F32)
        kw = k_ref[...]
        xp = jnp.concatenate([jnp.zeros((CONV_PAD, LANES), F32), a * sg], axis=0)
        dp = jnp.concatenate([d, jnp.zeros((CONV_PAD, LANES), F32)], axis=0)
        dxg = jnp.zeros((T, LANES), F32)
        dk_ref[...] = jnp.zeros_like(dk_ref)
        for s in range(SUBLANES):
            xs = xp if s == 0 else pltpu.roll(xp, s, 0)
            ds = dp if s == 0 else pltpu.roll(dp, T + CONV_PAD - s, 0)
            for q in range(CONV_PAD // SUBLANES):
                k = CONV_TAPS - 1 - (SUBLANES * q + s)
                if k >= 0:
                    lo = CONV_PAD - SUBLANES * q
                    dk_ref[k:k + 1, :] = jnp.sum(d * xs[lo:lo + T, :], axis=0, keepdims=True)
                    dxg = dxg + kw[k:k + 1, :] * ds[SUBLANES * q:SUBLANES * q + T, :]
        db_ref[...] = jnp.sum(d, axis=0, keepdims=True)
        dza_ref[...] = (dxg * sg).astype(dza_ref.dtype)
        dzb_ref[...] = (dxg * a * sg * (1.0 - sg)).astype(dzb_ref.dtype)

    col = pl.BlockSpec((T, LANES), lambda g: (0, g))
    return pl.pallas_call(
        body, grid=(4,),
        in_specs=[col, pl.BlockSpec((T, LANES), lambda g: (0, 4 * ZB_A + g)),
                  pl.BlockSpec((T, LANES), lambda g: (0, 4 * ZB_B + g)),
                  pl.BlockSpec((None, CONV_PAD, LANES), lambda g: (g, 0, 0))],
        out_specs=[col, col, pl.BlockSpec((CONV_PAD, LANES), lambda g: (0, g)),
                   pl.BlockSpec((1, LANES), lambda g: (0, g))],
        out_shape=[jax.ShapeDtypeStruct((T, 512), BF), jax.ShapeDtypeStruct((T, 512), BF),
                   jax.ShapeDtypeStruct((CONV_PAD, 512), F32), jax.ShapeDtypeStruct((1, 512), F32)],
        name="conv_bwd", compiler_params=_params(1))(dy, z, z, convk[l])


D = D_MODEL


def _ffn_fwd(l, h, S, W, pre, deps=()):
    n, gp, u, a = _norm_mm("ffn_in", h, S[pre + "_pre_g"], [W[pre + "_w_gate"][l], W[pre + "_w_up"][l]], True,
                           act=True, deps=deps)
    f, out = _mm_res("ffn_out", a, W[pre + "_w_down"][l], h, S[pre + "_post_g"], 0.5)
    return out, dict(h=h, n=n, gp=gp, u=u, a=a, f=f)


def _ffn_bwd(l, dh, sv, S, W, G, SG, pre, deps=()):
    df, SG[pre + "_post_g"], dgp, du, dwd = _resbwd_mm(
        "ffn_bwd_act", dh, sv["f"], S[pre + "_post_g"], 0.5, W[pre + "_w_down"][l], True,
        act=(sv["gp"], sv["u"], sv["a"]), deps=deps, tm=ROW_TILE)
    G[pre + "_w_down"] = G[pre + "_w_down"][:l] + [dwd] + G[pre + "_w_down"][l + 1:]
    G[pre + "_w_gate"] = _mm_tn("ffn_dw_gate", dgp, sv["n"], G[pre + "_w_gate"], l, True)
    G[pre + "_w_up"] = _mm_tn("ffn_dw_up", du, sv["n"], G[pre + "_w_up"], l, True)
    dh_in, SG[pre + "_pre_g"] = _dn_prenorm("ffn_bwd_in", [dgp, du], [W[pre + "_w_gate"][l], W[pre + "_w_up"][l]],
                                            False, dh, sv["h"], S[pre + "_pre_g"])
    return dh_in


def _gates(zg):
    return [_sigmoid(jnp.concatenate([zg[2 * k].astype(F32), zg[2 * k + 1].astype(F32)], axis=1)) for k in range(3)]


def _merge_fwd(z, rs, ws, tm=ROW_TILE):
    T = z.shape[0]
    tm = _tile(T, tm)
    nb, kk, bw = ws[0].shape

    def body(*refs):
        r_refs, g_refs, w_refs, y_refs, m_ref = refs[:3], refs[3:9], refs[9:12], refs[12:15], refs[15]
        for r_ref, w_ref, y_ref in zip(r_refs, w_refs, y_refs):
            for b in range(nb):
                y_ref[:, b * bw:(b + 1) * bw] = jnp.dot(r_ref[...], w_ref[b],
                                                        preferred_element_type=F32).astype(y_ref.dtype)
        g = _gates([q[...] for q in g_refs])
        m_ref[...] = (g[0] * y_refs[0][...].astype(F32) + g[1] * y_refs[1][...].astype(F32)
                      + g[2] * y_refs[2][...].astype(F32)).astype(m_ref.dtype)

    row = pl.BlockSpec((tm, D_MODEL), lambda i: (i, 0))
    return pl.pallas_call(
        body, grid=(T // tm,),
        in_specs=[pl.BlockSpec((tm, kk), lambda i: (i, 0))] * 3
        + [pl.BlockSpec((tm, 512), lambda i, j=j: (i, ZB_GATES + j)) for j in range(6)]
        + [pl.BlockSpec(ws[0].shape, lambda i: (0, 0, 0))] * 3,
        out_specs=[row] * 4, out_shape=[jax.ShapeDtypeStruct((T, D_MODEL), BF)] * 4,
        name="mix_merge", compiler_params=_params(1))(*rs, *[z] * 6, *ws)


def _merge_bwd(dmerged, z, ys, ws, tm=ROW_TILE // 2):
    T = z.shape[0]
    tm = _tile(T, tm)
    nb, kk, bw = ws[0].shape

    def body(*refs):
        dm_ref, g_refs, y_refs, w_refs = refs[0], refs[1:7], refs[7:10], refs[10:13]
        dy_refs, lo_ref, hi_ref, dr_refs = refs[13:16], refs[16], refs[17], refs[18:21]
        cut = DZ_HALF - ZB_GATES * 512
        dm = dm_ref[...].astype(F32)
        g = _gates([q[...] for q in g_refs])
        for k in range(3):
            dy_refs[k][...] = (dm * g[k]).astype(BF)
            dzg = (dm * y_refs[k][...].astype(F32) * g[k] * (1.0 - g[k])).astype(BF)
            if k == 0:
                lo_ref[...] = dzg[:, :cut]
                hi_ref[:, :D_MODEL - cut] = dzg[:, cut:]
            else:
                hi_ref[:, k * D_MODEL - cut:(k + 1) * D_MODEL - cut] = dzg
            dr = None
            for b in range(nb):
                p = lax.dot_general(dy_refs[k][:, b * bw:(b + 1) * bw], w_refs[k][b], _NT,
                                    preferred_element_type=F32)
                dr = p if dr is None else dr + p
            dr_refs[k][...] = dr.astype(BF)

    row = pl.BlockSpec((tm, D_MODEL), lambda i: (i, 0))
    return pl.pallas_call(
        body, grid=(T // tm,),
        in_specs=[row] + [pl.BlockSpec((tm, 512), lambda i, j=j: (i, ZB_GATES + j)) for j in range(6)] + [row] * 3
        + [pl.BlockSpec(ws[0].shape, lambda i: (0, 0, 0))] * 3,
        out_specs=[row] * 3 + [pl.BlockSpec((tm, DZ_HALF - ZB_GATES * 512), lambda i: (i, 0)),
                               pl.BlockSpec((tm, DZ_HALF), lambda i: (i, 0))]
        + [pl.BlockSpec((tm, kk), lambda i: (i, 0))] * 3,
        out_shape=[jax.ShapeDtypeStruct((T, D_MODEL), BF)] * 3
        + [jax.ShapeDtypeStruct((T, DZ_HALF - ZB_GATES * 512), BF), jax.ShapeDtypeStruct((T, DZ_HALF), BF)]
        + [jax.ShapeDtypeStruct((T, kk), BF)] * 3,
        name="mix_merge_bwd", compiler_params=_params(1))(dmerged, *[z] * 6, *ys, *ws)


def _mix_fwd(l, h, S, W, deps=()):
    n, z = _norm_mm("mix_in", h, S["mix_pre_g"], [W["w_in"][l]], False, deps=deps)
    r_pool = _pool_fwd(z, S["pool_w"], S["pool_scale"])
    r_sgu = _sgu_fwd(z, S["sgu_ln_g"], S["sgu_ln_b"], S["sgu_w_s"], S["sgu_bias"])
    yc = _conv_fwd(z, W["conv_dw_k"], l, S["conv_dw_b"])

    def ln_silu(y, g, b):
        xh, _ = _ln_stats(y.astype(F32))
        return (_silu_and_grad(xh * g + b)[0],)

    r_conv = _rowwise("conv_ln", ln_silu, [(yc, 512, 0)], [S["conv_ln_g"], S["conv_ln_b"]], [(512, BF)])[0]
    y_pool, y_sgu, y_conv, merged = _merge_fwd(z, (r_pool, r_sgu, r_conv),
                                               [W["w_%s_out" % br][l] for br in ("pool", "sgu", "conv")])
    o, out = _mm_res("mix_out", merged, W["w_out"][l], h, S["mix_post_g"], 1.0)
    return out, dict(h=h, n=n, z=z, r_pool=r_pool, r_sgu=r_sgu, yc=yc, r_conv=r_conv, y_pool=y_pool, y_sgu=y_sgu,
                     y_conv=y_conv, merged=merged, o=o)


def _mix_bwd(l, dh, sv, S, W, G, SG, deps=()):
    z = sv["z"]
    do, SG["mix_post_g"], dmerged = _resbwd_mm("mix_bwd_out", dh, sv["o"], S["mix_post_g"], 1.0,
                                               W["w_out"][l].reshape(1, D, D), True, deps=deps)
    G["w_out"] = _mm_tn("mix_dw_out", sv["merged"], do, G["w_out"], l, True)

    branches = ("pool", "sgu", "conv")
    res = _merge_bwd(dmerged, z, [sv["y_" + br] for br in branches], [W["w_%s_out" % br][l] for br in branches])
    dz_gate_lo, dz_hi, dr = res[3], res[4], dict(zip(branches, res[5:]))
    for br, dy in zip(branches, res[:3]):
        wn = "w_%s_out" % br
        G[wn] = _mm_tn("branch_dw", sv["r_" + br], dy, G[wn], l, False)
    dz_pool, SG["pool_w"], SG["pool_scale"] = _pool_bwd(dr["pool"], z, S["pool_w"], S["pool_scale"])
    dzu, dzv, SG["sgu_ln_g"], SG["sgu_ln_b"], SG["sgu_w_s"], dbias = _sgu_bwd(
        dr["sgu"], z, S["sgu_ln_g"], S["sgu_ln_b"], S["sgu_w_s"], S["sgu_w_st"], S["sgu_bias"])
    SG["sgu_b_s"] = dbias[:, ::LANES].T

    def ln_silu_bwd(d, y, g, b):
        xh, r = _ln_stats(y.astype(F32))
        _, grad = _silu_and_grad(xh * g + b)
        return _ln_bwd(xh, r, g, d.astype(F32) * grad)

    dyc, SG["conv_ln_g"], SG["conv_ln_b"] = _rowwise(
        "conv_ln_bwd", ln_silu_bwd, [(dr["conv"], 512, 0), (sv["yc"], 512, 0)], [S["conv_ln_g"], S["conv_ln_b"]],
        [(512, BF)], [(1, 512), (1, 512)])
    dza, dzb, SG["conv_dw_k"], SG["conv_dw_b"] = _conv_bwd(dyc, z, W["conv_dw_k"], l)
    dz_lo = jnp.concatenate([dz_pool, dzu, dzv, dza, dzb, dz_gate_lo], axis=1)
    G["w_in"] = _mm_tn("mix_dw_in", sv["n"], dz_lo, G["w_in"], l, False)
    G["w_in"] = _mm_tn("mix_dw_in", sv["n"], dz_hi, G["w_in"], l, False, first=2)
    dh_in, SG["mix_pre_g"] = _dn_prenorm("mix_bwd_in", [dz_lo, dz_hi], W["w_in"][l], True, dh, sv["h"],
                                         S["mix_pre_g"])
    return dh_in


def _ple_out(h, p, gp, w3, g, tm=ROW_TILE):
    T, kp = p.shape
    nb, _, bw = w3.shape
    tm = _tile(T, tm)

    def body(h_ref, p_ref, gp_ref, w_ref, g_ref, e_ref, o_ref):
        p16 = p_ref[...].astype(BF)
        for b in range(nb):
            e_ref[:, b * bw:(b + 1) * bw] = jnp.dot(p16, w_ref[b], preferred_element_type=F32).astype(BF)
        q = _sigmoid(gp_ref[...].astype(F32)) * e_ref[...].astype(F32)
        o_ref[...] = h_ref[...] + _rms_fwd(q, g_ref[...])

    row = pl.BlockSpec((tm, D_MODEL), lambda i: (i, 0))
    return pl.pallas_call(
        body, grid=(T // tm,),
        in_specs=[row, pl.BlockSpec((tm, kp), lambda i: (i, 0)), row, pl.BlockSpec(w3.shape, lambda i: (0, 0, 0)),
                  pl.BlockSpec(g.shape, lambda i: (0, 0))],
        out_specs=[row, row],
        out_shape=[jax.ShapeDtypeStruct((T, D_MODEL), BF), jax.ShapeDtypeStruct((T, D_MODEL), F32)],
        name="ple_out", compiler_params=_params(1))(h, p, gp, w3, g)


def _ple_fwd(l, h, p_l, S, W, deps=()):
    n, gp = _norm_mm("ple_in", h, S["ple_pre_g"], [W["ple_w_gate"][l].reshape(1, D, D)], False, deps=deps)
    e, out = _ple_out(h, p_l, gp, W["ple_w_proj"][l], S["ple_post_g"])
    return out, dict(h=h, n=n, e=e, gp=gp, p=p_l)


def _ple_bwd_rows(dh, e, gp, g_post, w3, h, g_pre, deps=(), tm=ROW_TILE):
    T = dh.shape[0]
    w2 = w3.reshape(D_MODEL, D_MODEL)
    tm = _tile(T, tm)
    n_dep = len(deps)

    def body(*refs):
        dh_ref, e_ref, gp_ref, gpost_ref, w_ref, h_ref, gpre_ref, de_ref, dgp_ref, o_ref, dpost_ref, dpre_ref = \
            refs[n_dep:]
        first = pl.program_id(0) == 0
        d = dh_ref[...]
        sg = _sigmoid(gp_ref[...].astype(F32))
        ee = e_ref[...].astype(F32)
        dq, dpost = _rms_bwd(sg * ee, gpost_ref[...], d)
        de_ref[...] = (dq * sg).astype(BF)
        dgp = (dq * ee * sg * (1.0 - sg)).astype(BF)
        dgp_ref[...] = dgp
        dn = lax.dot_general(dgp, w_ref[...], _NT, preferred_element_type=F32)
        dx, dpre = _rms_bwd(h_ref[...], gpre_ref[...], dn)
        o_ref[...] = d + dx
        _acc_rows(dpost_ref, dpost, first)
        _acc_rows(dpre_ref, dpre, first)

    row = pl.BlockSpec((tm, D_MODEL), lambda i: (i, 0))
    vec = pl.BlockSpec((1, D_MODEL), lambda i: (0, 0))
    return pl.pallas_call(
        body, grid=(T // tm,),
        in_specs=[ANY] * n_dep + [row, row, row, vec, pl.BlockSpec(w2.shape, lambda i: (0, 0)), row, vec],
        out_specs=[row, row, row, vec, vec],
        out_shape=[jax.ShapeDtypeStruct((T, D_MODEL), BF)] * 2 + [jax.ShapeDtypeStruct((T, D_MODEL), F32)]
        + [jax.ShapeDtypeStruct((1, D_MODEL), F32)] * 2,
        name="ple_bwd", compiler_params=_params(1))(*deps, dh, e, gp, g_post, w2, h, g_pre)


def _ple_bwd(l, dh, sv, S, W, G, SG, deps=()):
    de, dgp, dh_in, SG["ple_post_g"], SG["ple_pre_g"] = _ple_bwd_rows(
        dh, sv["e"], sv["gp"], S["ple_post_g"], W["ple_w_gate"][l], sv["h"], S["ple_pre_g"], deps)
    G["ple_w_proj"] = _mm_tn("ple_dw_proj", sv["p"], de, G["ple_w_proj"], l, False)
    G["ple_w_gate"] = _mm_tn("ple_dw_gate", sv["n"], dgp, G["ple_w_gate"], l, True)
    return dh_in


def _layer_small(a, l):
    S = {}
    for name in SMALL:
        v = a[name][l]
        S[name] = v.reshape(1, -1) if v.ndim == 1 else v
    S["sgu_w_st"] = jnp.swapaxes(S["sgu_w_s"], 1, 2)
    S["sgu_bias"] = jnp.repeat(S["sgu_b_s"].T, LANES, axis=1)
    return S


def _layer_fwd(l, h, p_l, S, W, deps=(), hooks=None):
    hooks = hooks or {}

    def after(part, hv):
        return hooks[part](hv) if part in hooks else ()

    h, sv1 = _ffn_fwd(l, h, S, W, "ffn1", deps)
    h, sv2 = _mix_fwd(l, h, S, W, after("ffn1", h))
    h, sv3 = _ffn_fwd(l, h, S, W, "ffn2", after("mix", h))
    h, sv4 = _ple_fwd(l, h, p_l, S, W, after("ffn2", h))
    return h, (sv1, sv2, sv3, sv4)


def _layer_bwd(l, dh, sv, S, W, G, deps=(), mid=None):
    SG = {}
    dh = _ple_bwd(l, dh, sv[3], S, W, G, SG, deps)
    dh = _ffn_bwd(l, dh, sv[2], S, W, G, SG, "ffn2")
    dh = _mix_bwd(l, dh, sv[1], S, W, G, SG, mid(dh) if mid else ())
    dh = _ffn_bwd(l, dh, sv[0], S, W, G, SG, "ffn1")
    return dh, SG


HBM = pl.BlockSpec(memory_space=pltpu.HBM)
SEM = pl.BlockSpec(memory_space=pltpu.SEMAPHORE)
SIDE_EFFECT = pltpu.SideEffectType.DATAFLOW_SIDE_EFFECTING


def _place():
    x, y, c = lax.axis_index("x"), lax.axis_index("y"), lax.axis_index("c")
    chips = [(1 - x, y), (x, 1 - y), (1 - x, 1 - y)]
    return x, y, c, chips


def _remote(src, dst, send_sem, recv_sem, to):
    return pltpu.make_async_remote_copy(src_ref=src, dst_ref=dst, send_sem=send_sem, recv_sem=recv_sem,
                                        device_id=to, device_id_type=MESH)


def _split_start(name, plan, bufs, deps):
    count, fn = plan
    n, nd = len(bufs), len(deps)

    def body(*refs):
        send, recv = refs[nd + n], refs[nd + n + 1]
        x, y, c, chips = _place()
        for k, (src, dst, _, to) in enumerate(fn(refs[nd:nd + n], x, y, c, chips)):
            _remote(src, dst, send.at[k], recv.at[k], to).start()
        refs[-1][...] = jnp.zeros_like(refs[-1])

    res = pl.pallas_call(
        body, in_specs=[ANY] * nd + [HBM] * n,
        out_specs=[SEM, SEM] + [HBM] * n + [pl.BlockSpec(memory_space=pltpu.VMEM)],
        out_shape=[pltpu.SemaphoreType.DMA((count,)), pltpu.SemaphoreType.DMA((count,))]
        + [pltpu.HBM(b.shape, b.dtype) for b in bufs] + [jax.ShapeDtypeStruct((8, LANES), F32)],
        input_output_aliases={nd + i: 2 + i for i in range(n)}, name=name,
        compiler_params=pltpu.CompilerParams(has_side_effects=SIDE_EFFECT),
    )(*deps, *[pltpu.with_memory_space_constraint(b, pltpu.HBM) for b in bufs])
    return (res[0], res[1]), list(res[2:2 + n]), res[-1]


def _split_wait(name, plan, sems, bufs, after):
    _, fn = plan
    n = len(bufs)

    def body(*refs):
        send, recv = refs[n], refs[n + 1]
        x, y, c, chips = _place()
        for k, (src, _, land, to) in enumerate(fn(refs[:n], x, y, c, chips)):
            cp = _remote(src, land, send.at[k], recv.at[k], to)
            cp.wait_send()
            cp.wait_recv()

    res = pl.pallas_call(
        body, in_specs=[HBM] * n + [SEM, SEM] + [ANY] * len(after), out_specs=[HBM] * n,
        out_shape=[pltpu.HBM(b.shape, b.dtype) for b in bufs], input_output_aliases={i: i for i in range(n)},
        name=name, compiler_params=pltpu.CompilerParams(has_side_effects=SIDE_EFFECT))(*bufs, *sems, *after)
    return list(res)


def _gather_plans(n):
    def across(b, x, y, c, chips):
        me, out = 2 * x + y, []
        for a in range(n):
            rh = b[a].shape[1] // 2
            mine = b[a].at[me, pl.ds(c * rh, rh)]
            for cx, cy in chips:
                out.append((mine, mine, b[a].at[2 * cx + cy, pl.ds(c * rh, rh)], (cx, cy, c)))
        return out

    def to_sibling(b, x, y, c, chips):
        out = []
        for a in range(n):
            rh = b[a].shape[1] // 2
            for cx, cy in chips:
                piece = b[a].at[2 * cx + cy, pl.ds(c * rh, rh)]
                out.append((piece, piece, b[a].at[2 * cx + cy, pl.ds((1 - c) * rh, rh)], (x, y, 1 - c)))
        return out

    return (3 * n, across), (3 * n, to_sibling)


def _pair_plan(n):
    def fn(b, x, y, c, chips):
        out = []
        for a in range(n):
            rh = b[a].shape[1] // 2
            out.append((b[a].at[:, pl.ds((1 - c) * rh, rh)], b[n + a], b[n + a], (x, y, 1 - c)))
        return out

    return n, fn


def _cross_plan(n):
    def fn(b, x, y, c, chips):
        out = []
        for a in range(n):
            for j, (cx, cy) in enumerate(chips):
                out.append((b[a].at[2 * cx + cy], b[n + a].at[j], b[n + a].at[j], (cx, cy, c)))
        return out

    return 3 * n, fn


def _share_plan(n, l):
    def fn(b, x, y, c, chips):
        out = []
        for a in range(n):
            rh = b[a].shape[1] // 2
            mine = b[a].at[l, pl.ds(c * rh, rh)]
            out.append((mine, mine, b[a].at[l, pl.ds((1 - c) * rh, rh)], (x, y, 1 - c)))
        return out

    return n, fn


def _exchange_call(name, body, ins, out_shapes, n_remote, in_place=False):
    scratch = [pltpu.SemaphoreType.DMA((n_remote,)), pltpu.SemaphoreType.DMA((n_remote,))]
    aliases = {i: i for i in range(len(ins))} if in_place else {}
    return pl.pallas_call(body, in_specs=[ANY] * len(ins), out_specs=[ANY] * len(out_shapes), out_shape=out_shapes,
                          scratch_shapes=scratch, input_output_aliases=aliases, name=name)(*ins)


def _peers(x, y, c):
    return [(1 - x if m & 4 else x, 1 - y if m & 2 else y, 1 - c if m & 1 else c) for m in range(1, 8)]


def _scatter_small(v3):
    def body(v_ref, o_ref, send, recv):
        x, y, c, _ = _place()
        cps = []
        for m, (px, py, pc) in enumerate(_peers(x, y, c)):
            cps.append(_remote(v_ref.at[4 * px + 2 * py + pc], o_ref.at[m], send.at[m], recv.at[m], (px, py, pc)))
            cps[-1].start()
        for cp in cps:
            cp.wait()

    return _exchange_call("scatter_small", body, [v3], [jax.ShapeDtypeStruct((7,) + v3.shape[1:], v3.dtype)], 7)[0]


def _gather_small(buf):
    def body(_, o_ref, send, recv):
        x, y, c, _ = _place()
        mine = o_ref.at[4 * x + 2 * y + c]
        peers = _peers(x, y, c)
        cps = []
        for m, to in enumerate(peers):
            cps.append(_remote(mine, mine, send.at[m], recv.at[m], to))
            cps[-1].start()
        for m, (px, py, pc) in enumerate(peers):
            slab = o_ref.at[4 * px + 2 * py + pc]
            _remote(slab, slab, send.at[m], recv.at[m], (px, py, pc)).wait_recv()
            cps[m].wait_send()

    return _exchange_call("gather_small", body, [buf], [jax.ShapeDtypeStruct(buf.shape, buf.dtype)], 7,
                          in_place=True)[0]


def _allreduce_small(v, pos):
    rows = v.shape[0]
    rs = rows // 8
    v3 = v.reshape(8, rs, LANES)
    got = _scatter_small(v3)
    tm = _tile(rs)
    ins = [(v3, (None, tm, LANES), lambda i, p: (p[2], i, 0))]
    ins += [(got, (None, tm, LANES), lambda i, p, m=m: (m, i, 0)) for m in range(7)]
    buf = _tiled("sum_small", lambda *t: (((((((t[0] + t[1]) + t[2]) + t[3]) + t[4]) + t[5]) + t[6]) + t[7],),
                 (rs // tm,), pos, ins, [((8, rs, LANES), F32, (None, tm, LANES), lambda i, p: (p[2], i, 0))])[0]
    return _gather_small(buf).reshape(rows, LANES)


ADD_ROWS = 128


def _multi_tiled(name, fn, pos, groups, in_place=False):
    steps = max(g[1] for g in groups)
    flat_in, in_specs, out_specs, out_shape, counts, dests = [], [], [], [], [], []
    for ins, n_t, (shape, dtype, oidx, dest) in groups:
        for arr, idx in ins:
            flat_in.append(arr)
            in_specs.append(pl.BlockSpec((ADD_ROWS, arr.shape[1]),
                                         lambda i, p, idx=idx, n_t=n_t: (idx(jnp.minimum(i, n_t - 1), p), 0)))
        out_specs.append(pl.BlockSpec((ADD_ROWS, shape[1]),
                                      lambda i, p, oidx=oidx, n_t=n_t: (oidx(jnp.minimum(i, n_t - 1), p), 0)))
        out_shape.append(jax.ShapeDtypeStruct(shape, dtype))
        counts.append((len(ins), n_t))
        dests.append(dest)
    n_in = len(flat_in)
    extra = dests if in_place else []

    def body(_, *refs):
        outs = refs[n_in + len(extra):]
        k = 0
        for (n_a, n_t), o_ref in zip(counts, outs):
            tiles = refs[k:k + n_a]
            k += n_a

            @pl.when(pl.program_id(0) < n_t)
            def _(tiles=tiles, o_ref=o_ref):
                o_ref[...] = fn(*[t[...] for t in tiles]).astype(o_ref.dtype)

    spec = pltpu.PrefetchScalarGridSpec(num_scalar_prefetch=1, grid=(steps,),
                                        in_specs=in_specs + [ANY] * len(extra), out_specs=out_specs)
    return pl.pallas_call(body, grid_spec=spec, out_shape=out_shape,
                          input_output_aliases={1 + n_in + k: k for k in range(len(extra))}, name=name,
                          compiler_params=_params(1))(pos, *flat_in, *extra)


def _add_pair(grads, got, pos):
    groups = []
    for g, q in zip(grads, got):
        nb, R, C = g.shape
        rh = R // 2
        nh = rh // ADD_ROWS
        groups.append(([(g.reshape(nb * R, C), lambda t, p, nh=nh: (t // nh) * 2 * nh + p[1] * nh + t % nh),
                        (q.reshape(nb * rh, C), lambda t, p: t)], nb * nh,
                       ((nb * rh, C), BF, lambda t, p: t, None)))
    res = _multi_tiled("rs_add_pair", lambda u, w: u.astype(F32) + w.astype(F32), pos, groups)
    return [t.reshape(q.shape) for t, q in zip(res, got)]


def _add_chips(parts, slots, reduced, l, pos):
    def add(own, s0, s1, s2):
        return ((own.astype(F32) + s0.astype(F32)) + s1.astype(F32)) + s2.astype(F32)

    groups = []
    for t, s, red in zip(parts, slots, reduced):
        nb, rh, C = t.shape
        L = red.shape[0]
        nh = rh // ADD_ROWS
        ins = [(t.reshape(nb * rh, C), lambda i, p, nh=nh: p[0] * nh + i)]
        ins += [(s.reshape(3 * rh, C), lambda i, p, j=j, nh=nh: j * nh + i) for j in range(3)]
        groups.append((ins, nh, ((L * 2 * rh, C), F32, lambda i, p, nh=nh: l * 2 * nh + p[1] * nh + i,
                                 red.reshape(L * 2 * rh, C))))
    res = _multi_tiled("rs_add_chips", add, pos, groups, in_place=True)
    return [buf.reshape(red.shape) for buf, red in zip(res, reduced)]


def _adamw_math(w, g, m, v):
    m = ADAM_B1 * m + (1.0 - ADAM_B1) * g
    v = ADAM_B2 * v + (1.0 - ADAM_B2) * (g * g)
    m_hat = m / (1.0 - ADAM_B1 ** ADAM_STEP)
    v_hat = v / (1.0 - ADAM_B2 ** ADAM_STEP)
    return -ADAM_LR * (m_hat / (jnp.sqrt(v_hat) + ADAM_EPS) + ADAM_WD * w), m, v


def _adamw(w, g, m, v, lo=0, hi=None, into=None, deps=()):
    L, R, C = w.shape
    hi = L if hi is None else hi
    tr = _tile(R, max(16, ADAM_TILE_ELEMS // C))
    extra = (list(into) if into else []) + list(deps)
    n_alias = 4 if into else 0

    def body(w_ref, g_ref, m_ref, v_ref, *rest):
        go_ref, d_ref, mo_ref, vo_ref = rest[len(extra):]
        gv = g_ref[...]
        d, mn, vn = _adamw_math(w_ref[...], gv, m_ref[...], v_ref[...])
        go_ref[...] = gv
        d_ref[...] = d
        mo_ref[...] = mn
        vo_ref[...] = vn

    spec = pl.BlockSpec((None, tr, C), lambda l, i: (l + lo, i, 0))
    out = jax.ShapeDtypeStruct(w.shape, F32)
    return pl.pallas_call(body, grid=(hi - lo, R // tr), in_specs=[spec] * 4 + [ANY] * len(extra),
                          out_specs=[spec] * 4, out_shape=[out] * 4,
                          input_output_aliases={4 + k: k for k in range(n_alias)}, name="adamw",
                          compiler_params=_params(2))(w, g, m, v, *extra)


def _pack(parts):
    flat = jnp.concatenate([q.reshape(-1, LANES) for q in parts], axis=0)
    return jnp.pad(flat, ((0, -flat.shape[0] % ROW_TILE), (0, 0)))


def _unpack(flat, like):
    out, r = [], 0
    for q in like:
        n = q.size // LANES
        out.append(flat[r:r + n].reshape(q.shape))
        r += n
    return out


def _train_step(a):
    a = dict(a)
    L = a["ffn1_pre_g"].shape[0]
    x, y, c, _ = _place()
    chip = 2 * x + y
    pos = jnp.stack([chip, c, 2 * chip + c]).astype(jnp.int32)
    for name in TRANSPOSED:
        for pre in ("", "m_", "v_"):
            a[pre + name] = jnp.swapaxes(a[pre + name], 1, 2)
    big = [b[0] for b in BIG]
    gathered = big + ["conv_dw_k"]
    n_w, n_g = len(gathered), len(big)

    own = [None] * n_w
    W = {name: [None] * L for name in gathered}
    every = list(range(n_w))
    first, mixer, later = every[:3], every[3:8] + [n_g], every[8:n_g]
    rest = mixer + later

    def cast(i, deps):
        if i == n_g:
            taps = a["conv_dw_k"].reshape(L, CONV_TAPS, LANES)
            return _cast_layers("pad_conv_taps", taps, CONV_PAD, LANES, F32, pos, deps)
        name, _, _, _, rp, cp = BIG[i]
        return _cast_layers("cast_weight", a[name], rp, cp, BF, pos, deps)

    def gather_first(l, ids, tag, deps):
        return _split_start("gather_a%d%s" % (l, tag), _gather_plans(len(ids))[0], [own[i][l] for i in ids], deps)

    def gather_second(l, ids, tag, state, after):
        across, to_sibling = _gather_plans(len(ids))
        bufs = _split_wait("gather_a%d%s_done" % (l, tag), across, state[0], state[1], after)
        return _split_start("gather_b%d%s" % (l, tag), to_sibling, bufs, [])

    def gather_done(l, ids, tag, state, after):
        to_sibling = _gather_plans(len(ids))[1]
        bufs = _split_wait("gather_b%d%s_done" % (l, tag), to_sibling, state[0], state[1], after)
        for i, buf in zip(ids, bufs):
            W[gathered[i]][l] = buf

    for i in first:
        own[i] = cast(i, ())
    state = gather_first(0, first, "f", [])
    for i in rest:
        own[i] = cast(i, (state[2],))
    state = gather_second(0, first, "f", state, [own[i][0] for i in rest])
    gather_done(0, first, "f", state, [])
    box = {"mixer": gather_first(0, mixer, "m", [])}

    small = [_layer_small(a, l) for l in range(L)]
    h, saved = a["x"][0], []
    for l in range(L):
        hooks = {}
        if l == 0:
            deps = (box["mixer"][2],)

            def after_ffn1(hv, box=box):
                gather_done(0, mixer, "m", gather_second(0, mixer, "m", box["mixer"], [hv]), [])
                box["later"] = gather_first(0, later, "t", [hv])
                tokens = [box["later"][2]]
                if L > 1:
                    box["next"] = gather_first(1, every, "", [box["later"][2]])
                    tokens.append(box["next"][2])
                return tuple(tokens)

            def after_mix(hv, box=box):
                gather_done(0, later, "t", gather_second(0, later, "t", box["later"], [hv]), [])
                return ()

            hooks["ffn1"], hooks["mix"] = after_ffn1, after_mix
            switch = "ffn2"
        else:
            deps = ()
            if l + 1 < L:
                box["next"] = gather_first(l + 1, every, "", [h])
                deps = (box["next"][2],)
            switch = "mix"
        if l + 1 < L:
            def second(hv, l=l, box=box):
                box["next"] = gather_second(l + 1, every, "", box["next"], [hv])
                return (box["next"][2],)

            hooks[switch] = second
        h, sv = _layer_fwd(l, h, a["p"][l, 0], small[l], W, deps, hooks)
        saved.append(sv)
        if l + 1 < L:
            gather_done(l + 1, every, "", box["next"], [h])

    def loss_fn(yv, t):
        e = yv - t
        return e * (1.0 / D), jnp.sum(e * e, axis=0, keepdims=True)

    dh, lsum = _rowwise("loss", loss_fn, [(h, D, 0), (a["loss_target"][0], D, 0)], [], [(D, F32)], [(1, D)])
    loss = lax.psum(0.5 * jnp.sum(lsum) / D, ("x", "y", "c"))

    G = {name: [jax.ShapeDtypeStruct((N_CHIPS, rp, cp), BF)] * L for name, _, _, _, rp, cp in BIG}
    reduced = [lax.empty((L, rp, cp), F32) for _, _, _, _, rp, cp in BIG]
    pair, cross = _pair_plan(n_g), _cross_plan(n_g)
    small_grads = [None] * L

    def pair_start(l, deps):
        grads = [G[name][l] for name in big]
        lands = [lax.empty((N_CHIPS, g.shape[1] // 2, g.shape[2]), BF) for g in grads]
        return _split_start("rs_pair%d" % l, pair, grads + lands, deps)

    def cross_start(l, state, after):
        bufs = _split_wait("rs_pair%d_done" % l, pair, state[0], state[1], after)
        parts = _add_pair(bufs[:n_g], bufs[n_g:], pos)
        lands = [lax.empty((3,) + t.shape[1:], BF) for t in parts]
        return _split_start("rs_cross%d" % l, cross, parts + lands, [])

    def share_start(l, state, after, reduced):
        bufs = _split_wait("rs_cross%d_done" % l, cross, state[0], state[1], after)
        reduced = _add_chips(bufs[:n_g], bufs[n_g:], reduced, l, pos)
        return _split_start("rs_share%d" % l, _share_plan(n_g, l), reduced, [])

    def share_done(l, state, after):
        return _split_wait("rs_share%d_done" % l, _share_plan(n_g, l), state[0], state[1], after)

    st_pair = st_cross = st_share = None
    for l in reversed(range(L)):
        deps = tuple(s[2] for s in (st_pair, st_share) if s is not None)
        box = {"cross": None}

        def mid(dm, l=l, box=box, st_pair=st_pair, st_share=st_share):
            out = []
            if st_share is not None:
                box["reduced"] = share_done(l + 2, st_share, [dm])
            if st_pair is not None:
                box["cross"] = cross_start(l + 1, st_pair, [dm])
                out.append(box["cross"][2])
            return tuple(out)

        dh, small_grads[l] = _layer_bwd(l, dh, saved[l], small[l], W, G, deps, mid)
        if st_share is not None:
            reduced = box["reduced"]
        st_share = share_start(l + 1, box["cross"], [dh], reduced) if box["cross"] is not None else None
        st_pair = pair_start(l, [dh])
    grad_x = dh
    st_cross = cross_start(0, st_pair, [])
    small_names = SMALL + ("conv_dw_k",)
    stacked = [jnp.stack([small_grads[l][name] for l in range(L)]) for name in small_names]
    summed = dict(zip(small_names, _unpack(_allreduce_small(_pack(stacked), pos), stacked)))
    if st_share is not None:
        reduced = share_done(1, st_share, [summed[small_names[0]], st_cross[2]])
    upper = {}
    if L > 1:
        for name, red in zip(big, reduced):
            upper[name] = _adamw(a[name], red, a["m_" + name], a["v_" + name], 1, L, deps=[st_cross[2]])
    st_share = share_start(0, st_cross, [r[1] for r in upper.values()] + [summed[small_names[0]]], reduced)
    reduced = share_done(0, st_share, [])
    big_grads = dict(zip(big, reduced))

    grads, deltas, new_m, new_v = {}, {}, {}, {}
    for name in big:
        res = _adamw(a[name], big_grads[name], a["m_" + name], a["v_" + name], 0, 1, upper.get(name))
        if name in TRANSPOSED:
            res = [jnp.swapaxes(r, 1, 2) for r in res]
        grads[name], deltas[name], new_m[name], new_v[name] = res
    taps = lax.dynamic_slice_in_dim(summed["conv_dw_k"], chip * LANES, LANES, axis=2)[:, :CONV_TAPS]
    grads["conv_dw_k"] = taps.reshape(a["conv_dw_k"].shape)
    for name in SMALL:
        grads[name] = summed[name].reshape(a[name].shape)
    shapes = [a[name] for name in small_names]
    res = _adamw(*[_pack([a[pre + name] if pre != "g" else grads[name] for name in small_names])[None]
                   for pre in ("", "g", "m_", "v_")])
    for dst, flat in zip((deltas, new_m, new_v), res[1:]):
        for name, val in zip(small_names, _unpack(flat[0], shapes)):
            dst[name] = val

    return (loss, grad_x[None], *[grads[n] for n in WEIGHTS], *[deltas[n] for n in WEIGHTS],
            *[new_m[n] for n in WEIGHTS], *[new_v[n] for n in WEIGHTS])


def kernel(x, p, ffn1_pre_g, ffn1_w_gate, ffn1_w_up, ffn1_w_down, ffn1_post_g, mix_pre_g, w_in, pool_w, pool_scale, w_pool_out, sgu_ln_g, sgu_ln_b, sgu_w_s, sgu_b_s, w_sgu_out, conv_dw_k, conv_dw_b, conv_ln_g, conv_ln_b, w_conv_out, w_out, mix_post_g, ffn2_pre_g, ffn2_w_gate, ffn2_w_up, ffn2_w_down, ffn2_post_g, ple_w_proj, ple_pre_g, ple_w_gate, ple_post_g, loss_target, m_ffn1_pre_g, m_ffn1_w_gate, m_ffn1_w_up, m_ffn1_w_down, m_ffn1_post_g, m_mix_pre_g, m_w_in, m_pool_w, m_pool_scale, m_w_pool_out, m_sgu_ln_g, m_sgu_ln_b, m_sgu_w_s, m_sgu_b_s, m_w_sgu_out, m_conv_dw_k, m_conv_dw_b, m_conv_ln_g, m_conv_ln_b, m_w_conv_out, m_w_out, m_mix_post_g, m_ffn2_pre_g, m_ffn2_w_gate, m_ffn2_w_up, m_ffn2_w_down, m_ffn2_post_g, m_ple_w_proj, m_ple_pre_g, m_ple_w_gate, m_ple_post_g, v_ffn1_pre_g, v_ffn1_w_gate, v_ffn1_w_up, v_ffn1_w_down, v_ffn1_post_g, v_mix_pre_g, v_w_in, v_pool_w, v_pool_scale, v_w_pool_out, v_sgu_ln_g, v_sgu_ln_b, v_sgu_w_s, v_sgu_b_s, v_w_sgu_out, v_conv_dw_k, v_conv_dw_b, v_conv_ln_g, v_conv_ln_b, v_w_conv_out, v_w_out, v_mix_post_g, v_ffn2_pre_g, v_ffn2_w_gate, v_ffn2_w_up, v_ffn2_w_down, v_ffn2_post_g, v_ple_w_proj, v_ple_pre_g, v_ple_w_gate, v_ple_post_g):
    return _train_step(dict(locals()))
```

```python
import math

import jax
import jax.numpy as jnp
from jax import lax
from jax.experimental import pallas as pl
from jax.experimental.pallas import tpu as pltpu

BF = jnp.bfloat16
F32 = jnp.float32
EPS = 1e-6
D_MODEL = 1024
LANES = 128
SUBLANES = 8
MXU_TILE = 256
N_CHIPS = 4
FFN_SHARD = 704
FFN_SHARD_PAD = 768
POOL_WINDOWS = (2, 4, 8, 16)
SGU_HEADS = 4
CHUNK = 128
CONV_TAPS = 31
CONV_PAD = 32
ROW_TILE = 512
EPI_ROWS = 256
VMEM_LIMIT_BYTES = 56 * 1024 * 1024
ADAM_TILE_ELEMS = 3 * 128 * 1024
ADAM_LR, ADAM_B1, ADAM_B2, ADAM_EPS, ADAM_WD, ADAM_STEP =0.001, 0.9, 0.999, 1e-08, 0.01, 10
MESH = pl.DeviceIdType.MESH
ANY = pl.BlockSpec(memory_space=pl.ANY)

ZB_POOL, ZB_U, ZB_V, ZB_A, ZB_B, ZB_GATES = 0, 1, 2, 3, 4, 5
DZ_HALF = 2816

TRANSPOSED = ("ffn1_w_gate", "ffn1_w_up", "ffn2_w_gate", "ffn2_w_up")
BIG = (
    ("ffn1_w_gate", "row", FFN_SHARD, 1024, FFN_SHARD_PAD, 1024),
    ("ffn1_w_up", "row", FFN_SHARD, 1024, FFN_SHARD_PAD, 1024),
    ("ffn1_w_down", "row", FFN_SHARD, 1024, FFN_SHARD_PAD, 1024),
    ("w_in", "col", 1024, 1408, 1024, 1408),
    ("w_pool_out", "col", 512, 256, 512, 256),
    ("w_sgu_out", "col", 512, 256, 512, 256),
    ("w_conv_out", "col", 512, 256, 512, 256),
    ("w_out", "row", 256, 1024, 256, 1024),
    ("ffn2_w_gate", "row", FFN_SHARD, 1024, FFN_SHARD_PAD, 1024),
    ("ffn2_w_up", "row", FFN_SHARD, 1024, FFN_SHARD_PAD, 1024),
    ("ffn2_w_down", "row", FFN_SHARD, 1024, FFN_SHARD_PAD, 1024),
    ("ple_w_proj", "col", 256, 256, 256, 256),
    ("ple_w_gate", "row", 256, 1024, 256, 1024),
)
SMALL = ("ffn1_pre_g", "ffn1_post_g", "mix_pre_g", "pool_w", "pool_scale", "sgu_ln_g", "sgu_ln_b", "sgu_w_s",
         "sgu_b_s", "conv_dw_b", "conv_ln_g", "conv_ln_b", "mix_post_g", "ffn2_pre_g", "ffn2_post_g",
         "ple_pre_g", "ple_post_g")
WEIGHTS = ("ffn1_pre_g", "ffn1_w_gate", "ffn1_w_up", "ffn1_w_down", "ffn1_post_g", "mix_pre_g", "w_in", "pool_w",
           "pool_scale", "w_pool_out", "sgu_ln_g", "sgu_ln_b", "sgu_w_s", "sgu_b_s", "w_sgu_out", "conv_dw_k",
           "conv_dw_b", "conv_ln_g", "conv_ln_b", "w_conv_out", "w_out", "mix_post_g", "ffn2_pre_g", "ffn2_w_gate",
           "ffn2_w_up", "ffn2_w_down", "ffn2_post_g", "ple_w_proj", "ple_pre_g", "ple_w_gate", "ple_post_g")


def _params(n_grid):
    return pltpu.CompilerParams(dimension_semantics=("arbitrary",) * n_grid, vmem_limit_bytes=VMEM_LIMIT_BYTES)


def _tile(n, cap=ROW_TILE):
    for t in range(min(cap, n) - min(cap, n) % 16, 0, -16):
        if n % t == 0:
            return t
    return n


def _sigmoid(x):
    return 0.5 * jnp.tanh(0.5 * x) + 0.5


def _silu_and_grad(x):
    s = _sigmoid(x)
    return x * s, s * (1.0 + x * (1.0 - s))


def _gelu_and_grad(x):
    cdf = 0.5 * (1.0 + lax.erf(x * (1.0 / math.sqrt(2.0))))
    pdf = jnp.exp(-0.5 * x * x) * (1.0 / math.sqrt(2.0 * math.pi))
    return x * cdf, cdf + x * pdf


def _rms_fwd(x, g):
    return x * lax.rsqrt(jnp.mean(x * x, axis=-1, keepdims=True) + EPS) * g


def _rms_bwd(x, g, dy):
    r = lax.rsqrt(jnp.mean(x * x, axis=-1, keepdims=True) + EPS)
    xh = x * r
    dxh = dy * g
    dx = r * (dxh - xh * jnp.mean(dxh * xh, axis=-1, keepdims=True))
    return dx, jnp.sum(dy * xh, axis=0, keepdims=True)


def _ln_stats(x):
    xc = x - jnp.mean(x, axis=-1, keepdims=True)
    r = lax.rsqrt(jnp.mean(xc * xc, axis=-1, keepdims=True) + EPS)
    return xc * r, r


def _ln_bwd(xh, r, g, dy):
    dxh = dy * g
    dx = r * (dxh - jnp.mean(dxh, axis=-1, keepdims=True) - xh * jnp.mean(dxh * xh, axis=-1, keepdims=True))
    return dx, jnp.sum(dy * xh, axis=0, keepdims=True), jnp.sum(dy, axis=0, keepdims=True)


def _rowwise(name, fn, rows, consts, outs, accs=(), tm=ROW_TILE, deps=()):
    T = rows[0][0].shape[-2]
    tm = _tile(T, tm)
    n_in, n_o, n_dep = len(rows) + len(consts), len(outs), len(deps)

    def body(*refs):
        refs = refs[n_dep:]
        res = fn(*[r[...] for r in refs[:n_in]])
        for ref, val in zip(refs[n_in:n_in + n_o], res[:n_o]):
            ref[...] = val.astype(ref.dtype)
        acc_refs = refs[n_in + n_o:]
        if acc_refs:
            @pl.when(pl.program_id(0) == 0)
            def _():
                for ref, val in zip(acc_refs, res[n_o:]):
                    ref[...] = val

            @pl.when(pl.program_id(0) != 0)
            def _():
                for ref, val in zip(acc_refs, res[n_o:]):
                    ref[...] += val

    in_specs = [ANY] * n_dep
    for row in rows:
        w, cb = row[1], row[2]
        if len(row) == 4:
            in_specs.append(pl.BlockSpec((None, tm, w), lambda i, cb=cb, ld=row[3]: (ld, i, cb)))
        else:
            in_specs.append(pl.BlockSpec((tm, w), lambda i, cb=cb: (i, cb)))
    in_specs += [pl.BlockSpec(c.shape, lambda i: (0, 0)) for c in consts]
    out_specs = [pl.BlockSpec((tm, w), lambda i: (i, 0)) for w, _ in outs]
    out_specs += [pl.BlockSpec(s, lambda i: (0, 0)) for s in accs]
    out_shape = [jax.ShapeDtypeStruct((T, w), dt) for w, dt in outs]
    out_shape += [jax.ShapeDtypeStruct(s, F32) for s in accs]
    return pl.pallas_call(body, grid=(T // tm,), in_specs=in_specs, out_specs=out_specs, out_shape=out_shape,
                          name=name, compiler_params=_params(1))(*deps, *[r[0] for r in rows], *consts)


def _tiled(name, fn, grid, pos, ins, outs):
    n_in = len(ins)

    def body(_, *refs):
        res = fn(*[r[...] for r in refs[:n_in]])
        for ref, val in zip(refs[n_in:], res):
            ref[...] = val.astype(ref.dtype)

    spec = pltpu.PrefetchScalarGridSpec(
        num_scalar_prefetch=1, grid=grid, in_specs=[pl.BlockSpec(bs, im) for _, bs, im in ins],
        out_specs=[pl.BlockSpec(bs, im) for _, _, bs, im in outs])
    return pl.pallas_call(body, grid_spec=spec, out_shape=[jax.ShapeDtypeStruct(s, d) for s, d, _, _ in outs],
                          name=name, compiler_params=_params(len(grid)))(pos, *[a for a, _, _ in ins])


def _cast_layers(name, w, rp, cp, dtype, pos, deps=()):
    L, r, c = w.shape

    def body(_, w_ref, *rest):
        for k, o_ref in enumerate(rest[len(deps):]):
            @pl.when(pl.program_id(0) == k)
            def _(o_ref=o_ref):
                if (rp, cp) != (r, c):
                    o_ref[...] = jnp.zeros_like(o_ref)
                    o_ref[pl.ds(0, r), pl.ds(0, c)] = w_ref[...].astype(dtype)
                else:
                    o_ref[...] = w_ref[...].astype(dtype)

    spec = pltpu.PrefetchScalarGridSpec(
        num_scalar_prefetch=1, grid=(L,),
        in_specs=[pl.BlockSpec((None, r, c), lambda l, p: (l, 0, 0))] + [ANY] * len(deps),
        out_specs=[pl.BlockSpec((None, rp, cp), lambda l, p: (p[0], 0, 0))] * L)
    return pl.pallas_call(body, grid_spec=spec, out_shape=[jax.ShapeDtypeStruct((N_CHIPS, rp, cp), dtype)] * L,
                          name=name, compiler_params=_params(1))(pos, w, *deps)


_NN = (((1,), (0,)), ((), ()))
_NT = (((1,), (1,)), ((), ()))
_TN = (((0,), (0,)), ((), ()))


def _mm_tn(name, a, dy, buf, l, a_blocked, tk=ROW_TILE, first=0):
    T = a.shape[0]
    nb, R, C = buf[l].shape
    extra = [buf[l]] if first else []

    def body(a_ref, dy_ref, *rest):
        rest[-1][...] = lax.dot_general(a_ref[...].astype(BF), dy_ref[...].astype(BF), _TN,
                                        preferred_element_type=F32).astype(BF)

    if a_blocked:
        grid = (nb,)
        in_specs = [pl.BlockSpec((T, R), lambda b: (0, b)), pl.BlockSpec((T, C), lambda b: (0, 0))]
        out_specs = pl.BlockSpec((None, R, C), lambda b: (b, 0, 0))
    else:
        tk = min(tk, R)
        grid = (dy.shape[1] // C, R // tk)
        in_specs = [pl.BlockSpec((T, tk), lambda b, k: (0, k)), pl.BlockSpec((T, C), lambda b, k: (0, b))]
        out_specs = pl.BlockSpec((None, tk, C), lambda b, k: (b + first, k, 0))
    buf = list(buf)
    buf[l] = pl.pallas_call(body, grid=grid, in_specs=in_specs + [ANY] * len(extra), out_specs=out_specs,
                            out_shape=jax.ShapeDtypeStruct((nb, R, C), BF),
                            input_output_aliases={2: 0} if extra else {}, name=name,
                            compiler_params=_params(len(grid)))(a, dy, *extra)
    return buf


def _acc_rows(ref, val, first):
    @pl.when(first)
    def _():
        ref[...] = val

    @pl.when(jnp.logical_not(first))
    def _():
        ref[...] += val


def _norm_mm(name, h, g, ws, trans_w, act=False, deps=(), tm=2 * ROW_TILE):
    T = h.shape[0]
    nb, r, cc = ws[0].shape
    bo = r if trans_w else cc
    tm = _tile(T, tm)
    n_w, n_dep = len(ws), len(deps)

    def body(*refs):
        refs = refs[n_dep:]
        h_ref, g_ref, w_refs = refs[0], refs[1], refs[2:2 + n_w]
        n_ref, o_refs, n_s = refs[2 + n_w], refs[3 + n_w:3 + 2 * n_w], refs[-1]

        @pl.when(pl.program_id(1) == 0)
        def _():
            n = _rms_fwd(h_ref[...].astype(F32), g_ref[...]).astype(BF)
            n_s[...] = n
            n_ref[...] = n

        n = n_s[...]
        prods = []
        for w_ref, o_ref in zip(w_refs, o_refs):
            prods.append(lax.dot_general(n, w_ref[...], _NT if trans_w else _NN,
                                         preferred_element_type=F32).astype(BF))
            o_ref[...] = prods[-1]
        if act:
            refs[3 + 2 * n_w][...] = (_silu_and_grad(prods[0].astype(F32))[0] * prods[1].astype(F32)).astype(BF)

    wide = pl.BlockSpec((tm, bo), lambda i, b: (i, b))
    n_out = n_w + (1 if act else 0)
    return pl.pallas_call(
        body, grid=(T // tm, nb),
        in_specs=[ANY] * n_dep + [pl.BlockSpec((tm, D_MODEL), lambda i, b: (i, 0)),
                                  pl.BlockSpec(g.shape, lambda i, b: (0, 0))]
        + [pl.BlockSpec((None, r, cc), lambda i, b: (b, 0, 0))] * n_w,
        out_specs=[pl.BlockSpec((tm, D_MODEL), lambda i, b: (i, 0))] + [wide] * n_out,
        out_shape=[jax.ShapeDtypeStruct((T, D_MODEL), BF)] + [jax.ShapeDtypeStruct((T, nb * bo), BF)] * n_out,
        scratch_shapes=[pltpu.VMEM((tm, D_MODEL), BF)], name=name, compiler_params=_params(2))(*deps, h, g, *ws)


def _mm_res(name, x, w3, h, g, coef, tm=ROW_TILE):
    T, kx = x.shape
    w2 = w3.reshape(kx, D_MODEL)
    tm = _tile(T, tm)

    def body(x_ref, w_ref, h_ref, g_ref, f_ref, o_ref):
        f = jnp.dot(x_ref[...], w_ref[...], preferred_element_type=F32).astype(BF)
        f_ref[...] = f
        o_ref[...] = h_ref[...] + coef * _rms_fwd(f.astype(F32), g_ref[...])

    row = pl.BlockSpec((tm, D_MODEL), lambda i: (i, 0))
    return pl.pallas_call(
        body, grid=(T // tm,),
        in_specs=[pl.BlockSpec((tm, kx), lambda i: (i, 0)), pl.BlockSpec(w2.shape, lambda i: (0, 0)), row,
                  pl.BlockSpec(g.shape, lambda i: (0, 0))],
        out_specs=[row, row],
        out_shape=[jax.ShapeDtypeStruct((T, D_MODEL), BF), jax.ShapeDtypeStruct((T, D_MODEL), F32)],
        name=name, compiler_params=_params(1))(x, w2, h, g)


def _resbwd_mm(name, dh, f, g, coef, w3, trans_w, act=None, deps=(), tm=2 * ROW_TILE):
    T = dh.shape[0]
    nb, r, cc = w3.shape
    bo = r if trans_w else cc
    tm = _tile(T, tm)
    n_dep, n_act = len(deps), 3 if act else 0
    n_i = T // tm

    def body(*refs):
        refs = refs[n_dep:]
        dh_ref, f_ref, g_ref, w_ref = refs[:4]
        df_ref, dg_ref = refs[4 + n_act], refs[5 + n_act]
        df_s = refs[-2] if act else refs[-1]
        i, b = pl.program_id(0), pl.program_id(1)

        @pl.when(b == 0)
        def _():
            dg = jnp.zeros((1, D_MODEL), F32)
            for c in range(tm // EPI_ROWS):
                rows = slice(c * EPI_ROWS, (c + 1) * EPI_ROWS)
                dx, dg_c = _rms_bwd(f_ref[rows, :].astype(F32), g_ref[...], coef * dh_ref[rows, :])
                df_s[rows, :] = dx.astype(BF)
                df_ref[rows, :] = dx.astype(BF)
                dg = dg + dg_c
            _acc_rows(dg_ref, dg, i == 0)

        if act:
            for j in range(bo // MXU_TILE):
                cols = slice(j * MXU_TILE, (j + 1) * MXU_TILE)
                prod = lax.dot_general(df_s[...], w_ref[cols, :], _NT, preferred_element_type=F32)
                val, grad = _silu_and_grad(refs[4][:, cols].astype(F32))
                refs[6 + n_act][:, cols] = (prod * refs[5][:, cols].astype(F32) * grad).astype(BF)
                refs[7 + n_act][:, cols] = (prod * val).astype(BF)
            acc = refs[-1]
            part = lax.dot_general(refs[6][...], df_s[...], _TN, preferred_element_type=F32)

            @pl.when(i == 0)
            def _():
                acc[b] = part

            @pl.when(i != 0)
            def _():
                acc[b] += part

            @pl.when(i == n_i - 1)
            def _():
                refs[8 + n_act][...] = acc[b].astype(BF)
        else:
            refs[6][...] = lax.dot_general(df_s[...], w_ref[...], _NT if trans_w else _NN,
                                           preferred_element_type=F32).astype(BF)

    row = pl.BlockSpec((tm, D_MODEL), lambda i, b: (i, 0))
    wide = pl.BlockSpec((tm, bo), lambda i, b: (i, b))
    vec = pl.BlockSpec((1, D_MODEL), lambda i, b: (0, 0))
    out_specs = [row, vec] + [wide] * (2 if act else 1)
    out_shape = [jax.ShapeDtypeStruct((T, D_MODEL), BF), jax.ShapeDtypeStruct((1, D_MODEL), F32)]
    out_shape += [jax.ShapeDtypeStruct((T, nb * bo), BF)] * (2 if act else 1)
    scratch = [pltpu.VMEM((tm, D_MODEL), BF)]
    if act:
        out_specs.append(pl.BlockSpec((None, r, cc), lambda i, b: (jnp.where(i == n_i - 1, b, 0), 0, 0)))
        out_shape.append(jax.ShapeDtypeStruct((nb, r, cc), BF))
        scratch.append(pltpu.VMEM((nb, r, cc), F32))
    return pl.pallas_call(
        body, grid=(n_i, nb),
        in_specs=[ANY] * n_dep + [row, row, vec, pl.BlockSpec((None, r, cc), lambda i, b: (b, 0, 0))] + [wide] * n_act,
        out_specs=out_specs, out_shape=out_shape, scratch_shapes=scratch, name=name,
        compiler_params=_params(2))(*deps, dh, f, g, w3, *(act or ()))


def _dn_prenorm(name, xs, ws, trans_w, dh, h, g, tm=2 * ROW_TILE):
    T = dh.shape[0]
    chained = not isinstance(ws, (list, tuple))
    ws = [ws] if chained else list(ws)
    _, r, cc = ws[0].shape
    bw = cc if trans_w else r
    per_x = xs[0].shape[1] // bw
    nb = per_x * len(xs) if chained else per_x
    tm = _tile(T, tm)
    n_x, n_w = len(xs), len(ws)

    def body(*refs):
        x_refs, w_refs = refs[:n_x], refs[n_x:n_x + n_w]
        dh_ref, h_ref, g_ref, o_ref, dg_ref, acc = refs[n_x + n_w:]
        i, b = pl.program_id(0), pl.program_id(1)

        @pl.when(b == 0)
        def _():
            acc[...] = jnp.zeros_like(acc)

        def add(x_ref, w_ref):
            acc[...] += lax.dot_general(x_ref[...], w_ref[...], _NT if trans_w else _NN, preferred_element_type=F32)

        if chained:
            for k, x_ref in enumerate(x_refs):
                pl.when(b // per_x == k)(lambda x_ref=x_ref: add(x_ref, w_refs[0]))
        else:
            for x_ref, w_ref in zip(x_refs, w_refs):
                add(x_ref, w_ref)

        @pl.when(b == nb - 1)
        def _():
            dg = jnp.zeros((1, D_MODEL), F32)
            for c in range(tm // EPI_ROWS):
                rows = slice(c * EPI_ROWS, (c + 1) * EPI_ROWS)
                dx, dg_c = _rms_bwd(h_ref[rows, :], g_ref[...], acc[rows, :])
                o_ref[rows, :] = dh_ref[rows, :] + dx
                dg = dg + dg_c
            _acc_rows(dg_ref, dg, i == 0)

    row = pl.BlockSpec((tm, D_MODEL), lambda i, b: (i, 0))
    vec = pl.BlockSpec((1, D_MODEL), lambda i, b: (0, 0))
    if chained:
        x_specs = [pl.BlockSpec((tm, bw), lambda i, b, k=k: (i, jnp.clip(b - k * per_x, 0, per_x - 1)))
                   for k in range(n_x)]
    else:
        x_specs = [pl.BlockSpec((tm, bw), lambda i, b: (i, b))] * n_x
    return pl.pallas_call(
        body, grid=(T // tm, nb),
        in_specs=x_specs + [pl.BlockSpec((None, r, cc), lambda i, b: (b, 0, 0))] * n_w + [row, row, vec],
        out_specs=[row, vec],
        out_shape=[jax.ShapeDtypeStruct((T, D_MODEL), F32), jax.ShapeDtypeStruct((1, D_MODEL), F32)],
        scratch_shapes=[pltpu.VMEM((tm, D_MODEL), F32)], name=name,
        compiler_params=_params(2))(*xs, *ws, dh, h, g)


def _pool_apply(x, win, row):
    s, k = x, 1
    while k < win:
        s = s + jnp.where(row >= k, pltpu.roll(s, k, 0), 0.0)
        k *= 2
    return s / jnp.minimum(row + 1, win).astype(F32) - x


def _pool_apply_t(dp, win, row):
    T = dp.shape[0]
    s, k = dp / jnp.minimum(row + 1, win).astype(F32), 1
    while k < win:
        s = s + jnp.where(row < T - k, pltpu.roll(s, T - k, 0), 0.0)
        k *= 2
    return s - dp


def _pool_fwd(z, w, scale):
    T = z.shape[0]

    def body(z_ref, w_ref, s_ref, o_ref):
        row = lax.broadcasted_iota(jnp.int32, (T, LANES), 0)
        for gi, win in enumerate(POOL_WINDOWS):
            cols = pl.ds(gi * LANES, LANES)
            pooled = _pool_apply(z_ref[:, cols].astype(F32), win, row)
            y = jnp.dot(pooled.astype(BF), w_ref[gi].astype(BF), preferred_element_type=F32)
            o_ref[:, cols] = (y * s_ref[:, cols]).astype(o_ref.dtype)

    return pl.pallas_call(
        body, grid=(1,),
        in_specs=[pl.BlockSpec((T, 512), lambda i: (0, ZB_POOL)), pl.BlockSpec(w.shape, lambda i: (0, 0, 0)),
                  pl.BlockSpec(scale.shape, lambda i: (0, 0))],
        out_specs=pl.BlockSpec((T, 512), lambda i: (0, 0)), out_shape=jax.ShapeDtypeStruct((T, 512), BF),
        name="pool_fwd", compiler_params=_params(1))(z, w, scale)


def _pool_bwd(dr, z, w, scale):
    T = z.shape[0]

    def body(dr_ref, z_ref, w_ref, s_ref, dz_ref, dw_ref, ds_ref):
        row = lax.broadcasted_iota(jnp.int32, (T, LANES), 0)
        for gi, win in enumerate(POOL_WINDOWS):
            cols = pl.ds(gi * LANES, LANES)
            pooled = _pool_apply(z_ref[:, cols].astype(F32), win, row).astype(BF)
            wg = w_ref[gi].astype(BF)
            y = jnp.dot(pooled, wg, preferred_element_type=F32)
            d = dr_ref[:, cols].astype(F32)
            ds_ref[:, cols] = jnp.sum(d * y, axis=0, keepdims=True)
            dy = (d * s_ref[:, cols]).astype(BF)
            dw_ref[gi] = lax.dot_general(pooled, dy, _TN, preferred_element_type=F32)
            dpooled = lax.dot_general(dy, wg, _NT, preferred_element_type=F32)
            dz_ref[:, cols] = _pool_apply_t(dpooled, win, row).astype(dz_ref.dtype)

    return pl.pallas_call(
        body, grid=(1,),
        in_specs=[pl.BlockSpec((T, 512), lambda i: (0, 0)), pl.BlockSpec((T, 512), lambda i: (0, ZB_POOL)),
                  pl.BlockSpec(w.shape, lambda i: (0, 0, 0)), pl.BlockSpec(scale.shape, lambda i: (0, 0))],
        out_specs=[pl.BlockSpec((T, 512), lambda i: (0, 0)), pl.BlockSpec(w.shape, lambda i: (0, 0, 0)),
                   pl.BlockSpec(scale.shape, lambda i: (0, 0))],
        out_shape=[jax.ShapeDtypeStruct((T, 512), BF), jax.ShapeDtypeStruct(w.shape, F32),
                   jax.ShapeDtypeStruct(scale.shape, F32)],
        name="pool_bwd", compiler_params=_params(1))(dr, z, w, scale)


def _tril(transposed=False):
    r = lax.broadcasted_iota(jnp.int32, (CHUNK, CHUNK), 0)
    c = lax.broadcasted_iota(jnp.int32, (CHUNK, CHUNK), 1)
    return c >= r if transposed else r >= c


def _sgu_fwd(z, ln_g, ln_b, w_s, bias):
    T = z.shape[0]
    tm = _tile(T)

    def body(zu_ref, zv_ref, g_ref, b_ref, w_ref, bias_ref, o_ref):
        gu, _ = _gelu_and_grad(zu_ref[...].astype(F32))
        gv, _ = _gelu_and_grad(zv_ref[...].astype(F32))
        xh, _ = _ln_stats(gv)
        v16 = (xh * g_ref[...] + b_ref[...]).astype(BF)
        tri = _tril()
        for h in range(SGU_HEADS):
            cols = slice(h * LANES, (h + 1) * LANES)
            wh = jnp.where(tri, w_ref[h], 0.0).astype(BF)
            for c in range(tm // CHUNK):
                rows = slice(c * CHUNK, (c + 1) * CHUNK)
                s = jnp.dot(wh, v16[rows, cols], preferred_element_type=F32) + bias_ref[:, cols]
                o_ref[rows, cols] = (gu[rows, cols] * s).astype(o_ref.dtype)

    small = [pl.BlockSpec(a.shape, lambda i, n=a.ndim: (0,) * n) for a in (ln_g, ln_b, w_s, bias)]
    return pl.pallas_call(
        body, grid=(T // tm,),
        in_specs=[pl.BlockSpec((tm, 512), lambda i: (i, ZB_U)), pl.BlockSpec((tm, 512), lambda i: (i, ZB_V))] + small,
        out_specs=pl.BlockSpec((tm, 512), lambda i: (i, 0)), out_shape=jax.ShapeDtypeStruct((T, 512), BF),
        name="sgu_fwd", compiler_params=_params(1))(z, z, ln_g, ln_b, w_s, bias)


def _sgu_bwd(dr, z, ln_g, ln_b, w_s, w_st, bias):
    T = z.shape[0]
    tm = _tile(T)
    n_steps = T // tm

    def body(dr_ref, zu_ref, zv_ref, g_ref, b_ref, w_ref, wt_ref, bias_ref,
             dzu_ref, dzv_ref, dg_ref, db_ref, dw_ref, dbias_ref, dgu_s, dv_s):
        i = pl.program_id(0)

        @pl.when(i == 0)
        def _():
            dg_ref[...] = jnp.zeros_like(dg_ref)
            db_ref[...] = jnp.zeros_like(db_ref)
            dw_ref[...] = jnp.zeros_like(dw_ref)
            dbias_ref[...] = jnp.zeros_like(dbias_ref)

        zu = zu_ref[...].astype(F32)
        zv = zv_ref[...].astype(F32)
        gu, gu_grad = _gelu_and_grad(zu)
        gv, gv_grad = _gelu_and_grad(zv)
        xh, r = _ln_stats(gv)
        v16 = (xh * g_ref[...] + b_ref[...]).astype(BF)
        dr = dr_ref[...].astype(F32)
        tri = _tril()
        for h in range(SGU_HEADS):
            cols = slice(h * LANES, (h + 1) * LANES)
            wh = jnp.where(tri, w_ref[h], 0.0).astype(BF)
            wht = jnp.where(_tril(transposed=True), wt_ref[h], 0.0).astype(BF)
            for c in range(tm // CHUNK):
                rows = slice(c * CHUNK, (c + 1) * CHUNK)
                v_blk = v16[rows, cols]
                s = jnp.dot(wh, v_blk, preferred_element_type=F32) + bias_ref[:, cols]
                ds = dr[rows, cols] * gu[rows, cols]
                dgu_s[rows, cols] = dr[rows, cols] * s
                ds16 = ds.astype(BF)
                dw_ref[h] += jnp.where(tri, lax.dot_general(ds16, v_blk, _NT, preferred_element_type=F32), 0.0)
                dv_s[rows, cols] = jnp.dot(wht, ds16, preferred_element_type=F32)
                dbias_ref[:, cols] += ds
        dzu_ref[...] = (dgu_s[...] * gu_grad).astype(dzu_ref.dtype)
        dgv, dg, db = _ln_bwd(xh, r, g_ref[...], dv_s[...])
        dzv_ref[...] = (dgv * gv_grad).astype(dzv_ref.dtype)
        dg_ref[...] += dg
        db_ref[...] += db

        @pl.when(i == n_steps - 1)
        def _():
            for h in range(SGU_HEADS):
                cols = slice(h * LANES, (h + 1) * LANES)
                tot = jnp.sum(dbias_ref[:, cols], axis=1, keepdims=True)
                dbias_ref[:, cols] = jnp.broadcast_to(tot, (CHUNK, LANES))

    small = (ln_g, ln_b, w_s, w_st, bias)
    small_specs = [pl.BlockSpec(a.shape, lambda i, n=a.ndim: (0,) * n) for a in small]
    return pl.pallas_call(
        body, grid=(n_steps,),
        in_specs=[pl.BlockSpec((tm, 512), lambda i: (i, 0)), pl.BlockSpec((tm, 512), lambda i: (i, ZB_U)),
                  pl.BlockSpec((tm, 512), lambda i: (i, ZB_V))] + small_specs,
        out_specs=[pl.BlockSpec((tm, 512), lambda i: (i, 0)), pl.BlockSpec((tm, 512), lambda i: (i, 0)),
                   pl.BlockSpec((1, 512), lambda i: (0, 0)), pl.BlockSpec((1, 512), lambda i: (0, 0)),
                   pl.BlockSpec(w_s.shape, lambda i: (0, 0, 0)), pl.BlockSpec(bias.shape, lambda i: (0, 0))],
        out_shape=[jax.ShapeDtypeStruct((T, 512), BF), jax.ShapeDtypeStruct((T, 512), BF),
                   jax.ShapeDtypeStruct((1, 512), F32), jax.ShapeDtypeStruct((1, 512), F32),
                   jax.ShapeDtypeStruct(w_s.shape, F32), jax.ShapeDtypeStruct(bias.shape, F32)],
        scratch_shapes=[pltpu.VMEM((tm, 512), F32), pltpu.VMEM((tm, 512), F32)],
        name="sgu_bwd", compiler_params=_params(1))(dr, z, z, ln_g, ln_b, w_s, w_st, bias)


def _conv_fwd(z, convk, l, bias):
    T = z.shape[0]

    def body(za_ref, zb_ref, k_ref, b_ref, o_ref):
        xg = za_ref[...].astype(F32) * _sigmoid(zb_ref[...].astype(F32))
        xp = jnp.concatenate([jnp.zeros((CONV_PAD, LANES), F32), xg], axis=0)
        kw = k_ref[...]
        acc = jnp.broadcast_to(b_ref[...], (T, LANES))
        for s in range(SUBLANES):
            xs = xp if s == 0 else pltpu.roll(xp, s, 0)
            for q in range(CONV_PAD // SUBLANES):
                k = CONV_TAPS - 1 - (SUBLANES * q + s)
                if k >= 0:
                    lo = CONV_PAD - SUBLANES * q
                    acc = acc + kw[k:k + 1, :] * xs[lo:lo + T, :]
        o_ref[...] = acc.astype(o_ref.dtype)

    return pl.pallas_call(
        body, grid=(4,),
        in_specs=[pl.BlockSpec((T, LANES), lambda g: (0, 4 * ZB_A + g)),
                  pl.BlockSpec((T, LANES), lambda g: (0, 4 * ZB_B + g)),
                  pl.BlockSpec((None, CONV_PAD, LANES), lambda g: (g, 0, 0)),
                  pl.BlockSpec((1, LANES), lambda g: (0, g))],
        out_specs=pl.BlockSpec((T, LANES), lambda g: (0, g)), out_shape=jax.ShapeDtypeStruct((T, 512), BF),
        name="conv_fwd", compiler_params=_params(1))(z, z, convk[l], bias)


def _conv_bwd(dy, z, convk, l):
    T = z.shape[0]

    def body(dy_ref, za_ref, zb_ref, k_ref, dza_ref, dzb_ref, dk_ref, db_ref):
        a = za_ref[...].astype(F32)
        sg = _sigmoid(zb_ref[...].astype(F32))
        d = dy_ref[...].astype(F32)
        kw = k_ref[...]
        xp = jnp.concatenate([jnp.zeros((CONV_PAD, LANES), F32), a * sg], axis=0)
        dp = jnp.concatenate([d, jnp.zeros((CONV_PAD, LANES), F32)], axis=0)
        dxg = jnp.zeros((T, LANES), F32)
        dk_ref[...] = jnp.zeros_like(dk_ref)
        for s in range(SUBLANES):
            xs = xp if s == 0 else pltpu.roll(xp, s, 0)
            ds = dp if s == 0 else pltpu.roll(dp, T + CONV_PAD - s, 0)
            for q in range(CONV_PAD // SUBLANES):
                k = CONV_TAPS - 1 - (SUBLANES * q + s)
                if k >= 0:
                    lo = CONV_PAD - SUBLANES * q
                    dk_ref[k:k + 1, :] = jnp.sum(d * xs[lo:lo + T, :], axis=0, keepdims=True)
                    dxg = dxg + kw[k:k + 1, :] * ds[SUBLANES * q:SUBLANES * q + T, :]
        db_ref[...] = jnp.sum(d, axis=0, keepdims=True)
        dza_ref[...] = (dxg * sg).astype(dza_ref.dtype)
        dzb_ref[...] = (dxg * a * sg * (1.0 - sg)).astype(dzb_ref.dtype)

    col = pl.BlockSpec((T, LANES), lambda g: (0, g))
    return pl.pallas_call(
        body, grid=(4,),
        in_specs=[col, pl.BlockSpec((T, LANES), lambda g: (0, 4 * ZB_A + g)),
                  pl.BlockSpec((T, LANES), lambda g: (0, 4 * ZB_B + g)),
                  pl.BlockSpec((None, CONV_PAD, LANES), lambda g: (g, 0, 0))],
        out_specs=[col, col, pl.BlockSpec((CONV_PAD, LANES), lambda g: (0, g)),
                   pl.BlockSpec((1, LANES), lambda g: (0, g))],
        out_shape=[jax.ShapeDtypeStruct((T, 512), BF), jax.ShapeDtypeStruct((T, 512), BF),
                   jax.ShapeDtypeStruct((CONV_PAD, 512), F32), jax.ShapeDtypeStruct((1, 512), F32)],
        name="conv_bwd", compiler_params=_params(1))(dy, z, z, convk[l])


D = D_MODEL


def _ffn_fwd(l, h, S, W, pre, deps=()):
    n, gp, u, a = _norm_mm("ffn_in", h, S[pre + "_pre_g"], [W[pre + "_w_gate"][l], W[pre + "_w_up"][l]], True,
                           act=True, deps=deps)
    f, out = _mm_res("ffn_out", a, W[pre + "_w_down"][l], h, S[pre + "_post_g"], 0.5)
    return out, dict(h=h, n=n, gp=gp, u=u, a=a, f=f)


def _ffn_bwd(l, dh, sv, S, W, G, SG, pre, deps=()):
    df, SG[pre + "_post_g"], dgp, du, dwd = _resbwd_mm(
        "ffn_bwd_act", dh, sv["f"], S[pre + "_post_g"], 0.5, W[pre + "_w_down"][l], True,
        act=(sv["gp"], sv["u"], sv["a"]), deps=deps, tm=ROW_TILE)
    G[pre + "_w_down"] = G[pre + "_w_down"][:l] + [dwd] + G[pre + "_w_down"][l + 1:]
    G[pre + "_w_gate"] = _mm_tn("ffn_dw_gate", dgp, sv["n"], G[pre + "_w_gate"], l, True)
    G[pre + "_w_up"] = _mm_tn("ffn_dw_up", du, sv["n"], G[pre + "_w_up"], l, True)
    dh_in, SG[pre + "_pre_g"] = _dn_prenorm("ffn_bwd_in", [dgp, du], [W[pre + "_w_gate"][l], W[pre + "_w_up"][l]],
                                            False, dh, sv["h"], S[pre + "_pre_g"])
    return dh_in


def _gates(zg):
    return [_sigmoid(jnp.concatenate([zg[2 * k].astype(F32), zg[2 * k + 1].astype(F32)], axis=1)) for k in range(3)]


def _merge_fwd(z, rs, ws, tm=ROW_TILE):
    T = z.shape[0]
    tm = _tile(T, tm)
    nb, kk, bw = ws[0].shape

    def body(*refs):
        r_refs, g_refs, w_refs, y_refs, m_ref = refs[:3], refs[3:9], refs[9:12], refs[12:15], refs[15]
        for r_ref, w_ref, y_ref in zip(r_refs, w_refs, y_refs):
            for b in range(nb):
                y_ref[:, b * bw:(b + 1) * bw] = jnp.dot(r_ref[...], w_ref[b],
                                                        preferred_element_type=F32).astype(y_ref.dtype)
        g = _gates([q[...] for q in g_refs])
        m_ref[...] = (g[0] * y_refs[0][...].astype(F32) + g[1] * y_refs[1][...].astype(F32)
                      + g[2] * y_refs[2][...].astype(F32)).astype(m_ref.dtype)

    row = pl.BlockSpec((tm, D_MODEL), lambda i: (i, 0))
    return pl.pallas_call(
        body, grid=(T // tm,),
        in_specs=[pl.BlockSpec((tm, kk), lambda i: (i, 0))] * 3
        + [pl.BlockSpec((tm, 512), lambda i, j=j: (i, ZB_GATES + j)) for j in range(6)]
        + [pl.BlockSpec(ws[0].shape, lambda i: (0, 0, 0))] * 3,
        out_specs=[row] * 4, out_shape=[jax.ShapeDtypeStruct((T, D_MODEL), BF)] * 4,
        name="mix_merge", compiler_params=_params(1))(*rs, *[z] * 6, *ws)


def _merge_bwd(dmerged, z, ys, ws, tm=ROW_TILE // 2):
    T = z.shape[0]
    tm = _tile(T, tm)
    nb, kk, bw = ws[0].shape

    def body(*refs):
        dm_ref, g_refs, y_refs, w_refs = refs[0], refs[1:7], refs[7:10], refs[10:13]
        dy_refs, lo_ref, hi_ref, dr_refs = refs[13:16], refs[16], refs[17], refs[18:21]
        cut = DZ_HALF - ZB_GATES * 512
        dm = dm_ref[...].astype(F32)
        g = _gates([q[...] for q in g_refs])
        for k in range(3):
            dy_refs[k][...] = (dm * g[k]).astype(BF)
            dzg = (dm * y_refs[k][...].astype(F32) * g[k] * (1.0 - g[k])).astype(BF)
            if k == 0:
                lo_ref[...] = dzg[:, :cut]
                hi_ref[:, :D_MODEL - cut] = dzg[:, cut:]
            else:
                hi_ref[:, k * D_MODEL - cut:(k + 1) * D_MODEL - cut] = dzg
            dr = None
            for b in range(nb):
                p = lax.dot_general(dy_refs[k][:, b * bw:(b + 1) * bw], w_refs[k][b], _NT,
                                    preferred_element_type=F32)
                dr = p if dr is None else dr + p
            dr_refs[k][...] = dr.astype(BF)

    row = pl.BlockSpec((tm, D_MODEL), lambda i: (i, 0))
    return pl.pallas_call(
        body, grid=(T // tm,),
        in_specs=[row] + [pl.BlockSpec((tm, 512), lambda i, j=j: (i, ZB_GATES + j)) for j in range(6)] + [row] * 3
        + [pl.BlockSpec(ws[0].shape, lambda i: (0, 0, 0))] * 3,
        out_specs=[row] * 3 + [pl.BlockSpec((tm, DZ_HALF - ZB_GATES * 512), lambda i: (i, 0)),
                               pl.BlockSpec((tm, DZ_HALF), lambda i: (i, 0))]
        + [pl.BlockSpec((tm, kk), lambda i: (i, 0))] * 3,
        out_shape=[jax.ShapeDtypeStruct((T, D_MODEL), BF)] * 3
        + [jax.ShapeDtypeStruct((T, DZ_HALF - ZB_GATES * 512), BF), jax.ShapeDtypeStruct((T, DZ_HALF), BF)]
        + [jax.ShapeDtypeStruct((T, kk), BF)] * 3,
        name="mix_merge_bwd", compiler_params=_params(1))(dmerged, *[z] * 6, *ys, *ws)


def _mix_fwd(l, h, S, W, deps=()):
    n, z = _norm_mm("mix_in", h, S["mix_pre_g"], [W["w_in"][l]], False, deps=deps)
    r_pool = _pool_fwd(z, S["pool_w"], S["pool_scale"])
    r_sgu = _sgu_fwd(z, S["sgu_ln_g"], S["sgu_ln_b"], S["sgu_w_s"], S["sgu_bias"])
    yc = _conv_fwd(z, W["conv_dw_k"], l, S["conv_dw_b"])

    def ln_silu(y, g, b):
        xh, _ = _ln_stats(y.astype(F32))
        return (_silu_and_grad(xh * g + b)[0],)

    r_conv = _rowwise("conv_ln", ln_silu, [(yc, 512, 0)], [S["conv_ln_g"], S["conv_ln_b"]], [(512, BF)])[0]
    y_pool, y_sgu, y_conv, merged = _merge_fwd(z, (r_pool, r_sgu, r_conv),
                                               [W["w_%s_out" % br][l] for br in ("pool", "sgu", "conv")])
    o, out = _mm_res("mix_out", merged, W["w_out"][l], h, S["mix_post_g"], 1.0)
    return out, dict(h=h, n=n, z=z, r_pool=r_pool, r_sgu=r_sgu, yc=yc, r_conv=r_conv, y_pool=y_pool, y_sgu=y_sgu,
                     y_conv=y_conv, merged=merged, o=o)


def _mix_bwd(l, dh, sv, S, W, G, SG, deps=()):
    z = sv["z"]
    do, SG["mix_post_g"], dmerged = _resbwd_mm("mix_bwd_out", dh, sv["o"], S["mix_post_g"], 1.0,
                                               W["w_out"][l].reshape(1, D, D), True, deps=deps)
    G["w_out"] = _mm_tn("mix_dw_out", sv["merged"], do, G["w_out"], l, True)

    branches = ("pool", "sgu", "conv")
    res = _merge_bwd(dmerged, z, [sv["y_" + br] for br in branches], [W["w_%s_out" % br][l] for br in branches])
    dz_gate_lo, dz_hi, dr = res[3], res[4], dict(zip(branches, res[5:]))
    for br, dy in zip(branches, res[:3]):
        wn = "w_%s_out" % br
        G[wn] = _mm_tn("branch_dw", sv["r_" + br], dy, G[wn], l, False)
    dz_pool, SG["pool_w"], SG["pool_scale"] = _pool_bwd(dr["pool"], z, S["pool_w"], S["pool_scale"])
    dzu, dzv, SG["sgu_ln_g"], SG["sgu_ln_b"], SG["sgu_w_s"], dbias = _sgu_bwd(
        dr["sgu"], z, S["sgu_ln_g"], S["sgu_ln_b"], S["sgu_w_s"], S["sgu_w_st"], S["sgu_bias"])
    SG["sgu_b_s"] = dbias[:, ::LANES].T

    def ln_silu_bwd(d, y, g, b):
        xh, r = _ln_stats(y.astype(F32))
        _, grad = _silu_and_grad(xh * g + b)
        return _ln_bwd(xh, r, g, d.astype(F32) * grad)

    dyc, SG["conv_ln_g"], SG["conv_ln_b"] = _rowwise(
        "conv_ln_bwd", ln_silu_bwd, [(dr["conv"], 512, 0), (sv["yc"], 512, 0)], [S["conv_ln_g"], S["conv_ln_b"]],
        [(512, BF)], [(1, 512), (1, 512)])
    dza, dzb, SG["conv_dw_k"], SG["conv_dw_b"] = _conv_bwd(dyc, z, W["conv_dw_k"], l)
    dz_lo = jnp.concatenate([dz_pool, dzu, dzv, dza, dzb, dz_gate_lo], axis=1)
    G["w_in"] = _mm_tn("mix_dw_in", sv["n"], dz_lo, G["w_in"], l, False)
    G["w_in"] = _mm_tn("mix_dw_in", sv["n"], dz_hi, G["w_in"], l, False, first=2)
    dh_in, SG["mix_pre_g"] = _dn_prenorm("mix_bwd_in", [dz_lo, dz_hi], W["w_in"][l], True, dh, sv["h"],
                                         S["mix_pre_g"])
    return dh_in


def _ple_out(h, p, gp, w3, g, tm=ROW_TILE):
    T, kp = p.shape
    nb, _, bw = w3.shape
    tm = _tile(T, tm)

    def body(h_ref, p_ref, gp_ref, w_ref, g_ref, e_ref, o_ref):
        p16 = p_ref[...].astype(BF)
        for b in range(nb):
            e_ref[:, b * bw:(b + 1) * bw] = jnp.dot(p16, w_ref[b], preferred_element_type=F32).astype(BF)
        q = _sigmoid(gp_ref[...].astype(F32)) * e_ref[...].astype(F32)
        o_ref[...] = h_ref[...] + _rms_fwd(q, g_ref[...])

    row = pl.BlockSpec((tm, D_MODEL), lambda i: (i, 0))
    return pl.pallas_call(
        body, grid=(T // tm,),
        in_specs=[row, pl.BlockSpec((tm, kp), lambda i: (i, 0)), row, pl.BlockSpec(w3.shape, lambda i: (0, 0, 0)),
                  pl.BlockSpec(g.shape, lambda i: (0, 0))],
        out_specs=[row, row],
        out_shape=[jax.ShapeDtypeStruct((T, D_MODEL), BF), jax.ShapeDtypeStruct((T, D_MODEL), F32)],
        name="ple_out", compiler_params=_params(1))(h, p, gp, w3, g)


def _ple_fwd(l, h, p_l, S, W, deps=()):
    n, gp = _norm_mm("ple_in", h, S["ple_pre_g"], [W["ple_w_gate"][l].reshape(1, D, D)], False, deps=deps)
    e, out = _ple_out(h, p_l, gp, W["ple_w_proj"][l], S["ple_post_g"])
    return out, dict(h=h, n=n, e=e, gp=gp, p=p_l)


def _ple_bwd_rows(dh, e, gp, g_post, w3, h, g_pre, deps=(), tm=ROW_TILE):
    T = dh.shape[0]
    w2 = w3.reshape(D_MODEL, D_MODEL)
    tm = _tile(T, tm)
    n_dep = len(deps)

    def body(*refs):
        dh_ref, e_ref, gp_ref, gpost_ref, w_ref, h_ref, gpre_ref, de_ref, dgp_ref, o_ref, dpost_ref, dpre_ref = \
            refs[n_dep:]
        first = pl.program_id(0) == 0
        d = dh_ref[...]
        sg = _sigmoid(gp_ref[...].astype(F32))
        ee = e_ref[...].astype(F32)
        dq, dpost = _rms_bwd(sg * ee, gpost_ref[...], d)
        de_ref[...] = (dq * sg).astype(BF)
        dgp = (dq * ee * sg * (1.0 - sg)).astype(BF)
        dgp_ref[...] = dgp
        dn = lax.dot_general(dgp, w_ref[...], _NT, preferred_element_type=F32)
        dx, dpre = _rms_bwd(h_ref[...], gpre_ref[...], dn)
        o_ref[...] = d + dx
        _acc_rows(dpost_ref, dpost, first)
        _acc_rows(dpre_ref, dpre, first)

    row = pl.BlockSpec((tm, D_MODEL), lambda i: (i, 0))
    vec = pl.BlockSpec((1, D_MODEL), lambda i: (0, 0))
    return pl.pallas_call(
        body, grid=(T // tm,),
        in_specs=[ANY] * n_dep + [row, row, row, vec, pl.BlockSpec(w2.shape, lambda i: (0, 0)), row, vec],
        out_specs=[row, row, row, vec, vec],
        out_shape=[jax.ShapeDtypeStruct((T, D_MODEL), BF)] * 2 + [jax.ShapeDtypeStruct((T, D_MODEL), F32)]
        + [jax.ShapeDtypeStruct((1, D_MODEL), F32)] * 2,
        name="ple_bwd", compiler_params=_params(1))(*deps, dh, e, gp, g_post, w2, h, g_pre)


def _ple_bwd(l, dh, sv, S, W, G, SG, deps=()):
    de, dgp, dh_in, SG["ple_post_g"], SG["ple_pre_g"] = _ple_bwd_rows(
        dh, sv["e"], sv["gp"], S["ple_post_g"], W["ple_w_gate"][l], sv["h"], S["ple_pre_g"], deps)
    G["ple_w_proj"] = _mm_tn("ple_dw_proj", sv["p"], de, G["ple_w_proj"], l, False)
    G["ple_w_gate"] = _mm_tn("ple_dw_gate", sv["n"], dgp, G["ple_w_gate"], l, True)
    return dh_in


def _layer_small(a, l):
    S = {}
    for name in SMALL:
        v = a[name][l]
        S[name] = v.reshape(1, -1) if v.ndim == 1 else v
    S["sgu_w_st"] = jnp.swapaxes(S["sgu_w_s"], 1, 2)
    S["sgu_bias"] = jnp.repeat(S["sgu_b_s"].T, LANES, axis=1)
    return S


def _layer_fwd(l, h, p_l, S, W, deps=(), hooks=None):
    hooks = hooks or {}

    def after(part, hv):
        return hooks[part](hv) if part in hooks else ()

    h, sv1 = _ffn_fwd(l, h, S, W, "ffn1", deps)
    h, sv2 = _mix_fwd(l, h, S, W, after("ffn1", h))
    h, sv3 = _ffn_fwd(l, h, S, W, "ffn2", after("mix", h))
    h, sv4 = _ple_fwd(l, h, p_l, S, W, after("ffn2", h))
    return h, (sv1, sv2, sv3, sv4)


def _layer_bwd(l, dh, sv, S, W, G, deps=(), mid=None):
    SG = {}
    dh = _ple_bwd(l, dh, sv[3], S, W, G, SG, deps)
    dh = _ffn_bwd(l, dh, sv[2], S, W, G, SG, "ffn2")
    dh = _mix_bwd(l, dh, sv[1], S, W, G, SG, mid(dh) if mid else ())
    dh = _ffn_bwd(l, dh, sv[0], S, W, G, SG, "ffn1")
    return dh, SG


HBM = pl.BlockSpec(memory_space=pltpu.HBM)
SEM = pl.BlockSpec(memory_space=pltpu.SEMAPHORE)
SIDE_EFFECT = pltpu.SideEffectType.DATAFLOW_SIDE_EFFECTING


def _place():
    x, y, c = lax.axis_index("x"), lax.axis_index("y"), lax.axis_index("c")
    chips = [(1 - x, y), (x, 1 - y), (1 - x, 1 - y)]
    return x, y, c, chips


def _remote(src, dst, send_sem, recv_sem, to):
    return pltpu.make_async_remote_copy(src_ref=src, dst_ref=dst, send_sem=send_sem, recv_sem=recv_sem,
                                        device_id=to, device_id_type=MESH)


def _split_start(name, plan, bufs, deps):
    count, fn = plan
    n, nd = len(bufs), len(deps)

    def body(*refs):
        send, recv = refs[nd + n], refs[nd + n + 1]
        x, y, c, chips = _place()
        for k, (src, dst, _, to) in enumerate(fn(refs[nd:nd + n], x, y, c, chips)):
            _remote(src, dst, send.at[k], recv.at[k], to).start()
        refs[-1][...] = jnp.zeros_like(refs[-1])

    res = pl.pallas_call(
        body, in_specs=[ANY] * nd + [HBM] * n,
        out_specs=[SEM, SEM] + [HBM] * n + [pl.BlockSpec(memory_space=pltpu.VMEM)],
        out_shape=[pltpu.SemaphoreType.DMA((count,)), pltpu.SemaphoreType.DMA((count,))]
        + [pltpu.HBM(b.shape, b.dtype) for b in bufs] + [jax.ShapeDtypeStruct((8, LANES), F32)],
        input_output_aliases={nd + i: 2 + i for i in range(n)}, name=name,
        compiler_params=pltpu.CompilerParams(has_side_effects=SIDE_EFFECT),
    )(*deps, *[pltpu.with_memory_space_constraint(b, pltpu.HBM) for b in bufs])
    return (res[0], res[1]), list(res[2:2 + n]), res[-1]


def _split_wait(name, plan, sems, bufs, after):
    _, fn = plan
    n = len(bufs)

    def body(*refs):
        send, recv = refs[n], refs[n + 1]
        x, y, c, chips = _place()
        for k, (src, _, land, to) in enumerate(fn(refs[:n], x, y, c, chips)):
            cp = _remote(src, land, send.at[k], recv.at[k], to)
            cp.wait_send()
            cp.wait_recv()

    res = pl.pallas_call(
        body, in_specs=[HBM] * n + [SEM, SEM] + [ANY] * len(after), out_specs=[HBM] * n,
        out_shape=[pltpu.HBM(b.shape, b.dtype) for b in bufs], input_output_aliases={i: i for i in range(n)},
        name=name, compiler_params=pltpu.CompilerParams(has_side_effects=SIDE_EFFECT))(*bufs, *sems, *after)
    return list(res)


def _gather_plans(n):
    def across(b, x, y, c, chips):
        me, out = 2 * x + y, []
        for a in range(n):
            rh = b[a].shape[1] // 2
            mine = b[a].at[me, pl.ds(c * rh, rh)]
            for cx, cy in chips:
                out.append((mine, mine, b[a].at[2 * cx + cy, pl.ds(c * rh, rh)], (cx, cy, c)))
        return out

    def to_sibling(b, x, y, c, chips):
        out = []
        for a in range(n):
            rh = b[a].shape[1] // 2
            for cx, cy in chips:
                piece = b[a].at[2 * cx + cy, pl.ds(c * rh, rh)]
                out.append((piece, piece, b[a].at[2 * cx + cy, pl.ds((1 - c) * rh, rh)], (x, y, 1 - c)))
        return out

    return (3 * n, across), (3 * n, to_sibling)


def _pair_plan(n):
    def fn(b, x, y, c, chips):
        out = []
        for a in range(n):
            rh = b[a].shape[1] // 2
            out.append((b[a].at[:, pl.ds((1 - c) * rh, rh)], b[n + a], b[n + a], (x, y, 1 - c)))
        return out

    return n, fn


def _cross_plan(n):
    def fn(b, x, y, c, chips):
        out = []
        for a in range(n):
            for j, (cx, cy) in enumerate(chips):
                out.append((b[a].at[2 * cx + cy], b[n + a].at[j], b[n + a].at[j], (cx, cy, c)))
        return out

    return 3 * n, fn


def _share_plan(n, l):
    def fn(b, x, y, c, chips):
        out = []
        for a in range(n):
            rh = b[a].shape[1] // 2
            mine = b[a].at[l, pl.ds(c * rh, rh)]
            out.append((mine, mine, b[a].at[l, pl.ds((1 - c) * rh, rh)], (x, y, 1 - c)))
        return out

    return n, fn


def _exchange_call(name, body, ins, out_shapes, n_remote, in_place=False):
    scratch = [pltpu.SemaphoreType.DMA((n_remote,)), pltpu.SemaphoreType.DMA((n_remote,))]
    aliases = {i: i for i in range(len(ins))} if in_place else {}
    return pl.pallas_call(body, in_specs=[ANY] * len(ins), out_specs=[ANY] * len(out_shapes), out_shape=out_shapes,
                          scratch_shapes=scratch, input_output_aliases=aliases, name=name)(*ins)


def _peers(x, y, c):
    return [(1 - x if m & 4 else x, 1 - y if m & 2 else y, 1 - c if m & 1 else c) for m in range(1, 8)]


def _scatter_small(v3):
    def body(v_ref, o_ref, send, recv):
        x, y, c, _ = _place()
        cps = []
        for m, (px, py, pc) in enumerate(_peers(x, y, c)):
            cps.append(_remote(v_ref.at[4 * px + 2 * py + pc], o_ref.at[m], send.at[m], recv.at[m], (px, py, pc)))
            cps[-1].start()
        for cp in cps:
            cp.wait()

    return _exchange_call("scatter_small", body, [v3], [jax.ShapeDtypeStruct((7,) + v3.shape[1:], v3.dtype)], 7)[0]


def _gather_small(buf):
    def body(_, o_ref, send, recv):
        x, y, c, _ = _place()
        mine = o_ref.at[4 * x + 2 * y + c]
        peers = _peers(x, y, c)
        cps = []
        for m, to in enumerate(peers):
            cps.append(_remote(mine, mine, send.at[m], recv.at[m], to))
            cps[-1].start()
        for m, (px, py, pc) in enumerate(peers):
            slab = o_ref.at[4 * px + 2 * py + pc]
            _remote(slab, slab, send.at[m], recv.at[m], (px, py, pc)).wait_recv()
            cps[m].wait_send()

    return _exchange_call("gather_small", body, [buf], [jax.ShapeDtypeStruct(buf.shape, buf.dtype)], 7,
                          in_place=True)[0]


def _allreduce_small(v, pos):
    rows = v.shape[0]
    rs = rows // 8
    v3 = v.reshape(8, rs, LANES)
    got = _scatter_small(v3)
    tm = _tile(rs)
    ins = [(v3, (None, tm, LANES), lambda i, p: (p[2], i, 0))]
    ins += [(got, (None, tm, LANES), lambda i, p, m=m: (m, i, 0)) for m in range(7)]
    buf = _tiled("sum_small", lambda *t: (((((((t[0] + t[1]) + t[2]) + t[3]) + t[4]) + t[5]) + t[6]) + t[7],),
                 (rs // tm,), pos, ins, [((8, rs, LANES), F32, (None, tm, LANES), lambda i, p: (p[2], i, 0))])[0]
    return _gather_small(buf).reshape(rows, LANES)


ADD_ROWS = 128


def _multi_tiled(name, fn, pos, groups, in_place=False):
    steps = max(g[1] for g in groups)
    flat_in, in_specs, out_specs, out_shape, counts, dests = [], [], [], [], [], []
    for ins, n_t, (shape, dtype, oidx, dest) in groups:
        for arr, idx in ins:
            flat_in.append(arr)
            in_specs.append(pl.BlockSpec((ADD_ROWS, arr.shape[1]),
                                         lambda i, p, idx=idx, n_t=n_t: (idx(jnp.minimum(i, n_t - 1), p), 0)))
        out_specs.append(pl.BlockSpec((ADD_ROWS, shape[1]),
                                      lambda i, p, oidx=oidx, n_t=n_t: (oidx(jnp.minimum(i, n_t - 1), p), 0)))
        out_shape.append(jax.ShapeDtypeStruct(shape, dtype))
        counts.append((len(ins), n_t))
        dests.append(dest)
    n_in = len(flat_in)
    extra = dests if in_place else []

    def body(_, *refs):
        outs = refs[n_in + len(extra):]
        k = 0
        for (n_a, n_t), o_ref in zip(counts, outs):
            tiles = refs[k:k + n_a]
            k += n_a

            @pl.when(pl.program_id(0) < n_t)
            def _(tiles=tiles, o_ref=o_ref):
                o_ref[...] = fn(*[t[...] for t in tiles]).astype(o_ref.dtype)

    spec = pltpu.PrefetchScalarGridSpec(num_scalar_prefetch=1, grid=(steps,),
                                        in_specs=in_specs + [ANY] * len(extra), out_specs=out_specs)
    return pl.pallas_call(body, grid_spec=spec, out_shape=out_shape,
                          input_output_aliases={1 + n_in + k: k for k in range(len(extra))}, name=name,
                          compiler_params=_params(1))(pos, *flat_in, *extra)


def _add_pair(grads, got, pos):
    groups = []
    for g, q in zip(grads, got):
        nb, R, C = g.shape
        rh = R // 2
        nh = rh // ADD_ROWS
        groups.append(([(g.reshape(nb * R, C), lambda t, p, nh=nh: (t // nh) * 2 * nh + p[1] * nh + t % nh),
                        (q.reshape(nb * rh, C), lambda t, p: t)], nb * nh,
                       ((nb * rh, C), BF, lambda t, p: t, None)))
    res = _multi_tiled("rs_add_pair", lambda u, w: u.astype(F32) + w.astype(F32), pos, groups)
    return [t.reshape(q.shape) for t, q in zip(res, got)]


def _add_chips(parts, slots, reduced, l, pos):
    def add(own, s0, s1, s2):
        return ((own.astype(F32) + s0.astype(F32)) + s1.astype(F32)) + s2.astype(F32)

    groups = []
    for t, s, red in zip(parts, slots, reduced):
        nb, rh, C = t.shape
        L = red.shape[0]
        nh = rh // ADD_ROWS
        ins = [(t.reshape(nb * rh, C), lambda i, p, nh=nh: p[0] * nh + i)]
        ins += [(s.reshape(3 * rh, C), lambda i, p, j=j, nh=nh: j * nh + i) for j in range(3)]
        groups.append((ins, nh, ((L * 2 * rh, C), F32, lambda i, p, nh=nh: l * 2 * nh + p[1] * nh + i,
                                 red.reshape(L * 2 * rh, C))))
    res = _multi_tiled("rs_add_chips", add, pos, groups, in_place=True)
    return [buf.reshape(red.shape) for buf, red in zip(res, reduced)]


def _adamw_math(w, g, m, v):
    m = ADAM_B1 * m + (1.0 - ADAM_B1) * g
    v = ADAM_B2 * v + (1.0 - ADAM_B2) * (g * g)
    m_hat = m / (1.0 - ADAM_B1 ** ADAM_STEP)
    v_hat = v / (1.0 - ADAM_B2 ** ADAM_STEP)
    return -ADAM_LR * (m_hat / (jnp.sqrt(v_hat) + ADAM_EPS) + ADAM_WD * w), m, v


def _adamw(w, g, m, v, lo=0, hi=None, into=None, deps=()):
    L, R, C = w.shape
    hi = L if hi is None else hi
    tr = _tile(R, max(16, ADAM_TILE_ELEMS // C))
    extra = (list(into) if into else []) + list(deps)
    n_alias = 4 if into else 0

    def body(w_ref, g_ref, m_ref, v_ref, *rest):
        go_ref, d_ref, mo_ref, vo_ref = rest[len(extra):]
        gv = g_ref[...]
        d, mn, vn = _adamw_math(w_ref[...], gv, m_ref[...], v_ref[...])
        go_ref[...] = gv
        d_ref[...] = d
        mo_ref[...] = mn
        vo_ref[...] = vn

    spec = pl.BlockSpec((None, tr, C), lambda l, i: (l + lo, i, 0))
    out = jax.ShapeDtypeStruct(w.shape, F32)
    return pl.pallas_call(body, grid=(hi - lo, R // tr), in_specs=[spec] * 4 + [ANY] * len(extra),
                          out_specs=[spec] * 4, out_shape=[out] * 4,
                          input_output_aliases={4 + k: k for k in range(n_alias)}, name="adamw",
                          compiler_params=_params(2))(w, g, m, v, *extra)


def _pack(parts):
    flat = jnp.concatenate([q.reshape(-1, LANES) for q in parts], axis=0)
    return jnp.pad(flat, ((0, -flat.shape[0] % ROW_TILE), (0, 0)))


def _unpack(flat, like):
    out, r = [], 0
    for q in like:
        n = q.size // LANES
        out.append(flat[r:r + n].reshape(q.shape))
        r += n
    return out


def _train_step(a):
    a = dict(a)
    L = a["ffn1_pre_g"].shape[0]
    x, y, c, _ = _place()
    chip = 2 * x + y
    pos = jnp.stack([chip, c, 2 * chip + c]).astype(jnp.int32)
    for name in TRANSPOSED:
        for pre in ("", "m_", "v_"):
            a[pre + name] = jnp.swapaxes(a[pre + name], 1, 2)
    big = [b[0] for b in BIG]
    gathered = big + ["conv_dw_k"]
    n_w, n_g = len(gathered), len(big)

    own = [None] * n_w
    W = {name: [None] * L for name in gathered}
    every = list(range(n_w))
    first, mixer, later = every[:3], every[3:8] + [n_g], every[8:n_g]
    rest = mixer + later

    def cast(i, deps):
        if i == n_g:
            taps = a["conv_dw_k"].reshape(L, CONV_TAPS, LANES)
            return _cast_layers("pad_conv_taps", taps, CONV_PAD, LANES, F32, pos, deps)
        name, _, _, _, rp, cp = BIG[i]
        return _cast_layers("cast_weight", a[name], rp, cp, BF, pos, deps)

    def gather_first(l, ids, tag, deps):
        return _split_start("gather_a%d%s" % (l, tag), _gather_plans(len(ids))[0], [own[i][l] for i in ids], deps)

    def gather_second(l, ids, tag, state, after):
        across, to_sibling = _gather_plans(len(ids))
        bufs = _split_wait("gather_a%d%s_done" % (l, tag), across, state[0], state[1], after)
        return _split_start("gather_b%d%s" % (l, tag), to_sibling, bufs, [])

    def gather_done(l, ids, tag, state, after):
        to_sibling = _gather_plans(len(ids))[1]
        bufs = _split_wait("gather_b%d%s_done" % (l, tag), to_sibling, state[0], state[1], after)
        for i, buf in zip(ids, bufs):
            W[gathered[i]][l] = buf

    for i in first:
        own[i] = cast(i, ())
    state = gather_first(0, first, "f", [])
    for i in rest:
        own[i] = cast(i, (state[2],))
    state = gather_second(0, first, "f", state, [own[i][0] for i in rest])
    gather_done(0, first, "f", state, [])

    parts = {"f": first, "m": mixer, "t": later}
    flying = {}

    def begin(l, part, deps):
        flying[l, part] = gather_first(l, parts[part], part, deps)
        return flying[l, part][2]

    def hand_on(l, part, after):
        flying[l, part] = gather_second(l, parts[part], part, flying[l, part], after)
        return flying[l, part][2]

    def arrive(l, part, after):
        gather_done(l, parts[part], part, flying.pop((l, part)), after)

    def hooks_of(l):
        nxt = l + 1 < L

        def after_ffn1(hv):
            tokens = []
            if l == 0:
                hand_on(0, "m", [hv])
            arrive(l, "m", [hv])
            if l == 0:
                tokens.append(begin(0, "t", [hv]))
            else:
                tokens.append(hand_on(l, "t", [hv]))
            if nxt:
                tokens.append(begin(l + 1, "f", tokens[-1:]))
            return tuple(tokens)

        def after_mix(hv):
            tokens = []
            if l == 0:
                hand_on(0, "t", [hv])
            arrive(l, "t", [hv])
            if nxt:
                tokens.append(hand_on(l + 1, "f", [hv]))
                tokens.append(begin(l + 1, "m", tokens[-1:]))
            return tuple(tokens)

        def after_ffn2(hv):
            tokens = []
            if nxt:
                arrive(l + 1, "f", [hv])
                tokens.append(hand_on(l + 1, "m", [hv]))
                tokens.append(begin(l + 1, "t", tokens[-1:]))
            return tuple(tokens)

        return {"ffn1": after_ffn1, "mix": after_mix, "ffn2": after_ffn2}

    small = [_layer_small(a, l) for l in range(L)]
    h, saved = a["x"][0], []
    deps = (begin(0, "m", []),)
    for l in range(L):
        h, sv = _layer_fwd(l, h, a["p"][l, 0], small[l], W, deps, hooks_of(l))
        saved.append(sv)
        deps = ()

    def loss_fn(yv, t):
        e = yv - t
        return e * (1.0 / D), jnp.sum(e * e, axis=0, keepdims=True)

    dh, lsum = _rowwise("loss", loss_fn, [(h, D, 0), (a["loss_target"][0], D, 0)], [], [(D, F32)], [(1, D)])
    loss = lax.psum(0.5 * jnp.sum(lsum) / D, ("x", "y", "c"))

    G = {name: [jax.ShapeDtypeStruct((N_CHIPS, rp, cp), BF)] * L for name, _, _, _, rp, cp in BIG}
    reduced = [lax.empty((L, rp, cp), F32) for _, _, _, _, rp, cp in BIG]
    pair, cross = _pair_plan(n_g), _cross_plan(n_g)
    small_grads = [None] * L

    def pair_start(l, deps):
        grads = [G[name][l] for name in big]
        lands = [lax.empty((N_CHIPS, g.shape[1] // 2, g.shape[2]), BF) for g in grads]
        return _split_start("rs_pair%d" % l, pair, grads + lands, deps)

    def cross_start(l, state, after):
        bufs = _split_wait("rs_pair%d_done" % l, pair, state[0], state[1], after)
        parts = _add_pair(bufs[:n_g], bufs[n_g:], pos)
        lands = [lax.empty((3,) + t.shape[1:], BF) for t in parts]
        return _split_start("rs_cross%d" % l, cross, parts + lands, [])

    def share_start(l, state, after, reduced):
        bufs = _split_wait("rs_cross%d_done" % l, cross, state[0], state[1], after)
        reduced = _add_chips(bufs[:n_g], bufs[n_g:], reduced, l, pos)
        return _split_start("rs_share%d" % l, _share_plan(n_g, l), reduced, [])

    def share_done(l, state, after):
        return _split_wait("rs_share%d_done" % l, _share_plan(n_g, l), state[0], state[1], after)

    st_pair = st_cross = st_share = None
    for l in reversed(range(L)):
        deps = tuple(s[2] for s in (st_pair, st_share) if s is not None)
        box = {"cross": None}

        def mid(dm, l=l, box=box, st_pair=st_pair, st_share=st_share):
            out = []
            if st_share is not None:
                box["reduced"] = share_done(l + 2, st_share, [dm])
            if st_pair is not None:
                box["cross"] = cross_start(l + 1, st_pair, [dm])
                out.append(box["cross"][2])
            return tuple(out)

        dh, small_grads[l] = _layer_bwd(l, dh, saved[l], small[l], W, G, deps, mid)
        if st_share is not None:
            reduced = box["reduced"]
        st_share = share_start(l + 1, box["cross"], [dh], reduced) if box["cross"] is not None else None
        st_pair = pair_start(l, [dh])
    grad_x = dh
    st_cross = cross_start(0, st_pair, [])
    small_names = SMALL + ("conv_dw_k",)
    stacked = [jnp.stack([small_grads[l][name] for l in range(L)]) for name in small_names]
    summed = dict(zip(small_names, _unpack(_allreduce_small(_pack(stacked), pos), stacked)))
    if st_share is not None:
        reduced = share_done(1, st_share, [summed[small_names[0]], st_cross[2]])
    upper = {}
    if L > 1:
        for name, red in zip(big, reduced):
            upper[name] = _adamw(a[name], red, a["m_" + name], a["v_" + name], 1, L, deps=[st_cross[2]])
    st_share = share_start(0, st_cross, [r[1] for r in upper.values()] + [summed[small_names[0]]], reduced)
    reduced = share_done(0, st_share, [])
    big_grads = dict(zip(big, reduced))

    grads, deltas, new_m, new_v = {}, {}, {}, {}
    for name in big:
        res = _adamw(a[name], big_grads[name], a["m_" + name], a["v_" + name], 0, 1, upper.get(name))
        if name in TRANSPOSED:
            res = [jnp.swapaxes(r, 1, 2) for r in res]
        grads[name], deltas[name], new_m[name], new_v[name] = res
    taps = lax.dynamic_slice_in_dim(summed["conv_dw_k"], chip * LANES, LANES, axis=2)[:, :CONV_TAPS]
    grads["conv_dw_k"] = taps.reshape(a["conv_dw_k"].shape)
    for name in SMALL:
        grads[name] = summed[name].reshape(a[name].shape)
    shapes = [a[name] for name in small_names]
    res = _adamw(*[_pack([a[pre + name] if pre != "g" else grads[name] for name in small_names])[None]
                   for pre in ("", "g", "m_", "v_")])
    for dst, flat in zip((deltas, new_m, new_v), res[1:]):
        for name, val in zip(small_names, _unpack(flat[0], shapes)):
            dst[name] = val

    return (loss, grad_x[None], *[grads[n] for n in WEIGHTS], *[deltas[n] for n in WEIGHTS],
            *[new_m[n] for n in WEIGHTS], *[new_v[n] for n in WEIGHTS])


def kernel(x, p, ffn1_pre_g, ffn1_w_gate, ffn1_w_up, ffn1_w_down, ffn1_post_g, mix_pre_g, w_in, pool_w, pool_scale, w_pool_out, sgu_ln_g, sgu_ln_b, sgu_w_s, sgu_b_s, w_sgu_out, conv_dw_k, conv_dw_b, conv_ln_g, conv_ln_b, w_conv_out, w_out, mix_post_g, ffn2_pre_g, ffn2_w_gate, ffn2_w_up, ffn2_w_down, ffn2_post_g, ple_w_proj, ple_pre_g, ple_w_gate, ple_post_g, loss_target, m_ffn1_pre_g, m_ffn1_w_gate, m_ffn1_w_up, m_ffn1_w_down, m_ffn1_post_g, m_mix_pre_g, m_w_in, m_pool_w, m_pool_scale, m_w_pool_out, m_sgu_ln_g, m_sgu_ln_b, m_sgu_w_s, m_sgu_b_s, m_w_sgu_out, m_conv_dw_k, m_conv_dw_b, m_conv_ln_g, m_conv_ln_b, m_w_conv_out, m_w_out, m_mix_post_g, m_ffn2_pre_g, m_ffn2_w_gate, m_ffn2_w_up, m_ffn2_w_down, m_ffn2_post_g, m_ple_w_proj, m_ple_pre_g, m_ple_w_gate, m_ple_post_g, v_ffn1_pre_g, v_ffn1_w_gate, v_ffn1_w_up, v_ffn1_w_down, v_ffn1_post_g, v_mix_pre_g, v_w_in, v_pool_w, v_pool_scale, v_w_pool_out, v_sgu_ln_g, v_sgu_ln_b, v_sgu_w_s, v_sgu_b_s, v_w_sgu_out, v_conv_dw_k, v_conv_dw_b, v_conv_ln_g, v_conv_ln_b, v_w_conv_out, v_w_out, v_mix_post_g, v_ffn2_pre_g, v_ffn2_w_gate, v_ffn2_w_up, v_ffn2_w_down, v_ffn2_post_g, v_ple_w_proj, v_ple_pre_g, v_ple_w_gate, v_ple_post_g):
    return _train_step(dict(locals()))
```

```python
import math

import jax
import jax.numpy as jnp
from jax import lax
from jax.experimental import pallas as pl
from jax.experimental.pallas import tpu as pltpu

BF = jnp.bfloat16
F32 = jnp.float32
EPS = 1e-6
D_MODEL = 1024
LANES = 128
SUBLANES = 8
MXU_TILE = 256
N_CHIPS = 4
FFN_SHARD = 704
FFN_SHARD_PAD = 768
POOL_WINDOWS = (2, 4, 8, 16)
SGU_HEADS = 4
CHUNK = 128
CONV_TAPS = 31
CONV_PAD = 32
ROW_TILE = 512
EPI_ROWS = 256
VMEM_LIMIT_BYTES = 56 * 1024 * 1024
ADAM_TILE_ELEMS = 3 * 128 * 1024
ADAM_LR, ADAM_B1, ADAM_B2, ADAM_EPS, ADAM_WD, ADAM_STEP =0.001, 0.9, 0.999, 1e-08, 0.01, 10
MESH = pl.DeviceIdType.MESH
ANY = pl.BlockSpec(memory_space=pl.ANY)

ZB_POOL, ZB_U, ZB_V, ZB_A, ZB_B, ZB_GATES = 0, 1, 2, 3, 4, 5
DZ_HALF = 2816

TRANSPOSED = ("ffn1_w_gate", "ffn1_w_up", "ffn2_w_gate", "ffn2_w_up")
BIG = (
    ("ffn1_w_gate", "row", FFN_SHARD, 1024, FFN_SHARD_PAD, 1024),
    ("ffn1_w_up", "row", FFN_SHARD, 1024, FFN_SHARD_PAD, 1024),
    ("ffn1_w_down", "row", FFN_SHARD, 1024, FFN_SHARD_PAD, 1024),
    ("w_in", "col", 1024, 1408, 1024, 1408),
    ("w_pool_out", "col", 512, 256, 512, 256),
    ("w_sgu_out", "col", 512, 256, 512, 256),
    ("w_conv_out", "col", 512, 256, 512, 256),
    ("w_out", "row", 256, 1024, 256, 1024),
    ("ffn2_w_gate", "row", FFN_SHARD, 1024, FFN_SHARD_PAD, 1024),
    ("ffn2_w_up", "row", FFN_SHARD, 1024, FFN_SHARD_PAD, 1024),
    ("ffn2_w_down", "row", FFN_SHARD, 1024, FFN_SHARD_PAD, 1024),
    ("ple_w_proj", "col", 256, 256, 256, 256),
    ("ple_w_gate", "row", 256, 1024, 256, 1024),
)
SMALL = ("ffn1_pre_g", "ffn1_post_g", "mix_pre_g", "pool_w", "pool_scale", "sgu_ln_g", "sgu_ln_b", "sgu_w_s",
         "sgu_b_s", "conv_dw_b", "conv_ln_g", "conv_ln_b", "mix_post_g", "ffn2_pre_g", "ffn2_post_g",
         "ple_pre_g", "ple_post_g")
WEIGHTS = ("ffn1_pre_g", "ffn1_w_gate", "ffn1_w_up", "ffn1_w_down", "ffn1_post_g", "mix_pre_g", "w_in", "pool_w",
           "pool_scale", "w_pool_out", "sgu_ln_g", "sgu_ln_b", "sgu_w_s", "sgu_b_s", "w_sgu_out", "conv_dw_k",
           "conv_dw_b", "conv_ln_g", "conv_ln_b", "w_conv_out", "w_out", "mix_post_g", "ffn2_pre_g", "ffn2_w_gate",
           "ffn2_w_up", "ffn2_w_down", "ffn2_post_g", "ple_w_proj", "ple_pre_g", "ple_w_gate", "ple_post_g")


def _params(n_grid):
    return pltpu.CompilerParams(dimension_semantics=("arbitrary",) * n_grid, vmem_limit_bytes=VMEM_LIMIT_BYTES)


def _tile(n, cap=ROW_TILE):
    for t in range(min(cap, n) - min(cap, n) % 16, 0, -16):
        if n % t == 0:
            return t
    return n


def _sigmoid(x):
    return 0.5 * jnp.tanh(0.5 * x) + 0.5


def _silu_and_grad(x):
    s = _sigmoid(x)
    return x * s, s * (1.0 + x * (1.0 - s))


def _gelu_and_grad(x):
    cdf = 0.5 * (1.0 + lax.erf(x * (1.0 / math.sqrt(2.0))))
    pdf = jnp.exp(-0.5 * x * x) * (1.0 / math.sqrt(2.0 * math.pi))
    return x * cdf, cdf + x * pdf


def _rms_fwd(x, g):
    return x * lax.rsqrt(jnp.mean(x * x, axis=-1, keepdims=True) + EPS) * g


def _rms_bwd(x, g, dy):
    r = lax.rsqrt(jnp.mean(x * x, axis=-1, keepdims=True) + EPS)
    xh = x * r
    dxh = dy * g
    dx = r * (dxh - xh * jnp.mean(dxh * xh, axis=-1, keepdims=True))
    return dx, jnp.sum(dy * xh, axis=0, keepdims=True)


def _ln_stats(x):
    xc = x - jnp.mean(x, axis=-1, keepdims=True)
    r = lax.rsqrt(jnp.mean(xc * xc, axis=-1, keepdims=True) + EPS)
    return xc * r, r


def _ln_bwd(xh, r, g, dy):
    dxh = dy * g
    dx = r * (dxh - jnp.mean(dxh, axis=-1, keepdims=True) - xh * jnp.mean(dxh * xh, axis=-1, keepdims=True))
    return dx, jnp.sum(dy * xh, axis=0, keepdims=True), jnp.sum(dy, axis=0, keepdims=True)


def _rowwise(name, fn, rows, consts, outs, accs=(), tm=ROW_TILE, deps=()):
    T = rows[0][0].shape[-2]
    tm = _tile(T, tm)
    n_in, n_o, n_dep = len(rows) + len(consts), len(outs), len(deps)

    def body(*refs):
        refs = refs[n_dep:]
        res = fn(*[r[...] for r in refs[:n_in]])
        for ref, val in zip(refs[n_in:n_in + n_o], res[:n_o]):
            ref[...] = val.astype(ref.dtype)
        acc_refs = refs[n_in + n_o:]
        if acc_refs:
            @pl.when(pl.program_id(0) == 0)
            def _():
                for ref, val in zip(acc_refs, res[n_o:]):
                    ref[...] = val

            @pl.when(pl.program_id(0) != 0)
            def _():
                for ref, val in zip(acc_refs, res[n_o:]):
                    ref[...] += val

    in_specs = [ANY] * n_dep
    for row in rows:
        w, cb = row[1], row[2]
        if len(row) == 4:
            in_specs.append(pl.BlockSpec((None, tm, w), lambda i, cb=cb, ld=row[3]: (ld, i, cb)))
        else:
            in_specs.append(pl.BlockSpec((tm, w), lambda i, cb=cb: (i, cb)))
    in_specs += [pl.BlockSpec(c.shape, lambda i: (0, 0)) for c in consts]
    out_specs = [pl.BlockSpec((tm, w), lambda i: (i, 0)) for w, _ in outs]
    out_specs += [pl.BlockSpec(s, lambda i: (0, 0)) for s in accs]
    out_shape = [jax.ShapeDtypeStruct((T, w), dt) for w, dt in outs]
    out_shape += [jax.ShapeDtypeStruct(s, F32) for s in accs]
    return pl.pallas_call(body, grid=(T // tm,), in_specs=in_specs, out_specs=out_specs, out_shape=out_shape,
                          name=name, compiler_params=_params(1))(*deps, *[r[0] for r in rows], *consts)


def _tiled(name, fn, grid, pos, ins, outs):
    n_in = len(ins)

    def body(_, *refs):
        res = fn(*[r[...] for r in refs[:n_in]])
        for ref, val in zip(refs[n_in:], res):
            ref[...] = val.astype(ref.dtype)

    spec = pltpu.PrefetchScalarGridSpec(
        num_scalar_prefetch=1, grid=grid, in_specs=[pl.BlockSpec(bs, im) for _, bs, im in ins],
        out_specs=[pl.BlockSpec(bs, im) for _, _, bs, im in outs])
    return pl.pallas_call(body, grid_spec=spec, out_shape=[jax.ShapeDtypeStruct(s, d) for s, d, _, _ in outs],
                          name=name, compiler_params=_params(len(grid)))(pos, *[a for a, _, _ in ins])


def _cast_layers(name, w, rp, cp, dtype, pos, deps=()):
    L, r, c = w.shape

    def body(_, w_ref, *rest):
        for k, o_ref in enumerate(rest[len(deps):]):
            @pl.when(pl.program_id(0) == k)
            def _(o_ref=o_ref):
                if (rp, cp) != (r, c):
                    o_ref[...] = jnp.zeros_like(o_ref)
                    o_ref[pl.ds(0, r), pl.ds(0, c)] = w_ref[...].astype(dtype)
                else:
                    o_ref[...] = w_ref[...].astype(dtype)

    spec = pltpu.PrefetchScalarGridSpec(
        num_scalar_prefetch=1, grid=(L,),
        in_specs=[pl.BlockSpec((None, r, c), lambda l, p: (l, 0, 0))] + [ANY] * len(deps),
        out_specs=[pl.BlockSpec((None, rp, cp), lambda l, p: (p[0], 0, 0))] * L)
    return pl.pallas_call(body, grid_spec=spec, out_shape=[jax.ShapeDtypeStruct((N_CHIPS, rp, cp), dtype)] * L,
                          name=name, compiler_params=_params(1))(pos, w, *deps)


_NN = (((1,), (0,)), ((), ()))
_NT = (((1,), (1,)), ((), ()))
_TN = (((0,), (0,)), ((), ()))


def _mm_tn(name, a, dy, buf, l, a_blocked, tk=ROW_TILE, first=0):
    T = a.shape[0]
    nb, R, C = buf[l].shape
    extra = [buf[l]] if first else []

    def body(a_ref, dy_ref, *rest):
        rest[-1][...] = lax.dot_general(a_ref[...].astype(BF), dy_ref[...].astype(BF), _TN,
                                        preferred_element_type=F32).astype(BF)

    if a_blocked:
        grid = (nb,)
        in_specs = [pl.BlockSpec((T, R), lambda b: (0, b)), pl.BlockSpec((T, C), lambda b: (0, 0))]
        out_specs = pl.BlockSpec((None, R, C), lambda b: (b, 0, 0))
    else:
        tk = min(tk, R)
        grid = (dy.shape[1] // C, R // tk)
        in_specs = [pl.BlockSpec((T, tk), lambda b, k: (0, k)), pl.BlockSpec((T, C), lambda b, k: (0, b))]
        out_specs = pl.BlockSpec((None, tk, C), lambda b, k: (b + first, k, 0))
    buf = list(buf)
    buf[l] = pl.pallas_call(body, grid=grid, in_specs=in_specs + [ANY] * len(extra), out_specs=out_specs,
                            out_shape=jax.ShapeDtypeStruct((nb, R, C), BF),
                            input_output_aliases={2: 0} if extra else {}, name=name,
                            compiler_params=_params(len(grid)))(a, dy, *extra)
    return buf


def _acc_rows(ref, val, first):
    @pl.when(first)
    def _():
        ref[...] = val

    @pl.when(jnp.logical_not(first))
    def _():
        ref[...] += val


def _norm_mm(name, h, g, ws, trans_w, act=False, deps=(), tm=2 * ROW_TILE):
    T = h.shape[0]
    nb, r, cc = ws[0].shape
    bo = r if trans_w else cc
    tm = _tile(T, tm)
    n_w, n_dep = len(ws), len(deps)

    def body(*refs):
        refs = refs[n_dep:]
        h_ref, g_ref, w_refs = refs[0], refs[1], refs[2:2 + n_w]
        n_ref, o_refs, n_s = refs[2 + n_w], refs[3 + n_w:3 + 2 * n_w], refs[-1]

        @pl.when(pl.program_id(1) == 0)
        def _():
            n = _rms_fwd(h_ref[...].astype(F32), g_ref[...]).astype(BF)
            n_s[...] = n
            n_ref[...] = n

        n = n_s[...]
        prods = []
        for w_ref, o_ref in zip(w_refs, o_refs):
            prods.append(lax.dot_general(n, w_ref[...], _NT if trans_w else _NN,
                                         preferred_element_type=F32).astype(BF))
            o_ref[...] = prods[-1]
        if act:
            refs[3 + 2 * n_w][...] = (_silu_and_grad(prods[0].astype(F32))[0] * prods[1].astype(F32)).astype(BF)

    wide = pl.BlockSpec((tm, bo), lambda i, b: (i, b))
    n_out = n_w + (1 if act else 0)
    return pl.pallas_call(
        body, grid=(T // tm, nb),
        in_specs=[ANY] * n_dep + [pl.BlockSpec((tm, D_MODEL), lambda i, b: (i, 0)),
                                  pl.BlockSpec(g.shape, lambda i, b: (0, 0))]
        + [pl.BlockSpec((None, r, cc), lambda i, b: (b, 0, 0))] * n_w,
        out_specs=[pl.BlockSpec((tm, D_MODEL), lambda i, b: (i, 0))] + [wide] * n_out,
        out_shape=[jax.ShapeDtypeStruct((T, D_MODEL), BF)] + [jax.ShapeDtypeStruct((T, nb * bo), BF)] * n_out,
        scratch_shapes=[pltpu.VMEM((tm, D_MODEL), BF)], name=name, compiler_params=_params(2))(*deps, h, g, *ws)


def _mm_res(name, x, w3, h, g, coef, tm=ROW_TILE):
    T, kx = x.shape
    w2 = w3.reshape(kx, D_MODEL)
    tm = _tile(T, tm)

    def body(x_ref, w_ref, h_ref, g_ref, f_ref, o_ref):
        f = jnp.dot(x_ref[...], w_ref[...], preferred_element_type=F32).astype(BF)
        f_ref[...] = f
        o_ref[...] = h_ref[...] + coef * _rms_fwd(f.astype(F32), g_ref[...])

    row = pl.BlockSpec((tm, D_MODEL), lambda i: (i, 0))
    return pl.pallas_call(
        body, grid=(T // tm,),
        in_specs=[pl.BlockSpec((tm, kx), lambda i: (i, 0)), pl.BlockSpec(w2.shape, lambda i: (0, 0)), row,
                  pl.BlockSpec(g.shape, lambda i: (0, 0))],
        out_specs=[row, row],
        out_shape=[jax.ShapeDtypeStruct((T, D_MODEL), BF), jax.ShapeDtypeStruct((T, D_MODEL), F32)],
        name=name, compiler_params=_params(1))(x, w2, h, g)


def _resbwd_mm(name, dh, f, g, coef, w3, trans_w, act=None, deps=(), tm=2 * ROW_TILE):
    T = dh.shape[0]
    nb, r, cc = w3.shape
    bo = r if trans_w else cc
    tm = _tile(T, tm)
    n_dep, n_act = len(deps), 3 if act else 0
    n_i = T // tm

    def body(*refs):
        refs = refs[n_dep:]
        dh_ref, f_ref, g_ref, w_ref = refs[:4]
        df_ref, dg_ref = refs[4 + n_act], refs[5 + n_act]
        df_s = refs[-2] if act else refs[-1]
        i, b = pl.program_id(0), pl.program_id(1)

        @pl.when(b == 0)
        def _():
            dg = jnp.zeros((1, D_MODEL), F32)
            for c in range(tm // EPI_ROWS):
                rows = slice(c * EPI_ROWS, (c + 1) * EPI_ROWS)
                dx, dg_c = _rms_bwd(f_ref[rows, :].astype(F32), g_ref[...], coef * dh_ref[rows, :])
                df_s[rows, :] = dx.astype(BF)
                df_ref[rows, :] = dx.astype(BF)
                dg = dg + dg_c
            _acc_rows(dg_ref, dg, i == 0)

        if act:
            for j in range(bo // MXU_TILE):
                cols = slice(j * MXU_TILE, (j + 1) * MXU_TILE)
                prod = lax.dot_general(df_s[...], w_ref[cols, :], _NT, preferred_element_type=F32)
                val, grad = _silu_and_grad(refs[4][:, cols].astype(F32))
                refs[6 + n_act][:, cols] = (prod * refs[5][:, cols].astype(F32) * grad).astype(BF)
                refs[7 + n_act][:, cols] = (prod * val).astype(BF)
            acc = refs[-1]
            part = lax.dot_general(refs[6][...], df_s[...], _TN, preferred_element_type=F32)

            @pl.when(i == 0)
            def _():
                acc[b] = part

            @pl.when(i != 0)
            def _():
                acc[b] += part

            @pl.when(i == n_i - 1)
            def _():
                refs[8 + n_act][...] = acc[b].astype(BF)
        else:
            refs[6][...] = lax.dot_general(df_s[...], w_ref[...], _NT if trans_w else _NN,
                                           preferred_element_type=F32).astype(BF)

    row = pl.BlockSpec((tm, D_MODEL), lambda i, b: (i, 0))
    wide = pl.BlockSpec((tm, bo), lambda i, b: (i, b))
    vec = pl.BlockSpec((1, D_MODEL), lambda i, b: (0, 0))
    out_specs = [row, vec] + [wide] * (2 if act else 1)
    out_shape = [jax.ShapeDtypeStruct((T, D_MODEL), BF), jax.ShapeDtypeStruct((1, D_MODEL), F32)]
    out_shape += [jax.ShapeDtypeStruct((T, nb * bo), BF)] * (2 if act else 1)
    scratch = [pltpu.VMEM((tm, D_MODEL), BF)]
    if act:
        out_specs.append(pl.BlockSpec((None, r, cc), lambda i, b: (jnp.where(i == n_i - 1, b, 0), 0, 0)))
        out_shape.append(jax.ShapeDtypeStruct((nb, r, cc), BF))
        scratch.append(pltpu.VMEM((nb, r, cc), F32))
    return pl.pallas_call(
        body, grid=(n_i, nb),
        in_specs=[ANY] * n_dep + [row, row, vec, pl.BlockSpec((None, r, cc), lambda i, b: (b, 0, 0))] + [wide] * n_act,
        out_specs=out_specs, out_shape=out_shape, scratch_shapes=scratch, name=name,
        compiler_params=_params(2))(*deps, dh, f, g, w3, *(act or ()))


def _dn_prenorm(name, xs, ws, trans_w, dh, h, g, tm=2 * ROW_TILE):
    T = dh.shape[0]
    chained = not isinstance(ws, (list, tuple))
    ws = [ws] if chained else list(ws)
    _, r, cc = ws[0].shape
    bw = cc if trans_w else r
    per_x = xs[0].shape[1] // bw
    nb = per_x * len(xs) if chained else per_x
    tm = _tile(T, tm)
    n_x, n_w = len(xs), len(ws)

    def body(*refs):
        x_refs, w_refs = refs[:n_x], refs[n_x:n_x + n_w]
        dh_ref, h_ref, g_ref, o_ref, dg_ref, acc = refs[n_x + n_w:]
        i, b = pl.program_id(0), pl.program_id(1)

        @pl.when(b == 0)
        def _():
            acc[...] = jnp.zeros_like(acc)

        def add(x_ref, w_ref):
            acc[...] += lax.dot_general(x_ref[...], w_ref[...], _NT if trans_w else _NN, preferred_element_type=F32)

        if chained:
            for k, x_ref in enumerate(x_refs):
                pl.when(b // per_x == k)(lambda x_ref=x_ref: add(x_ref, w_refs[0]))
        else:
            for x_ref, w_ref in zip(x_refs, w_refs):
                add(x_ref, w_ref)

        @pl.when(b == nb - 1)
        def _():
            dg = jnp.zeros((1, D_MODEL), F32)
            for c in range(tm // EPI_ROWS):
                rows = slice(c * EPI_ROWS, (c + 1) * EPI_ROWS)
                dx, dg_c = _rms_bwd(h_ref[rows, :], g_ref[...], acc[rows, :])
                o_ref[rows, :] = dh_ref[rows, :] + dx
                dg = dg + dg_c
            _acc_rows(dg_ref, dg, i == 0)

    row = pl.BlockSpec((tm, D_MODEL), lambda i, b: (i, 0))
    vec = pl.BlockSpec((1, D_MODEL), lambda i, b: (0, 0))
    if chained:
        x_specs = [pl.BlockSpec((tm, bw), lambda i, b, k=k: (i, jnp.clip(b - k * per_x, 0, per_x - 1)))
                   for k in range(n_x)]
    else:
        x_specs = [pl.BlockSpec((tm, bw), lambda i, b: (i, b))] * n_x
    return pl.pallas_call(
        body, grid=(T // tm, nb),
        in_specs=x_specs + [pl.BlockSpec((None, r, cc), lambda i, b: (b, 0, 0))] * n_w + [row, row, vec],
        out_specs=[row, vec],
        out_shape=[jax.ShapeDtypeStruct((T, D_MODEL), F32), jax.ShapeDtypeStruct((1, D_MODEL), F32)],
        scratch_shapes=[pltpu.VMEM((tm, D_MODEL), F32)], name=name,
        compiler_params=_params(2))(*xs, *ws, dh, h, g)


def _pool_apply(x, win, row):
    s, k = x, 1
    while k < win:
        s = s + jnp.where(row >= k, pltpu.roll(s, k, 0), 0.0)
        k *= 2
    return s / jnp.minimum(row + 1, win).astype(F32) - x


def _pool_apply_t(dp, win, row):
    T = dp.shape[0]
    s, k = dp / jnp.minimum(row + 1, win).astype(F32), 1
    while k < win:
        s = s + jnp.where(row < T - k, pltpu.roll(s, T - k, 0), 0.0)
        k *= 2
    return s - dp


def _pool_fwd(z, w, scale):
    T = z.shape[0]

    def body(z_ref, w_ref, s_ref, o_ref):
        row = lax.broadcasted_iota(jnp.int32, (T, LANES), 0)
        for gi, win in enumerate(POOL_WINDOWS):
            cols = pl.ds(gi * LANES, LANES)
            pooled = _pool_apply(z_ref[:, cols].astype(F32), win, row)
            y = jnp.dot(pooled.astype(BF), w_ref[gi].astype(BF), preferred_element_type=F32)
            o_ref[:, cols] = (y * s_ref[:, cols]).astype(o_ref.dtype)

    return pl.pallas_call(
        body, grid=(1,),
        in_specs=[pl.BlockSpec((T, 512), lambda i: (0, ZB_POOL)), pl.BlockSpec(w.shape, lambda i: (0, 0, 0)),
                  pl.BlockSpec(scale.shape, lambda i: (0, 0))],
        out_specs=pl.BlockSpec((T, 512), lambda i: (0, 0)), out_shape=jax.ShapeDtypeStruct((T, 512), BF),
        name="pool_fwd", compiler_params=_params(1))(z, w, scale)


def _pool_bwd(dr, z, w, scale):
    T = z.shape[0]

    def body(dr_ref, z_ref, w_ref, s_ref, dz_ref, dw_ref, ds_ref):
        row = lax.broadcasted_iota(jnp.int32, (T, LANES), 0)
        for gi, win in enumerate(POOL_WINDOWS):
            cols = pl.ds(gi * LANES, LANES)
            pooled = _pool_apply(z_ref[:, cols].astype(F32), win, row).astype(BF)
            wg = w_ref[gi].astype(BF)
            y = jnp.dot(pooled, wg, preferred_element_type=F32)
            d = dr_ref[:, cols].astype(F32)
            ds_ref[:, cols] = jnp.sum(d * y, axis=0, keepdims=True)
            dy = (d * s_ref[:, cols]).astype(BF)
            dw_ref[gi] = lax.dot_general(pooled, dy, _TN, preferred_element_type=F32)
            dpooled = lax.dot_general(dy, wg, _NT, preferred_element_type=F32)
            dz_ref[:, cols] = _pool_apply_t(dpooled, win, row).astype(dz_ref.dtype)

    return pl.pallas_call(
        body, grid=(1,),
        in_specs=[pl.BlockSpec((T, 512), lambda i: (0, 0)), pl.BlockSpec((T, 512), lambda i: (0, ZB_POOL)),
                  pl.BlockSpec(w.shape, lambda i: (0, 0, 0)), pl.BlockSpec(scale.shape, lambda i: (0, 0))],
        out_specs=[pl.BlockSpec((T, 512), lambda i: (0, 0)), pl.BlockSpec(w.shape, lambda i: (0, 0, 0)),
                   pl.BlockSpec(scale.shape, lambda i: (0, 0))],
        out_shape=[jax.ShapeDtypeStruct((T, 512), BF), jax.ShapeDtypeStruct(w.shape, F32),
                   jax.ShapeDtypeStruct(scale.shape, F32)],
        name="pool_bwd", compiler_params=_params(1))(dr, z, w, scale)


def _tril(transposed=False):
    r = lax.broadcasted_iota(jnp.int32, (CHUNK, CHUNK), 0)
    c = lax.broadcasted_iota(jnp.int32, (CHUNK, CHUNK), 1)
    return c >= r if transposed else r >= c


def _sgu_fwd(z, ln_g, ln_b, w_s, bias):
    T = z.shape[0]
    tm = _tile(T)

    def body(zu_ref, zv_ref, g_ref, b_ref, w_ref, bias_ref, o_ref):
        gu, _ = _gelu_and_grad(zu_ref[...].astype(F32))
        gv, _ = _gelu_and_grad(zv_ref[...].astype(F32))
        xh, _ = _ln_stats(gv)
        v16 = (xh * g_ref[...] + b_ref[...]).astype(BF)
        tri = _tril()
        for h in range(SGU_HEADS):
            cols = slice(h * LANES, (h + 1) * LANES)
            wh = jnp.where(tri, w_ref[h], 0.0).astype(BF)
            for c in range(tm // CHUNK):
                rows = slice(c * CHUNK, (c + 1) * CHUNK)
                s = jnp.dot(wh, v16[rows, cols], preferred_element_type=F32) + bias_ref[:, cols]
                o_ref[rows, cols] = (gu[rows, cols] * s).astype(o_ref.dtype)

    small = [pl.BlockSpec(a.shape, lambda i, n=a.ndim: (0,) * n) for a in (ln_g, ln_b, w_s, bias)]
    return pl.pallas_call(
        body, grid=(T // tm,),
        in_specs=[pl.BlockSpec((tm, 512), lambda i: (i, ZB_U)), pl.BlockSpec((tm, 512), lambda i: (i, ZB_V))] + small,
        out_specs=pl.BlockSpec((tm, 512), lambda i: (i, 0)), out_shape=jax.ShapeDtypeStruct((T, 512), BF),
        name="sgu_fwd", compiler_params=_params(1))(z, z, ln_g, ln_b, w_s, bias)


def _sgu_bwd(dr, z, ln_g, ln_b, w_s, w_st, bias):
    T = z.shape[0]
    tm = _tile(T)
    n_steps = T // tm

    def body(dr_ref, zu_ref, zv_ref, g_ref, b_ref, w_ref, wt_ref, bias_ref,
             dzu_ref, dzv_ref, dg_ref, db_ref, dw_ref, dbias_ref, dgu_s, dv_s):
        i = pl.program_id(0)

        @pl.when(i == 0)
        def _():
            dg_ref[...] = jnp.zeros_like(dg_ref)
            db_ref[...] = jnp.zeros_like(db_ref)
            dw_ref[...] = jnp.zeros_like(dw_ref)
            dbias_ref[...] = jnp.zeros_like(dbias_ref)

        zu = zu_ref[...].astype(F32)
        zv = zv_ref[...].astype(F32)
        gu, gu_grad = _gelu_and_grad(zu)
        gv, gv_grad = _gelu_and_grad(zv)
        xh, r = _ln_stats(gv)
        v16 = (xh * g_ref[...] + b_ref[...]).astype(BF)
        dr = dr_ref[...].astype(F32)
        tri = _tril()
        for h in range(SGU_HEADS):
            cols = slice(h * LANES, (h + 1) * LANES)
            wh = jnp.where(tri, w_ref[h], 0.0).astype(BF)
            wht = jnp.where(_tril(transposed=True), wt_ref[h], 0.0).astype(BF)
            for c in range(tm // CHUNK):
                rows = slice(c * CHUNK, (c + 1) * CHUNK)
                v_blk = v16[rows, cols]
                s = jnp.dot(wh, v_blk, preferred_element_type=F32) + bias_ref[:, cols]
                ds = dr[rows, cols] * gu[rows, cols]
                dgu_s[rows, cols] = dr[rows, cols] * s
                ds16 = ds.astype(BF)
                dw_ref[h] += jnp.where(tri, lax.dot_general(ds16, v_blk, _NT, preferred_element_type=F32), 0.0)
                dv_s[rows, cols] = jnp.dot(wht, ds16, preferred_element_type=F32)
                dbias_ref[:, cols] += ds
        dzu_ref[...] = (dgu_s[...] * gu_grad).astype(dzu_ref.dtype)
        dgv, dg, db = _ln_bwd(xh, r, g_ref[...], dv_s[...])
        dzv_ref[...] = (dgv * gv_grad).astype(dzv_ref.dtype)
        dg_ref[...] += dg
        db_ref[...] += db

        @pl.when(i == n_steps - 1)
        def _():
            for h in range(SGU_HEADS):
                cols = slice(h * LANES, (h + 1) * LANES)
                tot = jnp.sum(dbias_ref[:, cols], axis=1, keepdims=True)
                dbias_ref[:, cols] = jnp.broadcast_to(tot, (CHUNK, LANES))

    small = (ln_g, ln_b, w_s, w_st, bias)
    small_specs = [pl.BlockSpec(a.shape, lambda i, n=a.ndim: (0,) * n) for a in small]
    return pl.pallas_call(
        body, grid=(n_steps,),
        in_specs=[pl.BlockSpec((tm, 512), lambda i: (i, 0)), pl.BlockSpec((tm, 512), lambda i: (i, ZB_U)),
                  pl.BlockSpec((tm, 512), lambda i: (i, ZB_V))] + small_specs,
        out_specs=[pl.BlockSpec((tm, 512), lambda i: (i, 0)), pl.BlockSpec((tm, 512), lambda i: (i, 0)),
                   pl.BlockSpec((1, 512), lambda i: (0, 0)), pl.BlockSpec((1, 512), lambda i: (0, 0)),
                   pl.BlockSpec(w_s.shape, lambda i: (0, 0, 0)), pl.BlockSpec(bias.shape, lambda i: (0, 0))],
        out_shape=[jax.ShapeDtypeStruct((T, 512), BF), jax.ShapeDtypeStruct((T, 512), BF),
                   jax.ShapeDtypeStruct((1, 512), F32), jax.ShapeDtypeStruct((1, 512), F32),
                   jax.ShapeDtypeStruct(w_s.shape, F32), jax.ShapeDtypeStruct(bias.shape, F32)],
        scratch_shapes=[pltpu.VMEM((tm, 512), F32), pltpu.VMEM((tm, 512), F32)],
        name="sgu_bwd", compiler_params=_params(1))(dr, z, z, ln_g, ln_b, w_s, w_st, bias)


def _conv_fwd(z, convk, l, bias):
    T = z.shape[0]

    def body(za_ref, zb_ref, k_ref, b_ref, o_ref):
        xg = za_ref[...].astype(F32) * _sigmoid(zb_ref[...].astype(F32))
        xp = jnp.concatenate([jnp.zeros((CONV_PAD, LANES), F32), xg], axis=0)
        kw = k_ref[...]
        acc = jnp.broadcast_to(b_ref[...], (T, LANES))
        for s in range(SUBLANES):
            xs = xp if s == 0 else pltpu.roll(xp, s, 0)
            for q in range(CONV_PAD // SUBLANES):
                k = CONV_TAPS - 1 - (SUBLANES * q + s)
                if k >= 0:
                    lo = CONV_PAD - SUBLANES * q
                    acc = acc + kw[k:k + 1, :] * xs[lo:lo + T, :]
        o_ref[...] = acc.astype(o_ref.dtype)

    return pl.pallas_call(
        body, grid=(4,),
        in_specs=[pl.BlockSpec((T, LANES), lambda g: (0, 4 * ZB_A + g)),
                  pl.BlockSpec((T, LANES), lambda g: (0, 4 * ZB_B + g)),
                  pl.BlockSpec((None, CONV_PAD, LANES), lambda g: (g, 0, 0)),
                  pl.BlockSpec((1, LANES), lambda g: (0, g))],
        out_specs=pl.BlockSpec((T, LANES), lambda g: (0, g)), out_shape=jax.ShapeDtypeStruct((T, 512), BF),
        name="conv_fwd", compiler_params=_params(1))(z, z, convk[l], bias)


def _conv_bwd(dy, z, convk, l):
    T = z.shape[0]

    def body(dy_ref, za_ref, zb_ref, k_ref, dza_ref, dzb_ref, dk_ref, db_ref):
        a = za_ref[...].astype(F32)
        sg = _sigmoid(zb_ref[...].astype(F32))
        d = dy_ref[...].astype(F32)
        kw = k_ref[...]
        xp = jnp.concatenate([jnp.zeros((CONV_PAD, LANES), F32), a * sg], axis=0)
        dp = jnp.concatenate([d, jnp.zeros((CONV_PAD, LANES), F32)], axis=0)
        dxg = jnp.zeros((T, LANES), F32)
        dk_ref[...] = jnp.zeros_like(dk_ref)
        for s in range(SUBLANES):
            xs = xp if s == 0 else pltpu.roll(xp, s, 0)
            ds = dp if s == 0 else pltpu.roll(dp, T + CONV_PAD - s, 0)
            for q in range(CONV_PAD // SUBLANES):
                k = CONV_TAPS - 1 - (SUBLANES * q + s)
                if k >= 0:
                    lo = CONV_PAD - SUBLANES * q
                    dk_ref[k:k + 1, :] = jnp.sum(d * xs[lo:lo + T, :], axis=0, keepdims=True)
                    dxg = dxg + kw[k:k + 1, :] * ds[SUBLANES * q:SUBLANES * q + T, :]
        db_ref[...] = jnp.sum(d, axis=0, keepdims=True)
        dza_ref[...] = (dxg * sg).astype(dza_ref.dtype)
        dzb_ref[...] = (dxg * a * sg * (1.0 - sg)).astype(dzb_ref.dtype)

    col = pl.BlockSpec((T, LANES), lambda g: (0, g))
    return pl.pallas_call(
        body, grid=(4,),
        in_specs=[col, pl.BlockSpec((T, LANES), lambda g: (0, 4 * ZB_A + g)),
                  pl.BlockSpec((T, LANES), lambda g: (0, 4 * ZB_B + g)),
                  pl.BlockSpec((None, CONV_PAD, LANES), lambda g: (g, 0, 0))],
        out_specs=[col, col, pl.BlockSpec((CONV_PAD, LANES), lambda g: (0, g)),
                   pl.BlockSpec((1, LANES), lambda g: (0, g))],
        out_shape=[jax.ShapeDtypeStruct((T, 512), BF), jax.ShapeDtypeStruct((T, 512), BF),
                   jax.ShapeDtypeStruct((CONV_PAD, 512), F32), jax.ShapeDtypeStruct((1, 512), F32)],
        name="conv_bwd", compiler_params=_params(1))(dy, z, z, convk[l])


D = D_MODEL


def _ffn_fwd(l, h, S, W, pre, deps=()):
    n, gp, u, a = _norm_mm("ffn_in", h, S[pre + "_pre_g"], [W[pre + "_w_gate"][l], W[pre + "_w_up"][l]], True,
                           act=True, deps=deps)
    f, out = _mm_res("ffn_out", a, W[pre + "_w_down"][l], h, S[pre + "_post_g"], 0.5)
    return out, dict(h=h, n=n, gp=gp, u=u, a=a, f=f)


def _ffn_bwd(l, dh, sv, S, W, G, SG, pre, deps=()):
    df, SG[pre + "_post_g"], dgp, du, dwd = _resbwd_mm(
        "ffn_bwd_act", dh, sv["f"], S[pre + "_post_g"], 0.5, W[pre + "_w_down"][l], True,
        act=(sv["gp"], sv["u"], sv["a"]), deps=deps, tm=ROW_TILE)
    G[pre + "_w_down"] = G[pre + "_w_down"][:l] + [dwd] + G[pre + "_w_down"][l + 1:]
    G[pre + "_w_gate"] = _mm_tn("ffn_dw_gate", dgp, sv["n"], G[pre + "_w_gate"], l, True)
    G[pre + "_w_up"] = _mm_tn("ffn_dw_up", du, sv["n"], G[pre + "_w_up"], l, True)
    dh_in, SG[pre + "_pre_g"] = _dn_prenorm("ffn_bwd_in", [dgp, du], [W[pre + "_w_gate"][l], W[pre + "_w_up"][l]],
                                            False, dh, sv["h"], S[pre + "_pre_g"])
    return dh_in


def _gates(zg):
    return [_sigmoid(jnp.concatenate([zg[2 * k].astype(F32), zg[2 * k + 1].astype(F32)], axis=1)) for k in range(3)]


def _merge_fwd(z, rs, ws, tm=ROW_TILE):
    T = z.shape[0]
    tm = _tile(T, tm)
    nb, kk, bw = ws[0].shape

    def body(*refs):
        r_refs, g_refs, w_refs, y_refs, m_ref = refs[:3], refs[3:9], refs[9:12], refs[12:15], refs[15]
        for r_ref, w_ref, y_ref in zip(r_refs, w_refs, y_refs):
            for b in range(nb):
                y_ref[:, b * bw:(b + 1) * bw] = jnp.dot(r_ref[...], w_ref[b],
                                                        preferred_element_type=F32).astype(y_ref.dtype)
        g = _gates([q[...] for q in g_refs])
        m_ref[...] = (g[0] * y_refs[0][...].astype(F32) + g[1] * y_refs[1][...].astype(F32)
                      + g[2] * y_refs[2][...].astype(F32)).astype(m_ref.dtype)

    row = pl.BlockSpec((tm, D_MODEL), lambda i: (i, 0))
    return pl.pallas_call(
        body, grid=(T // tm,),
        in_specs=[pl.BlockSpec((tm, kk), lambda i: (i, 0))] * 3
        + [pl.BlockSpec((tm, 512), lambda i, j=j: (i, ZB_GATES + j)) for j in range(6)]
        + [pl.BlockSpec(ws[0].shape, lambda i: (0, 0, 0))] * 3,
        out_specs=[row] * 4, out_shape=[jax.ShapeDtypeStruct((T, D_MODEL), BF)] * 4,
        name="mix_merge", compiler_params=_params(1))(*rs, *[z] * 6, *ws)


def _merge_bwd(dmerged, z, ys, ws, tm=ROW_TILE // 2):
    T = z.shape[0]
    tm = _tile(T, tm)
    nb, kk, bw = ws[0].shape

    def body(*refs):
        dm_ref, g_refs, y_refs, w_refs = refs[0], refs[1:7], refs[7:10], refs[10:13]
        dy_refs, lo_ref, hi_ref, dr_refs = refs[13:16], refs[16], refs[17], refs[18:21]
        cut = DZ_HALF - ZB_GATES * 512
        dm = dm_ref[...].astype(F32)
        g = _gates([q[...] for q in g_refs])
        for k in range(3):
            dy_refs[k][...] = (dm * g[k]).astype(BF)
            dzg = (dm * y_refs[k][...].astype(F32) * g[k] * (1.0 - g[k])).astype(BF)
            if k == 0:
                lo_ref[...] = dzg[:, :cut]
                hi_ref[:, :D_MODEL - cut] = dzg[:, cut:]
            else:
                hi_ref[:, k * D_MODEL - cut:(k + 1) * D_MODEL - cut] = dzg
            dr = None
            for b in range(nb):
                p = lax.dot_general(dy_refs[k][:, b * bw:(b + 1) * bw], w_refs[k][b], _NT,
                                    preferred_element_type=F32)
                dr = p if dr is None else dr + p
            dr_refs[k][...] = dr.astype(BF)

    row = pl.BlockSpec((tm, D_MODEL), lambda i: (i, 0))
    return pl.pallas_call(
        body, grid=(T // tm,),
        in_specs=[row] + [pl.BlockSpec((tm, 512), lambda i, j=j: (i, ZB_GATES + j)) for j in range(6)] + [row] * 3
        + [pl.BlockSpec(ws[0].shape, lambda i: (0, 0, 0))] * 3,
        out_specs=[row] * 3 + [pl.BlockSpec((tm, DZ_HALF - ZB_GATES * 512), lambda i: (i, 0)),
                               pl.BlockSpec((tm, DZ_HALF), lambda i: (i, 0))]
        + [pl.BlockSpec((tm, kk), lambda i: (i, 0))] * 3,
        out_shape=[jax.ShapeDtypeStruct((T, D_MODEL), BF)] * 3
        + [jax.ShapeDtypeStruct((T, DZ_HALF - ZB_GATES * 512), BF), jax.ShapeDtypeStruct((T, DZ_HALF), BF)]
        + [jax.ShapeDtypeStruct((T, kk), BF)] * 3,
        name="mix_merge_bwd", compiler_params=_params(1))(dmerged, *[z] * 6, *ys, *ws)


def _mix_fwd(l, h, S, W, deps=()):
    n, z = _norm_mm("mix_in", h, S["mix_pre_g"], [W["w_in"][l]], False, deps=deps)
    r_pool = _pool_fwd(z, S["pool_w"], S["pool_scale"])
    r_sgu = _sgu_fwd(z, S["sgu_ln_g"], S["sgu_ln_b"], S["sgu_w_s"], S["sgu_bias"])
    yc = _conv_fwd(z, W["conv_dw_k"], l, S["conv_dw_b"])

    def ln_silu(y, g, b):
        xh, _ = _ln_stats(y.astype(F32))
        return (_silu_and_grad(xh * g + b)[0],)

    r_conv = _rowwise("conv_ln", ln_silu, [(yc, 512, 0)], [S["conv_ln_g"], S["conv_ln_b"]], [(512, BF)])[0]
    y_pool, y_sgu, y_conv, merged = _merge_fwd(z, (r_pool, r_sgu, r_conv),
                                               [W["w_%s_out" % br][l] for br in ("pool", "sgu", "conv")])
    o, out = _mm_res("mix_out", merged, W["w_out"][l], h, S["mix_post_g"], 1.0)
    return out, dict(h=h, n=n, z=z, r_pool=r_pool, r_sgu=r_sgu, yc=yc, r_conv=r_conv, y_pool=y_pool, y_sgu=y_sgu,
                     y_conv=y_conv, merged=merged, o=o)


def _mix_bwd(l, dh, sv, S, W, G, SG, deps=()):
    z = sv["z"]
    do, SG["mix_post_g"], dmerged = _resbwd_mm("mix_bwd_out", dh, sv["o"], S["mix_post_g"], 1.0,
                                               W["w_out"][l].reshape(1, D, D), True, deps=deps)
    G["w_out"] = _mm_tn("mix_dw_out", sv["merged"], do, G["w_out"], l, True)

    branches = ("pool", "sgu", "conv")
    res = _merge_bwd(dmerged, z, [sv["y_" + br] for br in branches], [W["w_%s_out" % br][l] for br in branches])
    dz_gate_lo, dz_hi, dr = res[3], res[4], dict(zip(branches, res[5:]))
    for br, dy in zip(branches, res[:3]):
        wn = "w_%s_out" % br
        G[wn] = _mm_tn("branch_dw", sv["r_" + br], dy, G[wn], l, False)
    dz_pool, SG["pool_w"], SG["pool_scale"] = _pool_bwd(dr["pool"], z, S["pool_w"], S["pool_scale"])
    dzu, dzv, SG["sgu_ln_g"], SG["sgu_ln_b"], SG["sgu_w_s"], dbias = _sgu_bwd(
        dr["sgu"], z, S["sgu_ln_g"], S["sgu_ln_b"], S["sgu_w_s"], S["sgu_w_st"], S["sgu_bias"])
    SG["sgu_b_s"] = dbias[:, ::LANES].T

    def ln_silu_bwd(d, y, g, b):
        xh, r = _ln_stats(y.astype(F32))
        _, grad = _silu_and_grad(xh * g + b)
        return _ln_bwd(xh, r, g, d.astype(F32) * grad)

    dyc, SG["conv_ln_g"], SG["conv_ln_b"] = _rowwise(
        "conv_ln_bwd", ln_silu_bwd, [(dr["conv"], 512, 0), (sv["yc"], 512, 0)], [S["conv_ln_g"], S["conv_ln_b"]],
        [(512, BF)], [(1, 512), (1, 512)])
    dza, dzb, SG["conv_dw_k"], SG["conv_dw_b"] = _conv_bwd(dyc, z, W["conv_dw_k"], l)
    dz_lo = jnp.concatenate([dz_pool, dzu, dzv, dza, dzb, dz_gate_lo], axis=1)
    G["w_in"] = _mm_tn("mix_dw_in", sv["n"], dz_lo, G["w_in"], l, False)
    G["w_in"] = _mm_tn("mix_dw_in", sv["n"], dz_hi, G["w_in"], l, False, first=2)
    dh_in, SG["mix_pre_g"] = _dn_prenorm("mix_bwd_in", [dz_lo, dz_hi], W["w_in"][l], True, dh, sv["h"],
                                         S["mix_pre_g"])
    return dh_in


def _ple_out(h, p, gp, w3, g, tm=ROW_TILE):
    T, kp = p.shape
    nb, _, bw = w3.shape
    tm = _tile(T, tm)

    def body(h_ref, p_ref, gp_ref, w_ref, g_ref, e_ref, o_ref):
        p16 = p_ref[...].astype(BF)
        for b in range(nb):
            e_ref[:, b * bw:(b + 1) * bw] = jnp.dot(p16, w_ref[b], preferred_element_type=F32).astype(BF)
        q = _sigmoid(gp_ref[...].astype(F32)) * e_ref[...].astype(F32)
        o_ref[...] = h_ref[...] + _rms_fwd(q, g_ref[...])

    row = pl.BlockSpec((tm, D_MODEL), lambda i: (i, 0))
    return pl.pallas_call(
        body, grid=(T // tm,),
        in_specs=[row, pl.BlockSpec((tm, kp), lambda i: (i, 0)), row, pl.BlockSpec(w3.shape, lambda i: (0, 0, 0)),
                  pl.BlockSpec(g.shape, lambda i: (0, 0))],
        out_specs=[row, row],
        out_shape=[jax.ShapeDtypeStruct((T, D_MODEL), BF), jax.ShapeDtypeStruct((T, D_MODEL), F32)],
        name="ple_out", compiler_params=_params(1))(h, p, gp, w3, g)


def _ple_fwd(l, h, p_l, S, W, deps=()):
    n, gp = _norm_mm("ple_in", h, S["ple_pre_g"], [W["ple_w_gate"][l].reshape(1, D, D)], False, deps=deps)
    e, out = _ple_out(h, p_l, gp, W["ple_w_proj"][l], S["ple_post_g"])
    return out, dict(h=h, n=n, e=e, gp=gp, p=p_l)


def _ple_bwd_rows(dh, e, gp, g_post, w3, h, g_pre, deps=(), tm=ROW_TILE):
    T = dh.shape[0]
    w2 = w3.reshape(D_MODEL, D_MODEL)
    tm = _tile(T, tm)
    n_dep = len(deps)

    def body(*refs):
        dh_ref, e_ref, gp_ref, gpost_ref, w_ref, h_ref, gpre_ref, de_ref, dgp_ref, o_ref, dpost_ref, dpre_ref = \
            refs[n_dep:]
        first = pl.program_id(0) == 0
        d = dh_ref[...]
        sg = _sigmoid(gp_ref[...].astype(F32))
        ee = e_ref[...].astype(F32)
        dq, dpost = _rms_bwd(sg * ee, gpost_ref[...], d)
        de_ref[...] = (dq * sg).astype(BF)
        dgp = (dq * ee * sg * (1.0 - sg)).astype(BF)
        dgp_ref[...] = dgp
        dn = lax.dot_general(dgp, w_ref[...], _NT, preferred_element_type=F32)
        dx, dpre = _rms_bwd(h_ref[...], gpre_ref[...], dn)
        o_ref[...] = d + dx
        _acc_rows(dpost_ref, dpost, first)
        _acc_rows(dpre_ref, dpre, first)

    row = pl.BlockSpec((tm, D_MODEL), lambda i: (i, 0))
    vec = pl.BlockSpec((1, D_MODEL), lambda i: (0, 0))
    return pl.pallas_call(
        body, grid=(T // tm,),
        in_specs=[ANY] * n_dep + [row, row, row, vec, pl.BlockSpec(w2.shape, lambda i: (0, 0)), row, vec],
        out_specs=[row, row, row, vec, vec],
        out_shape=[jax.ShapeDtypeStruct((T, D_MODEL), BF)] * 2 + [jax.ShapeDtypeStruct((T, D_MODEL), F32)]
        + [jax.ShapeDtypeStruct((1, D_MODEL), F32)] * 2,
        name="ple_bwd", compiler_params=_params(1))(*deps, dh, e, gp, g_post, w2, h, g_pre)


def _ple_bwd(l, dh, sv, S, W, G, SG, deps=()):
    de, dgp, dh_in, SG["ple_post_g"], SG["ple_pre_g"] = _ple_bwd_rows(
        dh, sv["e"], sv["gp"], S["ple_post_g"], W["ple_w_gate"][l], sv["h"], S["ple_pre_g"], deps)
    G["ple_w_proj"] = _mm_tn("ple_dw_proj", sv["p"], de, G["ple_w_proj"], l, False)
    G["ple_w_gate"] = _mm_tn("ple_dw_gate", sv["n"], dgp, G["ple_w_gate"], l, True)
    return dh_in


def _layer_small(a, l):
    S = {}
    for name in SMALL:
        v = a[name][l]
        S[name] = v.reshape(1, -1) if v.ndim == 1 else v
    S["sgu_w_st"] = jnp.swapaxes(S["sgu_w_s"], 1, 2)
    S["sgu_bias"] = jnp.repeat(S["sgu_b_s"].T, LANES, axis=1)
    return S


def _layer_fwd(l, h, p_l, S, W, deps=(), hooks=None):
    hooks = hooks or {}

    def after(part, hv):
        return hooks[part](hv) if part in hooks else ()

    h, sv1 = _ffn_fwd(l, h, S, W, "ffn1", deps)
    h, sv2 = _mix_fwd(l, h, S, W, after("ffn1", h))
    h, sv3 = _ffn_fwd(l, h, S, W, "ffn2", after("mix", h))
    h, sv4 = _ple_fwd(l, h, p_l, S, W, after("ffn2", h))
    return h, (sv1, sv2, sv3, sv4)


def _layer_bwd(l, dh, sv, S, W, G, deps=(), hooks=None):
    hooks = hooks or {}

    def after(part, dv):
        return hooks[part](dv) if part in hooks else ()

    SG = {}
    dh = _ple_bwd(l, dh, sv[3], S, W, G, SG, deps)
    dh = _ffn_bwd(l, dh, sv[2], S, W, G, SG, "ffn2")
    dh = _mix_bwd(l, dh, sv[1], S, W, G, SG, after("ffn2", dh))
    dh = _ffn_bwd(l, dh, sv[0], S, W, G, SG, "ffn1", after("mix", dh))
    return dh, SG


HBM = pl.BlockSpec(memory_space=pltpu.HBM)
SEM = pl.BlockSpec(memory_space=pltpu.SEMAPHORE)
SIDE_EFFECT = pltpu.SideEffectType.DATAFLOW_SIDE_EFFECTING


def _place():
    x, y, c = lax.axis_index("x"), lax.axis_index("y"), lax.axis_index("c")
    chips = [(1 - x, y), (x, 1 - y), (1 - x, 1 - y)]
    return x, y, c, chips


def _remote(src, dst, send_sem, recv_sem, to):
    return pltpu.make_async_remote_copy(src_ref=src, dst_ref=dst, send_sem=send_sem, recv_sem=recv_sem,
                                        device_id=to, device_id_type=MESH)


def _split_start(name, plan, bufs, deps):
    count, fn = plan
    n, nd = len(bufs), len(deps)

    def body(*refs):
        send, recv = refs[nd + n], refs[nd + n + 1]
        x, y, c, chips = _place()
        for k, (src, dst, _, to) in enumerate(fn(refs[nd:nd + n], x, y, c, chips)):
            _remote(src, dst, send.at[k], recv.at[k], to).start()
        refs[-1][...] = jnp.zeros_like(refs[-1])

    res = pl.pallas_call(
        body, in_specs=[ANY] * nd + [HBM] * n,
        out_specs=[SEM, SEM] + [HBM] * n + [pl.BlockSpec(memory_space=pltpu.VMEM)],
        out_shape=[pltpu.SemaphoreType.DMA((count,)), pltpu.SemaphoreType.DMA((count,))]
        + [pltpu.HBM(b.shape, b.dtype) for b in bufs] + [jax.ShapeDtypeStruct((8, LANES), F32)],
        input_output_aliases={nd + i: 2 + i for i in range(n)}, name=name,
        compiler_params=pltpu.CompilerParams(has_side_effects=SIDE_EFFECT),
    )(*deps, *[pltpu.with_memory_space_constraint(b, pltpu.HBM) for b in bufs])
    return (res[0], res[1]), list(res[2:2 + n]), res[-1]


def _split_wait(name, plan, sems, bufs, after):
    _, fn = plan
    n = len(bufs)

    def body(*refs):
        send, recv = refs[n], refs[n + 1]
        x, y, c, chips = _place()
        for k, (src, _, land, to) in enumerate(fn(refs[:n], x, y, c, chips)):
            cp = _remote(src, land, send.at[k], recv.at[k], to)
            cp.wait_send()
            cp.wait_recv()

    res = pl.pallas_call(
        body, in_specs=[HBM] * n + [SEM, SEM] + [ANY] * len(after), out_specs=[HBM] * n,
        out_shape=[pltpu.HBM(b.shape, b.dtype) for b in bufs], input_output_aliases={i: i for i in range(n)},
        name=name, compiler_params=pltpu.CompilerParams(has_side_effects=SIDE_EFFECT))(*bufs, *sems, *after)
    return list(res)


def _gather_plans(n):
    def across(b, x, y, c, chips):
        me, out = 2 * x + y, []
        for a in range(n):
            rh = b[a].shape[1] // 2
            mine = b[a].at[me, pl.ds(c * rh, rh)]
            for cx, cy in chips:
                out.append((mine, mine, b[a].at[2 * cx + cy, pl.ds(c * rh, rh)], (cx, cy, c)))
        return out

    def to_sibling(b, x, y, c, chips):
        out = []
        for a in range(n):
            rh = b[a].shape[1] // 2
            for cx, cy in chips:
                piece = b[a].at[2 * cx + cy, pl.ds(c * rh, rh)]
                out.append((piece, piece, b[a].at[2 * cx + cy, pl.ds((1 - c) * rh, rh)], (x, y, 1 - c)))
        return out

    return (3 * n, across), (3 * n, to_sibling)


def _pair_plan(n):
    def fn(b, x, y, c, chips):
        out = []
        for a in range(n):
            rh = b[a].shape[1] // 2
            out.append((b[a].at[:, pl.ds((1 - c) * rh, rh)], b[n + a], b[n + a], (x, y, 1 - c)))
        return out

    return n, fn


def _cross_plan(n):
    def fn(b, x, y, c, chips):
        out = []
        for a in range(n):
            for j, (cx, cy) in enumerate(chips):
                out.append((b[a].at[2 * cx + cy], b[n + a].at[j], b[n + a].at[j], (cx, cy, c)))
        return out

    return 3 * n, fn


def _share_plan(n, l):
    def fn(b, x, y, c, chips):
        out = []
        for a in range(n):
            rh = b[a].shape[1] // 2
            mine = b[a].at[l, pl.ds(c * rh, rh)]
            out.append((mine, mine, b[a].at[l, pl.ds((1 - c) * rh, rh)], (x, y, 1 - c)))
        return out

    return n, fn


def _exchange_call(name, body, ins, out_shapes, n_remote, in_place=False):
    scratch = [pltpu.SemaphoreType.DMA((n_remote,)), pltpu.SemaphoreType.DMA((n_remote,))]
    aliases = {i: i for i in range(len(ins))} if in_place else {}
    return pl.pallas_call(body, in_specs=[ANY] * len(ins), out_specs=[ANY] * len(out_shapes), out_shape=out_shapes,
                          scratch_shapes=scratch, input_output_aliases=aliases, name=name)(*ins)


def _peers(x, y, c):
    return [(1 - x if m & 4 else x, 1 - y if m & 2 else y, 1 - c if m & 1 else c) for m in range(1, 8)]


def _scatter_small(v3):
    def body(v_ref, o_ref, send, recv):
        x, y, c, _ = _place()
        cps = []
        for m, (px, py, pc) in enumerate(_peers(x, y, c)):
            cps.append(_remote(v_ref.at[4 * px + 2 * py + pc], o_ref.at[m], send.at[m], recv.at[m], (px, py, pc)))
            cps[-1].start()
        for cp in cps:
            cp.wait()

    return _exchange_call("scatter_small", body, [v3], [jax.ShapeDtypeStruct((7,) + v3.shape[1:], v3.dtype)], 7)[0]


def _gather_small(buf):
    def body(_, o_ref, send, recv):
        x, y, c, _ = _place()
        mine = o_ref.at[4 * x + 2 * y + c]
        peers = _peers(x, y, c)
        cps = []
        for m, to in enumerate(peers):
            cps.append(_remote(mine, mine, send.at[m], recv.at[m], to))
            cps[-1].start()
        for m, (px, py, pc) in enumerate(peers):
            slab = o_ref.at[4 * px + 2 * py + pc]
            _remote(slab, slab, send.at[m], recv.at[m], (px, py, pc)).wait_recv()
            cps[m].wait_send()

    return _exchange_call("gather_small", body, [buf], [jax.ShapeDtypeStruct(buf.shape, buf.dtype)], 7,
                          in_place=True)[0]


def _allreduce_small(v, pos):
    rows = v.shape[0]
    rs = rows // 8
    v3 = v.reshape(8, rs, LANES)
    got = _scatter_small(v3)
    tm = _tile(rs)
    ins = [(v3, (None, tm, LANES), lambda i, p: (p[2], i, 0))]
    ins += [(got, (None, tm, LANES), lambda i, p, m=m: (m, i, 0)) for m in range(7)]
    buf = _tiled("sum_small", lambda *t: (((((((t[0] + t[1]) + t[2]) + t[3]) + t[4]) + t[5]) + t[6]) + t[7],),
                 (rs // tm,), pos, ins, [((8, rs, LANES), F32, (None, tm, LANES), lambda i, p: (p[2], i, 0))])[0]
    return _gather_small(buf).reshape(rows, LANES)


ADD_ROWS = 128


def _multi_tiled(name, fn, pos, groups, in_place=False):
    steps = max(g[1] for g in groups)
    flat_in, in_specs, out_specs, out_shape, counts, dests = [], [], [], [], [], []
    for ins, n_t, (shape, dtype, oidx, dest) in groups:
        for arr, idx in ins:
            flat_in.append(arr)
            in_specs.append(pl.BlockSpec((ADD_ROWS, arr.shape[1]),
                                         lambda i, p, idx=idx, n_t=n_t: (idx(jnp.minimum(i, n_t - 1), p), 0)))
        out_specs.append(pl.BlockSpec((ADD_ROWS, shape[1]),
                                      lambda i, p, oidx=oidx, n_t=n_t: (oidx(jnp.minimum(i, n_t - 1), p), 0)))
        out_shape.append(jax.ShapeDtypeStruct(shape, dtype))
        counts.append((len(ins), n_t))
        dests.append(dest)
    n_in = len(flat_in)
    extra = dests if in_place else []

    def body(_, *refs):
        outs = refs[n_in + len(extra):]
        k = 0
        for (n_a, n_t), o_ref in zip(counts, outs):
            tiles = refs[k:k + n_a]
            k += n_a

            @pl.when(pl.program_id(0) < n_t)
            def _(tiles=tiles, o_ref=o_ref):
                o_ref[...] = fn(*[t[...] for t in tiles]).astype(o_ref.dtype)

    spec = pltpu.PrefetchScalarGridSpec(num_scalar_prefetch=1, grid=(steps,),
                                        in_specs=in_specs + [ANY] * len(extra), out_specs=out_specs)
    return pl.pallas_call(body, grid_spec=spec, out_shape=out_shape,
                          input_output_aliases={1 + n_in + k: k for k in range(len(extra))}, name=name,
                          compiler_params=_params(1))(pos, *flat_in, *extra)


def _add_pair(grads, got, pos):
    groups = []
    for g, q in zip(grads, got):
        nb, R, C = g.shape
        rh = R // 2
        nh = rh // ADD_ROWS
        groups.append(([(g.reshape(nb * R, C), lambda t, p, nh=nh: (t // nh) * 2 * nh + p[1] * nh + t % nh),
                        (q.reshape(nb * rh, C), lambda t, p: t)], nb * nh,
                       ((nb * rh, C), BF, lambda t, p: t, None)))
    res = _multi_tiled("rs_add_pair", lambda u, w: u.astype(F32) + w.astype(F32), pos, groups)
    return [t.reshape(q.shape) for t, q in zip(res, got)]


def _add_chips(parts, slots, reduced, l, pos):
    def add(own, s0, s1, s2):
        return ((own.astype(F32) + s0.astype(F32)) + s1.astype(F32)) + s2.astype(F32)

    groups = []
    for t, s, red in zip(parts, slots, reduced):
        nb, rh, C = t.shape
        L = red.shape[0]
        nh = rh // ADD_ROWS
        ins = [(t.reshape(nb * rh, C), lambda i, p, nh=nh: p[0] * nh + i)]
        ins += [(s.reshape(3 * rh, C), lambda i, p, j=j, nh=nh: j * nh + i) for j in range(3)]
        groups.append((ins, nh, ((L * 2 * rh, C), F32, lambda i, p, nh=nh: l * 2 * nh + p[1] * nh + i,
                                 red.reshape(L * 2 * rh, C))))
    res = _multi_tiled("rs_add_chips", add, pos, groups, in_place=True)
    return [buf.reshape(red.shape) for buf, red in zip(res, reduced)]


def _adamw_math(w, g, m, v):
    m = ADAM_B1 * m + (1.0 - ADAM_B1) * g
    v = ADAM_B2 * v + (1.0 - ADAM_B2) * (g * g)
    m_hat = m / (1.0 - ADAM_B1 ** ADAM_STEP)
    v_hat = v / (1.0 - ADAM_B2 ** ADAM_STEP)
    return -ADAM_LR * (m_hat / (jnp.sqrt(v_hat) + ADAM_EPS) + ADAM_WD * w), m, v


def _adamw(w, g, m, v, lo=0, hi=None, into=None, deps=()):
    L, R, C = w.shape
    hi = L if hi is None else hi
    tr = _tile(R, max(16, ADAM_TILE_ELEMS // C))
    extra = (list(into) if into else []) + list(deps)
    n_alias = 4 if into else 0

    def body(w_ref, g_ref, m_ref, v_ref, *rest):
        go_ref, d_ref, mo_ref, vo_ref = rest[len(extra):]
        gv = g_ref[...]
        d, mn, vn = _adamw_math(w_ref[...], gv, m_ref[...], v_ref[...])
        go_ref[...] = gv
        d_ref[...] = d
        mo_ref[...] = mn
        vo_ref[...] = vn

    spec = pl.BlockSpec((None, tr, C), lambda l, i: (l + lo, i, 0))
    out = jax.ShapeDtypeStruct(w.shape, F32)
    return pl.pallas_call(body, grid=(hi - lo, R // tr), in_specs=[spec] * 4 + [ANY] * len(extra),
                          out_specs=[spec] * 4, out_shape=[out] * 4,
                          input_output_aliases={4 + k: k for k in range(n_alias)}, name="adamw",
                          compiler_params=_params(2))(w, g, m, v, *extra)


def _pack(parts):
    flat = jnp.concatenate([q.reshape(-1, LANES) for q in parts], axis=0)
    return jnp.pad(flat, ((0, -flat.shape[0] % ROW_TILE), (0, 0)))


def _unpack(flat, like):
    out, r = [], 0
    for q in like:
        n = q.size // LANES
        out.append(flat[r:r + n].reshape(q.shape))
        r += n
    return out


def _train_step(a):
    a = dict(a)
    L = a["ffn1_pre_g"].shape[0]
    x, y, c, _ = _place()
    chip = 2 * x + y
    pos = jnp.stack([chip, c, 2 * chip + c]).astype(jnp.int32)
    for name in TRANSPOSED:
        for pre in ("", "m_", "v_"):
            a[pre + name] = jnp.swapaxes(a[pre + name], 1, 2)
    big = [b[0] for b in BIG]
    gathered = big + ["conv_dw_k"]
    n_w, n_g = len(gathered), len(big)

    own = [None] * n_w
    W = {name: [None] * L for name in gathered}
    every = list(range(n_w))
    first, mixer, later = every[:3], every[3:8] + [n_g], every[8:n_g]
    rest = mixer + later

    def cast(i, deps):
        if i == n_g:
            taps = a["conv_dw_k"].reshape(L, CONV_TAPS, LANES)
            return _cast_layers("pad_conv_taps", taps, CONV_PAD, LANES, F32, pos, deps)
        name, _, _, _, rp, cp = BIG[i]
        return _cast_layers("cast_weight", a[name], rp, cp, BF, pos, deps)

    def gather_first(l, ids, tag, deps):
        return _split_start("gather_a%d%s" % (l, tag), _gather_plans(len(ids))[0], [own[i][l] for i in ids], deps)

    def gather_second(l, ids, tag, state, after):
        across, to_sibling = _gather_plans(len(ids))
        bufs = _split_wait("gather_a%d%s_done" % (l, tag), across, state[0], state[1], after)
        return _split_start("gather_b%d%s" % (l, tag), to_sibling, bufs, [])

    def gather_done(l, ids, tag, state, after):
        to_sibling = _gather_plans(len(ids))[1]
        bufs = _split_wait("gather_b%d%s_done" % (l, tag), to_sibling, state[0], state[1], after)
        for i, buf in zip(ids, bufs):
            W[gathered[i]][l] = buf

    for i in first:
        own[i] = cast(i, ())
    state = gather_first(0, first, "f", [])
    for i in rest:
        own[i] = cast(i, (state[2],))
    state = gather_second(0, first, "f", state, [own[i][0] for i in rest])
    gather_done(0, first, "f", state, [])

    parts = {"f": first, "m": mixer, "t": later}
    flying = {}

    def begin(l, part, deps):
        flying[l, part] = gather_first(l, parts[part], part, deps)
        return flying[l, part][2]

    def hand_on(l, part, after):
        flying[l, part] = gather_second(l, parts[part], part, flying[l, part], after)
        return flying[l, part][2]

    def arrive(l, part, after):
        gather_done(l, parts[part], part, flying.pop((l, part)), after)

    def hooks_of(l):
        nxt = l + 1 < L

        def after_ffn1(hv):
            tokens = []
            if l == 0:
                hand_on(0, "m", [hv])
            arrive(l, "m", [hv])
            if l == 0:
                tokens.append(begin(0, "t", [hv]))
            else:
                tokens.append(hand_on(l, "t", [hv]))
            if nxt:
                tokens.append(begin(l + 1, "f", tokens[-1:]))
            return tuple(tokens)

        def after_mix(hv):
            tokens = []
            if l == 0:
                hand_on(0, "t", [hv])
            arrive(l, "t", [hv])
            if nxt:
                tokens.append(hand_on(l + 1, "f", [hv]))
                tokens.append(begin(l + 1, "m", tokens[-1:]))
            return tuple(tokens)

        def after_ffn2(hv):
            tokens = []
            if nxt:
                arrive(l + 1, "f", [hv])
                tokens.append(hand_on(l + 1, "m", [hv]))
                tokens.append(begin(l + 1, "t", tokens[-1:]))
            return tuple(tokens)

        return {"ffn1": after_ffn1, "mix": after_mix, "ffn2": after_ffn2}

    small = [_layer_small(a, l) for l in range(L)]
    h, saved = a["x"][0], []
    deps = (begin(0, "m", []),)
    for l in range(L):
        h, sv = _layer_fwd(l, h, a["p"][l, 0], small[l], W, deps, hooks_of(l))
        saved.append(sv)
        deps = ()

    def loss_fn(yv, t):
        e = yv - t
        return e * (1.0 / D), jnp.sum(e * e, axis=0, keepdims=True)

    dh, lsum = _rowwise("loss", loss_fn, [(h, D, 0), (a["loss_target"][0], D, 0)], [], [(D, F32)], [(1, D)])
    loss = lax.psum(0.5 * jnp.sum(lsum) / D, ("x", "y", "c"))

    G = {name: [jax.ShapeDtypeStruct((N_CHIPS, rp, cp), BF)] * L for name, _, _, _, rp, cp in BIG}
    reduced = [lax.empty((L, rp, cp), F32) for _, _, _, _, rp, cp in BIG]
    small_grads = [None] * L
    whole = list(range(n_g))
    piece_a, piece_b, piece_c = whole[8:], whole[3:8], whole[:3]

    def pair_start(l, ids, tag, deps):
        grads = [G[big[i]][l] for i in ids]
        lands = [lax.empty((N_CHIPS, g.shape[1] // 2, g.shape[2]), BF) for g in grads]
        return _split_start("rs_pair%d%s" % (l, tag), _pair_plan(len(ids)), grads + lands, deps)

    def cross_start(l, ids, tag, state, after):
        n = len(ids)
        bufs = _split_wait("rs_pair%d%s_done" % (l, tag), _pair_plan(n), state[0], state[1], after)
        parts = _add_pair(bufs[:n], bufs[n:], pos)
        lands = [lax.empty((3,) + t.shape[1:], BF) for t in parts]
        return _split_start("rs_cross%d%s" % (l, tag), _cross_plan(n), parts + lands, [])

    def cross_finish(l, ids, tag, state, after, reduced):
        n = len(ids)
        bufs = _split_wait("rs_cross%d%s_done" % (l, tag), _cross_plan(n), state[0], state[1], after)
        reduced = list(reduced)
        for i, r in zip(ids, _add_chips(bufs[:n], bufs[n:], [reduced[i] for i in ids], l, pos)):
            reduced[i] = r
        return reduced

    def share_start(l, reduced):
        return _split_start("rs_share%d" % l, _share_plan(n_g, l), reduced, [])

    def share_done(l, state, after):
        return _split_wait("rs_share%d_done" % l, _share_plan(n_g, l), state[0], state[1], after)

    st_pair = st_share = None
    for l in reversed(range(L)):
        deps = tuple(s[2] for s in (st_pair, st_share) if s is not None)
        box = {}

        def after_ffn2(dm, l=l, box=box, st_pair=st_pair, st_share=st_share):
            out = []
            if st_share is not None:
                box["reduced"] = share_done(l + 2, st_share, [dm])
            if st_pair is not None:
                box["cross"] = cross_start(l + 1, whole, "", st_pair, [dm])
                out.append(box["cross"][2])
            if l == 0:
                box["pair_a"] = pair_start(0, piece_a, "a", [dm])
                out.append(box["pair_a"][2])
            return tuple(out)

        def after_mix(dm, box=box):
            box["cross_a"] = cross_start(0, piece_a, "a", box["pair_a"], [dm])
            box["pair_b"] = pair_start(0, piece_b, "b", [dm])
            return (box["cross_a"][2], box["pair_b"][2])

        hooks = {"ffn2": after_ffn2, "mix": after_mix} if l == 0 else {"ffn2": after_ffn2}
        dh, small_grads[l] = _layer_bwd(l, dh, saved[l], small[l], W, G, deps, hooks)
        if st_share is not None:
            reduced = box["reduced"]
        st_share = None
        if "cross" in box:
            reduced = cross_finish(l + 1, whole, "", box["cross"], [dh], reduced)
            st_share = share_start(l + 1, reduced)
        st_pair = pair_start(l, whole, "", [dh]) if l else None
    grad_x = dh
    cross_b = cross_start(0, piece_b, "b", box["pair_b"], [dh])
    cross_c = cross_start(0, piece_c, "c", pair_start(0, piece_c, "c", [cross_b[2]]), [])
    small_names = SMALL + ("conv_dw_k",)
    stacked = [jnp.stack([small_grads[l][name] for l in range(L)]) for name in small_names]
    summed = dict(zip(small_names, _unpack(_allreduce_small(_pack(stacked), pos), stacked)))
    if st_share is not None:
        reduced = share_done(1, st_share, [summed[small_names[0]], cross_c[2]])
    upper = {}
    if L > 1:
        for name, red in zip(big, reduced):
            upper[name] = _adamw(a[name], red, a["m_" + name], a["v_" + name], 1, L, deps=[cross_c[2]])
    done = [r[1] for r in upper.values()] + [summed[small_names[0]]]
    for ids, tag, state in ((piece_a, "a", box["cross_a"]), (piece_b, "b", cross_b), (piece_c, "c", cross_c)):
        reduced = cross_finish(0, ids, tag, state, done, reduced)
    reduced = share_done(0, share_start(0, reduced), [])
    big_grads = dict(zip(big, reduced))

    grads, deltas, new_m, new_v = {}, {}, {}, {}
    for name in big:
        res = _adamw(a[name], big_grads[name], a["m_" + name], a["v_" + name], 0, 1, upper.get(name))
        if name in TRANSPOSED:
            res = [jnp.swapaxes(r, 1, 2) for r in res]
        grads[name], deltas[name], new_m[name], new_v[name] = res
    taps = lax.dynamic_slice_in_dim(summed["conv_dw_k"], chip * LANES, LANES, axis=2)[:, :CONV_TAPS]
    grads["conv_dw_k"] = taps.reshape(a["conv_dw_k"].shape)
    for name in SMALL:
        grads[name] = summed[name].reshape(a[name].shape)
    shapes = [a[name] for name in small_names]
    res = _adamw(*[_pack([a[pre + name] if pre != "g" else grads[name] for name in small_names])[None]
                   for pre in ("", "g", "m_", "v_")])
    for dst, flat in zip((deltas, new_m, new_v), res[1:]):
        for name, val in zip(small_names, _unpack(flat[0], shapes)):
            dst[name] = val

    return (loss, grad_x[None], *[grads[n] for n in WEIGHTS], *[deltas[n] for n in WEIGHTS],
            *[new_m[n] for n in WEIGHTS], *[new_v[n] for n in WEIGHTS])


def kernel(x, p, ffn1_pre_g, ffn1_w_gate, ffn1_w_up, ffn1_w_down, ffn1_post_g, mix_pre_g, w_in, pool_w, pool_scale, w_pool_out, sgu_ln_g, sgu_ln_b, sgu_w_s, sgu_b_s, w_sgu_out, conv_dw_k, conv_dw_b, conv_ln_g, conv_ln_b, w_conv_out, w_out, mix_post_g, ffn2_pre_g, ffn2_w_gate, ffn2_w_up, ffn2_w_down, ffn2_post_g, ple_w_proj, ple_pre_g, ple_w_gate, ple_post_g, loss_target, m_ffn1_pre_g, m_ffn1_w_gate, m_ffn1_w_up, m_ffn1_w_down, m_ffn1_post_g, m_mix_pre_g, m_w_in, m_pool_w, m_pool_scale, m_w_pool_out, m_sgu_ln_g, m_sgu_ln_b, m_sgu_w_s, m_sgu_b_s, m_w_sgu_out, m_conv_dw_k, m_conv_dw_b, m_conv_ln_g, m_conv_ln_b, m_w_conv_out, m_w_out, m_mix_post_g, m_ffn2_pre_g, m_ffn2_w_gate, m_ffn2_w_up, m_ffn2_w_down, m_ffn2_post_g, m_ple_w_proj, m_ple_pre_g, m_ple_w_gate, m_ple_post_g, v_ffn1_pre_g, v_ffn1_w_gate, v_ffn1_w_up, v_ffn1_w_down, v_ffn1_post_g, v_mix_pre_g, v_w_in, v_pool_w, v_pool_scale, v_w_pool_out, v_sgu_ln_g, v_sgu_ln_b, v_sgu_w_s, v_sgu_b_s, v_w_sgu_out, v_conv_dw_k, v_conv_dw_b, v_conv_ln_g, v_conv_ln_b, v_w_conv_out, v_w_out, v_mix_post_g, v_ffn2_pre_g, v_ffn2_w_gate, v_ffn2_w_up, v_ffn2_w_down, v_ffn2_post_g, v_ple_w_proj, v_ple_pre_g, v_ple_w_gate, v_ple_post_g):
    return _train_step(dict(locals()))
```

```python
import math

import jax
import jax.numpy as jnp
from jax import lax
from jax.experimental import pallas as pl
from jax.experimental.pallas import tpu as pltpu

BF = jnp.bfloat16
F32 = jnp.float32
EPS = 1e-6
D_MODEL = 1024
LANES = 128
SUBLANES = 8
MXU_TILE = 256
N_CHIPS = 4
FFN_SHARD = 704
FFN_SHARD_PAD = 768
POOL_WINDOWS = (2, 4, 8, 16)
SGU_HEADS = 4
CHUNK = 128
CONV_TAPS = 31
CONV_PAD = 32
ROW_TILE = 512
EPI_ROWS = 256
VMEM_LIMIT_BYTES = 56 * 1024 * 1024
ADAM_TILE_ELEMS = 3 * 128 * 1024
ADAM_LR, ADAM_B1, ADAM_B2, ADAM_EPS, ADAM_WD, ADAM_STEP =0.001, 0.9, 0.999, 1e-08, 0.01, 10
MESH = pl.DeviceIdType.MESH
ANY = pl.BlockSpec(memory_space=pl.ANY)

ZB_POOL, ZB_U, ZB_V, ZB_A, ZB_B, ZB_GATES = 0, 1, 2, 3, 4, 5
DZ_HALF = 2816

TRANSPOSED = ("ffn1_w_gate", "ffn1_w_up", "ffn2_w_gate", "ffn2_w_up")
BIG = (
    ("ffn1_w_gate", "row", FFN_SHARD, 1024, FFN_SHARD_PAD, 1024),
    ("ffn1_w_up", "row", FFN_SHARD, 1024, FFN_SHARD_PAD, 1024),
    ("ffn1_w_down", "row", FFN_SHARD, 1024, FFN_SHARD_PAD, 1024),
    ("w_in", "col", 1024, 1408, 1024, 1408),
    ("w_pool_out", "col", 512, 256, 512, 256),
    ("w_sgu_out", "col", 512, 256, 512, 256),
    ("w_conv_out", "col", 512, 256, 512, 256),
    ("w_out", "row", 256, 1024, 256, 1024),
    ("ffn2_w_gate", "row", FFN_SHARD, 1024, FFN_SHARD_PAD, 1024),
    ("ffn2_w_up", "row", FFN_SHARD, 1024, FFN_SHARD_PAD, 1024),
    ("ffn2_w_down", "row", FFN_SHARD, 1024, FFN_SHARD_PAD, 1024),
    ("ple_w_proj", "col", 256, 256, 256, 256),
    ("ple_w_gate", "row", 256, 1024, 256, 1024),
)
SMALL = ("ffn1_pre_g", "ffn1_post_g", "mix_pre_g", "pool_w", "pool_scale", "sgu_ln_g", "sgu_ln_b", "sgu_w_s",
         "sgu_b_s", "conv_dw_b", "conv_ln_g", "conv_ln_b", "mix_post_g", "ffn2_pre_g", "ffn2_post_g",
         "ple_pre_g", "ple_post_g")
WEIGHTS = ("ffn1_pre_g", "ffn1_w_gate", "ffn1_w_up", "ffn1_w_down", "ffn1_post_g", "mix_pre_g", "w_in", "pool_w",
           "pool_scale", "w_pool_out", "sgu_ln_g", "sgu_ln_b", "sgu_w_s", "sgu_b_s", "w_sgu_out", "conv_dw_k",
           "conv_dw_b", "conv_ln_g", "conv_ln_b", "w_conv_out", "w_out", "mix_post_g", "ffn2_pre_g", "ffn2_w_gate",
           "ffn2_w_up", "ffn2_w_down", "ffn2_post_g", "ple_w_proj", "ple_pre_g", "ple_w_gate", "ple_post_g")


def _params(n_grid):
    return pltpu.CompilerParams(dimension_semantics=("arbitrary",) * n_grid, vmem_limit_bytes=VMEM_LIMIT_BYTES)


def _tile(n, cap=ROW_TILE):
    for t in range(min(cap, n) - min(cap, n) % 16, 0, -16):
        if n % t == 0:
            return t
    return n


def _sigmoid(x):
    return 0.5 * jnp.tanh(0.5 * x) + 0.5


def _silu_and_grad(x):
    s = _sigmoid(x)
    return x * s, s * (1.0 + x * (1.0 - s))


def _gelu_and_grad(x):
    cdf = 0.5 * (1.0 + lax.erf(x * (1.0 / math.sqrt(2.0))))
    pdf = jnp.exp(-0.5 * x * x) * (1.0 / math.sqrt(2.0 * math.pi))
    return x * cdf, cdf + x * pdf


def _rms_fwd(x, g):
    return x * lax.rsqrt(jnp.mean(x * x, axis=-1, keepdims=True) + EPS) * g


def _rms_bwd(x, g, dy):
    r = lax.rsqrt(jnp.mean(x * x, axis=-1, keepdims=True) + EPS)
    xh = x * r
    dxh = dy * g
    dx = r * (dxh - xh * jnp.mean(dxh * xh, axis=-1, keepdims=True))
    return dx, jnp.sum(dy * xh, axis=0, keepdims=True)


def _ln_stats(x):
    xc = x - jnp.mean(x, axis=-1, keepdims=True)
    r = lax.rsqrt(jnp.mean(xc * xc, axis=-1, keepdims=True) + EPS)
    return xc * r, r


def _ln_bwd(xh, r, g, dy):
    dxh = dy * g
    dx = r * (dxh - jnp.mean(dxh, axis=-1, keepdims=True) - xh * jnp.mean(dxh * xh, axis=-1, keepdims=True))
    return dx, jnp.sum(dy * xh, axis=0, keepdims=True), jnp.sum(dy, axis=0, keepdims=True)


def _rowwise(name, fn, rows, consts, outs, accs=(), tm=ROW_TILE, deps=()):
    T = rows[0][0].shape[-2]
    tm = _tile(T, tm)
    n_in, n_o, n_dep = len(rows) + len(consts), len(outs), len(deps)

    def body(*refs):
        refs = refs[n_dep:]
        res = fn(*[r[...] for r in refs[:n_in]])
        for ref, val in zip(refs[n_in:n_in + n_o], res[:n_o]):
            ref[...] = val.astype(ref.dtype)
        acc_refs = refs[n_in + n_o:]
        if acc_refs:
            @pl.when(pl.program_id(0) == 0)
            def _():
                for ref, val in zip(acc_refs, res[n_o:]):
                    ref[...] = val

            @pl.when(pl.program_id(0) != 0)
            def _():
                for ref, val in zip(acc_refs, res[n_o:]):
                    ref[...] += val

    in_specs = [ANY] * n_dep
    for row in rows:
        w, cb = row[1], row[2]
        if len(row) == 4:
            in_specs.append(pl.BlockSpec((None, tm, w), lambda i, cb=cb, ld=row[3]: (ld, i, cb)))
        else:
            in_specs.append(pl.BlockSpec((tm, w), lambda i, cb=cb: (i, cb)))
    in_specs += [pl.BlockSpec(c.shape, lambda i: (0, 0)) for c in consts]
    out_specs = [pl.BlockSpec((tm, w), lambda i: (i, 0)) for w, _ in outs]
    out_specs += [pl.BlockSpec(s, lambda i: (0, 0)) for s in accs]
    out_shape = [jax.ShapeDtypeStruct((T, w), dt) for w, dt in outs]
    out_shape += [jax.ShapeDtypeStruct(s, F32) for s in accs]
    return pl.pallas_call(body, grid=(T // tm,), in_specs=in_specs, out_specs=out_specs, out_shape=out_shape,
                          name=name, compiler_params=_params(1))(*deps, *[r[0] for r in rows], *consts)


def _tiled(name, fn, grid, pos, ins, outs):
    n_in = len(ins)

    def body(_, *refs):
        res = fn(*[r[...] for r in refs[:n_in]])
        for ref, val in zip(refs[n_in:], res):
            ref[...] = val.astype(ref.dtype)

    spec = pltpu.PrefetchScalarGridSpec(
        num_scalar_prefetch=1, grid=grid, in_specs=[pl.BlockSpec(bs, im) for _, bs, im in ins],
        out_specs=[pl.BlockSpec(bs, im) for _, _, bs, im in outs])
    return pl.pallas_call(body, grid_spec=spec, out_shape=[jax.ShapeDtypeStruct(s, d) for s, d, _, _ in outs],
                          name=name, compiler_params=_params(len(grid)))(pos, *[a for a, _, _ in ins])


def _cast_layers(name, w, rp, cp, dtype, pos, deps=()):
    L, r, c = w.shape

    def body(_, w_ref, *rest):
        for k, o_ref in enumerate(rest[len(deps):]):
            @pl.when(pl.program_id(0) == k)
            def _(o_ref=o_ref):
                if (rp, cp) != (r, c):
                    o_ref[...] = jnp.zeros_like(o_ref)
                    o_ref[pl.ds(0, r), pl.ds(0, c)] = w_ref[...].astype(dtype)
                else:
                    o_ref[...] = w_ref[...].astype(dtype)

    spec = pltpu.PrefetchScalarGridSpec(
        num_scalar_prefetch=1, grid=(L,),
        in_specs=[pl.BlockSpec((None, r, c), lambda l, p: (l, 0, 0))] + [ANY] * len(deps),
        out_specs=[pl.BlockSpec((None, rp, cp), lambda l, p: (p[0], 0, 0))] * L)
    return pl.pallas_call(body, grid_spec=spec, out_shape=[jax.ShapeDtypeStruct((N_CHIPS, rp, cp), dtype)] * L,
                          name=name, compiler_params=_params(1))(pos, w, *deps)


_NN = (((1,), (0,)), ((), ()))
_NT = (((1,), (1,)), ((), ()))
_TN = (((0,), (0,)), ((), ()))


def _mm_tn(name, a, dy, buf, l, a_blocked, tk=ROW_TILE, first=0):
    T = a.shape[0]
    nb, R, C = buf[l].shape
    extra = [buf[l]] if first else []

    def body(a_ref, dy_ref, *rest):
        rest[-1][...] = lax.dot_general(a_ref[...].astype(BF), dy_ref[...].astype(BF), _TN,
                                        preferred_element_type=F32).astype(BF)

    if a_blocked:
        grid = (nb,)
        in_specs = [pl.BlockSpec((T, R), lambda b: (0, b)), pl.BlockSpec((T, C), lambda b: (0, 0))]
        out_specs = pl.BlockSpec((None, R, C), lambda b: (b, 0, 0))
    else:
        tk = min(tk, R)
        grid = (dy.shape[1] // C, R // tk)
        in_specs = [pl.BlockSpec((T, tk), lambda b, k: (0, k)), pl.BlockSpec((T, C), lambda b, k: (0, b))]
        out_specs = pl.BlockSpec((None, tk, C), lambda b, k: (b + first, k, 0))
    buf = list(buf)
    buf[l] = pl.pallas_call(body, grid=grid, in_specs=in_specs + [ANY] * len(extra), out_specs=out_specs,
                            out_shape=jax.ShapeDtypeStruct((nb, R, C), BF),
                            input_output_aliases={2: 0} if extra else {}, name=name,
                            compiler_params=_params(len(grid)))(a, dy, *extra)
    return buf


def _acc_rows(ref, val, first):
    @pl.when(first)
    def _():
        ref[...] = val

    @pl.when(jnp.logical_not(first))
    def _():
        ref[...] += val


def _norm_mm(name, h, g, ws, trans_w, act=False, deps=(), tm=2 * ROW_TILE):
    T = h.shape[0]
    nb, r, cc = ws[0].shape
    bo = r if trans_w else cc
    tm = _tile(T, tm)
    n_w, n_dep = len(ws), len(deps)

    def body(*refs):
        refs = refs[n_dep:]
        h_ref, g_ref, w_refs = refs[0], refs[1], refs[2:2 + n_w]
        n_ref, o_refs, n_s = refs[2 + n_w], refs[3 + n_w:3 + 2 * n_w], refs[-1]

        @pl.when(pl.program_id(1) == 0)
        def _():
            n = _rms_fwd(h_ref[...].astype(F32), g_ref[...]).astype(BF)
            n_s[...] = n
            n_ref[...] = n

        n = n_s[...]
        prods = []
        for w_ref, o_ref in zip(w_refs, o_refs):
            prods.append(lax.dot_general(n, w_ref[...], _NT if trans_w else _NN,
                                         preferred_element_type=F32).astype(BF))
            o_ref[...] = prods[-1]
        if act:
            refs[3 + 2 * n_w][...] = (_silu_and_grad(prods[0].astype(F32))[0] * prods[1].astype(F32)).astype(BF)

    wide = pl.BlockSpec((tm, bo), lambda i, b: (i, b))
    n_out = n_w + (1 if act else 0)
    return pl.pallas_call(
        body, grid=(T // tm, nb),
        in_specs=[ANY] * n_dep + [pl.BlockSpec((tm, D_MODEL), lambda i, b: (i, 0)),
                                  pl.BlockSpec(g.shape, lambda i, b: (0, 0))]
        + [pl.BlockSpec((None, r, cc), lambda i, b: (b, 0, 0))] * n_w,
        out_specs=[pl.BlockSpec((tm, D_MODEL), lambda i, b: (i, 0))] + [wide] * n_out,
        out_shape=[jax.ShapeDtypeStruct((T, D_MODEL), BF)] + [jax.ShapeDtypeStruct((T, nb * bo), BF)] * n_out,
        scratch_shapes=[pltpu.VMEM((tm, D_MODEL), BF)], name=name, compiler_params=_params(2))(*deps, h, g, *ws)


def _mm_res(name, x, w3, h, g, coef, tm=ROW_TILE):
    T, kx = x.shape
    w2 = w3.reshape(kx, D_MODEL)
    tm = _tile(T, tm)

    def body(x_ref, w_ref, h_ref, g_ref, f_ref, o_ref):
        f = jnp.dot(x_ref[...], w_ref[...], preferred_element_type=F32).astype(BF)
        f_ref[...] = f
        o_ref[...] = h_ref[...] + coef * _rms_fwd(f.astype(F32), g_ref[...])

    row = pl.BlockSpec((tm, D_MODEL), lambda i: (i, 0))
    return pl.pallas_call(
        body, grid=(T // tm,),
        in_specs=[pl.BlockSpec((tm, kx), lambda i: (i, 0)), pl.BlockSpec(w2.shape, lambda i: (0, 0)), row,
                  pl.BlockSpec(g.shape, lambda i: (0, 0))],
        out_specs=[row, row],
        out_shape=[jax.ShapeDtypeStruct((T, D_MODEL), BF), jax.ShapeDtypeStruct((T, D_MODEL), F32)],
        name=name, compiler_params=_params(1))(x, w2, h, g)


def _resbwd_mm(name, dh, f, g, coef, w3, trans_w, act=None, deps=(), tm=2 * ROW_TILE):
    T = dh.shape[0]
    nb, r, cc = w3.shape
    bo = r if trans_w else cc
    tm = _tile(T, tm)
    n_dep, n_act = len(deps), 3 if act else 0
    n_i = T // tm

    def body(*refs):
        refs = refs[n_dep:]
        dh_ref, f_ref, g_ref, w_ref = refs[:4]
        df_ref, dg_ref = refs[4 + n_act], refs[5 + n_act]
        df_s = refs[-2] if act else refs[-1]
        i, b = pl.program_id(0), pl.program_id(1)

        @pl.when(b == 0)
        def _():
            dg = jnp.zeros((1, D_MODEL), F32)
            for c in range(tm // EPI_ROWS):
                rows = slice(c * EPI_ROWS, (c + 1) * EPI_ROWS)
                dx, dg_c = _rms_bwd(f_ref[rows, :].astype(F32), g_ref[...], coef * dh_ref[rows, :])
                df_s[rows, :] = dx.astype(BF)
                df_ref[rows, :] = dx.astype(BF)
                dg = dg + dg_c
            _acc_rows(dg_ref, dg, i == 0)

        if act:
            for j in range(bo // MXU_TILE):
                cols = slice(j * MXU_TILE, (j + 1) * MXU_TILE)
                prod = lax.dot_general(df_s[...], w_ref[cols, :], _NT, preferred_element_type=F32)
                val, grad = _silu_and_grad(refs[4][:, cols].astype(F32))
                refs[6 + n_act][:, cols] = (prod * refs[5][:, cols].astype(F32) * grad).astype(BF)
                refs[7 + n_act][:, cols] = (prod * val).astype(BF)
            acc = refs[-1]
            part = lax.dot_general(refs[6][...], df_s[...], _TN, preferred_element_type=F32)

            @pl.when(i == 0)
            def _():
                acc[b] = part

            @pl.when(i != 0)
            def _():
                acc[b] += part

            @pl.when(i == n_i - 1)
            def _():
                refs[8 + n_act][...] = acc[b].astype(BF)
        else:
            refs[6][...] = lax.dot_general(df_s[...], w_ref[...], _NT if trans_w else _NN,
                                           preferred_element_type=F32).astype(BF)

    row = pl.BlockSpec((tm, D_MODEL), lambda i, b: (i, 0))
    wide = pl.BlockSpec((tm, bo), lambda i, b: (i, b))
    vec = pl.BlockSpec((1, D_MODEL), lambda i, b: (0, 0))
    out_specs = [row, vec] + [wide] * (2 if act else 1)
    out_shape = [jax.ShapeDtypeStruct((T, D_MODEL), BF), jax.ShapeDtypeStruct((1, D_MODEL), F32)]
    out_shape += [jax.ShapeDtypeStruct((T, nb * bo), BF)] * (2 if act else 1)
    scratch = [pltpu.VMEM((tm, D_MODEL), BF)]
    if act:
        out_specs.append(pl.BlockSpec((None, r, cc), lambda i, b: (jnp.where(i == n_i - 1, b, 0), 0, 0)))
        out_shape.append(jax.ShapeDtypeStruct((nb, r, cc), BF))
        scratch.append(pltpu.VMEM((nb, r, cc), F32))
    return pl.pallas_call(
        body, grid=(n_i, nb),
        in_specs=[ANY] * n_dep + [row, row, vec, pl.BlockSpec((None, r, cc), lambda i, b: (b, 0, 0))] + [wide] * n_act,
        out_specs=out_specs, out_shape=out_shape, scratch_shapes=scratch, name=name,
        compiler_params=_params(2))(*deps, dh, f, g, w3, *(act or ()))


def _dn_prenorm(name, xs, ws, trans_w, dh, h, g, tm=2 * ROW_TILE):
    T = dh.shape[0]
    chained = not isinstance(ws, (list, tuple))
    ws = [ws] if chained else list(ws)
    _, r, cc = ws[0].shape
    bw = cc if trans_w else r
    per_x = xs[0].shape[1] // bw
    nb = per_x * len(xs) if chained else per_x
    tm = _tile(T, tm)
    n_x, n_w = len(xs), len(ws)

    def body(*refs):
        x_refs, w_refs = refs[:n_x], refs[n_x:n_x + n_w]
        dh_ref, h_ref, g_ref, o_ref, dg_ref, acc = refs[n_x + n_w:]
        i, b = pl.program_id(0), pl.program_id(1)

        @pl.when(b == 0)
        def _():
            acc[...] = jnp.zeros_like(acc)

        def add(x_ref, w_ref):
            acc[...] += lax.dot_general(x_ref[...], w_ref[...], _NT if trans_w else _NN, preferred_element_type=F32)

        if chained:
            for k, x_ref in enumerate(x_refs):
                pl.when(b // per_x == k)(lambda x_ref=x_ref: add(x_ref, w_refs[0]))
        else:
            for x_ref, w_ref in zip(x_refs, w_refs):
                add(x_ref, w_ref)

        @pl.when(b == nb - 1)
        def _():
            dg = jnp.zeros((1, D_MODEL), F32)
            for c in range(tm // EPI_ROWS):
                rows = slice(c * EPI_ROWS, (c + 1) * EPI_ROWS)
                dx, dg_c = _rms_bwd(h_ref[rows, :], g_ref[...], acc[rows, :])
                o_ref[rows, :] = dh_ref[rows, :] + dx
                dg = dg + dg_c
            _acc_rows(dg_ref, dg, i == 0)

    row = pl.BlockSpec((tm, D_MODEL), lambda i, b: (i, 0))
    vec = pl.BlockSpec((1, D_MODEL), lambda i, b: (0, 0))
    if chained:
        x_specs = [pl.BlockSpec((tm, bw), lambda i, b, k=k: (i, jnp.clip(b - k * per_x, 0, per_x - 1)))
                   for k in range(n_x)]
    else:
        x_specs = [pl.BlockSpec((tm, bw), lambda i, b: (i, b))] * n_x
    return pl.pallas_call(
        body, grid=(T // tm, nb),
        in_specs=x_specs + [pl.BlockSpec((None, r, cc), lambda i, b: (b, 0, 0))] * n_w + [row, row, vec],
        out_specs=[row, vec],
        out_shape=[jax.ShapeDtypeStruct((T, D_MODEL), F32), jax.ShapeDtypeStruct((1, D_MODEL), F32)],
        scratch_shapes=[pltpu.VMEM((tm, D_MODEL), F32)], name=name,
        compiler_params=_params(2))(*xs, *ws, dh, h, g)


def _pool_apply(x, win, row):
    s, k = x, 1
    while k < win:
        s = s + jnp.where(row >= k, pltpu.roll(s, k, 0), 0.0)
        k *= 2
    return s / jnp.minimum(row + 1, win).astype(F32) - x


def _pool_apply_t(dp, win, row):
    T = dp.shape[0]
    s, k = dp / jnp.minimum(row + 1, win).astype(F32), 1
    while k < win:
        s = s + jnp.where(row < T - k, pltpu.roll(s, T - k, 0), 0.0)
        k *= 2
    return s - dp


def _pool_fwd(z, w, scale):
    T = z.shape[0]

    def body(z_ref, w_ref, s_ref, o_ref):
        row = lax.broadcasted_iota(jnp.int32, (T, LANES), 0)
        for gi, win in enumerate(POOL_WINDOWS):
            cols = pl.ds(gi * LANES, LANES)
            pooled = _pool_apply(z_ref[:, cols].astype(F32), win, row)
            y = jnp.dot(pooled.astype(BF), w_ref[gi].astype(BF), preferred_element_type=F32)
            o_ref[:, cols] = (y * s_ref[:, cols]).astype(o_ref.dtype)

    return pl.pallas_call(
        body, grid=(1,),
        in_specs=[pl.BlockSpec((T, 512), lambda i: (0, ZB_POOL)), pl.BlockSpec(w.shape, lambda i: (0, 0, 0)),
                  pl.BlockSpec(scale.shape, lambda i: (0, 0))],
        out_specs=pl.BlockSpec((T, 512), lambda i: (0, 0)), out_shape=jax.ShapeDtypeStruct((T, 512), BF),
        name="pool_fwd", compiler_params=_params(1))(z, w, scale)


def _pool_bwd(dr, z, w, scale):
    T = z.shape[0]

    def body(dr_ref, z_ref, w_ref, s_ref, dz_ref, dw_ref, ds_ref):
        row = lax.broadcasted_iota(jnp.int32, (T, LANES), 0)
        for gi, win in enumerate(POOL_WINDOWS):
            cols = pl.ds(gi * LANES, LANES)
            pooled = _pool_apply(z_ref[:, cols].astype(F32), win, row).astype(BF)
            wg = w_ref[gi].astype(BF)
            y = jnp.dot(pooled, wg, preferred_element_type=F32)
            d = dr_ref[:, cols].astype(F32)
            ds_ref[:, cols] = jnp.sum(d * y, axis=0, keepdims=True)
            dy = (d * s_ref[:, cols]).astype(BF)
            dw_ref[gi] = lax.dot_general(pooled, dy, _TN, preferred_element_type=F32)
            dpooled = lax.dot_general(dy, wg, _NT, preferred_element_type=F32)
            dz_ref[:, cols] = _pool_apply_t(dpooled, win, row).astype(dz_ref.dtype)

    return pl.pallas_call(
        body, grid=(1,),
        in_specs=[pl.BlockSpec((T, 512), lambda i: (0, 0)), pl.BlockSpec((T, 512), lambda i: (0, ZB_POOL)),
                  pl.BlockSpec(w.shape, lambda i: (0, 0, 0)), pl.BlockSpec(scale.shape, lambda i: (0, 0))],
        out_specs=[pl.BlockSpec((T, 512), lambda i: (0, 0)), pl.BlockSpec(w.shape, lambda i: (0, 0, 0)),
                   pl.BlockSpec(scale.shape, lambda i: (0, 0))],
        out_shape=[jax.ShapeDtypeStruct((T, 512), BF), jax.ShapeDtypeStruct(w.shape, F32),
                   jax.ShapeDtypeStruct(scale.shape, F32)],
        name="pool_bwd", compiler_params=_params(1))(dr, z, w, scale)


def _tril(transposed=False):
    r = lax.broadcasted_iota(jnp.int32, (CHUNK, CHUNK), 0)
    c = lax.broadcasted_iota(jnp.int32, (CHUNK, CHUNK), 1)
    return c >= r if transposed else r >= c


def _sgu_fwd(z, ln_g, ln_b, w_s, bias):
    T = z.shape[0]
    tm = _tile(T)

    def body(zu_ref, zv_ref, g_ref, b_ref, w_ref, bias_ref, o_ref):
        gu, _ = _gelu_and_grad(zu_ref[...].astype(F32))
        gv, _ = _gelu_and_grad(zv_ref[...].astype(F32))
        xh, _ = _ln_stats(gv)
        v16 = (xh * g_ref[...] + b_ref[...]).astype(BF)
        tri = _tril()
        for h in range(SGU_HEADS):
            cols = slice(h * LANES, (h + 1) * LANES)
            wh = jnp.where(tri, w_ref[h], 0.0).astype(BF)
            for c in range(tm // CHUNK):
                rows = slice(c * CHUNK, (c + 1) * CHUNK)
                s = jnp.dot(wh, v16[rows, cols], preferred_element_type=F32) + bias_ref[:, cols]
                o_ref[rows, cols] = (gu[rows, cols] * s).astype(o_ref.dtype)

    small = [pl.BlockSpec(a.shape, lambda i, n=a.ndim: (0,) * n) for a in (ln_g, ln_b, w_s, bias)]
    return pl.pallas_call(
        body, grid=(T // tm,),
        in_specs=[pl.BlockSpec((tm, 512), lambda i: (i, ZB_U)), pl.BlockSpec((tm, 512), lambda i: (i, ZB_V))] + small,
        out_specs=pl.BlockSpec((tm, 512), lambda i: (i, 0)), out_shape=jax.ShapeDtypeStruct((T, 512), BF),
        name="sgu_fwd", compiler_params=_params(1))(z, z, ln_g, ln_b, w_s, bias)


def _sgu_bwd(dr, z, ln_g, ln_b, w_s, w_st, bias):
    T = z.shape[0]
    tm = _tile(T)
    n_steps = T // tm

    def body(dr_ref, zu_ref, zv_ref, g_ref, b_ref, w_ref, wt_ref, bias_ref,
             dzu_ref, dzv_ref, dg_ref, db_ref, dw_ref, dbias_ref, dgu_s, dv_s):
        i = pl.program_id(0)

        @pl.when(i == 0)
        def _():
            dg_ref[...] = jnp.zeros_like(dg_ref)
            db_ref[...] = jnp.zeros_like(db_ref)
            dw_ref[...] = jnp.zeros_like(dw_ref)
            dbias_ref[...] = jnp.zeros_like(dbias_ref)

        zu = zu_ref[...].astype(F32)
        zv = zv_ref[...].astype(F32)
        gu, gu_grad = _gelu_and_grad(zu)
        gv, gv_grad = _gelu_and_grad(zv)
        xh, r = _ln_stats(gv)
        v16 = (xh * g_ref[...] + b_ref[...]).astype(BF)
        dr = dr_ref[...].astype(F32)
        tri = _tril()
        for h in range(SGU_HEADS):
            cols = slice(h * LANES, (h + 1) * LANES)
            wh = jnp.where(tri, w_ref[h], 0.0).astype(BF)
            wht = jnp.where(_tril(transposed=True), wt_ref[h], 0.0).astype(BF)
            for c in range(tm // CHUNK):
                rows = slice(c * CHUNK, (c + 1) * CHUNK)
                v_blk = v16[rows, cols]
                s = jnp.dot(wh, v_blk, preferred_element_type=F32) + bias_ref[:, cols]
                ds = dr[rows, cols] * gu[rows, cols]
                dgu_s[rows, cols] = dr[rows, cols] * s
                ds16 = ds.astype(BF)
                dw_ref[h] += jnp.where(tri, lax.dot_general(ds16, v_blk, _NT, preferred_element_type=F32), 0.0)
                dv_s[rows, cols] = jnp.dot(wht, ds16, preferred_element_type=F32)
                dbias_ref[:, cols] += ds
        dzu_ref[...] = (dgu_s[...] * gu_grad).astype(dzu_ref.dtype)
        dgv, dg, db = _ln_bwd(xh, r, g_ref[...], dv_s[...])
        dzv_ref[...] = (dgv * gv_grad).astype(dzv_ref.dtype)
        dg_ref[...] += dg
        db_ref[...] += db

        @pl.when(i == n_steps - 1)
        def _():
            for h in range(SGU_HEADS):
                cols = slice(h * LANES, (h + 1) * LANES)
                tot = jnp.sum(dbias_ref[:, cols], axis=1, keepdims=True)
                dbias_ref[:, cols] = jnp.broadcast_to(tot, (CHUNK, LANES))

    small = (ln_g, ln_b, w_s, w_st, bias)
    small_specs = [pl.BlockSpec(a.shape, lambda i, n=a.ndim: (0,) * n) for a in small]
    return pl.pallas_call(
        body, grid=(n_steps,),
        in_specs=[pl.BlockSpec((tm, 512), lambda i: (i, 0)), pl.BlockSpec((tm, 512), lambda i: (i, ZB_U)),
                  pl.BlockSpec((tm, 512), lambda i: (i, ZB_V))] + small_specs,
        out_specs=[pl.BlockSpec((tm, 512), lambda i: (i, 0)), pl.BlockSpec((tm, 512), lambda i: (i, 0)),
                   pl.BlockSpec((1, 512), lambda i: (0, 0)), pl.BlockSpec((1, 512), lambda i: (0, 0)),
                   pl.BlockSpec(w_s.shape, lambda i: (0, 0, 0)), pl.BlockSpec(bias.shape, lambda i: (0, 0))],
        out_shape=[jax.ShapeDtypeStruct((T, 512), BF), jax.ShapeDtypeStruct((T, 512), BF),
                   jax.ShapeDtypeStruct((1, 512), F32), jax.ShapeDtypeStruct((1, 512), F32),
                   jax.ShapeDtypeStruct(w_s.shape, F32), jax.ShapeDtypeStruct(bias.shape, F32)],
        scratch_shapes=[pltpu.VMEM((tm, 512), F32), pltpu.VMEM((tm, 512), F32)],
        name="sgu_bwd", compiler_params=_params(1))(dr, z, z, ln_g, ln_b, w_s, w_st, bias)


def _conv_fwd(z, convk, l, bias):
    T = z.shape[0]

    def body(za_ref, zb_ref, k_ref, b_ref, o_ref):
        xg = za_ref[...].astype(F32) * _sigmoid(zb_ref[...].astype(F32))
        xp = jnp.concatenate([jnp.zeros((CONV_PAD, LANES), F32), xg], axis=0)
        kw = k_ref[...]
        acc = jnp.broadcast_to(b_ref[...], (T, LANES))
        for s in range(SUBLANES):
            xs = xp if s == 0 else pltpu.roll(xp, s, 0)
            for q in range(CONV_PAD // SUBLANES):
                k = CONV_TAPS - 1 - (SUBLANES * q + s)
                if k >= 0:
                    lo = CONV_PAD - SUBLANES * q
                    acc = acc + kw[k:k + 1, :] * xs[lo:lo + T, :]
        o_ref[...] = acc.astype(o_ref.dtype)

    return pl.pallas_call(
        body, grid=(4,),
        in_specs=[pl.BlockSpec((T, LANES), lambda g: (0, 4 * ZB_A + g)),
                  pl.BlockSpec((T, LANES), lambda g: (0, 4 * ZB_B + g)),
                  pl.BlockSpec((None, CONV_PAD, LANES), lambda g: (g, 0, 0)),
                  pl.BlockSpec((1, LANES), lambda g: (0, g))],
        out_specs=pl.BlockSpec((T, LANES), lambda g: (0, g)), out_shape=jax.ShapeDtypeStruct((T, 512), BF),
        name="conv_fwd", compiler_params=_params(1))(z, z, convk[l], bias)


def _conv_bwd(dy, z, convk, l):
    T = z.shape[0]

    def body(dy_ref, za_ref, zb_ref, k_ref, dza_ref, dzb_ref, dk_ref, db_ref):
        a = za_ref[...].astype(F32)
        sg = _sigmoid(zb_ref[...].astype(F32))
        d = dy_ref[...].astype(F32)
        kw = k_ref[...]
        xp = jnp.concatenate([jnp.zeros((CONV_PAD, LANES), F32), a * sg], axis=0)
        dp = jnp.concatenate([d, jnp.zeros((CONV_PAD, LANES), F32)], axis=0)
        dxg = jnp.zeros((T, LANES), F32)
        dk_ref[...] = jnp.zeros_like(dk_ref)
        for s in range(SUBLANES):
            xs = xp if s == 0 else pltpu.roll(xp, s, 0)
            ds = dp if s == 0 else pltpu.roll(dp, T + CONV_PAD - s, 0)
            for q in range(CONV_PAD // SUBLANES):
                k = CONV_TAPS - 1 - (SUBLANES * q + s)
                if k >= 0:
                    lo = CONV_PAD - SUBLANES * q
                    dk_ref[k:k + 1, :] = jnp.sum(d * xs[lo:lo + T, :], axis=0, keepdims=True)
                    dxg = dxg + kw[k:k + 1, :] * ds[SUBLANES * q:SUBLANES * q + T, :]
        db_ref[...] = jnp.sum(d, axis=0, keepdims=True)
        dza_ref[...] = (dxg * sg).astype(dza_ref.dtype)
        dzb_ref[...] = (dxg * a * sg * (1.0 - sg)).astype(dzb_ref.dtype)

    col = pl.BlockSpec((T, LANES), lambda g: (0, g))
    return pl.pallas_call(
        body, grid=(4,),
        in_specs=[col, pl.BlockSpec((T, LANES), lambda g: (0, 4 * ZB_A + g)),
                  pl.BlockSpec((T, LANES), lambda g: (0, 4 * ZB_B + g)),
                  pl.BlockSpec((None, CONV_PAD, LANES), lambda g: (g, 0, 0))],
        out_specs=[col, col, pl.BlockSpec((CONV_PAD, LANES), lambda g: (0, g)),
                   pl.BlockSpec((1, LANES), lambda g: (0, g))],
        out_shape=[jax.ShapeDtypeStruct((T, 512), BF), jax.ShapeDtypeStruct((T, 512), BF),
                   jax.ShapeDtypeStruct((CONV_PAD, 512), F32), jax.ShapeDtypeStruct((1, 512), F32)],
        name="conv_bwd", compiler_params=_params(1))(dy, z, z, convk[l])


D = D_MODEL


def _ffn_fwd(l, h, S, W, pre, deps=()):
    n, gp, u, a = _norm_mm("ffn_in", h, S[pre + "_pre_g"], [W[pre + "_w_gate"][l], W[pre + "_w_up"][l]], True,
                           act=True, deps=deps)
    f, out = _mm_res("ffn_out", a, W[pre + "_w_down"][l], h, S[pre + "_post_g"], 0.5)
    return out, dict(h=h, n=n, gp=gp, u=u, a=a, f=f)


def _ffn_bwd(l, dh, sv, S, W, G, SG, pre, deps=()):
    df, SG[pre + "_post_g"], dgp, du, dwd = _resbwd_mm(
        "ffn_bwd_act", dh, sv["f"], S[pre + "_post_g"], 0.5, W[pre + "_w_down"][l], True,
        act=(sv["gp"], sv["u"], sv["a"]), deps=deps, tm=ROW_TILE)
    G[pre + "_w_down"] = G[pre + "_w_down"][:l] + [dwd] + G[pre + "_w_down"][l + 1:]
    G[pre + "_w_gate"] = _mm_tn("ffn_dw_gate", dgp, sv["n"], G[pre + "_w_gate"], l, True)
    G[pre + "_w_up"] = _mm_tn("ffn_dw_up", du, sv["n"], G[pre + "_w_up"], l, True)
    dh_in, SG[pre + "_pre_g"] = _dn_prenorm("ffn_bwd_in", [dgp, du], [W[pre + "_w_gate"][l], W[pre + "_w_up"][l]],
                                            False, dh, sv["h"], S[pre + "_pre_g"])
    return dh_in


def _gates(zg):
    return [_sigmoid(jnp.concatenate([zg[2 * k].astype(F32), zg[2 * k + 1].astype(F32)], axis=1)) for k in range(3)]


def _merge_fwd(z, rs, ws, tm=ROW_TILE):
    T = z.shape[0]
    tm = _tile(T, tm)
    nb, kk, bw = ws[0].shape

    def body(*refs):
        r_refs, g_refs, w_refs, y_refs, m_ref = refs[:3], refs[3:9], refs[9:12], refs[12:15], refs[15]
        for r_ref, w_ref, y_ref in zip(r_refs, w_refs, y_refs):
            for b in range(nb):
                y_ref[:, b * bw:(b + 1) * bw] = jnp.dot(r_ref[...], w_ref[b],
                                                        preferred_element_type=F32).astype(y_ref.dtype)
        g = _gates([q[...] for q in g_refs])
        m_ref[...] = (g[0] * y_refs[0][...].astype(F32) + g[1] * y_refs[1][...].astype(F32)
                      + g[2] * y_refs[2][...].astype(F32)).astype(m_ref.dtype)

    row = pl.BlockSpec((tm, D_MODEL), lambda i: (i, 0))
    return pl.pallas_call(
        body, grid=(T // tm,),
        in_specs=[pl.BlockSpec((tm, kk), lambda i: (i, 0))] * 3
        + [pl.BlockSpec((tm, 512), lambda i, j=j: (i, ZB_GATES + j)) for j in range(6)]
        + [pl.BlockSpec(ws[0].shape, lambda i: (0, 0, 0))] * 3,
        out_specs=[row] * 4, out_shape=[jax.ShapeDtypeStruct((T, D_MODEL), BF)] * 4,
        name="mix_merge", compiler_params=_params(1))(*rs, *[z] * 6, *ws)


def _merge_bwd(dmerged, z, ys, ws, tm=ROW_TILE // 2):
    T = z.shape[0]
    tm = _tile(T, tm)
    nb, kk, bw = ws[0].shape

    def body(*refs):
        dm_ref, g_refs, y_refs, w_refs = refs[0], refs[1:7], refs[7:10], refs[10:13]
        dy_refs, lo_ref, hi_ref, dr_refs = refs[13:16], refs[16], refs[17], refs[18:21]
        cut = DZ_HALF - ZB_GATES * 512
        dm = dm_ref[...].astype(F32)
        g = _gates([q[...] for q in g_refs])
        for k in range(3):
            dy_refs[k][...] = (dm * g[k]).astype(BF)
            dzg = (dm * y_refs[k][...].astype(F32) * g[k] * (1.0 - g[k])).astype(BF)
            if k == 0:
                lo_ref[...] = dzg[:, :cut]
                hi_ref[:, :D_MODEL - cut] = dzg[:, cut:]
            else:
                hi_ref[:, k * D_MODEL - cut:(k + 1) * D_MODEL - cut] = dzg
            dr = None
            for b in range(nb):
                p = lax.dot_general(dy_refs[k][:, b * bw:(b + 1) * bw], w_refs[k][b], _NT,
                                    preferred_element_type=F32)
                dr = p if dr is None else dr + p
            dr_refs[k][...] = dr.astype(BF)

    row = pl.BlockSpec((tm, D_MODEL), lambda i: (i, 0))
    return pl.pallas_call(
        body, grid=(T // tm,),
        in_specs=[row] + [pl.BlockSpec((tm, 512), lambda i, j=j: (i, ZB_GATES + j)) for j in range(6)] + [row] * 3
        + [pl.BlockSpec(ws[0].shape, lambda i: (0, 0, 0))] * 3,
        out_specs=[row] * 3 + [pl.BlockSpec((tm, DZ_HALF - ZB_GATES * 512), lambda i: (i, 0)),
                               pl.BlockSpec((tm, DZ_HALF), lambda i: (i, 0))]
        + [pl.BlockSpec((tm, kk), lambda i: (i, 0))] * 3,
        out_shape=[jax.ShapeDtypeStruct((T, D_MODEL), BF)] * 3
        + [jax.ShapeDtypeStruct((T, DZ_HALF - ZB_GATES * 512), BF), jax.ShapeDtypeStruct((T, DZ_HALF), BF)]
        + [jax.ShapeDtypeStruct((T, kk), BF)] * 3,
        name="mix_merge_bwd", compiler_params=_params(1))(dmerged, *[z] * 6, *ys, *ws)


def _mix_fwd(l, h, S, W, deps=()):
    n, z = _norm_mm("mix_in", h, S["mix_pre_g"], [W["w_in"][l]], False, deps=deps)
    r_pool = _pool_fwd(z, S["pool_w"], S["pool_scale"])
    r_sgu = _sgu_fwd(z, S["sgu_ln_g"], S["sgu_ln_b"], S["sgu_w_s"], S["sgu_bias"])
    yc = _conv_fwd(z, W["conv_dw_k"], l, S["conv_dw_b"])

    def ln_silu(y, g, b):
        xh, _ = _ln_stats(y.astype(F32))
        return (_silu_and_grad(xh * g + b)[0],)

    r_conv = _rowwise("conv_ln", ln_silu, [(yc, 512, 0)], [S["conv_ln_g"], S["conv_ln_b"]], [(512, BF)])[0]
    y_pool, y_sgu, y_conv, merged = _merge_fwd(z, (r_pool, r_sgu, r_conv),
                                               [W["w_%s_out" % br][l] for br in ("pool", "sgu", "conv")])
    o, out = _mm_res("mix_out", merged, W["w_out"][l], h, S["mix_post_g"], 1.0)
    return out, dict(h=h, n=n, z=z, r_pool=r_pool, r_sgu=r_sgu, yc=yc, r_conv=r_conv, y_pool=y_pool, y_sgu=y_sgu,
                     y_conv=y_conv, merged=merged, o=o)


def _branch_dw(rs, dys, shape):
    T, kk = rs[0].shape
    nb, _, bw = shape

    def body(*refs):
        for k in range(3):
            refs[6 + k][...] = lax.dot_general(refs[k][...], refs[3 + k][...], _TN,
                                               preferred_element_type=F32).astype(BF)

    return pl.pallas_call(
        body, grid=(nb,),
        in_specs=[pl.BlockSpec((T, kk), lambda b: (0, 0))] * 3 + [pl.BlockSpec((T, bw), lambda b: (0, b))] * 3,
        out_specs=[pl.BlockSpec((None, kk, bw), lambda b: (b, 0, 0))] * 3,
        out_shape=[jax.ShapeDtypeStruct((nb, kk, bw), BF)] * 3, name="branch_dw",
        compiler_params=_params(1))(*rs, *dys)


def _mix_bwd(l, dh, sv, S, W, G, SG, deps=()):
    z = sv["z"]
    do, SG["mix_post_g"], dmerged = _resbwd_mm("mix_bwd_out", dh, sv["o"], S["mix_post_g"], 1.0,
                                               W["w_out"][l].reshape(1, D, D), True, deps=deps)
    G["w_out"] = _mm_tn("mix_dw_out", sv["merged"], do, G["w_out"], l, True)

    branches = ("pool", "sgu", "conv")
    res = _merge_bwd(dmerged, z, [sv["y_" + br] for br in branches], [W["w_%s_out" % br][l] for br in branches])
    dz_gate_lo, dz_hi, dr = res[3], res[4], dict(zip(branches, res[5:]))
    for br, dw in zip(branches, _branch_dw([sv["r_" + br] for br in branches], res[:3], G["w_pool_out"][l].shape)):
        wn = "w_%s_out" % br
        G[wn] = G[wn][:l] + [dw] + G[wn][l + 1:]
    dz_pool, SG["pool_w"], SG["pool_scale"] = _pool_bwd(dr["pool"], z, S["pool_w"], S["pool_scale"])
    dzu, dzv, SG["sgu_ln_g"], SG["sgu_ln_b"], SG["sgu_w_s"], dbias = _sgu_bwd(
        dr["sgu"], z, S["sgu_ln_g"], S["sgu_ln_b"], S["sgu_w_s"], S["sgu_w_st"], S["sgu_bias"])
    SG["sgu_b_s"] = dbias[:, ::LANES].T

    def ln_silu_bwd(d, y, g, b):
        xh, r = _ln_stats(y.astype(F32))
        _, grad = _silu_and_grad(xh * g + b)
        return _ln_bwd(xh, r, g, d.astype(F32) * grad)

    dyc, SG["conv_ln_g"], SG["conv_ln_b"] = _rowwise(
        "conv_ln_bwd", ln_silu_bwd, [(dr["conv"], 512, 0), (sv["yc"], 512, 0)], [S["conv_ln_g"], S["conv_ln_b"]],
        [(512, BF)], [(1, 512), (1, 512)])
    dza, dzb, SG["conv_dw_k"], SG["conv_dw_b"] = _conv_bwd(dyc, z, W["conv_dw_k"], l)
    dz_lo = jnp.concatenate([dz_pool, dzu, dzv, dza, dzb, dz_gate_lo], axis=1)
    G["w_in"] = _mm_tn("mix_dw_in", sv["n"], dz_lo, G["w_in"], l, False)
    G["w_in"] = _mm_tn("mix_dw_in", sv["n"], dz_hi, G["w_in"], l, False, first=2)
    dh_in, SG["mix_pre_g"] = _dn_prenorm("mix_bwd_in", [dz_lo, dz_hi], W["w_in"][l], True, dh, sv["h"],
                                         S["mix_pre_g"])
    return dh_in


def _ple_out(h, p, gp, w3, g, tm=ROW_TILE):
    T, kp = p.shape
    nb, _, bw = w3.shape
    tm = _tile(T, tm)

    def body(h_ref, p_ref, gp_ref, w_ref, g_ref, e_ref, o_ref):
        p16 = p_ref[...].astype(BF)
        for b in range(nb):
            e_ref[:, b * bw:(b + 1) * bw] = jnp.dot(p16, w_ref[b], preferred_element_type=F32).astype(BF)
        q = _sigmoid(gp_ref[...].astype(F32)) * e_ref[...].astype(F32)
        o_ref[...] = h_ref[...] + _rms_fwd(q, g_ref[...])

    row = pl.BlockSpec((tm, D_MODEL), lambda i: (i, 0))
    return pl.pallas_call(
        body, grid=(T // tm,),
        in_specs=[row, pl.BlockSpec((tm, kp), lambda i: (i, 0)), row, pl.BlockSpec(w3.shape, lambda i: (0, 0, 0)),
                  pl.BlockSpec(g.shape, lambda i: (0, 0))],
        out_specs=[row, row],
        out_shape=[jax.ShapeDtypeStruct((T, D_MODEL), BF), jax.ShapeDtypeStruct((T, D_MODEL), F32)],
        name="ple_out", compiler_params=_params(1))(h, p, gp, w3, g)


def _ple_fwd(l, h, p_l, S, W, deps=()):
    n, gp = _norm_mm("ple_in", h, S["ple_pre_g"], [W["ple_w_gate"][l].reshape(1, D, D)], False, deps=deps)
    e, out = _ple_out(h, p_l, gp, W["ple_w_proj"][l], S["ple_post_g"])
    return out, dict(h=h, n=n, e=e, gp=gp, p=p_l)


def _ple_bwd_rows(dh, e, gp, g_post, w3, h, g_pre, deps=(), tm=ROW_TILE):
    T = dh.shape[0]
    w2 = w3.reshape(D_MODEL, D_MODEL)
    tm = _tile(T, tm)
    n_dep = len(deps)

    def body(*refs):
        dh_ref, e_ref, gp_ref, gpost_ref, w_ref, h_ref, gpre_ref, de_ref, dgp_ref, o_ref, dpost_ref, dpre_ref = \
            refs[n_dep:]
        first = pl.program_id(0) == 0
        d = dh_ref[...]
        sg = _sigmoid(gp_ref[...].astype(F32))
        ee = e_ref[...].astype(F32)
        dq, dpost = _rms_bwd(sg * ee, gpost_ref[...], d)
        de_ref[...] = (dq * sg).astype(BF)
        dgp = (dq * ee * sg * (1.0 - sg)).astype(BF)
        dgp_ref[...] = dgp
        dn = lax.dot_general(dgp, w_ref[...], _NT, preferred_element_type=F32)
        dx, dpre = _rms_bwd(h_ref[...], gpre_ref[...], dn)
        o_ref[...] = d + dx
        _acc_rows(dpost_ref, dpost, first)
        _acc_rows(dpre_ref, dpre, first)

    row = pl.BlockSpec((tm, D_MODEL), lambda i: (i, 0))
    vec = pl.BlockSpec((1, D_MODEL), lambda i: (0, 0))
    return pl.pallas_call(
        body, grid=(T // tm,),
        in_specs=[ANY] * n_dep + [row, row, row, vec, pl.BlockSpec(w2.shape, lambda i: (0, 0)), row, vec],
        out_specs=[row, row, row, vec, vec],
        out_shape=[jax.ShapeDtypeStruct((T, D_MODEL), BF)] * 2 + [jax.ShapeDtypeStruct((T, D_MODEL), F32)]
        + [jax.ShapeDtypeStruct((1, D_MODEL), F32)] * 2,
        name="ple_bwd", compiler_params=_params(1))(*deps, dh, e, gp, g_post, w2, h, g_pre)


def _ple_bwd(l, dh, sv, S, W, G, SG, deps=()):
    de, dgp, dh_in, SG["ple_post_g"], SG["ple_pre_g"] = _ple_bwd_rows(
        dh, sv["e"], sv["gp"], S["ple_post_g"], W["ple_w_gate"][l], sv["h"], S["ple_pre_g"], deps)
    G["ple_w_proj"] = _mm_tn("ple_dw_proj", sv["p"], de, G["ple_w_proj"], l, False)
    G["ple_w_gate"] = _mm_tn("ple_dw_gate", sv["n"], dgp, G["ple_w_gate"], l, True)
    return dh_in


def _layer_small(a, l):
    S = {}
    for name in SMALL:
        v = a[name][l]
        S[name] = v.reshape(1, -1) if v.ndim == 1 else v
    S["sgu_w_st"] = jnp.swapaxes(S["sgu_w_s"], 1, 2)
    S["sgu_bias"] = jnp.repeat(S["sgu_b_s"].T, LANES, axis=1)
    return S


def _layer_fwd(l, h, p_l, S, W, deps=(), hooks=None):
    hooks = hooks or {}

    def after(part, hv):
        return hooks[part](hv) if part in hooks else ()

    h, sv1 = _ffn_fwd(l, h, S, W, "ffn1", deps)
    h, sv2 = _mix_fwd(l, h, S, W, after("ffn1", h))
    h, sv3 = _ffn_fwd(l, h, S, W, "ffn2", after("mix", h))
    h, sv4 = _ple_fwd(l, h, p_l, S, W, after("ffn2", h))
    return h, (sv1, sv2, sv3, sv4)


def _layer_bwd(l, dh, sv, S, W, G, deps=(), hooks=None):
    hooks = hooks or {}

    def after(part, dv):
        return hooks[part](dv) if part in hooks else ()

    SG = {}
    dh = _ple_bwd(l, dh, sv[3], S, W, G, SG, deps)
    dh = _ffn_bwd(l, dh, sv[2], S, W, G, SG, "ffn2")
    dh = _mix_bwd(l, dh, sv[1], S, W, G, SG, after("ffn2", dh))
    dh = _ffn_bwd(l, dh, sv[0], S, W, G, SG, "ffn1", after("mix", dh))
    return dh, SG


HBM = pl.BlockSpec(memory_space=pltpu.HBM)
SEM = pl.BlockSpec(memory_space=pltpu.SEMAPHORE)
SIDE_EFFECT = pltpu.SideEffectType.DATAFLOW_SIDE_EFFECTING


def _place():
    x, y, c = lax.axis_index("x"), lax.axis_index("y"), lax.axis_index("c")
    chips = [(1 - x, y), (x, 1 - y), (1 - x, 1 - y)]
    return x, y, c, chips


def _remote(src, dst, send_sem, recv_sem, to):
    return pltpu.make_async_remote_copy(src_ref=src, dst_ref=dst, send_sem=send_sem, recv_sem=recv_sem,
                                        device_id=to, device_id_type=MESH)


def _split_start(name, plan, bufs, deps):
    count, fn = plan
    n, nd = len(bufs), len(deps)

    def body(*refs):
        send, recv = refs[nd + n], refs[nd + n + 1]
        x, y, c, chips = _place()
        for k, (src, dst, _, to) in enumerate(fn(refs[nd:nd + n], x, y, c, chips)):
            _remote(src, dst, send.at[k], recv.at[k], to).start()
        refs[-1][...] = jnp.zeros_like(refs[-1])

    res = pl.pallas_call(
        body, in_specs=[ANY] * nd + [HBM] * n,
        out_specs=[SEM, SEM] + [HBM] * n + [pl.BlockSpec(memory_space=pltpu.VMEM)],
        out_shape=[pltpu.SemaphoreType.DMA((count,)), pltpu.SemaphoreType.DMA((count,))]
        + [pltpu.HBM(b.shape, b.dtype) for b in bufs] + [jax.ShapeDtypeStruct((8, LANES), F32)],
        input_output_aliases={nd + i: 2 + i for i in range(n)}, name=name,
        compiler_params=pltpu.CompilerParams(has_side_effects=SIDE_EFFECT),
    )(*deps, *[pltpu.with_memory_space_constraint(b, pltpu.HBM) for b in bufs])
    return (res[0], res[1]), list(res[2:2 + n]), res[-1]


def _split_wait(name, plan, sems, bufs, after):
    _, fn = plan
    n = len(bufs)

    def body(*refs):
        send, recv = refs[n], refs[n + 1]
        x, y, c, chips = _place()
        for k, (src, _, land, to) in enumerate(fn(refs[:n], x, y, c, chips)):
            cp = _remote(src, land, send.at[k], recv.at[k], to)
            cp.wait_send()
            cp.wait_recv()

    res = pl.pallas_call(
        body, in_specs=[HBM] * n + [SEM, SEM] + [ANY] * len(after), out_specs=[HBM] * n,
        out_shape=[pltpu.HBM(b.shape, b.dtype) for b in bufs], input_output_aliases={i: i for i in range(n)},
        name=name, compiler_params=pltpu.CompilerParams(has_side_effects=SIDE_EFFECT))(*bufs, *sems, *after)
    return list(res)


def _gather_plans(n):
    def across(b, x, y, c, chips):
        me, out = 2 * x + y, []
        for a in range(n):
            rh = b[a].shape[1] // 2
            mine = b[a].at[me, pl.ds(c * rh, rh)]
            for cx, cy in chips:
                out.append((mine, mine, b[a].at[2 * cx + cy, pl.ds(c * rh, rh)], (cx, cy, c)))
        return out

    def to_sibling(b, x, y, c, chips):
        out = []
        for a in range(n):
            rh = b[a].shape[1] // 2
            for cx, cy in chips:
                piece = b[a].at[2 * cx + cy, pl.ds(c * rh, rh)]
                out.append((piece, piece, b[a].at[2 * cx + cy, pl.ds((1 - c) * rh, rh)], (x, y, 1 - c)))
        return out

    return (3 * n, across), (3 * n, to_sibling)


def _pair_plan(n):
    def fn(b, x, y, c, chips):
        out = []
        for a in range(n):
            rh = b[a].shape[1] // 2
            out.append((b[a].at[:, pl.ds((1 - c) * rh, rh)], b[n + a], b[n + a], (x, y, 1 - c)))
        return out

    return n, fn


def _cross_plan(n):
    def fn(b, x, y, c, chips):
        out = []
        for a in range(n):
            for j, (cx, cy) in enumerate(chips):
                out.append((b[a].at[2 * cx + cy], b[n + a].at[j], b[n + a].at[j], (cx, cy, c)))
        return out

    return 3 * n, fn


def _share_plan(n, l):
    def fn(b, x, y, c, chips):
        out = []
        for a in range(n):
            rh = b[a].shape[1] // 2
            mine = b[a].at[l, pl.ds(c * rh, rh)]
            out.append((mine, mine, b[a].at[l, pl.ds((1 - c) * rh, rh)], (x, y, 1 - c)))
        return out

    return n, fn


def _peers(x, y, c):
    return [(1 - x if m & 4 else x, 1 - y if m & 2 else y, 1 - c if m & 1 else c) for m in range(1, 8)]


def _small_plans():
    def scatter(b, x, y, c, chips):
        return [(b[0].at[4 * px + 2 * py + pc], b[1].at[m], b[1].at[m], (px, py, pc))
                for m, (px, py, pc) in enumerate(_peers(x, y, c))]

    def gather(b, x, y, c, chips):
        mine = b[0].at[4 * x + 2 * y + c]
        return [(mine, mine, b[0].at[4 * px + 2 * py + pc], (px, py, pc)) for px, py, pc in _peers(x, y, c)]

    return (7, scatter), (7, gather)


def _sum_small(v3, got, pos):
    rs = v3.shape[1]
    tm = _tile(rs)
    ins = [(v3, (None, tm, LANES), lambda i, p: (p[2], i, 0))]
    ins += [(got, (None, tm, LANES), lambda i, p, m=m: (m, i, 0)) for m in range(7)]
    return _tiled("sum_small", lambda *t: (((((((t[0] + t[1]) + t[2]) + t[3]) + t[4]) + t[5]) + t[6]) + t[7],),
                  (rs // tm,), pos, ins, [((8, rs, LANES), F32, (None, tm, LANES), lambda i, p: (p[2], i, 0))])[0]


ADD_ROWS = 128


def _multi_tiled(name, fn, pos, groups, in_place=False):
    steps = max(g[1] for g in groups)
    flat_in, in_specs, out_specs, out_shape, counts, dests = [], [], [], [], [], []
    for ins, n_t, (shape, dtype, oidx, dest) in groups:
        for arr, idx in ins:
            flat_in.append(arr)
            in_specs.append(pl.BlockSpec((ADD_ROWS, arr.shape[1]),
                                         lambda i, p, idx=idx, n_t=n_t: (idx(jnp.minimum(i, n_t - 1), p), 0)))
        out_specs.append(pl.BlockSpec((ADD_ROWS, shape[1]),
                                      lambda i, p, oidx=oidx, n_t=n_t: (oidx(jnp.minimum(i, n_t - 1), p), 0)))
        out_shape.append(jax.ShapeDtypeStruct(shape, dtype))
        counts.append((len(ins), n_t))
        dests.append(dest)
    n_in = len(flat_in)
    extra = dests if in_place else []

    def body(_, *refs):
        outs = refs[n_in + len(extra):]
        k = 0
        for (n_a, n_t), o_ref in zip(counts, outs):
            tiles = refs[k:k + n_a]
            k += n_a

            @pl.when(pl.program_id(0) < n_t)
            def _(tiles=tiles, o_ref=o_ref):
                o_ref[...] = fn(*[t[...] for t in tiles]).astype(o_ref.dtype)

    spec = pltpu.PrefetchScalarGridSpec(num_scalar_prefetch=1, grid=(steps,),
                                        in_specs=in_specs + [ANY] * len(extra), out_specs=out_specs)
    return pl.pallas_call(body, grid_spec=spec, out_shape=out_shape,
                          input_output_aliases={1 + n_in + k: k for k in range(len(extra))}, name=name,
                          compiler_params=_params(1))(pos, *flat_in, *extra)


def _add_pair(grads, got, pos):
    groups = []
    for g, q in zip(grads, got):
        nb, R, C = g.shape
        rh = R // 2
        nh = rh // ADD_ROWS
        groups.append(([(g.reshape(nb * R, C), lambda t, p, nh=nh: (t // nh) * 2 * nh + p[1] * nh + t % nh),
                        (q.reshape(nb * rh, C), lambda t, p: t)], nb * nh,
                       ((nb * rh, C), BF, lambda t, p: t, None)))
    res = _multi_tiled("rs_add_pair", lambda u, w: u.astype(F32) + w.astype(F32), pos, groups)
    return [t.reshape(q.shape) for t, q in zip(res, got)]


def _add_chips(parts, slots, reduced, l, pos):
    def add(own, s0, s1, s2):
        return ((own.astype(F32) + s0.astype(F32)) + s1.astype(F32)) + s2.astype(F32)

    groups = []
    for t, s, red in zip(parts, slots, reduced):
        nb, rh, C = t.shape
        L = red.shape[0]
        nh = rh // ADD_ROWS
        ins = [(t.reshape(nb * rh, C), lambda i, p, nh=nh: p[0] * nh + i)]
        ins += [(s.reshape(3 * rh, C), lambda i, p, j=j, nh=nh: j * nh + i) for j in range(3)]
        groups.append((ins, nh, ((L * 2 * rh, C), F32, lambda i, p, nh=nh: l * 2 * nh + p[1] * nh + i,
                                 red.reshape(L * 2 * rh, C))))
    res = _multi_tiled("rs_add_chips", add, pos, groups, in_place=True)
    return [buf.reshape(red.shape) for buf, red in zip(res, reduced)]


def _adamw_math(w, g, m, v):
    m = ADAM_B1 * m + (1.0 - ADAM_B1) * g
    v = ADAM_B2 * v + (1.0 - ADAM_B2) * (g * g)
    m_hat = m / (1.0 - ADAM_B1 ** ADAM_STEP)
    v_hat = v / (1.0 - ADAM_B2 ** ADAM_STEP)
    return -ADAM_LR * (m_hat / (jnp.sqrt(v_hat) + ADAM_EPS) + ADAM_WD * w), m, v


def _adamw(w, g, m, v, lo=0, hi=None, into=None, deps=()):
    L, R, C = w.shape
    hi = L if hi is None else hi
    tr = _tile(R, max(16, ADAM_TILE_ELEMS // C))
    extra = (list(into) if into else []) + list(deps)
    n_alias = 4 if into else 0

    def body(w_ref, g_ref, m_ref, v_ref, *rest):
        go_ref, d_ref, mo_ref, vo_ref = rest[len(extra):]
        gv = g_ref[...]
        d, mn, vn = _adamw_math(w_ref[...], gv, m_ref[...], v_ref[...])
        go_ref[...] = gv
        d_ref[...] = d
        mo_ref[...] = mn
        vo_ref[...] = vn

    spec = pl.BlockSpec((None, tr, C), lambda l, i: (l + lo, i, 0))
    out = jax.ShapeDtypeStruct(w.shape, F32)
    return pl.pallas_call(body, grid=(hi - lo, R // tr), in_specs=[spec] * 4 + [ANY] * len(extra),
                          out_specs=[spec] * 4, out_shape=[out] * 4,
                          input_output_aliases={4 + k: k for k in range(n_alias)}, name="adamw",
                          compiler_params=_params(2))(w, g, m, v, *extra)


def _pack(parts):
    flat = jnp.concatenate([q.reshape(-1, LANES) for q in parts], axis=0)
    return jnp.pad(flat, ((0, -flat.shape[0] % ROW_TILE), (0, 0)))


def _unpack(flat, like):
    out, r = [], 0
    for q in like:
        n = q.size // LANES
        out.append(flat[r:r + n].reshape(q.shape))
        r += n
    return out


def _train_step(a):
    a = dict(a)
    L = a["ffn1_pre_g"].shape[0]
    x, y, c, _ = _place()
    chip = 2 * x + y
    pos = jnp.stack([chip, c, 2 * chip + c]).astype(jnp.int32)
    for name in TRANSPOSED:
        for pre in ("", "m_", "v_"):
            a[pre + name] = jnp.swapaxes(a[pre + name], 1, 2)
    big = [b[0] for b in BIG]
    gathered = big + ["conv_dw_k"]
    n_w, n_g = len(gathered), len(big)

    own = [None] * n_w
    W = {name: [None] * L for name in gathered}
    every = list(range(n_w))
    first, mixer, later = every[:3], every[3:8] + [n_g], every[8:n_g]
    rest = mixer + later

    def cast(i, deps):
        if i == n_g:
            taps = a["conv_dw_k"].reshape(L, CONV_TAPS, LANES)
            return _cast_layers("pad_conv_taps", taps, CONV_PAD, LANES, F32, pos, deps)
        name, _, _, _, rp, cp = BIG[i]
        return _cast_layers("cast_weight", a[name], rp, cp, BF, pos, deps)

    def gather_first(l, ids, tag, deps):
        return _split_start("gather_a%d%s" % (l, tag), _gather_plans(len(ids))[0], [own[i][l] for i in ids], deps)

    def gather_second(l, ids, tag, state, after):
        across, to_sibling = _gather_plans(len(ids))
        bufs = _split_wait("gather_a%d%s_done" % (l, tag), across, state[0], state[1], after)
        return _split_start("gather_b%d%s" % (l, tag), to_sibling, bufs, [])

    def gather_done(l, ids, tag, state, after):
        to_sibling = _gather_plans(len(ids))[1]
        bufs = _split_wait("gather_b%d%s_done" % (l, tag), to_sibling, state[0], state[1], after)
        for i, buf in zip(ids, bufs):
            W[gathered[i]][l] = buf

    for i in first:
        own[i] = cast(i, ())
    state = gather_first(0, first, "f", [])
    for i in rest:
        own[i] = cast(i, (state[2],))
    state = gather_second(0, first, "f", state, [own[i][0] for i in rest])
    gather_done(0, first, "f", state, [])

    parts = {"f": first, "m": mixer, "t": later}
    flying = {}

    def begin(l, part, deps):
        flying[l, part] = gather_first(l, parts[part], part, deps)
        return flying[l, part][2]

    def hand_on(l, part, after):
        flying[l, part] = gather_second(l, parts[part], part, flying[l, part], after)
        return flying[l, part][2]

    def arrive(l, part, after):
        gather_done(l, parts[part], part, flying.pop((l, part)), after)

    def hooks_of(l):
        nxt = l + 1 < L

        def after_ffn1(hv):
            tokens = []
            if l == 0:
                hand_on(0, "m", [hv])
            arrive(l, "m", [hv])
            if l == 0:
                tokens.append(begin(0, "t", [hv]))
            else:
                tokens.append(hand_on(l, "t", [hv]))
            if nxt:
                tokens.append(begin(l + 1, "f", tokens[-1:]))
            return tuple(tokens)

        def after_mix(hv):
            tokens = []
            if l == 0:
                hand_on(0, "t", [hv])
            arrive(l, "t", [hv])
            if nxt:
                tokens.append(hand_on(l + 1, "f", [hv]))
                tokens.append(begin(l + 1, "m", tokens[-1:]))
            return tuple(tokens)

        def after_ffn2(hv):
            tokens = []
            if nxt:
                arrive(l + 1, "f", [hv])
                tokens.append(hand_on(l + 1, "m", [hv]))
                tokens.append(begin(l + 1, "t", tokens[-1:]))
            return tuple(tokens)

        return {"ffn1": after_ffn1, "mix": after_mix, "ffn2": after_ffn2}

    small = [_layer_small(a, l) for l in range(L)]
    h, saved = a["x"][0], []
    deps = (begin(0, "m", []),)
    for l in range(L):
        h, sv = _layer_fwd(l, h, a["p"][l, 0], small[l], W, deps, hooks_of(l))
        saved.append(sv)
        deps = ()

    def loss_fn(yv, t):
        e = yv - t
        return e * (1.0 / D), jnp.sum(e * e, axis=0, keepdims=True)

    dh, lsum = _rowwise("loss", loss_fn, [(h, D, 0), (a["loss_target"][0], D, 0)], [], [(D, F32)], [(1, D)])
    loss = lax.psum(0.5 * jnp.sum(lsum) / D, ("x", "y", "c"))

    G = {name: [jax.ShapeDtypeStruct((N_CHIPS, rp, cp), BF)] * L for name, _, _, _, rp, cp in BIG}
    reduced = [lax.empty((L, rp, cp), F32) for _, _, _, _, rp, cp in BIG]
    small_grads = [None] * L
    whole = list(range(n_g))
    piece_a, piece_b, piece_c = whole[8:], whole[3:8], whole[:3]

    def pair_start(l, ids, tag, deps):
        grads = [G[big[i]][l] for i in ids]
        lands = [lax.empty((N_CHIPS, g.shape[1] // 2, g.shape[2]), BF) for g in grads]
        return _split_start("rs_pair%d%s" % (l, tag), _pair_plan(len(ids)), grads + lands, deps)

    def cross_start(l, ids, tag, state, after):
        n = len(ids)
        bufs = _split_wait("rs_pair%d%s_done" % (l, tag), _pair_plan(n), state[0], state[1], after)
        parts = _add_pair(bufs[:n], bufs[n:], pos)
        lands = [lax.empty((3,) + t.shape[1:], BF) for t in parts]
        return _split_start("rs_cross%d%s" % (l, tag), _cross_plan(n), parts + lands, [])

    def cross_finish(l, ids, tag, state, after, reduced):
        n = len(ids)
        bufs = _split_wait("rs_cross%d%s_done" % (l, tag), _cross_plan(n), state[0], state[1], after)
        reduced = list(reduced)
        for i, r in zip(ids, _add_chips(bufs[:n], bufs[n:], [reduced[i] for i in ids], l, pos)):
            reduced[i] = r
        return reduced

    def share_start(l, reduced):
        return _split_start("rs_share%d" % l, _share_plan(n_g, l), reduced, [])

    def share_done(l, state, after):
        return _split_wait("rs_share%d_done" % l, _share_plan(n_g, l), state[0], state[1], after)

    st_pair = st_share = None
    for l in reversed(range(L)):
        deps = tuple(s[2] for s in (st_pair, st_share) if s is not None)
        box = {}

        def after_ffn2(dm, l=l, box=box, st_pair=st_pair, st_share=st_share):
            out = []
            if st_share is not None:
                box["reduced"] = share_done(l + 2, st_share, [dm])
            if st_pair is not None:
                box["cross"] = cross_start(l + 1, whole, "", st_pair, [dm])
                out.append(box["cross"][2])
            if l == 0:
                box["pair_a"] = pair_start(0, piece_a, "a", [dm])
                out.append(box["pair_a"][2])
            return tuple(out)

        def after_mix(dm, box=box):
            box["cross_a"] = cross_start(0, piece_a, "a", box["pair_a"], [dm])
            box["pair_b"] = pair_start(0, piece_b, "b", [dm])
            return (box["cross_a"][2], box["pair_b"][2])

        hooks = {"ffn2": after_ffn2, "mix": after_mix} if l == 0 else {"ffn2": after_ffn2}
        dh, small_grads[l] = _layer_bwd(l, dh, saved[l], small[l], W, G, deps, hooks)
        if st_share is not None:
            reduced = box["reduced"]
        st_share = None
        if "cross" in box:
            reduced = cross_finish(l + 1, whole, "", box["cross"], [dh], reduced)
            st_share = share_start(l + 1, reduced)
        st_pair = pair_start(l, whole, "", [dh]) if l else None
    grad_x = dh
    cross_b = cross_start(0, piece_b, "b", box["pair_b"], [dh])
    cross_c = cross_start(0, piece_c, "c", pair_start(0, piece_c, "c", [cross_b[2]]), [])
    small_names = SMALL + ("conv_dw_k",)
    stacked = [jnp.stack([small_grads[l][name] for l in range(L)]) for name in small_names]
    packed = _pack(stacked)
    v3 = packed.reshape(8, packed.shape[0] // 8, LANES)
    scatter, gather = _small_plans()
    st_small = _split_start("small_scatter", scatter, [v3, lax.empty((7,) + v3.shape[1:], F32)], [cross_c[2]])
    if st_share is not None:
        reduced = share_done(1, st_share, [st_small[2]])
    upper, token = {}, st_small[2]
    for k, (name, red) in enumerate(zip(big, reduced)):
        if k == 3:
            bufs = _split_wait("small_scatter_done", scatter, st_small[0], st_small[1], [token])
            st_small = _split_start("small_gather", gather, [_sum_small(bufs[0], bufs[1], pos)], [])
            token = st_small[2]
        if L > 1:
            upper[name] = _adamw(a[name], red, a["m_" + name], a["v_" + name], 1, L, deps=[token])
            token = upper[name][1]
    total = _split_wait("small_gather_done", gather, st_small[0], st_small[1], [token])[0]
    summed = dict(zip(small_names, _unpack(total.reshape(packed.shape), stacked)))
    done = [r[1] for r in upper.values()] + [summed[small_names[0]]]
    for ids, tag, state in ((piece_a, "a", box["cross_a"]), (piece_b, "b", cross_b), (piece_c, "c", cross_c)):
        reduced = cross_finish(0, ids, tag, state, done, reduced)
    reduced = share_done(0, share_start(0, reduced), [])
    big_grads = dict(zip(big, reduced))

    grads, deltas, new_m, new_v = {}, {}, {}, {}
    for name in big:
        res = _adamw(a[name], big_grads[name], a["m_" + name], a["v_" + name], 0, 1, upper.get(name))
        if name in TRANSPOSED:
            res = [jnp.swapaxes(r, 1, 2) for r in res]
        grads[name], deltas[name], new_m[name], new_v[name] = res
    taps = lax.dynamic_slice_in_dim(summed["conv_dw_k"], chip * LANES, LANES, axis=2)[:, :CONV_TAPS]
    grads["conv_dw_k"] = taps.reshape(a["conv_dw_k"].shape)
    for name in SMALL:
        grads[name] = summed[name].reshape(a[name].shape)
    shapes = [a[name] for name in small_names]
    res = _adamw(*[_pack([a[pre + name] if pre != "g" else grads[name] for name in small_names])[None]
                   for pre in ("", "g", "m_", "v_")])
    for dst, flat in zip((deltas, new_m, new_v), res[1:]):
        for name, val in zip(small_names, _unpack(flat[0], shapes)):
            dst[name] = val

    return (loss, grad_x[None], *[grads[n] for n in WEIGHTS], *[deltas[n] for n in WEIGHTS],
            *[new_m[n] for n in WEIGHTS], *[new_v[n] for n in WEIGHTS])


def kernel(x, p, ffn1_pre_g, ffn1_w_gate, ffn1_w_up, ffn1_w_down, ffn1_post_g, mix_pre_g, w_in, pool_w, pool_scale, w_pool_out, sgu_ln_g, sgu_ln_b, sgu_w_s, sgu_b_s, w_sgu_out, conv_dw_k, conv_dw_b, conv_ln_g, conv_ln_b, w_conv_out, w_out, mix_post_g, ffn2_pre_g, ffn2_w_gate, ffn2_w_up, ffn2_w_down, ffn2_post_g, ple_w_proj, ple_pre_g, ple_w_gate, ple_post_g, loss_target, m_ffn1_pre_g, m_ffn1_w_gate, m_ffn1_w_up, m_ffn1_w_down, m_ffn1_post_g, m_mix_pre_g, m_w_in, m_pool_w, m_pool_scale, m_w_pool_out, m_sgu_ln_g, m_sgu_ln_b, m_sgu_w_s, m_sgu_b_s, m_w_sgu_out, m_conv_dw_k, m_conv_dw_b, m_conv_ln_g, m_conv_ln_b, m_w_conv_out, m_w_out, m_mix_post_g, m_ffn2_pre_g, m_ffn2_w_gate, m_ffn2_w_up, m_ffn2_w_down, m_ffn2_post_g, m_ple_w_proj, m_ple_pre_g, m_ple_w_gate, m_ple_post_g, v_ffn1_pre_g, v_ffn1_w_gate, v_ffn1_w_up, v_ffn1_w_down, v_ffn1_post_g, v_mix_pre_g, v_w_in, v_pool_w, v_pool_scale, v_w_pool_out, v_sgu_ln_g, v_sgu_ln_b, v_sgu_w_s, v_sgu_b_s, v_w_sgu_out, v_conv_dw_k, v_conv_dw_b, v_conv_ln_g, v_conv_ln_b, v_w_conv_out, v_w_out, v_mix_post_g, v_ffn2_pre_g, v_ffn2_w_gate, v_ffn2_w_up, v_ffn2_w_down, v_ffn2_post_g, v_ple_w_proj, v_ple_pre_g, v_ple_w_gate, v_ple_post_g):
    return _train_step(dict(locals()))
```

```python
import math

import jax
import jax.numpy as jnp
from jax import lax
from jax.experimental import pallas as pl
from jax.experimental.pallas import tpu as pltpu

BF = jnp.bfloat16
F32 = jnp.float32
EPS = 1e-6
D_MODEL = 1024
LANES = 128
SUBLANES = 8
MXU_TILE = 256
N_CHIPS = 4
FFN_SHARD = 704
FFN_SHARD_PAD = 768
POOL_WINDOWS = (2, 4, 8, 16)
SGU_HEADS = 4
CHUNK = 128
CONV_TAPS = 31
CONV_PAD = 32
ROW_TILE = 512
EPI_ROWS = 256
VMEM_LIMIT_BYTES = 56 * 1024 * 1024
ADAM_TILE_ELEMS = 3 * 128 * 1024
ADAM_LR, ADAM_B1, ADAM_B2, ADAM_EPS, ADAM_WD, ADAM_STEP =0.001, 0.9, 0.999, 1e-08, 0.01, 10
MESH = pl.DeviceIdType.MESH
ANY = pl.BlockSpec(memory_space=pl.ANY)

ZB_POOL, ZB_U, ZB_V, ZB_A, ZB_B, ZB_GATES = 0, 1, 2, 3, 4, 5
DZ_HALF = 2816

TRANSPOSED = ("ffn1_w_gate", "ffn1_w_up", "ffn2_w_gate", "ffn2_w_up")
BIG = (
    ("ffn1_w_gate", "row", FFN_SHARD, 1024, FFN_SHARD_PAD, 1024),
    ("ffn1_w_up", "row", FFN_SHARD, 1024, FFN_SHARD_PAD, 1024),
    ("ffn1_w_down", "row", FFN_SHARD, 1024, FFN_SHARD_PAD, 1024),
    ("w_in", "col", 1024, 1408, 1024, 1408),
    ("w_pool_out", "col", 512, 256, 512, 256),
    ("w_sgu_out", "col", 512, 256, 512, 256),
    ("w_conv_out", "col", 512, 256, 512, 256),
    ("w_out", "row", 256, 1024, 256, 1024),
    ("ffn2_w_gate", "row", FFN_SHARD, 1024, FFN_SHARD_PAD, 1024),
    ("ffn2_w_up", "row", FFN_SHARD, 1024, FFN_SHARD_PAD, 1024),
    ("ffn2_w_down", "row", FFN_SHARD, 1024, FFN_SHARD_PAD, 1024),
    ("ple_w_proj", "col", 256, 256, 256, 256),
    ("ple_w_gate", "row", 256, 1024, 256, 1024),
)
SMALL = ("ffn1_pre_g", "ffn1_post_g", "mix_pre_g", "pool_w", "pool_scale", "sgu_ln_g", "sgu_ln_b", "sgu_w_s",
         "sgu_b_s", "conv_dw_b", "conv_ln_g", "conv_ln_b", "mix_post_g", "ffn2_pre_g", "ffn2_post_g",
         "ple_pre_g", "ple_post_g")
WEIGHTS = ("ffn1_pre_g", "ffn1_w_gate", "ffn1_w_up", "ffn1_w_down", "ffn1_post_g", "mix_pre_g", "w_in", "pool_w",
           "pool_scale", "w_pool_out", "sgu_ln_g", "sgu_ln_b", "sgu_w_s", "sgu_b_s", "w_sgu_out", "conv_dw_k",
           "conv_dw_b", "conv_ln_g", "conv_ln_b", "w_conv_out", "w_out", "mix_post_g", "ffn2_pre_g", "ffn2_w_gate",
           "ffn2_w_up", "ffn2_w_down", "ffn2_post_g", "ple_w_proj", "ple_pre_g", "ple_w_gate", "ple_post_g")


def _params(n_grid):
    return pltpu.CompilerParams(dimension_semantics=("arbitrary",) * n_grid, vmem_limit_bytes=VMEM_LIMIT_BYTES)


def _tile(n, cap=ROW_TILE):
    for t in range(min(cap, n) - min(cap, n) % 16, 0, -16):
        if n % t == 0:
            return t
    return n


def _sigmoid(x):
    return 0.5 * jnp.tanh(0.5 * x) + 0.5


def _silu_and_grad(x):
    s = _sigmoid(x)
    return x * s, s * (1.0 + x * (1.0 - s))


def _gelu_and_grad(x):
    cdf = 0.5 * (1.0 + lax.erf(x * (1.0 / math.sqrt(2.0))))
    pdf = jnp.exp(-0.5 * x * x) * (1.0 / math.sqrt(2.0 * math.pi))
    return x * cdf, cdf + x * pdf


def _rms_fwd(x, g):
    return x * lax.rsqrt(jnp.mean(x * x, axis=-1, keepdims=True) + EPS) * g


def _rms_bwd(x, g, dy):
    r = lax.rsqrt(jnp.mean(x * x, axis=-1, keepdims=True) + EPS)
    xh = x * r
    dxh = dy * g
    dx = r * (dxh - xh * jnp.mean(dxh * xh, axis=-1, keepdims=True))
    return dx, jnp.sum(dy * xh, axis=0, keepdims=True)


def _ln_stats(x):
    xc = x - jnp.mean(x, axis=-1, keepdims=True)
    r = lax.rsqrt(jnp.mean(xc * xc, axis=-1, keepdims=True) + EPS)
    return xc * r, r


def _ln_bwd(xh, r, g, dy):
    dxh = dy * g
    dx = r * (dxh - jnp.mean(dxh, axis=-1, keepdims=True) - xh * jnp.mean(dxh * xh, axis=-1, keepdims=True))
    return dx, jnp.sum(dy * xh, axis=0, keepdims=True), jnp.sum(dy, axis=0, keepdims=True)


def _rowwise(name, fn, rows, consts, outs, accs=(), tm=ROW_TILE, deps=()):
    T = rows[0][0].shape[-2]
    tm = _tile(T, tm)
    n_in, n_o, n_dep = len(rows) + len(consts), len(outs), len(deps)

    def body(*refs):
        refs = refs[n_dep:]
        res = fn(*[r[...] for r in refs[:n_in]])
        for ref, val in zip(refs[n_in:n_in + n_o], res[:n_o]):
            ref[...] = val.astype(ref.dtype)
        acc_refs = refs[n_in + n_o:]
        if acc_refs:
            @pl.when(pl.program_id(0) == 0)
            def _():
                for ref, val in zip(acc_refs, res[n_o:]):
                    ref[...] = val

            @pl.when(pl.program_id(0) != 0)
            def _():
                for ref, val in zip(acc_refs, res[n_o:]):
                    ref[...] += val

    in_specs = [ANY] * n_dep
    for row in rows:
        w, cb = row[1], row[2]
        if len(row) == 4:
            in_specs.append(pl.BlockSpec((None, tm, w), lambda i, cb=cb, ld=row[3]: (ld, i, cb)))
        else:
            in_specs.append(pl.BlockSpec((tm, w), lambda i, cb=cb: (i, cb)))
    in_specs += [pl.BlockSpec(c.shape, lambda i: (0, 0)) for c in consts]
    out_specs = [pl.BlockSpec((tm, w), lambda i: (i, 0)) for w, _ in outs]
    out_specs += [pl.BlockSpec(s, lambda i: (0, 0)) for s in accs]
    out_shape = [jax.ShapeDtypeStruct((T, w), dt) for w, dt in outs]
    out_shape += [jax.ShapeDtypeStruct(s, F32) for s in accs]
    return pl.pallas_call(body, grid=(T // tm,), in_specs=in_specs, out_specs=out_specs, out_shape=out_shape,
                          name=name, compiler_params=_params(1))(*deps, *[r[0] for r in rows], *consts)


def _tiled(name, fn, grid, pos, ins, outs):
    n_in = len(ins)

    def body(_, *refs):
        res = fn(*[r[...] for r in refs[:n_in]])
        for ref, val in zip(refs[n_in:], res):
            ref[...] = val.astype(ref.dtype)

    spec = pltpu.PrefetchScalarGridSpec(
        num_scalar_prefetch=1, grid=grid, in_specs=[pl.BlockSpec(bs, im) for _, bs, im in ins],
        out_specs=[pl.BlockSpec(bs, im) for _, _, bs, im in outs])
    return pl.pallas_call(body, grid_spec=spec, out_shape=[jax.ShapeDtypeStruct(s, d) for s, d, _, _ in outs],
                          name=name, compiler_params=_params(len(grid)))(pos, *[a for a, _, _ in ins])


def _cast_layers(name, w, rp, cp, dtype, pos, deps=()):
    L, r, c = w.shape

    def body(_, w_ref, *rest):
        for k, o_ref in enumerate(rest[len(deps):]):
            @pl.when(pl.program_id(0) == k)
            def _(o_ref=o_ref):
                if (rp, cp) != (r, c):
                    o_ref[...] = jnp.zeros_like(o_ref)
                    o_ref[pl.ds(0, r), pl.ds(0, c)] = w_ref[...].astype(dtype)
                else:
                    o_ref[...] = w_ref[...].astype(dtype)

    spec = pltpu.PrefetchScalarGridSpec(
        num_scalar_prefetch=1, grid=(L,),
        in_specs=[pl.BlockSpec((None, r, c), lambda l, p: (l, 0, 0))] + [ANY] * len(deps),
        out_specs=[pl.BlockSpec((None, rp, cp), lambda l, p: (p[0], 0, 0))] * L)
    return pl.pallas_call(body, grid_spec=spec, out_shape=[jax.ShapeDtypeStruct((N_CHIPS, rp, cp), dtype)] * L,
                          name=name, compiler_params=_params(1))(pos, w, *deps)


_NN = (((1,), (0,)), ((), ()))
_NT = (((1,), (1,)), ((), ()))
_TN = (((0,), (0,)), ((), ()))


def _mm_tn(name, a, dy, buf, l, a_blocked, tk=ROW_TILE, first=0):
    T = a.shape[0]
    nb, R, C = buf[l].shape
    extra = [buf[l]] if first else []

    def body(a_ref, dy_ref, *rest):
        rest[-1][...] = lax.dot_general(a_ref[...].astype(BF), dy_ref[...].astype(BF), _TN,
                                        preferred_element_type=F32).astype(BF)

    if a_blocked:
        grid = (nb,)
        in_specs = [pl.BlockSpec((T, R), lambda b: (0, b)), pl.BlockSpec((T, C), lambda b: (0, 0))]
        out_specs = pl.BlockSpec((None, R, C), lambda b: (b, 0, 0))
    else:
        tk = min(tk, R)
        grid = (dy.shape[1] // C, R // tk)
        in_specs = [pl.BlockSpec((T, tk), lambda b, k: (0, k)), pl.BlockSpec((T, C), lambda b, k: (0, b))]
        out_specs = pl.BlockSpec((None, tk, C), lambda b, k: (b + first, k, 0))
    buf = list(buf)
    buf[l] = pl.pallas_call(body, grid=grid, in_specs=in_specs + [ANY] * len(extra), out_specs=out_specs,
                            out_shape=jax.ShapeDtypeStruct((nb, R, C), BF),
                            input_output_aliases={2: 0} if extra else {}, name=name,
                            compiler_params=_params(len(grid)))(a, dy, *extra)
    return buf


def _acc_rows(ref, val, first):
    @pl.when(first)
    def _():
        ref[...] = val

    @pl.when(jnp.logical_not(first))
    def _():
        ref[...] += val


def _norm_mm(name, h, g, ws, trans_w, act=False, deps=(), tm=2 * ROW_TILE):
    T = h.shape[0]
    nb, r, cc = ws[0].shape
    bo = r if trans_w else cc
    tm = _tile(T, tm)
    n_w, n_dep = len(ws), len(deps)

    def body(*refs):
        refs = refs[n_dep:]
        h_ref, g_ref, w_refs = refs[0], refs[1], refs[2:2 + n_w]
        n_ref, o_refs, n_s = refs[2 + n_w], refs[3 + n_w:3 + 2 * n_w], refs[-1]

        @pl.when(pl.program_id(1) == 0)
        def _():
            n = _rms_fwd(h_ref[...].astype(F32), g_ref[...]).astype(BF)
            n_s[...] = n
            n_ref[...] = n

        n = n_s[...]
        prods = []
        for w_ref, o_ref in zip(w_refs, o_refs):
            prods.append(lax.dot_general(n, w_ref[...], _NT if trans_w else _NN,
                                         preferred_element_type=F32).astype(BF))
            o_ref[...] = prods[-1]
        if act:
            refs[3 + 2 * n_w][...] = (_silu_and_grad(prods[0].astype(F32))[0] * prods[1].astype(F32)).astype(BF)

    wide = pl.BlockSpec((tm, bo), lambda i, b: (i, b))
    n_out = n_w + (1 if act else 0)
    return pl.pallas_call(
        body, grid=(T // tm, nb),
        in_specs=[ANY] * n_dep + [pl.BlockSpec((tm, D_MODEL), lambda i, b: (i, 0)),
                                  pl.BlockSpec(g.shape, lambda i, b: (0, 0))]
        + [pl.BlockSpec((None, r, cc), lambda i, b: (b, 0, 0))] * n_w,
        out_specs=[pl.BlockSpec((tm, D_MODEL), lambda i, b: (i, 0))] + [wide] * n_out,
        out_shape=[jax.ShapeDtypeStruct((T, D_MODEL), BF)] + [jax.ShapeDtypeStruct((T, nb * bo), BF)] * n_out,
        scratch_shapes=[pltpu.VMEM((tm, D_MODEL), BF)], name=name, compiler_params=_params(2))(*deps, h, g, *ws)


def _mm_res(name, x, w3, h, g, coef, tm=ROW_TILE):
    T, kx = x.shape
    w2 = w3.reshape(kx, D_MODEL)
    tm = _tile(T, tm)

    def body(x_ref, w_ref, h_ref, g_ref, f_ref, o_ref):
        f = jnp.dot(x_ref[...], w_ref[...], preferred_element_type=F32).astype(BF)
        f_ref[...] = f
        o_ref[...] = h_ref[...] + coef * _rms_fwd(f.astype(F32), g_ref[...])

    row = pl.BlockSpec((tm, D_MODEL), lambda i: (i, 0))
    return pl.pallas_call(
        body, grid=(T // tm,),
        in_specs=[pl.BlockSpec((tm, kx), lambda i: (i, 0)), pl.BlockSpec(w2.shape, lambda i: (0, 0)), row,
                  pl.BlockSpec(g.shape, lambda i: (0, 0))],
        out_specs=[row, row],
        out_shape=[jax.ShapeDtypeStruct((T, D_MODEL), BF), jax.ShapeDtypeStruct((T, D_MODEL), F32)],
        name=name, compiler_params=_params(1))(x, w2, h, g)


def _resbwd_mm(name, dh, f, g, coef, w3, trans_w, act=None, deps=(), tm=2 * ROW_TILE):
    T = dh.shape[0]
    nb, r, cc = w3.shape
    bo = r if trans_w else cc
    tm = _tile(T, tm)
    n_dep, n_act = len(deps), 3 if act else 0
    n_i = T // tm

    def body(*refs):
        refs = refs[n_dep:]
        dh_ref, f_ref, g_ref, w_ref = refs[:4]
        df_ref, dg_ref = refs[4 + n_act], refs[5 + n_act]
        df_s = refs[-2] if act else refs[-1]
        i, b = pl.program_id(0), pl.program_id(1)

        @pl.when(b == 0)
        def _():
            dg = jnp.zeros((1, D_MODEL), F32)
            for c in range(tm // EPI_ROWS):
                rows = slice(c * EPI_ROWS, (c + 1) * EPI_ROWS)
                dx, dg_c = _rms_bwd(f_ref[rows, :].astype(F32), g_ref[...], coef * dh_ref[rows, :])
                df_s[rows, :] = dx.astype(BF)
                df_ref[rows, :] = dx.astype(BF)
                dg = dg + dg_c
            _acc_rows(dg_ref, dg, i == 0)

        if act:
            for j in range(bo // MXU_TILE):
                cols = slice(j * MXU_TILE, (j + 1) * MXU_TILE)
                prod = lax.dot_general(df_s[...], w_ref[cols, :], _NT, preferred_element_type=F32)
                val, grad = _silu_and_grad(refs[4][:, cols].astype(F32))
                refs[6 + n_act][:, cols] = (prod * refs[5][:, cols].astype(F32) * grad).astype(BF)
                refs[7 + n_act][:, cols] = (prod * val).astype(BF)
            acc = refs[-1]
            part = lax.dot_general(refs[6][...], df_s[...], _TN, preferred_element_type=F32)

            @pl.when(i == 0)
            def _():
                acc[b] = part

            @pl.when(i != 0)
            def _():
                acc[b] += part

            @pl.when(i == n_i - 1)
            def _():
                refs[8 + n_act][...] = acc[b].astype(BF)
        else:
            refs[6][...] = lax.dot_general(df_s[...], w_ref[...], _NT if trans_w else _NN,
                                           preferred_element_type=F32).astype(BF)

    row = pl.BlockSpec((tm, D_MODEL), lambda i, b: (i, 0))
    wide = pl.BlockSpec((tm, bo), lambda i, b: (i, b))
    vec = pl.BlockSpec((1, D_MODEL), lambda i, b: (0, 0))
    out_specs = [row, vec] + [wide] * (2 if act else 1)
    out_shape = [jax.ShapeDtypeStruct((T, D_MODEL), BF), jax.ShapeDtypeStruct((1, D_MODEL), F32)]
    out_shape += [jax.ShapeDtypeStruct((T, nb * bo), BF)] * (2 if act else 1)
    scratch = [pltpu.VMEM((tm, D_MODEL), BF)]
    if act:
        out_specs.append(pl.BlockSpec((None, r, cc), lambda i, b: (jnp.where(i == n_i - 1, b, 0), 0, 0)))
        out_shape.append(jax.ShapeDtypeStruct((nb, r, cc), BF))
        scratch.append(pltpu.VMEM((nb, r, cc), F32))
    return pl.pallas_call(
        body, grid=(n_i, nb),
        in_specs=[ANY] * n_dep + [row, row, vec, pl.BlockSpec((None, r, cc), lambda i, b: (b, 0, 0))] + [wide] * n_act,
        out_specs=out_specs, out_shape=out_shape, scratch_shapes=scratch, name=name,
        compiler_params=_params(2))(*deps, dh, f, g, w3, *(act or ()))


def _dn_prenorm(name, xs, ws, trans_w, dh, h, g, tm=2 * ROW_TILE):
    T = dh.shape[0]
    chained = not isinstance(ws, (list, tuple))
    ws = [ws] if chained else list(ws)
    _, r, cc = ws[0].shape
    bw = cc if trans_w else r
    per_x = xs[0].shape[1] // bw
    nb = per_x * len(xs) if chained else per_x
    tm = _tile(T, tm)
    n_x, n_w = len(xs), len(ws)

    def body(*refs):
        x_refs, w_refs = refs[:n_x], refs[n_x:n_x + n_w]
        dh_ref, h_ref, g_ref, o_ref, dg_ref, acc = refs[n_x + n_w:]
        i, b = pl.program_id(0), pl.program_id(1)

        @pl.when(b == 0)
        def _():
            acc[...] = jnp.zeros_like(acc)

        def add(x_ref, w_ref):
            acc[...] += lax.dot_general(x_ref[...], w_ref[...], _NT if trans_w else _NN, preferred_element_type=F32)

        if chained:
            for k, x_ref in enumerate(x_refs):
                pl.when(b // per_x == k)(lambda x_ref=x_ref: add(x_ref, w_refs[0]))
        else:
            for x_ref, w_ref in zip(x_refs, w_refs):
                add(x_ref, w_ref)

        @pl.when(b == nb - 1)
        def _():
            dg = jnp.zeros((1, D_MODEL), F32)
            for c in range(tm // EPI_ROWS):
                rows = slice(c * EPI_ROWS, (c + 1) * EPI_ROWS)
                dx, dg_c = _rms_bwd(h_ref[rows, :], g_ref[...], acc[rows, :])
                o_ref[rows, :] = dh_ref[rows, :] + dx
                dg = dg + dg_c
            _acc_rows(dg_ref, dg, i == 0)

    row = pl.BlockSpec((tm, D_MODEL), lambda i, b: (i, 0))
    vec = pl.BlockSpec((1, D_MODEL), lambda i, b: (0, 0))
    if chained:
        x_specs = [pl.BlockSpec((tm, bw), lambda i, b, k=k: (i, jnp.clip(b - k * per_x, 0, per_x - 1)))
                   for k in range(n_x)]
    else:
        x_specs = [pl.BlockSpec((tm, bw), lambda i, b: (i, b))] * n_x
    return pl.pallas_call(
        body, grid=(T // tm, nb),
        in_specs=x_specs + [pl.BlockSpec((None, r, cc), lambda i, b: (b, 0, 0))] * n_w + [row, row, vec],
        out_specs=[row, vec],
        out_shape=[jax.ShapeDtypeStruct((T, D_MODEL), F32), jax.ShapeDtypeStruct((1, D_MODEL), F32)],
        scratch_shapes=[pltpu.VMEM((tm, D_MODEL), F32)], name=name,
        compiler_params=_params(2))(*xs, *ws, dh, h, g)


def _pool_apply(x, win, row):
    s, k = x, 1
    while k < win:
        s = s + jnp.where(row >= k, pltpu.roll(s, k, 0), 0.0)
        k *= 2
    return s / jnp.minimum(row + 1, win).astype(F32) - x


def _pool_apply_t(dp, win, row):
    T = dp.shape[0]
    s, k = dp / jnp.minimum(row + 1, win).astype(F32), 1
    while k < win:
        s = s + jnp.where(row < T - k, pltpu.roll(s, T - k, 0), 0.0)
        k *= 2
    return s - dp


def _pool_fwd(z, w, scale):
    T = z.shape[0]

    def body(z_ref, w_ref, s_ref, o_ref):
        row = lax.broadcasted_iota(jnp.int32, (T, LANES), 0)
        for gi, win in enumerate(POOL_WINDOWS):
            cols = pl.ds(gi * LANES, LANES)
            pooled = _pool_apply(z_ref[:, cols].astype(F32), win, row)
            y = jnp.dot(pooled.astype(BF), w_ref[gi].astype(BF), preferred_element_type=F32)
            o_ref[:, cols] = (y * s_ref[:, cols]).astype(o_ref.dtype)

    return pl.pallas_call(
        body, grid=(1,),
        in_specs=[pl.BlockSpec((T, 512), lambda i: (0, ZB_POOL)), pl.BlockSpec(w.shape, lambda i: (0, 0, 0)),
                  pl.BlockSpec(scale.shape, lambda i: (0, 0))],
        out_specs=pl.BlockSpec((T, 512), lambda i: (0, 0)), out_shape=jax.ShapeDtypeStruct((T, 512), BF),
        name="pool_fwd", compiler_params=_params(1))(z, w, scale)


def _pool_bwd(dr, z, w, scale):
    T = z.shape[0]

    def body(dr_ref, z_ref, w_ref, s_ref, dz_ref, dw_ref, ds_ref):
        row = lax.broadcasted_iota(jnp.int32, (T, LANES), 0)
        for gi, win in enumerate(POOL_WINDOWS):
            cols = pl.ds(gi * LANES, LANES)
            pooled = _pool_apply(z_ref[:, cols].astype(F32), win, row).astype(BF)
            wg = w_ref[gi].astype(BF)
            y = jnp.dot(pooled, wg, preferred_element_type=F32)
            d = dr_ref[:, cols].astype(F32)
            ds_ref[:, cols] = jnp.sum(d * y, axis=0, keepdims=True)
            dy = (d * s_ref[:, cols]).astype(BF)
            dw_ref[gi] = lax.dot_general(pooled, dy, _TN, preferred_element_type=F32)
            dpooled = lax.dot_general(dy, wg, _NT, preferred_element_type=F32)
            dz_ref[:, cols] = _pool_apply_t(dpooled, win, row).astype(dz_ref.dtype)

    return pl.pallas_call(
        body, grid=(1,),
        in_specs=[pl.BlockSpec((T, 512), lambda i: (0, 0)), pl.BlockSpec((T, 512), lambda i: (0, ZB_POOL)),
                  pl.BlockSpec(w.shape, lambda i: (0, 0, 0)), pl.BlockSpec(scale.shape, lambda i: (0, 0))],
        out_specs=[pl.BlockSpec((T, 512), lambda i: (0, 0)), pl.BlockSpec(w.shape, lambda i: (0, 0, 0)),
                   pl.BlockSpec(scale.shape, lambda i: (0, 0))],
        out_shape=[jax.ShapeDtypeStruct((T, 512), BF), jax.ShapeDtypeStruct(w.shape, F32),
                   jax.ShapeDtypeStruct(scale.shape, F32)],
        name="pool_bwd", compiler_params=_params(1))(dr, z, w, scale)


def _tril(transposed=False):
    r = lax.broadcasted_iota(jnp.int32, (CHUNK, CHUNK), 0)
    c = lax.broadcasted_iota(jnp.int32, (CHUNK, CHUNK), 1)
    return c >= r if transposed else r >= c


def _sgu_fwd(z, ln_g, ln_b, w_s, bias):
    T = z.shape[0]
    tm = _tile(T)

    def body(zu_ref, zv_ref, g_ref, b_ref, w_ref, bias_ref, o_ref):
        gu, _ = _gelu_and_grad(zu_ref[...].astype(F32))
        gv, _ = _gelu_and_grad(zv_ref[...].astype(F32))
        xh, _ = _ln_stats(gv)
        v16 = (xh * g_ref[...] + b_ref[...]).astype(BF)
        tri = _tril()
        for h in range(SGU_HEADS):
            cols = slice(h * LANES, (h + 1) * LANES)
            wh = jnp.where(tri, w_ref[h], 0.0).astype(BF)
            for c in range(tm // CHUNK):
                rows = slice(c * CHUNK, (c + 1) * CHUNK)
                s = jnp.dot(wh, v16[rows, cols], preferred_element_type=F32) + bias_ref[:, cols]
                o_ref[rows, cols] = (gu[rows, cols] * s).astype(o_ref.dtype)

    small = [pl.BlockSpec(a.shape, lambda i, n=a.ndim: (0,) * n) for a in (ln_g, ln_b, w_s, bias)]
    return pl.pallas_call(
        body, grid=(T // tm,),
        in_specs=[pl.BlockSpec((tm, 512), lambda i: (i, ZB_U)), pl.BlockSpec((tm, 512), lambda i: (i, ZB_V))] + small,
        out_specs=pl.BlockSpec((tm, 512), lambda i: (i, 0)), out_shape=jax.ShapeDtypeStruct((T, 512), BF),
        name="sgu_fwd", compiler_params=_params(1))(z, z, ln_g, ln_b, w_s, bias)


def _sgu_bwd(dr, z, ln_g, ln_b, w_s, w_st, bias):
    T = z.shape[0]
    tm = _tile(T)
    n_steps = T // tm

    def body(dr_ref, zu_ref, zv_ref, g_ref, b_ref, w_ref, wt_ref, bias_ref,
             dzu_ref, dzv_ref, dg_ref, db_ref, dw_ref, dbias_ref, dgu_s, dv_s):
        i = pl.program_id(0)

        @pl.when(i == 0)
        def _():
            dg_ref[...] = jnp.zeros_like(dg_ref)
            db_ref[...] = jnp.zeros_like(db_ref)
            dw_ref[...] = jnp.zeros_like(dw_ref)
            dbias_ref[...] = jnp.zeros_like(dbias_ref)

        zu = zu_ref[...].astype(F32)
        zv = zv_ref[...].astype(F32)
        gu, gu_grad = _gelu_and_grad(zu)
        gv, gv_grad = _gelu_and_grad(zv)
        xh, r = _ln_stats(gv)
        v16 = (xh * g_ref[...] + b_ref[...]).astype(BF)
        dr = dr_ref[...].astype(F32)
        tri = _tril()
        for h in range(SGU_HEADS):
            cols = slice(h * LANES, (h + 1) * LANES)
            wh = jnp.where(tri, w_ref[h], 0.0).astype(BF)
            wht = jnp.where(_tril(transposed=True), wt_ref[h], 0.0).astype(BF)
            for c in range(tm // CHUNK):
                rows = slice(c * CHUNK, (c + 1) * CHUNK)
                v_blk = v16[rows, cols]
                s = jnp.dot(wh, v_blk, preferred_element_type=F32) + bias_ref[:, cols]
                ds = dr[rows, cols] * gu[rows, cols]
                dgu_s[rows, cols] = dr[rows, cols] * s
                ds16 = ds.astype(BF)
                dw_ref[h] += jnp.where(tri, lax.dot_general(ds16, v_blk, _NT, preferred_element_type=F32), 0.0)
                dv_s[rows, cols] = jnp.dot(wht, ds16, preferred_element_type=F32)
                dbias_ref[:, cols] += ds
        dzu_ref[...] = (dgu_s[...] * gu_grad).astype(dzu_ref.dtype)
        dgv, dg, db = _ln_bwd(xh, r, g_ref[...], dv_s[...])
        dzv_ref[...] = (dgv * gv_grad).astype(dzv_ref.dtype)
        dg_ref[...] += dg
        db_ref[...] += db

        @pl.when(i == n_steps - 1)
        def _():
            for h in range(SGU_HEADS):
                cols = slice(h * LANES, (h + 1) * LANES)
                tot = jnp.sum(dbias_ref[:, cols], axis=1, keepdims=True)
                dbias_ref[:, cols] = jnp.broadcast_to(tot, (CHUNK, LANES))

    small = (ln_g, ln_b, w_s, w_st, bias)
    small_specs = [pl.BlockSpec(a.shape, lambda i, n=a.ndim: (0,) * n) for a in small]
    return pl.pallas_call(
        body, grid=(n_steps,),
        in_specs=[pl.BlockSpec((tm, 512), lambda i: (i, 0)), pl.BlockSpec((tm, 512), lambda i: (i, ZB_U)),
                  pl.BlockSpec((tm, 512), lambda i: (i, ZB_V))] + small_specs,
        out_specs=[pl.BlockSpec((tm, 512), lambda i: (i, 0)), pl.BlockSpec((tm, 512), lambda i: (i, 0)),
                   pl.BlockSpec((1, 512), lambda i: (0, 0)), pl.BlockSpec((1, 512), lambda i: (0, 0)),
                   pl.BlockSpec(w_s.shape, lambda i: (0, 0, 0)), pl.BlockSpec(bias.shape, lambda i: (0, 0))],
        out_shape=[jax.ShapeDtypeStruct((T, 512), BF), jax.ShapeDtypeStruct((T, 512), BF),
                   jax.ShapeDtypeStruct((1, 512), F32), jax.ShapeDtypeStruct((1, 512), F32),
                   jax.ShapeDtypeStruct(w_s.shape, F32), jax.ShapeDtypeStruct(bias.shape, F32)],
        scratch_shapes=[pltpu.VMEM((tm, 512), F32), pltpu.VMEM((tm, 512), F32)],
        name="sgu_bwd", compiler_params=_params(1))(dr, z, z, ln_g, ln_b, w_s, w_st, bias)


def _conv_fwd(z, convk, l, bias):
    T = z.shape[0]

    def body(za_ref, zb_ref, k_ref, b_ref, o_ref):
        xg = za_ref[...].astype(F32) * _sigmoid(zb_ref[...].astype(F32))
        xp = jnp.concatenate([jnp.zeros((CONV_PAD, LANES), F32), xg], axis=0)
        kw = k_ref[...]
        acc = jnp.broadcast_to(b_ref[...], (T, LANES))
        for s in range(SUBLANES):
            xs = xp if s == 0 else pltpu.roll(xp, s, 0)
            for q in range(CONV_PAD // SUBLANES):
                k = CONV_TAPS - 1 - (SUBLANES * q + s)
                if k >= 0:
                    lo = CONV_PAD - SUBLANES * q
                    acc = acc + kw[k:k + 1, :] * xs[lo:lo + T, :]
        o_ref[...] = acc.astype(o_ref.dtype)

    return pl.pallas_call(
        body, grid=(4,),
        in_specs=[pl.BlockSpec((T, LANES), lambda g: (0, 4 * ZB_A + g)),
                  pl.BlockSpec((T, LANES), lambda g: (0, 4 * ZB_B + g)),
                  pl.BlockSpec((None, CONV_PAD, LANES), lambda g: (g, 0, 0)),
                  pl.BlockSpec((1, LANES), lambda g: (0, g))],
        out_specs=pl.BlockSpec((T, LANES), lambda g: (0, g)), out_shape=jax.ShapeDtypeStruct((T, 512), BF),
        name="conv_fwd", compiler_params=_params(1))(z, z, convk[l], bias)


def _conv_bwd(dy, z, convk, l):
    T = z.shape[0]

    def body(dy_ref, za_ref, zb_ref, k_ref, dza_ref, dzb_ref, dk_ref, db_ref):
        a = za_ref[...].astype(F32)
        sg = _sigmoid(zb_ref[...].astype(F32))
        d = dy_ref[...].astype(F32)
        kw = k_ref[...]
        xp = jnp.concatenate([jnp.zeros((CONV_PAD, LANES), F32), a * sg], axis=0)
        dp = jnp.concatenate([d, jnp.zeros((CONV_PAD, LANES), F32)], axis=0)
        dxg = jnp.zeros((T, LANES), F32)
        dk_ref[...] = jnp.zeros_like(dk_ref)
        for s in range(SUBLANES):
            xs = xp if s == 0 else pltpu.roll(xp, s, 0)
            ds = dp if s == 0 else pltpu.roll(dp, T + CONV_PAD - s, 0)
            for q in range(CONV_PAD // SUBLANES):
                k = CONV_TAPS - 1 - (SUBLANES * q + s)
                if k >= 0:
                    lo = CONV_PAD - SUBLANES * q
                    dk_ref[k:k + 1, :] = jnp.sum(d * xs[lo:lo + T, :], axis=0, keepdims=True)
                    dxg = dxg + kw[k:k + 1, :] * ds[SUBLANES * q:SUBLANES * q + T, :]
        db_ref[...] = jnp.sum(d, axis=0, keepdims=True)
        dza_ref[...] = (dxg * sg).astype(dza_ref.dtype)
        dzb_ref[...] = (dxg * a * sg * (1.0 - sg)).astype(dzb_ref.dtype)

    col = pl.BlockSpec((T, LANES), lambda g: (0, g))
    return pl.pallas_call(
        body, grid=(4,),
        in_specs=[col, pl.BlockSpec((T, LANES), lambda g: (0, 4 * ZB_A + g)),
                  pl.BlockSpec((T, LANES), lambda g: (0, 4 * ZB_B + g)),
                  pl.BlockSpec((None, CONV_PAD, LANES), lambda g: (g, 0, 0))],
        out_specs=[col, col, pl.BlockSpec((CONV_PAD, LANES), lambda g: (0, g)),
                   pl.BlockSpec((1, LANES), lambda g: (0, g))],
        out_shape=[jax.ShapeDtypeStruct((T, 512), BF), jax.ShapeDtypeStruct((T, 512), BF),
                   jax.ShapeDtypeStruct((CONV_PAD, 512), F32), jax.ShapeDtypeStruct((1, 512), F32)],
        name="conv_bwd", compiler_params=_params(1))(dy, z, z, convk[l])


D = D_MODEL


def _ffn_fwd(l, h, S, W, pre, deps=()):
    n, gp, u, a = _norm_mm("ffn_in", h, S[pre + "_pre_g"], [W[pre + "_w_gate"][l], W[pre + "_w_up"][l]], True,
                           act=True, deps=deps)
    f, out = _mm_res("ffn_out", a, W[pre + "_w_down"][l], h, S[pre + "_post_g"], 0.5)
    return out, dict(h=h, n=n, gp=gp, u=u, a=a, f=f)


def _ffn_bwd(l, dh, sv, S, W, G, SG, pre, deps=()):
    df, SG[pre + "_post_g"], dgp, du, dwd = _resbwd_mm(
        "ffn_bwd_act", dh, sv["f"], S[pre + "_post_g"], 0.5, W[pre + "_w_down"][l], True,
        act=(sv["gp"], sv["u"], sv["a"]), deps=deps, tm=ROW_TILE)
    G[pre + "_w_down"] = G[pre + "_w_down"][:l] + [dwd] + G[pre + "_w_down"][l + 1:]
    G[pre + "_w_gate"] = _mm_tn("ffn_dw_gate", dgp, sv["n"], G[pre + "_w_gate"], l, True)
    G[pre + "_w_up"] = _mm_tn("ffn_dw_up", du, sv["n"], G[pre + "_w_up"], l, True)
    dh_in, SG[pre + "_pre_g"] = _dn_prenorm("ffn_bwd_in", [dgp, du], [W[pre + "_w_gate"][l], W[pre + "_w_up"][l]],
                                            False, dh, sv["h"], S[pre + "_pre_g"])
    return dh_in


def _gates(zg):
    return [_sigmoid(jnp.concatenate([zg[2 * k].astype(F32), zg[2 * k + 1].astype(F32)], axis=1)) for k in range(3)]


def _merge_fwd(z, rs, ws, tm=ROW_TILE):
    T = z.shape[0]
    tm = _tile(T, tm)
    nb, kk, bw = ws[0].shape

    def body(*refs):
        r_refs, g_refs, w_refs, y_refs, m_ref = refs[:3], refs[3:9], refs[9:12], refs[12:15], refs[15]
        for r_ref, w_ref, y_ref in zip(r_refs, w_refs, y_refs):
            for b in range(nb):
                y_ref[:, b * bw:(b + 1) * bw] = jnp.dot(r_ref[...], w_ref[b],
                                                        preferred_element_type=F32).astype(y_ref.dtype)
        g = _gates([q[...] for q in g_refs])
        m_ref[...] = (g[0] * y_refs[0][...].astype(F32) + g[1] * y_refs[1][...].astype(F32)
                      + g[2] * y_refs[2][...].astype(F32)).astype(m_ref.dtype)

    row = pl.BlockSpec((tm, D_MODEL), lambda i: (i, 0))
    return pl.pallas_call(
        body, grid=(T // tm,),
        in_specs=[pl.BlockSpec((tm, kk), lambda i: (i, 0))] * 3
        + [pl.BlockSpec((tm, 512), lambda i, j=j: (i, ZB_GATES + j)) for j in range(6)]
        + [pl.BlockSpec(ws[0].shape, lambda i: (0, 0, 0))] * 3,
        out_specs=[row] * 4, out_shape=[jax.ShapeDtypeStruct((T, D_MODEL), BF)] * 4,
        name="mix_merge", compiler_params=_params(1))(*rs, *[z] * 6, *ws)


def _merge_bwd(dmerged, z, ys, ws, tm=ROW_TILE // 2):
    T = z.shape[0]
    tm = _tile(T, tm)
    nb, kk, bw = ws[0].shape

    def body(*refs):
        dm_ref, g_refs, y_refs, w_refs = refs[0], refs[1:7], refs[7:10], refs[10:13]
        dy_refs, lo_ref, hi_ref, dr_refs = refs[13:16], refs[16], refs[17], refs[18:21]
        cut = DZ_HALF - ZB_GATES * 512
        dm = dm_ref[...].astype(F32)
        g = _gates([q[...] for q in g_refs])
        for k in range(3):
            dy_refs[k][...] = (dm * g[k]).astype(BF)
            dzg = (dm * y_refs[k][...].astype(F32) * g[k] * (1.0 - g[k])).astype(BF)
            if k == 0:
                lo_ref[...] = dzg[:, :cut]
                hi_ref[:, :D_MODEL - cut] = dzg[:, cut:]
            else:
                hi_ref[:, k * D_MODEL - cut:(k + 1) * D_MODEL - cut] = dzg
            dr = None
            for b in range(nb):
                p = lax.dot_general(dy_refs[k][:, b * bw:(b + 1) * bw], w_refs[k][b], _NT,
                                    preferred_element_type=F32)
                dr = p if dr is None else dr + p
            dr_refs[k][...] = dr.astype(BF)

    row = pl.BlockSpec((tm, D_MODEL), lambda i: (i, 0))
    return pl.pallas_call(
        body, grid=(T // tm,),
        in_specs=[row] + [pl.BlockSpec((tm, 512), lambda i, j=j: (i, ZB_GATES + j)) for j in range(6)] + [row] * 3
        + [pl.BlockSpec(ws[0].shape, lambda i: (0, 0, 0))] * 3,
        out_specs=[row] * 3 + [pl.BlockSpec((tm, DZ_HALF - ZB_GATES * 512), lambda i: (i, 0)),
                               pl.BlockSpec((tm, DZ_HALF), lambda i: (i, 0))]
        + [pl.BlockSpec((tm, kk), lambda i: (i, 0))] * 3,
        out_shape=[jax.ShapeDtypeStruct((T, D_MODEL), BF)] * 3
        + [jax.ShapeDtypeStruct((T, DZ_HALF - ZB_GATES * 512), BF), jax.ShapeDtypeStruct((T, DZ_HALF), BF)]
        + [jax.ShapeDtypeStruct((T, kk), BF)] * 3,
        name="mix_merge_bwd", compiler_params=_params(1))(dmerged, *[z] * 6, *ys, *ws)


def _mix_fwd(l, h, S, W, deps=()):
    n, z = _norm_mm("mix_in", h, S["mix_pre_g"], [W["w_in"][l]], False, deps=deps)
    r_pool = _pool_fwd(z, S["pool_w"], S["pool_scale"])
    r_sgu = _sgu_fwd(z, S["sgu_ln_g"], S["sgu_ln_b"], S["sgu_w_s"], S["sgu_bias"])
    yc = _conv_fwd(z, W["conv_dw_k"], l, S["conv_dw_b"])

    def ln_silu(y, g, b):
        xh, _ = _ln_stats(y.astype(F32))
        return (_silu_and_grad(xh * g + b)[0],)

    r_conv = _rowwise("conv_ln", ln_silu, [(yc, 512, 0)], [S["conv_ln_g"], S["conv_ln_b"]], [(512, BF)])[0]
    y_pool, y_sgu, y_conv, merged = _merge_fwd(z, (r_pool, r_sgu, r_conv),
                                               [W["w_%s_out" % br][l] for br in ("pool", "sgu", "conv")])
    o, out = _mm_res("mix_out", merged, W["w_out"][l], h, S["mix_post_g"], 1.0)
    return out, dict(h=h, n=n, z=z, r_pool=r_pool, r_sgu=r_sgu, yc=yc, r_conv=r_conv, y_pool=y_pool, y_sgu=y_sgu,
                     y_conv=y_conv, merged=merged, o=o)


def _branch_dw(rs, dys, shape):
    T, kk = rs[0].shape
    nb, _, bw = shape

    def body(*refs):
        for k in range(3):
            refs[6 + k][...] = lax.dot_general(refs[k][...], refs[3 + k][...], _TN,
                                               preferred_element_type=F32).astype(BF)

    return pl.pallas_call(
        body, grid=(nb,),
        in_specs=[pl.BlockSpec((T, kk), lambda b: (0, 0))] * 3 + [pl.BlockSpec((T, bw), lambda b: (0, b))] * 3,
        out_specs=[pl.BlockSpec((None, kk, bw), lambda b: (b, 0, 0))] * 3,
        out_shape=[jax.ShapeDtypeStruct((nb, kk, bw), BF)] * 3, name="branch_dw",
        compiler_params=_params(1))(*rs, *dys)


def _mix_bwd(l, dh, sv, S, W, G, SG, deps=()):
    z = sv["z"]
    do, SG["mix_post_g"], dmerged = _resbwd_mm("mix_bwd_out", dh, sv["o"], S["mix_post_g"], 1.0,
                                               W["w_out"][l].reshape(1, D, D), True, deps=deps)
    G["w_out"] = _mm_tn("mix_dw_out", sv["merged"], do, G["w_out"], l, True)

    branches = ("pool", "sgu", "conv")
    res = _merge_bwd(dmerged, z, [sv["y_" + br] for br in branches], [W["w_%s_out" % br][l] for br in branches])
    dz_gate_lo, dz_hi, dr = res[3], res[4], dict(zip(branches, res[5:]))
    for br, dw in zip(branches, _branch_dw([sv["r_" + br] for br in branches], res[:3], G["w_pool_out"][l].shape)):
        wn = "w_%s_out" % br
        G[wn] = G[wn][:l] + [dw] + G[wn][l + 1:]
    dz_pool, SG["pool_w"], SG["pool_scale"] = _pool_bwd(dr["pool"], z, S["pool_w"], S["pool_scale"])
    dzu, dzv, SG["sgu_ln_g"], SG["sgu_ln_b"], SG["sgu_w_s"], dbias = _sgu_bwd(
        dr["sgu"], z, S["sgu_ln_g"], S["sgu_ln_b"], S["sgu_w_s"], S["sgu_w_st"], S["sgu_bias"])
    SG["sgu_b_s"] = dbias[:, ::LANES].T

    def ln_silu_bwd(d, y, g, b):
        xh, r = _ln_stats(y.astype(F32))
        _, grad = _silu_and_grad(xh * g + b)
        return _ln_bwd(xh, r, g, d.astype(F32) * grad)

    dyc, SG["conv_ln_g"], SG["conv_ln_b"] = _rowwise(
        "conv_ln_bwd", ln_silu_bwd, [(dr["conv"], 512, 0), (sv["yc"], 512, 0)], [S["conv_ln_g"], S["conv_ln_b"]],
        [(512, BF)], [(1, 512), (1, 512)])
    dza, dzb, SG["conv_dw_k"], SG["conv_dw_b"] = _conv_bwd(dyc, z, W["conv_dw_k"], l)
    dz_lo = jnp.concatenate([dz_pool, dzu, dzv, dza, dzb, dz_gate_lo], axis=1)
    G["w_in"] = _mm_tn("mix_dw_in", sv["n"], dz_lo, G["w_in"], l, False)
    G["w_in"] = _mm_tn("mix_dw_in", sv["n"], dz_hi, G["w_in"], l, False, first=2)
    dh_in, SG["mix_pre_g"] = _dn_prenorm("mix_bwd_in", [dz_lo, dz_hi], W["w_in"][l], True, dh, sv["h"],
                                         S["mix_pre_g"])
    return dh_in


def _ple_out(h, p, gp, w3, g, tm=ROW_TILE):
    T, kp = p.shape
    nb, _, bw = w3.shape
    tm = _tile(T, tm)

    def body(h_ref, p_ref, gp_ref, w_ref, g_ref, e_ref, o_ref):
        p16 = p_ref[...].astype(BF)
        for b in range(nb):
            e_ref[:, b * bw:(b + 1) * bw] = jnp.dot(p16, w_ref[b], preferred_element_type=F32).astype(BF)
        q = _sigmoid(gp_ref[...].astype(F32)) * e_ref[...].astype(F32)
        o_ref[...] = h_ref[...] + _rms_fwd(q, g_ref[...])

    row = pl.BlockSpec((tm, D_MODEL), lambda i: (i, 0))
    return pl.pallas_call(
        body, grid=(T // tm,),
        in_specs=[row, pl.BlockSpec((tm, kp), lambda i: (i, 0)), row, pl.BlockSpec(w3.shape, lambda i: (0, 0, 0)),
                  pl.BlockSpec(g.shape, lambda i: (0, 0))],
        out_specs=[row, row],
        out_shape=[jax.ShapeDtypeStruct((T, D_MODEL), BF), jax.ShapeDtypeStruct((T, D_MODEL), F32)],
        name="ple_out", compiler_params=_params(1))(h, p, gp, w3, g)


def _ple_fwd(l, h, p_l, S, W, deps=()):
    n, gp = _norm_mm("ple_in", h, S["ple_pre_g"], [W["ple_w_gate"][l].reshape(1, D, D)], False, deps=deps)
    e, out = _ple_out(h, p_l, gp, W["ple_w_proj"][l], S["ple_post_g"])
    return out, dict(h=h, n=n, e=e, gp=gp, p=p_l)


def _ple_bwd_rows(dh, e, gp, g_post, w3, h, g_pre, deps=(), tm=ROW_TILE):
    T = dh.shape[0]
    w2 = w3.reshape(D_MODEL, D_MODEL)
    tm = _tile(T, tm)
    n_dep = len(deps)

    def body(*refs):
        dh_ref, e_ref, gp_ref, gpost_ref, w_ref, h_ref, gpre_ref, de_ref, dgp_ref, o_ref, dpost_ref, dpre_ref = \
            refs[n_dep:]
        first = pl.program_id(0) == 0
        d = dh_ref[...]
        sg = _sigmoid(gp_ref[...].astype(F32))
        ee = e_ref[...].astype(F32)
        dq, dpost = _rms_bwd(sg * ee, gpost_ref[...], d)
        de_ref[...] = (dq * sg).astype(BF)
        dgp = (dq * ee * sg * (1.0 - sg)).astype(BF)
        dgp_ref[...] = dgp
        dn = lax.dot_general(dgp, w_ref[...], _NT, preferred_element_type=F32)
        dx, dpre = _rms_bwd(h_ref[...], gpre_ref[...], dn)
        o_ref[...] = d + dx
        _acc_rows(dpost_ref, dpost, first)
        _acc_rows(dpre_ref, dpre, first)

    row = pl.BlockSpec((tm, D_MODEL), lambda i: (i, 0))
    vec = pl.BlockSpec((1, D_MODEL), lambda i: (0, 0))
    return pl.pallas_call(
        body, grid=(T // tm,),
        in_specs=[ANY] * n_dep + [row, row, row, vec, pl.BlockSpec(w2.shape, lambda i: (0, 0)), row, vec],
        out_specs=[row, row, row, vec, vec],
        out_shape=[jax.ShapeDtypeStruct((T, D_MODEL), BF)] * 2 + [jax.ShapeDtypeStruct((T, D_MODEL), F32)]
        + [jax.ShapeDtypeStruct((1, D_MODEL), F32)] * 2,
        name="ple_bwd", compiler_params=_params(1))(*deps, dh, e, gp, g_post, w2, h, g_pre)


def _ple_bwd(l, dh, sv, S, W, G, SG, deps=()):
    de, dgp, dh_in, SG["ple_post_g"], SG["ple_pre_g"] = _ple_bwd_rows(
        dh, sv["e"], sv["gp"], S["ple_post_g"], W["ple_w_gate"][l], sv["h"], S["ple_pre_g"], deps)
    G["ple_w_proj"] = _mm_tn("ple_dw_proj", sv["p"], de, G["ple_w_proj"], l, False)
    G["ple_w_gate"] = _mm_tn("ple_dw_gate", sv["n"], dgp, G["ple_w_gate"], l, True)
    return dh_in


def _layer_small(a, l):
    S = {}
    for name in SMALL:
        v = a[name][l]
        S[name] = v.reshape(1, -1) if v.ndim == 1 else v
    S["sgu_w_st"] = jnp.swapaxes(S["sgu_w_s"], 1, 2)
    S["sgu_bias"] = jnp.repeat(S["sgu_b_s"].T, LANES, axis=1)
    return S


def _layer_fwd(l, h, p_l, S, W, deps=(), hooks=None):
    hooks = hooks or {}

    def after(part, hv):
        return hooks[part](hv) if part in hooks else ()

    h, sv1 = _ffn_fwd(l, h, S, W, "ffn1", deps)
    h, sv2 = _mix_fwd(l, h, S, W, after("ffn1", h))
    h, sv3 = _ffn_fwd(l, h, S, W, "ffn2", after("mix", h))
    h, sv4 = _ple_fwd(l, h, p_l, S, W, after("ffn2", h))
    return h, (sv1, sv2, sv3, sv4)


def _layer_bwd(l, dh, sv, S, W, G, deps=(), hooks=None):
    hooks = hooks or {}

    def after(part, dv):
        return hooks[part](dv) if part in hooks else ()

    SG = {}
    dh = _ple_bwd(l, dh, sv[3], S, W, G, SG, deps)
    dh = _ffn_bwd(l, dh, sv[2], S, W, G, SG, "ffn2")
    dh = _mix_bwd(l, dh, sv[1], S, W, G, SG, after("ffn2", dh))
    dh = _ffn_bwd(l, dh, sv[0], S, W, G, SG, "ffn1", after("mix", dh))
    return dh, SG


HBM = pl.BlockSpec(memory_space=pltpu.HBM)
SEM = pl.BlockSpec(memory_space=pltpu.SEMAPHORE)
SIDE_EFFECT = pltpu.SideEffectType.DATAFLOW_SIDE_EFFECTING


def _place():
    x, y, c = lax.axis_index("x"), lax.axis_index("y"), lax.axis_index("c")
    chips = [(1 - x, y), (x, 1 - y), (1 - x, 1 - y)]
    return x, y, c, chips


def _remote(src, dst, send_sem, recv_sem, to):
    return pltpu.make_async_remote_copy(src_ref=src, dst_ref=dst, send_sem=send_sem, recv_sem=recv_sem,
                                        device_id=to, device_id_type=MESH)


def _split_start(name, plan, bufs, deps):
    count, fn = plan
    n, nd = len(bufs), len(deps)

    def body(*refs):
        send, recv = refs[nd + n], refs[nd + n + 1]
        x, y, c, chips = _place()
        for k, (src, dst, _, to) in enumerate(fn(refs[nd:nd + n], x, y, c, chips)):
            _remote(src, dst, send.at[k], recv.at[k], to).start()
        refs[-1][...] = jnp.zeros_like(refs[-1])

    res = pl.pallas_call(
        body, in_specs=[ANY] * nd + [HBM] * n,
        out_specs=[SEM, SEM] + [HBM] * n + [pl.BlockSpec(memory_space=pltpu.VMEM)],
        out_shape=[pltpu.SemaphoreType.DMA((count,)), pltpu.SemaphoreType.DMA((count,))]
        + [pltpu.HBM(b.shape, b.dtype) for b in bufs] + [jax.ShapeDtypeStruct((8, LANES), F32)],
        input_output_aliases={nd + i: 2 + i for i in range(n)}, name=name,
        compiler_params=pltpu.CompilerParams(has_side_effects=SIDE_EFFECT),
    )(*deps, *[pltpu.with_memory_space_constraint(b, pltpu.HBM) for b in bufs])
    return (res[0], res[1]), list(res[2:2 + n]), res[-1]


def _split_wait(name, plan, sems, bufs, after):
    _, fn = plan
    n = len(bufs)

    def body(*refs):
        send, recv = refs[n], refs[n + 1]
        x, y, c, chips = _place()
        for k, (src, _, land, to) in enumerate(fn(refs[:n], x, y, c, chips)):
            cp = _remote(src, land, send.at[k], recv.at[k], to)
            cp.wait_send()
            cp.wait_recv()

    res = pl.pallas_call(
        body, in_specs=[HBM] * n + [SEM, SEM] + [ANY] * len(after), out_specs=[HBM] * n,
        out_shape=[pltpu.HBM(b.shape, b.dtype) for b in bufs], input_output_aliases={i: i for i in range(n)},
        name=name, compiler_params=pltpu.CompilerParams(has_side_effects=SIDE_EFFECT))(*bufs, *sems, *after)
    return list(res)


def _gather_plans(n):
    def across(b, x, y, c, chips):
        me, out = 2 * x + y, []
        for a in range(n):
            rh = b[a].shape[1] // 2
            mine = b[a].at[me, pl.ds(c * rh, rh)]
            for cx, cy in chips:
                out.append((mine, mine, b[a].at[2 * cx + cy, pl.ds(c * rh, rh)], (cx, cy, c)))
        return out

    def to_sibling(b, x, y, c, chips):
        out = []
        for a in range(n):
            rh = b[a].shape[1] // 2
            for cx, cy in chips:
                piece = b[a].at[2 * cx + cy, pl.ds(c * rh, rh)]
                out.append((piece, piece, b[a].at[2 * cx + cy, pl.ds((1 - c) * rh, rh)], (x, y, 1 - c)))
        return out

    return (3 * n, across), (3 * n, to_sibling)


def _pair_plan(n):
    def fn(b, x, y, c, chips):
        out = []
        for a in range(n):
            rh = b[a].shape[1] // 2
            out.append((b[a].at[:, pl.ds((1 - c) * rh, rh)], b[n + a], b[n + a], (x, y, 1 - c)))
        return out

    return n, fn


def _cross_plan(n):
    def fn(b, x, y, c, chips):
        out = []
        for a in range(n):
            for j, (cx, cy) in enumerate(chips):
                out.append((b[a].at[2 * cx + cy], b[n + a].at[j], b[n + a].at[j], (cx, cy, c)))
        return out

    return 3 * n, fn


def _share_plan(n, l):
    def fn(b, x, y, c, chips):
        out = []
        for a in range(n):
            rh = b[a].shape[1] // 2
            mine = b[a].at[l, pl.ds(c * rh, rh)]
            out.append((mine, mine, b[a].at[l, pl.ds((1 - c) * rh, rh)], (x, y, 1 - c)))
        return out

    return n, fn


def _peers(x, y, c):
    return [(1 - x if m & 4 else x, 1 - y if m & 2 else y, 1 - c if m & 1 else c) for m in range(1, 8)]


def _small_plans():
    def scatter(b, x, y, c, chips):
        return [(b[0].at[4 * px + 2 * py + pc], b[1].at[m], b[1].at[m], (px, py, pc))
                for m, (px, py, pc) in enumerate(_peers(x, y, c))]

    def gather(b, x, y, c, chips):
        mine = b[0].at[4 * x + 2 * y + c]
        return [(mine, mine, b[0].at[4 * px + 2 * py + pc], (px, py, pc)) for px, py, pc in _peers(x, y, c)]

    return (7, scatter), (7, gather)


def _sum_small(v3, got, pos):
    rs = v3.shape[1]
    tm = _tile(rs)
    ins = [(v3, (None, tm, LANES), lambda i, p: (p[2], i, 0))]
    ins += [(got, (None, tm, LANES), lambda i, p, m=m: (m, i, 0)) for m in range(7)]
    return _tiled("sum_small", lambda *t: (((((((t[0] + t[1]) + t[2]) + t[3]) + t[4]) + t[5]) + t[6]) + t[7],),
                  (rs // tm,), pos, ins, [((8, rs, LANES), F32, (None, tm, LANES), lambda i, p: (p[2], i, 0))])[0]


ADD_ROWS = 128


def _multi_tiled(name, fn, pos, groups, in_place=False):
    steps = max(g[1] for g in groups)
    flat_in, in_specs, out_specs, out_shape, counts, dests = [], [], [], [], [], []
    for ins, n_t, (shape, dtype, oidx, dest) in groups:
        for arr, idx in ins:
            flat_in.append(arr)
            in_specs.append(pl.BlockSpec((ADD_ROWS, arr.shape[1]),
                                         lambda i, p, idx=idx, n_t=n_t: (idx(jnp.minimum(i, n_t - 1), p), 0)))
        out_specs.append(pl.BlockSpec((ADD_ROWS, shape[1]),
                                      lambda i, p, oidx=oidx, n_t=n_t: (oidx(jnp.minimum(i, n_t - 1), p), 0)))
        out_shape.append(jax.ShapeDtypeStruct(shape, dtype))
        counts.append((len(ins), n_t))
        dests.append(dest)
    n_in = len(flat_in)
    extra = dests if in_place else []

    def body(_, *refs):
        outs = refs[n_in + len(extra):]
        k = 0
        for (n_a, n_t), o_ref in zip(counts, outs):
            tiles = refs[k:k + n_a]
            k += n_a

            @pl.when(pl.program_id(0) < n_t)
            def _(tiles=tiles, o_ref=o_ref):
                o_ref[...] = fn(*[t[...] for t in tiles]).astype(o_ref.dtype)

    spec = pltpu.PrefetchScalarGridSpec(num_scalar_prefetch=1, grid=(steps,),
                                        in_specs=in_specs + [ANY] * len(extra), out_specs=out_specs)
    return pl.pallas_call(body, grid_spec=spec, out_shape=out_shape,
                          input_output_aliases={1 + n_in + k: k for k in range(len(extra))}, name=name,
                          compiler_params=_params(1))(pos, *flat_in, *extra)


def _add_pair(grads, got, pos):
    groups = []
    for g, q in zip(grads, got):
        nb, R, C = g.shape
        rh = R // 2
        nh = rh // ADD_ROWS
        groups.append(([(g.reshape(nb * R, C), lambda t, p, nh=nh: (t // nh) * 2 * nh + p[1] * nh + t % nh),
                        (q.reshape(nb * rh, C), lambda t, p: t)], nb * nh,
                       ((nb * rh, C), BF, lambda t, p: t, None)))
    res = _multi_tiled("rs_add_pair", lambda u, w: u.astype(F32) + w.astype(F32), pos, groups)
    return [t.reshape(q.shape) for t, q in zip(res, got)]


def _add_chips(parts, slots, reduced, l, pos):
    def add(own, s0, s1, s2):
        return ((own.astype(F32) + s0.astype(F32)) + s1.astype(F32)) + s2.astype(F32)

    groups = []
    for t, s, red in zip(parts, slots, reduced):
        nb, rh, C = t.shape
        L = red.shape[0]
        nh = rh // ADD_ROWS
        ins = [(t.reshape(nb * rh, C), lambda i, p, nh=nh: p[0] * nh + i)]
        ins += [(s.reshape(3 * rh, C), lambda i, p, j=j, nh=nh: j * nh + i) for j in range(3)]
        groups.append((ins, nh, ((L * 2 * rh, C), F32, lambda i, p, nh=nh: l * 2 * nh + p[1] * nh + i,
                                 red.reshape(L * 2 * rh, C))))
    res = _multi_tiled("rs_add_chips", add, pos, groups, in_place=True)
    return [buf.reshape(red.shape) for buf, red in zip(res, reduced)]


def _adamw_math(w, g, m, v):
    m = ADAM_B1 * m + (1.0 - ADAM_B1) * g
    v = ADAM_B2 * v + (1.0 - ADAM_B2) * (g * g)
    m_hat = m / (1.0 - ADAM_B1 ** ADAM_STEP)
    v_hat = v / (1.0 - ADAM_B2 ** ADAM_STEP)
    return -ADAM_LR * (m_hat / (jnp.sqrt(v_hat) + ADAM_EPS) + ADAM_WD * w), m, v


def _adamw(w, g, m, v, lo=0, hi=None, into=None, deps=()):
    L, R, C = w.shape
    hi = L if hi is None else hi
    tr = _tile(R, max(16, ADAM_TILE_ELEMS // C))
    extra = (list(into) if into else []) + list(deps)
    n_alias = 4 if into else 0

    def body(w_ref, g_ref, m_ref, v_ref, *rest):
        go_ref, d_ref, mo_ref, vo_ref = rest[len(extra):]
        gv = g_ref[...]
        d, mn, vn = _adamw_math(w_ref[...], gv, m_ref[...], v_ref[...])
        go_ref[...] = gv
        d_ref[...] = d
        mo_ref[...] = mn
        vo_ref[...] = vn

    spec = pl.BlockSpec((None, tr, C), lambda l, i: (l + lo, i, 0))
    out = jax.ShapeDtypeStruct(w.shape, F32)
    return pl.pallas_call(body, grid=(hi - lo, R // tr), in_specs=[spec] * 4 + [ANY] * len(extra),
                          out_specs=[spec] * 4, out_shape=[out] * 4,
                          input_output_aliases={4 + k: k for k in range(n_alias)}, name="adamw",
                          compiler_params=_params(2))(w, g, m, v, *extra)


def _pack(parts):
    flat = jnp.concatenate([q.reshape(-1, LANES) for q in parts], axis=0)
    return jnp.pad(flat, ((0, -flat.shape[0] % ROW_TILE), (0, 0)))


def _unpack(flat, like):
    out, r = [], 0
    for q in like:
        n = q.size // LANES
        out.append(flat[r:r + n].reshape(q.shape))
        r += n
    return out


def _train_step(a):
    a = dict(a)
    L = a["ffn1_pre_g"].shape[0]
    x, y, c, _ = _place()
    chip = 2 * x + y
    pos = jnp.stack([chip, c, 2 * chip + c]).astype(jnp.int32)
    for name in TRANSPOSED:
        for pre in ("", "m_", "v_"):
            a[pre + name] = jnp.swapaxes(a[pre + name], 1, 2)
    big = [b[0] for b in BIG]
    gathered = big + ["conv_dw_k"]
    n_w, n_g = len(gathered), len(big)

    own = [None] * n_w
    W = {name: [None] * L for name in gathered}
    every = list(range(n_w))
    first, mixer, later = every[:3], every[3:8] + [n_g], every[8:n_g]
    rest = mixer + later

    def cast(i, deps):
        if i == n_g:
            taps = a["conv_dw_k"].reshape(L, CONV_TAPS, LANES)
            return _cast_layers("pad_conv_taps", taps, CONV_PAD, LANES, F32, pos, deps)
        name, _, _, _, rp, cp = BIG[i]
        return _cast_layers("cast_weight", a[name], rp, cp, BF, pos, deps)

    def gather_first(l, ids, tag, deps):
        return _split_start("gather_a%d%s" % (l, tag), _gather_plans(len(ids))[0], [own[i][l] for i in ids], deps)

    def gather_second(l, ids, tag, state, after):
        across, to_sibling = _gather_plans(len(ids))
        bufs = _split_wait("gather_a%d%s_done" % (l, tag), across, state[0], state[1], after)
        return _split_start("gather_b%d%s" % (l, tag), to_sibling, bufs, [])

    def gather_done(l, ids, tag, state, after):
        to_sibling = _gather_plans(len(ids))[1]
        bufs = _split_wait("gather_b%d%s_done" % (l, tag), to_sibling, state[0], state[1], after)
        for i, buf in zip(ids, bufs):
            W[gathered[i]][l] = buf

    for i in first:
        own[i] = cast(i, ())
    state = gather_first(0, first, "f", [])
    for i in rest:
        own[i] = cast(i, (state[2],))
    state = gather_second(0, first, "f", state, [own[i][0] for i in rest])
    gather_done(0, first, "f", state, [])

    parts = {"f": first, "m": mixer, "t": later}
    flying = {}

    def begin(l, part, deps):
        flying[l, part] = gather_first(l, parts[part], part, deps)
        return flying[l, part][2]

    def hand_on(l, part, after):
        flying[l, part] = gather_second(l, parts[part], part, flying[l, part], after)
        return flying[l, part][2]

    def arrive(l, part, after):
        gather_done(l, parts[part], part, flying.pop((l, part)), after)

    def hooks_of(l):
        nxt = l + 1 < L

        def after_ffn1(hv):
            tokens = []
            if l == 0:
                hand_on(0, "m", [hv])
            arrive(l, "m", [hv])
            if l == 0:
                tokens.append(begin(0, "t", [hv]))
            else:
                tokens.append(hand_on(l, "t", [hv]))
            if nxt:
                tokens.append(begin(l + 1, "f", tokens[-1:]))
            return tuple(tokens)

        def after_mix(hv):
            tokens = []
            if l == 0:
                hand_on(0, "t", [hv])
            arrive(l, "t", [hv])
            if nxt:
                tokens.append(hand_on(l + 1, "f", [hv]))
                tokens.append(begin(l + 1, "m", tokens[-1:]))
            return tuple(tokens)

        def after_ffn2(hv):
            tokens = []
            if nxt:
                arrive(l + 1, "f", [hv])
                tokens.append(hand_on(l + 1, "m", [hv]))
                tokens.append(begin(l + 1, "t", tokens[-1:]))
            return tuple(tokens)

        return {"ffn1": after_ffn1, "mix": after_mix, "ffn2": after_ffn2}

    small = [_layer_small(a, l) for l in range(L)]
    h, saved = a["x"][0], []
    deps = (begin(0, "m", []),)
    for l in range(L):
        h, sv = _layer_fwd(l, h, a["p"][l, 0], small[l], W, deps, hooks_of(l))
        saved.append(sv)
        deps = ()

    def loss_fn(yv, t):
        e = yv - t
        return e * (1.0 / D), jnp.sum(e * e, axis=0, keepdims=True)

    dh, lsum = _rowwise("loss", loss_fn, [(h, D, 0), (a["loss_target"][0], D, 0)], [], [(D, F32)], [(1, D)])
    loss = lax.psum(0.5 * jnp.sum(lsum) / D, ("x", "y", "c"))

    G = {name: [jax.ShapeDtypeStruct((N_CHIPS, rp, cp), BF)] * L for name, _, _, _, rp, cp in BIG}
    reduced = [lax.empty((L, rp, cp), F32) for _, _, _, _, rp, cp in BIG]
    small_grads = [None] * L
    whole = list(range(n_g))
    piece_a, piece_b, piece_c = whole[8:], whole[3:8], whole[:3]

    def pair_start(l, ids, tag, deps):
        grads = [G[big[i]][l] for i in ids]
        lands = [lax.empty((N_CHIPS, g.shape[1] // 2, g.shape[2]), BF) for g in grads]
        return _split_start("rs_pair%d%s" % (l, tag), _pair_plan(len(ids)), grads + lands, deps)

    def cross_start(l, ids, tag, state, after):
        n = len(ids)
        bufs = _split_wait("rs_pair%d%s_done" % (l, tag), _pair_plan(n), state[0], state[1], after)
        parts = _add_pair(bufs[:n], bufs[n:], pos)
        lands = [lax.empty((3,) + t.shape[1:], BF) for t in parts]
        return _split_start("rs_cross%d%s" % (l, tag), _cross_plan(n), parts + lands, [])

    def cross_finish(l, ids, tag, state, after, reduced):
        n = len(ids)
        bufs = _split_wait("rs_cross%d%s_done" % (l, tag), _cross_plan(n), state[0], state[1], after)
        reduced = list(reduced)
        for i, r in zip(ids, _add_chips(bufs[:n], bufs[n:], [reduced[i] for i in ids], l, pos)):
            reduced[i] = r
        return reduced

    def share_start(l, reduced):
        return _split_start("rs_share%d" % l, _share_plan(n_g, l), reduced, [])

    def share_done(l, state, after):
        return _split_wait("rs_share%d_done" % l, _share_plan(n_g, l), state[0], state[1], after)

    small_names = SMALL + ("conv_dw_k",)
    scatter, gather = _small_plans()
    totals = [None] * L

    def small_scatter(l, deps):
        packed = _pack([small_grads[l][name] for name in small_names])
        v3 = packed.reshape(8, packed.shape[0] // 8, LANES)
        return _split_start("small_scatter%d" % l, scatter, [v3, lax.empty((7,) + v3.shape[1:], F32)], deps)

    def small_gather(l, state, after):
        bufs = _split_wait("small_scatter%d_done" % l, scatter, state[0], state[1], after)
        return _split_start("small_gather%d" % l, gather, [_sum_small(bufs[0], bufs[1], pos)], [])

    def small_done(l, state, after):
        total = _split_wait("small_gather%d_done" % l, gather, state[0], state[1], after)[0]
        totals[l] = total.reshape(-1, LANES)

    st_pair = st_share = st_small = None
    for l in reversed(range(L)):
        deps = tuple(s[2] for s in (st_pair, st_share, st_small) if s is not None)
        box = {}

        def after_ffn2(dm, l=l, box=box, st_pair=st_pair, st_share=st_share, st_small=st_small):
            out = []
            if st_share is not None:
                box["reduced"] = share_done(l + 2, st_share, [dm])
            if st_small is not None:
                box["small"] = small_gather(l + 1, st_small, [dm])
                out.append(box["small"][2])
            if st_pair is not None:
                box["cross"] = cross_start(l + 1, whole, "", st_pair, [dm])
                out.append(box["cross"][2])
            if l == 0:
                box["pair_a"] = pair_start(0, piece_a, "a", [dm])
                out.append(box["pair_a"][2])
            return tuple(out)

        def after_mix(dm, box=box):
            box["cross_a"] = cross_start(0, piece_a, "a", box["pair_a"], [dm])
            box["pair_b"] = pair_start(0, piece_b, "b", [dm])
            return (box["cross_a"][2], box["pair_b"][2])

        hooks = {"ffn2": after_ffn2, "mix": after_mix} if l == 0 else {"ffn2": after_ffn2}
        dh, small_grads[l] = _layer_bwd(l, dh, saved[l], small[l], W, G, deps, hooks)
        if st_share is not None:
            reduced = box["reduced"]
        if "small" in box:
            small_done(l + 1, box["small"], [dh])
        st_share = None
        if "cross" in box:
            reduced = cross_finish(l + 1, whole, "", box["cross"], [dh], reduced)
            st_share = share_start(l + 1, reduced)
        st_small = small_scatter(l, [dh])
        st_pair = pair_start(l, whole, "", [st_small[2]]) if l else None
    grad_x = dh
    cross_b = cross_start(0, piece_b, "b", box["pair_b"], [st_small[2]])
    cross_c = cross_start(0, piece_c, "c", pair_start(0, piece_c, "c", [cross_b[2]]), [])
    if st_share is not None:
        reduced = share_done(1, st_share, [cross_c[2]])
    upper, token = {}, cross_c[2]
    for k, (name, red) in enumerate(zip(big, reduced)):
        if k == 2:
            st_small = small_gather(0, st_small, [token])
            token = st_small[2]
        if L > 1:
            upper[name] = _adamw(a[name], red, a["m_" + name], a["v_" + name], 1, L, deps=[token])
            token = upper[name][1]
    small_done(0, st_small, [token])
    per_layer = [_unpack(totals[l], [small_grads[l][name] for name in small_names]) for l in range(L)]
    summed = {name: jnp.stack([per_layer[l][k] for l in range(L)]) for k, name in enumerate(small_names)}
    done = [r[1] for r in upper.values()] + [summed[small_names[0]]]
    for ids, tag, state in ((piece_a, "a", box["cross_a"]), (piece_b, "b", cross_b), (piece_c, "c", cross_c)):
        reduced = cross_finish(0, ids, tag, state, done, reduced)
    reduced = share_done(0, share_start(0, reduced), [])
    big_grads = dict(zip(big, reduced))

    grads, deltas, new_m, new_v = {}, {}, {}, {}
    for name in big:
        res = _adamw(a[name], big_grads[name], a["m_" + name], a["v_" + name], 0, 1, upper.get(name))
        if name in TRANSPOSED:
            res = [jnp.swapaxes(r, 1, 2) for r in res]
        grads[name], deltas[name], new_m[name], new_v[name] = res
    taps = lax.dynamic_slice_in_dim(summed["conv_dw_k"], chip * LANES, LANES, axis=2)[:, :CONV_TAPS]
    grads["conv_dw_k"] = taps.reshape(a["conv_dw_k"].shape)
    for name in SMALL:
        grads[name] = summed[name].reshape(a[name].shape)
    shapes = [a[name] for name in small_names]
    res = _adamw(*[_pack([a[pre + name] if pre != "g" else grads[name] for name in small_names])[None]
                   for pre in ("", "g", "m_", "v_")])
    for dst, flat in zip((deltas, new_m, new_v), res[1:]):
        for name, val in zip(small_names, _unpack(flat[0], shapes)):
            dst[name] = val

    return (loss, grad_x[None], *[grads[n] for n in WEIGHTS], *[deltas[n] for n in WEIGHTS],
            *[new_m[n] for n in WEIGHTS], *[new_v[n] for n in WEIGHTS])


def kernel(x, p, ffn1_pre_g, ffn1_w_gate, ffn1_w_up, ffn1_w_down, ffn1_post_g, mix_pre_g, w_in, pool_w, pool_scale, w_pool_out, sgu_ln_g, sgu_ln_b, sgu_w_s, sgu_b_s, w_sgu_out, conv_dw_k, conv_dw_b, conv_ln_g, conv_ln_b, w_conv_out, w_out, mix_post_g, ffn2_pre_g, ffn2_w_gate, ffn2_w_up, ffn2_w_down, ffn2_post_g, ple_w_proj, ple_pre_g, ple_w_gate, ple_post_g, loss_target, m_ffn1_pre_g, m_ffn1_w_gate, m_ffn1_w_up, m_ffn1_w_down, m_ffn1_post_g, m_mix_pre_g, m_w_in, m_pool_w, m_pool_scale, m_w_pool_out, m_sgu_ln_g, m_sgu_ln_b, m_sgu_w_s, m_sgu_b_s, m_w_sgu_out, m_conv_dw_k, m_conv_dw_b, m_conv_ln_g, m_conv_ln_b, m_w_conv_out, m_w_out, m_mix_post_g, m_ffn2_pre_g, m_ffn2_w_gate, m_ffn2_w_up, m_ffn2_w_down, m_ffn2_post_g, m_ple_w_proj, m_ple_pre_g, m_ple_w_gate, m_ple_post_g, v_ffn1_pre_g, v_ffn1_w_gate, v_ffn1_w_up, v_ffn1_w_down, v_ffn1_post_g, v_mix_pre_g, v_w_in, v_pool_w, v_pool_scale, v_w_pool_out, v_sgu_ln_g, v_sgu_ln_b, v_sgu_w_s, v_sgu_b_s, v_w_sgu_out, v_conv_dw_k, v_conv_dw_b, v_conv_ln_g, v_conv_ln_b, v_w_conv_out, v_w_out, v_mix_post_g, v_ffn2_pre_g, v_ffn2_w_gate, v_ffn2_w_up, v_ffn2_w_down, v_ffn2_post_g, v_ple_w_proj, v_ple_pre_g, v_ple_w_gate, v_ple_post_g):
    return _train_step(dict(locals()))
```

```python
import math

import jax
import jax.numpy as jnp
from jax import lax
from jax.experimental import pallas as pl
from jax.experimental.pallas import tpu as pltpu

BF = jnp.bfloat16
F32 = jnp.float32
EPS = 1e-6
D_MODEL = 1024
LANES = 128
SUBLANES = 8
MXU_TILE = 256
N_CHIPS = 4
FFN_SHARD = 704
FFN_SHARD_PAD = 768
POOL_WINDOWS = (2, 4, 8, 16)
SGU_HEADS = 4
CHUNK = 128
CONV_TAPS = 31
CONV_PAD = 32
ROW_TILE = 512
EPI_ROWS = 256
VMEM_LIMIT_BYTES = 56 * 1024 * 1024
ADAM_TILE_ELEMS = 3 * 128 * 1024
ADAM_LR, ADAM_B1, ADAM_B2, ADAM_EPS, ADAM_WD, ADAM_STEP =0.001, 0.9, 0.999, 1e-08, 0.01, 10
MESH = pl.DeviceIdType.MESH
ANY = pl.BlockSpec(memory_space=pl.ANY)

BRANCH = 512
ZB_POOL, ZB_U, ZB_V, ZB_A, ZB_B, ZB_GATES = 0, 1, 2, 3, 4, 5
DZ_HALF = 2816

TRANSPOSED = ("ffn1_w_gate", "ffn1_w_up", "ffn2_w_gate", "ffn2_w_up")
BIG = (
    ("ffn1_w_gate", "row", FFN_SHARD, 1024, FFN_SHARD_PAD, 1024),
    ("ffn1_w_up", "row", FFN_SHARD, 1024, FFN_SHARD_PAD, 1024),
    ("ffn1_w_down", "row", FFN_SHARD, 1024, FFN_SHARD_PAD, 1024),
    ("w_in", "col", 1024, 1408, 1024, 1408),
    ("w_pool_out", "col", 512, 256, 512, 256),
    ("w_sgu_out", "col", 512, 256, 512, 256),
    ("w_conv_out", "col", 512, 256, 512, 256),
    ("w_out", "row", 256, 1024, 256, 1024),
    ("ffn2_w_gate", "row", FFN_SHARD, 1024, FFN_SHARD_PAD, 1024),
    ("ffn2_w_up", "row", FFN_SHARD, 1024, FFN_SHARD_PAD, 1024),
    ("ffn2_w_down", "row", FFN_SHARD, 1024, FFN_SHARD_PAD, 1024),
    ("ple_w_proj", "col", 256, 256, 256, 256),
    ("ple_w_gate", "row", 256, 1024, 256, 1024),
)
SMALL = ("ffn1_pre_g", "ffn1_post_g", "mix_pre_g", "pool_w", "pool_scale", "sgu_ln_g", "sgu_ln_b", "sgu_w_s",
         "sgu_b_s", "conv_dw_b", "conv_ln_g", "conv_ln_b", "mix_post_g", "ffn2_pre_g", "ffn2_post_g",
         "ple_pre_g", "ple_post_g")
WEIGHTS = ("ffn1_pre_g", "ffn1_w_gate", "ffn1_w_up", "ffn1_w_down", "ffn1_post_g", "mix_pre_g", "w_in", "pool_w",
           "pool_scale", "w_pool_out", "sgu_ln_g", "sgu_ln_b", "sgu_w_s", "sgu_b_s", "w_sgu_out", "conv_dw_k",
           "conv_dw_b", "conv_ln_g", "conv_ln_b", "w_conv_out", "w_out", "mix_post_g", "ffn2_pre_g", "ffn2_w_gate",
           "ffn2_w_up", "ffn2_w_down", "ffn2_post_g", "ple_w_proj", "ple_pre_g", "ple_w_gate", "ple_post_g")


def _params(n_grid):
    return pltpu.CompilerParams(dimension_semantics=("arbitrary",) * n_grid, vmem_limit_bytes=VMEM_LIMIT_BYTES)


def _tile(n, cap=ROW_TILE):
    for t in range(min(cap, n) - min(cap, n) % 16, 0, -16):
        if n % t == 0:
            return t
    return n


def _sigmoid(x):
    return 0.5 * jnp.tanh(0.5 * x) + 0.5


def _silu_and_grad(x):
    s = _sigmoid(x)
    return x * s, s * (1.0 + x * (1.0 - s))


def _gelu_and_grad(x):
    cdf = 0.5 * (1.0 + lax.erf(x * (1.0 / math.sqrt(2.0))))
    pdf = jnp.exp(-0.5 * x * x) * (1.0 / math.sqrt(2.0 * math.pi))
    return x * cdf, cdf + x * pdf


def _rms_fwd(x, g):
    return x * lax.rsqrt(jnp.mean(x * x, axis=-1, keepdims=True) + EPS) * g


def _rms_bwd(x, g, dy):
    r = lax.rsqrt(jnp.mean(x * x, axis=-1, keepdims=True) + EPS)
    xh = x * r
    dxh = dy * g
    dx = r * (dxh - xh * jnp.mean(dxh * xh, axis=-1, keepdims=True))
    return dx, jnp.sum(dy * xh, axis=0, keepdims=True)


def _ln_stats(x):
    xc = x - jnp.mean(x, axis=-1, keepdims=True)
    r = lax.rsqrt(jnp.mean(xc * xc, axis=-1, keepdims=True) + EPS)
    return xc * r, r


def _ln_bwd(xh, r, g, dy):
    dxh = dy * g
    dx = r * (dxh - jnp.mean(dxh, axis=-1, keepdims=True) - xh * jnp.mean(dxh * xh, axis=-1, keepdims=True))
    return dx, jnp.sum(dy * xh, axis=0, keepdims=True), jnp.sum(dy, axis=0, keepdims=True)


def _rowwise(name, fn, rows, consts, outs, accs=(), tm=ROW_TILE, deps=()):
    T = rows[0][0].shape[-2]
    tm = _tile(T, tm)
    n_in, n_o, n_dep = len(rows) + len(consts), len(outs), len(deps)

    def body(*refs):
        refs = refs[n_dep:]
        res = fn(*[r[...] for r in refs[:n_in]])
        for ref, val in zip(refs[n_in:n_in + n_o], res[:n_o]):
            ref[...] = val.astype(ref.dtype)
        acc_refs = refs[n_in + n_o:]
        if acc_refs:
            @pl.when(pl.program_id(0) == 0)
            def _():
                for ref, val in zip(acc_refs, res[n_o:]):
                    ref[...] = val

            @pl.when(pl.program_id(0) != 0)
            def _():
                for ref, val in zip(acc_refs, res[n_o:]):
                    ref[...] += val

    in_specs = [ANY] * n_dep
    for row in rows:
        w, cb = row[1], row[2]
        if len(row) == 4:
            in_specs.append(pl.BlockSpec((None, tm, w), lambda i, cb=cb, ld=row[3]: (ld, i, cb)))
        else:
            in_specs.append(pl.BlockSpec((tm, w), lambda i, cb=cb: (i, cb)))
    in_specs += [pl.BlockSpec(c.shape, lambda i: (0, 0)) for c in consts]
    out_specs = [pl.BlockSpec((tm, w), lambda i: (i, 0)) for w, _ in outs]
    out_specs += [pl.BlockSpec(s, lambda i: (0, 0)) for s in accs]
    out_shape = [jax.ShapeDtypeStruct((T, w), dt) for w, dt in outs]
    out_shape += [jax.ShapeDtypeStruct(s, F32) for s in accs]
    return pl.pallas_call(body, grid=(T // tm,), in_specs=in_specs, out_specs=out_specs, out_shape=out_shape,
                          name=name, compiler_params=_params(1))(*deps, *[r[0] for r in rows], *consts)


def _tiled(name, fn, grid, pos, ins, outs):
    n_in = len(ins)

    def body(_, *refs):
        res = fn(*[r[...] for r in refs[:n_in]])
        for ref, val in zip(refs[n_in:], res):
            ref[...] = val.astype(ref.dtype)

    spec = pltpu.PrefetchScalarGridSpec(
        num_scalar_prefetch=1, grid=grid, in_specs=[pl.BlockSpec(bs, im) for _, bs, im in ins],
        out_specs=[pl.BlockSpec(bs, im) for _, _, bs, im in outs])
    return pl.pallas_call(body, grid_spec=spec, out_shape=[jax.ShapeDtypeStruct(s, d) for s, d, _, _ in outs],
                          name=name, compiler_params=_params(len(grid)))(pos, *[a for a, _, _ in ins])


def _cast_layers(name, w, rp, cp, dtype, pos, deps=()):
    L, r, c = w.shape

    def body(_, w_ref, *rest):
        for k, o_ref in enumerate(rest[len(deps):]):
            @pl.when(pl.program_id(0) == k)
            def _(o_ref=o_ref):
                if (rp, cp) != (r, c):
                    o_ref[...] = jnp.zeros_like(o_ref)
                    o_ref[pl.ds(0, r), pl.ds(0, c)] = w_ref[...].astype(dtype)
                else:
                    o_ref[...] = w_ref[...].astype(dtype)

    spec = pltpu.PrefetchScalarGridSpec(
        num_scalar_prefetch=1, grid=(L,),
        in_specs=[pl.BlockSpec((None, r, c), lambda l, p: (l, 0, 0))] + [ANY] * len(deps),
        out_specs=[pl.BlockSpec((None, rp, cp), lambda l, p: (p[0], 0, 0))] * L)
    return pl.pallas_call(body, grid_spec=spec, out_shape=[jax.ShapeDtypeStruct((N_CHIPS, rp, cp), dtype)] * L,
                          name=name, compiler_params=_params(1))(pos, w, *deps)


_NN = (((1,), (0,)), ((), ()))
_NT = (((1,), (1,)), ((), ()))
_TN = (((0,), (0,)), ((), ()))


def _mm_tn(name, a, dy, buf, l, a_blocked, tk=ROW_TILE, first=0):
    T = a.shape[0]
    nb, R, C = buf[l].shape
    extra = [buf[l]] if first else []

    def body(a_ref, dy_ref, *rest):
        rest[-1][...] = lax.dot_general(a_ref[...].astype(BF), dy_ref[...].astype(BF), _TN,
                                        preferred_element_type=F32).astype(BF)

    if a_blocked:
        grid = (nb,)
        in_specs = [pl.BlockSpec((T, R), lambda b: (0, b)), pl.BlockSpec((T, C), lambda b: (0, 0))]
        out_specs = pl.BlockSpec((None, R, C), lambda b: (b, 0, 0))
    else:
        tk = min(tk, R)
        grid = (dy.shape[1] // C, R // tk)
        in_specs = [pl.BlockSpec((T, tk), lambda b, k: (0, k)), pl.BlockSpec((T, C), lambda b, k: (0, b))]
        out_specs = pl.BlockSpec((None, tk, C), lambda b, k: (b + first, k, 0))
    buf = list(buf)
    buf[l] = pl.pallas_call(body, grid=grid, in_specs=in_specs + [ANY] * len(extra), out_specs=out_specs,
                            out_shape=jax.ShapeDtypeStruct((nb, R, C), BF),
                            input_output_aliases={2: 0} if extra else {}, name=name,
                            compiler_params=_params(len(grid)))(a, dy, *extra)
    return buf


def _acc_rows(ref, val, first):
    @pl.when(first)
    def _():
        ref[...] = val

    @pl.when(jnp.logical_not(first))
    def _():
        ref[...] += val


def _norm_mm(name, h, g, ws, trans_w, act=False, deps=(), tm=2 * ROW_TILE):
    T = h.shape[0]
    nb, r, cc = ws[0].shape
    bo = r if trans_w else cc
    tm = _tile(T, tm)
    n_w, n_dep = len(ws), len(deps)

    def body(*refs):
        refs = refs[n_dep:]
        h_ref, g_ref, w_refs = refs[0], refs[1], refs[2:2 + n_w]
        n_ref, o_refs, n_s = refs[2 + n_w], refs[3 + n_w:3 + 2 * n_w], refs[-1]

        @pl.when(pl.program_id(1) == 0)
        def _():
            n = _rms_fwd(h_ref[...].astype(F32), g_ref[...]).astype(BF)
            n_s[...] = n
            n_ref[...] = n

        n = n_s[...]
        prods = []
        for w_ref, o_ref in zip(w_refs, o_refs):
            prods.append(lax.dot_general(n, w_ref[...], _NT if trans_w else _NN,
                                         preferred_element_type=F32).astype(BF))
            o_ref[...] = prods[-1]
        if act:
            refs[3 + 2 * n_w][...] = (_silu_and_grad(prods[0].astype(F32))[0] * prods[1].astype(F32)).astype(BF)

    wide = pl.BlockSpec((tm, bo), lambda i, b: (i, b))
    n_out = n_w + (1 if act else 0)
    return pl.pallas_call(
        body, grid=(T // tm, nb),
        in_specs=[ANY] * n_dep + [pl.BlockSpec((tm, D_MODEL), lambda i, b: (i, 0)),
                                  pl.BlockSpec(g.shape, lambda i, b: (0, 0))]
        + [pl.BlockSpec((None, r, cc), lambda i, b: (b, 0, 0))] * n_w,
        out_specs=[pl.BlockSpec((tm, D_MODEL), lambda i, b: (i, 0))] + [wide] * n_out,
        out_shape=[jax.ShapeDtypeStruct((T, D_MODEL), BF)] + [jax.ShapeDtypeStruct((T, nb * bo), BF)] * n_out,
        scratch_shapes=[pltpu.VMEM((tm, D_MODEL), BF)], name=name, compiler_params=_params(2))(*deps, h, g, *ws)


def _mm_res(name, x, w3, h, g, coef, tm=ROW_TILE):
    T, kx = x.shape
    w2 = w3.reshape(kx, D_MODEL)
    tm = _tile(T, tm)

    def body(x_ref, w_ref, h_ref, g_ref, f_ref, o_ref):
        f = jnp.dot(x_ref[...], w_ref[...], preferred_element_type=F32).astype(BF)
        f_ref[...] = f
        o_ref[...] = h_ref[...] + coef * _rms_fwd(f.astype(F32), g_ref[...])

    row = pl.BlockSpec((tm, D_MODEL), lambda i: (i, 0))
    return pl.pallas_call(
        body, grid=(T // tm,),
        in_specs=[pl.BlockSpec((tm, kx), lambda i: (i, 0)), pl.BlockSpec(w2.shape, lambda i: (0, 0)), row,
                  pl.BlockSpec(g.shape, lambda i: (0, 0))],
        out_specs=[row, row],
        out_shape=[jax.ShapeDtypeStruct((T, D_MODEL), BF), jax.ShapeDtypeStruct((T, D_MODEL), F32)],
        name=name, compiler_params=_params(1))(x, w2, h, g)


def _resbwd_mm(name, dh, f, g, coef, w3, trans_w, act=None, deps=(), tm=2 * ROW_TILE):
    T = dh.shape[0]
    nb, r, cc = w3.shape
    bo = r if trans_w else cc
    tm = _tile(T, tm)
    n_dep, n_act = len(deps), 3 if act else 0
    n_i = T // tm

    def body(*refs):
        refs = refs[n_dep:]
        dh_ref, f_ref, g_ref, w_ref = refs[:4]
        df_ref, dg_ref = refs[4 + n_act], refs[5 + n_act]
        df_s = refs[-2] if act else refs[-1]
        i, b = pl.program_id(0), pl.program_id(1)

        @pl.when(b == 0)
        def _():
            dg = jnp.zeros((1, D_MODEL), F32)
            for c in range(tm // EPI_ROWS):
                rows = slice(c * EPI_ROWS, (c + 1) * EPI_ROWS)
                dx, dg_c = _rms_bwd(f_ref[rows, :].astype(F32), g_ref[...], coef * dh_ref[rows, :])
                df_s[rows, :] = dx.astype(BF)
                df_ref[rows, :] = dx.astype(BF)
                dg = dg + dg_c
            _acc_rows(dg_ref, dg, i == 0)

        if act:
            for j in range(bo // MXU_TILE):
                cols = slice(j * MXU_TILE, (j + 1) * MXU_TILE)
                prod = lax.dot_general(df_s[...], w_ref[cols, :], _NT, preferred_element_type=F32)
                val, grad = _silu_and_grad(refs[4][:, cols].astype(F32))
                refs[6 + n_act][:, cols] = (prod * refs[5][:, cols].astype(F32) * grad).astype(BF)
                refs[7 + n_act][:, cols] = (prod * val).astype(BF)
            acc = refs[-1]
            part = lax.dot_general(refs[6][...], df_s[...], _TN, preferred_element_type=F32)

            @pl.when(i == 0)
            def _():
                acc[b] = part

            @pl.when(i != 0)
            def _():
                acc[b] += part

            @pl.when(i == n_i - 1)
            def _():
                refs[8 + n_act][...] = acc[b].astype(BF)
        else:
            refs[6][...] = lax.dot_general(df_s[...], w_ref[...], _NT if trans_w else _NN,
                                           preferred_element_type=F32).astype(BF)

    row = pl.BlockSpec((tm, D_MODEL), lambda i, b: (i, 0))
    wide = pl.BlockSpec((tm, bo), lambda i, b: (i, b))
    vec = pl.BlockSpec((1, D_MODEL), lambda i, b: (0, 0))
    out_specs = [row, vec] + [wide] * (2 if act else 1)
    out_shape = [jax.ShapeDtypeStruct((T, D_MODEL), BF), jax.ShapeDtypeStruct((1, D_MODEL), F32)]
    out_shape += [jax.ShapeDtypeStruct((T, nb * bo), BF)] * (2 if act else 1)
    scratch = [pltpu.VMEM((tm, D_MODEL), BF)]
    if act:
        out_specs.append(pl.BlockSpec((None, r, cc), lambda i, b: (jnp.where(i == n_i - 1, b, 0), 0, 0)))
        out_shape.append(jax.ShapeDtypeStruct((nb, r, cc), BF))
        scratch.append(pltpu.VMEM((nb, r, cc), F32))
    return pl.pallas_call(
        body, grid=(n_i, nb),
        in_specs=[ANY] * n_dep + [row, row, vec, pl.BlockSpec((None, r, cc), lambda i, b: (b, 0, 0))] + [wide] * n_act,
        out_specs=out_specs, out_shape=out_shape, scratch_shapes=scratch, name=name,
        compiler_params=_params(2))(*deps, dh, f, g, w3, *(act or ()))


def _dn_prenorm(name, xs, ws, trans_w, dh, h, g, tm=2 * ROW_TILE):
    T = dh.shape[0]
    chained = not isinstance(ws, (list, tuple))
    ws = [ws] if chained else list(ws)
    _, r, cc = ws[0].shape
    bw = cc if trans_w else r
    per_x = xs[0].shape[1] // bw
    nb = per_x * len(xs) if chained else per_x
    tm = _tile(T, tm)
    n_x, n_w = len(xs), len(ws)

    def body(*refs):
        x_refs, w_refs = refs[:n_x], refs[n_x:n_x + n_w]
        dh_ref, h_ref, g_ref, o_ref, dg_ref, acc = refs[n_x + n_w:]
        i, b = pl.program_id(0), pl.program_id(1)

        @pl.when(b == 0)
        def _():
            acc[...] = jnp.zeros_like(acc)

        def add(x_ref, w_ref):
            acc[...] += lax.dot_general(x_ref[...], w_ref[...], _NT if trans_w else _NN, preferred_element_type=F32)

        if chained:
            for k, x_ref in enumerate(x_refs):
                pl.when(b // per_x == k)(lambda x_ref=x_ref: add(x_ref, w_refs[0]))
        else:
            for x_ref, w_ref in zip(x_refs, w_refs):
                add(x_ref, w_ref)

        @pl.when(b == nb - 1)
        def _():
            dg = jnp.zeros((1, D_MODEL), F32)
            for c in range(tm // EPI_ROWS):
                rows = slice(c * EPI_ROWS, (c + 1) * EPI_ROWS)
                dx, dg_c = _rms_bwd(h_ref[rows, :], g_ref[...], acc[rows, :])
                o_ref[rows, :] = dh_ref[rows, :] + dx
                dg = dg + dg_c
            _acc_rows(dg_ref, dg, i == 0)

    row = pl.BlockSpec((tm, D_MODEL), lambda i, b: (i, 0))
    vec = pl.BlockSpec((1, D_MODEL), lambda i, b: (0, 0))
    if chained:
        x_specs = [pl.BlockSpec((tm, bw), lambda i, b, k=k: (i, jnp.clip(b - k * per_x, 0, per_x - 1)))
                   for k in range(n_x)]
    else:
        x_specs = [pl.BlockSpec((tm, bw), lambda i, b: (i, b))] * n_x
    return pl.pallas_call(
        body, grid=(T // tm, nb),
        in_specs=x_specs + [pl.BlockSpec((None, r, cc), lambda i, b: (b, 0, 0))] * n_w + [row, row, vec],
        out_specs=[row, vec],
        out_shape=[jax.ShapeDtypeStruct((T, D_MODEL), F32), jax.ShapeDtypeStruct((1, D_MODEL), F32)],
        scratch_shapes=[pltpu.VMEM((tm, D_MODEL), F32)], name=name,
        compiler_params=_params(2))(*xs, *ws, dh, h, g)


def _pool_apply(x, win, row):
    s, k = x, 1
    while k < win:
        s = s + jnp.where(row >= k, pltpu.roll(s, k, 0), 0.0)
        k *= 2
    return s / jnp.minimum(row + 1, win).astype(F32) - x


def _pool_apply_t(dp, win, row):
    T = dp.shape[0]
    s, k = dp / jnp.minimum(row + 1, win).astype(F32), 1
    while k < win:
        s = s + jnp.where(row < T - k, pltpu.roll(s, T - k, 0), 0.0)
        k *= 2
    return s - dp


def _pool_fwd(z, w, scale):
    T = z.shape[0]

    def body(z_ref, w_ref, s_ref, o_ref):
        row = lax.broadcasted_iota(jnp.int32, (T, LANES), 0)
        for gi, win in enumerate(POOL_WINDOWS):
            cols = pl.ds(gi * LANES, LANES)
            pooled = _pool_apply(z_ref[:, cols].astype(F32), win, row)
            y = jnp.dot(pooled.astype(BF), w_ref[gi].astype(BF), preferred_element_type=F32)
            o_ref[:, cols] = (y * s_ref[:, cols]).astype(o_ref.dtype)

    return pl.pallas_call(
        body, grid=(1,),
        in_specs=[pl.BlockSpec((T, BRANCH), lambda i: (0, ZB_POOL)), pl.BlockSpec(w.shape, lambda i: (0, 0, 0)),
                  pl.BlockSpec(scale.shape, lambda i: (0, 0))],
        out_specs=pl.BlockSpec((T, BRANCH), lambda i: (0, 0)), out_shape=jax.ShapeDtypeStruct((T, BRANCH), BF),
        name="pool_fwd", compiler_params=_params(1))(z, w, scale)


def _pool_bwd(dr, z, w, scale):
    T = z.shape[0]

    def body(dr_ref, z_ref, w_ref, s_ref, dz_ref, dw_ref, ds_ref):
        row = lax.broadcasted_iota(jnp.int32, (T, LANES), 0)
        for gi, win in enumerate(POOL_WINDOWS):
            cols = pl.ds(gi * LANES, LANES)
            pooled = _pool_apply(z_ref[:, cols].astype(F32), win, row).astype(BF)
            wg = w_ref[gi].astype(BF)
            y = jnp.dot(pooled, wg, preferred_element_type=F32)
            d = dr_ref[:, cols].astype(F32)
            ds_ref[:, cols] = jnp.sum(d * y, axis=0, keepdims=True)
            dy = (d * s_ref[:, cols]).astype(BF)
            dw_ref[gi] = lax.dot_general(pooled, dy, _TN, preferred_element_type=F32)
            dpooled = lax.dot_general(dy, wg, _NT, preferred_element_type=F32)
            dz_ref[:, cols] = _pool_apply_t(dpooled, win, row).astype(dz_ref.dtype)

    return pl.pallas_call(
        body, grid=(1,),
        in_specs=[pl.BlockSpec((T, BRANCH), lambda i: (0, 0)), pl.BlockSpec((T, BRANCH), lambda i: (0, ZB_POOL)),
                  pl.BlockSpec(w.shape, lambda i: (0, 0, 0)), pl.BlockSpec(scale.shape, lambda i: (0, 0))],
        out_specs=[pl.BlockSpec((T, BRANCH), lambda i: (0, 0)), pl.BlockSpec(w.shape, lambda i: (0, 0, 0)),
                   pl.BlockSpec(scale.shape, lambda i: (0, 0))],
        out_shape=[jax.ShapeDtypeStruct((T, BRANCH), BF), jax.ShapeDtypeStruct(w.shape, F32),
                   jax.ShapeDtypeStruct(scale.shape, F32)],
        name="pool_bwd", compiler_params=_params(1))(dr, z, w, scale)


def _tril(transposed=False):
    r = lax.broadcasted_iota(jnp.int32, (CHUNK, CHUNK), 0)
    c = lax.broadcasted_iota(jnp.int32, (CHUNK, CHUNK), 1)
    return c >= r if transposed else r >= c


def _sgu_fwd(z, ln_g, ln_b, w_s, bias):
    T = z.shape[0]
    tm = _tile(T)

    def body(zu_ref, zv_ref, g_ref, b_ref, w_ref, bias_ref, o_ref):
        gu, _ = _gelu_and_grad(zu_ref[...].astype(F32))
        gv, _ = _gelu_and_grad(zv_ref[...].astype(F32))
        xh, _ = _ln_stats(gv)
        v16 = (xh * g_ref[...] + b_ref[...]).astype(BF)
        tri = _tril()
        for h in range(SGU_HEADS):
            cols = slice(h * LANES, (h + 1) * LANES)
            wh = jnp.where(tri, w_ref[h], 0.0).astype(BF)
            for c in range(tm // CHUNK):
                rows = slice(c * CHUNK, (c + 1) * CHUNK)
                s = jnp.dot(wh, v16[rows, cols], preferred_element_type=F32) + bias_ref[:, cols]
                o_ref[rows, cols] = (gu[rows, cols] * s).astype(o_ref.dtype)

    small = [pl.BlockSpec(a.shape, lambda i, n=a.ndim: (0,) * n) for a in (ln_g, ln_b, w_s, bias)]
    return pl.pallas_call(
        body, grid=(T // tm,),
        in_specs=[pl.BlockSpec((tm, BRANCH), lambda i: (i, ZB_U)), pl.BlockSpec((tm, BRANCH), lambda i: (i, ZB_V))] + small,
        out_specs=pl.BlockSpec((tm, BRANCH), lambda i: (i, 0)), out_shape=jax.ShapeDtypeStruct((T, BRANCH), BF),
        name="sgu_fwd", compiler_params=_params(1))(z, z, ln_g, ln_b, w_s, bias)


def _sgu_bwd(dr, z, ln_g, ln_b, w_s, w_st, bias):
    T = z.shape[0]
    tm = _tile(T)
    n_steps = T // tm

    def body(dr_ref, zu_ref, zv_ref, g_ref, b_ref, w_ref, wt_ref, bias_ref,
             dzu_ref, dzv_ref, dg_ref, db_ref, dw_ref, dbias_ref, dgu_s, dv_s):
        i = pl.program_id(0)

        @pl.when(i == 0)
        def _():
            dg_ref[...] = jnp.zeros_like(dg_ref)
            db_ref[...] = jnp.zeros_like(db_ref)
            dw_ref[...] = jnp.zeros_like(dw_ref)
            dbias_ref[...] = jnp.zeros_like(dbias_ref)

        zu = zu_ref[...].astype(F32)
        zv = zv_ref[...].astype(F32)
        gu, gu_grad = _gelu_and_grad(zu)
        gv, gv_grad = _gelu_and_grad(zv)
        xh, r = _ln_stats(gv)
        v16 = (xh * g_ref[...] + b_ref[...]).astype(BF)
        dr = dr_ref[...].astype(F32)
        tri = _tril()
        for h in range(SGU_HEADS):
            cols = slice(h * LANES, (h + 1) * LANES)
            wh = jnp.where(tri, w_ref[h], 0.0).astype(BF)
            wht = jnp.where(_tril(transposed=True), wt_ref[h], 0.0).astype(BF)
            for c in range(tm // CHUNK):
                rows = slice(c * CHUNK, (c + 1) * CHUNK)
                v_blk = v16[rows, cols]
                s = jnp.dot(wh, v_blk, preferred_element_type=F32) + bias_ref[:, cols]
                ds = dr[rows, cols] * gu[rows, cols]
                dgu_s[rows, cols] = dr[rows, cols] * s
                ds16 = ds.astype(BF)
                dw_ref[h] += jnp.where(tri, lax.dot_general(ds16, v_blk, _NT, preferred_element_type=F32), 0.0)
                dv_s[rows, cols] = jnp.dot(wht, ds16, preferred_element_type=F32)
                dbias_ref[:, cols] += ds
        dzu_ref[...] = (dgu_s[...] * gu_grad).astype(dzu_ref.dtype)
        dgv, dg, db = _ln_bwd(xh, r, g_ref[...], dv_s[...])
        dzv_ref[...] = (dgv * gv_grad).astype(dzv_ref.dtype)
        dg_ref[...] += dg
        db_ref[...] += db

        @pl.when(i == n_steps - 1)
        def _():
            for h in range(SGU_HEADS):
                cols = slice(h * LANES, (h + 1) * LANES)
                tot = jnp.sum(dbias_ref[:, cols], axis=1, keepdims=True)
                dbias_ref[:, cols] = jnp.broadcast_to(tot, (CHUNK, LANES))

    small = (ln_g, ln_b, w_s, w_st, bias)
    small_specs = [pl.BlockSpec(a.shape, lambda i, n=a.ndim: (0,) * n) for a in small]
    return pl.pallas_call(
        body, grid=(n_steps,),
        in_specs=[pl.BlockSpec((tm, BRANCH), lambda i: (i, 0)), pl.BlockSpec((tm, BRANCH), lambda i: (i, ZB_U)),
                  pl.BlockSpec((tm, BRANCH), lambda i: (i, ZB_V))] + small_specs,
        out_specs=[pl.BlockSpec((tm, BRANCH), lambda i: (i, 0)), pl.BlockSpec((tm, BRANCH), lambda i: (i, 0)),
                   pl.BlockSpec((1, BRANCH), lambda i: (0, 0)), pl.BlockSpec((1, BRANCH), lambda i: (0, 0)),
                   pl.BlockSpec(w_s.shape, lambda i: (0, 0, 0)), pl.BlockSpec(bias.shape, lambda i: (0, 0))],
        out_shape=[jax.ShapeDtypeStruct((T, BRANCH), BF), jax.ShapeDtypeStruct((T, BRANCH), BF),
                   jax.ShapeDtypeStruct((1, BRANCH), F32), jax.ShapeDtypeStruct((1, BRANCH), F32),
                   jax.ShapeDtypeStruct(w_s.shape, F32), jax.ShapeDtypeStruct(bias.shape, F32)],
        scratch_shapes=[pltpu.VMEM((tm, BRANCH), F32), pltpu.VMEM((tm, BRANCH), F32)],
        name="sgu_bwd", compiler_params=_params(1))(dr, z, z, ln_g, ln_b, w_s, w_st, bias)


def _conv_fwd(z, convk, l, bias):
    T = z.shape[0]

    def body(za_ref, zb_ref, k_ref, b_ref, o_ref):
        xg = za_ref[...].astype(F32) * _sigmoid(zb_ref[...].astype(F32))
        xp = jnp.concatenate([jnp.zeros((CONV_PAD, LANES), F32), xg], axis=0)
        kw = k_ref[...]
        acc = jnp.broadcast_to(b_ref[...], (T, LANES))
        for s in range(SUBLANES):
            xs = xp if s == 0 else pltpu.roll(xp, s, 0)
            for q in range(CONV_PAD // SUBLANES):
                k = CONV_TAPS - 1 - (SUBLANES * q + s)
                if k >= 0:
                    lo = CONV_PAD - SUBLANES * q
                    acc = acc + kw[k:k + 1, :] * xs[lo:lo + T, :]
        o_ref[...] = acc.astype(o_ref.dtype)

    return pl.pallas_call(
        body, grid=(4,),
        in_specs=[pl.BlockSpec((T, LANES), lambda g: (0, 4 * ZB_A + g)),
                  pl.BlockSpec((T, LANES), lambda g: (0, 4 * ZB_B + g)),
                  pl.BlockSpec((None, CONV_PAD, LANES), lambda g: (g, 0, 0)),
                  pl.BlockSpec((1, LANES), lambda g: (0, g))],
        out_specs=pl.BlockSpec((T, LANES), lambda g: (0, g)), out_shape=jax.ShapeDtypeStruct((T, BRANCH), BF),
        name="conv_fwd", compiler_params=_params(1))(z, z, convk[l], bias)


def _conv_bwd(dy, z, convk, l):
    T = z.shape[0]

    def body(dy_ref, za_ref, zb_ref, k_ref, dza_ref, dzb_ref, dk_ref, db_ref):
        a = za_ref[...].astype(F32)
        sg = _sigmoid(zb_ref[...].astype(F32))
        d = dy_ref[...].astype(F32)
        kw = k_ref[...]
        xp = jnp.concatenate([jnp.zeros((CONV_PAD, LANES), F32), a * sg], axis=0)
        dp = jnp.concatenate([d, jnp.zeros((CONV_PAD, LANES), F32)], axis=0)
        dxg = jnp.zeros((T, LANES), F32)
        dk_ref[...] = jnp.zeros_like(dk_ref)
        for s in range(SUBLANES):
            xs = xp if s == 0 else pltpu.roll(xp, s, 0)
            ds = dp if s == 0 else pltpu.roll(dp, T + CONV_PAD - s, 0)
            for q in range(CONV_PAD // SUBLANES):
                k = CONV_TAPS - 1 - (SUBLANES * q + s)
                if k >= 0:
                    lo = CONV_PAD - SUBLANES * q
                    dk_ref[k:k + 1, :] = jnp.sum(d * xs[lo:lo + T, :], axis=0, keepdims=True)
                    dxg = dxg + kw[k:k + 1, :] * ds[SUBLANES * q:SUBLANES * q + T, :]
        db_ref[...] = jnp.sum(d, axis=0, keepdims=True)
        dza_ref[...] = (dxg * sg).astype(dza_ref.dtype)
        dzb_ref[...] = (dxg * a * sg * (1.0 - sg)).astype(dzb_ref.dtype)

    col = pl.BlockSpec((T, LANES), lambda g: (0, g))
    return pl.pallas_call(
        body, grid=(4,),
        in_specs=[col, pl.BlockSpec((T, LANES), lambda g: (0, 4 * ZB_A + g)),
                  pl.BlockSpec((T, LANES), lambda g: (0, 4 * ZB_B + g)),
                  pl.BlockSpec((None, CONV_PAD, LANES), lambda g: (g, 0, 0))],
        out_specs=[col, col, pl.BlockSpec((CONV_PAD, LANES), lambda g: (0, g)),
                   pl.BlockSpec((1, LANES), lambda g: (0, g))],
        out_shape=[jax.ShapeDtypeStruct((T, BRANCH), BF), jax.ShapeDtypeStruct((T, BRANCH), BF),
                   jax.ShapeDtypeStruct((CONV_PAD, BRANCH), F32), jax.ShapeDtypeStruct((1, BRANCH), F32)],
        name="conv_bwd", compiler_params=_params(1))(dy, z, z, convk[l])


D = D_MODEL


def _ffn_fwd(l, h, S, W, pre, deps=()):
    n, gp, u, a = _norm_mm("ffn_in", h, S[pre + "_pre_g"], [W[pre + "_w_gate"][l], W[pre + "_w_up"][l]], True,
                           act=True, deps=deps)
    f, out = _mm_res("ffn_out", a, W[pre + "_w_down"][l], h, S[pre + "_post_g"], 0.5)
    return out, dict(h=h, n=n, gp=gp, u=u, a=a, f=f)


def _ffn_bwd(l, dh, sv, S, W, G, SG, pre, deps=()):
    df, SG[pre + "_post_g"], dgp, du, dwd = _resbwd_mm(
        "ffn_bwd_act", dh, sv["f"], S[pre + "_post_g"], 0.5, W[pre + "_w_down"][l], True,
        act=(sv["gp"], sv["u"], sv["a"]), deps=deps, tm=ROW_TILE)
    G[pre + "_w_down"] = G[pre + "_w_down"][:l] + [dwd] + G[pre + "_w_down"][l + 1:]
    G[pre + "_w_gate"] = _mm_tn("ffn_dw_gate", dgp, sv["n"], G[pre + "_w_gate"], l, True)
    G[pre + "_w_up"] = _mm_tn("ffn_dw_up", du, sv["n"], G[pre + "_w_up"], l, True)
    dh_in, SG[pre + "_pre_g"] = _dn_prenorm("ffn_bwd_in", [dgp, du], [W[pre + "_w_gate"][l], W[pre + "_w_up"][l]],
                                            False, dh, sv["h"], S[pre + "_pre_g"])
    return dh_in


def _gates(zg):
    return [_sigmoid(jnp.concatenate([zg[2 * k].astype(F32), zg[2 * k + 1].astype(F32)], axis=1)) for k in range(3)]


def _merge_fwd(z, rs, ws, tm=ROW_TILE):
    T = z.shape[0]
    tm = _tile(T, tm)
    nb, kk, bw = ws[0].shape

    def body(*refs):
        r_refs, g_refs, w_refs, y_refs, m_ref = refs[:3], refs[3:9], refs[9:12], refs[12:15], refs[15]
        for r_ref, w_ref, y_ref in zip(r_refs, w_refs, y_refs):
            for b in range(nb):
                y_ref[:, b * bw:(b + 1) * bw] = jnp.dot(r_ref[...], w_ref[b],
                                                        preferred_element_type=F32).astype(y_ref.dtype)
        g = _gates([q[...] for q in g_refs])
        m_ref[...] = (g[0] * y_refs[0][...].astype(F32) + g[1] * y_refs[1][...].astype(F32)
                      + g[2] * y_refs[2][...].astype(F32)).astype(m_ref.dtype)

    row = pl.BlockSpec((tm, D_MODEL), lambda i: (i, 0))
    return pl.pallas_call(
        body, grid=(T // tm,),
        in_specs=[pl.BlockSpec((tm, kk), lambda i: (i, 0))] * 3
        + [pl.BlockSpec((tm, BRANCH), lambda i, j=j: (i, ZB_GATES + j)) for j in range(6)]
        + [pl.BlockSpec(ws[0].shape, lambda i: (0, 0, 0))] * 3,
        out_specs=[row] * 4, out_shape=[jax.ShapeDtypeStruct((T, D_MODEL), BF)] * 4,
        name="mix_merge", compiler_params=_params(1))(*rs, *[z] * 6, *ws)


def _merge_bwd(dmerged, z, ys, ws, tm=ROW_TILE // 2):
    T = z.shape[0]
    tm = _tile(T, tm)
    nb, kk, bw = ws[0].shape

    def body(*refs):
        dm_ref, g_refs, y_refs, w_refs = refs[0], refs[1:7], refs[7:10], refs[10:13]
        dy_refs, lo_ref, hi_ref, dr_refs = refs[13:16], refs[16], refs[17], refs[18:21]
        cut = DZ_HALF - ZB_GATES * BRANCH
        dm = dm_ref[...].astype(F32)
        g = _gates([q[...] for q in g_refs])
        for k in range(3):
            dy_refs[k][...] = (dm * g[k]).astype(BF)
            dzg = (dm * y_refs[k][...].astype(F32) * g[k] * (1.0 - g[k])).astype(BF)
            if k == 0:
                lo_ref[...] = dzg[:, :cut]
                hi_ref[:, :D_MODEL - cut] = dzg[:, cut:]
            else:
                hi_ref[:, k * D_MODEL - cut:(k + 1) * D_MODEL - cut] = dzg
            dr = None
            for b in range(nb):
                p = lax.dot_general(dy_refs[k][:, b * bw:(b + 1) * bw], w_refs[k][b], _NT,
                                    preferred_element_type=F32)
                dr = p if dr is None else dr + p
            dr_refs[k][...] = dr.astype(BF)

    row = pl.BlockSpec((tm, D_MODEL), lambda i: (i, 0))
    return pl.pallas_call(
        body, grid=(T // tm,),
        in_specs=[row] + [pl.BlockSpec((tm, BRANCH), lambda i, j=j: (i, ZB_GATES + j)) for j in range(6)] + [row] * 3
        + [pl.BlockSpec(ws[0].shape, lambda i: (0, 0, 0))] * 3,
        out_specs=[row] * 3 + [pl.BlockSpec((tm, DZ_HALF - ZB_GATES * BRANCH), lambda i: (i, 0)),
                               pl.BlockSpec((tm, DZ_HALF), lambda i: (i, 0))]
        + [pl.BlockSpec((tm, kk), lambda i: (i, 0))] * 3,
        out_shape=[jax.ShapeDtypeStruct((T, D_MODEL), BF)] * 3
        + [jax.ShapeDtypeStruct((T, DZ_HALF - ZB_GATES * BRANCH), BF), jax.ShapeDtypeStruct((T, DZ_HALF), BF)]
        + [jax.ShapeDtypeStruct((T, kk), BF)] * 3,
        name="mix_merge_bwd", compiler_params=_params(1))(dmerged, *[z] * 6, *ys, *ws)


def _mix_fwd(l, h, S, W, deps=()):
    n, z = _norm_mm("mix_in", h, S["mix_pre_g"], [W["w_in"][l]], False, deps=deps)
    r_pool = _pool_fwd(z, S["pool_w"], S["pool_scale"])
    r_sgu = _sgu_fwd(z, S["sgu_ln_g"], S["sgu_ln_b"], S["sgu_w_s"], S["sgu_bias"])
    yc = _conv_fwd(z, W["conv_dw_k"], l, S["conv_dw_b"])

    def ln_silu(y, g, b):
        xh, _ = _ln_stats(y.astype(F32))
        return (_silu_and_grad(xh * g + b)[0],)

    r_conv = _rowwise("conv_ln", ln_silu, [(yc, BRANCH, 0)], [S["conv_ln_g"], S["conv_ln_b"]], [(BRANCH, BF)])[0]
    y_pool, y_sgu, y_conv, merged = _merge_fwd(z, (r_pool, r_sgu, r_conv),
                                               [W["w_%s_out" % br][l] for br in ("pool", "sgu", "conv")])
    o, out = _mm_res("mix_out", merged, W["w_out"][l], h, S["mix_post_g"], 1.0)
    return out, dict(h=h, n=n, z=z, r_pool=r_pool, r_sgu=r_sgu, yc=yc, r_conv=r_conv, y_pool=y_pool, y_sgu=y_sgu,
                     y_conv=y_conv, merged=merged, o=o)


def _branch_dw(rs, dys, shape):
    T, kk = rs[0].shape
    nb, _, bw = shape

    def body(*refs):
        for k in range(3):
            refs[6 + k][...] = lax.dot_general(refs[k][...], refs[3 + k][...], _TN,
                                               preferred_element_type=F32).astype(BF)

    return pl.pallas_call(
        body, grid=(nb,),
        in_specs=[pl.BlockSpec((T, kk), lambda b: (0, 0))] * 3 + [pl.BlockSpec((T, bw), lambda b: (0, b))] * 3,
        out_specs=[pl.BlockSpec((None, kk, bw), lambda b: (b, 0, 0))] * 3,
        out_shape=[jax.ShapeDtypeStruct((nb, kk, bw), BF)] * 3, name="branch_dw",
        compiler_params=_params(1))(*rs, *dys)


def _mix_bwd(l, dh, sv, S, W, G, SG, deps=()):
    z = sv["z"]
    do, SG["mix_post_g"], dmerged = _resbwd_mm("mix_bwd_out", dh, sv["o"], S["mix_post_g"], 1.0,
                                               W["w_out"][l].reshape(1, D, D), True, deps=deps)
    G["w_out"] = _mm_tn("mix_dw_out", sv["merged"], do, G["w_out"], l, True)

    branches = ("pool", "sgu", "conv")
    res = _merge_bwd(dmerged, z, [sv["y_" + br] for br in branches], [W["w_%s_out" % br][l] for br in branches])
    dz_gate_lo, dz_hi, dr = res[3], res[4], dict(zip(branches, res[5:]))
    for br, dw in zip(branches, _branch_dw([sv["r_" + br] for br in branches], res[:3], G["w_pool_out"][l].shape)):
        wn = "w_%s_out" % br
        G[wn] = G[wn][:l] + [dw] + G[wn][l + 1:]
    dz_pool, SG["pool_w"], SG["pool_scale"] = _pool_bwd(dr["pool"], z, S["pool_w"], S["pool_scale"])
    dzu, dzv, SG["sgu_ln_g"], SG["sgu_ln_b"], SG["sgu_w_s"], dbias = _sgu_bwd(
        dr["sgu"], z, S["sgu_ln_g"], S["sgu_ln_b"], S["sgu_w_s"], S["sgu_w_st"], S["sgu_bias"])
    SG["sgu_b_s"] = dbias[:, ::LANES].T

    def ln_silu_bwd(d, y, g, b):
        xh, r = _ln_stats(y.astype(F32))
        _, grad = _silu_and_grad(xh * g + b)
        return _ln_bwd(xh, r, g, d.astype(F32) * grad)

    dyc, SG["conv_ln_g"], SG["conv_ln_b"] = _rowwise(
        "conv_ln_bwd", ln_silu_bwd, [(dr["conv"], BRANCH, 0), (sv["yc"], BRANCH, 0)],
        [S["conv_ln_g"], S["conv_ln_b"]], [(BRANCH, BF)], [(1, BRANCH), (1, BRANCH)])
    dza, dzb, SG["conv_dw_k"], SG["conv_dw_b"] = _conv_bwd(dyc, z, W["conv_dw_k"], l)
    dz_lo = jnp.concatenate([dz_pool, dzu, dzv, dza, dzb, dz_gate_lo], axis=1)
    G["w_in"] = _mm_tn("mix_dw_in", sv["n"], dz_lo, G["w_in"], l, False)
    G["w_in"] = _mm_tn("mix_dw_in", sv["n"], dz_hi, G["w_in"], l, False, first=2)
    dh_in, SG["mix_pre_g"] = _dn_prenorm("mix_bwd_in", [dz_lo, dz_hi], W["w_in"][l], True, dh, sv["h"],
                                         S["mix_pre_g"])
    return dh_in


def _ple_out(h, p, gp, w3, g, tm=ROW_TILE):
    T, kp = p.shape
    nb, _, bw = w3.shape
    tm = _tile(T, tm)

    def body(h_ref, p_ref, gp_ref, w_ref, g_ref, e_ref, o_ref):
        p16 = p_ref[...].astype(BF)
        for b in range(nb):
            e_ref[:, b * bw:(b + 1) * bw] = jnp.dot(p16, w_ref[b], preferred_element_type=F32).astype(BF)
        q = _sigmoid(gp_ref[...].astype(F32)) * e_ref[...].astype(F32)
        o_ref[...] = h_ref[...] + _rms_fwd(q, g_ref[...])

    row = pl.BlockSpec((tm, D_MODEL), lambda i: (i, 0))
    return pl.pallas_call(
        body, grid=(T // tm,),
        in_specs=[row, pl.BlockSpec((tm, kp), lambda i: (i, 0)), row, pl.BlockSpec(w3.shape, lambda i: (0, 0, 0)),
                  pl.BlockSpec(g.shape, lambda i: (0, 0))],
        out_specs=[row, row],
        out_shape=[jax.ShapeDtypeStruct((T, D_MODEL), BF), jax.ShapeDtypeStruct((T, D_MODEL), F32)],
        name="ple_out", compiler_params=_params(1))(h, p, gp, w3, g)


def _ple_fwd(l, h, p_l, S, W, deps=()):
    n, gp = _norm_mm("ple_in", h, S["ple_pre_g"], [W["ple_w_gate"][l].reshape(1, D, D)], False, deps=deps)
    e, out = _ple_out(h, p_l, gp, W["ple_w_proj"][l], S["ple_post_g"])
    return out, dict(h=h, n=n, e=e, gp=gp, p=p_l)


def _ple_bwd_rows(dh, e, gp, g_post, w3, h, g_pre, deps=(), tm=ROW_TILE):
    T = dh.shape[0]
    w2 = w3.reshape(D_MODEL, D_MODEL)
    tm = _tile(T, tm)
    n_dep = len(deps)

    def body(*refs):
        dh_ref, e_ref, gp_ref, gpost_ref, w_ref, h_ref, gpre_ref, de_ref, dgp_ref, o_ref, dpost_ref, dpre_ref = \
            refs[n_dep:]
        first = pl.program_id(0) == 0
        d = dh_ref[...]
        sg = _sigmoid(gp_ref[...].astype(F32))
        ee = e_ref[...].astype(F32)
        dq, dpost = _rms_bwd(sg * ee, gpost_ref[...], d)
        de_ref[...] = (dq * sg).astype(BF)
        dgp = (dq * ee * sg * (1.0 - sg)).astype(BF)
        dgp_ref[...] = dgp
        dn = lax.dot_general(dgp, w_ref[...], _NT, preferred_element_type=F32)
        dx, dpre = _rms_bwd(h_ref[...], gpre_ref[...], dn)
        o_ref[...] = d + dx
        _acc_rows(dpost_ref, dpost, first)
        _acc_rows(dpre_ref, dpre, first)

    row = pl.BlockSpec((tm, D_MODEL), lambda i: (i, 0))
    vec = pl.BlockSpec((1, D_MODEL), lambda i: (0, 0))
    return pl.pallas_call(
        body, grid=(T // tm,),
        in_specs=[ANY] * n_dep + [row, row, row, vec, pl.BlockSpec(w2.shape, lambda i: (0, 0)), row, vec],
        out_specs=[row, row, row, vec, vec],
        out_shape=[jax.ShapeDtypeStruct((T, D_MODEL), BF)] * 2 + [jax.ShapeDtypeStruct((T, D_MODEL), F32)]
        + [jax.ShapeDtypeStruct((1, D_MODEL), F32)] * 2,
        name="ple_bwd", compiler_params=_params(1))(*deps, dh, e, gp, g_post, w2, h, g_pre)


def _ple_bwd(l, dh, sv, S, W, G, SG, deps=()):
    de, dgp, dh_in, SG["ple_post_g"], SG["ple_pre_g"] = _ple_bwd_rows(
        dh, sv["e"], sv["gp"], S["ple_post_g"], W["ple_w_gate"][l], sv["h"], S["ple_pre_g"], deps)
    G["ple_w_proj"] = _mm_tn("ple_dw_proj", sv["p"], de, G["ple_w_proj"], l, False)
    G["ple_w_gate"] = _mm_tn("ple_dw_gate", sv["n"], dgp, G["ple_w_gate"], l, True)
    return dh_in


def _layer_small(a, l):
    S = {}
    for name in SMALL:
        v = a[name][l]
        S[name] = v.reshape(1, -1) if v.ndim == 1 else v
    S["sgu_w_st"] = jnp.swapaxes(S["sgu_w_s"], 1, 2)
    S["sgu_bias"] = jnp.repeat(S["sgu_b_s"].T, LANES, axis=1)
    return S


def _layer_fwd(l, h, p_l, S, W, deps=(), hooks=None):
    hooks = hooks or {}

    def after(part, hv):
        return hooks[part](hv) if part in hooks else ()

    h, sv1 = _ffn_fwd(l, h, S, W, "ffn1", deps)
    h, sv2 = _mix_fwd(l, h, S, W, after("ffn1", h))
    h, sv3 = _ffn_fwd(l, h, S, W, "ffn2", after("mix", h))
    h, sv4 = _ple_fwd(l, h, p_l, S, W, after("ffn2", h))
    return h, (sv1, sv2, sv3, sv4)


def _layer_bwd(l, dh, sv, S, W, G, deps=(), hooks=None):
    hooks = hooks or {}

    def after(part, dv):
        return hooks[part](dv) if part in hooks else ()

    SG = {}
    dh = _ple_bwd(l, dh, sv[3], S, W, G, SG, deps)
    dh = _ffn_bwd(l, dh, sv[2], S, W, G, SG, "ffn2")
    dh = _mix_bwd(l, dh, sv[1], S, W, G, SG, after("ffn2", dh))
    dh = _ffn_bwd(l, dh, sv[0], S, W, G, SG, "ffn1", after("mix", dh))
    return dh, SG


HBM = pl.BlockSpec(memory_space=pltpu.HBM)
SEM = pl.BlockSpec(memory_space=pltpu.SEMAPHORE)
SIDE_EFFECT = pltpu.SideEffectType.DATAFLOW_SIDE_EFFECTING


def _place():
    x, y, c = lax.axis_index("x"), lax.axis_index("y"), lax.axis_index("c")
    chips = [(1 - x, y), (x, 1 - y), (1 - x, 1 - y)]
    return x, y, c, chips


def _remote(src, dst, send_sem, recv_sem, to):
    return pltpu.make_async_remote_copy(src_ref=src, dst_ref=dst, send_sem=send_sem, recv_sem=recv_sem,
                                        device_id=to, device_id_type=MESH)


def _split_start(name, plan, bufs, deps):
    count, fn = plan
    n, nd = len(bufs), len(deps)

    def body(*refs):
        send, recv = refs[nd + n], refs[nd + n + 1]
        x, y, c, chips = _place()
        for k, (src, dst, _, to) in enumerate(fn(refs[nd:nd + n], x, y, c, chips)):
            _remote(src, dst, send.at[k], recv.at[k], to).start()
        refs[-1][...] = jnp.zeros_like(refs[-1])

    res = pl.pallas_call(
        body, in_specs=[ANY] * nd + [HBM] * n,
        out_specs=[SEM, SEM] + [HBM] * n + [pl.BlockSpec(memory_space=pltpu.VMEM)],
        out_shape=[pltpu.SemaphoreType.DMA((count,)), pltpu.SemaphoreType.DMA((count,))]
        + [pltpu.HBM(b.shape, b.dtype) for b in bufs] + [jax.ShapeDtypeStruct((8, LANES), F32)],
        input_output_aliases={nd + i: 2 + i for i in range(n)}, name=name,
        compiler_params=pltpu.CompilerParams(has_side_effects=SIDE_EFFECT),
    )(*deps, *[pltpu.with_memory_space_constraint(b, pltpu.HBM) for b in bufs])
    return (res[0], res[1]), list(res[2:2 + n]), res[-1]


def _split_wait(name, plan, sems, bufs, after):
    _, fn = plan
    n = len(bufs)

    def body(*refs):
        send, recv = refs[n], refs[n + 1]
        x, y, c, chips = _place()
        for k, (src, _, land, to) in enumerate(fn(refs[:n], x, y, c, chips)):
            cp = _remote(src, land, send.at[k], recv.at[k], to)
            cp.wait_send()
            cp.wait_recv()

    res = pl.pallas_call(
        body, in_specs=[HBM] * n + [SEM, SEM] + [ANY] * len(after), out_specs=[HBM] * n,
        out_shape=[pltpu.HBM(b.shape, b.dtype) for b in bufs], input_output_aliases={i: i for i in range(n)},
        name=name, compiler_params=pltpu.CompilerParams(has_side_effects=SIDE_EFFECT))(*bufs, *sems, *after)
    return list(res)


def _gather_plans(n):
    def across(b, x, y, c, chips):
        me, out = 2 * x + y, []
        for a in range(n):
            rh = b[a].shape[1] // 2
            mine = b[a].at[me, pl.ds(c * rh, rh)]
            for cx, cy in chips:
                out.append((mine, mine, b[a].at[2 * cx + cy, pl.ds(c * rh, rh)], (cx, cy, c)))
        return out

    def to_sibling(b, x, y, c, chips):
        out = []
        for a in range(n):
            rh = b[a].shape[1] // 2
            for cx, cy in chips:
                piece = b[a].at[2 * cx + cy, pl.ds(c * rh, rh)]
                out.append((piece, piece, b[a].at[2 * cx + cy, pl.ds((1 - c) * rh, rh)], (x, y, 1 - c)))
        return out

    return (3 * n, across), (3 * n, to_sibling)


def _pair_plan(n):
    def fn(b, x, y, c, chips):
        out = []
        for a in range(n):
            rh = b[a].shape[1] // 2
            out.append((b[a].at[:, pl.ds((1 - c) * rh, rh)], b[n + a], b[n + a], (x, y, 1 - c)))
        return out

    return n, fn


def _cross_plan(n):
    def fn(b, x, y, c, chips):
        out = []
        for a in range(n):
            for j, (cx, cy) in enumerate(chips):
                out.append((b[a].at[2 * cx + cy], b[n + a].at[j], b[n + a].at[j], (cx, cy, c)))
        return out

    return 3 * n, fn


def _share_plan(n, l):
    def fn(b, x, y, c, chips):
        out = []
        for a in range(n):
            rh = b[a].shape[1] // 2
            mine = b[a].at[l, pl.ds(c * rh, rh)]
            out.append((mine, mine, b[a].at[l, pl.ds((1 - c) * rh, rh)], (x, y, 1 - c)))
        return out

    return n, fn


def _peers(x, y, c):
    return [(1 - x if m & 4 else x, 1 - y if m & 2 else y, 1 - c if m & 1 else c) for m in range(1, 8)]


def _small_plans():
    def scatter(b, x, y, c, chips):
        return [(b[0].at[4 * px + 2 * py + pc], b[1].at[m], b[1].at[m], (px, py, pc))
                for m, (px, py, pc) in enumerate(_peers(x, y, c))]

    def gather(b, x, y, c, chips):
        mine = b[0].at[4 * x + 2 * y + c]
        return [(mine, mine, b[0].at[4 * px + 2 * py + pc], (px, py, pc)) for px, py, pc in _peers(x, y, c)]

    return (7, scatter), (7, gather)


def _sum_small(v3, got, pos):
    rs = v3.shape[1]
    tm = _tile(rs)
    ins = [(v3, (None, tm, LANES), lambda i, p: (p[2], i, 0))]
    ins += [(got, (None, tm, LANES), lambda i, p, m=m: (m, i, 0)) for m in range(7)]
    return _tiled("sum_small", lambda *t: (((((((t[0] + t[1]) + t[2]) + t[3]) + t[4]) + t[5]) + t[6]) + t[7],),
                  (rs // tm,), pos, ins, [((8, rs, LANES), F32, (None, tm, LANES), lambda i, p: (p[2], i, 0))])[0]


ADD_ROWS = 256


def _multi_tiled(name, fn, pos, groups, in_place=False):
    steps = max(g[2] for g in groups)
    flat_in, in_specs, out_specs, out_shape, counts, dests = [], [], [], [], [], []
    for ins, rows, n_t, (shape, dtype, oidx, dest) in groups:
        for arr, idx in ins:
            flat_in.append(arr)
            in_specs.append(pl.BlockSpec((rows, arr.shape[1]),
                                         lambda i, p, idx=idx, n_t=n_t: (idx(jnp.minimum(i, n_t - 1), p), 0)))
        out_specs.append(pl.BlockSpec((rows, shape[1]),
                                      lambda i, p, oidx=oidx, n_t=n_t: (oidx(jnp.minimum(i, n_t - 1), p), 0)))
        out_shape.append(jax.ShapeDtypeStruct(shape, dtype))
        counts.append((len(ins), n_t))
        dests.append(dest)
    n_in = len(flat_in)
    extra = dests if in_place else []

    def body(_, *refs):
        outs = refs[n_in + len(extra):]
        k = 0
        for (n_a, n_t), o_ref in zip(counts, outs):
            tiles = refs[k:k + n_a]
            k += n_a

            @pl.when(pl.program_id(0) < n_t)
            def _(tiles=tiles, o_ref=o_ref):
                o_ref[...] = fn(*[t[...] for t in tiles]).astype(o_ref.dtype)

    spec = pltpu.PrefetchScalarGridSpec(num_scalar_prefetch=1, grid=(steps,),
                                        in_specs=in_specs + [ANY] * len(extra), out_specs=out_specs)
    return pl.pallas_call(body, grid_spec=spec, out_shape=out_shape,
                          input_output_aliases={1 + n_in + k: k for k in range(len(extra))}, name=name,
                          compiler_params=_params(1))(pos, *flat_in, *extra)


def _add_pair(grads, got, pos):
    groups = []
    for g, q in zip(grads, got):
        nb, R, C = g.shape
        rh = R // 2
        rows = _tile(rh, ADD_ROWS)
        nh = rh // rows
        groups.append(([(g.reshape(nb * R, C), lambda t, p, nh=nh: (t // nh) * 2 * nh + p[1] * nh + t % nh),
                        (q.reshape(nb * rh, C), lambda t, p: t)], rows, nb * nh,
                       ((nb * rh, C), BF, lambda t, p: t, None)))
    res = _multi_tiled("rs_add_pair", lambda u, w: u.astype(F32) + w.astype(F32), pos, groups)
    return [t.reshape(q.shape) for t, q in zip(res, got)]


def _add_chips(parts, slots, reduced, l, pos):
    def add(own, s0, s1, s2):
        return ((own.astype(F32) + s0.astype(F32)) + s1.astype(F32)) + s2.astype(F32)

    groups = []
    for t, s, red in zip(parts, slots, reduced):
        nb, rh, C = t.shape
        L = red.shape[0]
        rows = _tile(rh, ADD_ROWS)
        nh = rh // rows
        ins = [(t.reshape(nb * rh, C), lambda i, p, nh=nh: p[0] * nh + i)]
        ins += [(s.reshape(3 * rh, C), lambda i, p, j=j, nh=nh: j * nh + i) for j in range(3)]
        groups.append((ins, rows, nh, ((L * 2 * rh, C), F32, lambda i, p, nh=nh: l * 2 * nh + p[1] * nh + i,
                                 red.reshape(L * 2 * rh, C))))
    res = _multi_tiled("rs_add_chips", add, pos, groups, in_place=True)
    return [buf.reshape(red.shape) for buf, red in zip(res, reduced)]


def _adamw_math(w, g, m, v):
    m = ADAM_B1 * m + (1.0 - ADAM_B1) * g
    v = ADAM_B2 * v + (1.0 - ADAM_B2) * (g * g)
    m_hat = m / (1.0 - ADAM_B1 ** ADAM_STEP)
    v_hat = v / (1.0 - ADAM_B2 ** ADAM_STEP)
    return -ADAM_LR * (m_hat / (jnp.sqrt(v_hat) + ADAM_EPS) + ADAM_WD * w), m, v


def _adamw(w, g, m, v, lo=0, hi=None, into=None, deps=()):
    L, R, C = w.shape
    hi = L if hi is None else hi
    tr = _tile(R, max(16, ADAM_TILE_ELEMS // C))
    extra = (list(into) if into else []) + list(deps)
    n_alias = 4 if into else 0

    def body(w_ref, g_ref, m_ref, v_ref, *rest):
        go_ref, d_ref, mo_ref, vo_ref = rest[len(extra):]
        gv = g_ref[...]
        d, mn, vn = _adamw_math(w_ref[...], gv, m_ref[...], v_ref[...])
        go_ref[...] = gv
        d_ref[...] = d
        mo_ref[...] = mn
        vo_ref[...] = vn

    spec = pl.BlockSpec((None, tr, C), lambda l, i: (l + lo, i, 0))
    out = jax.ShapeDtypeStruct(w.shape, F32)
    return pl.pallas_call(body, grid=(hi - lo, R // tr), in_specs=[spec] * 4 + [ANY] * len(extra),
                          out_specs=[spec] * 4, out_shape=[out] * 4,
                          input_output_aliases={4 + k: k for k in range(n_alias)}, name="adamw",
                          compiler_params=_params(2))(w, g, m, v, *extra)


def _pack(parts):
    flat = jnp.concatenate([q.reshape(-1, LANES) for q in parts], axis=0)
    return jnp.pad(flat, ((0, -flat.shape[0] % ROW_TILE), (0, 0)))


def _unpack(flat, like):
    out, r = [], 0
    for q in like:
        n = q.size // LANES
        out.append(flat[r:r + n].reshape(q.shape))
        r += n
    return out


def _train_step(a):
    a = dict(a)
    L = a["ffn1_pre_g"].shape[0]
    x, y, c, _ = _place()
    chip = 2 * x + y
    pos = jnp.stack([chip, c, 2 * chip + c]).astype(jnp.int32)
    for name in TRANSPOSED:
        for pre in ("", "m_", "v_"):
            a[pre + name] = jnp.swapaxes(a[pre + name], 1, 2)
    big = [b[0] for b in BIG]
    gathered = big + ["conv_dw_k"]
    n_w, n_g = len(gathered), len(big)

    own = [None] * n_w
    W = {name: [None] * L for name in gathered}
    every = list(range(n_w))
    first, mixer, later = every[:3], every[3:8] + [n_g], every[8:n_g]
    rest = mixer + later

    def cast(i, deps):
        if i == n_g:
            taps = a["conv_dw_k"].reshape(L, CONV_TAPS, LANES)
            return _cast_layers("pad_conv_taps", taps, CONV_PAD, LANES, F32, pos, deps)
        name, _, _, _, rp, cp = BIG[i]
        return _cast_layers("cast_weight", a[name], rp, cp, BF, pos, deps)

    def gather_first(l, ids, tag, deps):
        return _split_start("gather_a%d%s" % (l, tag), _gather_plans(len(ids))[0], [own[i][l] for i in ids], deps)

    def gather_second(l, ids, tag, state, after):
        across, to_sibling = _gather_plans(len(ids))
        bufs = _split_wait("gather_a%d%s_done" % (l, tag), across, state[0], state[1], after)
        return _split_start("gather_b%d%s" % (l, tag), to_sibling, bufs, [])

    def gather_done(l, ids, tag, state, after):
        to_sibling = _gather_plans(len(ids))[1]
        bufs = _split_wait("gather_b%d%s_done" % (l, tag), to_sibling, state[0], state[1], after)
        for i, buf in zip(ids, bufs):
            W[gathered[i]][l] = buf

    for i in first:
        own[i] = cast(i, ())
    state = gather_first(0, first, "f", [])
    for i in rest:
        own[i] = cast(i, (state[2],))
    state = gather_second(0, first, "f", state, [own[i][0] for i in rest])
    gather_done(0, first, "f", state, [])

    parts = {"f": first, "m": mixer, "t": later}
    flying = {}

    def begin(l, part, deps):
        flying[l, part] = gather_first(l, parts[part], part, deps)
        return flying[l, part][2]

    def hand_on(l, part, after):
        flying[l, part] = gather_second(l, parts[part], part, flying[l, part], after)
        return flying[l, part][2]

    def arrive(l, part, after):
        gather_done(l, parts[part], part, flying.pop((l, part)), after)

    def hooks_of(l):
        nxt = l + 1 < L

        def after_ffn1(hv):
            tokens = []
            if l == 0:
                hand_on(0, "m", [hv])
            arrive(l, "m", [hv])
            if l == 0:
                tokens.append(begin(0, "t", [hv]))
            else:
                tokens.append(hand_on(l, "t", [hv]))
            if nxt:
                tokens.append(begin(l + 1, "f", tokens[-1:]))
            return tuple(tokens)

        def after_mix(hv):
            tokens = []
            if l == 0:
                hand_on(0, "t", [hv])
            arrive(l, "t", [hv])
            if nxt:
                tokens.append(hand_on(l + 1, "f", [hv]))
                tokens.append(begin(l + 1, "m", tokens[-1:]))
            return tuple(tokens)

        def after_ffn2(hv):
            tokens = []
            if nxt:
                arrive(l + 1, "f", [hv])
                tokens.append(hand_on(l + 1, "m", [hv]))
                tokens.append(begin(l + 1, "t", tokens[-1:]))
            return tuple(tokens)

        return {"ffn1": after_ffn1, "mix": after_mix, "ffn2": after_ffn2}

    small = [_layer_small(a, l) for l in range(L)]
    h, saved = a["x"][0], []
    deps = (begin(0, "m", []),)
    for l in range(L):
        h, sv = _layer_fwd(l, h, a["p"][l, 0], small[l], W, deps, hooks_of(l))
        saved.append(sv)
        deps = ()

    def loss_fn(yv, t):
        e = yv - t
        return e * (1.0 / D), jnp.sum(e * e, axis=0, keepdims=True)

    dh, lsum = _rowwise("loss", loss_fn, [(h, D, 0), (a["loss_target"][0], D, 0)], [], [(D, F32)], [(1, D)])
    loss = lax.psum(0.5 * jnp.sum(lsum) / D, ("x", "y", "c"))

    G = {name: [jax.ShapeDtypeStruct((N_CHIPS, rp, cp), BF)] * L for name, _, _, _, rp, cp in BIG}
    reduced = [lax.empty((L, rp, cp), F32) for _, _, _, _, rp, cp in BIG]
    small_grads = [None] * L
    whole = list(range(n_g))
    piece_a, piece_b, piece_c = whole[8:], whole[3:8], whole[:3]

    def pair_start(l, ids, tag, deps):
        grads = [G[big[i]][l] for i in ids]
        lands = [lax.empty((N_CHIPS, g.shape[1] // 2, g.shape[2]), BF) for g in grads]
        return _split_start("rs_pair%d%s" % (l, tag), _pair_plan(len(ids)), grads + lands, deps)

    def cross_start(l, ids, tag, state, after):
        n = len(ids)
        bufs = _split_wait("rs_pair%d%s_done" % (l, tag), _pair_plan(n), state[0], state[1], after)
        parts = _add_pair(bufs[:n], bufs[n:], pos)
        lands = [lax.empty((3,) + t.shape[1:], BF) for t in parts]
        return _split_start("rs_cross%d%s" % (l, tag), _cross_plan(n), parts + lands, [])

    def cross_finish(l, ids, tag, state, after, reduced):
        n = len(ids)
        bufs = _split_wait("rs_cross%d%s_done" % (l, tag), _cross_plan(n), state[0], state[1], after)
        reduced = list(reduced)
        for i, r in zip(ids, _add_chips(bufs[:n], bufs[n:], [reduced[i] for i in ids], l, pos)):
            reduced[i] = r
        return reduced

    def share_start(l, reduced):
        return _split_start("rs_share%d" % l, _share_plan(n_g, l), reduced, [])

    def share_done(l, state, after):
        return _split_wait("rs_share%d_done" % l, _share_plan(n_g, l), state[0], state[1], after)

    small_names = SMALL + ("conv_dw_k",)
    scatter, gather = _small_plans()
    totals = [None] * L

    def small_scatter(l, deps):
        packed = _pack([small_grads[l][name] for name in small_names])
        v3 = packed.reshape(8, packed.shape[0] // 8, LANES)
        return _split_start("small_scatter%d" % l, scatter, [v3, lax.empty((7,) + v3.shape[1:], F32)], deps)

    def small_gather(l, state, after):
        bufs = _split_wait("small_scatter%d_done" % l, scatter, state[0], state[1], after)
        return _split_start("small_gather%d" % l, gather, [_sum_small(bufs[0], bufs[1], pos)], [])

    def small_done(l, state, after):
        total = _split_wait("small_gather%d_done" % l, gather, state[0], state[1], after)[0]
        totals[l] = total.reshape(-1, LANES)

    st_pair = st_share = st_small = None
    for l in reversed(range(L)):
        deps = tuple(s[2] for s in (st_pair, st_share, st_small) if s is not None)
        box = {}

        def after_ffn2(dm, l=l, box=box, st_pair=st_pair, st_share=st_share, st_small=st_small):
            out = []
            if st_share is not None:
                box["reduced"] = share_done(l + 2, st_share, [dm])
            if st_small is not None:
                box["small"] = small_gather(l + 1, st_small, [dm])
                out.append(box["small"][2])
            if st_pair is not None:
                box["cross"] = cross_start(l + 1, whole, "", st_pair, [dm])
                out.append(box["cross"][2])
            if l == 0:
                box["pair_a"] = pair_start(0, piece_a, "a", [dm])
                out.append(box["pair_a"][2])
            return tuple(out)

        def after_mix(dm, box=box):
            box["cross_a"] = cross_start(0, piece_a, "a", box["pair_a"], [dm])
            box["pair_b"] = pair_start(0, piece_b, "b", [dm])
            return (box["cross_a"][2], box["pair_b"][2])

        hooks = {"ffn2": after_ffn2, "mix": after_mix} if l == 0 else {"ffn2": after_ffn2}
        dh, small_grads[l] = _layer_bwd(l, dh, saved[l], small[l], W, G, deps, hooks)
        if st_share is not None:
            reduced = box["reduced"]
        if "small" in box:
            small_done(l + 1, box["small"], [dh])
        st_share = None
        if "cross" in box:
            reduced = cross_finish(l + 1, whole, "", box["cross"], [dh], reduced)
            st_share = share_start(l + 1, reduced)
        st_small = small_scatter(l, [dh])
        st_pair = pair_start(l, whole, "", [st_small[2]]) if l else None
    grad_x = dh
    cross_b = cross_start(0, piece_b, "b", box["pair_b"], [st_small[2]])
    cross_c = cross_start(0, piece_c, "c", pair_start(0, piece_c, "c", [cross_b[2]]), [])
    if st_share is not None:
        reduced = share_done(1, st_share, [cross_c[2]])
    upper, token = {}, cross_c[2]
    for k, (name, red) in enumerate(zip(big, reduced)):
        if k == 2:
            st_small = small_gather(0, st_small, [token])
            token = st_small[2]
        if L > 1:
            upper[name] = _adamw(a[name], red, a["m_" + name], a["v_" + name], 1, L, deps=[token])
            token = upper[name][1]
    small_done(0, st_small, [token])
    per_layer = [_unpack(totals[l], [small_grads[l][name] for name in small_names]) for l in range(L)]
    summed = {name: jnp.stack([per_layer[l][k] for l in range(L)]) for k, name in enumerate(small_names)}
    done = [r[1] for r in upper.values()] + [summed[small_names[0]]]
    for ids, tag, state in ((piece_a, "a", box["cross_a"]), (piece_b, "b", cross_b), (piece_c, "c", cross_c)):
        reduced = cross_finish(0, ids, tag, state, done, reduced)
    reduced = share_done(0, share_start(0, reduced), [])
    big_grads = dict(zip(big, reduced))

    grads, deltas, new_m, new_v = {}, {}, {}, {}
    for name in big:
        res = _adamw(a[name], big_grads[name], a["m_" + name], a["v_" + name], 0, 1, upper.get(name))
        if name in TRANSPOSED:
            res = [jnp.swapaxes(r, 1, 2) for r in res]
        grads[name], deltas[name], new_m[name], new_v[name] = res
    taps = lax.dynamic_slice_in_dim(summed["conv_dw_k"], chip * LANES, LANES, axis=2)[:, :CONV_TAPS]
    grads["conv_dw_k"] = taps.reshape(a["conv_dw_k"].shape)
    for name in SMALL:
        grads[name] = summed[name].reshape(a[name].shape)
    shapes = [a[name] for name in small_names]
    res = _adamw(*[_pack([a[pre + name] if pre != "g" else grads[name] for name in small_names])[None]
                   for pre in ("", "g", "m_", "v_")])
    for dst, flat in zip((deltas, new_m, new_v), res[1:]):
        for name, val in zip(small_names, _unpack(flat[0], shapes)):
            dst[name] = val

    return (loss, grad_x[None], *[grads[n] for n in WEIGHTS], *[deltas[n] for n in WEIGHTS],
            *[new_m[n] for n in WEIGHTS], *[new_v[n] for n in WEIGHTS])


def kernel(x, p, ffn1_pre_g, ffn1_w_gate, ffn1_w_up, ffn1_w_down, ffn1_post_g, mix_pre_g, w_in, pool_w, pool_scale, w_pool_out, sgu_ln_g, sgu_ln_b, sgu_w_s, sgu_b_s, w_sgu_out, conv_dw_k, conv_dw_b, conv_ln_g, conv_ln_b, w_conv_out, w_out, mix_post_g, ffn2_pre_g, ffn2_w_gate, ffn2_w_up, ffn2_w_down, ffn2_post_g, ple_w_proj, ple_pre_g, ple_w_gate, ple_post_g, loss_target, m_ffn1_pre_g, m_ffn1_w_gate, m_ffn1_w_up, m_ffn1_w_down, m_ffn1_post_g, m_mix_pre_g, m_w_in, m_pool_w, m_pool_scale, m_w_pool_out, m_sgu_ln_g, m_sgu_ln_b, m_sgu_w_s, m_sgu_b_s, m_w_sgu_out, m_conv_dw_k, m_conv_dw_b, m_conv_ln_g, m_conv_ln_b, m_w_conv_out, m_w_out, m_mix_post_g, m_ffn2_pre_g, m_ffn2_w_gate, m_ffn2_w_up, m_ffn2_w_down, m_ffn2_post_g, m_ple_w_proj, m_ple_pre_g, m_ple_w_gate, m_ple_post_g, v_ffn1_pre_g, v_ffn1_w_gate, v_ffn1_w_up, v_ffn1_w_down, v_ffn1_post_g, v_mix_pre_g, v_w_in, v_pool_w, v_pool_scale, v_w_pool_out, v_sgu_ln_g, v_sgu_ln_b, v_sgu_w_s, v_sgu_b_s, v_w_sgu_out, v_conv_dw_k, v_conv_dw_b, v_conv_ln_g, v_conv_ln_b, v_w_conv_out, v_w_out, v_mix_post_g, v_ffn2_pre_g, v_ffn2_w_gate, v_ffn2_w_up, v_ffn2_w_down, v_ffn2_post_g, v_ple_w_proj, v_ple_pre_g, v_ple_w_gate, v_ple_post_g):
    return _train_step(dict(locals()))
```

```python
import math

import jax
import jax.numpy as jnp
from jax import lax
from jax.experimental import pallas as pl
from jax.experimental.pallas import tpu as pltpu

BF = jnp.bfloat16
F32 = jnp.float32
EPS = 1e-6
D_MODEL = 1024
LANES = 128
SUBLANES = 8
MXU_TILE = 256
N_CHIPS = 4
FFN_SHARD = 704
FFN_SHARD_PAD = 768
POOL_WINDOWS = (2, 4, 8, 16)
SGU_HEADS = 4
CHUNK = 128
CONV_TAPS = 31
CONV_PAD = 32
ROW_TILE = 512
EPI_ROWS = 256
VMEM_LIMIT_BYTES = 56 * 1024 * 1024
ADAM_TILE_ELEMS = 3 * 128 * 1024
ADAM_LR, ADAM_B1, ADAM_B2, ADAM_EPS, ADAM_WD, ADAM_STEP =0.001, 0.9, 0.999, 1e-08, 0.01, 10
MESH = pl.DeviceIdType.MESH
ANY = pl.BlockSpec(memory_space=pl.ANY)

BRANCH = 512
ZB_POOL, ZB_U, ZB_V, ZB_A, ZB_B, ZB_GATES = 0, 1, 2, 3, 4, 5
DZ_HALF = 2816

TRANSPOSED = ("ffn1_w_gate", "ffn1_w_up", "ffn2_w_gate", "ffn2_w_up")
BIG = (
    ("ffn1_w_gate", "row", FFN_SHARD, 1024, FFN_SHARD_PAD, 1024),
    ("ffn1_w_up", "row", FFN_SHARD, 1024, FFN_SHARD_PAD, 1024),
    ("ffn1_w_down", "row", FFN_SHARD, 1024, FFN_SHARD_PAD, 1024),
    ("w_in", "col", 1024, 1408, 1024, 1408),
    ("w_pool_out", "col", 512, 256, 512, 256),
    ("w_sgu_out", "col", 512, 256, 512, 256),
    ("w_conv_out", "col", 512, 256, 512, 256),
    ("w_out", "row", 256, 1024, 256, 1024),
    ("ffn2_w_gate", "row", FFN_SHARD, 1024, FFN_SHARD_PAD, 1024),
    ("ffn2_w_up", "row", FFN_SHARD, 1024, FFN_SHARD_PAD, 1024),
    ("ffn2_w_down", "row", FFN_SHARD, 1024, FFN_SHARD_PAD, 1024),
    ("ple_w_proj", "col", 256, 256, 256, 256),
    ("ple_w_gate", "row", 256, 1024, 256, 1024),
)
SMALL = ("ffn1_pre_g", "ffn1_post_g", "mix_pre_g", "pool_w", "pool_scale", "sgu_ln_g", "sgu_ln_b", "sgu_w_s",
         "sgu_b_s", "conv_dw_b", "conv_ln_g", "conv_ln_b", "mix_post_g", "ffn2_pre_g", "ffn2_post_g",
         "ple_pre_g", "ple_post_g")
WEIGHTS = ("ffn1_pre_g", "ffn1_w_gate", "ffn1_w_up", "ffn1_w_down", "ffn1_post_g", "mix_pre_g", "w_in", "pool_w",
           "pool_scale", "w_pool_out", "sgu_ln_g", "sgu_ln_b", "sgu_w_s", "sgu_b_s", "w_sgu_out", "conv_dw_k",
           "conv_dw_b", "conv_ln_g", "conv_ln_b", "w_conv_out", "w_out", "mix_post_g", "ffn2_pre_g", "ffn2_w_gate",
           "ffn2_w_up", "ffn2_w_down", "ffn2_post_g", "ple_w_proj", "ple_pre_g", "ple_w_gate", "ple_post_g")


def _params(n_grid):
    return pltpu.CompilerParams(dimension_semantics=("arbitrary",) * n_grid, vmem_limit_bytes=VMEM_LIMIT_BYTES)


def _tile(n, cap=ROW_TILE):
    for t in range(min(cap, n) - min(cap, n) % 16, 0, -16):
        if n % t == 0:
            return t
    return n


def _sigmoid(x):
    return 0.5 * jnp.tanh(0.5 * x) + 0.5


def _silu_and_grad(x):
    s = _sigmoid(x)
    return x * s, s * (1.0 + x * (1.0 - s))


def _gelu_and_grad(x):
    cdf = 0.5 * (1.0 + lax.erf(x * (1.0 / math.sqrt(2.0))))
    pdf = jnp.exp(-0.5 * x * x) * (1.0 / math.sqrt(2.0 * math.pi))
    return x * cdf, cdf + x * pdf


def _rms_fwd(x, g):
    return x * lax.rsqrt(jnp.mean(x * x, axis=-1, keepdims=True) + EPS) * g


def _rms_bwd(x, g, dy):
    r = lax.rsqrt(jnp.mean(x * x, axis=-1, keepdims=True) + EPS)
    xh = x * r
    dxh = dy * g
    dx = r * (dxh - xh * jnp.mean(dxh * xh, axis=-1, keepdims=True))
    return dx, jnp.sum(dy * xh, axis=0, keepdims=True)


def _ln_stats(x):
    xc = x - jnp.mean(x, axis=-1, keepdims=True)
    r = lax.rsqrt(jnp.mean(xc * xc, axis=-1, keepdims=True) + EPS)
    return xc * r, r


def _ln_bwd(xh, r, g, dy):
    dxh = dy * g
    dx = r * (dxh - jnp.mean(dxh, axis=-1, keepdims=True) - xh * jnp.mean(dxh * xh, axis=-1, keepdims=True))
    return dx, jnp.sum(dy * xh, axis=0, keepdims=True), jnp.sum(dy, axis=0, keepdims=True)


def _rowwise(name, fn, rows, consts, outs, accs=(), tm=ROW_TILE, deps=()):
    T = rows[0][0].shape[-2]
    tm = _tile(T, tm)
    n_in, n_o, n_dep = len(rows) + len(consts), len(outs), len(deps)

    def body(*refs):
        refs = refs[n_dep:]
        res = fn(*[r[...] for r in refs[:n_in]])
        for ref, val in zip(refs[n_in:n_in + n_o], res[:n_o]):
            ref[...] = val.astype(ref.dtype)
        acc_refs = refs[n_in + n_o:]
        if acc_refs:
            @pl.when(pl.program_id(0) == 0)
            def _():
                for ref, val in zip(acc_refs, res[n_o:]):
                    ref[...] = val

            @pl.when(pl.program_id(0) != 0)
            def _():
                for ref, val in zip(acc_refs, res[n_o:]):
                    ref[...] += val

    in_specs = [ANY] * n_dep
    for row in rows:
        w, cb = row[1], row[2]
        if len(row) == 4:
            in_specs.append(pl.BlockSpec((None, tm, w), lambda i, cb=cb, ld=row[3]: (ld, i, cb)))
        else:
            in_specs.append(pl.BlockSpec((tm, w), lambda i, cb=cb: (i, cb)))
    in_specs += [pl.BlockSpec(c.shape, lambda i: (0, 0)) for c in consts]
    out_specs = [pl.BlockSpec((tm, w), lambda i: (i, 0)) for w, _ in outs]
    out_specs += [pl.BlockSpec(s, lambda i: (0, 0)) for s in accs]
    out_shape = [jax.ShapeDtypeStruct((T, w), dt) for w, dt in outs]
    out_shape += [jax.ShapeDtypeStruct(s, F32) for s in accs]
    return pl.pallas_call(body, grid=(T // tm,), in_specs=in_specs, out_specs=out_specs, out_shape=out_shape,
                          name=name, compiler_params=_params(1))(*deps, *[r[0] for r in rows], *consts)


def _tiled(name, fn, grid, pos, ins, outs):
    n_in = len(ins)

    def body(_, *refs):
        res = fn(*[r[...] for r in refs[:n_in]])
        for ref, val in zip(refs[n_in:], res):
            ref[...] = val.astype(ref.dtype)

    spec = pltpu.PrefetchScalarGridSpec(
        num_scalar_prefetch=1, grid=grid, in_specs=[pl.BlockSpec(bs, im) for _, bs, im in ins],
        out_specs=[pl.BlockSpec(bs, im) for _, _, bs, im in outs])
    return pl.pallas_call(body, grid_spec=spec, out_shape=[jax.ShapeDtypeStruct(s, d) for s, d, _, _ in outs],
                          name=name, compiler_params=_params(len(grid)))(pos, *[a for a, _, _ in ins])


def _cast_layers(name, w, rp, cp, dtype, pos, deps=()):
    L, r, c = w.shape

    def body(_, w_ref, *rest):
        for k, o_ref in enumerate(rest[len(deps):]):
            @pl.when(pl.program_id(0) == k)
            def _(o_ref=o_ref):
                if (rp, cp) != (r, c):
                    o_ref[...] = jnp.zeros_like(o_ref)
                    o_ref[pl.ds(0, r), pl.ds(0, c)] = w_ref[...].astype(dtype)
                else:
                    o_ref[...] = w_ref[...].astype(dtype)

    spec = pltpu.PrefetchScalarGridSpec(
        num_scalar_prefetch=1, grid=(L,),
        in_specs=[pl.BlockSpec((None, r, c), lambda l, p: (l, 0, 0))] + [ANY] * len(deps),
        out_specs=[pl.BlockSpec((None, rp, cp), lambda l, p: (p[0], 0, 0))] * L)
    return pl.pallas_call(body, grid_spec=spec, out_shape=[jax.ShapeDtypeStruct((N_CHIPS, rp, cp), dtype)] * L,
                          name=name, compiler_params=_params(1))(pos, w, *deps)


_NN = (((1,), (0,)), ((), ()))
_NT = (((1,), (1,)), ((), ()))
_TN = (((0,), (0,)), ((), ()))


def _mm_tn(name, a, dy, buf, l, a_blocked, tk=ROW_TILE, first=0):
    T = a.shape[0]
    nb, R, C = buf[l].shape
    extra = [buf[l]] if first else []

    def body(a_ref, dy_ref, *rest):
        rest[-1][...] = lax.dot_general(a_ref[...].astype(BF), dy_ref[...].astype(BF), _TN,
                                        preferred_element_type=F32).astype(BF)

    if a_blocked:
        grid = (nb,)
        in_specs = [pl.BlockSpec((T, R), lambda b: (0, b)), pl.BlockSpec((T, C), lambda b: (0, 0))]
        out_specs = pl.BlockSpec((None, R, C), lambda b: (b, 0, 0))
    else:
        tk = min(tk, R)
        grid = (dy.shape[1] // C, R // tk)
        in_specs = [pl.BlockSpec((T, tk), lambda b, k: (0, k)), pl.BlockSpec((T, C), lambda b, k: (0, b))]
        out_specs = pl.BlockSpec((None, tk, C), lambda b, k: (b + first, k, 0))
    buf = list(buf)
    buf[l] = pl.pallas_call(body, grid=grid, in_specs=in_specs + [ANY] * len(extra), out_specs=out_specs,
                            out_shape=jax.ShapeDtypeStruct((nb, R, C), BF),
                            input_output_aliases={2: 0} if extra else {}, name=name,
                            compiler_params=_params(len(grid)))(a, dy, *extra)
    return buf


def _acc_rows(ref, val, first):
    @pl.when(first)
    def _():
        ref[...] = val

    @pl.when(jnp.logical_not(first))
    def _():
        ref[...] += val


def _norm_mm(name, h, g, ws, trans_w, act=False, deps=(), tm=2 * ROW_TILE):
    T = h.shape[0]
    nb, r, cc = ws[0].shape
    bo = r if trans_w else cc
    tm = _tile(T, tm)
    n_w, n_dep = len(ws), len(deps)

    def body(*refs):
        refs = refs[n_dep:]
        h_ref, g_ref, w_refs = refs[0], refs[1], refs[2:2 + n_w]
        n_ref, o_refs, n_s = refs[2 + n_w], refs[3 + n_w:3 + 2 * n_w], refs[-1]

        @pl.when(pl.program_id(1) == 0)
        def _():
            n = _rms_fwd(h_ref[...].astype(F32), g_ref[...]).astype(BF)
            n_s[...] = n
            n_ref[...] = n

        n = n_s[...]
        prods = []
        for w_ref, o_ref in zip(w_refs, o_refs):
            prods.append(lax.dot_general(n, w_ref[...], _NT if trans_w else _NN,
                                         preferred_element_type=F32).astype(BF))
            o_ref[...] = prods[-1]
        if act:
            refs[3 + 2 * n_w][...] = (_silu_and_grad(prods[0].astype(F32))[0] * prods[1].astype(F32)).astype(BF)

    wide = pl.BlockSpec((tm, bo), lambda i, b: (i, b))
    n_out = n_w + (1 if act else 0)
    return pl.pallas_call(
        body, grid=(T // tm, nb),
        in_specs=[ANY] * n_dep + [pl.BlockSpec((tm, D_MODEL), lambda i, b: (i, 0)),
                                  pl.BlockSpec(g.shape, lambda i, b: (0, 0))]
        + [pl.BlockSpec((None, r, cc), lambda i, b: (b, 0, 0))] * n_w,
        out_specs=[pl.BlockSpec((tm, D_MODEL), lambda i, b: (i, 0))] + [wide] * n_out,
        out_shape=[jax.ShapeDtypeStruct((T, D_MODEL), BF)] + [jax.ShapeDtypeStruct((T, nb * bo), BF)] * n_out,
        scratch_shapes=[pltpu.VMEM((tm, D_MODEL), BF)], name=name, compiler_params=_params(2))(*deps, h, g, *ws)


def _mm_res(name, x, w3, h, g, coef, tm=ROW_TILE):
    T, kx = x.shape
    w2 = w3.reshape(kx, D_MODEL)
    tm = _tile(T, tm)

    def body(x_ref, w_ref, h_ref, g_ref, f_ref, o_ref):
        f = jnp.dot(x_ref[...], w_ref[...], preferred_element_type=F32).astype(BF)
        f_ref[...] = f
        o_ref[...] = h_ref[...] + coef * _rms_fwd(f.astype(F32), g_ref[...])

    row = pl.BlockSpec((tm, D_MODEL), lambda i: (i, 0))
    return pl.pallas_call(
        body, grid=(T // tm,),
        in_specs=[pl.BlockSpec((tm, kx), lambda i: (i, 0)), pl.BlockSpec(w2.shape, lambda i: (0, 0)), row,
                  pl.BlockSpec(g.shape, lambda i: (0, 0))],
        out_specs=[row, row],
        out_shape=[jax.ShapeDtypeStruct((T, D_MODEL), BF), jax.ShapeDtypeStruct((T, D_MODEL), F32)],
        name=name, compiler_params=_params(1))(x, w2, h, g)


def _resbwd_mm(name, dh, f, g, coef, w3, trans_w, act=None, deps=(), tm=2 * ROW_TILE):
    T = dh.shape[0]
    nb, r, cc = w3.shape
    bo = r if trans_w else cc
    tm = _tile(T, tm)
    n_dep, n_act = len(deps), 3 if act else 0
    n_i = T // tm

    def body(*refs):
        refs = refs[n_dep:]
        dh_ref, f_ref, g_ref, w_ref = refs[:4]
        df_ref, dg_ref = refs[4 + n_act], refs[5 + n_act]
        df_s = refs[-2] if act else refs[-1]
        i, b = pl.program_id(0), pl.program_id(1)

        @pl.when(b == 0)
        def _():
            dg = jnp.zeros((1, D_MODEL), F32)
            for c in range(tm // EPI_ROWS):
                rows = slice(c * EPI_ROWS, (c + 1) * EPI_ROWS)
                dx, dg_c = _rms_bwd(f_ref[rows, :].astype(F32), g_ref[...], coef * dh_ref[rows, :])
                df_s[rows, :] = dx.astype(BF)
                df_ref[rows, :] = dx.astype(BF)
                dg = dg + dg_c
            _acc_rows(dg_ref, dg, i == 0)

        if act:
            for j in range(bo // MXU_TILE):
                cols = slice(j * MXU_TILE, (j + 1) * MXU_TILE)
                prod = lax.dot_general(df_s[...], w_ref[cols, :], _NT, preferred_element_type=F32)
                val, grad = _silu_and_grad(refs[4][:, cols])
                prod = prod.astype(BF)
                refs[6 + n_act][:, cols] = prod * refs[5][:, cols] * grad
                refs[7 + n_act][:, cols] = prod * val
            acc = refs[-1]
            part = lax.dot_general(refs[6][...], df_s[...], _TN, preferred_element_type=F32)

            @pl.when(i == 0)
            def _():
                acc[b] = part

            @pl.when(i != 0)
            def _():
                acc[b] += part

            @pl.when(i == n_i - 1)
            def _():
                refs[8 + n_act][...] = acc[b].astype(BF)
        else:
            refs[6][...] = lax.dot_general(df_s[...], w_ref[...], _NT if trans_w else _NN,
                                           preferred_element_type=F32).astype(BF)

    row = pl.BlockSpec((tm, D_MODEL), lambda i, b: (i, 0))
    wide = pl.BlockSpec((tm, bo), lambda i, b: (i, b))
    vec = pl.BlockSpec((1, D_MODEL), lambda i, b: (0, 0))
    out_specs = [row, vec] + [wide] * (2 if act else 1)
    out_shape = [jax.ShapeDtypeStruct((T, D_MODEL), BF), jax.ShapeDtypeStruct((1, D_MODEL), F32)]
    out_shape += [jax.ShapeDtypeStruct((T, nb * bo), BF)] * (2 if act else 1)
    scratch = [pltpu.VMEM((tm, D_MODEL), BF)]
    if act:
        out_specs.append(pl.BlockSpec((None, r, cc), lambda i, b: (jnp.where(i == n_i - 1, b, 0), 0, 0)))
        out_shape.append(jax.ShapeDtypeStruct((nb, r, cc), BF))
        scratch.append(pltpu.VMEM((nb, r, cc), F32))
    return pl.pallas_call(
        body, grid=(n_i, nb),
        in_specs=[ANY] * n_dep + [row, row, vec, pl.BlockSpec((None, r, cc), lambda i, b: (b, 0, 0))] + [wide] * n_act,
        out_specs=out_specs, out_shape=out_shape, scratch_shapes=scratch, name=name,
        compiler_params=_params(2))(*deps, dh, f, g, w3, *(act or ()))


def _dn_prenorm(name, xs, ws, trans_w, dh, h, g, tm=2 * ROW_TILE):
    T = dh.shape[0]
    chained = not isinstance(ws, (list, tuple))
    ws = [ws] if chained else list(ws)
    _, r, cc = ws[0].shape
    bw = cc if trans_w else r
    per_x = xs[0].shape[1] // bw
    nb = per_x * len(xs) if chained else per_x
    tm = _tile(T, tm)
    n_x, n_w = len(xs), len(ws)

    def body(*refs):
        x_refs, w_refs = refs[:n_x], refs[n_x:n_x + n_w]
        dh_ref, h_ref, g_ref, o_ref, dg_ref, acc = refs[n_x + n_w:]
        i, b = pl.program_id(0), pl.program_id(1)

        @pl.when(b == 0)
        def _():
            acc[...] = jnp.zeros_like(acc)

        def add(x_ref, w_ref):
            acc[...] += lax.dot_general(x_ref[...], w_ref[...], _NT if trans_w else _NN, preferred_element_type=F32)

        if chained:
            for k, x_ref in enumerate(x_refs):
                pl.when(b // per_x == k)(lambda x_ref=x_ref: add(x_ref, w_refs[0]))
        else:
            for x_ref, w_ref in zip(x_refs, w_refs):
                add(x_ref, w_ref)

        @pl.when(b == nb - 1)
        def _():
            dg = jnp.zeros((1, D_MODEL), F32)
            for c in range(tm // EPI_ROWS):
                rows = slice(c * EPI_ROWS, (c + 1) * EPI_ROWS)
                dx, dg_c = _rms_bwd(h_ref[rows, :], g_ref[...], acc[rows, :])
                o_ref[rows, :] = dh_ref[rows, :] + dx
                dg = dg + dg_c
            _acc_rows(dg_ref, dg, i == 0)

    row = pl.BlockSpec((tm, D_MODEL), lambda i, b: (i, 0))
    vec = pl.BlockSpec((1, D_MODEL), lambda i, b: (0, 0))
    if chained:
        x_specs = [pl.BlockSpec((tm, bw), lambda i, b, k=k: (i, jnp.clip(b - k * per_x, 0, per_x - 1)))
                   for k in range(n_x)]
    else:
        x_specs = [pl.BlockSpec((tm, bw), lambda i, b: (i, b))] * n_x
    return pl.pallas_call(
        body, grid=(T // tm, nb),
        in_specs=x_specs + [pl.BlockSpec((None, r, cc), lambda i, b: (b, 0, 0))] * n_w + [row, row, vec],
        out_specs=[row, vec],
        out_shape=[jax.ShapeDtypeStruct((T, D_MODEL), F32), jax.ShapeDtypeStruct((1, D_MODEL), F32)],
        scratch_shapes=[pltpu.VMEM((tm, D_MODEL), F32)], name=name,
        compiler_params=_params(2))(*xs, *ws, dh, h, g)


def _pool_apply(x, win, row):
    s, k = x, 1
    while k < win:
        s = s + jnp.where(row >= k, pltpu.roll(s, k, 0), 0.0)
        k *= 2
    return s / jnp.minimum(row + 1, win).astype(F32) - x


def _pool_apply_t(dp, win, row):
    T = dp.shape[0]
    s, k = dp / jnp.minimum(row + 1, win).astype(F32), 1
    while k < win:
        s = s + jnp.where(row < T - k, pltpu.roll(s, T - k, 0), 0.0)
        k *= 2
    return s - dp


def _pool_fwd(z, w, scale):
    T = z.shape[0]

    def body(z_ref, w_ref, s_ref, o_ref):
        row = lax.broadcasted_iota(jnp.int32, (T, LANES), 0)
        for gi, win in enumerate(POOL_WINDOWS):
            cols = pl.ds(gi * LANES, LANES)
            pooled = _pool_apply(z_ref[:, cols].astype(F32), win, row)
            y = jnp.dot(pooled.astype(BF), w_ref[gi].astype(BF), preferred_element_type=F32)
            o_ref[:, cols] = (y * s_ref[:, cols]).astype(o_ref.dtype)

    return pl.pallas_call(
        body, grid=(1,),
        in_specs=[pl.BlockSpec((T, BRANCH), lambda i: (0, ZB_POOL)), pl.BlockSpec(w.shape, lambda i: (0, 0, 0)),
                  pl.BlockSpec(scale.shape, lambda i: (0, 0))],
        out_specs=pl.BlockSpec((T, BRANCH), lambda i: (0, 0)), out_shape=jax.ShapeDtypeStruct((T, BRANCH), BF),
        name="pool_fwd", compiler_params=_params(1))(z, w, scale)


def _pool_bwd(dr, z, w, scale):
    T = z.shape[0]

    def body(dr_ref, z_ref, w_ref, s_ref, dz_ref, dw_ref, ds_ref):
        row = lax.broadcasted_iota(jnp.int32, (T, LANES), 0)
        for gi, win in enumerate(POOL_WINDOWS):
            cols = pl.ds(gi * LANES, LANES)
            pooled = _pool_apply(z_ref[:, cols].astype(F32), win, row).astype(BF)
            wg = w_ref[gi].astype(BF)
            y = jnp.dot(pooled, wg, preferred_element_type=F32)
            d = dr_ref[:, cols].astype(F32)
            ds_ref[:, cols] = jnp.sum(d * y, axis=0, keepdims=True)
            dy = (d * s_ref[:, cols]).astype(BF)
            dw_ref[gi] = lax.dot_general(pooled, dy, _TN, preferred_element_type=F32)
            dpooled = lax.dot_general(dy, wg, _NT, preferred_element_type=F32)
            dz_ref[:, cols] = _pool_apply_t(dpooled, win, row).astype(dz_ref.dtype)

    return pl.pallas_call(
        body, grid=(1,),
        in_specs=[pl.BlockSpec((T, BRANCH), lambda i: (0, 0)), pl.BlockSpec((T, BRANCH), lambda i: (0, ZB_POOL)),
                  pl.BlockSpec(w.shape, lambda i: (0, 0, 0)), pl.BlockSpec(scale.shape, lambda i: (0, 0))],
        out_specs=[pl.BlockSpec((T, BRANCH), lambda i: (0, 0)), pl.BlockSpec(w.shape, lambda i: (0, 0, 0)),
                   pl.BlockSpec(scale.shape, lambda i: (0, 0))],
        out_shape=[jax.ShapeDtypeStruct((T, BRANCH), BF), jax.ShapeDtypeStruct(w.shape, F32),
                   jax.ShapeDtypeStruct(scale.shape, F32)],
        name="pool_bwd", compiler_params=_params(1))(dr, z, w, scale)


def _tril(transposed=False):
    r = lax.broadcasted_iota(jnp.int32, (CHUNK, CHUNK), 0)
    c = lax.broadcasted_iota(jnp.int32, (CHUNK, CHUNK), 1)
    return c >= r if transposed else r >= c


def _sgu_fwd(z, ln_g, ln_b, w_s, bias):
    T = z.shape[0]
    tm = _tile(T)

    def body(zu_ref, zv_ref, g_ref, b_ref, w_ref, bias_ref, o_ref):
        gu, _ = _gelu_and_grad(zu_ref[...].astype(F32))
        gv, _ = _gelu_and_grad(zv_ref[...].astype(F32))
        xh, _ = _ln_stats(gv)
        v16 = (xh * g_ref[...] + b_ref[...]).astype(BF)
        tri = _tril()
        for h in range(SGU_HEADS):
            cols = slice(h * LANES, (h + 1) * LANES)
            wh = jnp.where(tri, w_ref[h], 0.0).astype(BF)
            for c in range(tm // CHUNK):
                rows = slice(c * CHUNK, (c + 1) * CHUNK)
                s = jnp.dot(wh, v16[rows, cols], preferred_element_type=F32) + bias_ref[:, cols]
                o_ref[rows, cols] = (gu[rows, cols] * s).astype(o_ref.dtype)

    small = [pl.BlockSpec(a.shape, lambda i, n=a.ndim: (0,) * n) for a in (ln_g, ln_b, w_s, bias)]
    return pl.pallas_call(
        body, grid=(T // tm,),
        in_specs=[pl.BlockSpec((tm, BRANCH), lambda i: (i, ZB_U)), pl.BlockSpec((tm, BRANCH), lambda i: (i, ZB_V))] + small,
        out_specs=pl.BlockSpec((tm, BRANCH), lambda i: (i, 0)), out_shape=jax.ShapeDtypeStruct((T, BRANCH), BF),
        name="sgu_fwd", compiler_params=_params(1))(z, z, ln_g, ln_b, w_s, bias)


def _sgu_bwd(dr, z, ln_g, ln_b, w_s, w_st, bias):
    T = z.shape[0]
    tm = _tile(T)
    n_steps = T // tm

    def body(dr_ref, zu_ref, zv_ref, g_ref, b_ref, w_ref, wt_ref, bias_ref,
             dzu_ref, dzv_ref, dg_ref, db_ref, dw_ref, dbias_ref, dgu_s, dv_s):
        i = pl.program_id(0)

        @pl.when(i == 0)
        def _():
            dg_ref[...] = jnp.zeros_like(dg_ref)
            db_ref[...] = jnp.zeros_like(db_ref)
            dw_ref[...] = jnp.zeros_like(dw_ref)
            dbias_ref[...] = jnp.zeros_like(dbias_ref)

        zu = zu_ref[...].astype(F32)
        zv = zv_ref[...].astype(F32)
        gu, gu_grad = _gelu_and_grad(zu)
        gv, gv_grad = _gelu_and_grad(zv)
        xh, r = _ln_stats(gv)
        v16 = (xh * g_ref[...] + b_ref[...]).astype(BF)
        dr = dr_ref[...].astype(F32)
        tri = _tril()
        for h in range(SGU_HEADS):
            cols = slice(h * LANES, (h + 1) * LANES)
            wh = jnp.where(tri, w_ref[h], 0.0).astype(BF)
            wht = jnp.where(_tril(transposed=True), wt_ref[h], 0.0).astype(BF)
            for c in range(tm // CHUNK):
                rows = slice(c * CHUNK, (c + 1) * CHUNK)
                v_blk = v16[rows, cols]
                s = jnp.dot(wh, v_blk, preferred_element_type=F32) + bias_ref[:, cols]
                ds = dr[rows, cols] * gu[rows, cols]
                dgu_s[rows, cols] = dr[rows, cols] * s
                ds16 = ds.astype(BF)
                dw_ref[h] += jnp.where(tri, lax.dot_general(ds16, v_blk, _NT, preferred_element_type=F32), 0.0)
                dv_s[rows, cols] = jnp.dot(wht, ds16, preferred_element_type=F32)
                dbias_ref[:, cols] += ds
        dzu_ref[...] = (dgu_s[...] * gu_grad).astype(dzu_ref.dtype)
        dgv, dg, db = _ln_bwd(xh, r, g_ref[...], dv_s[...])
        dzv_ref[...] = (dgv * gv_grad).astype(dzv_ref.dtype)
        dg_ref[...] += dg
        db_ref[...] += db

        @pl.when(i == n_steps - 1)
        def _():
            for h in range(SGU_HEADS):
                cols = slice(h * LANES, (h + 1) * LANES)
                tot = jnp.sum(dbias_ref[:, cols], axis=1, keepdims=True)
                dbias_ref[:, cols] = jnp.broadcast_to(tot, (CHUNK, LANES))

    small = (ln_g, ln_b, w_s, w_st, bias)
    small_specs = [pl.BlockSpec(a.shape, lambda i, n=a.ndim: (0,) * n) for a in small]
    return pl.pallas_call(
        body, grid=(n_steps,),
        in_specs=[pl.BlockSpec((tm, BRANCH), lambda i: (i, 0)), pl.BlockSpec((tm, BRANCH), lambda i: (i, ZB_U)),
                  pl.BlockSpec((tm, BRANCH), lambda i: (i, ZB_V))] + small_specs,
        out_specs=[pl.BlockSpec((tm, BRANCH), lambda i: (i, 0)), pl.BlockSpec((tm, BRANCH), lambda i: (i, 0)),
                   pl.BlockSpec((1, BRANCH), lambda i: (0, 0)), pl.BlockSpec((1, BRANCH), lambda i: (0, 0)),
                   pl.BlockSpec(w_s.shape, lambda i: (0, 0, 0)), pl.BlockSpec(bias.shape, lambda i: (0, 0))],
        out_shape=[jax.ShapeDtypeStruct((T, BRANCH), BF), jax.ShapeDtypeStruct((T, BRANCH), BF),
                   jax.ShapeDtypeStruct((1, BRANCH), F32), jax.ShapeDtypeStruct((1, BRANCH), F32),
                   jax.ShapeDtypeStruct(w_s.shape, F32), jax.ShapeDtypeStruct(bias.shape, F32)],
        scratch_shapes=[pltpu.VMEM((tm, BRANCH), F32), pltpu.VMEM((tm, BRANCH), F32)],
        name="sgu_bwd", compiler_params=_params(1))(dr, z, z, ln_g, ln_b, w_s, w_st, bias)


def _conv_fwd(z, convk, l, bias):
    T = z.shape[0]

    def body(za_ref, zb_ref, k_ref, b_ref, o_ref):
        xg = za_ref[...].astype(F32) * _sigmoid(zb_ref[...].astype(F32))
        xp = jnp.concatenate([jnp.zeros((CONV_PAD, LANES), F32), xg], axis=0)
        kw = k_ref[...]
        acc = jnp.broadcast_to(b_ref[...], (T, LANES))
        for s in range(SUBLANES):
            xs = xp if s == 0 else pltpu.roll(xp, s, 0)
            for q in range(CONV_PAD // SUBLANES):
                k = CONV_TAPS - 1 - (SUBLANES * q + s)
                if k >= 0:
                    lo = CONV_PAD - SUBLANES * q
                    acc = acc + kw[k:k + 1, :] * xs[lo:lo + T, :]
        o_ref[...] = acc.astype(o_ref.dtype)

    return pl.pallas_call(
        body, grid=(4,),
        in_specs=[pl.BlockSpec((T, LANES), lambda g: (0, 4 * ZB_A + g)),
                  pl.BlockSpec((T, LANES), lambda g: (0, 4 * ZB_B + g)),
                  pl.BlockSpec((None, CONV_PAD, LANES), lambda g: (g, 0, 0)),
                  pl.BlockSpec((1, LANES), lambda g: (0, g))],
        out_specs=pl.BlockSpec((T, LANES), lambda g: (0, g)), out_shape=jax.ShapeDtypeStruct((T, BRANCH), BF),
        name="conv_fwd", compiler_params=_params(1))(z, z, convk[l], bias)


def _conv_bwd(dy, z, convk, l):
    T = z.shape[0]

    def body(dy_ref, za_ref, zb_ref, k_ref, dza_ref, dzb_ref, dk_ref, db_ref):
        a = za_ref[...].astype(F32)
        sg = _sigmoid(zb_ref[...].astype(F32))
        d = dy_ref[...].astype(F32)
        kw = k_ref[...]
        xp = jnp.concatenate([jnp.zeros((CONV_PAD, LANES), F32), a * sg], axis=0)
        dp = jnp.concatenate([d, jnp.zeros((CONV_PAD, LANES), F32)], axis=0)
        dxg = jnp.zeros((T, LANES), F32)
        dk_ref[...] = jnp.zeros_like(dk_ref)
        for s in range(SUBLANES):
            xs = xp if s == 0 else pltpu.roll(xp, s, 0)
            ds = dp if s == 0 else pltpu.roll(dp, T + CONV_PAD - s, 0)
            for q in range(CONV_PAD // SUBLANES):
                k = CONV_TAPS - 1 - (SUBLANES * q + s)
                if k >= 0:
                    lo = CONV_PAD - SUBLANES * q
                    dk_ref[k:k + 1, :] = jnp.sum(d * xs[lo:lo + T, :], axis=0, keepdims=True)
                    dxg = dxg + kw[k:k + 1, :] * ds[SUBLANES * q:SUBLANES * q + T, :]
        db_ref[...] = jnp.sum(d, axis=0, keepdims=True)
        dza_ref[...] = (dxg * sg).astype(dza_ref.dtype)
        dzb_ref[...] = (dxg * a * sg * (1.0 - sg)).astype(dzb_ref.dtype)

    col = pl.BlockSpec((T, LANES), lambda g: (0, g))
    return pl.pallas_call(
        body, grid=(4,),
        in_specs=[col, pl.BlockSpec((T, LANES), lambda g: (0, 4 * ZB_A + g)),
                  pl.BlockSpec((T, LANES), lambda g: (0, 4 * ZB_B + g)),
                  pl.BlockSpec((None, CONV_PAD, LANES), lambda g: (g, 0, 0))],
        out_specs=[col, col, pl.BlockSpec((CONV_PAD, LANES), lambda g: (0, g)),
                   pl.BlockSpec((1, LANES), lambda g: (0, g))],
        out_shape=[jax.ShapeDtypeStruct((T, BRANCH), BF), jax.ShapeDtypeStruct((T, BRANCH), BF),
                   jax.ShapeDtypeStruct((CONV_PAD, BRANCH), F32), jax.ShapeDtypeStruct((1, BRANCH), F32)],
        name="conv_bwd", compiler_params=_params(1))(dy, z, z, convk[l])


D = D_MODEL


def _ffn_fwd(l, h, S, W, pre, deps=()):
    n, gp, u, a = _norm_mm("ffn_in", h, S[pre + "_pre_g"], [W[pre + "_w_gate"][l], W[pre + "_w_up"][l]], True,
                           act=True, deps=deps)
    f, out = _mm_res("ffn_out", a, W[pre + "_w_down"][l], h, S[pre + "_post_g"], 0.5)
    return out, dict(h=h, n=n, gp=gp, u=u, a=a, f=f)


def _ffn_bwd(l, dh, sv, S, W, G, SG, pre, deps=()):
    df, SG[pre + "_post_g"], dgp, du, dwd = _resbwd_mm(
        "ffn_bwd_act", dh, sv["f"], S[pre + "_post_g"], 0.5, W[pre + "_w_down"][l], True,
        act=(sv["gp"], sv["u"], sv["a"]), deps=deps, tm=ROW_TILE)
    G[pre + "_w_down"] = G[pre + "_w_down"][:l] + [dwd] + G[pre + "_w_down"][l + 1:]
    G[pre + "_w_gate"] = _mm_tn("ffn_dw_gate", dgp, sv["n"], G[pre + "_w_gate"], l, True)
    G[pre + "_w_up"] = _mm_tn("ffn_dw_up", du, sv["n"], G[pre + "_w_up"], l, True)
    dh_in, SG[pre + "_pre_g"] = _dn_prenorm("ffn_bwd_in", [dgp, du], [W[pre + "_w_gate"][l], W[pre + "_w_up"][l]],
                                            False, dh, sv["h"], S[pre + "_pre_g"])
    return dh_in


def _gates(zg):
    return [_sigmoid(jnp.concatenate([zg[2 * k].astype(F32), zg[2 * k + 1].astype(F32)], axis=1)) for k in range(3)]


def _merge_fwd(z, rs, ws, tm=ROW_TILE):
    T = z.shape[0]
    tm = _tile(T, tm)
    nb, kk, bw = ws[0].shape

    def body(*refs):
        r_refs, g_refs, w_refs, y_refs, m_ref = refs[:3], refs[3:9], refs[9:12], refs[12:15], refs[15]
        for r_ref, w_ref, y_ref in zip(r_refs, w_refs, y_refs):
            for b in range(nb):
                y_ref[:, b * bw:(b + 1) * bw] = jnp.dot(r_ref[...], w_ref[b],
                                                        preferred_element_type=F32).astype(y_ref.dtype)
        g = _gates([q[...] for q in g_refs])
        m_ref[...] = (g[0] * y_refs[0][...].astype(F32) + g[1] * y_refs[1][...].astype(F32)
                      + g[2] * y_refs[2][...].astype(F32)).astype(m_ref.dtype)

    row = pl.BlockSpec((tm, D_MODEL), lambda i: (i, 0))
    return pl.pallas_call(
        body, grid=(T // tm,),
        in_specs=[pl.BlockSpec((tm, kk), lambda i: (i, 0))] * 3
        + [pl.BlockSpec((tm, BRANCH), lambda i, j=j: (i, ZB_GATES + j)) for j in range(6)]
        + [pl.BlockSpec(ws[0].shape, lambda i: (0, 0, 0))] * 3,
        out_specs=[row] * 4, out_shape=[jax.ShapeDtypeStruct((T, D_MODEL), BF)] * 4,
        name="mix_merge", compiler_params=_params(1))(*rs, *[z] * 6, *ws)


def _merge_bwd(dmerged, z, ys, ws, tm=ROW_TILE // 2):
    T = z.shape[0]
    tm = _tile(T, tm)
    nb, kk, bw = ws[0].shape

    def body(*refs):
        dm_ref, g_refs, y_refs, w_refs = refs[0], refs[1:7], refs[7:10], refs[10:13]
        dy_refs, lo_ref, hi_ref, dr_refs = refs[13:16], refs[16], refs[17], refs[18:21]
        cut = DZ_HALF - ZB_GATES * BRANCH
        dm = dm_ref[...].astype(F32)
        g = _gates([q[...] for q in g_refs])
        for k in range(3):
            dy_refs[k][...] = (dm * g[k]).astype(BF)
            dzg = (dm * y_refs[k][...].astype(F32) * g[k] * (1.0 - g[k])).astype(BF)
            if k == 0:
                lo_ref[...] = dzg[:, :cut]
                hi_ref[:, :D_MODEL - cut] = dzg[:, cut:]
            else:
                hi_ref[:, k * D_MODEL - cut:(k + 1) * D_MODEL - cut] = dzg
            dr = None
            for b in range(nb):
                p = lax.dot_general(dy_refs[k][:, b * bw:(b + 1) * bw], w_refs[k][b], _NT,
                                    preferred_element_type=F32)
                dr = p if dr is None else dr + p
            dr_refs[k][...] = dr.astype(BF)

    row = pl.BlockSpec((tm, D_MODEL), lambda i: (i, 0))
    return pl.pallas_call(
        body, grid=(T // tm,),
        in_specs=[row] + [pl.BlockSpec((tm, BRANCH), lambda i, j=j: (i, ZB_GATES + j)) for j in range(6)] + [row] * 3
        + [pl.BlockSpec(ws[0].shape, lambda i: (0, 0, 0))] * 3,
        out_specs=[row] * 3 + [pl.BlockSpec((tm, DZ_HALF - ZB_GATES * BRANCH), lambda i: (i, 0)),
                               pl.BlockSpec((tm, DZ_HALF), lambda i: (i, 0))]
        + [pl.BlockSpec((tm, kk), lambda i: (i, 0))] * 3,
        out_shape=[jax.ShapeDtypeStruct((T, D_MODEL), BF)] * 3
        + [jax.ShapeDtypeStruct((T, DZ_HALF - ZB_GATES * BRANCH), BF), jax.ShapeDtypeStruct((T, DZ_HALF), BF)]
        + [jax.ShapeDtypeStruct((T, kk), BF)] * 3,
        name="mix_merge_bwd", compiler_params=_params(1))(dmerged, *[z] * 6, *ys, *ws)


def _mix_fwd(l, h, S, W, deps=()):
    n, z = _norm_mm("mix_in", h, S["mix_pre_g"], [W["w_in"][l]], False, deps=deps)
    r_pool = _pool_fwd(z, S["pool_w"], S["pool_scale"])
    r_sgu = _sgu_fwd(z, S["sgu_ln_g"], S["sgu_ln_b"], S["sgu_w_s"], S["sgu_bias"])
    yc = _conv_fwd(z, W["conv_dw_k"], l, S["conv_dw_b"])

    def ln_silu(y, g, b):
        xh, _ = _ln_stats(y.astype(F32))
        return (_silu_and_grad(xh * g + b)[0],)

    r_conv = _rowwise("conv_ln", ln_silu, [(yc, BRANCH, 0)], [S["conv_ln_g"], S["conv_ln_b"]], [(BRANCH, BF)])[0]
    y_pool, y_sgu, y_conv, merged = _merge_fwd(z, (r_pool, r_sgu, r_conv),
                                               [W["w_%s_out" % br][l] for br in ("pool", "sgu", "conv")])
    o, out = _mm_res("mix_out", merged, W["w_out"][l], h, S["mix_post_g"], 1.0)
    return out, dict(h=h, n=n, z=z, r_pool=r_pool, r_sgu=r_sgu, yc=yc, r_conv=r_conv, y_pool=y_pool, y_sgu=y_sgu,
                     y_conv=y_conv, merged=merged, o=o)


def _branch_dw(rs, dys, shape):
    T, kk = rs[0].shape
    nb, _, bw = shape

    def body(*refs):
        for k in range(3):
            refs[6 + k][...] = lax.dot_general(refs[k][...], refs[3 + k][...], _TN,
                                               preferred_element_type=F32).astype(BF)

    return pl.pallas_call(
        body, grid=(nb,),
        in_specs=[pl.BlockSpec((T, kk), lambda b: (0, 0))] * 3 + [pl.BlockSpec((T, bw), lambda b: (0, b))] * 3,
        out_specs=[pl.BlockSpec((None, kk, bw), lambda b: (b, 0, 0))] * 3,
        out_shape=[jax.ShapeDtypeStruct((nb, kk, bw), BF)] * 3, name="branch_dw",
        compiler_params=_params(1))(*rs, *dys)


def _mix_bwd(l, dh, sv, S, W, G, SG, deps=()):
    z = sv["z"]
    do, SG["mix_post_g"], dmerged = _resbwd_mm("mix_bwd_out", dh, sv["o"], S["mix_post_g"], 1.0,
                                               W["w_out"][l].reshape(1, D, D), True, deps=deps)
    G["w_out"] = _mm_tn("mix_dw_out", sv["merged"], do, G["w_out"], l, True)

    branches = ("pool", "sgu", "conv")
    res = _merge_bwd(dmerged, z, [sv["y_" + br] for br in branches], [W["w_%s_out" % br][l] for br in branches])
    dz_gate_lo, dz_hi, dr = res[3], res[4], dict(zip(branches, res[5:]))
    for br, dw in zip(branches, _branch_dw([sv["r_" + br] for br in branches], res[:3], G["w_pool_out"][l].shape)):
        wn = "w_%s_out" % br
        G[wn] = G[wn][:l] + [dw] + G[wn][l + 1:]
    dz_pool, SG["pool_w"], SG["pool_scale"] = _pool_bwd(dr["pool"], z, S["pool_w"], S["pool_scale"])
    dzu, dzv, SG["sgu_ln_g"], SG["sgu_ln_b"], SG["sgu_w_s"], dbias = _sgu_bwd(
        dr["sgu"], z, S["sgu_ln_g"], S["sgu_ln_b"], S["sgu_w_s"], S["sgu_w_st"], S["sgu_bias"])
    SG["sgu_b_s"] = dbias[:, ::LANES].T

    def ln_silu_bwd(d, y, g, b):
        xh, r = _ln_stats(y.astype(F32))
        _, grad = _silu_and_grad(xh * g + b)
        return _ln_bwd(xh, r, g, d.astype(F32) * grad)

    dyc, SG["conv_ln_g"], SG["conv_ln_b"] = _rowwise(
        "conv_ln_bwd", ln_silu_bwd, [(dr["conv"], BRANCH, 0), (sv["yc"], BRANCH, 0)],
        [S["conv_ln_g"], S["conv_ln_b"]], [(BRANCH, BF)], [(1, BRANCH), (1, BRANCH)])
    dza, dzb, SG["conv_dw_k"], SG["conv_dw_b"] = _conv_bwd(dyc, z, W["conv_dw_k"], l)
    dz_lo = jnp.concatenate([dz_pool, dzu, dzv, dza, dzb, dz_gate_lo], axis=1)
    G["w_in"] = _mm_tn("mix_dw_in", sv["n"], dz_lo, G["w_in"], l, False)
    G["w_in"] = _mm_tn("mix_dw_in", sv["n"], dz_hi, G["w_in"], l, False, first=2)
    dh_in, SG["mix_pre_g"] = _dn_prenorm("mix_bwd_in", [dz_lo, dz_hi], W["w_in"][l], True, dh, sv["h"],
                                         S["mix_pre_g"])
    return dh_in


def _ple_out(h, p, gp, w3, g, tm=ROW_TILE):
    T, kp = p.shape
    nb, _, bw = w3.shape
    tm = _tile(T, tm)

    def body(h_ref, p_ref, gp_ref, w_ref, g_ref, e_ref, o_ref):
        p16 = p_ref[...].astype(BF)
        for b in range(nb):
            e_ref[:, b * bw:(b + 1) * bw] = jnp.dot(p16, w_ref[b], preferred_element_type=F32).astype(BF)
        q = _sigmoid(gp_ref[...].astype(F32)) * e_ref[...].astype(F32)
        o_ref[...] = h_ref[...] + _rms_fwd(q, g_ref[...])

    row = pl.BlockSpec((tm, D_MODEL), lambda i: (i, 0))
    return pl.pallas_call(
        body, grid=(T // tm,),
        in_specs=[row, pl.BlockSpec((tm, kp), lambda i: (i, 0)), row, pl.BlockSpec(w3.shape, lambda i: (0, 0, 0)),
                  pl.BlockSpec(g.shape, lambda i: (0, 0))],
        out_specs=[row, row],
        out_shape=[jax.ShapeDtypeStruct((T, D_MODEL), BF), jax.ShapeDtypeStruct((T, D_MODEL), F32)],
        name="ple_out", compiler_params=_params(1))(h, p, gp, w3, g)


def _ple_fwd(l, h, p_l, S, W, deps=()):
    n, gp = _norm_mm("ple_in", h, S["ple_pre_g"], [W["ple_w_gate"][l].reshape(1, D, D)], False, deps=deps)
    e, out = _ple_out(h, p_l, gp, W["ple_w_proj"][l], S["ple_post_g"])
    return out, dict(h=h, n=n, e=e, gp=gp, p=p_l)


def _ple_bwd_rows(dh, e, gp, g_post, w3, h, g_pre, deps=(), tm=ROW_TILE):
    T = dh.shape[0]
    w2 = w3.reshape(D_MODEL, D_MODEL)
    tm = _tile(T, tm)
    n_dep = len(deps)

    def body(*refs):
        dh_ref, e_ref, gp_ref, gpost_ref, w_ref, h_ref, gpre_ref, de_ref, dgp_ref, o_ref, dpost_ref, dpre_ref = \
            refs[n_dep:]
        first = pl.program_id(0) == 0
        d = dh_ref[...]
        sg = _sigmoid(gp_ref[...].astype(F32))
        ee = e_ref[...].astype(F32)
        dq, dpost = _rms_bwd(sg * ee, gpost_ref[...], d)
        de_ref[...] = (dq * sg).astype(BF)
        dgp = (dq * ee * sg * (1.0 - sg)).astype(BF)
        dgp_ref[...] = dgp
        dn = lax.dot_general(dgp, w_ref[...], _NT, preferred_element_type=F32)
        dx, dpre = _rms_bwd(h_ref[...], gpre_ref[...], dn)
        o_ref[...] = d + dx
        _acc_rows(dpost_ref, dpost, first)
        _acc_rows(dpre_ref, dpre, first)

    row = pl.BlockSpec((tm, D_MODEL), lambda i: (i, 0))
    vec = pl.BlockSpec((1, D_MODEL), lambda i: (0, 0))
    return pl.pallas_call(
        body, grid=(T // tm,),
        in_specs=[ANY] * n_dep + [row, row, row, vec, pl.BlockSpec(w2.shape, lambda i: (0, 0)), row, vec],
        out_specs=[row, row, row, vec, vec],
        out_shape=[jax.ShapeDtypeStruct((T, D_MODEL), BF)] * 2 + [jax.ShapeDtypeStruct((T, D_MODEL), F32)]
        + [jax.ShapeDtypeStruct((1, D_MODEL), F32)] * 2,
        name="ple_bwd", compiler_params=_params(1))(*deps, dh, e, gp, g_post, w2, h, g_pre)


def _ple_bwd(l, dh, sv, S, W, G, SG, deps=()):
    de, dgp, dh_in, SG["ple_post_g"], SG["ple_pre_g"] = _ple_bwd_rows(
        dh, sv["e"], sv["gp"], S["ple_post_g"], W["ple_w_gate"][l], sv["h"], S["ple_pre_g"], deps)
    G["ple_w_proj"] = _mm_tn("ple_dw_proj", sv["p"], de, G["ple_w_proj"], l, False)
    G["ple_w_gate"] = _mm_tn("ple_dw_gate", sv["n"], dgp, G["ple_w_gate"], l, True)
    return dh_in


def _layer_small(a, l):
    S = {}
    for name in SMALL:
        v = a[name][l]
        S[name] = v.reshape(1, -1) if v.ndim == 1 else v
    S["sgu_w_st"] = jnp.swapaxes(S["sgu_w_s"], 1, 2)
    S["sgu_bias"] = jnp.repeat(S["sgu_b_s"].T, LANES, axis=1)
    return S


def _layer_fwd(l, h, p_l, S, W, deps=(), hooks=None):
    hooks = hooks or {}

    def after(part, hv):
        return hooks[part](hv) if part in hooks else ()

    h, sv1 = _ffn_fwd(l, h, S, W, "ffn1", deps)
    h, sv2 = _mix_fwd(l, h, S, W, after("ffn1", h))
    h, sv3 = _ffn_fwd(l, h, S, W, "ffn2", after("mix", h))
    h, sv4 = _ple_fwd(l, h, p_l, S, W, after("ffn2", h))
    return h, (sv1, sv2, sv3, sv4)


def _layer_bwd(l, dh, sv, S, W, G, deps=(), hooks=None):
    hooks = hooks or {}

    def after(part, dv):
        return hooks[part](dv) if part in hooks else ()

    SG = {}
    dh = _ple_bwd(l, dh, sv[3], S, W, G, SG, deps)
    dh = _ffn_bwd(l, dh, sv[2], S, W, G, SG, "ffn2")
    dh = _mix_bwd(l, dh, sv[1], S, W, G, SG, after("ffn2", dh))
    dh = _ffn_bwd(l, dh, sv[0], S, W, G, SG, "ffn1", after("mix", dh))
    return dh, SG


HBM = pl.BlockSpec(memory_space=pltpu.HBM)
SEM = pl.BlockSpec(memory_space=pltpu.SEMAPHORE)
SIDE_EFFECT = pltpu.SideEffectType.DATAFLOW_SIDE_EFFECTING


def _place():
    x, y, c = lax.axis_index("x"), lax.axis_index("y"), lax.axis_index("c")
    chips = [(1 - x, y), (x, 1 - y), (1 - x, 1 - y)]
    return x, y, c, chips


def _remote(src, dst, send_sem, recv_sem, to):
    return pltpu.make_async_remote_copy(src_ref=src, dst_ref=dst, send_sem=send_sem, recv_sem=recv_sem,
                                        device_id=to, device_id_type=MESH)


def _split_start(name, plan, bufs, deps):
    count, fn = plan
    n, nd = len(bufs), len(deps)

    def body(*refs):
        send, recv = refs[nd + n], refs[nd + n + 1]
        x, y, c, chips = _place()
        for k, (src, dst, _, to) in enumerate(fn(refs[nd:nd + n], x, y, c, chips)):
            _remote(src, dst, send.at[k], recv.at[k], to).start()
        refs[-1][...] = jnp.zeros_like(refs[-1])

    res = pl.pallas_call(
        body, in_specs=[ANY] * nd + [HBM] * n,
        out_specs=[SEM, SEM] + [HBM] * n + [pl.BlockSpec(memory_space=pltpu.VMEM)],
        out_shape=[pltpu.SemaphoreType.DMA((count,)), pltpu.SemaphoreType.DMA((count,))]
        + [pltpu.HBM(b.shape, b.dtype) for b in bufs] + [jax.ShapeDtypeStruct((8, LANES), F32)],
        input_output_aliases={nd + i: 2 + i for i in range(n)}, name=name,
        compiler_params=pltpu.CompilerParams(has_side_effects=SIDE_EFFECT),
    )(*deps, *[pltpu.with_memory_space_constraint(b, pltpu.HBM) for b in bufs])
    return (res[0], res[1]), list(res[2:2 + n]), res[-1]


def _split_wait(name, plan, sems, bufs, after):
    _, fn = plan
    n = len(bufs)

    def body(*refs):
        send, recv = refs[n], refs[n + 1]
        x, y, c, chips = _place()
        for k, (src, _, land, to) in enumerate(fn(refs[:n], x, y, c, chips)):
            cp = _remote(src, land, send.at[k], recv.at[k], to)
            cp.wait_send()
            cp.wait_recv()

    res = pl.pallas_call(
        body, in_specs=[HBM] * n + [SEM, SEM] + [ANY] * len(after), out_specs=[HBM] * n,
        out_shape=[pltpu.HBM(b.shape, b.dtype) for b in bufs], input_output_aliases={i: i for i in range(n)},
        name=name, compiler_params=pltpu.CompilerParams(has_side_effects=SIDE_EFFECT))(*bufs, *sems, *after)
    return list(res)


def _gather_plans(n):
    def across(b, x, y, c, chips):
        me, out = 2 * x + y, []
        for a in range(n):
            rh = b[a].shape[1] // 2
            mine = b[a].at[me, pl.ds(c * rh, rh)]
            for cx, cy in chips:
                out.append((mine, mine, b[a].at[2 * cx + cy, pl.ds(c * rh, rh)], (cx, cy, c)))
        return out

    def to_sibling(b, x, y, c, chips):
        out = []
        for a in range(n):
            rh = b[a].shape[1] // 2
            for cx, cy in chips:
                piece = b[a].at[2 * cx + cy, pl.ds(c * rh, rh)]
                out.append((piece, piece, b[a].at[2 * cx + cy, pl.ds((1 - c) * rh, rh)], (x, y, 1 - c)))
        return out

    return (3 * n, across), (3 * n, to_sibling)


def _pair_plan(n):
    def fn(b, x, y, c, chips):
        out = []
        for a in range(n):
            rh = b[a].shape[1] // 2
            out.append((b[a].at[:, pl.ds((1 - c) * rh, rh)], b[n + a], b[n + a], (x, y, 1 - c)))
        return out

    return n, fn


def _cross_plan(n):
    def fn(b, x, y, c, chips):
        out = []
        for a in range(n):
            for j, (cx, cy) in enumerate(chips):
                out.append((b[a].at[2 * cx + cy], b[n + a].at[j], b[n + a].at[j], (cx, cy, c)))
        return out

    return 3 * n, fn


def _share_plan(n, l):
    def fn(b, x, y, c, chips):
        out = []
        for a in range(n):
            rh = b[a].shape[1] // 2
            mine = b[a].at[l, pl.ds(c * rh, rh)]
            out.append((mine, mine, b[a].at[l, pl.ds((1 - c) * rh, rh)], (x, y, 1 - c)))
        return out

    return n, fn


def _peers(x, y, c):
    return [(1 - x if m & 4 else x, 1 - y if m & 2 else y, 1 - c if m & 1 else c) for m in range(1, 8)]


def _small_plans():
    def scatter(b, x, y, c, chips):
        return [(b[0].at[4 * px + 2 * py + pc], b[1].at[m], b[1].at[m], (px, py, pc))
                for m, (px, py, pc) in enumerate(_peers(x, y, c))]

    def gather(b, x, y, c, chips):
        mine = b[0].at[4 * x + 2 * y + c]
        return [(mine, mine, b[0].at[4 * px + 2 * py + pc], (px, py, pc)) for px, py, pc in _peers(x, y, c)]

    return (7, scatter), (7, gather)


def _sum_small(v3, got, pos):
    rs = v3.shape[1]
    tm = _tile(rs)
    ins = [(v3, (None, tm, LANES), lambda i, p: (p[2], i, 0))]
    ins += [(got, (None, tm, LANES), lambda i, p, m=m: (m, i, 0)) for m in range(7)]
    return _tiled("sum_small", lambda *t: (((((((t[0] + t[1]) + t[2]) + t[3]) + t[4]) + t[5]) + t[6]) + t[7],),
                  (rs // tm,), pos, ins, [((8, rs, LANES), F32, (None, tm, LANES), lambda i, p: (p[2], i, 0))])[0]


ADD_ROWS = 256


def _multi_tiled(name, fn, pos, groups, in_place=False):
    steps = max(g[2] for g in groups)
    flat_in, in_specs, out_specs, out_shape, counts, dests = [], [], [], [], [], []
    for ins, rows, n_t, (shape, dtype, oidx, dest) in groups:
        for arr, idx in ins:
            flat_in.append(arr)
            in_specs.append(pl.BlockSpec((rows, arr.shape[1]),
                                         lambda i, p, idx=idx, n_t=n_t: (idx(jnp.minimum(i, n_t - 1), p), 0)))
        out_specs.append(pl.BlockSpec((rows, shape[1]),
                                      lambda i, p, oidx=oidx, n_t=n_t: (oidx(jnp.minimum(i, n_t - 1), p), 0)))
        out_shape.append(jax.ShapeDtypeStruct(shape, dtype))
        counts.append((len(ins), n_t))
        dests.append(dest)
    n_in = len(flat_in)
    extra = dests if in_place else []

    def body(_, *refs):
        outs = refs[n_in + len(extra):]
        k = 0
        for (n_a, n_t), o_ref in zip(counts, outs):
            tiles = refs[k:k + n_a]
            k += n_a

            @pl.when(pl.program_id(0) < n_t)
            def _(tiles=tiles, o_ref=o_ref):
                o_ref[...] = fn(*[t[...] for t in tiles]).astype(o_ref.dtype)

    spec = pltpu.PrefetchScalarGridSpec(num_scalar_prefetch=1, grid=(steps,),
                                        in_specs=in_specs + [ANY] * len(extra), out_specs=out_specs)
    return pl.pallas_call(body, grid_spec=spec, out_shape=out_shape,
                          input_output_aliases={1 + n_in + k: k for k in range(len(extra))}, name=name,
                          compiler_params=_params(1))(pos, *flat_in, *extra)


def _add_pair(grads, got, pos):
    groups = []
    for g, q in zip(grads, got):
        nb, R, C = g.shape
        rh = R // 2
        rows = _tile(rh, ADD_ROWS)
        nh = rh // rows
        groups.append(([(g.reshape(nb * R, C), lambda t, p, nh=nh: (t // nh) * 2 * nh + p[1] * nh + t % nh),
                        (q.reshape(nb * rh, C), lambda t, p: t)], rows, nb * nh,
                       ((nb * rh, C), BF, lambda t, p: t, None)))
    res = _multi_tiled("rs_add_pair", lambda u, w: u.astype(F32) + w.astype(F32), pos, groups)
    return [t.reshape(q.shape) for t, q in zip(res, got)]


def _add_chips(parts, slots, reduced, l, pos):
    def add(own, s0, s1, s2):
        return ((own.astype(F32) + s0.astype(F32)) + s1.astype(F32)) + s2.astype(F32)

    groups = []
    for t, s, red in zip(parts, slots, reduced):
        nb, rh, C = t.shape
        L = red.shape[0]
        rows = _tile(rh, ADD_ROWS)
        nh = rh // rows
        ins = [(t.reshape(nb * rh, C), lambda i, p, nh=nh: p[0] * nh + i)]
        ins += [(s.reshape(3 * rh, C), lambda i, p, j=j, nh=nh: j * nh + i) for j in range(3)]
        groups.append((ins, rows, nh, ((L * 2 * rh, C), F32, lambda i, p, nh=nh: l * 2 * nh + p[1] * nh + i,
                                 red.reshape(L * 2 * rh, C))))
    res = _multi_tiled("rs_add_chips", add, pos, groups, in_place=True)
    return [buf.reshape(red.shape) for buf, red in zip(res, reduced)]


def _adamw_math(w, g, m, v):
    m = ADAM_B1 * m + (1.0 - ADAM_B1) * g
    v = ADAM_B2 * v + (1.0 - ADAM_B2) * (g * g)
    m_hat = m / (1.0 - ADAM_B1 ** ADAM_STEP)
    v_hat = v / (1.0 - ADAM_B2 ** ADAM_STEP)
    return -ADAM_LR * (m_hat / (jnp.sqrt(v_hat) + ADAM_EPS) + ADAM_WD * w), m, v


def _adamw(w, g, m, v, lo=0, hi=None, into=None, deps=()):
    L, R, C = w.shape
    hi = L if hi is None else hi
    tr = _tile(R, max(16, ADAM_TILE_ELEMS // C))
    extra = (list(into) if into else []) + list(deps)
    n_alias = 4 if into else 0

    def body(w_ref, g_ref, m_ref, v_ref, *rest):
        go_ref, d_ref, mo_ref, vo_ref = rest[len(extra):]
        gv = g_ref[...]
        d, mn, vn = _adamw_math(w_ref[...], gv, m_ref[...], v_ref[...])
        go_ref[...] = gv
        d_ref[...] = d
        mo_ref[...] = mn
        vo_ref[...] = vn

    spec = pl.BlockSpec((None, tr, C), lambda l, i: (l + lo, i, 0))
    out = jax.ShapeDtypeStruct(w.shape, F32)
    return pl.pallas_call(body, grid=(hi - lo, R // tr), in_specs=[spec] * 4 + [ANY] * len(extra),
                          out_specs=[spec] * 4, out_shape=[out] * 4,
                          input_output_aliases={4 + k: k for k in range(n_alias)}, name="adamw",
                          compiler_params=_params(2))(w, g, m, v, *extra)


def _pack(parts):
    flat = jnp.concatenate([q.reshape(-1, LANES) for q in parts], axis=0)
    return jnp.pad(flat, ((0, -flat.shape[0] % ROW_TILE), (0, 0)))


def _unpack(flat, like):
    out, r = [], 0
    for q in like:
        n = q.size // LANES
        out.append(flat[r:r + n].reshape(q.shape))
        r += n
    return out


def _train_step(a):
    a = dict(a)
    L = a["ffn1_pre_g"].shape[0]
    x, y, c, _ = _place()
    chip = 2 * x + y
    pos = jnp.stack([chip, c, 2 * chip + c]).astype(jnp.int32)
    for name in TRANSPOSED:
        for pre in ("", "m_", "v_"):
            a[pre + name] = jnp.swapaxes(a[pre + name], 1, 2)
    big = [b[0] for b in BIG]
    gathered = big + ["conv_dw_k"]
    n_w, n_g = len(gathered), len(big)

    own = [None] * n_w
    W = {name: [None] * L for name in gathered}
    every = list(range(n_w))
    first, mixer, later = every[:3], every[3:8] + [n_g], every[8:n_g]
    rest = mixer + later

    def cast(i, deps):
        if i == n_g:
            taps = a["conv_dw_k"].reshape(L, CONV_TAPS, LANES)
            return _cast_layers("pad_conv_taps", taps, CONV_PAD, LANES, F32, pos, deps)
        name, _, _, _, rp, cp = BIG[i]
        return _cast_layers("cast_weight", a[name], rp, cp, BF, pos, deps)

    def gather_first(l, ids, tag, deps):
        return _split_start("gather_a%d%s" % (l, tag), _gather_plans(len(ids))[0], [own[i][l] for i in ids], deps)

    def gather_second(l, ids, tag, state, after):
        across, to_sibling = _gather_plans(len(ids))
        bufs = _split_wait("gather_a%d%s_done" % (l, tag), across, state[0], state[1], after)
        return _split_start("gather_b%d%s" % (l, tag), to_sibling, bufs, [])

    def gather_done(l, ids, tag, state, after):
        to_sibling = _gather_plans(len(ids))[1]
        bufs = _split_wait("gather_b%d%s_done" % (l, tag), to_sibling, state[0], state[1], after)
        for i, buf in zip(ids, bufs):
            W[gathered[i]][l] = buf

    for i in first:
        own[i] = cast(i, ())
    state = gather_first(0, first, "f", [])
    for i in rest:
        own[i] = cast(i, (state[2],))
    state = gather_second(0, first, "f", state, [own[i][0] for i in rest])
    gather_done(0, first, "f", state, [])

    parts = {"f": first, "m": mixer, "t": later}
    flying = {}

    def begin(l, part, deps):
        flying[l, part] = gather_first(l, parts[part], part, deps)
        return flying[l, part][2]

    def hand_on(l, part, after):
        flying[l, part] = gather_second(l, parts[part], part, flying[l, part], after)
        return flying[l, part][2]

    def arrive(l, part, after):
        gather_done(l, parts[part], part, flying.pop((l, part)), after)

    def hooks_of(l):
        nxt = l + 1 < L

        def after_ffn1(hv):
            tokens = []
            if l == 0:
                hand_on(0, "m", [hv])
            arrive(l, "m", [hv])
            if l == 0:
                tokens.append(begin(0, "t", [hv]))
            else:
                tokens.append(hand_on(l, "t", [hv]))
            if nxt:
                tokens.append(begin(l + 1, "f", tokens[-1:]))
            return tuple(tokens)

        def after_mix(hv):
            tokens = []
            if l == 0:
                hand_on(0, "t", [hv])
            arrive(l, "t", [hv])
            if nxt:
                tokens.append(hand_on(l + 1, "f", [hv]))
                tokens.append(begin(l + 1, "m", tokens[-1:]))
            return tuple(tokens)

        def after_ffn2(hv):
            tokens = []
            if nxt:
                arrive(l + 1, "f", [hv])
                tokens.append(hand_on(l + 1, "m", [hv]))
                tokens.append(begin(l + 1, "t", tokens[-1:]))
            return tuple(tokens)

        return {"ffn1": after_ffn1, "mix": after_mix, "ffn2": after_ffn2}

    small = [_layer_small(a, l) for l in range(L)]
    h, saved = a["x"][0], []
    deps = (begin(0, "m", []),)
    for l in range(L):
        h, sv = _layer_fwd(l, h, a["p"][l, 0], small[l], W, deps, hooks_of(l))
        saved.append(sv)
        deps = ()

    def loss_fn(yv, t):
        e = yv - t
        return e * (1.0 / D), jnp.sum(e * e, axis=0, keepdims=True)

    dh, lsum = _rowwise("loss", loss_fn, [(h, D, 0), (a["loss_target"][0], D, 0)], [], [(D, F32)], [(1, D)])
    loss = lax.psum(0.5 * jnp.sum(lsum) / D, ("x", "y", "c"))

    G = {name: [jax.ShapeDtypeStruct((N_CHIPS, rp, cp), BF)] * L for name, _, _, _, rp, cp in BIG}
    reduced = [lax.empty((L, rp, cp), F32) for _, _, _, _, rp, cp in BIG]
    small_grads = [None] * L
    whole = list(range(n_g))
    piece_a, piece_b, piece_c = whole[8:], whole[3:8], whole[:3]

    def pair_start(l, ids, tag, deps):
        grads = [G[big[i]][l] for i in ids]
        lands = [lax.empty((N_CHIPS, g.shape[1] // 2, g.shape[2]), BF) for g in grads]
        return _split_start("rs_pair%d%s" % (l, tag), _pair_plan(len(ids)), grads + lands, deps)

    def cross_start(l, ids, tag, state, after):
        n = len(ids)
        bufs = _split_wait("rs_pair%d%s_done" % (l, tag), _pair_plan(n), state[0], state[1], after)
        parts = _add_pair(bufs[:n], bufs[n:], pos)
        lands = [lax.empty((3,) + t.shape[1:], BF) for t in parts]
        return _split_start("rs_cross%d%s" % (l, tag), _cross_plan(n), parts + lands, [])

    def cross_finish(l, ids, tag, state, after, reduced):
        n = len(ids)
        bufs = _split_wait("rs_cross%d%s_done" % (l, tag), _cross_plan(n), state[0], state[1], after)
        reduced = list(reduced)
        for i, r in zip(ids, _add_chips(bufs[:n], bufs[n:], [reduced[i] for i in ids], l, pos)):
            reduced[i] = r
        return reduced

    def share_start(l, reduced):
        return _split_start("rs_share%d" % l, _share_plan(n_g, l), reduced, [])

    def share_done(l, state, after):
        return _split_wait("rs_share%d_done" % l, _share_plan(n_g, l), state[0], state[1], after)

    small_names = SMALL + ("conv_dw_k",)
    scatter, gather = _small_plans()
    totals = [None] * L

    def small_scatter(l, deps):
        packed = _pack([small_grads[l][name] for name in small_names])
        v3 = packed.reshape(8, packed.shape[0] // 8, LANES)
        return _split_start("small_scatter%d" % l, scatter, [v3, lax.empty((7,) + v3.shape[1:], F32)], deps)

    def small_gather(l, state, after):
        bufs = _split_wait("small_scatter%d_done" % l, scatter, state[0], state[1], after)
        return _split_start("small_gather%d" % l, gather, [_sum_small(bufs[0], bufs[1], pos)], [])

    def small_done(l, state, after):
        total = _split_wait("small_gather%d_done" % l, gather, state[0], state[1], after)[0]
        totals[l] = total.reshape(-1, LANES)

    st_pair = st_share = st_small = None
    for l in reversed(range(L)):
        deps = tuple(s[2] for s in (st_pair, st_share, st_small) if s is not None)
        box = {}

        def after_ffn2(dm, l=l, box=box, st_pair=st_pair, st_share=st_share, st_small=st_small):
            out = []
            if st_share is not None:
                box["reduced"] = share_done(l + 2, st_share, [dm])
            if st_small is not None:
                box["small"] = small_gather(l + 1, st_small, [dm])
                out.append(box["small"][2])
            if st_pair is not None:
                box["cross"] = cross_start(l + 1, whole, "", st_pair, [dm])
                out.append(box["cross"][2])
            if l == 0:
                box["pair_a"] = pair_start(0, piece_a, "a", [dm])
                out.append(box["pair_a"][2])
            return tuple(out)

        def after_mix(dm, box=box):
            box["cross_a"] = cross_start(0, piece_a, "a", box["pair_a"], [dm])
            box["pair_b"] = pair_start(0, piece_b, "b", [dm])
            return (box["cross_a"][2], box["pair_b"][2])

        hooks = {"ffn2": after_ffn2, "mix": after_mix} if l == 0 else {"ffn2": after_ffn2}
        dh, small_grads[l] = _layer_bwd(l, dh, saved[l], small[l], W, G, deps, hooks)
        if st_share is not None:
            reduced = box["reduced"]
        if "small" in box:
            small_done(l + 1, box["small"], [dh])
        st_share = None
        if "cross" in box:
            reduced = cross_finish(l + 1, whole, "", box["cross"], [dh], reduced)
            st_share = share_start(l + 1, reduced)
        st_small = small_scatter(l, [dh])
        st_pair = pair_start(l, whole, "", [st_small[2]]) if l else None
    grad_x = dh
    cross_b = cross_start(0, piece_b, "b", box["pair_b"], [st_small[2]])
    cross_c = cross_start(0, piece_c, "c", pair_start(0, piece_c, "c", [cross_b[2]]), [])
    if st_share is not None:
        reduced = share_done(1, st_share, [cross_c[2]])
    upper, token = {}, cross_c[2]
    for k, (name, red) in enumerate(zip(big, reduced)):
        if k == 2:
            st_small = small_gather(0, st_small, [token])
            token = st_small[2]
        if L > 1:
            upper[name] = _adamw(a[name], red, a["m_" + name], a["v_" + name], 1, L, deps=[token])
            token = upper[name][1]
    small_done(0, st_small, [token])
    per_layer = [_unpack(totals[l], [small_grads[l][name] for name in small_names]) for l in range(L)]
    summed = {name: jnp.stack([per_layer[l][k] for l in range(L)]) for k, name in enumerate(small_names)}
    done = [r[1] for r in upper.values()] + [summed[small_names[0]]]
    for ids, tag, state in ((piece_a, "a", box["cross_a"]), (piece_b, "b", cross_b), (piece_c, "c", cross_c)):
        reduced = cross_finish(0, ids, tag, state, done, reduced)
    reduced = share_done(0, share_start(0, reduced), [])
    big_grads = dict(zip(big, reduced))

    grads, deltas, new_m, new_v = {}, {}, {}, {}
    for name in big:
        res = _adamw(a[name], big_grads[name], a["m_" + name], a["v_" + name], 0, 1, upper.get(name))
        if name in TRANSPOSED:
            res = [jnp.swapaxes(r, 1, 2) for r in res]
        grads[name], deltas[name], new_m[name], new_v[name] = res
    taps = lax.dynamic_slice_in_dim(summed["conv_dw_k"], chip * LANES, LANES, axis=2)[:, :CONV_TAPS]
    grads["conv_dw_k"] = taps.reshape(a["conv_dw_k"].shape)
    for name in SMALL:
        grads[name] = summed[name].reshape(a[name].shape)
    shapes = [a[name] for name in small_names]
    res = _adamw(*[_pack([a[pre + name] if pre != "g" else grads[name] for name in small_names])[None]
                   for pre in ("", "g", "m_", "v_")])
    for dst, flat in zip((deltas, new_m, new_v), res[1:]):
        for name, val in zip(small_names, _unpack(flat[0], shapes)):
            dst[name] = val

    return (loss, grad_x[None], *[grads[n] for n in WEIGHTS], *[deltas[n] for n in WEIGHTS],
            *[new_m[n] for n in WEIGHTS], *[new_v[n] for n in WEIGHTS])


def kernel(x, p, ffn1_pre_g, ffn1_w_gate, ffn1_w_up, ffn1_w_down, ffn1_post_g, mix_pre_g, w_in, pool_w, pool_scale, w_pool_out, sgu_ln_g, sgu_ln_b, sgu_w_s, sgu_b_s, w_sgu_out, conv_dw_k, conv_dw_b, conv_ln_g, conv_ln_b, w_conv_out, w_out, mix_post_g, ffn2_pre_g, ffn2_w_gate, ffn2_w_up, ffn2_w_down, ffn2_post_g, ple_w_proj, ple_pre_g, ple_w_gate, ple_post_g, loss_target, m_ffn1_pre_g, m_ffn1_w_gate, m_ffn1_w_up, m_ffn1_w_down, m_ffn1_post_g, m_mix_pre_g, m_w_in, m_pool_w, m_pool_scale, m_w_pool_out, m_sgu_ln_g, m_sgu_ln_b, m_sgu_w_s, m_sgu_b_s, m_w_sgu_out, m_conv_dw_k, m_conv_dw_b, m_conv_ln_g, m_conv_ln_b, m_w_conv_out, m_w_out, m_mix_post_g, m_ffn2_pre_g, m_ffn2_w_gate, m_ffn2_w_up, m_ffn2_w_down, m_ffn2_post_g, m_ple_w_proj, m_ple_pre_g, m_ple_w_gate, m_ple_post_g, v_ffn1_pre_g, v_ffn1_w_gate, v_ffn1_w_up, v_ffn1_w_down, v_ffn1_post_g, v_mix_pre_g, v_w_in, v_pool_w, v_pool_scale, v_w_pool_out, v_sgu_ln_g, v_sgu_ln_b, v_sgu_w_s, v_sgu_b_s, v_w_sgu_out, v_conv_dw_k, v_conv_dw_b, v_conv_ln_g, v_conv_ln_b, v_w_conv_out, v_w_out, v_mix_post_g, v_ffn2_pre_g, v_ffn2_w_gate, v_ffn2_w_up, v_ffn2_w_down, v_ffn2_post_g, v_ple_w_proj, v_ple_pre_g, v_ple_w_gate, v_ple_post_g):
    return _train_step(dict(locals()))
```

```python
import math

import jax
import jax.numpy as jnp
from jax import lax
from jax.experimental import pallas as pl
from jax.experimental.pallas import tpu as pltpu

BF = jnp.bfloat16
F32 = jnp.float32
EPS = 1e-6
D_MODEL = 1024
LANES = 128
SUBLANES = 8
MXU_TILE = 256
N_CHIPS = 4
FFN_SHARD = 704
FFN_SHARD_PAD = 768
POOL_WINDOWS = (2, 4, 8, 16)
SGU_HEADS = 4
CHUNK = 128
CONV_TAPS = 31
CONV_PAD = 32
ROW_TILE = 512
EPI_ROWS = 256
VMEM_LIMIT_BYTES = 56 * 1024 * 1024
ADAM_TILE_ELEMS = 3 * 128 * 1024
ADAM_LR, ADAM_B1, ADAM_B2, ADAM_EPS, ADAM_WD, ADAM_STEP =0.001, 0.9, 0.999, 1e-08, 0.01, 10
MESH = pl.DeviceIdType.MESH
ANY = pl.BlockSpec(memory_space=pl.ANY)

BRANCH = 512
ZB_POOL, ZB_U, ZB_V, ZB_A, ZB_B, ZB_GATES = 0, 1, 2, 3, 4, 5
DZ_HALF = 2816

TRANSPOSED = ("ffn1_w_gate", "ffn1_w_up", "ffn2_w_gate", "ffn2_w_up")
BIG = (
    ("ffn1_w_gate", "row", FFN_SHARD, 1024, FFN_SHARD_PAD, 1024),
    ("ffn1_w_up", "row", FFN_SHARD, 1024, FFN_SHARD_PAD, 1024),
    ("ffn1_w_down", "row", FFN_SHARD, 1024, FFN_SHARD_PAD, 1024),
    ("w_in", "col", 1024, 1408, 1024, 1408),
    ("w_pool_out", "col", 512, 256, 512, 256),
    ("w_sgu_out", "col", 512, 256, 512, 256),
    ("w_conv_out", "col", 512, 256, 512, 256),
    ("w_out", "row", 256, 1024, 256, 1024),
    ("ffn2_w_gate", "row", FFN_SHARD, 1024, FFN_SHARD_PAD, 1024),
    ("ffn2_w_up", "row", FFN_SHARD, 1024, FFN_SHARD_PAD, 1024),
    ("ffn2_w_down", "row", FFN_SHARD, 1024, FFN_SHARD_PAD, 1024),
    ("ple_w_proj", "col", 256, 256, 256, 256),
    ("ple_w_gate", "row", 256, 1024, 256, 1024),
)
SMALL = ("ffn1_pre_g", "ffn1_post_g", "mix_pre_g", "pool_w", "pool_scale", "sgu_ln_g", "sgu_ln_b", "sgu_w_s",
         "sgu_b_s", "conv_dw_b", "conv_ln_g", "conv_ln_b", "mix_post_g", "ffn2_pre_g", "ffn2_post_g",
         "ple_pre_g", "ple_post_g")
WEIGHTS = ("ffn1_pre_g", "ffn1_w_gate", "ffn1_w_up", "ffn1_w_down", "ffn1_post_g", "mix_pre_g", "w_in", "pool_w",
           "pool_scale", "w_pool_out", "sgu_ln_g", "sgu_ln_b", "sgu_w_s", "sgu_b_s", "w_sgu_out", "conv_dw_k",
           "conv_dw_b", "conv_ln_g", "conv_ln_b", "w_conv_out", "w_out", "mix_post_g", "ffn2_pre_g", "ffn2_w_gate",
           "ffn2_w_up", "ffn2_w_down", "ffn2_post_g", "ple_w_proj", "ple_pre_g", "ple_w_gate", "ple_post_g")


def _params(n_grid):
    return pltpu.CompilerParams(dimension_semantics=("arbitrary",) * n_grid, vmem_limit_bytes=VMEM_LIMIT_BYTES)


def _tile(n, cap=ROW_TILE):
    for t in range(min(cap, n) - min(cap, n) % 16, 0, -16):
        if n % t == 0:
            return t
    return n


def _sigmoid(x):
    return 0.5 * jnp.tanh(0.5 * x) + 0.5


def _silu_and_grad(x):
    s = _sigmoid(x)
    return x * s, s * (1.0 + x * (1.0 - s))


def _gelu_and_grad(x):
    cdf = 0.5 * (1.0 + lax.erf(x * (1.0 / math.sqrt(2.0))))
    pdf = jnp.exp(-0.5 * x * x) * (1.0 / math.sqrt(2.0 * math.pi))
    return x * cdf, cdf + x * pdf


def _rms_fwd(x, g):
    return x * lax.rsqrt(jnp.mean(x * x, axis=-1, keepdims=True) + EPS) * g


def _rms_bwd(x, g, dy):
    r = lax.rsqrt(jnp.mean(x * x, axis=-1, keepdims=True) + EPS)
    xh = x * r
    dxh = dy * g
    dx = r * (dxh - xh * jnp.mean(dxh * xh, axis=-1, keepdims=True))
    return dx, jnp.sum(dy * xh, axis=0, keepdims=True)


def _ln_stats(x):
    xc = x - jnp.mean(x, axis=-1, keepdims=True)
    r = lax.rsqrt(jnp.mean(xc * xc, axis=-1, keepdims=True) + EPS)
    return xc * r, r


def _ln_bwd(xh, r, g, dy):
    dxh = dy * g
    dx = r * (dxh - jnp.mean(dxh, axis=-1, keepdims=True) - xh * jnp.mean(dxh * xh, axis=-1, keepdims=True))
    return dx, jnp.sum(dy * xh, axis=0, keepdims=True), jnp.sum(dy, axis=0, keepdims=True)


def _rowwise(name, fn, rows, consts, outs, accs=(), tm=ROW_TILE, deps=()):
    T = rows[0][0].shape[-2]
    tm = _tile(T, tm)
    n_in, n_o, n_dep = len(rows) + len(consts), len(outs), len(deps)

    def body(*refs):
        refs = refs[n_dep:]
        res = fn(*[r[...] for r in refs[:n_in]])
        for ref, val in zip(refs[n_in:n_in + n_o], res[:n_o]):
            ref[...] = val.astype(ref.dtype)
        acc_refs = refs[n_in + n_o:]
        if acc_refs:
            @pl.when(pl.program_id(0) == 0)
            def _():
                for ref, val in zip(acc_refs, res[n_o:]):
                    ref[...] = val

            @pl.when(pl.program_id(0) != 0)
            def _():
                for ref, val in zip(acc_refs, res[n_o:]):
                    ref[...] += val

    in_specs = [ANY] * n_dep
    for row in rows:
        w, cb = row[1], row[2]
        if len(row) == 4:
            in_specs.append(pl.BlockSpec((None, tm, w), lambda i, cb=cb, ld=row[3]: (ld, i, cb)))
        else:
            in_specs.append(pl.BlockSpec((tm, w), lambda i, cb=cb: (i, cb)))
    in_specs += [pl.BlockSpec(c.shape, lambda i: (0, 0)) for c in consts]
    out_specs = [pl.BlockSpec((tm, w), lambda i: (i, 0)) for w, _ in outs]
    out_specs += [pl.BlockSpec(s, lambda i: (0, 0)) for s in accs]
    out_shape = [jax.ShapeDtypeStruct((T, w), dt) for w, dt in outs]
    out_shape += [jax.ShapeDtypeStruct(s, F32) for s in accs]
    return pl.pallas_call(body, grid=(T // tm,), in_specs=in_specs, out_specs=out_specs, out_shape=out_shape,
                          name=name, compiler_params=_params(1))(*deps, *[r[0] for r in rows], *consts)


def _tiled(name, fn, grid, pos, ins, outs):
    n_in = len(ins)

    def body(_, *refs):
        res = fn(*[r[...] for r in refs[:n_in]])
        for ref, val in zip(refs[n_in:], res):
            ref[...] = val.astype(ref.dtype)

    spec = pltpu.PrefetchScalarGridSpec(
        num_scalar_prefetch=1, grid=grid, in_specs=[pl.BlockSpec(bs, im) for _, bs, im in ins],
        out_specs=[pl.BlockSpec(bs, im) for _, _, bs, im in outs])
    return pl.pallas_call(body, grid_spec=spec, out_shape=[jax.ShapeDtypeStruct(s, d) for s, d, _, _ in outs],
                          name=name, compiler_params=_params(len(grid)))(pos, *[a for a, _, _ in ins])


def _cast_layers(name, w, rp, cp, dtype, pos, deps=()):
    L, r, c = w.shape

    def body(_, w_ref, *rest):
        for k, o_ref in enumerate(rest[len(deps):]):
            @pl.when(pl.program_id(0) == k)
            def _(o_ref=o_ref):
                if (rp, cp) != (r, c):
                    o_ref[...] = jnp.zeros_like(o_ref)
                    o_ref[pl.ds(0, r), pl.ds(0, c)] = w_ref[...].astype(dtype)
                else:
                    o_ref[...] = w_ref[...].astype(dtype)

    spec = pltpu.PrefetchScalarGridSpec(
        num_scalar_prefetch=1, grid=(L,),
        in_specs=[pl.BlockSpec((None, r, c), lambda l, p: (l, 0, 0))] + [ANY] * len(deps),
        out_specs=[pl.BlockSpec((None, rp, cp), lambda l, p: (p[0], 0, 0))] * L)
    return pl.pallas_call(body, grid_spec=spec, out_shape=[jax.ShapeDtypeStruct((N_CHIPS, rp, cp), dtype)] * L,
                          name=name, compiler_params=_params(1))(pos, w, *deps)


_NN = (((1,), (0,)), ((), ()))
_NT = (((1,), (1,)), ((), ()))
_TN = (((0,), (0,)), ((), ()))


def _mm_tn(name, a, dy, buf, l, a_blocked, tk=ROW_TILE, first=0):
    T = a.shape[0]
    nb, R, C = buf[l].shape
    extra = [buf[l]] if first else []

    def body(a_ref, dy_ref, *rest):
        rest[-1][...] = lax.dot_general(a_ref[...].astype(BF), dy_ref[...].astype(BF), _TN,
                                        preferred_element_type=F32).astype(BF)

    if a_blocked:
        grid = (nb,)
        in_specs = [pl.BlockSpec((T, R), lambda b: (0, b)), pl.BlockSpec((T, C), lambda b: (0, 0))]
        out_specs = pl.BlockSpec((None, R, C), lambda b: (b, 0, 0))
    else:
        tk = min(tk, R)
        grid = (dy.shape[1] // C, R // tk)
        in_specs = [pl.BlockSpec((T, tk), lambda b, k: (0, k)), pl.BlockSpec((T, C), lambda b, k: (0, b))]
        out_specs = pl.BlockSpec((None, tk, C), lambda b, k: (b + first, k, 0))
    buf = list(buf)
    buf[l] = pl.pallas_call(body, grid=grid, in_specs=in_specs + [ANY] * len(extra), out_specs=out_specs,
                            out_shape=jax.ShapeDtypeStruct((nb, R, C), BF),
                            input_output_aliases={2: 0} if extra else {}, name=name,
                            compiler_params=_params(len(grid)))(a, dy, *extra)
    return buf


def _acc_rows(ref, val, first):
    @pl.when(first)
    def _():
        ref[...] = val

    @pl.when(jnp.logical_not(first))
    def _():
        ref[...] += val


def _norm_mm(name, h, g, ws, trans_w, act=False, deps=(), tm=2 * ROW_TILE):
    T = h.shape[0]
    nb, r, cc = ws[0].shape
    bo = r if trans_w else cc
    tm = _tile(T, tm)
    n_w, n_dep = len(ws), len(deps)

    def body(*refs):
        refs = refs[n_dep:]
        h_ref, g_ref, w_refs = refs[0], refs[1], refs[2:2 + n_w]
        n_ref, o_refs, n_s = refs[2 + n_w], refs[3 + n_w:3 + 2 * n_w], refs[-1]

        @pl.when(pl.program_id(1) == 0)
        def _():
            n = _rms_fwd(h_ref[...].astype(F32), g_ref[...]).astype(BF)
            n_s[...] = n
            n_ref[...] = n

        n = n_s[...]
        prods = []
        for w_ref, o_ref in zip(w_refs, o_refs):
            prods.append(lax.dot_general(n, w_ref[...], _NT if trans_w else _NN,
                                         preferred_element_type=F32).astype(BF))
            o_ref[...] = prods[-1]
        if act:
            refs[3 + 2 * n_w][...] = (_silu_and_grad(prods[0].astype(F32))[0] * prods[1].astype(F32)).astype(BF)

    wide = pl.BlockSpec((tm, bo), lambda i, b: (i, b))
    n_out = n_w + (1 if act else 0)
    return pl.pallas_call(
        body, grid=(T // tm, nb),
        in_specs=[ANY] * n_dep + [pl.BlockSpec((tm, D_MODEL), lambda i, b: (i, 0)),
                                  pl.BlockSpec(g.shape, lambda i, b: (0, 0))]
        + [pl.BlockSpec((None, r, cc), lambda i, b: (b, 0, 0))] * n_w,
        out_specs=[pl.BlockSpec((tm, D_MODEL), lambda i, b: (i, 0))] + [wide] * n_out,
        out_shape=[jax.ShapeDtypeStruct((T, D_MODEL), BF)] + [jax.ShapeDtypeStruct((T, nb * bo), BF)] * n_out,
        scratch_shapes=[pltpu.VMEM((tm, D_MODEL), BF)], name=name, compiler_params=_params(2))(*deps, h, g, *ws)


def _mm_res(name, x, w3, h, g, coef, tm=ROW_TILE):
    T, kx = x.shape
    w2 = w3.reshape(kx, D_MODEL)
    tm = _tile(T, tm)

    def body(x_ref, w_ref, h_ref, g_ref, f_ref, o_ref):
        f = jnp.dot(x_ref[...], w_ref[...], preferred_element_type=F32).astype(BF)
        f_ref[...] = f
        o_ref[...] = h_ref[...] + coef * _rms_fwd(f.astype(F32), g_ref[...])

    row = pl.BlockSpec((tm, D_MODEL), lambda i: (i, 0))
    return pl.pallas_call(
        body, grid=(T // tm,),
        in_specs=[pl.BlockSpec((tm, kx), lambda i: (i, 0)), pl.BlockSpec(w2.shape, lambda i: (0, 0)), row,
                  pl.BlockSpec(g.shape, lambda i: (0, 0))],
        out_specs=[row, row],
        out_shape=[jax.ShapeDtypeStruct((T, D_MODEL), BF), jax.ShapeDtypeStruct((T, D_MODEL), F32)],
        name=name, compiler_params=_params(1))(x, w2, h, g)


def _resbwd_mm(name, dh, f, g, coef, w3, trans_w, act=None, deps=(), tm=2 * ROW_TILE):
    T = dh.shape[0]
    nb, r, cc = w3.shape
    bo = r if trans_w else cc
    tm = _tile(T, tm)
    n_dep, n_act = len(deps), 2 if act else 0

    def body(*refs):
        refs = refs[n_dep:]
        dh_ref, f_ref, g_ref, w_ref = refs[:4]
        df_ref, dg_ref = refs[4 + n_act], refs[5 + n_act]
        df_s = refs[-1]
        i, b = pl.program_id(0), pl.program_id(1)

        @pl.when(b == 0)
        def _():
            dg = jnp.zeros((1, D_MODEL), F32)
            for c in range(tm // EPI_ROWS):
                rows = slice(c * EPI_ROWS, (c + 1) * EPI_ROWS)
                dx, dg_c = _rms_bwd(f_ref[rows, :].astype(F32), g_ref[...], coef * dh_ref[rows, :])
                df_s[rows, :] = dx.astype(BF)
                df_ref[rows, :] = dx.astype(BF)
                dg = dg + dg_c
            _acc_rows(dg_ref, dg, i == 0)

        if act:
            for j in range(bo // MXU_TILE):
                cols = slice(j * MXU_TILE, (j + 1) * MXU_TILE)
                prod = lax.dot_general(df_s[...], w_ref[cols, :], _NT, preferred_element_type=F32)
                val, grad = _silu_and_grad(refs[4][:, cols])
                prod = prod.astype(BF)
                refs[6 + n_act][:, cols] = prod * refs[5][:, cols] * grad
                refs[7 + n_act][:, cols] = prod * val
        else:
            refs[6][...] = lax.dot_general(df_s[...], w_ref[...], _NT if trans_w else _NN,
                                           preferred_element_type=F32).astype(BF)

    row = pl.BlockSpec((tm, D_MODEL), lambda i, b: (i, 0))
    wide = pl.BlockSpec((tm, bo), lambda i, b: (i, b))
    vec = pl.BlockSpec((1, D_MODEL), lambda i, b: (0, 0))
    n_prod = 2 if act else 1
    return pl.pallas_call(
        body, grid=(T // tm, nb),
        in_specs=[ANY] * n_dep + [row, row, vec, pl.BlockSpec((None, r, cc), lambda i, b: (b, 0, 0))] + [wide] * n_act,
        out_specs=[row, vec] + [wide] * n_prod,
        out_shape=[jax.ShapeDtypeStruct((T, D_MODEL), BF), jax.ShapeDtypeStruct((1, D_MODEL), F32)]
        + [jax.ShapeDtypeStruct((T, nb * bo), BF)] * n_prod,
        scratch_shapes=[pltpu.VMEM((tm, D_MODEL), BF)], name=name,
        compiler_params=_params(2))(*deps, dh, f, g, w3, *(act or ()))


def _dn_prenorm(name, xs, ws, trans_w, dh, h, g, tm=2 * ROW_TILE):
    T = dh.shape[0]
    chained = not isinstance(ws, (list, tuple))
    ws = [ws] if chained else list(ws)
    _, r, cc = ws[0].shape
    bw = cc if trans_w else r
    per_x = xs[0].shape[1] // bw
    nb = per_x * len(xs) if chained else per_x
    tm = _tile(T, tm)
    n_x, n_w = len(xs), len(ws)

    def body(*refs):
        x_refs, w_refs = refs[:n_x], refs[n_x:n_x + n_w]
        dh_ref, h_ref, g_ref, o_ref, dg_ref, acc = refs[n_x + n_w:]
        i, b = pl.program_id(0), pl.program_id(1)

        @pl.when(b == 0)
        def _():
            acc[...] = jnp.zeros_like(acc)

        def add(x_ref, w_ref):
            acc[...] += lax.dot_general(x_ref[...], w_ref[...], _NT if trans_w else _NN, preferred_element_type=F32)

        if chained:
            for k, x_ref in enumerate(x_refs):
                pl.when(b // per_x == k)(lambda x_ref=x_ref: add(x_ref, w_refs[0]))
        else:
            for x_ref, w_ref in zip(x_refs, w_refs):
                add(x_ref, w_ref)

        @pl.when(b == nb - 1)
        def _():
            dg = jnp.zeros((1, D_MODEL), F32)
            for c in range(tm // EPI_ROWS):
                rows = slice(c * EPI_ROWS, (c + 1) * EPI_ROWS)
                dx, dg_c = _rms_bwd(h_ref[rows, :], g_ref[...], acc[rows, :])
                o_ref[rows, :] = dh_ref[rows, :] + dx
                dg = dg + dg_c
            _acc_rows(dg_ref, dg, i == 0)

    row = pl.BlockSpec((tm, D_MODEL), lambda i, b: (i, 0))
    vec = pl.BlockSpec((1, D_MODEL), lambda i, b: (0, 0))
    if chained:
        x_specs = [pl.BlockSpec((tm, bw), lambda i, b, k=k: (i, jnp.clip(b - k * per_x, 0, per_x - 1)))
                   for k in range(n_x)]
    else:
        x_specs = [pl.BlockSpec((tm, bw), lambda i, b: (i, b))] * n_x
    return pl.pallas_call(
        body, grid=(T // tm, nb),
        in_specs=x_specs + [pl.BlockSpec((None, r, cc), lambda i, b: (b, 0, 0))] * n_w + [row, row, vec],
        out_specs=[row, vec],
        out_shape=[jax.ShapeDtypeStruct((T, D_MODEL), F32), jax.ShapeDtypeStruct((1, D_MODEL), F32)],
        scratch_shapes=[pltpu.VMEM((tm, D_MODEL), F32)], name=name,
        compiler_params=_params(2))(*xs, *ws, dh, h, g)


def _pool_apply(x, win, row):
    s, k = x, 1
    while k < win:
        s = s + jnp.where(row >= k, pltpu.roll(s, k, 0), 0.0)
        k *= 2
    return s / jnp.minimum(row + 1, win).astype(F32) - x


def _pool_apply_t(dp, win, row):
    T = dp.shape[0]
    s, k = dp / jnp.minimum(row + 1, win).astype(F32), 1
    while k < win:
        s = s + jnp.where(row < T - k, pltpu.roll(s, T - k, 0), 0.0)
        k *= 2
    return s - dp


def _pool_fwd(z, w, scale):
    T = z.shape[0]

    def body(z_ref, w_ref, s_ref, o_ref):
        row = lax.broadcasted_iota(jnp.int32, (T, LANES), 0)
        for gi, win in enumerate(POOL_WINDOWS):
            cols = pl.ds(gi * LANES, LANES)
            pooled = _pool_apply(z_ref[:, cols].astype(F32), win, row)
            y = jnp.dot(pooled.astype(BF), w_ref[gi].astype(BF), preferred_element_type=F32)
            o_ref[:, cols] = (y * s_ref[:, cols]).astype(o_ref.dtype)

    return pl.pallas_call(
        body, grid=(1,),
        in_specs=[pl.BlockSpec((T, BRANCH), lambda i: (0, ZB_POOL)), pl.BlockSpec(w.shape, lambda i: (0, 0, 0)),
                  pl.BlockSpec(scale.shape, lambda i: (0, 0))],
        out_specs=pl.BlockSpec((T, BRANCH), lambda i: (0, 0)), out_shape=jax.ShapeDtypeStruct((T, BRANCH), BF),
        name="pool_fwd", compiler_params=_params(1))(z, w, scale)


def _pool_bwd(dr, z, w, scale):
    T = z.shape[0]

    def body(dr_ref, z_ref, w_ref, s_ref, dz_ref, dw_ref, ds_ref):
        row = lax.broadcasted_iota(jnp.int32, (T, LANES), 0)
        for gi, win in enumerate(POOL_WINDOWS):
            cols = pl.ds(gi * LANES, LANES)
            pooled = _pool_apply(z_ref[:, cols].astype(F32), win, row).astype(BF)
            wg = w_ref[gi].astype(BF)
            y = jnp.dot(pooled, wg, preferred_element_type=F32)
            d = dr_ref[:, cols].astype(F32)
            ds_ref[:, cols] = jnp.sum(d * y, axis=0, keepdims=True)
            dy = (d * s_ref[:, cols]).astype(BF)
            dw_ref[gi] = lax.dot_general(pooled, dy, _TN, preferred_element_type=F32)
            dpooled = lax.dot_general(dy, wg, _NT, preferred_element_type=F32)
            dz_ref[:, cols] = _pool_apply_t(dpooled, win, row).astype(dz_ref.dtype)

    return pl.pallas_call(
        body, grid=(1,),
        in_specs=[pl.BlockSpec((T, BRANCH), lambda i: (0, 0)), pl.BlockSpec((T, BRANCH), lambda i: (0, ZB_POOL)),
                  pl.BlockSpec(w.shape, lambda i: (0, 0, 0)), pl.BlockSpec(scale.shape, lambda i: (0, 0))],
        out_specs=[pl.BlockSpec((T, BRANCH), lambda i: (0, 0)), pl.BlockSpec(w.shape, lambda i: (0, 0, 0)),
                   pl.BlockSpec(scale.shape, lambda i: (0, 0))],
        out_shape=[jax.ShapeDtypeStruct((T, BRANCH), BF), jax.ShapeDtypeStruct(w.shape, F32),
                   jax.ShapeDtypeStruct(scale.shape, F32)],
        name="pool_bwd", compiler_params=_params(1))(dr, z, w, scale)


def _tril(transposed=False):
    r = lax.broadcasted_iota(jnp.int32, (CHUNK, CHUNK), 0)
    c = lax.broadcasted_iota(jnp.int32, (CHUNK, CHUNK), 1)
    return c >= r if transposed else r >= c


def _sgu_fwd(z, ln_g, ln_b, w_s, bias):
    T = z.shape[0]
    tm = _tile(T)

    def body(zu_ref, zv_ref, g_ref, b_ref, w_ref, bias_ref, o_ref):
        gu, _ = _gelu_and_grad(zu_ref[...].astype(F32))
        gv, _ = _gelu_and_grad(zv_ref[...].astype(F32))
        xh, _ = _ln_stats(gv)
        v16 = (xh * g_ref[...] + b_ref[...]).astype(BF)
        tri = _tril()
        for h in range(SGU_HEADS):
            cols = slice(h * LANES, (h + 1) * LANES)
            wh = jnp.where(tri, w_ref[h], 0.0).astype(BF)
            for c in range(tm // CHUNK):
                rows = slice(c * CHUNK, (c + 1) * CHUNK)
                s = jnp.dot(wh, v16[rows, cols], preferred_element_type=F32) + bias_ref[:, cols]
                o_ref[rows, cols] = (gu[rows, cols] * s).astype(o_ref.dtype)

    small = [pl.BlockSpec(a.shape, lambda i, n=a.ndim: (0,) * n) for a in (ln_g, ln_b, w_s, bias)]
    return pl.pallas_call(
        body, grid=(T // tm,),
        in_specs=[pl.BlockSpec((tm, BRANCH), lambda i: (i, ZB_U)), pl.BlockSpec((tm, BRANCH), lambda i: (i, ZB_V))] + small,
        out_specs=pl.BlockSpec((tm, BRANCH), lambda i: (i, 0)), out_shape=jax.ShapeDtypeStruct((T, BRANCH), BF),
        name="sgu_fwd", compiler_params=_params(1))(z, z, ln_g, ln_b, w_s, bias)


def _sgu_bwd(dr, z, ln_g, ln_b, w_s, w_st, bias):
    T = z.shape[0]
    tm = _tile(T)
    n_steps = T // tm

    def body(dr_ref, zu_ref, zv_ref, g_ref, b_ref, w_ref, wt_ref, bias_ref,
             dzu_ref, dzv_ref, dg_ref, db_ref, dw_ref, dbias_ref, dgu_s, dv_s):
        i = pl.program_id(0)

        @pl.when(i == 0)
        def _():
            dg_ref[...] = jnp.zeros_like(dg_ref)
            db_ref[...] = jnp.zeros_like(db_ref)
            dw_ref[...] = jnp.zeros_like(dw_ref)
            dbias_ref[...] = jnp.zeros_like(dbias_ref)

        zu = zu_ref[...].astype(F32)
        zv = zv_ref[...].astype(F32)
        gu, gu_grad = _gelu_and_grad(zu)
        gv, gv_grad = _gelu_and_grad(zv)
        xh, r = _ln_stats(gv)
        v16 = (xh * g_ref[...] + b_ref[...]).astype(BF)
        dr = dr_ref[...].astype(F32)
        tri = _tril()
        for h in range(SGU_HEADS):
            cols = slice(h * LANES, (h + 1) * LANES)
            wh = jnp.where(tri, w_ref[h], 0.0).astype(BF)
            wht = jnp.where(_tril(transposed=True), wt_ref[h], 0.0).astype(BF)
            for c in range(tm // CHUNK):
                rows = slice(c * CHUNK, (c + 1) * CHUNK)
                v_blk = v16[rows, cols]
                s = jnp.dot(wh, v_blk, preferred_element_type=F32) + bias_ref[:, cols]
                ds = dr[rows, cols] * gu[rows, cols]
                dgu_s[rows, cols] = dr[rows, cols] * s
                ds16 = ds.astype(BF)
                dw_ref[h] += jnp.where(tri, lax.dot_general(ds16, v_blk, _NT, preferred_element_type=F32), 0.0)
                dv_s[rows, cols] = jnp.dot(wht, ds16, preferred_element_type=F32)
                dbias_ref[:, cols] += ds
        dzu_ref[...] = (dgu_s[...] * gu_grad).astype(dzu_ref.dtype)
        dgv, dg, db = _ln_bwd(xh, r, g_ref[...], dv_s[...])
        dzv_ref[...] = (dgv * gv_grad).astype(dzv_ref.dtype)
        dg_ref[...] += dg
        db_ref[...] += db

        @pl.when(i == n_steps - 1)
        def _():
            for h in range(SGU_HEADS):
                cols = slice(h * LANES, (h + 1) * LANES)
                tot = jnp.sum(dbias_ref[:, cols], axis=1, keepdims=True)
                dbias_ref[:, cols] = jnp.broadcast_to(tot, (CHUNK, LANES))

    small = (ln_g, ln_b, w_s, w_st, bias)
    small_specs = [pl.BlockSpec(a.shape, lambda i, n=a.ndim: (0,) * n) for a in small]
    return pl.pallas_call(
        body, grid=(n_steps,),
        in_specs=[pl.BlockSpec((tm, BRANCH), lambda i: (i, 0)), pl.BlockSpec((tm, BRANCH), lambda i: (i, ZB_U)),
                  pl.BlockSpec((tm, BRANCH), lambda i: (i, ZB_V))] + small_specs,
        out_specs=[pl.BlockSpec((tm, BRANCH), lambda i: (i, 0)), pl.BlockSpec((tm, BRANCH), lambda i: (i, 0)),
                   pl.BlockSpec((1, BRANCH), lambda i: (0, 0)), pl.BlockSpec((1, BRANCH), lambda i: (0, 0)),
                   pl.BlockSpec(w_s.shape, lambda i: (0, 0, 0)), pl.BlockSpec(bias.shape, lambda i: (0, 0))],
        out_shape=[jax.ShapeDtypeStruct((T, BRANCH), BF), jax.ShapeDtypeStruct((T, BRANCH), BF),
                   jax.ShapeDtypeStruct((1, BRANCH), F32), jax.ShapeDtypeStruct((1, BRANCH), F32),
                   jax.ShapeDtypeStruct(w_s.shape, F32), jax.ShapeDtypeStruct(bias.shape, F32)],
        scratch_shapes=[pltpu.VMEM((tm, BRANCH), F32), pltpu.VMEM((tm, BRANCH), F32)],
        name="sgu_bwd", compiler_params=_params(1))(dr, z, z, ln_g, ln_b, w_s, w_st, bias)


def _conv_fwd(z, convk, l, bias):
    T = z.shape[0]

    def body(za_ref, zb_ref, k_ref, b_ref, o_ref):
        xg = za_ref[...].astype(F32) * _sigmoid(zb_ref[...].astype(F32))
        xp = jnp.concatenate([jnp.zeros((CONV_PAD, LANES), F32), xg], axis=0)
        kw = k_ref[...]
        acc = jnp.broadcast_to(b_ref[...], (T, LANES))
        for s in range(SUBLANES):
            xs = xp if s == 0 else pltpu.roll(xp, s, 0)
            for q in range(CONV_PAD // SUBLANES):
                k = CONV_TAPS - 1 - (SUBLANES * q + s)
                if k >= 0:
                    lo = CONV_PAD - SUBLANES * q
                    acc = acc + kw[k:k + 1, :] * xs[lo:lo + T, :]
        o_ref[...] = acc.astype(o_ref.dtype)

    return pl.pallas_call(
        body, grid=(4,),
        in_specs=[pl.BlockSpec((T, LANES), lambda g: (0, 4 * ZB_A + g)),
                  pl.BlockSpec((T, LANES), lambda g: (0, 4 * ZB_B + g)),
                  pl.BlockSpec((None, CONV_PAD, LANES), lambda g: (g, 0, 0)),
                  pl.BlockSpec((1, LANES), lambda g: (0, g))],
        out_specs=pl.BlockSpec((T, LANES), lambda g: (0, g)), out_shape=jax.ShapeDtypeStruct((T, BRANCH), BF),
        name="conv_fwd", compiler_params=_params(1))(z, z, convk[l], bias)


def _conv_bwd(dy, z, convk, l):
    T = z.shape[0]

    def body(dy_ref, za_ref, zb_ref, k_ref, dza_ref, dzb_ref, dk_ref, db_ref):
        a = za_ref[...].astype(F32)
        sg = _sigmoid(zb_ref[...].astype(F32))
        d = dy_ref[...].astype(F32)
        kw = k_ref[...]
        xp = jnp.concatenate([jnp.zeros((CONV_PAD, LANES), F32), a * sg], axis=0)
        dp = jnp.concatenate([d, jnp.zeros((CONV_PAD, LANES), F32)], axis=0)
        dxg = jnp.zeros((T, LANES), F32)
        dk_ref[...] = jnp.zeros_like(dk_ref)
        for s in range(SUBLANES):
            xs = xp if s == 0 else pltpu.roll(xp, s, 0)
            ds = dp if s == 0 else pltpu.roll(dp, T + CONV_PAD - s, 0)
            for q in range(CONV_PAD // SUBLANES):
                k = CONV_TAPS - 1 - (SUBLANES * q + s)
                if k >= 0:
                    lo = CONV_PAD - SUBLANES * q
                    dk_ref[k:k + 1, :] = jnp.sum(d * xs[lo:lo + T, :], axis=0, keepdims=True)
                    dxg = dxg + kw[k:k + 1, :] * ds[SUBLANES * q:SUBLANES * q + T, :]
        db_ref[...] = jnp.sum(d, axis=0, keepdims=True)
        dza_ref[...] = (dxg * sg).astype(dza_ref.dtype)
        dzb_ref[...] = (dxg * a * sg * (1.0 - sg)).astype(dzb_ref.dtype)

    col = pl.BlockSpec((T, LANES), lambda g: (0, g))
    return pl.pallas_call(
        body, grid=(4,),
        in_specs=[col, pl.BlockSpec((T, LANES), lambda g: (0, 4 * ZB_A + g)),
                  pl.BlockSpec((T, LANES), lambda g: (0, 4 * ZB_B + g)),
                  pl.BlockSpec((None, CONV_PAD, LANES), lambda g: (g, 0, 0))],
        out_specs=[col, col, pl.BlockSpec((CONV_PAD, LANES), lambda g: (0, g)),
                   pl.BlockSpec((1, LANES), lambda g: (0, g))],
        out_shape=[jax.ShapeDtypeStruct((T, BRANCH), BF), jax.ShapeDtypeStruct((T, BRANCH), BF),
                   jax.ShapeDtypeStruct((CONV_PAD, BRANCH), F32), jax.ShapeDtypeStruct((1, BRANCH), F32)],
        name="conv_bwd", compiler_params=_params(1))(dy, z, z, convk[l])


D = D_MODEL


def _ffn_fwd(l, h, S, W, pre, deps=()):
    n, gp, u, a = _norm_mm("ffn_in", h, S[pre + "_pre_g"], [W[pre + "_w_gate"][l], W[pre + "_w_up"][l]], True,
                           act=True, deps=deps)
    f, out = _mm_res("ffn_out", a, W[pre + "_w_down"][l], h, S[pre + "_post_g"], 0.5)
    return out, dict(h=h, n=n, gp=gp, u=u, a=a, f=f)


def _ffn_bwd(l, dh, sv, S, W, G, SG, pre, deps=()):
    df, SG[pre + "_post_g"], dgp, du = _resbwd_mm("ffn_bwd_act", dh, sv["f"], S[pre + "_post_g"], 0.5,
                                                  W[pre + "_w_down"][l], True, act=(sv["gp"], sv["u"]), deps=deps)
    G[pre + "_w_down"] = _mm_tn("ffn_dw_down", sv["a"], df, G[pre + "_w_down"], l, True)
    G[pre + "_w_gate"] = _mm_tn("ffn_dw_gate", dgp, sv["n"], G[pre + "_w_gate"], l, True)
    G[pre + "_w_up"] = _mm_tn("ffn_dw_up", du, sv["n"], G[pre + "_w_up"], l, True)
    dh_in, SG[pre + "_pre_g"] = _dn_prenorm("ffn_bwd_in", [dgp, du], [W[pre + "_w_gate"][l], W[pre + "_w_up"][l]],
                                            False, dh, sv["h"], S[pre + "_pre_g"])
    return dh_in


def _gates(zg):
    return [_sigmoid(jnp.concatenate([zg[2 * k].astype(F32), zg[2 * k + 1].astype(F32)], axis=1)) for k in range(3)]


def _merge_fwd(z, rs, ws, tm=ROW_TILE):
    T = z.shape[0]
    tm = _tile(T, tm)
    nb, kk, bw = ws[0].shape

    def body(*refs):
        r_refs, g_refs, w_refs, y_refs, m_ref = refs[:3], refs[3:9], refs[9:12], refs[12:15], refs[15]
        for r_ref, w_ref, y_ref in zip(r_refs, w_refs, y_refs):
            for b in range(nb):
                y_ref[:, b * bw:(b + 1) * bw] = jnp.dot(r_ref[...], w_ref[b],
                                                        preferred_element_type=F32).astype(y_ref.dtype)
        g = _gates([q[...] for q in g_refs])
        m_ref[...] = (g[0] * y_refs[0][...].astype(F32) + g[1] * y_refs[1][...].astype(F32)
                      + g[2] * y_refs[2][...].astype(F32)).astype(m_ref.dtype)

    row = pl.BlockSpec((tm, D_MODEL), lambda i: (i, 0))
    return pl.pallas_call(
        body, grid=(T // tm,),
        in_specs=[pl.BlockSpec((tm, kk), lambda i: (i, 0))] * 3
        + [pl.BlockSpec((tm, BRANCH), lambda i, j=j: (i, ZB_GATES + j)) for j in range(6)]
        + [pl.BlockSpec(ws[0].shape, lambda i: (0, 0, 0))] * 3,
        out_specs=[row] * 4, out_shape=[jax.ShapeDtypeStruct((T, D_MODEL), BF)] * 4,
        name="mix_merge", compiler_params=_params(1))(*rs, *[z] * 6, *ws)


def _merge_bwd(dmerged, z, ys, ws, tm=ROW_TILE // 2):
    T = z.shape[0]
    tm = _tile(T, tm)
    nb, kk, bw = ws[0].shape

    def body(*refs):
        dm_ref, g_refs, y_refs, w_refs = refs[0], refs[1:7], refs[7:10], refs[10:13]
        dy_refs, lo_ref, hi_ref, dr_refs = refs[13:16], refs[16], refs[17], refs[18:21]
        cut = DZ_HALF - ZB_GATES * BRANCH
        dm = dm_ref[...].astype(F32)
        g = _gates([q[...] for q in g_refs])
        for k in range(3):
            dy_refs[k][...] = (dm * g[k]).astype(BF)
            dzg = (dm * y_refs[k][...].astype(F32) * g[k] * (1.0 - g[k])).astype(BF)
            if k == 0:
                lo_ref[...] = dzg[:, :cut]
                hi_ref[:, :D_MODEL - cut] = dzg[:, cut:]
            else:
                hi_ref[:, k * D_MODEL - cut:(k + 1) * D_MODEL - cut] = dzg
            dr = None
            for b in range(nb):
                p = lax.dot_general(dy_refs[k][:, b * bw:(b + 1) * bw], w_refs[k][b], _NT,
                                    preferred_element_type=F32)
                dr = p if dr is None else dr + p
            dr_refs[k][...] = dr.astype(BF)

    row = pl.BlockSpec((tm, D_MODEL), lambda i: (i, 0))
    return pl.pallas_call(
        body, grid=(T // tm,),
        in_specs=[row] + [pl.BlockSpec((tm, BRANCH), lambda i, j=j: (i, ZB_GATES + j)) for j in range(6)] + [row] * 3
        + [pl.BlockSpec(ws[0].shape, lambda i: (0, 0, 0))] * 3,
        out_specs=[row] * 3 + [pl.BlockSpec((tm, DZ_HALF - ZB_GATES * BRANCH), lambda i: (i, 0)),
                               pl.BlockSpec((tm, DZ_HALF), lambda i: (i, 0))]
        + [pl.BlockSpec((tm, kk), lambda i: (i, 0))] * 3,
        out_shape=[jax.ShapeDtypeStruct((T, D_MODEL), BF)] * 3
        + [jax.ShapeDtypeStruct((T, DZ_HALF - ZB_GATES * BRANCH), BF), jax.ShapeDtypeStruct((T, DZ_HALF), BF)]
        + [jax.ShapeDtypeStruct((T, kk), BF)] * 3,
        name="mix_merge_bwd", compiler_params=_params(1))(dmerged, *[z] * 6, *ys, *ws)


def _mix_fwd(l, h, S, W, deps=()):
    n, z = _norm_mm("mix_in", h, S["mix_pre_g"], [W["w_in"][l]], False, deps=deps)
    r_pool = _pool_fwd(z, S["pool_w"], S["pool_scale"])
    r_sgu = _sgu_fwd(z, S["sgu_ln_g"], S["sgu_ln_b"], S["sgu_w_s"], S["sgu_bias"])
    yc = _conv_fwd(z, W["conv_dw_k"], l, S["conv_dw_b"])

    def ln_silu(y, g, b):
        xh, _ = _ln_stats(y.astype(F32))
        return (_silu_and_grad(xh * g + b)[0],)

    r_conv = _rowwise("conv_ln", ln_silu, [(yc, BRANCH, 0)], [S["conv_ln_g"], S["conv_ln_b"]], [(BRANCH, BF)])[0]
    y_pool, y_sgu, y_conv, merged = _merge_fwd(z, (r_pool, r_sgu, r_conv),
                                               [W["w_%s_out" % br][l] for br in ("pool", "sgu", "conv")])
    o, out = _mm_res("mix_out", merged, W["w_out"][l], h, S["mix_post_g"], 1.0)
    return out, dict(h=h, n=n, z=z, r_pool=r_pool, r_sgu=r_sgu, yc=yc, r_conv=r_conv, y_pool=y_pool, y_sgu=y_sgu,
                     y_conv=y_conv, merged=merged, o=o)


def _branch_dw(rs, dys, shape):
    T, kk = rs[0].shape
    nb, _, bw = shape

    def body(*refs):
        for k in range(3):
            refs[6 + k][...] = lax.dot_general(refs[k][...], refs[3 + k][...], _TN,
                                               preferred_element_type=F32).astype(BF)

    return pl.pallas_call(
        body, grid=(nb,),
        in_specs=[pl.BlockSpec((T, kk), lambda b: (0, 0))] * 3 + [pl.BlockSpec((T, bw), lambda b: (0, b))] * 3,
        out_specs=[pl.BlockSpec((None, kk, bw), lambda b: (b, 0, 0))] * 3,
        out_shape=[jax.ShapeDtypeStruct((nb, kk, bw), BF)] * 3, name="branch_dw",
        compiler_params=_params(1))(*rs, *dys)


def _mix_bwd(l, dh, sv, S, W, G, SG, deps=()):
    z = sv["z"]
    do, SG["mix_post_g"], dmerged = _resbwd_mm("mix_bwd_out", dh, sv["o"], S["mix_post_g"], 1.0,
                                               W["w_out"][l].reshape(1, D, D), True, deps=deps)
    G["w_out"] = _mm_tn("mix_dw_out", sv["merged"], do, G["w_out"], l, True)

    branches = ("pool", "sgu", "conv")
    res = _merge_bwd(dmerged, z, [sv["y_" + br] for br in branches], [W["w_%s_out" % br][l] for br in branches])
    dz_gate_lo, dz_hi, dr = res[3], res[4], dict(zip(branches, res[5:]))
    for br, dw in zip(branches, _branch_dw([sv["r_" + br] for br in branches], res[:3], G["w_pool_out"][l].shape)):
        wn = "w_%s_out" % br
        G[wn] = G[wn][:l] + [dw] + G[wn][l + 1:]
    dz_pool, SG["pool_w"], SG["pool_scale"] = _pool_bwd(dr["pool"], z, S["pool_w"], S["pool_scale"])
    dzu, dzv, SG["sgu_ln_g"], SG["sgu_ln_b"], SG["sgu_w_s"], dbias = _sgu_bwd(
        dr["sgu"], z, S["sgu_ln_g"], S["sgu_ln_b"], S["sgu_w_s"], S["sgu_w_st"], S["sgu_bias"])
    SG["sgu_b_s"] = dbias[:, ::LANES].T

    def ln_silu_bwd(d, y, g, b):
        xh, r = _ln_stats(y.astype(F32))
        _, grad = _silu_and_grad(xh * g + b)
        return _ln_bwd(xh, r, g, d.astype(F32) * grad)

    dyc, SG["conv_ln_g"], SG["conv_ln_b"] = _rowwise(
        "conv_ln_bwd", ln_silu_bwd, [(dr["conv"], BRANCH, 0), (sv["yc"], BRANCH, 0)],
        [S["conv_ln_g"], S["conv_ln_b"]], [(BRANCH, BF)], [(1, BRANCH), (1, BRANCH)])
    dza, dzb, SG["conv_dw_k"], SG["conv_dw_b"] = _conv_bwd(dyc, z, W["conv_dw_k"], l)
    dz_lo = jnp.concatenate([dz_pool, dzu, dzv, dza, dzb, dz_gate_lo], axis=1)
    G["w_in"] = _mm_tn("mix_dw_in", sv["n"], dz_lo, G["w_in"], l, False)
    G["w_in"] = _mm_tn("mix_dw_in", sv["n"], dz_hi, G["w_in"], l, False, first=2)
    dh_in, SG["mix_pre_g"] = _dn_prenorm("mix_bwd_in", [dz_lo, dz_hi], W["w_in"][l], True, dh, sv["h"],
                                         S["mix_pre_g"])
    return dh_in


def _ple_out(h, p, gp, w3, g, tm=ROW_TILE):
    T, kp = p.shape
    nb, _, bw = w3.shape
    tm = _tile(T, tm)

    def body(h_ref, p_ref, gp_ref, w_ref, g_ref, e_ref, o_ref):
        p16 = p_ref[...].astype(BF)
        for b in range(nb):
            e_ref[:, b * bw:(b + 1) * bw] = jnp.dot(p16, w_ref[b], preferred_element_type=F32).astype(BF)
        q = _sigmoid(gp_ref[...].astype(F32)) * e_ref[...].astype(F32)
        o_ref[...] = h_ref[...] + _rms_fwd(q, g_ref[...])

    row = pl.BlockSpec((tm, D_MODEL), lambda i: (i, 0))
    return pl.pallas_call(
        body, grid=(T // tm,),
        in_specs=[row, pl.BlockSpec((tm, kp), lambda i: (i, 0)), row, pl.BlockSpec(w3.shape, lambda i: (0, 0, 0)),
                  pl.BlockSpec(g.shape, lambda i: (0, 0))],
        out_specs=[row, row],
        out_shape=[jax.ShapeDtypeStruct((T, D_MODEL), BF), jax.ShapeDtypeStruct((T, D_MODEL), F32)],
        name="ple_out", compiler_params=_params(1))(h, p, gp, w3, g)


def _ple_fwd(l, h, p_l, S, W, deps=()):
    n, gp = _norm_mm("ple_in", h, S["ple_pre_g"], [W["ple_w_gate"][l].reshape(1, D, D)], False, deps=deps)
    e, out = _ple_out(h, p_l, gp, W["ple_w_proj"][l], S["ple_post_g"])
    return out, dict(h=h, n=n, e=e, gp=gp, p=p_l)


def _ple_bwd_rows(dh, e, gp, g_post, w3, h, g_pre, deps=(), tm=ROW_TILE):
    T = dh.shape[0]
    w2 = w3.reshape(D_MODEL, D_MODEL)
    tm = _tile(T, tm)
    n_dep = len(deps)

    def body(*refs):
        dh_ref, e_ref, gp_ref, gpost_ref, w_ref, h_ref, gpre_ref, de_ref, dgp_ref, o_ref, dpost_ref, dpre_ref = \
            refs[n_dep:]
        first = pl.program_id(0) == 0
        d = dh_ref[...]
        sg = _sigmoid(gp_ref[...].astype(F32))
        ee = e_ref[...].astype(F32)
        dq, dpost = _rms_bwd(sg * ee, gpost_ref[...], d)
        de_ref[...] = (dq * sg).astype(BF)
        dgp = (dq * ee * sg * (1.0 - sg)).astype(BF)
        dgp_ref[...] = dgp
        dn = lax.dot_general(dgp, w_ref[...], _NT, preferred_element_type=F32)
        dx, dpre = _rms_bwd(h_ref[...], gpre_ref[...], dn)
        o_ref[...] = d + dx
        _acc_rows(dpost_ref, dpost, first)
        _acc_rows(dpre_ref, dpre, first)

    row = pl.BlockSpec((tm, D_MODEL), lambda i: (i, 0))
    vec = pl.BlockSpec((1, D_MODEL), lambda i: (0, 0))
    return pl.pallas_call(
        body, grid=(T // tm,),
        in_specs=[ANY] * n_dep + [row, row, row, vec, pl.BlockSpec(w2.shape, lambda i: (0, 0)), row, vec],
        out_specs=[row, row, row, vec, vec],
        out_shape=[jax.ShapeDtypeStruct((T, D_MODEL), BF)] * 2 + [jax.ShapeDtypeStruct((T, D_MODEL), F32)]
        + [jax.ShapeDtypeStruct((1, D_MODEL), F32)] * 2,
        name="ple_bwd", compiler_params=_params(1))(*deps, dh, e, gp, g_post, w2, h, g_pre)


def _ple_bwd(l, dh, sv, S, W, G, SG, deps=()):
    de, dgp, dh_in, SG["ple_post_g"], SG["ple_pre_g"] = _ple_bwd_rows(
        dh, sv["e"], sv["gp"], S["ple_post_g"], W["ple_w_gate"][l], sv["h"], S["ple_pre_g"], deps)
    G["ple_w_proj"] = _mm_tn("ple_dw_proj", sv["p"], de, G["ple_w_proj"], l, False)
    G["ple_w_gate"] = _mm_tn("ple_dw_gate", sv["n"], dgp, G["ple_w_gate"], l, True)
    return dh_in


def _layer_small(a, l):
    S = {}
    for name in SMALL:
        v = a[name][l]
        S[name] = v.reshape(1, -1) if v.ndim == 1 else v
    S["sgu_w_st"] = jnp.swapaxes(S["sgu_w_s"], 1, 2)
    S["sgu_bias"] = jnp.repeat(S["sgu_b_s"].T, LANES, axis=1)
    return S


def _layer_fwd(l, h, p_l, S, W, deps=(), hooks=None):
    hooks = hooks or {}

    def after(part, hv):
        return hooks[part](hv) if part in hooks else ()

    h, sv1 = _ffn_fwd(l, h, S, W, "ffn1", deps)
    h, sv2 = _mix_fwd(l, h, S, W, after("ffn1", h))
    h, sv3 = _ffn_fwd(l, h, S, W, "ffn2", after("mix", h))
    h, sv4 = _ple_fwd(l, h, p_l, S, W, after("ffn2", h))
    return h, (sv1, sv2, sv3, sv4)


def _layer_bwd(l, dh, sv, S, W, G, deps=(), hooks=None):
    hooks = hooks or {}

    def after(part, dv):
        return hooks[part](dv) if part in hooks else ()

    SG = {}
    dh = _ple_bwd(l, dh, sv[3], S, W, G, SG, deps)
    dh = _ffn_bwd(l, dh, sv[2], S, W, G, SG, "ffn2")
    dh = _mix_bwd(l, dh, sv[1], S, W, G, SG, after("ffn2", dh))
    dh = _ffn_bwd(l, dh, sv[0], S, W, G, SG, "ffn1", after("mix", dh))
    return dh, SG


HBM = pl.BlockSpec(memory_space=pltpu.HBM)
SEM = pl.BlockSpec(memory_space=pltpu.SEMAPHORE)
SIDE_EFFECT = pltpu.SideEffectType.DATAFLOW_SIDE_EFFECTING


def _place():
    x, y, c = lax.axis_index("x"), lax.axis_index("y"), lax.axis_index("c")
    chips = [(1 - x, y), (x, 1 - y), (1 - x, 1 - y)]
    return x, y, c, chips


def _remote(src, dst, send_sem, recv_sem, to):
    return pltpu.make_async_remote_copy(src_ref=src, dst_ref=dst, send_sem=send_sem, recv_sem=recv_sem,
                                        device_id=to, device_id_type=MESH)


def _split_start(name, plan, bufs, deps):
    count, fn = plan
    n, nd = len(bufs), len(deps)

    def body(*refs):
        send, recv = refs[nd + n], refs[nd + n + 1]
        x, y, c, chips = _place()
        for k, (src, dst, _, to) in enumerate(fn(refs[nd:nd + n], x, y, c, chips)):
            _remote(src, dst, send.at[k], recv.at[k], to).start()
        refs[-1][...] = jnp.zeros_like(refs[-1])

    res = pl.pallas_call(
        body, in_specs=[ANY] * nd + [HBM] * n,
        out_specs=[SEM, SEM] + [HBM] * n + [pl.BlockSpec(memory_space=pltpu.VMEM)],
        out_shape=[pltpu.SemaphoreType.DMA((count,)), pltpu.SemaphoreType.DMA((count,))]
        + [pltpu.HBM(b.shape, b.dtype) for b in bufs] + [jax.ShapeDtypeStruct((8, LANES), F32)],
        input_output_aliases={nd + i: 2 + i for i in range(n)}, name=name,
        compiler_params=pltpu.CompilerParams(has_side_effects=SIDE_EFFECT),
    )(*deps, *[pltpu.with_memory_space_constraint(b, pltpu.HBM) for b in bufs])
    return (res[0], res[1]), list(res[2:2 + n]), res[-1]


def _split_wait(name, plan, sems, bufs, after):
    _, fn = plan
    n = len(bufs)

    def body(*refs):
        send, recv = refs[n], refs[n + 1]
        x, y, c, chips = _place()
        for k, (src, _, land, to) in enumerate(fn(refs[:n], x, y, c, chips)):
            cp = _remote(src, land, send.at[k], recv.at[k], to)
            cp.wait_send()
            cp.wait_recv()

    res = pl.pallas_call(
        body, in_specs=[HBM] * n + [SEM, SEM] + [ANY] * len(after), out_specs=[HBM] * n,
        out_shape=[pltpu.HBM(b.shape, b.dtype) for b in bufs], input_output_aliases={i: i for i in range(n)},
        name=name, compiler_params=pltpu.CompilerParams(has_side_effects=SIDE_EFFECT))(*bufs, *sems, *after)
    return list(res)


def _gather_plans(n):
    def across(b, x, y, c, chips):
        me, out = 2 * x + y, []
        for a in range(n):
            rh = b[a].shape[1] // 2
            mine = b[a].at[me, pl.ds(c * rh, rh)]
            for cx, cy in chips:
                out.append((mine, mine, b[a].at[2 * cx + cy, pl.ds(c * rh, rh)], (cx, cy, c)))
        return out

    def to_sibling(b, x, y, c, chips):
        out = []
        for a in range(n):
            rh = b[a].shape[1] // 2
            for cx, cy in chips:
                piece = b[a].at[2 * cx + cy, pl.ds(c * rh, rh)]
                out.append((piece, piece, b[a].at[2 * cx + cy, pl.ds((1 - c) * rh, rh)], (x, y, 1 - c)))
        return out

    return (3 * n, across), (3 * n, to_sibling)


def _pair_plan(n):
    def fn(b, x, y, c, chips):
        out = []
        for a in range(n):
            rh = b[a].shape[1] // 2
            out.append((b[a].at[:, pl.ds((1 - c) * rh, rh)], b[n + a], b[n + a], (x, y, 1 - c)))
        return out

    return n, fn


def _cross_plan(n):
    def fn(b, x, y, c, chips):
        out = []
        for a in range(n):
            for j, (cx, cy) in enumerate(chips):
                out.append((b[a].at[2 * cx + cy], b[n + a].at[j], b[n + a].at[j], (cx, cy, c)))
        return out

    return 3 * n, fn


def _share_plan(n, l):
    def fn(b, x, y, c, chips):
        out = []
        for a in range(n):
            rh = b[a].shape[1] // 2
            mine = b[a].at[l, pl.ds(c * rh, rh)]
            out.append((mine, mine, b[a].at[l, pl.ds((1 - c) * rh, rh)], (x, y, 1 - c)))
        return out

    return n, fn


def _peers(x, y, c):
    return [(1 - x if m & 4 else x, 1 - y if m & 2 else y, 1 - c if m & 1 else c) for m in range(1, 8)]


def _small_plans():
    def scatter(b, x, y, c, chips):
        return [(b[0].at[4 * px + 2 * py + pc], b[1].at[m], b[1].at[m], (px, py, pc))
                for m, (px, py, pc) in enumerate(_peers(x, y, c))]

    def gather(b, x, y, c, chips):
        mine = b[0].at[4 * x + 2 * y + c]
        return [(mine, mine, b[0].at[4 * px + 2 * py + pc], (px, py, pc)) for px, py, pc in _peers(x, y, c)]

    return (7, scatter), (7, gather)


def _sum_small(v3, got, pos):
    rs = v3.shape[1]
    tm = _tile(rs)
    ins = [(v3, (None, tm, LANES), lambda i, p: (p[2], i, 0))]
    ins += [(got, (None, tm, LANES), lambda i, p, m=m: (m, i, 0)) for m in range(7)]
    return _tiled("sum_small", lambda *t: (((((((t[0] + t[1]) + t[2]) + t[3]) + t[4]) + t[5]) + t[6]) + t[7],),
                  (rs // tm,), pos, ins, [((8, rs, LANES), F32, (None, tm, LANES), lambda i, p: (p[2], i, 0))])[0]


ADD_ROWS = 256


def _multi_tiled(name, fn, pos, groups, in_place=False):
    steps = max(g[2] for g in groups)
    flat_in, in_specs, out_specs, out_shape, counts, dests = [], [], [], [], [], []
    for ins, rows, n_t, (shape, dtype, oidx, dest) in groups:
        for arr, idx in ins:
            flat_in.append(arr)
            in_specs.append(pl.BlockSpec((rows, arr.shape[1]),
                                         lambda i, p, idx=idx, n_t=n_t: (idx(jnp.minimum(i, n_t - 1), p), 0)))
        out_specs.append(pl.BlockSpec((rows, shape[1]),
                                      lambda i, p, oidx=oidx, n_t=n_t: (oidx(jnp.minimum(i, n_t - 1), p), 0)))
        out_shape.append(jax.ShapeDtypeStruct(shape, dtype))
        counts.append((len(ins), n_t))
        dests.append(dest)
    n_in = len(flat_in)
    extra = dests if in_place else []

    def body(_, *refs):
        outs = refs[n_in + len(extra):]
        k = 0
        for (n_a, n_t), o_ref in zip(counts, outs):
            tiles = refs[k:k + n_a]
            k += n_a

            @pl.when(pl.program_id(0) < n_t)
            def _(tiles=tiles, o_ref=o_ref):
                o_ref[...] = fn(*[t[...] for t in tiles]).astype(o_ref.dtype)

    spec = pltpu.PrefetchScalarGridSpec(num_scalar_prefetch=1, grid=(steps,),
                                        in_specs=in_specs + [ANY] * len(extra), out_specs=out_specs)
    return pl.pallas_call(body, grid_spec=spec, out_shape=out_shape,
                          input_output_aliases={1 + n_in + k: k for k in range(len(extra))}, name=name,
                          compiler_params=_params(1))(pos, *flat_in, *extra)


def _add_pair(grads, got, pos):
    groups = []
    for g, q in zip(grads, got):
        nb, R, C = g.shape
        rh = R // 2
        rows = _tile(rh, ADD_ROWS)
        nh = rh // rows
        groups.append(([(g.reshape(nb * R, C), lambda t, p, nh=nh: (t // nh) * 2 * nh + p[1] * nh + t % nh),
                        (q.reshape(nb * rh, C), lambda t, p: t)], rows, nb * nh,
                       ((nb * rh, C), BF, lambda t, p: t, None)))
    res = _multi_tiled("rs_add_pair", lambda u, w: u.astype(F32) + w.astype(F32), pos, groups)
    return [t.reshape(q.shape) for t, q in zip(res, got)]


def _add_chips(parts, slots, reduced, l, pos):
    def add(own, s0, s1, s2):
        return ((own.astype(F32) + s0.astype(F32)) + s1.astype(F32)) + s2.astype(F32)

    groups = []
    for t, s, red in zip(parts, slots, reduced):
        nb, rh, C = t.shape
        L = red.shape[0]
        rows = _tile(rh, ADD_ROWS)
        nh = rh // rows
        ins = [(t.reshape(nb * rh, C), lambda i, p, nh=nh: p[0] * nh + i)]
        ins += [(s.reshape(3 * rh, C), lambda i, p, j=j, nh=nh: j * nh + i) for j in range(3)]
        groups.append((ins, rows, nh, ((L * 2 * rh, C), F32, lambda i, p, nh=nh: l * 2 * nh + p[1] * nh + i,
                                 red.reshape(L * 2 * rh, C))))
    res = _multi_tiled("rs_add_chips", add, pos, groups, in_place=True)
    return [buf.reshape(red.shape) for buf, red in zip(res, reduced)]


def _adamw_math(w, g, m, v):
    m = ADAM_B1 * m + (1.0 - ADAM_B1) * g
    v = ADAM_B2 * v + (1.0 - ADAM_B2) * (g * g)
    m_hat = m / (1.0 - ADAM_B1 ** ADAM_STEP)
    v_hat = v / (1.0 - ADAM_B2 ** ADAM_STEP)
    return -ADAM_LR * (m_hat / (jnp.sqrt(v_hat) + ADAM_EPS) + ADAM_WD * w), m, v


def _adamw(w, g, m, v, lo=0, hi=None, into=None, deps=()):
    L, R, C = w.shape
    hi = L if hi is None else hi
    tr = _tile(R, max(16, ADAM_TILE_ELEMS // C))
    extra = (list(into) if into else []) + list(deps)
    n_alias = 4 if into else 0

    def body(w_ref, g_ref, m_ref, v_ref, *rest):
        go_ref, d_ref, mo_ref, vo_ref = rest[len(extra):]
        gv = g_ref[...]
        d, mn, vn = _adamw_math(w_ref[...], gv, m_ref[...], v_ref[...])
        go_ref[...] = gv
        d_ref[...] = d
        mo_ref[...] = mn
        vo_ref[...] = vn

    spec = pl.BlockSpec((None, tr, C), lambda l, i: (l + lo, i, 0))
    out = jax.ShapeDtypeStruct(w.shape, F32)
    return pl.pallas_call(body, grid=(hi - lo, R // tr), in_specs=[spec] * 4 + [ANY] * len(extra),
                          out_specs=[spec] * 4, out_shape=[out] * 4,
                          input_output_aliases={4 + k: k for k in range(n_alias)}, name="adamw",
                          compiler_params=_params(2))(w, g, m, v, *extra)


def _pack(parts):
    flat = jnp.concatenate([q.reshape(-1, LANES) for q in parts], axis=0)
    return jnp.pad(flat, ((0, -flat.shape[0] % ROW_TILE), (0, 0)))


def _unpack(flat, like):
    out, r = [], 0
    for q in like:
        n = q.size // LANES
        out.append(flat[r:r + n].reshape(q.shape))
        r += n
    return out


def _train_step(a):
    a = dict(a)
    L = a["ffn1_pre_g"].shape[0]
    x, y, c, _ = _place()
    chip = 2 * x + y
    pos = jnp.stack([chip, c, 2 * chip + c]).astype(jnp.int32)
    for name in TRANSPOSED:
        for pre in ("", "m_", "v_"):
            a[pre + name] = jnp.swapaxes(a[pre + name], 1, 2)
    big = [b[0] for b in BIG]
    gathered = big + ["conv_dw_k"]
    n_w, n_g = len(gathered), len(big)

    own = [None] * n_w
    W = {name: [None] * L for name in gathered}
    every = list(range(n_w))
    first, mixer, later = every[:3], every[3:8] + [n_g], every[8:n_g]
    rest = mixer + later

    def cast(i, deps):
        if i == n_g:
            taps = a["conv_dw_k"].reshape(L, CONV_TAPS, LANES)
            return _cast_layers("pad_conv_taps", taps, CONV_PAD, LANES, F32, pos, deps)
        name, _, _, _, rp, cp = BIG[i]
        return _cast_layers("cast_weight", a[name], rp, cp, BF, pos, deps)

    def gather_first(l, ids, tag, deps):
        return _split_start("gather_a%d%s" % (l, tag), _gather_plans(len(ids))[0], [own[i][l] for i in ids], deps)

    def gather_second(l, ids, tag, state, after):
        across, to_sibling = _gather_plans(len(ids))
        bufs = _split_wait("gather_a%d%s_done" % (l, tag), across, state[0], state[1], after)
        return _split_start("gather_b%d%s" % (l, tag), to_sibling, bufs, [])

    def gather_done(l, ids, tag, state, after):
        to_sibling = _gather_plans(len(ids))[1]
        bufs = _split_wait("gather_b%d%s_done" % (l, tag), to_sibling, state[0], state[1], after)
        for i, buf in zip(ids, bufs):
            W[gathered[i]][l] = buf

    for i in first:
        own[i] = cast(i, ())
    state = gather_first(0, first, "f", [])
    for i in rest:
        own[i] = cast(i, (state[2],))
    state = gather_second(0, first, "f", state, [own[i][0] for i in rest])
    gather_done(0, first, "f", state, [])

    parts = {"f": first, "m": mixer, "t": later}
    flying = {}

    def begin(l, part, deps):
        flying[l, part] = gather_first(l, parts[part], part, deps)
        return flying[l, part][2]

    def hand_on(l, part, after):
        flying[l, part] = gather_second(l, parts[part], part, flying[l, part], after)
        return flying[l, part][2]

    def arrive(l, part, after):
        gather_done(l, parts[part], part, flying.pop((l, part)), after)

    def hooks_of(l):
        nxt = l + 1 < L

        def after_ffn1(hv):
            tokens = []
            if l == 0:
                hand_on(0, "m", [hv])
            arrive(l, "m", [hv])
            if l == 0:
                tokens.append(begin(0, "t", [hv]))
            else:
                tokens.append(hand_on(l, "t", [hv]))
            if nxt:
                tokens.append(begin(l + 1, "f", tokens[-1:]))
            return tuple(tokens)

        def after_mix(hv):
            tokens = []
            if l == 0:
                hand_on(0, "t", [hv])
            arrive(l, "t", [hv])
            if nxt:
                tokens.append(hand_on(l + 1, "f", [hv]))
                tokens.append(begin(l + 1, "m", tokens[-1:]))
            return tuple(tokens)

        def after_ffn2(hv):
            tokens = []
            if nxt:
                arrive(l + 1, "f", [hv])
                tokens.append(hand_on(l + 1, "m", [hv]))
                tokens.append(begin(l + 1, "t", tokens[-1:]))
            return tuple(tokens)

        return {"ffn1": after_ffn1, "mix": after_mix, "ffn2": after_ffn2}

    small = [_layer_small(a, l) for l in range(L)]
    h, saved = a["x"][0], []
    deps = (begin(0, "m", []),)
    for l in range(L):
        h, sv = _layer_fwd(l, h, a["p"][l, 0], small[l], W, deps, hooks_of(l))
        saved.append(sv)
        deps = ()

    def loss_fn(yv, t):
        e = yv - t
        return e * (1.0 / D), jnp.sum(e * e, axis=0, keepdims=True)

    dh, lsum = _rowwise("loss", loss_fn, [(h, D, 0), (a["loss_target"][0], D, 0)], [], [(D, F32)], [(1, D)])
    loss = lax.psum(0.5 * jnp.sum(lsum) / D, ("x", "y", "c"))

    G = {name: [jax.ShapeDtypeStruct((N_CHIPS, rp, cp), BF)] * L for name, _, _, _, rp, cp in BIG}
    reduced = [lax.empty((L, rp, cp), F32) for _, _, _, _, rp, cp in BIG]
    small_grads = [None] * L
    whole = list(range(n_g))
    piece_a, piece_b, piece_c = whole[8:], whole[3:8], whole[:3]

    def pair_start(l, ids, tag, deps):
        grads = [G[big[i]][l] for i in ids]
        lands = [lax.empty((N_CHIPS, g.shape[1] // 2, g.shape[2]), BF) for g in grads]
        return _split_start("rs_pair%d%s" % (l, tag), _pair_plan(len(ids)), grads + lands, deps)

    def cross_start(l, ids, tag, state, after):
        n = len(ids)
        bufs = _split_wait("rs_pair%d%s_done" % (l, tag), _pair_plan(n), state[0], state[1], after)
        parts = _add_pair(bufs[:n], bufs[n:], pos)
        lands = [lax.empty((3,) + t.shape[1:], BF) for t in parts]
        return _split_start("rs_cross%d%s" % (l, tag), _cross_plan(n), parts + lands, [])

    def cross_finish(l, ids, tag, state, after, reduced):
        n = len(ids)
        bufs = _split_wait("rs_cross%d%s_done" % (l, tag), _cross_plan(n), state[0], state[1], after)
        reduced = list(reduced)
        for i, r in zip(ids, _add_chips(bufs[:n], bufs[n:], [reduced[i] for i in ids], l, pos)):
            reduced[i] = r
        return reduced

    def share_start(l, reduced):
        return _split_start("rs_share%d" % l, _share_plan(n_g, l), reduced, [])

    def share_done(l, state, after):
        return _split_wait("rs_share%d_done" % l, _share_plan(n_g, l), state[0], state[1], after)

    small_names = SMALL + ("conv_dw_k",)
    scatter, gather = _small_plans()
    totals = [None] * L

    def small_scatter(l, deps):
        packed = _pack([small_grads[l][name] for name in small_names])
        v3 = packed.reshape(8, packed.shape[0] // 8, LANES)
        return _split_start("small_scatter%d" % l, scatter, [v3, lax.empty((7,) + v3.shape[1:], F32)], deps)

    def small_gather(l, state, after):
        bufs = _split_wait("small_scatter%d_done" % l, scatter, state[0], state[1], after)
        return _split_start("small_gather%d" % l, gather, [_sum_small(bufs[0], bufs[1], pos)], [])

    def small_done(l, state, after):
        total = _split_wait("small_gather%d_done" % l, gather, state[0], state[1], after)[0]
        totals[l] = total.reshape(-1, LANES)

    st_pair = st_share = st_small = None
    for l in reversed(range(L)):
        deps = tuple(s[2] for s in (st_pair, st_share, st_small) if s is not None)
        box = {}

        def after_ffn2(dm, l=l, box=box, st_pair=st_pair, st_share=st_share, st_small=st_small):
            out = []
            if st_share is not None:
                box["reduced"] = share_done(l + 2, st_share, [dm])
            if st_small is not None:
                box["small"] = small_gather(l + 1, st_small, [dm])
                out.append(box["small"][2])
            if st_pair is not None:
                box["cross"] = cross_start(l + 1, whole, "", st_pair, [dm])
                out.append(box["cross"][2])
            if l == 0:
                box["pair_a"] = pair_start(0, piece_a, "a", [dm])
                out.append(box["pair_a"][2])
            return tuple(out)

        def after_mix(dm, box=box):
            box["cross_a"] = cross_start(0, piece_a, "a", box["pair_a"], [dm])
            box["pair_b"] = pair_start(0, piece_b, "b", [dm])
            return (box["cross_a"][2], box["pair_b"][2])

        hooks = {"ffn2": after_ffn2, "mix": after_mix} if l == 0 else {"ffn2": after_ffn2}
        dh, small_grads[l] = _layer_bwd(l, dh, saved[l], small[l], W, G, deps, hooks)
        if st_share is not None:
            reduced = box["reduced"]
        if "small" in box:
            small_done(l + 1, box["small"], [dh])
        st_share = None
        if "cross" in box:
            reduced = cross_finish(l + 1, whole, "", box["cross"], [dh], reduced)
            st_share = share_start(l + 1, reduced)
        st_small = small_scatter(l, [dh])
        st_pair = pair_start(l, whole, "", [st_small[2]]) if l else None
    grad_x = dh
    cross_b = cross_start(0, piece_b, "b", box["pair_b"], [st_small[2]])
    cross_c = cross_start(0, piece_c, "c", pair_start(0, piece_c, "c", [cross_b[2]]), [])
    if st_share is not None:
        reduced = share_done(1, st_share, [cross_c[2]])
    upper, token = {}, cross_c[2]
    for k, (name, red) in enumerate(zip(big, reduced)):
        if k == 2:
            st_small = small_gather(0, st_small, [token])
            token = st_small[2]
        if L > 1:
            upper[name] = _adamw(a[name], red, a["m_" + name], a["v_" + name], 1, L, deps=[token])
            token = upper[name][1]
    small_done(0, st_small, [token])
    per_layer = [_unpack(totals[l], [small_grads[l][name] for name in small_names]) for l in range(L)]
    summed = {name: jnp.stack([per_layer[l][k] for l in range(L)]) for k, name in enumerate(small_names)}
    done = [r[1] for r in upper.values()] + [summed[small_names[0]]]
    for ids, tag, state in ((piece_a, "a", box["cross_a"]), (piece_b, "b", cross_b), (piece_c, "c", cross_c)):
        reduced = cross_finish(0, ids, tag, state, done, reduced)
    reduced = share_done(0, share_start(0, reduced), [])
    big_grads = dict(zip(big, reduced))

    grads, deltas, new_m, new_v = {}, {}, {}, {}
    for name in big:
        res = _adamw(a[name], big_grads[name], a["m_" + name], a["v_" + name], 0, 1, upper.get(name))
        if name in TRANSPOSED:
            res = [jnp.swapaxes(r, 1, 2) for r in res]
        grads[name], deltas[name], new_m[name], new_v[name] = res
    taps = lax.dynamic_slice_in_dim(summed["conv_dw_k"], chip * LANES, LANES, axis=2)[:, :CONV_TAPS]
    grads["conv_dw_k"] = taps.reshape(a["conv_dw_k"].shape)
    for name in SMALL:
        grads[name] = summed[name].reshape(a[name].shape)
    shapes = [a[name] for name in small_names]
    res = _adamw(*[_pack([a[pre + name] if pre != "g" else grads[name] for name in small_names])[None]
                   for pre in ("", "g", "m_", "v_")])
    for dst, flat in zip((deltas, new_m, new_v), res[1:]):
        for name, val in zip(small_names, _unpack(flat[0], shapes)):
            dst[name] = val

    return (loss, grad_x[None], *[grads[n] for n in WEIGHTS], *[deltas[n] for n in WEIGHTS],
            *[new_m[n] for n in WEIGHTS], *[new_v[n] for n in WEIGHTS])


def kernel(x, p, ffn1_pre_g, ffn1_w_gate, ffn1_w_up, ffn1_w_down, ffn1_post_g, mix_pre_g, w_in, pool_w, pool_scale, w_pool_out, sgu_ln_g, sgu_ln_b, sgu_w_s, sgu_b_s, w_sgu_out, conv_dw_k, conv_dw_b, conv_ln_g, conv_ln_b, w_conv_out, w_out, mix_post_g, ffn2_pre_g, ffn2_w_gate, ffn2_w_up, ffn2_w_down, ffn2_post_g, ple_w_proj, ple_pre_g, ple_w_gate, ple_post_g, loss_target, m_ffn1_pre_g, m_ffn1_w_gate, m_ffn1_w_up, m_ffn1_w_down, m_ffn1_post_g, m_mix_pre_g, m_w_in, m_pool_w, m_pool_scale, m_w_pool_out, m_sgu_ln_g, m_sgu_ln_b, m_sgu_w_s, m_sgu_b_s, m_w_sgu_out, m_conv_dw_k, m_conv_dw_b, m_conv_ln_g, m_conv_ln_b, m_w_conv_out, m_w_out, m_mix_post_g, m_ffn2_pre_g, m_ffn2_w_gate, m_ffn2_w_up, m_ffn2_w_down, m_ffn2_post_g, m_ple_w_proj, m_ple_pre_g, m_ple_w_gate, m_ple_post_g, v_ffn1_pre_g, v_ffn1_w_gate, v_ffn1_w_up, v_ffn1_w_down, v_ffn1_post_g, v_mix_pre_g, v_w_in, v_pool_w, v_pool_scale, v_w_pool_out, v_sgu_ln_g, v_sgu_ln_b, v_sgu_w_s, v_sgu_b_s, v_w_sgu_out, v_conv_dw_k, v_conv_dw_b, v_conv_ln_g, v_conv_ln_b, v_w_conv_out, v_w_out, v_mix_post_g, v_ffn2_pre_g, v_ffn2_w_gate, v_ffn2_w_up, v_ffn2_w_down, v_ffn2_post_g, v_ple_w_proj, v_ple_pre_g, v_ple_w_gate, v_ple_post_g):
    return _train_step(dict(locals()))
```

```python
import math

import jax
import jax.numpy as jnp
from jax import lax
from jax.experimental import pallas as pl
from jax.experimental.pallas import tpu as pltpu

BF = jnp.bfloat16
F32 = jnp.float32
EPS = 1e-6
D_MODEL = 1024
LANES = 128
SUBLANES = 8
MXU_TILE = 256
N_CHIPS = 4
FFN_SHARD = 704
FFN_SHARD_PAD = 768
POOL_WINDOWS = (2, 4, 8, 16)
SGU_HEADS = 4
CHUNK = 128
CONV_TAPS = 31
CONV_PAD = 32
ROW_TILE = 512
EPI_ROWS = 256
VMEM_LIMIT_BYTES = 56 * 1024 * 1024
ADAM_TILE_ELEMS = 3 * 128 * 1024
ADAM_LR, ADAM_B1, ADAM_B2, ADAM_EPS, ADAM_WD, ADAM_STEP =0.001, 0.9, 0.999, 1e-08, 0.01, 10
MESH = pl.DeviceIdType.MESH
ANY = pl.BlockSpec(memory_space=pl.ANY)

BRANCH = 512
ZB_POOL, ZB_U, ZB_V, ZB_A, ZB_B, ZB_GATES = 0, 1, 2, 3, 4, 5
DZ_HALF = 2816

TRANSPOSED = ("ffn1_w_gate", "ffn1_w_up", "ffn2_w_gate", "ffn2_w_up")
BIG = (
    ("ffn1_w_gate", "row", FFN_SHARD, 1024, FFN_SHARD_PAD, 1024),
    ("ffn1_w_up", "row", FFN_SHARD, 1024, FFN_SHARD_PAD, 1024),
    ("ffn1_w_down", "row", FFN_SHARD, 1024, FFN_SHARD_PAD, 1024),
    ("w_in", "col", 1024, 1408, 1024, 1408),
    ("w_pool_out", "col", 512, 256, 512, 256),
    ("w_sgu_out", "col", 512, 256, 512, 256),
    ("w_conv_out", "col", 512, 256, 512, 256),
    ("w_out", "row", 256, 1024, 256, 1024),
    ("ffn2_w_gate", "row", FFN_SHARD, 1024, FFN_SHARD_PAD, 1024),
    ("ffn2_w_up", "row", FFN_SHARD, 1024, FFN_SHARD_PAD, 1024),
    ("ffn2_w_down", "row", FFN_SHARD, 1024, FFN_SHARD_PAD, 1024),
    ("ple_w_proj", "col", 256, 256, 256, 256),
    ("ple_w_gate", "row", 256, 1024, 256, 1024),
)
SMALL = ("ffn1_pre_g", "ffn1_post_g", "mix_pre_g", "pool_w", "pool_scale", "sgu_ln_g", "sgu_ln_b", "sgu_w_s",
         "sgu_b_s", "conv_dw_b", "conv_ln_g", "conv_ln_b", "mix_post_g", "ffn2_pre_g", "ffn2_post_g",
         "ple_pre_g", "ple_post_g")
WEIGHTS = ("ffn1_pre_g", "ffn1_w_gate", "ffn1_w_up", "ffn1_w_down", "ffn1_post_g", "mix_pre_g", "w_in", "pool_w",
           "pool_scale", "w_pool_out", "sgu_ln_g", "sgu_ln_b", "sgu_w_s", "sgu_b_s", "w_sgu_out", "conv_dw_k",
           "conv_dw_b", "conv_ln_g", "conv_ln_b", "w_conv_out", "w_out", "mix_post_g", "ffn2_pre_g", "ffn2_w_gate",
           "ffn2_w_up", "ffn2_w_down", "ffn2_post_g", "ple_w_proj", "ple_pre_g", "ple_w_gate", "ple_post_g")


def _params(n_grid):
    return pltpu.CompilerParams(dimension_semantics=("arbitrary",) * n_grid, vmem_limit_bytes=VMEM_LIMIT_BYTES)


def _tile(n, cap=ROW_TILE):
    for t in range(min(cap, n) - min(cap, n) % 16, 0, -16):
        if n % t == 0:
            return t
    return n


def _sigmoid(x):
    return 0.5 * jnp.tanh(0.5 * x) + 0.5


def _silu_and_grad(x):
    s = _sigmoid(x)
    return x * s, s * (1.0 + x * (1.0 - s))


def _gelu_and_grad(x):
    cdf = 0.5 * (1.0 + lax.erf(x * (1.0 / math.sqrt(2.0))))
    pdf = jnp.exp(-0.5 * x * x) * (1.0 / math.sqrt(2.0 * math.pi))
    return x * cdf, cdf + x * pdf


def _rms_fwd(x, g):
    return x * lax.rsqrt(jnp.mean(x * x, axis=-1, keepdims=True) + EPS) * g


def _rms_bwd(x, g, dy):
    r = lax.rsqrt(jnp.mean(x * x, axis=-1, keepdims=True) + EPS)
    xh = x * r
    dxh = dy * g
    dx = r * (dxh - xh * jnp.mean(dxh * xh, axis=-1, keepdims=True))
    return dx, jnp.sum(dy * xh, axis=0, keepdims=True)


def _ln_stats(x):
    xc = x - jnp.mean(x, axis=-1, keepdims=True)
    r = lax.rsqrt(jnp.mean(xc * xc, axis=-1, keepdims=True) + EPS)
    return xc * r, r


def _ln_bwd(xh, r, g, dy):
    dxh = dy * g
    dx = r * (dxh - jnp.mean(dxh, axis=-1, keepdims=True) - xh * jnp.mean(dxh * xh, axis=-1, keepdims=True))
    return dx, jnp.sum(dy * xh, axis=0, keepdims=True), jnp.sum(dy, axis=0, keepdims=True)


def _rowwise(name, fn, rows, consts, outs, accs=(), tm=ROW_TILE, deps=()):
    T = rows[0][0].shape[-2]
    tm = _tile(T, tm)
    n_in, n_o, n_dep = len(rows) + len(consts), len(outs), len(deps)

    def body(*refs):
        refs = refs[n_dep:]
        res = fn(*[r[...] for r in refs[:n_in]])
        for ref, val in zip(refs[n_in:n_in + n_o], res[:n_o]):
            ref[...] = val.astype(ref.dtype)
        acc_refs = refs[n_in + n_o:]
        if acc_refs:
            @pl.when(pl.program_id(0) == 0)
            def _():
                for ref, val in zip(acc_refs, res[n_o:]):
                    ref[...] = val

            @pl.when(pl.program_id(0) != 0)
            def _():
                for ref, val in zip(acc_refs, res[n_o:]):
                    ref[...] += val

    in_specs = [ANY] * n_dep
    for row in rows:
        w, cb = row[1], row[2]
        if len(row) == 4:
            in_specs.append(pl.BlockSpec((None, tm, w), lambda i, cb=cb, ld=row[3]: (ld, i, cb)))
        else:
            in_specs.append(pl.BlockSpec((tm, w), lambda i, cb=cb: (i, cb)))
    in_specs += [pl.BlockSpec(c.shape, lambda i: (0, 0)) for c in consts]
    out_specs = [pl.BlockSpec((tm, w), lambda i: (i, 0)) for w, _ in outs]
    out_specs += [pl.BlockSpec(s, lambda i: (0, 0)) for s in accs]
    out_shape = [jax.ShapeDtypeStruct((T, w), dt) for w, dt in outs]
    out_shape += [jax.ShapeDtypeStruct(s, F32) for s in accs]
    return pl.pallas_call(body, grid=(T // tm,), in_specs=in_specs, out_specs=out_specs, out_shape=out_shape,
                          name=name, compiler_params=_params(1))(*deps, *[r[0] for r in rows], *consts)


def _tiled(name, fn, grid, pos, ins, outs):
    n_in = len(ins)

    def body(_, *refs):
        res = fn(*[r[...] for r in refs[:n_in]])
        for ref, val in zip(refs[n_in:], res):
            ref[...] = val.astype(ref.dtype)

    spec = pltpu.PrefetchScalarGridSpec(
        num_scalar_prefetch=1, grid=grid, in_specs=[pl.BlockSpec(bs, im) for _, bs, im in ins],
        out_specs=[pl.BlockSpec(bs, im) for _, _, bs, im in outs])
    return pl.pallas_call(body, grid_spec=spec, out_shape=[jax.ShapeDtypeStruct(s, d) for s, d, _, _ in outs],
                          name=name, compiler_params=_params(len(grid)))(pos, *[a for a, _, _ in ins])


def _cast_layers(name, w, rp, cp, dtype, pos, deps=()):
    L, r, c = w.shape

    def body(_, w_ref, *rest):
        for k, o_ref in enumerate(rest[len(deps):]):
            @pl.when(pl.program_id(0) == k)
            def _(o_ref=o_ref):
                if (rp, cp) != (r, c):
                    o_ref[...] = jnp.zeros_like(o_ref)
                    o_ref[pl.ds(0, r), pl.ds(0, c)] = w_ref[...].astype(dtype)
                else:
                    o_ref[...] = w_ref[...].astype(dtype)

    spec = pltpu.PrefetchScalarGridSpec(
        num_scalar_prefetch=1, grid=(L,),
        in_specs=[pl.BlockSpec((None, r, c), lambda l, p: (l, 0, 0))] + [ANY] * len(deps),
        out_specs=[pl.BlockSpec((None, rp, cp), lambda l, p: (p[0], 0, 0))] * L)
    return pl.pallas_call(body, grid_spec=spec, out_shape=[jax.ShapeDtypeStruct((N_CHIPS, rp, cp), dtype)] * L,
                          name=name, compiler_params=_params(1))(pos, w, *deps)


_NN = (((1,), (0,)), ((), ()))
_NT = (((1,), (1,)), ((), ()))
_TN = (((0,), (0,)), ((), ()))


def _mm_tn(name, a, dy, buf, l, a_blocked, tk=ROW_TILE, first=0):
    T = a.shape[0]
    nb, R, C = buf[l].shape
    extra = [buf[l]] if first else []

    def body(a_ref, dy_ref, *rest):
        rest[-1][...] = lax.dot_general(a_ref[...].astype(BF), dy_ref[...].astype(BF), _TN,
                                        preferred_element_type=F32).astype(BF)

    if a_blocked:
        grid = (nb,)
        in_specs = [pl.BlockSpec((T, R), lambda b: (0, b)), pl.BlockSpec((T, C), lambda b: (0, 0))]
        out_specs = pl.BlockSpec((None, R, C), lambda b: (b, 0, 0))
    else:
        tk = min(tk, R)
        grid = (dy.shape[1] // C, R // tk)
        in_specs = [pl.BlockSpec((T, tk), lambda b, k: (0, k)), pl.BlockSpec((T, C), lambda b, k: (0, b))]
        out_specs = pl.BlockSpec((None, tk, C), lambda b, k: (b + first, k, 0))
    buf = list(buf)
    buf[l] = pl.pallas_call(body, grid=grid, in_specs=in_specs + [ANY] * len(extra), out_specs=out_specs,
                            out_shape=jax.ShapeDtypeStruct((nb, R, C), BF),
                            input_output_aliases={2: 0} if extra else {}, name=name,
                            compiler_params=_params(len(grid)))(a, dy, *extra)
    return buf


def _acc_rows(ref, val, first):
    @pl.when(first)
    def _():
        ref[...] = val

    @pl.when(jnp.logical_not(first))
    def _():
        ref[...] += val


def _norm_mm(name, h, g, ws, trans_w, act=False, deps=(), tm=2 * ROW_TILE):
    T = h.shape[0]
    nb, r, cc = ws[0].shape
    bo = r if trans_w else cc
    tm = _tile(T, tm)
    n_w, n_dep = len(ws), len(deps)
    n_i, chunk = T // tm, tm // nb

    def body(*refs):
        refs = refs[n_dep:]
        h_ref, next_ref, g_ref, w_refs = refs[0], refs[1], refs[2], refs[3:3 + n_w]
        n_ref, o_refs, n_s = refs[3 + n_w], refs[4 + n_w:4 + 2 * n_w], refs[-1]
        i, b = pl.program_id(0), pl.program_id(1)
        slot = i % 2

        @pl.when(jnp.logical_and(i == 0, b == 0))
        def _():
            n_s[0] = _rms_fwd(h_ref[...].astype(F32), g_ref[...]).astype(BF)

        @pl.when(b == 0)
        def _():
            n_ref[...] = n_s[slot]

        n = n_s[slot]
        prods = []
        for w_ref, o_ref in zip(w_refs, o_refs):
            prods.append(lax.dot_general(n, w_ref[...], _NT if trans_w else _NN,
                                         preferred_element_type=F32).astype(BF))
            o_ref[...] = prods[-1]
        if act:
            refs[4 + 2 * n_w][...] = (_silu_and_grad(prods[0].astype(F32))[0] * prods[1].astype(F32)).astype(BF)
        rows = pl.ds(pl.multiple_of(b * chunk, chunk), chunk)
        n_s[1 - slot, rows, :] = _rms_fwd(next_ref[rows, :].astype(F32), g_ref[...]).astype(BF)

    wide = pl.BlockSpec((tm, bo), lambda i, b: (i, b))
    n_out = n_w + (1 if act else 0)
    return pl.pallas_call(
        body, grid=(n_i, nb),
        in_specs=[ANY] * n_dep + [pl.BlockSpec((tm, D_MODEL), lambda i, b: (i, 0)),
                                  pl.BlockSpec((tm, D_MODEL), lambda i, b: (jnp.minimum(i + 1, n_i - 1), 0)),
                                  pl.BlockSpec(g.shape, lambda i, b: (0, 0))]
        + [pl.BlockSpec((None, r, cc), lambda i, b: (b, 0, 0))] * n_w,
        out_specs=[pl.BlockSpec((tm, D_MODEL), lambda i, b: (i, 0))] + [wide] * n_out,
        out_shape=[jax.ShapeDtypeStruct((T, D_MODEL), BF)] + [jax.ShapeDtypeStruct((T, nb * bo), BF)] * n_out,
        scratch_shapes=[pltpu.VMEM((2, tm, D_MODEL), BF)], name=name,
        compiler_params=_params(2))(*deps, h, h, g, *ws)


def _mm_res(name, x, w3, h, g, coef, tm=ROW_TILE):
    T, kx = x.shape
    w2 = w3.reshape(kx, D_MODEL)
    tm = _tile(T, tm)

    def body(x_ref, w_ref, h_ref, g_ref, f_ref, o_ref):
        f = jnp.dot(x_ref[...], w_ref[...], preferred_element_type=F32).astype(BF)
        f_ref[...] = f
        o_ref[...] = h_ref[...] + coef * _rms_fwd(f.astype(F32), g_ref[...])

    row = pl.BlockSpec((tm, D_MODEL), lambda i: (i, 0))
    return pl.pallas_call(
        body, grid=(T // tm,),
        in_specs=[pl.BlockSpec((tm, kx), lambda i: (i, 0)), pl.BlockSpec(w2.shape, lambda i: (0, 0)), row,
                  pl.BlockSpec(g.shape, lambda i: (0, 0))],
        out_specs=[row, row],
        out_shape=[jax.ShapeDtypeStruct((T, D_MODEL), BF), jax.ShapeDtypeStruct((T, D_MODEL), F32)],
        name=name, compiler_params=_params(1))(x, w2, h, g)


def _resbwd_mm(name, dh, f, g, coef, w3, trans_w, act=None, deps=(), tm=2 * ROW_TILE):
    T = dh.shape[0]
    nb, r, cc = w3.shape
    bo = r if trans_w else cc
    tm = _tile(T, tm)
    n_dep, n_act = len(deps), 3 if act else 0
    n_i = T // tm

    def body(*refs):
        refs = refs[n_dep:]
        dh_ref, f_ref, g_ref, w_ref = refs[:4]
        df_ref, dg_ref = refs[4 + n_act], refs[5 + n_act]
        df_s = refs[-2] if act else refs[-1]
        i, b = pl.program_id(0), pl.program_id(1)

        @pl.when(b == 0)
        def _():
            dg = jnp.zeros((1, D_MODEL), F32)
            for c in range(tm // EPI_ROWS):
                rows = slice(c * EPI_ROWS, (c + 1) * EPI_ROWS)
                dx, dg_c = _rms_bwd(f_ref[rows, :].astype(F32), g_ref[...], coef * dh_ref[rows, :])
                df_s[rows, :] = dx.astype(BF)
                df_ref[rows, :] = dx.astype(BF)
                dg = dg + dg_c
            _acc_rows(dg_ref, dg, i == 0)

        if act:
            for j in range(bo // MXU_TILE):
                cols = slice(j * MXU_TILE, (j + 1) * MXU_TILE)
                prod = lax.dot_general(df_s[...], w_ref[cols, :], _NT, preferred_element_type=F32)
                val, grad = _silu_and_grad(refs[4][:, cols])
                prod = prod.astype(BF)
                refs[6 + n_act][:, cols] = prod * refs[5][:, cols] * grad
                refs[7 + n_act][:, cols] = prod * val
            acc = refs[-1]
            part = lax.dot_general(refs[6][...], df_s[...], _TN, preferred_element_type=F32)

            @pl.when(i == 0)
            def _():
                acc[b] = part

            @pl.when(i != 0)
            def _():
                acc[b] += part

            @pl.when(i == n_i - 1)
            def _():
                refs[8 + n_act][...] = acc[b].astype(BF)
        else:
            refs[6][...] = lax.dot_general(df_s[...], w_ref[...], _NT if trans_w else _NN,
                                           preferred_element_type=F32).astype(BF)

    row = pl.BlockSpec((tm, D_MODEL), lambda i, b: (i, 0))
    wide = pl.BlockSpec((tm, bo), lambda i, b: (i, b))
    vec = pl.BlockSpec((1, D_MODEL), lambda i, b: (0, 0))
    out_specs = [row, vec] + [wide] * (2 if act else 1)
    out_shape = [jax.ShapeDtypeStruct((T, D_MODEL), BF), jax.ShapeDtypeStruct((1, D_MODEL), F32)]
    out_shape += [jax.ShapeDtypeStruct((T, nb * bo), BF)] * (2 if act else 1)
    scratch = [pltpu.VMEM((tm, D_MODEL), BF)]
    if act:
        out_specs.append(pl.BlockSpec((None, r, cc), lambda i, b: (jnp.where(i == n_i - 1, b, 0), 0, 0)))
        out_shape.append(jax.ShapeDtypeStruct((nb, r, cc), BF))
        scratch.append(pltpu.VMEM((nb, r, cc), F32))
    return pl.pallas_call(
        body, grid=(n_i, nb),
        in_specs=[ANY] * n_dep + [row, row, vec, pl.BlockSpec((None, r, cc), lambda i, b: (b, 0, 0))] + [wide] * n_act,
        out_specs=out_specs, out_shape=out_shape, scratch_shapes=scratch, name=name,
        compiler_params=_params(2))(*deps, dh, f, g, w3, *(act or ()))


def _dn_prenorm(name, xs, ws, trans_w, dh, h, g, tm=2 * ROW_TILE):
    T = dh.shape[0]
    chained = not isinstance(ws, (list, tuple))
    ws = [ws] if chained else list(ws)
    _, r, cc = ws[0].shape
    bw = cc if trans_w else r
    per_x = xs[0].shape[1] // bw
    nb = per_x * len(xs) if chained else per_x
    tm = _tile(T, tm)
    n_x, n_w = len(xs), len(ws)

    def body(*refs):
        x_refs, w_refs = refs[:n_x], refs[n_x:n_x + n_w]
        dh_ref, h_ref, g_ref, o_ref, dg_ref, acc = refs[n_x + n_w:]
        i, b = pl.program_id(0), pl.program_id(1)

        @pl.when(b == 0)
        def _():
            acc[...] = jnp.zeros_like(acc)

        def add(x_ref, w_ref):
            acc[...] += lax.dot_general(x_ref[...], w_ref[...], _NT if trans_w else _NN, preferred_element_type=F32)

        if chained:
            for k, x_ref in enumerate(x_refs):
                pl.when(b // per_x == k)(lambda x_ref=x_ref: add(x_ref, w_refs[0]))
        else:
            for x_ref, w_ref in zip(x_refs, w_refs):
                add(x_ref, w_ref)

        @pl.when(b == nb - 1)
        def _():
            dg = jnp.zeros((1, D_MODEL), F32)
            for c in range(tm // EPI_ROWS):
                rows = slice(c * EPI_ROWS, (c + 1) * EPI_ROWS)
                dx, dg_c = _rms_bwd(h_ref[rows, :], g_ref[...], acc[rows, :])
                o_ref[rows, :] = dh_ref[rows, :] + dx
                dg = dg + dg_c
            _acc_rows(dg_ref, dg, i == 0)

    row = pl.BlockSpec((tm, D_MODEL), lambda i, b: (i, 0))
    vec = pl.BlockSpec((1, D_MODEL), lambda i, b: (0, 0))
    if chained:
        x_specs = [pl.BlockSpec((tm, bw), lambda i, b, k=k: (i, jnp.clip(b - k * per_x, 0, per_x - 1)))
                   for k in range(n_x)]
    else:
        x_specs = [pl.BlockSpec((tm, bw), lambda i, b: (i, b))] * n_x
    return pl.pallas_call(
        body, grid=(T // tm, nb),
        in_specs=x_specs + [pl.BlockSpec((None, r, cc), lambda i, b: (b, 0, 0))] * n_w + [row, row, vec],
        out_specs=[row, vec],
        out_shape=[jax.ShapeDtypeStruct((T, D_MODEL), F32), jax.ShapeDtypeStruct((1, D_MODEL), F32)],
        scratch_shapes=[pltpu.VMEM((tm, D_MODEL), F32)], name=name,
        compiler_params=_params(2))(*xs, *ws, dh, h, g)


def _pool_apply(x, win, row):
    s, k = x, 1
    while k < win:
        s = s + jnp.where(row >= k, pltpu.roll(s, k, 0), 0.0)
        k *= 2
    return s / jnp.minimum(row + 1, win).astype(F32) - x


def _pool_apply_t(dp, win, row):
    T = dp.shape[0]
    s, k = dp / jnp.minimum(row + 1, win).astype(F32), 1
    while k < win:
        s = s + jnp.where(row < T - k, pltpu.roll(s, T - k, 0), 0.0)
        k *= 2
    return s - dp


def _pool_fwd(z, w, scale):
    T = z.shape[0]

    def body(z_ref, w_ref, s_ref, o_ref):
        row = lax.broadcasted_iota(jnp.int32, (T, LANES), 0)
        for gi, win in enumerate(POOL_WINDOWS):
            cols = pl.ds(gi * LANES, LANES)
            pooled = _pool_apply(z_ref[:, cols].astype(F32), win, row)
            y = jnp.dot(pooled.astype(BF), w_ref[gi].astype(BF), preferred_element_type=F32)
            o_ref[:, cols] = (y * s_ref[:, cols]).astype(o_ref.dtype)

    return pl.pallas_call(
        body, grid=(1,),
        in_specs=[pl.BlockSpec((T, BRANCH), lambda i: (0, ZB_POOL)), pl.BlockSpec(w.shape, lambda i: (0, 0, 0)),
                  pl.BlockSpec(scale.shape, lambda i: (0, 0))],
        out_specs=pl.BlockSpec((T, BRANCH), lambda i: (0, 0)), out_shape=jax.ShapeDtypeStruct((T, BRANCH), BF),
        name="pool_fwd", compiler_params=_params(1))(z, w, scale)


def _pool_bwd(dr, z, w, scale):
    T = z.shape[0]

    def body(dr_ref, z_ref, w_ref, s_ref, dz_ref, dw_ref, ds_ref):
        row = lax.broadcasted_iota(jnp.int32, (T, LANES), 0)
        for gi, win in enumerate(POOL_WINDOWS):
            cols = pl.ds(gi * LANES, LANES)
            pooled = _pool_apply(z_ref[:, cols].astype(F32), win, row).astype(BF)
            wg = w_ref[gi].astype(BF)
            y = jnp.dot(pooled, wg, preferred_element_type=F32)
            d = dr_ref[:, cols].astype(F32)
            ds_ref[:, cols] = jnp.sum(d * y, axis=0, keepdims=True)
            dy = (d * s_ref[:, cols]).astype(BF)
            dw_ref[gi] = lax.dot_general(pooled, dy, _TN, preferred_element_type=F32)
            dpooled = lax.dot_general(dy, wg, _NT, preferred_element_type=F32)
            dz_ref[:, cols] = _pool_apply_t(dpooled, win, row).astype(dz_ref.dtype)

    return pl.pallas_call(
        body, grid=(1,),
        in_specs=[pl.BlockSpec((T, BRANCH), lambda i: (0, 0)), pl.BlockSpec((T, BRANCH), lambda i: (0, ZB_POOL)),
                  pl.BlockSpec(w.shape, lambda i: (0, 0, 0)), pl.BlockSpec(scale.shape, lambda i: (0, 0))],
        out_specs=[pl.BlockSpec((T, BRANCH), lambda i: (0, 0)), pl.BlockSpec(w.shape, lambda i: (0, 0, 0)),
                   pl.BlockSpec(scale.shape, lambda i: (0, 0))],
        out_shape=[jax.ShapeDtypeStruct((T, BRANCH), BF), jax.ShapeDtypeStruct(w.shape, F32),
                   jax.ShapeDtypeStruct(scale.shape, F32)],
        name="pool_bwd", compiler_params=_params(1))(dr, z, w, scale)


def _tril(transposed=False):
    r = lax.broadcasted_iota(jnp.int32, (CHUNK, CHUNK), 0)
    c = lax.broadcasted_iota(jnp.int32, (CHUNK, CHUNK), 1)
    return c >= r if transposed else r >= c


def _sgu_fwd(z, ln_g, ln_b, w_s, bias):
    T = z.shape[0]
    tm = _tile(T)

    def body(zu_ref, zv_ref, g_ref, b_ref, w_ref, bias_ref, o_ref):
        gu, _ = _gelu_and_grad(zu_ref[...].astype(F32))
        gv, _ = _gelu_and_grad(zv_ref[...].astype(F32))
        xh, _ = _ln_stats(gv)
        v16 = (xh * g_ref[...] + b_ref[...]).astype(BF)
        tri = _tril()
        for h in range(SGU_HEADS):
            cols = slice(h * LANES, (h + 1) * LANES)
            wh = jnp.where(tri, w_ref[h], 0.0).astype(BF)
            for c in range(tm // CHUNK):
                rows = slice(c * CHUNK, (c + 1) * CHUNK)
                s = jnp.dot(wh, v16[rows, cols], preferred_element_type=F32) + bias_ref[:, cols]
                o_ref[rows, cols] = (gu[rows, cols] * s).astype(o_ref.dtype)

    small = [pl.BlockSpec(a.shape, lambda i, n=a.ndim: (0,) * n) for a in (ln_g, ln_b, w_s, bias)]
    return pl.pallas_call(
        body, grid=(T // tm,),
        in_specs=[pl.BlockSpec((tm, BRANCH), lambda i: (i, ZB_U)), pl.BlockSpec((tm, BRANCH), lambda i: (i, ZB_V))] + small,
        out_specs=pl.BlockSpec((tm, BRANCH), lambda i: (i, 0)), out_shape=jax.ShapeDtypeStruct((T, BRANCH), BF),
        name="sgu_fwd", compiler_params=_params(1))(z, z, ln_g, ln_b, w_s, bias)


def _sgu_bwd(dr, z, ln_g, ln_b, w_s, w_st, bias):
    T = z.shape[0]
    tm = _tile(T)
    n_steps = T // tm

    def body(dr_ref, zu_ref, zv_ref, g_ref, b_ref, w_ref, wt_ref, bias_ref,
             dzu_ref, dzv_ref, dg_ref, db_ref, dw_ref, dbias_ref, dgu_s, dv_s):
        i = pl.program_id(0)

        @pl.when(i == 0)
        def _():
            dg_ref[...] = jnp.zeros_like(dg_ref)
            db_ref[...] = jnp.zeros_like(db_ref)
            dw_ref[...] = jnp.zeros_like(dw_ref)
            dbias_ref[...] = jnp.zeros_like(dbias_ref)

        zu = zu_ref[...].astype(F32)
        zv = zv_ref[...].astype(F32)
        gu, gu_grad = _gelu_and_grad(zu)
        gv, gv_grad = _gelu_and_grad(zv)
        xh, r = _ln_stats(gv)
        v16 = (xh * g_ref[...] + b_ref[...]).astype(BF)
        dr = dr_ref[...].astype(F32)
        tri = _tril()
        for h in range(SGU_HEADS):
            cols = slice(h * LANES, (h + 1) * LANES)
            wh = jnp.where(tri, w_ref[h], 0.0).astype(BF)
            wht = jnp.where(_tril(transposed=True), wt_ref[h], 0.0).astype(BF)
            for c in range(tm // CHUNK):
                rows = slice(c * CHUNK, (c + 1) * CHUNK)
                v_blk = v16[rows, cols]
                s = jnp.dot(wh, v_blk, preferred_element_type=F32) + bias_ref[:, cols]
                ds = dr[rows, cols] * gu[rows, cols]
                dgu_s[rows, cols] = dr[rows, cols] * s
                ds16 = ds.astype(BF)
                dw_ref[h] += jnp.where(tri, lax.dot_general(ds16, v_blk, _NT, preferred_element_type=F32), 0.0)
                dv_s[rows, cols] = jnp.dot(wht, ds16, preferred_element_type=F32)
                dbias_ref[:, cols] += ds
        dzu_ref[...] = (dgu_s[...] * gu_grad).astype(dzu_ref.dtype)
        dgv, dg, db = _ln_bwd(xh, r, g_ref[...], dv_s[...])
        dzv_ref[...] = (dgv * gv_grad).astype(dzv_ref.dtype)
        dg_ref[...] += dg
        db_ref[...] += db

        @pl.when(i == n_steps - 1)
        def _():
            for h in range(SGU_HEADS):
                cols = slice(h * LANES, (h + 1) * LANES)
                tot = jnp.sum(dbias_ref[:, cols], axis=1, keepdims=True)
                dbias_ref[:, cols] = jnp.broadcast_to(tot, (CHUNK, LANES))

    small = (ln_g, ln_b, w_s, w_st, bias)
    small_specs = [pl.BlockSpec(a.shape, lambda i, n=a.ndim: (0,) * n) for a in small]
    return pl.pallas_call(
        body, grid=(n_steps,),
        in_specs=[pl.BlockSpec((tm, BRANCH), lambda i: (i, 0)), pl.BlockSpec((tm, BRANCH), lambda i: (i, ZB_U)),
                  pl.BlockSpec((tm, BRANCH), lambda i: (i, ZB_V))] + small_specs,
        out_specs=[pl.BlockSpec((tm, BRANCH), lambda i: (i, 0)), pl.BlockSpec((tm, BRANCH), lambda i: (i, 0)),
                   pl.BlockSpec((1, BRANCH), lambda i: (0, 0)), pl.BlockSpec((1, BRANCH), lambda i: (0, 0)),
                   pl.BlockSpec(w_s.shape, lambda i: (0, 0, 0)), pl.BlockSpec(bias.shape, lambda i: (0, 0))],
        out_shape=[jax.ShapeDtypeStruct((T, BRANCH), BF), jax.ShapeDtypeStruct((T, BRANCH), BF),
                   jax.ShapeDtypeStruct((1, BRANCH), F32), jax.ShapeDtypeStruct((1, BRANCH), F32),
                   jax.ShapeDtypeStruct(w_s.shape, F32), jax.ShapeDtypeStruct(bias.shape, F32)],
        scratch_shapes=[pltpu.VMEM((tm, BRANCH), F32), pltpu.VMEM((tm, BRANCH), F32)],
        name="sgu_bwd", compiler_params=_params(1))(dr, z, z, ln_g, ln_b, w_s, w_st, bias)


def _conv_fwd(z, convk, l, bias):
    T = z.shape[0]

    def body(za_ref, zb_ref, k_ref, b_ref, o_ref):
        xg = za_ref[...].astype(F32) * _sigmoid(zb_ref[...].astype(F32))
        xp = jnp.concatenate([jnp.zeros((CONV_PAD, LANES), F32), xg], axis=0)
        kw = k_ref[...]
        acc = jnp.broadcast_to(b_ref[...], (T, LANES))
        for s in range(SUBLANES):
            xs = xp if s == 0 else pltpu.roll(xp, s, 0)
            for q in range(CONV_PAD // SUBLANES):
                k = CONV_TAPS - 1 - (SUBLANES * q + s)
                if k >= 0:
                    lo = CONV_PAD - SUBLANES * q
                    acc = acc + kw[k:k + 1, :] * xs[lo:lo + T, :]
        o_ref[...] = acc.astype(o_ref.dtype)

    return pl.pallas_call(
        body, grid=(4,),
        in_specs=[pl.BlockSpec((T, LANES), lambda g: (0, 4 * ZB_A + g)),
                  pl.BlockSpec((T, LANES), lambda g: (0, 4 * ZB_B + g)),
                  pl.BlockSpec((None, CONV_PAD, LANES), lambda g: (g, 0, 0)),
                  pl.BlockSpec((1, LANES), lambda g: (0, g))],
        out_specs=pl.BlockSpec((T, LANES), lambda g: (0, g)), out_shape=jax.ShapeDtypeStruct((T, BRANCH), BF),
        name="conv_fwd", compiler_params=_params(1))(z, z, convk[l], bias)


def _conv_bwd(dy, z, convk, l):
    T = z.shape[0]

    def body(dy_ref, za_ref, zb_ref, k_ref, dza_ref, dzb_ref, dk_ref, db_ref):
        a = za_ref[...].astype(F32)
        sg = _sigmoid(zb_ref[...].astype(F32))
        d = dy_ref[...].astype(F32)
        kw = k_ref[...]
        xp = jnp.concatenate([jnp.zeros((CONV_PAD, LANES), F32), a * sg], axis=0)
        dp = jnp.concatenate([d, jnp.zeros((CONV_PAD, LANES), F32)], axis=0)
        dxg = jnp.zeros((T, LANES), F32)
        dk_ref[...] = jnp.zeros_like(dk_ref)
        for s in range(SUBLANES):
            xs = xp if s == 0 else pltpu.roll(xp, s, 0)
            ds = dp if s == 0 else pltpu.roll(dp, T + CONV_PAD - s, 0)
            for q in range(CONV_PAD // SUBLANES):
                k = CONV_TAPS - 1 - (SUBLANES * q + s)
                if k >= 0:
                    lo = CONV_PAD - SUBLANES * q
                    dk_ref[k:k + 1, :] = jnp.sum(d * xs[lo:lo + T, :], axis=0, keepdims=True)
                    dxg = dxg + kw[k:k + 1, :] * ds[SUBLANES * q:SUBLANES * q + T, :]
        db_ref[...] = jnp.sum(d, axis=0, keepdims=True)
        dza_ref[...] = (dxg * sg).astype(dza_ref.dtype)
        dzb_ref[...] = (dxg * a * sg * (1.0 - sg)).astype(dzb_ref.dtype)

    col = pl.BlockSpec((T, LANES), lambda g: (0, g))
    return pl.pallas_call(
        body, grid=(4,),
        in_specs=[col, pl.BlockSpec((T, LANES), lambda g: (0, 4 * ZB_A + g)),
                  pl.BlockSpec((T, LANES), lambda g: (0, 4 * ZB_B + g)),
                  pl.BlockSpec((None, CONV_PAD, LANES), lambda g: (g, 0, 0))],
        out_specs=[col, col, pl.BlockSpec((CONV_PAD, LANES), lambda g: (0, g)),
                   pl.BlockSpec((1, LANES), lambda g: (0, g))],
        out_shape=[jax.ShapeDtypeStruct((T, BRANCH), BF), jax.ShapeDtypeStruct((T, BRANCH), BF),
                   jax.ShapeDtypeStruct((CONV_PAD, BRANCH), F32), jax.ShapeDtypeStruct((1, BRANCH), F32)],
        name="conv_bwd", compiler_params=_params(1))(dy, z, z, convk[l])


D = D_MODEL


def _ffn_fwd(l, h, S, W, pre, deps=()):
    n, gp, u, a = _norm_mm("ffn_in", h, S[pre + "_pre_g"], [W[pre + "_w_gate"][l], W[pre + "_w_up"][l]], True,
                           act=True, deps=deps)
    f, out = _mm_res("ffn_out", a, W[pre + "_w_down"][l], h, S[pre + "_post_g"], 0.5)
    return out, dict(h=h, n=n, gp=gp, u=u, a=a, f=f)


def _ffn_bwd(l, dh, sv, S, W, G, SG, pre, deps=()):
    df, SG[pre + "_post_g"], dgp, du, dwd = _resbwd_mm(
        "ffn_bwd_act", dh, sv["f"], S[pre + "_post_g"], 0.5, W[pre + "_w_down"][l], True,
        act=(sv["gp"], sv["u"], sv["a"]), deps=deps, tm=ROW_TILE)
    G[pre + "_w_down"] = G[pre + "_w_down"][:l] + [dwd] + G[pre + "_w_down"][l + 1:]
    G[pre + "_w_gate"] = _mm_tn("ffn_dw_gate", dgp, sv["n"], G[pre + "_w_gate"], l, True)
    G[pre + "_w_up"] = _mm_tn("ffn_dw_up", du, sv["n"], G[pre + "_w_up"], l, True)
    dh_in, SG[pre + "_pre_g"] = _dn_prenorm("ffn_bwd_in", [dgp, du], [W[pre + "_w_gate"][l], W[pre + "_w_up"][l]],
                                            False, dh, sv["h"], S[pre + "_pre_g"])
    return dh_in


def _gates(zg):
    return [_sigmoid(jnp.concatenate([zg[2 * k].astype(F32), zg[2 * k + 1].astype(F32)], axis=1)) for k in range(3)]


def _merge_fwd(z, rs, ws, tm=ROW_TILE):
    T = z.shape[0]
    tm = _tile(T, tm)
    nb, kk, bw = ws[0].shape

    def body(*refs):
        r_refs, g_refs, w_refs, y_refs, m_ref = refs[:3], refs[3:9], refs[9:12], refs[12:15], refs[15]
        for r_ref, w_ref, y_ref in zip(r_refs, w_refs, y_refs):
            for b in range(nb):
                y_ref[:, b * bw:(b + 1) * bw] = jnp.dot(r_ref[...], w_ref[b],
                                                        preferred_element_type=F32).astype(y_ref.dtype)
        g = _gates([q[...] for q in g_refs])
        m_ref[...] = (g[0] * y_refs[0][...].astype(F32) + g[1] * y_refs[1][...].astype(F32)
                      + g[2] * y_refs[2][...].astype(F32)).astype(m_ref.dtype)

    row = pl.BlockSpec((tm, D_MODEL), lambda i: (i, 0))
    return pl.pallas_call(
        body, grid=(T // tm,),
        in_specs=[pl.BlockSpec((tm, kk), lambda i: (i, 0))] * 3
        + [pl.BlockSpec((tm, BRANCH), lambda i, j=j: (i, ZB_GATES + j)) for j in range(6)]
        + [pl.BlockSpec(ws[0].shape, lambda i: (0, 0, 0))] * 3,
        out_specs=[row] * 4, out_shape=[jax.ShapeDtypeStruct((T, D_MODEL), BF)] * 4,
        name="mix_merge", compiler_params=_params(1))(*rs, *[z] * 6, *ws)


def _merge_bwd(dmerged, z, ys, ws, tm=ROW_TILE // 2):
    T = z.shape[0]
    tm = _tile(T, tm)
    nb, kk, bw = ws[0].shape

    def body(*refs):
        dm_ref, g_refs, y_refs, w_refs = refs[0], refs[1:7], refs[7:10], refs[10:13]
        dy_refs, lo_ref, hi_ref, dr_refs = refs[13:16], refs[16], refs[17], refs[18:21]
        cut = DZ_HALF - ZB_GATES * BRANCH
        dm = dm_ref[...].astype(F32)
        g = _gates([q[...] for q in g_refs])
        for k in range(3):
            dy_refs[k][...] = (dm * g[k]).astype(BF)
            dzg = (dm * y_refs[k][...].astype(F32) * g[k] * (1.0 - g[k])).astype(BF)
            if k == 0:
                lo_ref[...] = dzg[:, :cut]
                hi_ref[:, :D_MODEL - cut] = dzg[:, cut:]
            else:
                hi_ref[:, k * D_MODEL - cut:(k + 1) * D_MODEL - cut] = dzg
            dr = None
            for b in range(nb):
                p = lax.dot_general(dy_refs[k][:, b * bw:(b + 1) * bw], w_refs[k][b], _NT,
                                    preferred_element_type=F32)
                dr = p if dr is None else dr + p
            dr_refs[k][...] = dr.astype(BF)

    row = pl.BlockSpec((tm, D_MODEL), lambda i: (i, 0))
    return pl.pallas_call(
        body, grid=(T // tm,),
        in_specs=[row] + [pl.BlockSpec((tm, BRANCH), lambda i, j=j: (i, ZB_GATES + j)) for j in range(6)] + [row] * 3
        + [pl.BlockSpec(ws[0].shape, lambda i: (0, 0, 0))] * 3,
        out_specs=[row] * 3 + [pl.BlockSpec((tm, DZ_HALF - ZB_GATES * BRANCH), lambda i: (i, 0)),
                               pl.BlockSpec((tm, DZ_HALF), lambda i: (i, 0))]
        + [pl.BlockSpec((tm, kk), lambda i: (i, 0))] * 3,
        out_shape=[jax.ShapeDtypeStruct((T, D_MODEL), BF)] * 3
        + [jax.ShapeDtypeStruct((T, DZ_HALF - ZB_GATES * BRANCH), BF), jax.ShapeDtypeStruct((T, DZ_HALF), BF)]
        + [jax.ShapeDtypeStruct((T, kk), BF)] * 3,
        name="mix_merge_bwd", compiler_params=_params(1))(dmerged, *[z] * 6, *ys, *ws)


def _mix_fwd(l, h, S, W, deps=()):
    n, z = _norm_mm("mix_in", h, S["mix_pre_g"], [W["w_in"][l]], False, deps=deps)
    r_pool = _pool_fwd(z, S["pool_w"], S["pool_scale"])
    r_sgu = _sgu_fwd(z, S["sgu_ln_g"], S["sgu_ln_b"], S["sgu_w_s"], S["sgu_bias"])
    yc = _conv_fwd(z, W["conv_dw_k"], l, S["conv_dw_b"])

    def ln_silu(y, g, b):
        xh, _ = _ln_stats(y.astype(F32))
        return (_silu_and_grad(xh * g + b)[0],)

    r_conv = _rowwise("conv_ln", ln_silu, [(yc, BRANCH, 0)], [S["conv_ln_g"], S["conv_ln_b"]], [(BRANCH, BF)])[0]
    y_pool, y_sgu, y_conv, merged = _merge_fwd(z, (r_pool, r_sgu, r_conv),
                                               [W["w_%s_out" % br][l] for br in ("pool", "sgu", "conv")])
    o, out = _mm_res("mix_out", merged, W["w_out"][l], h, S["mix_post_g"], 1.0)
    return out, dict(h=h, n=n, z=z, r_pool=r_pool, r_sgu=r_sgu, yc=yc, r_conv=r_conv, y_pool=y_pool, y_sgu=y_sgu,
                     y_conv=y_conv, merged=merged, o=o)


def _branch_dw(rs, dys, shape):
    T, kk = rs[0].shape
    nb, _, bw = shape

    def body(*refs):
        for k in range(3):
            refs[6 + k][...] = lax.dot_general(refs[k][...], refs[3 + k][...], _TN,
                                               preferred_element_type=F32).astype(BF)

    return pl.pallas_call(
        body, grid=(nb,),
        in_specs=[pl.BlockSpec((T, kk), lambda b: (0, 0))] * 3 + [pl.BlockSpec((T, bw), lambda b: (0, b))] * 3,
        out_specs=[pl.BlockSpec((None, kk, bw), lambda b: (b, 0, 0))] * 3,
        out_shape=[jax.ShapeDtypeStruct((nb, kk, bw), BF)] * 3, name="branch_dw",
        compiler_params=_params(1))(*rs, *dys)


def _mix_bwd(l, dh, sv, S, W, G, SG, deps=()):
    z = sv["z"]
    do, SG["mix_post_g"], dmerged = _resbwd_mm("mix_bwd_out", dh, sv["o"], S["mix_post_g"], 1.0,
                                               W["w_out"][l].reshape(1, D, D), True, deps=deps)
    G["w_out"] = _mm_tn("mix_dw_out", sv["merged"], do, G["w_out"], l, True)

    branches = ("pool", "sgu", "conv")
    res = _merge_bwd(dmerged, z, [sv["y_" + br] for br in branches], [W["w_%s_out" % br][l] for br in branches])
    dz_gate_lo, dz_hi, dr = res[3], res[4], dict(zip(branches, res[5:]))
    for br, dw in zip(branches, _branch_dw([sv["r_" + br] for br in branches], res[:3], G["w_pool_out"][l].shape)):
        wn = "w_%s_out" % br
        G[wn] = G[wn][:l] + [dw] + G[wn][l + 1:]
    dz_pool, SG["pool_w"], SG["pool_scale"] = _pool_bwd(dr["pool"], z, S["pool_w"], S["pool_scale"])
    dzu, dzv, SG["sgu_ln_g"], SG["sgu_ln_b"], SG["sgu_w_s"], dbias = _sgu_bwd(
        dr["sgu"], z, S["sgu_ln_g"], S["sgu_ln_b"], S["sgu_w_s"], S["sgu_w_st"], S["sgu_bias"])
    SG["sgu_b_s"] = dbias[:, ::LANES].T

    def ln_silu_bwd(d, y, g, b):
        xh, r = _ln_stats(y.astype(F32))
        _, grad = _silu_and_grad(xh * g + b)
        return _ln_bwd(xh, r, g, d.astype(F32) * grad)

    dyc, SG["conv_ln_g"], SG["conv_ln_b"] = _rowwise(
        "conv_ln_bwd", ln_silu_bwd, [(dr["conv"], BRANCH, 0), (sv["yc"], BRANCH, 0)],
        [S["conv_ln_g"], S["conv_ln_b"]], [(BRANCH, BF)], [(1, BRANCH), (1, BRANCH)])
    dza, dzb, SG["conv_dw_k"], SG["conv_dw_b"] = _conv_bwd(dyc, z, W["conv_dw_k"], l)
    dz_lo = jnp.concatenate([dz_pool, dzu, dzv, dza, dzb, dz_gate_lo], axis=1)
    G["w_in"] = _mm_tn("mix_dw_in", sv["n"], dz_lo, G["w_in"], l, False)
    G["w_in"] = _mm_tn("mix_dw_in", sv["n"], dz_hi, G["w_in"], l, False, first=2)
    dh_in, SG["mix_pre_g"] = _dn_prenorm("mix_bwd_in", [dz_lo, dz_hi], W["w_in"][l], True, dh, sv["h"],
                                         S["mix_pre_g"])
    return dh_in


def _ple_out(h, p, gp, w3, g, tm=ROW_TILE):
    T, kp = p.shape
    nb, _, bw = w3.shape
    tm = _tile(T, tm)

    def body(h_ref, p_ref, gp_ref, w_ref, g_ref, e_ref, o_ref):
        p16 = p_ref[...].astype(BF)
        for b in range(nb):
            e_ref[:, b * bw:(b + 1) * bw] = jnp.dot(p16, w_ref[b], preferred_element_type=F32).astype(BF)
        q = _sigmoid(gp_ref[...].astype(F32)) * e_ref[...].astype(F32)
        o_ref[...] = h_ref[...] + _rms_fwd(q, g_ref[...])

    row = pl.BlockSpec((tm, D_MODEL), lambda i: (i, 0))
    return pl.pallas_call(
        body, grid=(T // tm,),
        in_specs=[row, pl.BlockSpec((tm, kp), lambda i: (i, 0)), row, pl.BlockSpec(w3.shape, lambda i: (0, 0, 0)),
                  pl.BlockSpec(g.shape, lambda i: (0, 0))],
        out_specs=[row, row],
        out_shape=[jax.ShapeDtypeStruct((T, D_MODEL), BF), jax.ShapeDtypeStruct((T, D_MODEL), F32)],
        name="ple_out", compiler_params=_params(1))(h, p, gp, w3, g)


def _ple_fwd(l, h, p_l, S, W, deps=()):
    n, gp = _norm_mm("ple_in", h, S["ple_pre_g"], [W["ple_w_gate"][l].reshape(1, D, D)], False, deps=deps)
    e, out = _ple_out(h, p_l, gp, W["ple_w_proj"][l], S["ple_post_g"])
    return out, dict(h=h, n=n, e=e, gp=gp, p=p_l)


def _ple_bwd_rows(dh, e, gp, g_post, w3, h, g_pre, deps=(), tm=ROW_TILE):
    T = dh.shape[0]
    w2 = w3.reshape(D_MODEL, D_MODEL)
    tm = _tile(T, tm)
    n_dep = len(deps)

    def body(*refs):
        dh_ref, e_ref, gp_ref, gpost_ref, w_ref, h_ref, gpre_ref, de_ref, dgp_ref, o_ref, dpost_ref, dpre_ref = \
            refs[n_dep:]
        first = pl.program_id(0) == 0
        d = dh_ref[...]
        sg = _sigmoid(gp_ref[...].astype(F32))
        ee = e_ref[...].astype(F32)
        dq, dpost = _rms_bwd(sg * ee, gpost_ref[...], d)
        de_ref[...] = (dq * sg).astype(BF)
        dgp = (dq * ee * sg * (1.0 - sg)).astype(BF)
        dgp_ref[...] = dgp
        dn = lax.dot_general(dgp, w_ref[...], _NT, preferred_element_type=F32)
        dx, dpre = _rms_bwd(h_ref[...], gpre_ref[...], dn)
        o_ref[...] = d + dx
        _acc_rows(dpost_ref, dpost, first)
        _acc_rows(dpre_ref, dpre, first)

    row = pl.BlockSpec((tm, D_MODEL), lambda i: (i, 0))
    vec = pl.BlockSpec((1, D_MODEL), lambda i: (0, 0))
    return pl.pallas_call(
        body, grid=(T // tm,),
        in_specs=[ANY] * n_dep + [row, row, row, vec, pl.BlockSpec(w2.shape, lambda i: (0, 0)), row, vec],
        out_specs=[row, row, row, vec, vec],
        out_shape=[jax.ShapeDtypeStruct((T, D_MODEL), BF)] * 2 + [jax.ShapeDtypeStruct((T, D_MODEL), F32)]
        + [jax.ShapeDtypeStruct((1, D_MODEL), F32)] * 2,
        name="ple_bwd", compiler_params=_params(1))(*deps, dh, e, gp, g_post, w2, h, g_pre)


def _ple_bwd(l, dh, sv, S, W, G, SG, deps=()):
    de, dgp, dh_in, SG["ple_post_g"], SG["ple_pre_g"] = _ple_bwd_rows(
        dh, sv["e"], sv["gp"], S["ple_post_g"], W["ple_w_gate"][l], sv["h"], S["ple_pre_g"], deps)
    G["ple_w_proj"] = _mm_tn("ple_dw_proj", sv["p"], de, G["ple_w_proj"], l, False)
    G["ple_w_gate"] = _mm_tn("ple_dw_gate", sv["n"], dgp, G["ple_w_gate"], l, True)
    return dh_in


def _layer_small(a, l):
    S = {}
    for name in SMALL:
        v = a[name][l]
        S[name] = v.reshape(1, -1) if v.ndim == 1 else v
    S["sgu_w_st"] = jnp.swapaxes(S["sgu_w_s"], 1, 2)
    S["sgu_bias"] = jnp.repeat(S["sgu_b_s"].T, LANES, axis=1)
    return S


def _layer_fwd(l, h, p_l, S, W, deps=(), hooks=None):
    hooks = hooks or {}

    def after(part, hv):
        return hooks[part](hv) if part in hooks else ()

    h, sv1 = _ffn_fwd(l, h, S, W, "ffn1", deps)
    h, sv2 = _mix_fwd(l, h, S, W, after("ffn1", h))
    h, sv3 = _ffn_fwd(l, h, S, W, "ffn2", after("mix", h))
    h, sv4 = _ple_fwd(l, h, p_l, S, W, after("ffn2", h))
    return h, (sv1, sv2, sv3, sv4)


def _layer_bwd(l, dh, sv, S, W, G, deps=(), hooks=None):
    hooks = hooks or {}

    def after(part, dv):
        return hooks[part](dv) if part in hooks else ()

    SG = {}
    dh = _ple_bwd(l, dh, sv[3], S, W, G, SG, deps)
    dh = _ffn_bwd(l, dh, sv[2], S, W, G, SG, "ffn2")
    dh = _mix_bwd(l, dh, sv[1], S, W, G, SG, after("ffn2", dh))
    dh = _ffn_bwd(l, dh, sv[0], S, W, G, SG, "ffn1", after("mix", dh))
    return dh, SG


HBM = pl.BlockSpec(memory_space=pltpu.HBM)
SEM = pl.BlockSpec(memory_space=pltpu.SEMAPHORE)
SIDE_EFFECT = pltpu.SideEffectType.DATAFLOW_SIDE_EFFECTING


def _place():
    x, y, c = lax.axis_index("x"), lax.axis_index("y"), lax.axis_index("c")
    chips = [(1 - x, y), (x, 1 - y), (1 - x, 1 - y)]
    return x, y, c, chips


def _remote(src, dst, send_sem, recv_sem, to):
    return pltpu.make_async_remote_copy(src_ref=src, dst_ref=dst, send_sem=send_sem, recv_sem=recv_sem,
                                        device_id=to, device_id_type=MESH)


def _split_start(name, plan, bufs, deps):
    count, fn = plan
    n, nd = len(bufs), len(deps)

    def body(*refs):
        send, recv = refs[nd + n], refs[nd + n + 1]
        x, y, c, chips = _place()
        for k, (src, dst, _, to) in enumerate(fn(refs[nd:nd + n], x, y, c, chips)):
            _remote(src, dst, send.at[k], recv.at[k], to).start()
        refs[-1][...] = jnp.zeros_like(refs[-1])

    res = pl.pallas_call(
        body, in_specs=[ANY] * nd + [HBM] * n,
        out_specs=[SEM, SEM] + [HBM] * n + [pl.BlockSpec(memory_space=pltpu.VMEM)],
        out_shape=[pltpu.SemaphoreType.DMA((count,)), pltpu.SemaphoreType.DMA((count,))]
        + [pltpu.HBM(b.shape, b.dtype) for b in bufs] + [jax.ShapeDtypeStruct((8, LANES), F32)],
        input_output_aliases={nd + i: 2 + i for i in range(n)}, name=name,
        compiler_params=pltpu.CompilerParams(has_side_effects=SIDE_EFFECT),
    )(*deps, *[pltpu.with_memory_space_constraint(b, pltpu.HBM) for b in bufs])
    return (res[0], res[1]), list(res[2:2 + n]), res[-1]


def _split_wait(name, plan, sems, bufs, after):
    _, fn = plan
    n = len(bufs)

    def body(*refs):
        send, recv = refs[n], refs[n + 1]
        x, y, c, chips = _place()
        for k, (src, _, land, to) in enumerate(fn(refs[:n], x, y, c, chips)):
            cp = _remote(src, land, send.at[k], recv.at[k], to)
            cp.wait_send()
            cp.wait_recv()

    res = pl.pallas_call(
        body, in_specs=[HBM] * n + [SEM, SEM] + [ANY] * len(after), out_specs=[HBM] * n,
        out_shape=[pltpu.HBM(b.shape, b.dtype) for b in bufs], input_output_aliases={i: i for i in range(n)},
        name=name, compiler_params=pltpu.CompilerParams(has_side_effects=SIDE_EFFECT))(*bufs, *sems, *after)
    return list(res)


def _gather_plans(n):
    def across(b, x, y, c, chips):
        me, out = 2 * x + y, []
        for a in range(n):
            rh = b[a].shape[1] // 2
            mine = b[a].at[me, pl.ds(c * rh, rh)]
            for cx, cy in chips:
                out.append((mine, mine, b[a].at[2 * cx + cy, pl.ds(c * rh, rh)], (cx, cy, c)))
        return out

    def to_sibling(b, x, y, c, chips):
        out = []
        for a in range(n):
            rh = b[a].shape[1] // 2
            for cx, cy in chips:
                piece = b[a].at[2 * cx + cy, pl.ds(c * rh, rh)]
                out.append((piece, piece, b[a].at[2 * cx + cy, pl.ds((1 - c) * rh, rh)], (x, y, 1 - c)))
        return out

    return (3 * n, across), (3 * n, to_sibling)


def _pair_plan(n):
    def fn(b, x, y, c, chips):
        out = []
        for a in range(n):
            rh = b[a].shape[1] // 2
            out.append((b[a].at[:, pl.ds((1 - c) * rh, rh)], b[n + a], b[n + a], (x, y, 1 - c)))
        return out

    return n, fn


def _cross_plan(n):
    def fn(b, x, y, c, chips):
        out = []
        for a in range(n):
            for j, (cx, cy) in enumerate(chips):
                out.append((b[a].at[2 * cx + cy], b[n + a].at[j], b[n + a].at[j], (cx, cy, c)))
        return out

    return 3 * n, fn


def _share_plan(n, l):
    def fn(b, x, y, c, chips):
        out = []
        for a in range(n):
            rh = b[a].shape[1] // 2
            mine = b[a].at[l, pl.ds(c * rh, rh)]
            out.append((mine, mine, b[a].at[l, pl.ds((1 - c) * rh, rh)], (x, y, 1 - c)))
        return out

    return n, fn


def _peers(x, y, c):
    return [(1 - x if m & 4 else x, 1 - y if m & 2 else y, 1 - c if m & 1 else c) for m in range(1, 8)]


def _small_plans():
    def scatter(b, x, y, c, chips):
        return [(b[0].at[4 * px + 2 * py + pc], b[1].at[m], b[1].at[m], (px, py, pc))
                for m, (px, py, pc) in enumerate(_peers(x, y, c))]

    def gather(b, x, y, c, chips):
        mine = b[0].at[4 * x + 2 * y + c]
        return [(mine, mine, b[0].at[4 * px + 2 * py + pc], (px, py, pc)) for px, py, pc in _peers(x, y, c)]

    return (7, scatter), (7, gather)


def _sum_small(v3, got, pos):
    rs = v3.shape[1]
    tm = _tile(rs)
    ins = [(v3, (None, tm, LANES), lambda i, p: (p[2], i, 0))]
    ins += [(got, (None, tm, LANES), lambda i, p, m=m: (m, i, 0)) for m in range(7)]
    return _tiled("sum_small", lambda *t: (((((((t[0] + t[1]) + t[2]) + t[3]) + t[4]) + t[5]) + t[6]) + t[7],),
                  (rs // tm,), pos, ins, [((8, rs, LANES), F32, (None, tm, LANES), lambda i, p: (p[2], i, 0))])[0]


ADD_ROWS = 256


def _multi_tiled(name, fn, pos, groups, in_place=False):
    steps = max(g[2] for g in groups)
    flat_in, in_specs, out_specs, out_shape, counts, dests = [], [], [], [], [], []
    for ins, rows, n_t, (shape, dtype, oidx, dest) in groups:
        for arr, idx in ins:
            flat_in.append(arr)
            in_specs.append(pl.BlockSpec((rows, arr.shape[1]),
                                         lambda i, p, idx=idx, n_t=n_t: (idx(jnp.minimum(i, n_t - 1), p), 0)))
        out_specs.append(pl.BlockSpec((rows, shape[1]),
                                      lambda i, p, oidx=oidx, n_t=n_t: (oidx(jnp.minimum(i, n_t - 1), p), 0)))
        out_shape.append(jax.ShapeDtypeStruct(shape, dtype))
        counts.append((len(ins), n_t))
        dests.append(dest)
    n_in = len(flat_in)
    extra = dests if in_place else []

    def body(_, *refs):
        outs = refs[n_in + len(extra):]
        k = 0
        for (n_a, n_t), o_ref in zip(counts, outs):
            tiles = refs[k:k + n_a]
            k += n_a

            @pl.when(pl.program_id(0) < n_t)
            def _(tiles=tiles, o_ref=o_ref):
                o_ref[...] = fn(*[t[...] for t in tiles]).astype(o_ref.dtype)

    spec = pltpu.PrefetchScalarGridSpec(num_scalar_prefetch=1, grid=(steps,),
                                        in_specs=in_specs + [ANY] * len(extra), out_specs=out_specs)
    return pl.pallas_call(body, grid_spec=spec, out_shape=out_shape,
                          input_output_aliases={1 + n_in + k: k for k in range(len(extra))}, name=name,
                          compiler_params=_params(1))(pos, *flat_in, *extra)


def _add_pair(grads, got, pos):
    groups = []
    for g, q in zip(grads, got):
        nb, R, C = g.shape
        rh = R // 2
        rows = _tile(rh, ADD_ROWS)
        nh = rh // rows
        groups.append(([(g.reshape(nb * R, C), lambda t, p, nh=nh: (t // nh) * 2 * nh + p[1] * nh + t % nh),
                        (q.reshape(nb * rh, C), lambda t, p: t)], rows, nb * nh,
                       ((nb * rh, C), BF, lambda t, p: t, None)))
    res = _multi_tiled("rs_add_pair", lambda u, w: u.astype(F32) + w.astype(F32), pos, groups)
    return [t.reshape(q.shape) for t, q in zip(res, got)]


def _add_chips(parts, slots, reduced, l, pos):
    def add(own, s0, s1, s2):
        return ((own.astype(F32) + s0.astype(F32)) + s1.astype(F32)) + s2.astype(F32)

    groups = []
    for t, s, red in zip(parts, slots, reduced):
        nb, rh, C = t.shape
        L = red.shape[0]
        rows = _tile(rh, ADD_ROWS)
        nh = rh // rows
        ins = [(t.reshape(nb * rh, C), lambda i, p, nh=nh: p[0] * nh + i)]
        ins += [(s.reshape(3 * rh, C), lambda i, p, j=j, nh=nh: j * nh + i) for j in range(3)]
        groups.append((ins, rows, nh, ((L * 2 * rh, C), F32, lambda i, p, nh=nh: l * 2 * nh + p[1] * nh + i,
                                 red.reshape(L * 2 * rh, C))))
    res = _multi_tiled("rs_add_chips", add, pos, groups, in_place=True)
    return [buf.reshape(red.shape) for buf, red in zip(res, reduced)]


def _adamw_math(w, g, m, v):
    m = ADAM_B1 * m + (1.0 - ADAM_B1) * g
    v = ADAM_B2 * v + (1.0 - ADAM_B2) * (g * g)
    m_hat = m / (1.0 - ADAM_B1 ** ADAM_STEP)
    v_hat = v / (1.0 - ADAM_B2 ** ADAM_STEP)
    return -ADAM_LR * (m_hat / (jnp.sqrt(v_hat) + ADAM_EPS) + ADAM_WD * w), m, v


def _adamw(w, g, m, v, lo=0, hi=None, into=None, deps=()):
    L, R, C = w.shape
    hi = L if hi is None else hi
    tr = _tile(R, max(16, ADAM_TILE_ELEMS // C))
    extra = (list(into) if into else []) + list(deps)
    n_alias = 4 if into else 0

    def body(w_ref, g_ref, m_ref, v_ref, *rest):
        go_ref, d_ref, mo_ref, vo_ref = rest[len(extra):]
        gv = g_ref[...]
        d, mn, vn = _adamw_math(w_ref[...], gv, m_ref[...], v_ref[...])
        go_ref[...] = gv
        d_ref[...] = d
        mo_ref[...] = mn
        vo_ref[...] = vn

    spec = pl.BlockSpec((None, tr, C), lambda l, i: (l + lo, i, 0))
    out = jax.ShapeDtypeStruct(w.shape, F32)
    return pl.pallas_call(body, grid=(hi - lo, R // tr), in_specs=[spec] * 4 + [ANY] * len(extra),
                          out_specs=[spec] * 4, out_shape=[out] * 4,
                          input_output_aliases={4 + k: k for k in range(n_alias)}, name="adamw",
                          compiler_params=_params(2))(w, g, m, v, *extra)


def _pack(parts):
    flat = jnp.concatenate([q.reshape(-1, LANES) for q in parts], axis=0)
    return jnp.pad(flat, ((0, -flat.shape[0] % ROW_TILE), (0, 0)))


def _unpack(flat, like):
    out, r = [], 0
    for q in like:
        n = q.size // LANES
        out.append(flat[r:r + n].reshape(q.shape))
        r += n
    return out


def _train_step(a):
    a = dict(a)
    L = a["ffn1_pre_g"].shape[0]
    x, y, c, _ = _place()
    chip = 2 * x + y
    pos = jnp.stack([chip, c, 2 * chip + c]).astype(jnp.int32)
    for name in TRANSPOSED:
        for pre in ("", "m_", "v_"):
            a[pre + name] = jnp.swapaxes(a[pre + name], 1, 2)
    big = [b[0] for b in BIG]
    gathered = big + ["conv_dw_k"]
    n_w, n_g = len(gathered), len(big)

    own = [None] * n_w
    W = {name: [None] * L for name in gathered}
    every = list(range(n_w))
    first, mixer, later = every[:3], every[3:8] + [n_g], every[8:n_g]
    rest = mixer + later

    def cast(i, deps):
        if i == n_g:
            taps = a["conv_dw_k"].reshape(L, CONV_TAPS, LANES)
            return _cast_layers("pad_conv_taps", taps, CONV_PAD, LANES, F32, pos, deps)
        name, _, _, _, rp, cp = BIG[i]
        return _cast_layers("cast_weight", a[name], rp, cp, BF, pos, deps)

    def gather_first(l, ids, tag, deps):
        return _split_start("gather_a%d%s" % (l, tag), _gather_plans(len(ids))[0], [own[i][l] for i in ids], deps)

    def gather_second(l, ids, tag, state, after):
        across, to_sibling = _gather_plans(len(ids))
        bufs = _split_wait("gather_a%d%s_done" % (l, tag), across, state[0], state[1], after)
        return _split_start("gather_b%d%s" % (l, tag), to_sibling, bufs, [])

    def gather_done(l, ids, tag, state, after):
        to_sibling = _gather_plans(len(ids))[1]
        bufs = _split_wait("gather_b%d%s_done" % (l, tag), to_sibling, state[0], state[1], after)
        for i, buf in zip(ids, bufs):
            W[gathered[i]][l] = buf

    for i in first:
        own[i] = cast(i, ())
    state = gather_first(0, first, "f", [])
    for i in rest:
        own[i] = cast(i, (state[2],))
    state = gather_second(0, first, "f", state, [own[i][0] for i in rest])
    gather_done(0, first, "f", state, [])

    parts = {"f": first, "m": mixer, "t": later}
    flying = {}

    def begin(l, part, deps):
        flying[l, part] = gather_first(l, parts[part], part, deps)
        return flying[l, part][2]

    def hand_on(l, part, after):
        flying[l, part] = gather_second(l, parts[part], part, flying[l, part], after)
        return flying[l, part][2]

    def arrive(l, part, after):
        gather_done(l, parts[part], part, flying.pop((l, part)), after)

    def hooks_of(l):
        nxt = l + 1 < L

        def after_ffn1(hv):
            tokens = []
            if l == 0:
                hand_on(0, "m", [hv])
            arrive(l, "m", [hv])
            if l == 0:
                tokens.append(begin(0, "t", [hv]))
            else:
                tokens.append(hand_on(l, "t", [hv]))
            if nxt:
                tokens.append(begin(l + 1, "f", tokens[-1:]))
            return tuple(tokens)

        def after_mix(hv):
            tokens = []
            if l == 0:
                hand_on(0, "t", [hv])
            arrive(l, "t", [hv])
            if nxt:
                tokens.append(hand_on(l + 1, "f", [hv]))
                tokens.append(begin(l + 1, "m", tokens[-1:]))
            return tuple(tokens)

        def after_ffn2(hv):
            tokens = []
            if nxt:
                arrive(l + 1, "f", [hv])
                tokens.append(hand_on(l + 1, "m", [hv]))
                tokens.append(begin(l + 1, "t", tokens[-1:]))
            return tuple(tokens)

        return {"ffn1": after_ffn1, "mix": after_mix, "ffn2": after_ffn2}

    small = [_layer_small(a, l) for l in range(L)]
    h, saved = a["x"][0], []
    deps = (begin(0, "m", []),)
    for l in range(L):
        h, sv = _layer_fwd(l, h, a["p"][l, 0], small[l], W, deps, hooks_of(l))
        saved.append(sv)
        deps = ()

    def loss_fn(yv, t):
        e = yv - t
        return e * (1.0 / D), jnp.sum(e * e, axis=0, keepdims=True)

    dh, lsum = _rowwise("loss", loss_fn, [(h, D, 0), (a["loss_target"][0], D, 0)], [], [(D, F32)], [(1, D)])
    loss = lax.psum(0.5 * jnp.sum(lsum) / D, ("x", "y", "c"))

    G = {name: [jax.ShapeDtypeStruct((N_CHIPS, rp, cp), BF)] * L for name, _, _, _, rp, cp in BIG}
    reduced = [lax.empty((L, rp, cp), F32) for _, _, _, _, rp, cp in BIG]
    small_grads = [None] * L
    whole = list(range(n_g))
    piece_a, piece_b, piece_c = whole[8:], whole[3:8], whole[:3]

    def pair_start(l, ids, tag, deps):
        grads = [G[big[i]][l] for i in ids]
        lands = [lax.empty((N_CHIPS, g.shape[1] // 2, g.shape[2]), BF) for g in grads]
        return _split_start("rs_pair%d%s" % (l, tag), _pair_plan(len(ids)), grads + lands, deps)

    def cross_start(l, ids, tag, state, after):
        n = len(ids)
        bufs = _split_wait("rs_pair%d%s_done" % (l, tag), _pair_plan(n), state[0], state[1], after)
        parts = _add_pair(bufs[:n], bufs[n:], pos)
        lands = [lax.empty((3,) + t.shape[1:], BF) for t in parts]
        return _split_start("rs_cross%d%s" % (l, tag), _cross_plan(n), parts + lands, [])

    def cross_finish(l, ids, tag, state, after, reduced):
        n = len(ids)
        bufs = _split_wait("rs_cross%d%s_done" % (l, tag), _cross_plan(n), state[0], state[1], after)
        reduced = list(reduced)
        for i, r in zip(ids, _add_chips(bufs[:n], bufs[n:], [reduced[i] for i in ids], l, pos)):
            reduced[i] = r
        return reduced

    def share_start(l, reduced):
        return _split_start("rs_share%d" % l, _share_plan(n_g, l), reduced, [])

    def share_done(l, state, after):
        return _split_wait("rs_share%d_done" % l, _share_plan(n_g, l), state[0], state[1], after)

    small_names = SMALL + ("conv_dw_k",)
    scatter, gather = _small_plans()
    totals = [None] * L

    def small_scatter(l, deps):
        packed = _pack([small_grads[l][name] for name in small_names])
        v3 = packed.reshape(8, packed.shape[0] // 8, LANES)
        return _split_start("small_scatter%d" % l, scatter, [v3, lax.empty((7,) + v3.shape[1:], F32)], deps)

    def small_gather(l, state, after):
        bufs = _split_wait("small_scatter%d_done" % l, scatter, state[0], state[1], after)
        return _split_start("small_gather%d" % l, gather, [_sum_small(bufs[0], bufs[1], pos)], [])

    def small_done(l, state, after):
        total = _split_wait("small_gather%d_done" % l, gather, state[0], state[1], after)[0]
        totals[l] = total.reshape(-1, LANES)

    st_pair = st_share = st_small = None
    for l in reversed(range(L)):
        deps = tuple(s[2] for s in (st_pair, st_share, st_small) if s is not None)
        box = {}

        def after_ffn2(dm, l=l, box=box, st_pair=st_pair, st_share=st_share, st_small=st_small):
            out = []
            if st_share is not None:
                box["reduced"] = share_done(l + 2, st_share, [dm])
            if st_small is not None:
                box["small"] = small_gather(l + 1, st_small, [dm])
                out.append(box["small"][2])
            if st_pair is not None:
                box["cross"] = cross_start(l + 1, whole, "", st_pair, [dm])
                out.append(box["cross"][2])
            if l == 0:
                box["pair_a"] = pair_start(0, piece_a, "a", [dm])
                out.append(box["pair_a"][2])
            return tuple(out)

        def after_mix(dm, box=box):
            box["cross_a"] = cross_start(0, piece_a, "a", box["pair_a"], [dm])
            box["pair_b"] = pair_start(0, piece_b, "b", [dm])
            return (box["cross_a"][2], box["pair_b"][2])

        hooks = {"ffn2": after_ffn2, "mix": after_mix} if l == 0 else {"ffn2": after_ffn2}
        dh, small_grads[l] = _layer_bwd(l, dh, saved[l], small[l], W, G, deps, hooks)
        if st_share is not None:
            reduced = box["reduced"]
        if "small" in box:
            small_done(l + 1, box["small"], [dh])
        st_share = None
        if "cross" in box:
            reduced = cross_finish(l + 1, whole, "", box["cross"], [dh], reduced)
            st_share = share_start(l + 1, reduced)
        st_small = small_scatter(l, [dh])
        st_pair = pair_start(l, whole, "", [st_small[2]]) if l else None
    grad_x = dh
    cross_b = cross_start(0, piece_b, "b", box["pair_b"], [st_small[2]])
    cross_c = cross_start(0, piece_c, "c", pair_start(0, piece_c, "c", [cross_b[2]]), [])
    if st_share is not None:
        reduced = share_done(1, st_share, [cross_c[2]])
    upper, token = {}, cross_c[2]
    for k, (name, red) in enumerate(zip(big, reduced)):
        if k == 2:
            st_small = small_gather(0, st_small, [token])
            token = st_small[2]
        if L > 1:
            upper[name] = _adamw(a[name], red, a["m_" + name], a["v_" + name], 1, L, deps=[token])
            token = upper[name][1]
    small_done(0, st_small, [token])
    per_layer = [_unpack(totals[l], [small_grads[l][name] for name in small_names]) for l in range(L)]
    summed = {name: jnp.stack([per_layer[l][k] for l in range(L)]) for k, name in enumerate(small_names)}
    done = [r[1] for r in upper.values()] + [summed[small_names[0]]]
    for ids, tag, state in ((piece_a, "a", box["cross_a"]), (piece_b, "b", cross_b), (piece_c, "c", cross_c)):
        reduced = cross_finish(0, ids, tag, state, done, reduced)
    reduced = share_done(0, share_start(0, reduced), [])
    big_grads = dict(zip(big, reduced))

    grads, deltas, new_m, new_v = {}, {}, {}, {}
    for name in big:
        res = _adamw(a[name], big_grads[name], a["m_" + name], a["v_" + name], 0, 1, upper.get(name))
        if name in TRANSPOSED:
            res = [jnp.swapaxes(r, 1, 2) for r in res]
        grads[name], deltas[name], new_m[name], new_v[name] = res
    taps = lax.dynamic_slice_in_dim(summed["conv_dw_k"], chip * LANES, LANES, axis=2)[:, :CONV_TAPS]
    grads["conv_dw_k"] = taps.reshape(a["conv_dw_k"].shape)
    for name in SMALL:
        grads[name] = summed[name].reshape(a[name].shape)
    shapes = [a[name] for name in small_names]
    res = _adamw(*[_pack([a[pre + name] if pre != "g" else grads[name] for name in small_names])[None]
                   for pre in ("", "g", "m_", "v_")])
    for dst, flat in zip((deltas, new_m, new_v), res[1:]):
        for name, val in zip(small_names, _unpack(flat[0], shapes)):
            dst[name] = val

    return (loss, grad_x[None], *[grads[n] for n in WEIGHTS], *[deltas[n] for n in WEIGHTS],
            *[new_m[n] for n in WEIGHTS], *[new_v[n] for n in WEIGHTS])


def kernel(x, p, ffn1_pre_g, ffn1_w_gate, ffn1_w_up, ffn1_w_down, ffn1_post_g, mix_pre_g, w_in, pool_w, pool_scale, w_pool_out, sgu_ln_g, sgu_ln_b, sgu_w_s, sgu_b_s, w_sgu_out, conv_dw_k, conv_dw_b, conv_ln_g, conv_ln_b, w_conv_out, w_out, mix_post_g, ffn2_pre_g, ffn2_w_gate, ffn2_w_up, ffn2_w_down, ffn2_post_g, ple_w_proj, ple_pre_g, ple_w_gate, ple_post_g, loss_target, m_ffn1_pre_g, m_ffn1_w_gate, m_ffn1_w_up, m_ffn1_w_down, m_ffn1_post_g, m_mix_pre_g, m_w_in, m_pool_w, m_pool_scale, m_w_pool_out, m_sgu_ln_g, m_sgu_ln_b, m_sgu_w_s, m_sgu_b_s, m_w_sgu_out, m_conv_dw_k, m_conv_dw_b, m_conv_ln_g, m_conv_ln_b, m_w_conv_out, m_w_out, m_mix_post_g, m_ffn2_pre_g, m_ffn2_w_gate, m_ffn2_w_up, m_ffn2_w_down, m_ffn2_post_g, m_ple_w_proj, m_ple_pre_g, m_ple_w_gate, m_ple_post_g, v_ffn1_pre_g, v_ffn1_w_gate, v_ffn1_w_up, v_ffn1_w_down, v_ffn1_post_g, v_mix_pre_g, v_w_in, v_pool_w, v_pool_scale, v_w_pool_out, v_sgu_ln_g, v_sgu_ln_b, v_sgu_w_s, v_sgu_b_s, v_w_sgu_out, v_conv_dw_k, v_conv_dw_b, v_conv_ln_g, v_conv_ln_b, v_w_conv_out, v_w_out, v_mix_post_g, v_ffn2_pre_g, v_ffn2_w_gate, v_ffn2_w_up, v_ffn2_w_down, v_ffn2_post_g, v_ple_w_proj, v_ple_pre_g, v_ple_w_gate, v_ple_post_g):
    return _train_step(dict(locals()))
```

```python
import math

import jax
import jax.numpy as jnp
from jax import lax
from jax.experimental import pallas as pl
from jax.experimental.pallas import tpu as pltpu

BF = jnp.bfloat16
F32 = jnp.float32
EPS = 1e-6
D_MODEL = 1024
LANES = 128
SUBLANES = 8
MXU_TILE = 256
N_CHIPS = 4
FFN_SHARD = 704
FFN_SHARD_PAD = 768
POOL_WINDOWS = (2, 4, 8, 16)
SGU_HEADS = 4
CHUNK = 128
CONV_TAPS = 31
CONV_PAD = 32
ROW_TILE = 512
EPI_ROWS = 256
VMEM_LIMIT_BYTES = 56 * 1024 * 1024
ADAM_TILE_ELEMS = 4 * 128 * 1024
ADAM_LR, ADAM_B1, ADAM_B2, ADAM_EPS, ADAM_WD, ADAM_STEP =0.001, 0.9, 0.999, 1e-08, 0.01, 10
MESH = pl.DeviceIdType.MESH
ANY = pl.BlockSpec(memory_space=pl.ANY)

BRANCH = 512
ZB_POOL, ZB_U, ZB_V, ZB_A, ZB_B, ZB_GATES = 0, 1, 2, 3, 4, 5
DZ_HALF = 2816

TRANSPOSED = ("ffn1_w_gate", "ffn1_w_up", "ffn2_w_gate", "ffn2_w_up")
BIG = (
    ("ffn1_w_gate", "row", FFN_SHARD, 1024, FFN_SHARD_PAD, 1024),
    ("ffn1_w_up", "row", FFN_SHARD, 1024, FFN_SHARD_PAD, 1024),
    ("ffn1_w_down", "row", FFN_SHARD, 1024, FFN_SHARD_PAD, 1024),
    ("w_in", "col", 1024, 1408, 1024, 1408),
    ("w_pool_out", "col", 512, 256, 512, 256),
    ("w_sgu_out", "col", 512, 256, 512, 256),
    ("w_conv_out", "col", 512, 256, 512, 256),
    ("w_out", "row", 256, 1024, 256, 1024),
    ("ffn2_w_gate", "row", FFN_SHARD, 1024, FFN_SHARD_PAD, 1024),
    ("ffn2_w_up", "row", FFN_SHARD, 1024, FFN_SHARD_PAD, 1024),
    ("ffn2_w_down", "row", FFN_SHARD, 1024, FFN_SHARD_PAD, 1024),
    ("ple_w_proj", "col", 256, 256, 256, 256),
    ("ple_w_gate", "row", 256, 1024, 256, 1024),
)
SMALL = ("ffn1_pre_g", "ffn1_post_g", "mix_pre_g", "pool_w", "pool_scale", "sgu_ln_g", "sgu_ln_b", "sgu_w_s",
         "sgu_b_s", "conv_dw_b", "conv_ln_g", "conv_ln_b", "mix_post_g", "ffn2_pre_g", "ffn2_post_g",
         "ple_pre_g", "ple_post_g")
WEIGHTS = ("ffn1_pre_g", "ffn1_w_gate", "ffn1_w_up", "ffn1_w_down", "ffn1_post_g", "mix_pre_g", "w_in", "pool_w",
           "pool_scale", "w_pool_out", "sgu_ln_g", "sgu_ln_b", "sgu_w_s", "sgu_b_s", "w_sgu_out", "conv_dw_k",
           "conv_dw_b", "conv_ln_g", "conv_ln_b", "w_conv_out", "w_out", "mix_post_g", "ffn2_pre_g", "ffn2_w_gate",
           "ffn2_w_up", "ffn2_w_down", "ffn2_post_g", "ple_w_proj", "ple_pre_g", "ple_w_gate", "ple_post_g")


def _params(n_grid):
    return pltpu.CompilerParams(dimension_semantics=("arbitrary",) * n_grid, vmem_limit_bytes=VMEM_LIMIT_BYTES)


def _tile(n, cap=ROW_TILE):
    for t in range(min(cap, n) - min(cap, n) % 16, 0, -16):
        if n % t == 0:
            return t
    return n


def _sigmoid(x):
    return 0.5 * jnp.tanh(0.5 * x) + 0.5


def _silu_and_grad(x):
    s = _sigmoid(x)
    return x * s, s * (1.0 + x * (1.0 - s))


def _gelu_and_grad(x):
    cdf = 0.5 * (1.0 + lax.erf(x * (1.0 / math.sqrt(2.0))))
    pdf = jnp.exp(-0.5 * x * x) * (1.0 / math.sqrt(2.0 * math.pi))
    return x * cdf, cdf + x * pdf


def _rms_fwd(x, g):
    return x * lax.rsqrt(jnp.mean(x * x, axis=-1, keepdims=True) + EPS) * g


def _rms_bwd(x, g, dy):
    r = lax.rsqrt(jnp.mean(x * x, axis=-1, keepdims=True) + EPS)
    xh = x * r
    dxh = dy * g
    dx = r * (dxh - xh * jnp.mean(dxh * xh, axis=-1, keepdims=True))
    return dx, jnp.sum(dy * xh, axis=0, keepdims=True)


def _ln_stats(x):
    xc = x - jnp.mean(x, axis=-1, keepdims=True)
    r = lax.rsqrt(jnp.mean(xc * xc, axis=-1, keepdims=True) + EPS)
    return xc * r, r


def _ln_bwd(xh, r, g, dy):
    dxh = dy * g
    dx = r * (dxh - jnp.mean(dxh, axis=-1, keepdims=True) - xh * jnp.mean(dxh * xh, axis=-1, keepdims=True))
    return dx, jnp.sum(dy * xh, axis=0, keepdims=True), jnp.sum(dy, axis=0, keepdims=True)


def _rowwise(name, fn, rows, consts, outs, accs=(), tm=ROW_TILE, deps=()):
    T = rows[0][0].shape[-2]
    tm = _tile(T, tm)
    n_in, n_o, n_dep = len(rows) + len(consts), len(outs), len(deps)

    def body(*refs):
        refs = refs[n_dep:]
        res = fn(*[r[...] for r in refs[:n_in]])
        for ref, val in zip(refs[n_in:n_in + n_o], res[:n_o]):
            ref[...] = val.astype(ref.dtype)
        acc_refs = refs[n_in + n_o:]
        if acc_refs:
            @pl.when(pl.program_id(0) == 0)
            def _():
                for ref, val in zip(acc_refs, res[n_o:]):
                    ref[...] = val

            @pl.when(pl.program_id(0) != 0)
            def _():
                for ref, val in zip(acc_refs, res[n_o:]):
                    ref[...] += val

    in_specs = [ANY] * n_dep
    for row in rows:
        w, cb = row[1], row[2]
        if len(row) == 4:
            in_specs.append(pl.BlockSpec((None, tm, w), lambda i, cb=cb, ld=row[3]: (ld, i, cb)))
        else:
            in_specs.append(pl.BlockSpec((tm, w), lambda i, cb=cb: (i, cb)))
    in_specs += [pl.BlockSpec(c.shape, lambda i: (0, 0)) for c in consts]
    out_specs = [pl.BlockSpec((tm, w), lambda i: (i, 0)) for w, _ in outs]
    out_specs += [pl.BlockSpec(s, lambda i: (0, 0)) for s in accs]
    out_shape = [jax.ShapeDtypeStruct((T, w), dt) for w, dt in outs]
    out_shape += [jax.ShapeDtypeStruct(s, F32) for s in accs]
    return pl.pallas_call(body, grid=(T // tm,), in_specs=in_specs, out_specs=out_specs, out_shape=out_shape,
                          name=name, compiler_params=_params(1))(*deps, *[r[0] for r in rows], *consts)


def _tiled(name, fn, grid, pos, ins, outs):
    n_in = len(ins)

    def body(_, *refs):
        res = fn(*[r[...] for r in refs[:n_in]])
        for ref, val in zip(refs[n_in:], res):
            ref[...] = val.astype(ref.dtype)

    spec = pltpu.PrefetchScalarGridSpec(
        num_scalar_prefetch=1, grid=grid, in_specs=[pl.BlockSpec(bs, im) for _, bs, im in ins],
        out_specs=[pl.BlockSpec(bs, im) for _, _, bs, im in outs])
    return pl.pallas_call(body, grid_spec=spec, out_shape=[jax.ShapeDtypeStruct(s, d) for s, d, _, _ in outs],
                          name=name, compiler_params=_params(len(grid)))(pos, *[a for a, _, _ in ins])


def _cast_layers(name, w, rp, cp, dtype, pos, deps=()):
    L, r, c = w.shape

    def body(_, w_ref, *rest):
        for k, o_ref in enumerate(rest[len(deps):]):
            @pl.when(pl.program_id(0) == k)
            def _(o_ref=o_ref):
                if (rp, cp) != (r, c):
                    o_ref[...] = jnp.zeros_like(o_ref)
                    o_ref[pl.ds(0, r), pl.ds(0, c)] = w_ref[...].astype(dtype)
                else:
                    o_ref[...] = w_ref[...].astype(dtype)

    spec = pltpu.PrefetchScalarGridSpec(
        num_scalar_prefetch=1, grid=(L,),
        in_specs=[pl.BlockSpec((None, r, c), lambda l, p: (l, 0, 0))] + [ANY] * len(deps),
        out_specs=[pl.BlockSpec((None, rp, cp), lambda l, p: (p[0], 0, 0))] * L)
    return pl.pallas_call(body, grid_spec=spec, out_shape=[jax.ShapeDtypeStruct((N_CHIPS, rp, cp), dtype)] * L,
                          name=name, compiler_params=_params(1))(pos, w, *deps)


_NN = (((1,), (0,)), ((), ()))
_NT = (((1,), (1,)), ((), ()))
_TN = (((0,), (0,)), ((), ()))


def _mm_tn(name, a, dy, buf, l, a_blocked, tk=ROW_TILE, first=0):
    T = a.shape[0]
    nb, R, C = buf[l].shape
    extra = [buf[l]] if first else []

    def body(a_ref, dy_ref, *rest):
        rest[-1][...] = lax.dot_general(a_ref[...].astype(BF), dy_ref[...].astype(BF), _TN,
                                        preferred_element_type=F32).astype(BF)

    if a_blocked:
        grid = (nb,)
        in_specs = [pl.BlockSpec((T, R), lambda b: (0, b)), pl.BlockSpec((T, C), lambda b: (0, 0))]
        out_specs = pl.BlockSpec((None, R, C), lambda b: (b, 0, 0))
    else:
        tk = min(tk, R)
        grid = (dy.shape[1] // C, R // tk)
        in_specs = [pl.BlockSpec((T, tk), lambda b, k: (0, k)), pl.BlockSpec((T, C), lambda b, k: (0, b))]
        out_specs = pl.BlockSpec((None, tk, C), lambda b, k: (b + first, k, 0))
    buf = list(buf)
    buf[l] = pl.pallas_call(body, grid=grid, in_specs=in_specs + [ANY] * len(extra), out_specs=out_specs,
                            out_shape=jax.ShapeDtypeStruct((nb, R, C), BF),
                            input_output_aliases={2: 0} if extra else {}, name=name,
                            compiler_params=_params(len(grid)))(a, dy, *extra)
    return buf


def _acc_rows(ref, val, first):
    @pl.when(first)
    def _():
        ref[...] = val

    @pl.when(jnp.logical_not(first))
    def _():
        ref[...] += val


def _norm_mm(name, h, g, ws, trans_w, act=False, deps=(), tm=2 * ROW_TILE):
    T = h.shape[0]
    nb, r, cc = ws[0].shape
    bo = r if trans_w else cc
    tm = _tile(T, tm)
    n_w, n_dep = len(ws), len(deps)

    def body(*refs):
        refs = refs[n_dep:]
        h_ref, g_ref, w_refs = refs[0], refs[1], refs[2:2 + n_w]
        n_ref, o_refs, n_s = refs[2 + n_w], refs[3 + n_w:3 + 2 * n_w], refs[-1]

        @pl.when(pl.program_id(1) == 0)
        def _():
            n = _rms_fwd(h_ref[...].astype(F32), g_ref[...]).astype(BF)
            n_s[...] = n
            n_ref[...] = n

        n = n_s[...]
        prods = []
        for w_ref, o_ref in zip(w_refs, o_refs):
            prods.append(lax.dot_general(n, w_ref[...], _NT if trans_w else _NN,
                                         preferred_element_type=F32).astype(BF))
            o_ref[...] = prods[-1]
        if act:
            refs[3 + 2 * n_w][...] = (_silu_and_grad(prods[0].astype(F32))[0] * prods[1].astype(F32)).astype(BF)

    wide = pl.BlockSpec((tm, bo), lambda i, b: (i, b))
    n_out = n_w + (1 if act else 0)
    return pl.pallas_call(
        body, grid=(T // tm, nb),
        in_specs=[ANY] * n_dep + [pl.BlockSpec((tm, D_MODEL), lambda i, b: (i, 0)),
                                  pl.BlockSpec(g.shape, lambda i, b: (0, 0))]
        + [pl.BlockSpec((None, r, cc), lambda i, b: (b, 0, 0))] * n_w,
        out_specs=[pl.BlockSpec((tm, D_MODEL), lambda i, b: (i, 0))] + [wide] * n_out,
        out_shape=[jax.ShapeDtypeStruct((T, D_MODEL), BF)] + [jax.ShapeDtypeStruct((T, nb * bo), BF)] * n_out,
        scratch_shapes=[pltpu.VMEM((tm, D_MODEL), BF)], name=name, compiler_params=_params(2))(*deps, h, g, *ws)


def _mm_res(name, x, w3, h, g, coef, tm=2 * ROW_TILE):
    T, kx = x.shape
    w2 = w3.reshape(kx, D_MODEL)
    tm = _tile(T, tm)

    def body(x_ref, w_ref, h_ref, g_ref, f_ref, o_ref):
        f_ref[...] = jnp.dot(x_ref[...], w_ref[...], preferred_element_type=F32).astype(BF)
        for c in range(tm // EPI_ROWS):
            rows = slice(c * EPI_ROWS, (c + 1) * EPI_ROWS)
            o_ref[rows, :] = h_ref[rows, :] + coef * _rms_fwd(f_ref[rows, :].astype(F32), g_ref[...])

    row = pl.BlockSpec((tm, D_MODEL), lambda i: (i, 0))
    whole = pl.BlockSpec(w2.shape, lambda i: (0, 0), pipeline_mode=pl.Buffered(1))
    return pl.pallas_call(
        body, grid=(T // tm,),
        in_specs=[pl.BlockSpec((tm, kx), lambda i: (i, 0)), whole, row,
                  pl.BlockSpec(g.shape, lambda i: (0, 0))],
        out_specs=[row, row],
        out_shape=[jax.ShapeDtypeStruct((T, D_MODEL), BF), jax.ShapeDtypeStruct((T, D_MODEL), F32)],
        name=name, compiler_params=_params(1))(x, w2, h, g)


def _resbwd_mm(name, dh, f, g, coef, w3, trans_w, act=None, deps=(), tm=2 * ROW_TILE):
    T = dh.shape[0]
    nb, r, cc = w3.shape
    bo = r if trans_w else cc
    tm = _tile(T, tm)
    n_dep, n_act = len(deps), 3 if act else 0
    n_i = T // tm

    def body(*refs):
        refs = refs[n_dep:]
        dh_ref, f_ref, g_ref, w_ref = refs[:4]
        df_ref, dg_ref = refs[4 + n_act], refs[5 + n_act]
        df_s = refs[-2] if act else refs[-1]
        i, b = pl.program_id(0), pl.program_id(1)

        @pl.when(b == 0)
        def _():
            dg = jnp.zeros((1, D_MODEL), F32)
            for c in range(tm // EPI_ROWS):
                rows = slice(c * EPI_ROWS, (c + 1) * EPI_ROWS)
                dx, dg_c = _rms_bwd(f_ref[rows, :].astype(F32), g_ref[...], coef * dh_ref[rows, :])
                df_s[rows, :] = dx.astype(BF)
                df_ref[rows, :] = dx.astype(BF)
                dg = dg + dg_c
            _acc_rows(dg_ref, dg, i == 0)

        if act:
            for j in range(bo // MXU_TILE):
                cols = slice(j * MXU_TILE, (j + 1) * MXU_TILE)
                prod = lax.dot_general(df_s[...], w_ref[cols, :], _NT, preferred_element_type=F32)
                val, grad = _silu_and_grad(refs[4][:, cols])
                prod = prod.astype(BF)
                refs[6 + n_act][:, cols] = prod * refs[5][:, cols] * grad
                refs[7 + n_act][:, cols] = prod * val
            acc = refs[-1]
            part = lax.dot_general(refs[6][...], df_s[...], _TN, preferred_element_type=F32)

            @pl.when(i == 0)
            def _():
                acc[b] = part

            @pl.when(i != 0)
            def _():
                acc[b] += part

            @pl.when(i == n_i - 1)
            def _():
                refs[8 + n_act][...] = acc[b].astype(BF)
        else:
            refs[6][...] = lax.dot_general(df_s[...], w_ref[...], _NT if trans_w else _NN,
                                           preferred_element_type=F32).astype(BF)

    row = pl.BlockSpec((tm, D_MODEL), lambda i, b: (i, 0))
    wide = pl.BlockSpec((tm, bo), lambda i, b: (i, b))
    vec = pl.BlockSpec((1, D_MODEL), lambda i, b: (0, 0))
    out_specs = [row, vec] + [wide] * (2 if act else 1)
    out_shape = [jax.ShapeDtypeStruct((T, D_MODEL), BF), jax.ShapeDtypeStruct((1, D_MODEL), F32)]
    out_shape += [jax.ShapeDtypeStruct((T, nb * bo), BF)] * (2 if act else 1)
    scratch = [pltpu.VMEM((tm, D_MODEL), BF)]
    if act:
        out_specs.append(pl.BlockSpec((None, r, cc), lambda i, b: (jnp.where(i == n_i - 1, b, 0), 0, 0)))
        out_shape.append(jax.ShapeDtypeStruct((nb, r, cc), BF))
        scratch.append(pltpu.VMEM((nb, r, cc), F32))
    return pl.pallas_call(
        body, grid=(n_i, nb),
        in_specs=[ANY] * n_dep + [row, row, vec, pl.BlockSpec((None, r, cc), lambda i, b: (b, 0, 0))] + [wide] * n_act,
        out_specs=out_specs, out_shape=out_shape, scratch_shapes=scratch, name=name,
        compiler_params=_params(2))(*deps, dh, f, g, w3, *(act or ()))


def _dn_prenorm(name, xs, ws, trans_w, dh, h, g, tm=2 * ROW_TILE):
    T = dh.shape[0]
    chained = not isinstance(ws, (list, tuple))
    ws = [ws] if chained else list(ws)
    _, r, cc = ws[0].shape
    bw = cc if trans_w else r
    per_x = xs[0].shape[1] // bw
    nb = per_x * len(xs) if chained else per_x
    tm = _tile(T, tm)
    n_x, n_w = len(xs), len(ws)

    def body(*refs):
        x_refs, w_refs = refs[:n_x], refs[n_x:n_x + n_w]
        dh_ref, h_ref, g_ref, o_ref, dg_ref, acc = refs[n_x + n_w:]
        i, b = pl.program_id(0), pl.program_id(1)

        @pl.when(b == 0)
        def _():
            acc[...] = jnp.zeros_like(acc)

        def add(x_ref, w_ref):
            acc[...] += lax.dot_general(x_ref[...], w_ref[...], _NT if trans_w else _NN, preferred_element_type=F32)

        if chained:
            for k, x_ref in enumerate(x_refs):
                pl.when(b // per_x == k)(lambda x_ref=x_ref: add(x_ref, w_refs[0]))
        else:
            for x_ref, w_ref in zip(x_refs, w_refs):
                add(x_ref, w_ref)

        @pl.when(b == nb - 1)
        def _():
            dg = jnp.zeros((1, D_MODEL), F32)
            for c in range(tm // EPI_ROWS):
                rows = slice(c * EPI_ROWS, (c + 1) * EPI_ROWS)
                dx, dg_c = _rms_bwd(h_ref[rows, :], g_ref[...], acc[rows, :])
                o_ref[rows, :] = dh_ref[rows, :] + dx
                dg = dg + dg_c
            _acc_rows(dg_ref, dg, i == 0)

    row = pl.BlockSpec((tm, D_MODEL), lambda i, b: (i, 0))
    vec = pl.BlockSpec((1, D_MODEL), lambda i, b: (0, 0))
    if chained:
        x_specs = [pl.BlockSpec((tm, bw), lambda i, b, k=k: (i, jnp.clip(b - k * per_x, 0, per_x - 1)))
                   for k in range(n_x)]
    else:
        x_specs = [pl.BlockSpec((tm, bw), lambda i, b: (i, b))] * n_x
    return pl.pallas_call(
        body, grid=(T // tm, nb),
        in_specs=x_specs + [pl.BlockSpec((None, r, cc), lambda i, b: (b, 0, 0))] * n_w + [row, row, vec],
        out_specs=[row, vec],
        out_shape=[jax.ShapeDtypeStruct((T, D_MODEL), F32), jax.ShapeDtypeStruct((1, D_MODEL), F32)],
        scratch_shapes=[pltpu.VMEM((tm, D_MODEL), F32)], name=name,
        compiler_params=_params(2))(*xs, *ws, dh, h, g)


def _pool_apply(x, win, row):
    s, k = x, 1
    while k < win:
        s = s + jnp.where(row >= k, pltpu.roll(s, k, 0), 0.0)
        k *= 2
    return s / jnp.minimum(row + 1, win).astype(F32) - x


def _pool_apply_t(dp, win, row):
    T = dp.shape[0]
    s, k = dp / jnp.minimum(row + 1, win).astype(F32), 1
    while k < win:
        s = s + jnp.where(row < T - k, pltpu.roll(s, T - k, 0), 0.0)
        k *= 2
    return s - dp


def _pool_fwd(z, w, scale):
    T = z.shape[0]

    def body(z_ref, w_ref, s_ref, o_ref):
        row = lax.broadcasted_iota(jnp.int32, (T, LANES), 0)
        for gi, win in enumerate(POOL_WINDOWS):
            cols = pl.ds(gi * LANES, LANES)
            pooled = _pool_apply(z_ref[:, cols].astype(F32), win, row)
            y = jnp.dot(pooled.astype(BF), w_ref[gi].astype(BF), preferred_element_type=F32)
            o_ref[:, cols] = (y * s_ref[:, cols]).astype(o_ref.dtype)

    return pl.pallas_call(
        body, grid=(1,),
        in_specs=[pl.BlockSpec((T, BRANCH), lambda i: (0, ZB_POOL)), pl.BlockSpec(w.shape, lambda i: (0, 0, 0)),
                  pl.BlockSpec(scale.shape, lambda i: (0, 0))],
        out_specs=pl.BlockSpec((T, BRANCH), lambda i: (0, 0)), out_shape=jax.ShapeDtypeStruct((T, BRANCH), BF),
        name="pool_fwd", compiler_params=_params(1))(z, w, scale)


def _pool_bwd(dr, z, w, scale):
    T = z.shape[0]

    def body(dr_ref, z_ref, w_ref, s_ref, dz_ref, dw_ref, ds_ref):
        row = lax.broadcasted_iota(jnp.int32, (T, LANES), 0)
        for gi, win in enumerate(POOL_WINDOWS):
            cols = pl.ds(gi * LANES, LANES)
            pooled = _pool_apply(z_ref[:, cols].astype(F32), win, row).astype(BF)
            wg = w_ref[gi].astype(BF)
            y = jnp.dot(pooled, wg, preferred_element_type=F32)
            d = dr_ref[:, cols].astype(F32)
            ds_ref[:, cols] = jnp.sum(d * y, axis=0, keepdims=True)
            dy = (d * s_ref[:, cols]).astype(BF)
            dw_ref[gi] = lax.dot_general(pooled, dy, _TN, preferred_element_type=F32)
            dpooled = lax.dot_general(dy, wg, _NT, preferred_element_type=F32)
            dz_ref[:, cols] = _pool_apply_t(dpooled, win, row).astype(dz_ref.dtype)

    return pl.pallas_call(
        body, grid=(1,),
        in_specs=[pl.BlockSpec((T, BRANCH), lambda i: (0, 0)), pl.BlockSpec((T, BRANCH), lambda i: (0, ZB_POOL)),
                  pl.BlockSpec(w.shape, lambda i: (0, 0, 0)), pl.BlockSpec(scale.shape, lambda i: (0, 0))],
        out_specs=[pl.BlockSpec((T, BRANCH), lambda i: (0, 0)), pl.BlockSpec(w.shape, lambda i: (0, 0, 0)),
                   pl.BlockSpec(scale.shape, lambda i: (0, 0))],
        out_shape=[jax.ShapeDtypeStruct((T, BRANCH), BF), jax.ShapeDtypeStruct(w.shape, F32),
                   jax.ShapeDtypeStruct(scale.shape, F32)],
        name="pool_bwd", compiler_params=_params(1))(dr, z, w, scale)


def _tril(transposed=False):
    r = lax.broadcasted_iota(jnp.int32, (CHUNK, CHUNK), 0)
    c = lax.broadcasted_iota(jnp.int32, (CHUNK, CHUNK), 1)
    return c >= r if transposed else r >= c


def _sgu_fwd(z, ln_g, ln_b, w_s, bias):
    T = z.shape[0]
    tm = _tile(T)

    def body(zu_ref, zv_ref, g_ref, b_ref, w_ref, bias_ref, o_ref):
        gu, _ = _gelu_and_grad(zu_ref[...].astype(F32))
        gv, _ = _gelu_and_grad(zv_ref[...].astype(F32))
        xh, _ = _ln_stats(gv)
        v16 = (xh * g_ref[...] + b_ref[...]).astype(BF)
        tri = _tril()
        for h in range(SGU_HEADS):
            cols = slice(h * LANES, (h + 1) * LANES)
            wh = jnp.where(tri, w_ref[h], 0.0).astype(BF)
            for c in range(tm // CHUNK):
                rows = slice(c * CHUNK, (c + 1) * CHUNK)
                s = jnp.dot(wh, v16[rows, cols], preferred_element_type=F32) + bias_ref[:, cols]
                o_ref[rows, cols] = (gu[rows, cols] * s).astype(o_ref.dtype)

    small = [pl.BlockSpec(a.shape, lambda i, n=a.ndim: (0,) * n) for a in (ln_g, ln_b, w_s, bias)]
    return pl.pallas_call(
        body, grid=(T // tm,),
        in_specs=[pl.BlockSpec((tm, BRANCH), lambda i: (i, ZB_U)), pl.BlockSpec((tm, BRANCH), lambda i: (i, ZB_V))] + small,
        out_specs=pl.BlockSpec((tm, BRANCH), lambda i: (i, 0)), out_shape=jax.ShapeDtypeStruct((T, BRANCH), BF),
        name="sgu_fwd", compiler_params=_params(1))(z, z, ln_g, ln_b, w_s, bias)


def _sgu_bwd(dr, z, ln_g, ln_b, w_s, w_st, bias):
    T = z.shape[0]
    tm = _tile(T)
    n_steps = T // tm

    def body(dr_ref, zu_ref, zv_ref, g_ref, b_ref, w_ref, wt_ref, bias_ref,
             dzu_ref, dzv_ref, dg_ref, db_ref, dw_ref, dbias_ref, dgu_s, dv_s):
        i = pl.program_id(0)

        @pl.when(i == 0)
        def _():
            dg_ref[...] = jnp.zeros_like(dg_ref)
            db_ref[...] = jnp.zeros_like(db_ref)
            dw_ref[...] = jnp.zeros_like(dw_ref)
            dbias_ref[...] = jnp.zeros_like(dbias_ref)

        zu = zu_ref[...].astype(F32)
        zv = zv_ref[...].astype(F32)
        gu, gu_grad = _gelu_and_grad(zu)
        gv, gv_grad = _gelu_and_grad(zv)
        xh, r = _ln_stats(gv)
        v16 = (xh * g_ref[...] + b_ref[...]).astype(BF)
        dr = dr_ref[...].astype(F32)
        tri = _tril()
        for h in range(SGU_HEADS):
            cols = slice(h * LANES, (h + 1) * LANES)
            wh = jnp.where(tri, w_ref[h], 0.0).astype(BF)
            wht = jnp.where(_tril(transposed=True), wt_ref[h], 0.0).astype(BF)
            for c in range(tm // CHUNK):
                rows = slice(c * CHUNK, (c + 1) * CHUNK)
                v_blk = v16[rows, cols]
                s = jnp.dot(wh, v_blk, preferred_element_type=F32) + bias_ref[:, cols]
                ds = dr[rows, cols] * gu[rows, cols]
                dgu_s[rows, cols] = dr[rows, cols] * s
                ds16 = ds.astype(BF)
                dw_ref[h] += jnp.where(tri, lax.dot_general(ds16, v_blk, _NT, preferred_element_type=F32), 0.0)
                dv_s[rows, cols] = jnp.dot(wht, ds16, preferred_element_type=F32)
                dbias_ref[:, cols] += ds
        dzu_ref[...] = (dgu_s[...] * gu_grad).astype(dzu_ref.dtype)
        dgv, dg, db = _ln_bwd(xh, r, g_ref[...], dv_s[...])
        dzv_ref[...] = (dgv * gv_grad).astype(dzv_ref.dtype)
        dg_ref[...] += dg
        db_ref[...] += db

        @pl.when(i == n_steps - 1)
        def _():
            for h in range(SGU_HEADS):
                cols = slice(h * LANES, (h + 1) * LANES)
                tot = jnp.sum(dbias_ref[:, cols], axis=1, keepdims=True)
                dbias_ref[:, cols] = jnp.broadcast_to(tot, (CHUNK, LANES))

    small = (ln_g, ln_b, w_s, w_st, bias)
    small_specs = [pl.BlockSpec(a.shape, lambda i, n=a.ndim: (0,) * n) for a in small]
    return pl.pallas_call(
        body, grid=(n_steps,),
        in_specs=[pl.BlockSpec((tm, BRANCH), lambda i: (i, 0)), pl.BlockSpec((tm, BRANCH), lambda i: (i, ZB_U)),
                  pl.BlockSpec((tm, BRANCH), lambda i: (i, ZB_V))] + small_specs,
        out_specs=[pl.BlockSpec((tm, BRANCH), lambda i: (i, 0)), pl.BlockSpec((tm, BRANCH), lambda i: (i, 0)),
                   pl.BlockSpec((1, BRANCH), lambda i: (0, 0)), pl.BlockSpec((1, BRANCH), lambda i: (0, 0)),
                   pl.BlockSpec(w_s.shape, lambda i: (0, 0, 0)), pl.BlockSpec(bias.shape, lambda i: (0, 0))],
        out_shape=[jax.ShapeDtypeStruct((T, BRANCH), BF), jax.ShapeDtypeStruct((T, BRANCH), BF),
                   jax.ShapeDtypeStruct((1, BRANCH), F32), jax.ShapeDtypeStruct((1, BRANCH), F32),
                   jax.ShapeDtypeStruct(w_s.shape, F32), jax.ShapeDtypeStruct(bias.shape, F32)],
        scratch_shapes=[pltpu.VMEM((tm, BRANCH), F32), pltpu.VMEM((tm, BRANCH), F32)],
        name="sgu_bwd", compiler_params=_params(1))(dr, z, z, ln_g, ln_b, w_s, w_st, bias)


def _conv_fwd(z, convk, l, bias):
    T = z.shape[0]

    def body(za_ref, zb_ref, k_ref, b_ref, o_ref):
        xg = za_ref[...].astype(F32) * _sigmoid(zb_ref[...].astype(F32))
        xp = jnp.concatenate([jnp.zeros((CONV_PAD, LANES), F32), xg], axis=0)
        kw = k_ref[...]
        acc = jnp.broadcast_to(b_ref[...], (T, LANES))
        for s in range(SUBLANES):
            xs = xp if s == 0 else pltpu.roll(xp, s, 0)
            for q in range(CONV_PAD // SUBLANES):
                k = CONV_TAPS - 1 - (SUBLANES * q + s)
                if k >= 0:
                    lo = CONV_PAD - SUBLANES * q
                    acc = acc + kw[k:k + 1, :] * xs[lo:lo + T, :]
        o_ref[...] = acc.astype(o_ref.dtype)

    return pl.pallas_call(
        body, grid=(4,),
        in_specs=[pl.BlockSpec((T, LANES), lambda g: (0, 4 * ZB_A + g)),
                  pl.BlockSpec((T, LANES), lambda g: (0, 4 * ZB_B + g)),
                  pl.BlockSpec((None, CONV_PAD, LANES), lambda g: (g, 0, 0)),
                  pl.BlockSpec((1, LANES), lambda g: (0, g))],
        out_specs=pl.BlockSpec((T, LANES), lambda g: (0, g)), out_shape=jax.ShapeDtypeStruct((T, BRANCH), BF),
        name="conv_fwd", compiler_params=_params(1))(z, z, convk[l], bias)


def _conv_bwd(dy, z, convk, l):
    T = z.shape[0]

    def body(dy_ref, za_ref, zb_ref, k_ref, dza_ref, dzb_ref, dk_ref, db_ref):
        a = za_ref[...].astype(F32)
        sg = _sigmoid(zb_ref[...].astype(F32))
        d = dy_ref[...].astype(F32)
        kw = k_ref[...]
        xp = jnp.concatenate([jnp.zeros((CONV_PAD, LANES), F32), a * sg], axis=0)
        dp = jnp.concatenate([d, jnp.zeros((CONV_PAD, LANES), F32)], axis=0)
        dxg = jnp.zeros((T, LANES), F32)
        dk_ref[...] = jnp.zeros_like(dk_ref)
        for s in range(SUBLANES):
            xs = xp if s == 0 else pltpu.roll(xp, s, 0)
            ds = dp if s == 0 else pltpu.roll(dp, T + CONV_PAD - s, 0)
            for q in range(CONV_PAD // SUBLANES):
                k = CONV_TAPS - 1 - (SUBLANES * q + s)
                if k >= 0:
                    lo = CONV_PAD - SUBLANES * q
                    dk_ref[k:k + 1, :] = jnp.sum(d * xs[lo:lo + T, :], axis=0, keepdims=True)
                    dxg = dxg + kw[k:k + 1, :] * ds[SUBLANES * q:SUBLANES * q + T, :]
        db_ref[...] = jnp.sum(d, axis=0, keepdims=True)
        dza_ref[...] = (dxg * sg).astype(dza_ref.dtype)
        dzb_ref[...] = (dxg * a * sg * (1.0 - sg)).astype(dzb_ref.dtype)

    col = pl.BlockSpec((T, LANES), lambda g: (0, g))
    return pl.pallas_call(
        body, grid=(4,),
        in_specs=[col, pl.BlockSpec((T, LANES), lambda g: (0, 4 * ZB_A + g)),
                  pl.BlockSpec((T, LANES), lambda g: (0, 4 * ZB_B + g)),
                  pl.BlockSpec((None, CONV_PAD, LANES), lambda g: (g, 0, 0))],
        out_specs=[col, col, pl.BlockSpec((CONV_PAD, LANES), lambda g: (0, g)),
                   pl.BlockSpec((1, LANES), lambda g: (0, g))],
        out_shape=[jax.ShapeDtypeStruct((T, BRANCH), BF), jax.ShapeDtypeStruct((T, BRANCH), BF),
                   jax.ShapeDtypeStruct((CONV_PAD, BRANCH), F32), jax.ShapeDtypeStruct((1, BRANCH), F32)],
        name="conv_bwd", compiler_params=_params(1))(dy, z, z, convk[l])


D = D_MODEL


def _ffn_fwd(l, h, S, W, pre, deps=()):
    n, gp, u, a = _norm_mm("ffn_in", h, S[pre + "_pre_g"], [W[pre + "_w_gate"][l], W[pre + "_w_up"][l]], True,
                           act=True, deps=deps)
    f, out = _mm_res("ffn_out", a, W[pre + "_w_down"][l], h, S[pre + "_post_g"], 0.5)
    return out, dict(h=h, n=n, gp=gp, u=u, a=a, f=f)


def _ffn_bwd(l, dh, sv, S, W, G, SG, pre, deps=()):
    df, SG[pre + "_post_g"], dgp, du, dwd = _resbwd_mm(
        "ffn_bwd_act", dh, sv["f"], S[pre + "_post_g"], 0.5, W[pre + "_w_down"][l], True,
        act=(sv["gp"], sv["u"], sv["a"]), deps=deps, tm=ROW_TILE)
    G[pre + "_w_down"] = G[pre + "_w_down"][:l] + [dwd] + G[pre + "_w_down"][l + 1:]
    G[pre + "_w_gate"] = _mm_tn("ffn_dw_gate", dgp, sv["n"], G[pre + "_w_gate"], l, True)
    G[pre + "_w_up"] = _mm_tn("ffn_dw_up", du, sv["n"], G[pre + "_w_up"], l, True)
    dh_in, SG[pre + "_pre_g"] = _dn_prenorm("ffn_bwd_in", [dgp, du], [W[pre + "_w_gate"][l], W[pre + "_w_up"][l]],
                                            False, dh, sv["h"], S[pre + "_pre_g"])
    return dh_in


def _gates(zg):
    return [_sigmoid(jnp.concatenate([zg[2 * k].astype(F32), zg[2 * k + 1].astype(F32)], axis=1)) for k in range(3)]


def _merge_fwd(z, rs, ws, tm=ROW_TILE):
    T = z.shape[0]
    tm = _tile(T, tm)
    nb, kk, bw = ws[0].shape

    def body(*refs):
        r_refs, g_refs, w_refs, y_refs, m_ref = refs[:3], refs[3:9], refs[9:12], refs[12:15], refs[15]
        for r_ref, w_ref, y_ref in zip(r_refs, w_refs, y_refs):
            for b in range(nb):
                y_ref[:, b * bw:(b + 1) * bw] = jnp.dot(r_ref[...], w_ref[b],
                                                        preferred_element_type=F32).astype(y_ref.dtype)
        g = _gates([q[...] for q in g_refs])
        m_ref[...] = (g[0] * y_refs[0][...].astype(F32) + g[1] * y_refs[1][...].astype(F32)
                      + g[2] * y_refs[2][...].astype(F32)).astype(m_ref.dtype)

    row = pl.BlockSpec((tm, D_MODEL), lambda i: (i, 0))
    return pl.pallas_call(
        body, grid=(T // tm,),
        in_specs=[pl.BlockSpec((tm, kk), lambda i: (i, 0))] * 3
        + [pl.BlockSpec((tm, BRANCH), lambda i, j=j: (i, ZB_GATES + j)) for j in range(6)]
        + [pl.BlockSpec(ws[0].shape, lambda i: (0, 0, 0))] * 3,
        out_specs=[row] * 4, out_shape=[jax.ShapeDtypeStruct((T, D_MODEL), BF)] * 4,
        name="mix_merge", compiler_params=_params(1))(*rs, *[z] * 6, *ws)


def _merge_bwd(dmerged, z, ys, ws, tm=ROW_TILE):
    T = z.shape[0]
    tm = _tile(T, tm)
    nb, kk, bw = ws[0].shape

    def body(*refs):
        dm_ref, g_refs, y_refs, w_refs = refs[0], refs[1:7], refs[7:10], refs[10:13]
        dy_refs, lo_ref, hi_ref, dr_refs = refs[13:16], refs[16], refs[17], refs[18:21]
        cut = DZ_HALF - ZB_GATES * BRANCH
        dm = dm_ref[...].astype(F32)
        g = _gates([q[...] for q in g_refs])
        for k in range(3):
            dy_refs[k][...] = (dm * g[k]).astype(BF)
            dzg = (dm * y_refs[k][...].astype(F32) * g[k] * (1.0 - g[k])).astype(BF)
            if k == 0:
                lo_ref[...] = dzg[:, :cut]
                hi_ref[:, :D_MODEL - cut] = dzg[:, cut:]
            else:
                hi_ref[:, k * D_MODEL - cut:(k + 1) * D_MODEL - cut] = dzg
            dr = None
            for b in range(nb):
                p = lax.dot_general(dy_refs[k][:, b * bw:(b + 1) * bw], w_refs[k][b], _NT,
                                    preferred_element_type=F32)
                dr = p if dr is None else dr + p
            dr_refs[k][...] = dr.astype(BF)

    row = pl.BlockSpec((tm, D_MODEL), lambda i: (i, 0))
    return pl.pallas_call(
        body, grid=(T // tm,),
        in_specs=[row] + [pl.BlockSpec((tm, BRANCH), lambda i, j=j: (i, ZB_GATES + j)) for j in range(6)] + [row] * 3
        + [pl.BlockSpec(ws[0].shape, lambda i: (0, 0, 0))] * 3,
        out_specs=[row] * 3 + [pl.BlockSpec((tm, DZ_HALF - ZB_GATES * BRANCH), lambda i: (i, 0)),
                               pl.BlockSpec((tm, DZ_HALF), lambda i: (i, 0))]
        + [pl.BlockSpec((tm, kk), lambda i: (i, 0))] * 3,
        out_shape=[jax.ShapeDtypeStruct((T, D_MODEL), BF)] * 3
        + [jax.ShapeDtypeStruct((T, DZ_HALF - ZB_GATES * BRANCH), BF), jax.ShapeDtypeStruct((T, DZ_HALF), BF)]
        + [jax.ShapeDtypeStruct((T, kk), BF)] * 3,
        name="mix_merge_bwd", compiler_params=_params(1))(dmerged, *[z] * 6, *ys, *ws)


def _mix_fwd(l, h, S, W, deps=()):
    n, z = _norm_mm("mix_in", h, S["mix_pre_g"], [W["w_in"][l]], False, deps=deps)
    r_pool = _pool_fwd(z, S["pool_w"], S["pool_scale"])
    r_sgu = _sgu_fwd(z, S["sgu_ln_g"], S["sgu_ln_b"], S["sgu_w_s"], S["sgu_bias"])
    yc = _conv_fwd(z, W["conv_dw_k"], l, S["conv_dw_b"])

    def ln_silu(y, g, b):
        xh, _ = _ln_stats(y.astype(F32))
        return (_silu_and_grad(xh * g + b)[0],)

    r_conv = _rowwise("conv_ln", ln_silu, [(yc, BRANCH, 0)], [S["conv_ln_g"], S["conv_ln_b"]], [(BRANCH, BF)])[0]
    y_pool, y_sgu, y_conv, merged = _merge_fwd(z, (r_pool, r_sgu, r_conv),
                                               [W["w_%s_out" % br][l] for br in ("pool", "sgu", "conv")])
    o, out = _mm_res("mix_out", merged, W["w_out"][l], h, S["mix_post_g"], 1.0)
    return out, dict(h=h, n=n, z=z, r_pool=r_pool, r_sgu=r_sgu, yc=yc, r_conv=r_conv, y_pool=y_pool, y_sgu=y_sgu,
                     y_conv=y_conv, merged=merged, o=o)


def _branch_dw(rs, dys, shape):
    T, kk = rs[0].shape
    nb, _, bw = shape

    def body(*refs):
        for k in range(3):
            refs[6 + k][...] = lax.dot_general(refs[k][...], refs[3 + k][...], _TN,
                                               preferred_element_type=F32).astype(BF)

    return pl.pallas_call(
        body, grid=(nb,),
        in_specs=[pl.BlockSpec((T, kk), lambda b: (0, 0))] * 3 + [pl.BlockSpec((T, bw), lambda b: (0, b))] * 3,
        out_specs=[pl.BlockSpec((None, kk, bw), lambda b: (b, 0, 0))] * 3,
        out_shape=[jax.ShapeDtypeStruct((nb, kk, bw), BF)] * 3, name="branch_dw",
        compiler_params=_params(1))(*rs, *dys)


def _mix_bwd(l, dh, sv, S, W, G, SG, deps=()):
    z = sv["z"]
    do, SG["mix_post_g"], dmerged = _resbwd_mm("mix_bwd_out", dh, sv["o"], S["mix_post_g"], 1.0,
                                               W["w_out"][l].reshape(1, D, D), True, deps=deps)
    G["w_out"] = _mm_tn("mix_dw_out", sv["merged"], do, G["w_out"], l, True)

    branches = ("pool", "sgu", "conv")
    res = _merge_bwd(dmerged, z, [sv["y_" + br] for br in branches], [W["w_%s_out" % br][l] for br in branches])
    dz_gate_lo, dz_hi, dr = res[3], res[4], dict(zip(branches, res[5:]))
    for br, dw in zip(branches, _branch_dw([sv["r_" + br] for br in branches], res[:3], G["w_pool_out"][l].shape)):
        wn = "w_%s_out" % br
        G[wn] = G[wn][:l] + [dw] + G[wn][l + 1:]
    dz_pool, SG["pool_w"], SG["pool_scale"] = _pool_bwd(dr["pool"], z, S["pool_w"], S["pool_scale"])
    dzu, dzv, SG["sgu_ln_g"], SG["sgu_ln_b"], SG["sgu_w_s"], dbias = _sgu_bwd(
        dr["sgu"], z, S["sgu_ln_g"], S["sgu_ln_b"], S["sgu_w_s"], S["sgu_w_st"], S["sgu_bias"])
    SG["sgu_b_s"] = dbias[:, ::LANES].T

    def ln_silu_bwd(d, y, g, b):
        xh, r = _ln_stats(y.astype(F32))
        _, grad = _silu_and_grad(xh * g + b)
        return _ln_bwd(xh, r, g, d.astype(F32) * grad)

    dyc, SG["conv_ln_g"], SG["conv_ln_b"] = _rowwise(
        "conv_ln_bwd", ln_silu_bwd, [(dr["conv"], BRANCH, 0), (sv["yc"], BRANCH, 0)],
        [S["conv_ln_g"], S["conv_ln_b"]], [(BRANCH, BF)], [(1, BRANCH), (1, BRANCH)])
    dza, dzb, SG["conv_dw_k"], SG["conv_dw_b"] = _conv_bwd(dyc, z, W["conv_dw_k"], l)
    dz_lo = jnp.concatenate([dz_pool, dzu, dzv, dza, dzb, dz_gate_lo], axis=1)
    G["w_in"] = _mm_tn("mix_dw_in", sv["n"], dz_lo, G["w_in"], l, False)
    G["w_in"] = _mm_tn("mix_dw_in", sv["n"], dz_hi, G["w_in"], l, False, first=2)
    dh_in, SG["mix_pre_g"] = _dn_prenorm("mix_bwd_in", [dz_lo, dz_hi], W["w_in"][l], True, dh, sv["h"],
                                         S["mix_pre_g"])
    return dh_in


def _ple_out(h, p, gp, w3, g, tm=ROW_TILE):
    T, kp = p.shape
    nb, _, bw = w3.shape
    tm = _tile(T, tm)

    def body(h_ref, p_ref, gp_ref, w_ref, g_ref, e_ref, o_ref):
        p16 = p_ref[...].astype(BF)
        for b in range(nb):
            e_ref[:, b * bw:(b + 1) * bw] = jnp.dot(p16, w_ref[b], preferred_element_type=F32).astype(BF)
        q = _sigmoid(gp_ref[...].astype(F32)) * e_ref[...].astype(F32)
        o_ref[...] = h_ref[...] + _rms_fwd(q, g_ref[...])

    row = pl.BlockSpec((tm, D_MODEL), lambda i: (i, 0))
    return pl.pallas_call(
        body, grid=(T // tm,),
        in_specs=[row, pl.BlockSpec((tm, kp), lambda i: (i, 0)), row, pl.BlockSpec(w3.shape, lambda i: (0, 0, 0)),
                  pl.BlockSpec(g.shape, lambda i: (0, 0))],
        out_specs=[row, row],
        out_shape=[jax.ShapeDtypeStruct((T, D_MODEL), BF), jax.ShapeDtypeStruct((T, D_MODEL), F32)],
        name="ple_out", compiler_params=_params(1))(h, p, gp, w3, g)


def _ple_fwd(l, h, p_l, S, W, deps=()):
    n, gp = _norm_mm("ple_in", h, S["ple_pre_g"], [W["ple_w_gate"][l].reshape(1, D, D)], False, deps=deps)
    e, out = _ple_out(h, p_l, gp, W["ple_w_proj"][l], S["ple_post_g"])
    return out, dict(h=h, n=n, e=e, gp=gp, p=p_l)


def _ple_bwd_rows(dh, e, gp, g_post, w3, h, g_pre, deps=(), tm=ROW_TILE):
    T = dh.shape[0]
    w2 = w3.reshape(D_MODEL, D_MODEL)
    tm = _tile(T, tm)
    n_dep = len(deps)

    def body(*refs):
        dh_ref, e_ref, gp_ref, gpost_ref, w_ref, h_ref, gpre_ref, de_ref, dgp_ref, o_ref, dpost_ref, dpre_ref = \
            refs[n_dep:]
        first = pl.program_id(0) == 0
        d = dh_ref[...]
        sg = _sigmoid(gp_ref[...].astype(F32))
        ee = e_ref[...].astype(F32)
        dq, dpost = _rms_bwd(sg * ee, gpost_ref[...], d)
        de_ref[...] = (dq * sg).astype(BF)
        dgp = (dq * ee * sg * (1.0 - sg)).astype(BF)
        dgp_ref[...] = dgp
        dn = lax.dot_general(dgp, w_ref[...], _NT, preferred_element_type=F32)
        dx, dpre = _rms_bwd(h_ref[...], gpre_ref[...], dn)
        o_ref[...] = d + dx
        _acc_rows(dpost_ref, dpost, first)
        _acc_rows(dpre_ref, dpre, first)

    row = pl.BlockSpec((tm, D_MODEL), lambda i: (i, 0))
    vec = pl.BlockSpec((1, D_MODEL), lambda i: (0, 0))
    return pl.pallas_call(
        body, grid=(T // tm,),
        in_specs=[ANY] * n_dep + [row, row, row, vec, pl.BlockSpec(w2.shape, lambda i: (0, 0)), row, vec],
        out_specs=[row, row, row, vec, vec],
        out_shape=[jax.ShapeDtypeStruct((T, D_MODEL), BF)] * 2 + [jax.ShapeDtypeStruct((T, D_MODEL), F32)]
        + [jax.ShapeDtypeStruct((1, D_MODEL), F32)] * 2,
        name="ple_bwd", compiler_params=_params(1))(*deps, dh, e, gp, g_post, w2, h, g_pre)


def _ple_bwd(l, dh, sv, S, W, G, SG, deps=()):
    de, dgp, dh_in, SG["ple_post_g"], SG["ple_pre_g"] = _ple_bwd_rows(
        dh, sv["e"], sv["gp"], S["ple_post_g"], W["ple_w_gate"][l], sv["h"], S["ple_pre_g"], deps)
    G["ple_w_proj"] = _mm_tn("ple_dw_proj", sv["p"], de, G["ple_w_proj"], l, False)
    G["ple_w_gate"] = _mm_tn("ple_dw_gate", sv["n"], dgp, G["ple_w_gate"], l, True)
    return dh_in


def _layer_small(a, l):
    S = {}
    for name in SMALL:
        v = a[name][l]
        S[name] = v.reshape(1, -1) if v.ndim == 1 else v
    S["sgu_w_st"] = jnp.swapaxes(S["sgu_w_s"], 1, 2)
    S["sgu_bias"] = jnp.repeat(S["sgu_b_s"].T, LANES, axis=1)
    return S


def _layer_fwd(l, h, p_l, S, W, deps=(), hooks=None):
    hooks = hooks or {}

    def after(part, hv):
        return hooks[part](hv) if part in hooks else ()

    h, sv1 = _ffn_fwd(l, h, S, W, "ffn1", deps)
    h, sv2 = _mix_fwd(l, h, S, W, after("ffn1", h))
    h, sv3 = _ffn_fwd(l, h, S, W, "ffn2", after("mix", h))
    h, sv4 = _ple_fwd(l, h, p_l, S, W, after("ffn2", h))
    return h, (sv1, sv2, sv3, sv4)


def _layer_bwd(l, dh, sv, S, W, G, deps=(), hooks=None):
    hooks = hooks or {}

    def after(part, dv):
        return hooks[part](dv) if part in hooks else ()

    SG = {}
    dh = _ple_bwd(l, dh, sv[3], S, W, G, SG, deps)
    dh = _ffn_bwd(l, dh, sv[2], S, W, G, SG, "ffn2")
    dh = _mix_bwd(l, dh, sv[1], S, W, G, SG, after("ffn2", dh))
    dh = _ffn_bwd(l, dh, sv[0], S, W, G, SG, "ffn1", after("mix", dh))
    return dh, SG


HBM = pl.BlockSpec(memory_space=pltpu.HBM)
SEM = pl.BlockSpec(memory_space=pltpu.SEMAPHORE)
SIDE_EFFECT = pltpu.SideEffectType.DATAFLOW_SIDE_EFFECTING


def _place():
    x, y, c = lax.axis_index("x"), lax.axis_index("y"), lax.axis_index("c")
    chips = [(1 - x, y), (x, 1 - y), (1 - x, 1 - y)]
    return x, y, c, chips


def _remote(src, dst, send_sem, recv_sem, to):
    return pltpu.make_async_remote_copy(src_ref=src, dst_ref=dst, send_sem=send_sem, recv_sem=recv_sem,
                                        device_id=to, device_id_type=MESH)


def _split_start(name, plan, bufs, deps):
    count, fn = plan
    n, nd = len(bufs), len(deps)

    def body(*refs):
        send, recv = refs[nd + n], refs[nd + n + 1]
        x, y, c, chips = _place()
        for k, (src, dst, _, to) in enumerate(fn(refs[nd:nd + n], x, y, c, chips)):
            _remote(src, dst, send.at[k], recv.at[k], to).start()
        refs[-1][...] = jnp.zeros_like(refs[-1])

    res = pl.pallas_call(
        body, in_specs=[ANY] * nd + [HBM] * n,
        out_specs=[SEM, SEM] + [HBM] * n + [pl.BlockSpec(memory_space=pltpu.VMEM)],
        out_shape=[pltpu.SemaphoreType.DMA((count,)), pltpu.SemaphoreType.DMA((count,))]
        + [pltpu.HBM(b.shape, b.dtype) for b in bufs] + [jax.ShapeDtypeStruct((8, LANES), F32)],
        input_output_aliases={nd + i: 2 + i for i in range(n)}, name=name,
        compiler_params=pltpu.CompilerParams(has_side_effects=SIDE_EFFECT),
    )(*deps, *[pltpu.with_memory_space_constraint(b, pltpu.HBM) for b in bufs])
    return (res[0], res[1]), list(res[2:2 + n]), res[-1]


def _split_wait(name, plan, sems, bufs, after):
    _, fn = plan
    n = len(bufs)

    def body(*refs):
        send, recv = refs[n], refs[n + 1]
        x, y, c, chips = _place()
        for k, (src, _, land, to) in enumerate(fn(refs[:n], x, y, c, chips)):
            cp = _remote(src, land, send.at[k], recv.at[k], to)
            cp.wait_send()
            cp.wait_recv()

    res = pl.pallas_call(
        body, in_specs=[HBM] * n + [SEM, SEM] + [ANY] * len(after), out_specs=[HBM] * n,
        out_shape=[pltpu.HBM(b.shape, b.dtype) for b in bufs], input_output_aliases={i: i for i in range(n)},
        name=name, compiler_params=pltpu.CompilerParams(has_side_effects=SIDE_EFFECT))(*bufs, *sems, *after)
    return list(res)


def _gather_plans(n):
    def across(b, x, y, c, chips):
        me, out = 2 * x + y, []
        for a in range(n):
            rh = b[a].shape[1] // 2
            mine = b[a].at[me, pl.ds(c * rh, rh)]
            for cx, cy in chips:
                out.append((mine, mine, b[a].at[2 * cx + cy, pl.ds(c * rh, rh)], (cx, cy, c)))
        return out

    def to_sibling(b, x, y, c, chips):
        out = []
        for a in range(n):
            rh = b[a].shape[1] // 2
            for cx, cy in chips:
                piece = b[a].at[2 * cx + cy, pl.ds(c * rh, rh)]
                out.append((piece, piece, b[a].at[2 * cx + cy, pl.ds((1 - c) * rh, rh)], (x, y, 1 - c)))
        return out

    return (3 * n, across), (3 * n, to_sibling)


def _pair_plan(n):
    def fn(b, x, y, c, chips):
        out = []
        for a in range(n):
            rh = b[a].shape[1] // 2
            out.append((b[a].at[:, pl.ds((1 - c) * rh, rh)], b[n + a], b[n + a], (x, y, 1 - c)))
        return out

    return n, fn


def _cross_plan(n):
    def fn(b, x, y, c, chips):
        out = []
        for a in range(n):
            for j, (cx, cy) in enumerate(chips):
                out.append((b[a].at[2 * cx + cy], b[n + a].at[j], b[n + a].at[j], (cx, cy, c)))
        return out

    return 3 * n, fn


def _share_plan(n, l):
    def fn(b, x, y, c, chips):
        out = []
        for a in range(n):
            rh = b[a].shape[1] // 2
            mine = b[a].at[l, pl.ds(c * rh, rh)]
            out.append((mine, mine, b[a].at[l, pl.ds((1 - c) * rh, rh)], (x, y, 1 - c)))
        return out

    return n, fn


def _peers(x, y, c):
    return [(1 - x if m & 4 else x, 1 - y if m & 2 else y, 1 - c if m & 1 else c) for m in range(1, 8)]


def _small_plans():
    def scatter(b, x, y, c, chips):
        return [(b[0].at[4 * px + 2 * py + pc], b[1].at[m], b[1].at[m], (px, py, pc))
                for m, (px, py, pc) in enumerate(_peers(x, y, c))]

    def gather(b, x, y, c, chips):
        mine = b[0].at[4 * x + 2 * y + c]
        return [(mine, mine, b[0].at[4 * px + 2 * py + pc], (px, py, pc)) for px, py, pc in _peers(x, y, c)]

    return (7, scatter), (7, gather)


def _sum_small(v3, got, pos):
    rs = v3.shape[1]
    tm = _tile(rs)
    ins = [(v3, (None, tm, LANES), lambda i, p: (p[2], i, 0))]
    ins += [(got, (None, tm, LANES), lambda i, p, m=m: (m, i, 0)) for m in range(7)]
    return _tiled("sum_small", lambda *t: (((((((t[0] + t[1]) + t[2]) + t[3]) + t[4]) + t[5]) + t[6]) + t[7],),
                  (rs // tm,), pos, ins, [((8, rs, LANES), F32, (None, tm, LANES), lambda i, p: (p[2], i, 0))])[0]


ADD_ROWS = 256


def _multi_tiled(name, fn, pos, groups, in_place=False):
    steps = max(g[2] for g in groups)
    flat_in, in_specs, out_specs, out_shape, counts, dests = [], [], [], [], [], []
    for ins, rows, n_t, (shape, dtype, oidx, dest) in groups:
        for arr, idx in ins:
            flat_in.append(arr)
            in_specs.append(pl.BlockSpec((rows, arr.shape[1]),
                                         lambda i, p, idx=idx, n_t=n_t: (idx(jnp.minimum(i, n_t - 1), p), 0)))
        out_specs.append(pl.BlockSpec((rows, shape[1]),
                                      lambda i, p, oidx=oidx, n_t=n_t: (oidx(jnp.minimum(i, n_t - 1), p), 0)))
        out_shape.append(jax.ShapeDtypeStruct(shape, dtype))
        counts.append((len(ins), n_t))
        dests.append(dest)
    n_in = len(flat_in)
    extra = dests if in_place else []

    def body(_, *refs):
        outs = refs[n_in + len(extra):]
        k = 0
        for (n_a, n_t), o_ref in zip(counts, outs):
            tiles = refs[k:k + n_a]
            k += n_a

            @pl.when(pl.program_id(0) < n_t)
            def _(tiles=tiles, o_ref=o_ref):
                o_ref[...] = fn(*[t[...] for t in tiles]).astype(o_ref.dtype)

    spec = pltpu.PrefetchScalarGridSpec(num_scalar_prefetch=1, grid=(steps,),
                                        in_specs=in_specs + [ANY] * len(extra), out_specs=out_specs)
    return pl.pallas_call(body, grid_spec=spec, out_shape=out_shape,
                          input_output_aliases={1 + n_in + k: k for k in range(len(extra))}, name=name,
                          compiler_params=_params(1))(pos, *flat_in, *extra)


def _add_pair(grads, got, pos):
    groups = []
    for g, q in zip(grads, got):
        nb, R, C = g.shape
        rh = R // 2
        rows = _tile(rh, ADD_ROWS)
        nh = rh // rows
        groups.append(([(g.reshape(nb * R, C), lambda t, p, nh=nh: (t // nh) * 2 * nh + p[1] * nh + t % nh),
                        (q.reshape(nb * rh, C), lambda t, p: t)], rows, nb * nh,
                       ((nb * rh, C), BF, lambda t, p: t, None)))
    res = _multi_tiled("rs_add_pair", lambda u, w: u.astype(F32) + w.astype(F32), pos, groups)
    return [t.reshape(q.shape) for t, q in zip(res, got)]


def _add_chips(parts, slots, reduced, l, pos):
    def add(own, s0, s1, s2):
        return ((own.astype(F32) + s0.astype(F32)) + s1.astype(F32)) + s2.astype(F32)

    groups = []
    for t, s, red in zip(parts, slots, reduced):
        nb, rh, C = t.shape
        L = red.shape[0]
        rows = _tile(rh, ADD_ROWS)
        nh = rh // rows
        ins = [(t.reshape(nb * rh, C), lambda i, p, nh=nh: p[0] * nh + i)]
        ins += [(s.reshape(3 * rh, C), lambda i, p, j=j, nh=nh: j * nh + i) for j in range(3)]
        groups.append((ins, rows, nh, ((L * 2 * rh, C), F32, lambda i, p, nh=nh: l * 2 * nh + p[1] * nh + i,
                                 red.reshape(L * 2 * rh, C))))
    res = _multi_tiled("rs_add_chips", add, pos, groups, in_place=True)
    return [buf.reshape(red.shape) for buf, red in zip(res, reduced)]


def _adamw_math(w, g, m, v):
    m = ADAM_B1 * m + (1.0 - ADAM_B1) * g
    v = ADAM_B2 * v + (1.0 - ADAM_B2) * (g * g)
    m_hat = m / (1.0 - ADAM_B1 ** ADAM_STEP)
    v_hat = v / (1.0 - ADAM_B2 ** ADAM_STEP)
    return -ADAM_LR * (m_hat / (jnp.sqrt(v_hat) + ADAM_EPS) + ADAM_WD * w), m, v


def _adamw(w, g, m, v, lo=0, hi=None, into=None, deps=()):
    L, R, C = w.shape
    hi = L if hi is None else hi
    tr = _tile(R, max(16, ADAM_TILE_ELEMS // C))
    extra = (list(into) if into else []) + list(deps)
    n_alias = 4 if into else 0

    def body(w_ref, g_ref, m_ref, v_ref, *rest):
        go_ref, d_ref, mo_ref, vo_ref = rest[len(extra):]
        gv = g_ref[...]
        d, mn, vn = _adamw_math(w_ref[...], gv, m_ref[...], v_ref[...])
        go_ref[...] = gv
        d_ref[...] = d
        mo_ref[...] = mn
        vo_ref[...] = vn

    spec = pl.BlockSpec((None, tr, C), lambda l, i: (l + lo, i, 0))
    out = jax.ShapeDtypeStruct(w.shape, F32)
    return pl.pallas_call(body, grid=(hi - lo, R // tr), in_specs=[spec] * 4 + [ANY] * len(extra),
                          out_specs=[spec] * 4, out_shape=[out] * 4,
                          input_output_aliases={4 + k: k for k in range(n_alias)}, name="adamw",
                          compiler_params=_params(2))(w, g, m, v, *extra)


def _pack(parts):
    flat = jnp.concatenate([q.reshape(-1, LANES) for q in parts], axis=0)
    return jnp.pad(flat, ((0, -flat.shape[0] % ROW_TILE), (0, 0)))


def _unpack(flat, like):
    out, r = [], 0
    for q in like:
        n = q.size // LANES
        out.append(flat[r:r + n].reshape(q.shape))
        r += n
    return out


def _train_step(a):
    a = dict(a)
    L = a["ffn1_pre_g"].shape[0]
    x, y, c, _ = _place()
    chip = 2 * x + y
    pos = jnp.stack([chip, c, 2 * chip + c]).astype(jnp.int32)
    for name in TRANSPOSED:
        for pre in ("", "m_", "v_"):
            a[pre + name] = jnp.swapaxes(a[pre + name], 1, 2)
    big = [b[0] for b in BIG]
    gathered = big + ["conv_dw_k"]
    n_w, n_g = len(gathered), len(big)

    own = [None] * n_w
    W = {name: [None] * L for name in gathered}
    every = list(range(n_w))
    first, mixer, later = every[:3], every[3:8] + [n_g], every[8:n_g]
    rest = mixer + later

    def cast(i, deps):
        if i == n_g:
            taps = a["conv_dw_k"].reshape(L, CONV_TAPS, LANES)
            return _cast_layers("pad_conv_taps", taps, CONV_PAD, LANES, F32, pos, deps)
        name, _, _, _, rp, cp = BIG[i]
        return _cast_layers("cast_weight", a[name], rp, cp, BF, pos, deps)

    def gather_first(l, ids, tag, deps):
        return _split_start("gather_a%d%s" % (l, tag), _gather_plans(len(ids))[0], [own[i][l] for i in ids], deps)

    def gather_second(l, ids, tag, state, after):
        across, to_sibling = _gather_plans(len(ids))
        bufs = _split_wait("gather_a%d%s_done" % (l, tag), across, state[0], state[1], after)
        return _split_start("gather_b%d%s" % (l, tag), to_sibling, bufs, [])

    def gather_done(l, ids, tag, state, after):
        to_sibling = _gather_plans(len(ids))[1]
        bufs = _split_wait("gather_b%d%s_done" % (l, tag), to_sibling, state[0], state[1], after)
        for i, buf in zip(ids, bufs):
            W[gathered[i]][l] = buf

    for i in first:
        own[i] = cast(i, ())
    state = gather_first(0, first, "f", [])
    for i in rest:
        own[i] = cast(i, (state[2],))
    state = gather_second(0, first, "f", state, [own[i][0] for i in rest])
    gather_done(0, first, "f", state, [])

    parts = {"f": first, "m": mixer, "t": later}
    flying = {}

    def begin(l, part, deps):
        flying[l, part] = gather_first(l, parts[part], part, deps)
        return flying[l, part][2]

    def hand_on(l, part, after):
        flying[l, part] = gather_second(l, parts[part], part, flying[l, part], after)
        return flying[l, part][2]

    def arrive(l, part, after):
        gather_done(l, parts[part], part, flying.pop((l, part)), after)

    def hooks_of(l):
        nxt = l + 1 < L

        def after_ffn1(hv):
            tokens = []
            if l == 0:
                hand_on(0, "m", [hv])
            arrive(l, "m", [hv])
            if l == 0:
                tokens.append(begin(0, "t", [hv]))
            else:
                tokens.append(hand_on(l, "t", [hv]))
            if nxt:
                tokens.append(begin(l + 1, "f", tokens[-1:]))
            return tuple(tokens)

        def after_mix(hv):
            tokens = []
            if l == 0:
                hand_on(0, "t", [hv])
            arrive(l, "t", [hv])
            if nxt:
                tokens.append(hand_on(l + 1, "f", [hv]))
                tokens.append(begin(l + 1, "m", tokens[-1:]))
            return tuple(tokens)

        def after_ffn2(hv):
            tokens = []
            if nxt:
                arrive(l + 1, "f", [hv])
                tokens.append(hand_on(l + 1, "m", [hv]))
                tokens.append(begin(l + 1, "t", tokens[-1:]))
            return tuple(tokens)

        return {"ffn1": after_ffn1, "mix": after_mix, "ffn2": after_ffn2}

    small = [_layer_small(a, l) for l in range(L)]
    h, saved = a["x"][0], []
    deps = (begin(0, "m", []),)
    for l in range(L):
        h, sv = _layer_fwd(l, h, a["p"][l, 0], small[l], W, deps, hooks_of(l))
        saved.append(sv)
        deps = ()

    def loss_fn(yv, t):
        e = yv - t
        return e * (1.0 / D), jnp.sum(e * e, axis=0, keepdims=True)

    dh, lsum = _rowwise("loss", loss_fn, [(h, D, 0), (a["loss_target"][0], D, 0)], [], [(D, F32)], [(1, D)])
    loss = lax.psum(0.5 * jnp.sum(lsum) / D, ("x", "y", "c"))

    G = {name: [jax.ShapeDtypeStruct((N_CHIPS, rp, cp), BF)] * L for name, _, _, _, rp, cp in BIG}
    reduced = [lax.empty((L, rp, cp), F32) for _, _, _, _, rp, cp in BIG]
    small_grads = [None] * L
    whole = list(range(n_g))
    piece_a, piece_b, piece_c = whole[8:], whole[3:8], whole[:3]

    def pair_start(l, ids, tag, deps):
        grads = [G[big[i]][l] for i in ids]
        lands = [lax.empty((N_CHIPS, g.shape[1] // 2, g.shape[2]), BF) for g in grads]
        return _split_start("rs_pair%d%s" % (l, tag), _pair_plan(len(ids)), grads + lands, deps)

    def cross_start(l, ids, tag, state, after):
        n = len(ids)
        bufs = _split_wait("rs_pair%d%s_done" % (l, tag), _pair_plan(n), state[0], state[1], after)
        parts = _add_pair(bufs[:n], bufs[n:], pos)
        lands = [lax.empty((3,) + t.shape[1:], BF) for t in parts]
        return _split_start("rs_cross%d%s" % (l, tag), _cross_plan(n), parts + lands, [])

    def cross_finish(l, ids, tag, state, after, reduced):
        n = len(ids)
        bufs = _split_wait("rs_cross%d%s_done" % (l, tag), _cross_plan(n), state[0], state[1], after)
        reduced = list(reduced)
        for i, r in zip(ids, _add_chips(bufs[:n], bufs[n:], [reduced[i] for i in ids], l, pos)):
            reduced[i] = r
        return reduced

    def share_start(l, reduced):
        return _split_start("rs_share%d" % l, _share_plan(n_g, l), reduced, [])

    def share_done(l, state, after):
        return _split_wait("rs_share%d_done" % l, _share_plan(n_g, l), state[0], state[1], after)

    small_names = SMALL + ("conv_dw_k",)
    scatter, gather = _small_plans()
    totals = [None] * L

    def small_scatter(l, deps):
        packed = _pack([small_grads[l][name] for name in small_names])
        v3 = packed.reshape(8, packed.shape[0] // 8, LANES)
        return _split_start("small_scatter%d" % l, scatter, [v3, lax.empty((7,) + v3.shape[1:], F32)], deps)

    def small_gather(l, state, after):
        bufs = _split_wait("small_scatter%d_done" % l, scatter, state[0], state[1], after)
        return _split_start("small_gather%d" % l, gather, [_sum_small(bufs[0], bufs[1], pos)], [])

    def small_done(l, state, after):
        total = _split_wait("small_gather%d_done" % l, gather, state[0], state[1], after)[0]
        totals[l] = total.reshape(-1, LANES)

    st_pair = st_share = st_small = None
    for l in reversed(range(L)):
        deps = tuple(s[2] for s in (st_pair, st_share, st_small) if s is not None)
        box = {}

        def after_ffn2(dm, l=l, box=box, st_pair=st_pair, st_share=st_share, st_small=st_small):
            out = []
            if st_share is not None:
                box["reduced"] = share_done(l + 2, st_share, [dm])
            if st_small is not None:
                box["small"] = small_gather(l + 1, st_small, [dm])
                out.append(box["small"][2])
            if st_pair is not None:
                box["cross"] = cross_start(l + 1, whole, "", st_pair, [dm])
                out.append(box["cross"][2])
            if l == 0:
                box["pair_a"] = pair_start(0, piece_a, "a", [dm])
                out.append(box["pair_a"][2])
            return tuple(out)

        def after_mix(dm, box=box):
            box["cross_a"] = cross_start(0, piece_a, "a", box["pair_a"], [dm])
            box["pair_b"] = pair_start(0, piece_b, "b", [dm])
            return (box["cross_a"][2], box["pair_b"][2])

        hooks = {"ffn2": after_ffn2, "mix": after_mix} if l == 0 else {"ffn2": after_ffn2}
        dh, small_grads[l] = _layer_bwd(l, dh, saved[l], small[l], W, G, deps, hooks)
        if st_share is not None:
            reduced = box["reduced"]
        if "small" in box:
            small_done(l + 1, box["small"], [dh])
        st_share = None
        if "cross" in box:
            reduced = cross_finish(l + 1, whole, "", box["cross"], [dh], reduced)
            st_share = share_start(l + 1, reduced)
        st_small = small_scatter(l, [dh])
        st_pair = pair_start(l, whole, "", [st_small[2]]) if l else None
    grad_x = dh
    cross_b = cross_start(0, piece_b, "b", box["pair_b"], [st_small[2]])
    cross_c = cross_start(0, piece_c, "c", pair_start(0, piece_c, "c", [cross_b[2]]), [])
    if st_share is not None:
        reduced = share_done(1, st_share, [cross_c[2]])
    upper, token = {}, cross_c[2]
    for k, (name, red) in enumerate(zip(big, reduced)):
        if k == 2:
            st_small = small_gather(0, st_small, [token])
            token = st_small[2]
        if L > 1:
            upper[name] = _adamw(a[name], red, a["m_" + name], a["v_" + name], 1, L, deps=[token])
            token = upper[name][1]
    small_done(0, st_small, [token])
    per_layer = [_unpack(totals[l], [small_grads[l][name] for name in small_names]) for l in range(L)]
    summed = {name: jnp.stack([per_layer[l][k] for l in range(L)]) for k, name in enumerate(small_names)}
    done = [r[1] for r in upper.values()] + [summed[small_names[0]]]
    for ids, tag, state in ((piece_a, "a", box["cross_a"]), (piece_b, "b", cross_b), (piece_c, "c", cross_c)):
        reduced = cross_finish(0, ids, tag, state, done, reduced)
    reduced = share_done(0, share_start(0, reduced), [])
    big_grads = dict(zip(big, reduced))

    grads, deltas, new_m, new_v = {}, {}, {}, {}
    for name in big:
        res = _adamw(a[name], big_grads[name], a["m_" + name], a["v_" + name], 0, 1, upper.get(name))
        if name in TRANSPOSED:
            res = [jnp.swapaxes(r, 1, 2) for r in res]
        grads[name], deltas[name], new_m[name], new_v[name] = res
    taps = lax.dynamic_slice_in_dim(summed["conv_dw_k"], chip * LANES, LANES, axis=2)[:, :CONV_TAPS]
    grads["conv_dw_k"] = taps.reshape(a["conv_dw_k"].shape)
    for name in SMALL:
        grads[name] = summed[name].reshape(a[name].shape)
    shapes = [a[name] for name in small_names]
    res = _adamw(*[_pack([a[pre + name] if pre != "g" else grads[name] for name in small_names])[None]
                   for pre in ("", "g", "m_", "v_")])
    for dst, flat in zip((deltas, new_m, new_v), res[1:]):
        for name, val in zip(small_names, _unpack(flat[0], shapes)):
            dst[name] = val

    return (loss, grad_x[None], *[grads[n] for n in WEIGHTS], *[deltas[n] for n in WEIGHTS],
            *[new_m[n] for n in WEIGHTS], *[new_v[n] for n in WEIGHTS])


def kernel(x, p, ffn1_pre_g, ffn1_w_gate, ffn1_w_up, ffn1_w_down, ffn1_post_g, mix_pre_g, w_in, pool_w, pool_scale, w_pool_out, sgu_ln_g, sgu_ln_b, sgu_w_s, sgu_b_s, w_sgu_out, conv_dw_k, conv_dw_b, conv_ln_g, conv_ln_b, w_conv_out, w_out, mix_post_g, ffn2_pre_g, ffn2_w_gate, ffn2_w_up, ffn2_w_down, ffn2_post_g, ple_w_proj, ple_pre_g, ple_w_gate, ple_post_g, loss_target, m_ffn1_pre_g, m_ffn1_w_gate, m_ffn1_w_up, m_ffn1_w_down, m_ffn1_post_g, m_mix_pre_g, m_w_in, m_pool_w, m_pool_scale, m_w_pool_out, m_sgu_ln_g, m_sgu_ln_b, m_sgu_w_s, m_sgu_b_s, m_w_sgu_out, m_conv_dw_k, m_conv_dw_b, m_conv_ln_g, m_conv_ln_b, m_w_conv_out, m_w_out, m_mix_post_g, m_ffn2_pre_g, m_ffn2_w_gate, m_ffn2_w_up, m_ffn2_w_down, m_ffn2_post_g, m_ple_w_proj, m_ple_pre_g, m_ple_w_gate, m_ple_post_g, v_ffn1_pre_g, v_ffn1_w_gate, v_ffn1_w_up, v_ffn1_w_down, v_ffn1_post_g, v_mix_pre_g, v_w_in, v_pool_w, v_pool_scale, v_w_pool_out, v_sgu_ln_g, v_sgu_ln_b, v_sgu_w_s, v_sgu_b_s, v_w_sgu_out, v_conv_dw_k, v_conv_dw_b, v_conv_ln_g, v_conv_ln_b, v_w_conv_out, v_w_out, v_mix_post_g, v_ffn2_pre_g, v_ffn2_w_gate, v_ffn2_w_up, v_ffn2_w_down, v_ffn2_post_g, v_ple_w_proj, v_ple_pre_g, v_ple_w_gate, v_ple_post_g):
    return _train_step(dict(locals()))
```

```python
import math

import jax
import jax.numpy as jnp
from jax import lax
from jax.experimental import pallas as pl
from jax.experimental.pallas import tpu as pltpu

BF = jnp.bfloat16
F32 = jnp.float32
EPS = 1e-6
D_MODEL = 1024
LANES = 128
SUBLANES = 8
MXU_TILE = 256
N_CHIPS = 4
FFN_SHARD = 704
FFN_SHARD_PAD = 768
POOL_WINDOWS = (2, 4, 8, 16)
SGU_HEADS = 4
CHUNK = 128
CONV_TAPS = 31
CONV_PAD = 32
ROW_TILE = 512
EPI_ROWS = 256
VMEM_LIMIT_BYTES = 56 * 1024 * 1024
ADAM_TILE_ELEMS = 3 * 128 * 1024
ADAM_LR, ADAM_B1, ADAM_B2, ADAM_EPS, ADAM_WD, ADAM_STEP =0.001, 0.9, 0.999, 1e-08, 0.01, 10
MESH = pl.DeviceIdType.MESH
ANY = pl.BlockSpec(memory_space=pl.ANY)

BRANCH = 512
ZB_POOL, ZB_U, ZB_V, ZB_A, ZB_B, ZB_GATES = 0, 1, 2, 3, 4, 5
DZ_HALF = 2816

TRANSPOSED = ("ffn1_w_gate", "ffn1_w_up", "ffn2_w_gate", "ffn2_w_up")
BIG = (
    ("ffn1_w_gate", "row", FFN_SHARD, 1024, FFN_SHARD_PAD, 1024),
    ("ffn1_w_up", "row", FFN_SHARD, 1024, FFN_SHARD_PAD, 1024),
    ("ffn1_w_down", "row", FFN_SHARD, 1024, FFN_SHARD_PAD, 1024),
    ("w_in", "col", 1024, 1408, 1024, 1408),
    ("w_pool_out", "col", 512, 256, 512, 256),
    ("w_sgu_out", "col", 512, 256, 512, 256),
    ("w_conv_out", "col", 512, 256, 512, 256),
    ("w_out", "row", 256, 1024, 256, 1024),
    ("ffn2_w_gate", "row", FFN_SHARD, 1024, FFN_SHARD_PAD, 1024),
    ("ffn2_w_up", "row", FFN_SHARD, 1024, FFN_SHARD_PAD, 1024),
    ("ffn2_w_down", "row", FFN_SHARD, 1024, FFN_SHARD_PAD, 1024),
    ("ple_w_proj", "col", 256, 256, 256, 256),
    ("ple_w_gate", "row", 256, 1024, 256, 1024),
)
SMALL = ("ffn1_pre_g", "ffn1_post_g", "mix_pre_g", "pool_w", "pool_scale", "sgu_ln_g", "sgu_ln_b", "sgu_w_s",
         "sgu_b_s", "conv_dw_b", "conv_ln_g", "conv_ln_b", "mix_post_g", "ffn2_pre_g", "ffn2_post_g",
         "ple_pre_g", "ple_post_g")
WEIGHTS = ("ffn1_pre_g", "ffn1_w_gate", "ffn1_w_up", "ffn1_w_down", "ffn1_post_g", "mix_pre_g", "w_in", "pool_w",
           "pool_scale", "w_pool_out", "sgu_ln_g", "sgu_ln_b", "sgu_w_s", "sgu_b_s", "w_sgu_out", "conv_dw_k",
           "conv_dw_b", "conv_ln_g", "conv_ln_b", "w_conv_out", "w_out", "mix_post_g", "ffn2_pre_g", "ffn2_w_gate",
           "ffn2_w_up", "ffn2_w_down", "ffn2_post_g", "ple_w_proj", "ple_pre_g", "ple_w_gate", "ple_post_g")


def _params(n_grid):
    return pltpu.CompilerParams(dimension_semantics=("arbitrary",) * n_grid, vmem_limit_bytes=VMEM_LIMIT_BYTES)


def _tile(n, cap=ROW_TILE):
    for t in range(min(cap, n) - min(cap, n) % 16, 0, -16):
        if n % t == 0:
            return t
    return n


def _sigmoid(x):
    return 0.5 * jnp.tanh(0.5 * x) + 0.5


def _silu_and_grad(x):
    s = _sigmoid(x)
    return x * s, s * (1.0 + x * (1.0 - s))


def _gelu_and_grad(x):
    cdf = 0.5 * (1.0 + lax.erf(x * (1.0 / math.sqrt(2.0))))
    pdf = jnp.exp(-0.5 * x * x) * (1.0 / math.sqrt(2.0 * math.pi))
    return x * cdf, cdf + x * pdf


def _rms_fwd(x, g):
    return x * lax.rsqrt(jnp.mean(x * x, axis=-1, keepdims=True) + EPS) * g


def _rms_bwd(x, g, dy):
    r = lax.rsqrt(jnp.mean(x * x, axis=-1, keepdims=True) + EPS)
    xh = x * r
    dxh = dy * g
    dx = r * (dxh - xh * jnp.mean(dxh * xh, axis=-1, keepdims=True))
    return dx, jnp.sum(dy * xh, axis=0, keepdims=True)


def _ln_stats(x):
    xc = x - jnp.mean(x, axis=-1, keepdims=True)
    r = lax.rsqrt(jnp.mean(xc * xc, axis=-1, keepdims=True) + EPS)
    return xc * r, r


def _ln_bwd(xh, r, g, dy):
    dxh = dy * g
    dx = r * (dxh - jnp.mean(dxh, axis=-1, keepdims=True) - xh * jnp.mean(dxh * xh, axis=-1, keepdims=True))
    return dx, jnp.sum(dy * xh, axis=0, keepdims=True), jnp.sum(dy, axis=0, keepdims=True)


def _rowwise(name, fn, rows, consts, outs, accs=(), tm=ROW_TILE, deps=()):
    T = rows[0][0].shape[-2]
    tm = _tile(T, tm)
    n_in, n_o, n_dep = len(rows) + len(consts), len(outs), len(deps)

    def body(*refs):
        refs = refs[n_dep:]
        res = fn(*[r[...] for r in refs[:n_in]])
        for ref, val in zip(refs[n_in:n_in + n_o], res[:n_o]):
            ref[...] = val.astype(ref.dtype)
        acc_refs = refs[n_in + n_o:]
        if acc_refs:
            @pl.when(pl.program_id(0) == 0)
            def _():
                for ref, val in zip(acc_refs, res[n_o:]):
                    ref[...] = val

            @pl.when(pl.program_id(0) != 0)
            def _():
                for ref, val in zip(acc_refs, res[n_o:]):
                    ref[...] += val

    in_specs = [ANY] * n_dep
    for row in rows:
        w, cb = row[1], row[2]
        if len(row) == 4:
            in_specs.append(pl.BlockSpec((None, tm, w), lambda i, cb=cb, ld=row[3]: (ld, i, cb)))
        else:
            in_specs.append(pl.BlockSpec((tm, w), lambda i, cb=cb: (i, cb)))
    in_specs += [pl.BlockSpec(c.shape, lambda i: (0, 0)) for c in consts]
    out_specs = [pl.BlockSpec((tm, w), lambda i: (i, 0)) for w, _ in outs]
    out_specs += [pl.BlockSpec(s, lambda i: (0, 0)) for s in accs]
    out_shape = [jax.ShapeDtypeStruct((T, w), dt) for w, dt in outs]
    out_shape += [jax.ShapeDtypeStruct(s, F32) for s in accs]
    return pl.pallas_call(body, grid=(T // tm,), in_specs=in_specs, out_specs=out_specs, out_shape=out_shape,
                          name=name, compiler_params=_params(1))(*deps, *[r[0] for r in rows], *consts)


def _tiled(name, fn, grid, pos, ins, outs):
    n_in = len(ins)

    def body(_, *refs):
        res = fn(*[r[...] for r in refs[:n_in]])
        for ref, val in zip(refs[n_in:], res):
            ref[...] = val.astype(ref.dtype)

    spec = pltpu.PrefetchScalarGridSpec(
        num_scalar_prefetch=1, grid=grid, in_specs=[pl.BlockSpec(bs, im) for _, bs, im in ins],
        out_specs=[pl.BlockSpec(bs, im) for _, _, bs, im in outs])
    return pl.pallas_call(body, grid_spec=spec, out_shape=[jax.ShapeDtypeStruct(s, d) for s, d, _, _ in outs],
                          name=name, compiler_params=_params(len(grid)))(pos, *[a for a, _, _ in ins])


def _cast_layers(name, w, rp, cp, dtype, pos, deps=()):
    L, r, c = w.shape

    def body(_, w_ref, *rest):
        for k, o_ref in enumerate(rest[len(deps):]):
            @pl.when(pl.program_id(0) == k)
            def _(o_ref=o_ref):
                if (rp, cp) != (r, c):
                    o_ref[...] = jnp.zeros_like(o_ref)
                    o_ref[pl.ds(0, r), pl.ds(0, c)] = w_ref[...].astype(dtype)
                else:
                    o_ref[...] = w_ref[...].astype(dtype)

    spec = pltpu.PrefetchScalarGridSpec(
        num_scalar_prefetch=1, grid=(L,),
        in_specs=[pl.BlockSpec((None, r, c), lambda l, p: (l, 0, 0))] + [ANY] * len(deps),
        out_specs=[pl.BlockSpec((None, rp, cp), lambda l, p: (p[0], 0, 0))] * L)
    return pl.pallas_call(body, grid_spec=spec, out_shape=[jax.ShapeDtypeStruct((N_CHIPS, rp, cp), dtype)] * L,
                          name=name, compiler_params=_params(1))(pos, w, *deps)


_NN = (((1,), (0,)), ((), ()))
_NT = (((1,), (1,)), ((), ()))
_TN = (((0,), (0,)), ((), ()))


def _mm_tn(name, a, dy, buf, l, a_blocked, tk=ROW_TILE, first=0):
    T = a.shape[0]
    nb, R, C = buf[l].shape
    extra = [buf[l]] if first else []

    def body(a_ref, dy_ref, *rest):
        rest[-1][...] = lax.dot_general(a_ref[...].astype(BF), dy_ref[...].astype(BF), _TN,
                                        preferred_element_type=F32).astype(BF)

    if a_blocked:
        grid = (nb,)
        in_specs = [pl.BlockSpec((T, R), lambda b: (0, b)), pl.BlockSpec((T, C), lambda b: (0, 0))]
        out_specs = pl.BlockSpec((None, R, C), lambda b: (b, 0, 0))
    else:
        tk = min(tk, R)
        grid = (dy.shape[1] // C, R // tk)
        in_specs = [pl.BlockSpec((T, tk), lambda b, k: (0, k)), pl.BlockSpec((T, C), lambda b, k: (0, b))]
        out_specs = pl.BlockSpec((None, tk, C), lambda b, k: (b + first, k, 0))
    buf = list(buf)
    buf[l] = pl.pallas_call(body, grid=grid, in_specs=in_specs + [ANY] * len(extra), out_specs=out_specs,
                            out_shape=jax.ShapeDtypeStruct((nb, R, C), BF),
                            input_output_aliases={2: 0} if extra else {}, name=name,
                            compiler_params=_params(len(grid)))(a, dy, *extra)
    return buf


def _acc_rows(ref, val, first):
    @pl.when(first)
    def _():
        ref[...] = val

    @pl.when(jnp.logical_not(first))
    def _():
        ref[...] += val


def _norm_mm(name, h, g, ws, trans_w, act=False, deps=(), tm=2 * ROW_TILE):
    T = h.shape[0]
    nb, r, cc = ws[0].shape
    bo = r if trans_w else cc
    tm = _tile(T, tm)
    n_w, n_dep = len(ws), len(deps)

    def body(*refs):
        refs = refs[n_dep:]
        h_ref, g_ref, w_refs = refs[0], refs[1], refs[2:2 + n_w]
        n_ref, o_refs, n_s = refs[2 + n_w], refs[3 + n_w:3 + 2 * n_w], refs[-1]

        @pl.when(pl.program_id(1) == 0)
        def _():
            n = _rms_fwd(h_ref[...].astype(F32), g_ref[...]).astype(BF)
            n_s[...] = n
            n_ref[...] = n

        n = n_s[...]
        prods = []
        for w_ref, o_ref in zip(w_refs, o_refs):
            prods.append(lax.dot_general(n, w_ref[...], _NT if trans_w else _NN,
                                         preferred_element_type=F32).astype(BF))
            o_ref[...] = prods[-1]
        if act:
            refs[3 + 2 * n_w][...] = (_silu_and_grad(prods[0].astype(F32))[0] * prods[1].astype(F32)).astype(BF)

    wide = pl.BlockSpec((tm, bo), lambda i, b: (i, b))
    n_out = n_w + (1 if act else 0)
    return pl.pallas_call(
        body, grid=(T // tm, nb),
        in_specs=[ANY] * n_dep + [pl.BlockSpec((tm, D_MODEL), lambda i, b: (i, 0)),
                                  pl.BlockSpec(g.shape, lambda i, b: (0, 0))]
        + [pl.BlockSpec((None, r, cc), lambda i, b: (b, 0, 0))] * n_w,
        out_specs=[pl.BlockSpec((tm, D_MODEL), lambda i, b: (i, 0))] + [wide] * n_out,
        out_shape=[jax.ShapeDtypeStruct((T, D_MODEL), BF)] + [jax.ShapeDtypeStruct((T, nb * bo), BF)] * n_out,
        scratch_shapes=[pltpu.VMEM((tm, D_MODEL), BF)], name=name, compiler_params=_params(2))(*deps, h, g, *ws)


def _mm_res(name, x, w3, h, g, coef, tm=ROW_TILE):
    T, kx = x.shape
    w2 = w3.reshape(kx, D_MODEL)
    tm = _tile(T, tm)

    def body(x_ref, w_ref, h_ref, g_ref, f_ref, o_ref):
        f = jnp.dot(x_ref[...], w_ref[...], preferred_element_type=F32).astype(BF)
        f_ref[...] = f
        o_ref[...] = h_ref[...] + coef * _rms_fwd(f.astype(F32), g_ref[...])

    row = pl.BlockSpec((tm, D_MODEL), lambda i: (i, 0))
    return pl.pallas_call(
        body, grid=(T // tm,),
        in_specs=[pl.BlockSpec((tm, kx), lambda i: (i, 0)), pl.BlockSpec(w2.shape, lambda i: (0, 0)), row,
                  pl.BlockSpec(g.shape, lambda i: (0, 0))],
        out_specs=[row, row],
        out_shape=[jax.ShapeDtypeStruct((T, D_MODEL), BF), jax.ShapeDtypeStruct((T, D_MODEL), F32)],
        name=name, compiler_params=_params(1))(x, w2, h, g)


def _resbwd_mm(name, dh, f, g, coef, w3, trans_w, act=None, deps=(), tm=2 * ROW_TILE):
    T = dh.shape[0]
    nb, r, cc = w3.shape
    bo = r if trans_w else cc
    tm = _tile(T, tm)
    n_dep, n_act = len(deps), 3 if act else 0
    n_i = T // tm

    def body(*refs):
        refs = refs[n_dep:]
        dh_ref, f_ref, g_ref, w_ref = refs[:4]
        df_ref, dg_ref = refs[4 + n_act], refs[5 + n_act]
        df_s = refs[-2] if act else refs[-1]
        i, b = pl.program_id(0), pl.program_id(1)

        @pl.when(b == 0)
        def _():
            dg = jnp.zeros((1, D_MODEL), F32)
            for c in range(tm // EPI_ROWS):
                rows = slice(c * EPI_ROWS, (c + 1) * EPI_ROWS)
                dx, dg_c = _rms_bwd(f_ref[rows, :].astype(F32), g_ref[...], coef * dh_ref[rows, :])
                df_s[rows, :] = dx.astype(BF)
                df_ref[rows, :] = dx.astype(BF)
                dg = dg + dg_c
            _acc_rows(dg_ref, dg, i == 0)

        if act:
            for j in range(bo // MXU_TILE):
                cols = slice(j * MXU_TILE, (j + 1) * MXU_TILE)
                prod = lax.dot_general(df_s[...], w_ref[cols, :], _NT, preferred_element_type=F32)
                val, grad = _silu_and_grad(refs[4][:, cols])
                prod = prod.astype(BF)
                refs[6 + n_act][:, cols] = prod * refs[5][:, cols] * grad
                refs[7 + n_act][:, cols] = prod * val
            acc = refs[-1]
            part = lax.dot_general(refs[6][...], df_s[...], _TN, preferred_element_type=F32)

            @pl.when(i == 0)
            def _():
                acc[b] = part

            @pl.when(i != 0)
            def _():
                acc[b] += part

            @pl.when(i == n_i - 1)
            def _():
                refs[8 + n_act][...] = acc[b].astype(BF)
        else:
            refs[6][...] = lax.dot_general(df_s[...], w_ref[...], _NT if trans_w else _NN,
                                           preferred_element_type=F32).astype(BF)

    row = pl.BlockSpec((tm, D_MODEL), lambda i, b: (i, 0))
    wide = pl.BlockSpec((tm, bo), lambda i, b: (i, b))
    vec = pl.BlockSpec((1, D_MODEL), lambda i, b: (0, 0))
    out_specs = [row, vec] + [wide] * (2 if act else 1)
    out_shape = [jax.ShapeDtypeStruct((T, D_MODEL), BF), jax.ShapeDtypeStruct((1, D_MODEL), F32)]
    out_shape += [jax.ShapeDtypeStruct((T, nb * bo), BF)] * (2 if act else 1)
    scratch = [pltpu.VMEM((tm, D_MODEL), BF)]
    if act:
        out_specs.append(pl.BlockSpec((None, r, cc), lambda i, b: (jnp.where(i == n_i - 1, b, 0), 0, 0)))
        out_shape.append(jax.ShapeDtypeStruct((nb, r, cc), BF))
        scratch.append(pltpu.VMEM((nb, r, cc), F32))
    return pl.pallas_call(
        body, grid=(n_i, nb),
        in_specs=[ANY] * n_dep + [row, row, vec, pl.BlockSpec((None, r, cc), lambda i, b: (b, 0, 0))] + [wide] * n_act,
        out_specs=out_specs, out_shape=out_shape, scratch_shapes=scratch, name=name,
        compiler_params=_params(2))(*deps, dh, f, g, w3, *(act or ()))


def _dn_prenorm(name, xs, ws, trans_w, dh, h, g, tm=2 * ROW_TILE):
    T = dh.shape[0]
    chained = not isinstance(ws, (list, tuple))
    ws = [ws] if chained else list(ws)
    _, r, cc = ws[0].shape
    bw = cc if trans_w else r
    per_x = xs[0].shape[1] // bw
    nb = per_x * len(xs) if chained else per_x
    tm = _tile(T, tm)
    n_x, n_w = len(xs), len(ws)

    def body(*refs):
        x_refs, w_refs = refs[:n_x], refs[n_x:n_x + n_w]
        dh_ref, h_ref, g_ref, o_ref, dg_ref, acc = refs[n_x + n_w:]
        i, b = pl.program_id(0), pl.program_id(1)

        @pl.when(b == 0)
        def _():
            acc[...] = jnp.zeros_like(acc)

        def add(x_ref, w_ref):
            acc[...] += lax.dot_general(x_ref[...], w_ref[...], _NT if trans_w else _NN, preferred_element_type=F32)

        if chained:
            for k, x_ref in enumerate(x_refs):
                pl.when(b // per_x == k)(lambda x_ref=x_ref: add(x_ref, w_refs[0]))
        else:
            for x_ref, w_ref in zip(x_refs, w_refs):
                add(x_ref, w_ref)

        @pl.when(b == nb - 1)
        def _():
            dg = jnp.zeros((1, D_MODEL), F32)
            for c in range(tm // EPI_ROWS):
                rows = slice(c * EPI_ROWS, (c + 1) * EPI_ROWS)
                dx, dg_c = _rms_bwd(h_ref[rows, :], g_ref[...], acc[rows, :])
                o_ref[rows, :] = dh_ref[rows, :] + dx
                dg = dg + dg_c
            _acc_rows(dg_ref, dg, i == 0)

    row = pl.BlockSpec((tm, D_MODEL), lambda i, b: (i, 0))
    vec = pl.BlockSpec((1, D_MODEL), lambda i, b: (0, 0))
    if chained:
        x_specs = [pl.BlockSpec((tm, bw), lambda i, b, k=k: (i, jnp.clip(b - k * per_x, 0, per_x - 1)))
                   for k in range(n_x)]
    else:
        x_specs = [pl.BlockSpec((tm, bw), lambda i, b: (i, b))] * n_x
    return pl.pallas_call(
        body, grid=(T // tm, nb),
        in_specs=x_specs + [pl.BlockSpec((None, r, cc), lambda i, b: (b, 0, 0))] * n_w + [row, row, vec],
        out_specs=[row, vec],
        out_shape=[jax.ShapeDtypeStruct((T, D_MODEL), F32), jax.ShapeDtypeStruct((1, D_MODEL), F32)],
        scratch_shapes=[pltpu.VMEM((tm, D_MODEL), F32)], name=name,
        compiler_params=_params(2))(*xs, *ws, dh, h, g)


def _pool_apply(x, win, row):
    s, k = x, 1
    while k < win:
        s = s + jnp.where(row >= k, pltpu.roll(s, k, 0), 0.0)
        k *= 2
    return s / jnp.minimum(row + 1, win).astype(F32) - x


def _pool_apply_t(dp, win, row):
    T = dp.shape[0]
    s, k = dp / jnp.minimum(row + 1, win).astype(F32), 1
    while k < win:
        s = s + jnp.where(row < T - k, pltpu.roll(s, T - k, 0), 0.0)
        k *= 2
    return s - dp


def _pool_fwd(z, w, scale):
    T = z.shape[0]

    def body(z_ref, w_ref, s_ref, o_ref):
        row = lax.broadcasted_iota(jnp.int32, (T, LANES), 0)
        for gi, win in enumerate(POOL_WINDOWS):
            cols = pl.ds(gi * LANES, LANES)
            pooled = _pool_apply(z_ref[:, cols].astype(F32), win, row)
            y = jnp.dot(pooled.astype(BF), w_ref[gi].astype(BF), preferred_element_type=F32)
            o_ref[:, cols] = (y * s_ref[:, cols]).astype(o_ref.dtype)

    return pl.pallas_call(
        body, grid=(1,),
        in_specs=[pl.BlockSpec((T, BRANCH), lambda i: (0, ZB_POOL)), pl.BlockSpec(w.shape, lambda i: (0, 0, 0)),
                  pl.BlockSpec(scale.shape, lambda i: (0, 0))],
        out_specs=pl.BlockSpec((T, BRANCH), lambda i: (0, 0)), out_shape=jax.ShapeDtypeStruct((T, BRANCH), BF),
        name="pool_fwd", compiler_params=_params(1))(z, w, scale)


def _pool_bwd(dr, z, w, scale):
    T = z.shape[0]

    def body(dr_ref, z_ref, w_ref, s_ref, dz_ref, dw_ref, ds_ref):
        row = lax.broadcasted_iota(jnp.int32, (T, LANES), 0)
        for gi, win in enumerate(POOL_WINDOWS):
            cols = pl.ds(gi * LANES, LANES)
            pooled = _pool_apply(z_ref[:, cols].astype(F32), win, row).astype(BF)
            wg = w_ref[gi].astype(BF)
            y = jnp.dot(pooled, wg, preferred_element_type=F32)
            d = dr_ref[:, cols].astype(F32)
            ds_ref[:, cols] = jnp.sum(d * y, axis=0, keepdims=True)
            dy = (d * s_ref[:, cols]).astype(BF)
            dw_ref[gi] = lax.dot_general(pooled, dy, _TN, preferred_element_type=F32)
            dpooled = lax.dot_general(dy, wg, _NT, preferred_element_type=F32)
            dz_ref[:, cols] = _pool_apply_t(dpooled, win, row).astype(dz_ref.dtype)

    return pl.pallas_call(
        body, grid=(1,),
        in_specs=[pl.BlockSpec((T, BRANCH), lambda i: (0, 0)), pl.BlockSpec((T, BRANCH), lambda i: (0, ZB_POOL)),
                  pl.BlockSpec(w.shape, lambda i: (0, 0, 0)), pl.BlockSpec(scale.shape, lambda i: (0, 0))],
        out_specs=[pl.BlockSpec((T, BRANCH), lambda i: (0, 0)), pl.BlockSpec(w.shape, lambda i: (0, 0, 0)),
                   pl.BlockSpec(scale.shape, lambda i: (0, 0))],
        out_shape=[jax.ShapeDtypeStruct((T, BRANCH), BF), jax.ShapeDtypeStruct(w.shape, F32),
                   jax.ShapeDtypeStruct(scale.shape, F32)],
        name="pool_bwd", compiler_params=_params(1))(dr, z, w, scale)


def _tril(transposed=False):
    r = lax.broadcasted_iota(jnp.int32, (CHUNK, CHUNK), 0)
    c = lax.broadcasted_iota(jnp.int32, (CHUNK, CHUNK), 1)
    return c >= r if transposed else r >= c


def _sgu_fwd(z, ln_g, ln_b, w_s, bias):
    T = z.shape[0]
    tm = _tile(T)

    def body(zu_ref, zv_ref, g_ref, b_ref, w_ref, bias_ref, o_ref):
        gu, _ = _gelu_and_grad(zu_ref[...].astype(F32))
        gv, _ = _gelu_and_grad(zv_ref[...].astype(F32))
        xh, _ = _ln_stats(gv)
        v16 = (xh * g_ref[...] + b_ref[...]).astype(BF)
        tri = _tril()
        for h in range(SGU_HEADS):
            cols = slice(h * LANES, (h + 1) * LANES)
            wh = jnp.where(tri, w_ref[h], 0.0).astype(BF)
            for c in range(tm // CHUNK):
                rows = slice(c * CHUNK, (c + 1) * CHUNK)
                s = jnp.dot(wh, v16[rows, cols], preferred_element_type=F32) + bias_ref[:, cols]
                o_ref[rows, cols] = (gu[rows, cols] * s).astype(o_ref.dtype)

    small = [pl.BlockSpec(a.shape, lambda i, n=a.ndim: (0,) * n) for a in (ln_g, ln_b, w_s, bias)]
    return pl.pallas_call(
        body, grid=(T // tm,),
        in_specs=[pl.BlockSpec((tm, BRANCH), lambda i: (i, ZB_U)), pl.BlockSpec((tm, BRANCH), lambda i: (i, ZB_V))] + small,
        out_specs=pl.BlockSpec((tm, BRANCH), lambda i: (i, 0)), out_shape=jax.ShapeDtypeStruct((T, BRANCH), BF),
        name="sgu_fwd", compiler_params=_params(1))(z, z, ln_g, ln_b, w_s, bias)


def _sgu_bwd(dr, z, ln_g, ln_b, w_s, w_st, bias):
    T = z.shape[0]
    tm = _tile(T)
    n_steps = T // tm

    def body(dr_ref, zu_ref, zv_ref, g_ref, b_ref, w_ref, wt_ref, bias_ref,
             dzu_ref, dzv_ref, dg_ref, db_ref, dw_ref, dbias_ref, dgu_s, dv_s):
        i = pl.program_id(0)

        @pl.when(i == 0)
        def _():
            dg_ref[...] = jnp.zeros_like(dg_ref)
            db_ref[...] = jnp.zeros_like(db_ref)
            dw_ref[...] = jnp.zeros_like(dw_ref)
            dbias_ref[...] = jnp.zeros_like(dbias_ref)

        zu = zu_ref[...].astype(F32)
        zv = zv_ref[...].astype(F32)
        gu, gu_grad = _gelu_and_grad(zu)
        gv, gv_grad = _gelu_and_grad(zv)
        xh, r = _ln_stats(gv)
        v16 = (xh * g_ref[...] + b_ref[...]).astype(BF)
        dr = dr_ref[...].astype(F32)
        tri = _tril()
        for h in range(SGU_HEADS):
            cols = slice(h * LANES, (h + 1) * LANES)
            wh = jnp.where(tri, w_ref[h], 0.0).astype(BF)
            wht = jnp.where(_tril(transposed=True), wt_ref[h], 0.0).astype(BF)
            for c in range(tm // CHUNK):
                rows = slice(c * CHUNK, (c + 1) * CHUNK)
                v_blk = v16[rows, cols]
                s = jnp.dot(wh, v_blk, preferred_element_type=F32) + bias_ref[:, cols]
                ds = dr[rows, cols] * gu[rows, cols]
                dgu_s[rows, cols] = dr[rows, cols] * s
                ds16 = ds.astype(BF)
                dw_ref[h] += jnp.where(tri, lax.dot_general(ds16, v_blk, _NT, preferred_element_type=F32), 0.0)
                dv_s[rows, cols] = jnp.dot(wht, ds16, preferred_element_type=F32)
                dbias_ref[:, cols] += ds
        dzu_ref[...] = (dgu_s[...] * gu_grad).astype(dzu_ref.dtype)
        dgv, dg, db = _ln_bwd(xh, r, g_ref[...], dv_s[...])
        dzv_ref[...] = (dgv * gv_grad).astype(dzv_ref.dtype)
        dg_ref[...] += dg
        db_ref[...] += db

        @pl.when(i == n_steps - 1)
        def _():
            for h in range(SGU_HEADS):
                cols = slice(h * LANES, (h + 1) * LANES)
                tot = jnp.sum(dbias_ref[:, cols], axis=1, keepdims=True)
                dbias_ref[:, cols] = jnp.broadcast_to(tot, (CHUNK, LANES))

    small = (ln_g, ln_b, w_s, w_st, bias)
    small_specs = [pl.BlockSpec(a.shape, lambda i, n=a.ndim: (0,) * n) for a in small]
    return pl.pallas_call(
        body, grid=(n_steps,),
        in_specs=[pl.BlockSpec((tm, BRANCH), lambda i: (i, 0)), pl.BlockSpec((tm, BRANCH), lambda i: (i, ZB_U)),
                  pl.BlockSpec((tm, BRANCH), lambda i: (i, ZB_V))] + small_specs,
        out_specs=[pl.BlockSpec((tm, BRANCH), lambda i: (i, 0)), pl.BlockSpec((tm, BRANCH), lambda i: (i, 0)),
                   pl.BlockSpec((1, BRANCH), lambda i: (0, 0)), pl.BlockSpec((1, BRANCH), lambda i: (0, 0)),
                   pl.BlockSpec(w_s.shape, lambda i: (0, 0, 0)), pl.BlockSpec(bias.shape, lambda i: (0, 0))],
        out_shape=[jax.ShapeDtypeStruct((T, BRANCH), BF), jax.ShapeDtypeStruct((T, BRANCH), BF),
                   jax.ShapeDtypeStruct((1, BRANCH), F32), jax.ShapeDtypeStruct((1, BRANCH), F32),
                   jax.ShapeDtypeStruct(w_s.shape, F32), jax.ShapeDtypeStruct(bias.shape, F32)],
        scratch_shapes=[pltpu.VMEM((tm, BRANCH), F32), pltpu.VMEM((tm, BRANCH), F32)],
        name="sgu_bwd", compiler_params=_params(1))(dr, z, z, ln_g, ln_b, w_s, w_st, bias)


def _conv_fwd(z, convk, l, bias):
    T = z.shape[0]

    def body(za_ref, zb_ref, k_ref, b_ref, o_ref):
        xg = za_ref[...].astype(F32) * _sigmoid(zb_ref[...].astype(F32))
        xp = jnp.concatenate([jnp.zeros((CONV_PAD, LANES), F32), xg], axis=0)
        kw = k_ref[...]
        acc = jnp.broadcast_to(b_ref[...], (T, LANES))
        for s in range(SUBLANES):
            xs = xp if s == 0 else pltpu.roll(xp, s, 0)
            for q in range(CONV_PAD // SUBLANES):
                k = CONV_TAPS - 1 - (SUBLANES * q + s)
                if k >= 0:
                    lo = CONV_PAD - SUBLANES * q
                    acc = acc + kw[k:k + 1, :] * xs[lo:lo + T, :]
        o_ref[...] = acc.astype(o_ref.dtype)

    return pl.pallas_call(
        body, grid=(4,),
        in_specs=[pl.BlockSpec((T, LANES), lambda g: (0, 4 * ZB_A + g)),
                  pl.BlockSpec((T, LANES), lambda g: (0, 4 * ZB_B + g)),
                  pl.BlockSpec((None, CONV_PAD, LANES), lambda g: (g, 0, 0)),
                  pl.BlockSpec((1, LANES), lambda g: (0, g))],
        out_specs=pl.BlockSpec((T, LANES), lambda g: (0, g)), out_shape=jax.ShapeDtypeStruct((T, BRANCH), BF),
        name="conv_fwd", compiler_params=_params(1))(z, z, convk[l], bias)


def _conv_bwd(dy, z, convk, l):
    T = z.shape[0]

    def body(dy_ref, za_ref, zb_ref, k_ref, dza_ref, dzb_ref, dk_ref, db_ref):
        a = za_ref[...].astype(F32)
        sg = _sigmoid(zb_ref[...].astype(F32))
        d = dy_ref[...].astype(F32)
        kw = k_ref[...]
        xp = jnp.concatenate([jnp.zeros((CONV_PAD, LANES), F32), a * sg], axis=0)
        dp = jnp.concatenate([d, jnp.zeros((CONV_PAD, LANES), F32)], axis=0)
        dxg = jnp.zeros((T, LANES), F32)
        dk_ref[...] = jnp.zeros_like(dk_ref)
        for s in range(SUBLANES):
            xs = xp if s == 0 else pltpu.roll(xp, s, 0)
            ds = dp if s == 0 else pltpu.roll(dp, T + CONV_PAD - s, 0)
            for q in range(CONV_PAD // SUBLANES):
                k = CONV_TAPS - 1 - (SUBLANES * q + s)
                if k >= 0:
                    lo = CONV_PAD - SUBLANES * q
                    dk_ref[k:k + 1, :] = jnp.sum(d * xs[lo:lo + T, :], axis=0, keepdims=True)
                    dxg = dxg + kw[k:k + 1, :] * ds[SUBLANES * q:SUBLANES * q + T, :]
        db_ref[...] = jnp.sum(d, axis=0, keepdims=True)
        dza_ref[...] = (dxg * sg).astype(dza_ref.dtype)
        dzb_ref[...] = (dxg * a * sg * (1.0 - sg)).astype(dzb_ref.dtype)

    col = pl.BlockSpec((T, LANES), lambda g: (0, g))
    return pl.pallas_call(
        body, grid=(4,),
        in_specs=[col, pl.BlockSpec((T, LANES), lambda g: (0, 4 * ZB_A + g)),
                  pl.BlockSpec((T, LANES), lambda g: (0, 4 * ZB_B + g)),
                  pl.BlockSpec((None, CONV_PAD, LANES), lambda g: (g, 0, 0))],
        out_specs=[col, col, pl.BlockSpec((CONV_PAD, LANES), lambda g: (0, g)),
                   pl.BlockSpec((1, LANES), lambda g: (0, g))],
        out_shape=[jax.ShapeDtypeStruct((T, BRANCH), BF), jax.ShapeDtypeStruct((T, BRANCH), BF),
                   jax.ShapeDtypeStruct((CONV_PAD, BRANCH), F32), jax.ShapeDtypeStruct((1, BRANCH), F32)],
        name="conv_bwd", compiler_params=_params(1))(dy, z, z, convk[l])


D = D_MODEL


def _ffn_fwd(l, h, S, W, pre, deps=()):
    n, gp, u, a = _norm_mm("ffn_in", h, S[pre + "_pre_g"], [W[pre + "_w_gate"][l], W[pre + "_w_up"][l]], True,
                           act=True, deps=deps)
    f, out = _mm_res("ffn_out", a, W[pre + "_w_down"][l], h, S[pre + "_post_g"], 0.5)
    return out, dict(h=h, n=n, gp=gp, u=u, a=a, f=f)


def _ffn_bwd(l, dh, sv, S, W, G, SG, pre, deps=()):
    df, SG[pre + "_post_g"], dgp, du, dwd = _resbwd_mm(
        "ffn_bwd_act", dh, sv["f"], S[pre + "_post_g"], 0.5, W[pre + "_w_down"][l], True,
        act=(sv["gp"], sv["u"], sv["a"]), deps=deps, tm=ROW_TILE)
    G[pre + "_w_down"] = G[pre + "_w_down"][:l] + [dwd] + G[pre + "_w_down"][l + 1:]
    G[pre + "_w_gate"] = _mm_tn("ffn_dw_gate", dgp, sv["n"], G[pre + "_w_gate"], l, True)
    G[pre + "_w_up"] = _mm_tn("ffn_dw_up", du, sv["n"], G[pre + "_w_up"], l, True)
    dh_in, SG[pre + "_pre_g"] = _dn_prenorm("ffn_bwd_in", [dgp, du], [W[pre + "_w_gate"][l], W[pre + "_w_up"][l]],
                                            False, dh, sv["h"], S[pre + "_pre_g"])
    return dh_in


def _gates(zg):
    return [_sigmoid(jnp.concatenate([zg[2 * k].astype(F32), zg[2 * k + 1].astype(F32)], axis=1)) for k in range(3)]


def _merge_fwd(z, rs, ws, tm=ROW_TILE):
    T = z.shape[0]
    tm = _tile(T, tm)
    nb, kk, bw = ws[0].shape

    def body(*refs):
        r_refs, g_refs, w_refs, y_refs, m_ref = refs[:3], refs[3:9], refs[9:12], refs[12:15], refs[15]
        for r_ref, w_ref, y_ref in zip(r_refs, w_refs, y_refs):
            for b in range(nb):
                y_ref[:, b * bw:(b + 1) * bw] = jnp.dot(r_ref[...], w_ref[b],
                                                        preferred_element_type=F32).astype(y_ref.dtype)
        g = _gates([q[...] for q in g_refs])
        m_ref[...] = (g[0] * y_refs[0][...].astype(F32) + g[1] * y_refs[1][...].astype(F32)
                      + g[2] * y_refs[2][...].astype(F32)).astype(m_ref.dtype)

    row = pl.BlockSpec((tm, D_MODEL), lambda i: (i, 0))
    return pl.pallas_call(
        body, grid=(T // tm,),
        in_specs=[pl.BlockSpec((tm, kk), lambda i: (i, 0))] * 3
        + [pl.BlockSpec((tm, BRANCH), lambda i, j=j: (i, ZB_GATES + j)) for j in range(6)]
        + [pl.BlockSpec(ws[0].shape, lambda i: (0, 0, 0))] * 3,
        out_specs=[row] * 4, out_shape=[jax.ShapeDtypeStruct((T, D_MODEL), BF)] * 4,
        name="mix_merge", compiler_params=_params(1))(*rs, *[z] * 6, *ws)


def _merge_bwd(dmerged, z, ys, ws, tm=ROW_TILE):
    T = z.shape[0]
    tm = _tile(T, tm)
    nb, kk, bw = ws[0].shape

    def body(*refs):
        dm_ref, g_refs, y_refs, w_refs = refs[0], refs[1:7], refs[7:10], refs[10:13]
        dy_refs, lo_ref, hi_ref, dr_refs = refs[13:16], refs[16], refs[17], refs[18:21]
        cut = DZ_HALF - ZB_GATES * BRANCH
        dm = dm_ref[...].astype(F32)
        g = _gates([q[...] for q in g_refs])
        for k in range(3):
            dy_refs[k][...] = (dm * g[k]).astype(BF)
            dzg = (dm * y_refs[k][...].astype(F32) * g[k] * (1.0 - g[k])).astype(BF)
            if k == 0:
                lo_ref[...] = dzg[:, :cut]
                hi_ref[:, :D_MODEL - cut] = dzg[:, cut:]
            else:
                hi_ref[:, k * D_MODEL - cut:(k + 1) * D_MODEL - cut] = dzg
            dr = None
            for b in range(nb):
                p = lax.dot_general(dy_refs[k][:, b * bw:(b + 1) * bw], w_refs[k][b], _NT,
                                    preferred_element_type=F32)
                dr = p if dr is None else dr + p
            dr_refs[k][...] = dr.astype(BF)

    row = pl.BlockSpec((tm, D_MODEL), lambda i: (i, 0))
    return pl.pallas_call(
        body, grid=(T // tm,),
        in_specs=[row] + [pl.BlockSpec((tm, BRANCH), lambda i, j=j: (i, ZB_GATES + j)) for j in range(6)] + [row] * 3
        + [pl.BlockSpec(ws[0].shape, lambda i: (0, 0, 0))] * 3,
        out_specs=[row] * 3 + [pl.BlockSpec((tm, DZ_HALF - ZB_GATES * BRANCH), lambda i: (i, 0)),
                               pl.BlockSpec((tm, DZ_HALF), lambda i: (i, 0))]
        + [pl.BlockSpec((tm, kk), lambda i: (i, 0))] * 3,
        out_shape=[jax.ShapeDtypeStruct((T, D_MODEL), BF)] * 3
        + [jax.ShapeDtypeStruct((T, DZ_HALF - ZB_GATES * BRANCH), BF), jax.ShapeDtypeStruct((T, DZ_HALF), BF)]
        + [jax.ShapeDtypeStruct((T, kk), BF)] * 3,
        name="mix_merge_bwd", compiler_params=_params(1))(dmerged, *[z] * 6, *ys, *ws)


def _mix_fwd(l, h, S, W, deps=()):
    n, z = _norm_mm("mix_in", h, S["mix_pre_g"], [W["w_in"][l]], False, deps=deps)
    r_pool = _pool_fwd(z, S["pool_w"], S["pool_scale"])
    r_sgu = _sgu_fwd(z, S["sgu_ln_g"], S["sgu_ln_b"], S["sgu_w_s"], S["sgu_bias"])
    yc = _conv_fwd(z, W["conv_dw_k"], l, S["conv_dw_b"])

    def ln_silu(y, g, b):
        xh, _ = _ln_stats(y.astype(F32))
        return (_silu_and_grad(xh * g + b)[0],)

    r_conv = _rowwise("conv_ln", ln_silu, [(yc, BRANCH, 0)], [S["conv_ln_g"], S["conv_ln_b"]], [(BRANCH, BF)])[0]
    y_pool, y_sgu, y_conv, merged = _merge_fwd(z, (r_pool, r_sgu, r_conv),
                                               [W["w_%s_out" % br][l] for br in ("pool", "sgu", "conv")])
    o, out = _mm_res("mix_out", merged, W["w_out"][l], h, S["mix_post_g"], 1.0)
    return out, dict(h=h, n=n, z=z, r_pool=r_pool, r_sgu=r_sgu, yc=yc, r_conv=r_conv, y_pool=y_pool, y_sgu=y_sgu,
                     y_conv=y_conv, merged=merged, o=o)


def _branch_dw(rs, dys, shape):
    T, kk = rs[0].shape
    nb, _, bw = shape

    def body(*refs):
        for k in range(3):
            refs[6 + k][...] = lax.dot_general(refs[k][...], refs[3 + k][...], _TN,
                                               preferred_element_type=F32).astype(BF)

    return pl.pallas_call(
        body, grid=(nb,),
        in_specs=[pl.BlockSpec((T, kk), lambda b: (0, 0))] * 3 + [pl.BlockSpec((T, bw), lambda b: (0, b))] * 3,
        out_specs=[pl.BlockSpec((None, kk, bw), lambda b: (b, 0, 0))] * 3,
        out_shape=[jax.ShapeDtypeStruct((nb, kk, bw), BF)] * 3, name="branch_dw",
        compiler_params=_params(1))(*rs, *dys)


def _mix_bwd(l, dh, sv, S, W, G, SG, deps=()):
    z = sv["z"]
    do, SG["mix_post_g"], dmerged = _resbwd_mm("mix_bwd_out", dh, sv["o"], S["mix_post_g"], 1.0,
                                               W["w_out"][l].reshape(1, D, D), True, deps=deps)
    G["w_out"] = _mm_tn("mix_dw_out", sv["merged"], do, G["w_out"], l, True)

    branches = ("pool", "sgu", "conv")
    res = _merge_bwd(dmerged, z, [sv["y_" + br] for br in branches], [W["w_%s_out" % br][l] for br in branches])
    dz_gate_lo, dz_hi, dr = res[3], res[4], dict(zip(branches, res[5:]))
    for br, dw in zip(branches, _branch_dw([sv["r_" + br] for br in branches], res[:3], G["w_pool_out"][l].shape)):
        wn = "w_%s_out" % br
        G[wn] = G[wn][:l] + [dw] + G[wn][l + 1:]
    dz_pool, SG["pool_w"], SG["pool_scale"] = _pool_bwd(dr["pool"], z, S["pool_w"], S["pool_scale"])
    dzu, dzv, SG["sgu_ln_g"], SG["sgu_ln_b"], SG["sgu_w_s"], dbias = _sgu_bwd(
        dr["sgu"], z, S["sgu_ln_g"], S["sgu_ln_b"], S["sgu_w_s"], S["sgu_w_st"], S["sgu_bias"])
    SG["sgu_b_s"] = dbias[:, ::LANES].T

    def ln_silu_bwd(d, y, g, b):
        xh, r = _ln_stats(y.astype(F32))
        _, grad = _silu_and_grad(xh * g + b)
        return _ln_bwd(xh, r, g, d.astype(F32) * grad)

    dyc, SG["conv_ln_g"], SG["conv_ln_b"] = _rowwise(
        "conv_ln_bwd", ln_silu_bwd, [(dr["conv"], BRANCH, 0), (sv["yc"], BRANCH, 0)],
        [S["conv_ln_g"], S["conv_ln_b"]], [(BRANCH, BF)], [(1, BRANCH), (1, BRANCH)])
    dza, dzb, SG["conv_dw_k"], SG["conv_dw_b"] = _conv_bwd(dyc, z, W["conv_dw_k"], l)
    dz_lo = jnp.concatenate([dz_pool, dzu, dzv, dza, dzb, dz_gate_lo], axis=1)
    G["w_in"] = _mm_tn("mix_dw_in", sv["n"], dz_lo, G["w_in"], l, False)
    G["w_in"] = _mm_tn("mix_dw_in", sv["n"], dz_hi, G["w_in"], l, False, first=2)
    dh_in, SG["mix_pre_g"] = _dn_prenorm("mix_bwd_in", [dz_lo, dz_hi], W["w_in"][l], True, dh, sv["h"],
                                         S["mix_pre_g"])
    return dh_in


def _ple_out(h, p, gp, w3, g, tm=ROW_TILE):
    T, kp = p.shape
    nb, _, bw = w3.shape
    tm = _tile(T, tm)

    def body(h_ref, p_ref, gp_ref, w_ref, g_ref, e_ref, o_ref):
        p16 = p_ref[...].astype(BF)
        for b in range(nb):
            e_ref[:, b * bw:(b + 1) * bw] = jnp.dot(p16, w_ref[b], preferred_element_type=F32).astype(BF)
        q = _sigmoid(gp_ref[...].astype(F32)) * e_ref[...].astype(F32)
        o_ref[...] = h_ref[...] + _rms_fwd(q, g_ref[...])

    row = pl.BlockSpec((tm, D_MODEL), lambda i: (i, 0))
    return pl.pallas_call(
        body, grid=(T // tm,),
        in_specs=[row, pl.BlockSpec((tm, kp), lambda i: (i, 0)), row, pl.BlockSpec(w3.shape, lambda i: (0, 0, 0)),
                  pl.BlockSpec(g.shape, lambda i: (0, 0))],
        out_specs=[row, row],
        out_shape=[jax.ShapeDtypeStruct((T, D_MODEL), BF), jax.ShapeDtypeStruct((T, D_MODEL), F32)],
        name="ple_out", compiler_params=_params(1))(h, p, gp, w3, g)


def _ple_fwd(l, h, p_l, S, W, deps=()):
    n, gp = _norm_mm("ple_in", h, S["ple_pre_g"], [W["ple_w_gate"][l].reshape(1, D, D)], False, deps=deps)
    e, out = _ple_out(h, p_l, gp, W["ple_w_proj"][l], S["ple_post_g"])
    return out, dict(h=h, n=n, e=e, gp=gp, p=p_l)


def _ple_bwd_rows(dh, e, gp, g_post, w3, h, g_pre, deps=(), tm=ROW_TILE):
    T = dh.shape[0]
    w2 = w3.reshape(D_MODEL, D_MODEL)
    tm = _tile(T, tm)
    n_dep = len(deps)

    def body(*refs):
        dh_ref, e_ref, gp_ref, gpost_ref, w_ref, h_ref, gpre_ref, de_ref, dgp_ref, o_ref, dpost_ref, dpre_ref = \
            refs[n_dep:]
        first = pl.program_id(0) == 0
        d = dh_ref[...]
        sg = _sigmoid(gp_ref[...].astype(F32))
        ee = e_ref[...].astype(F32)
        dq, dpost = _rms_bwd(sg * ee, gpost_ref[...], d)
        de_ref[...] = (dq * sg).astype(BF)
        dgp = (dq * ee * sg * (1.0 - sg)).astype(BF)
        dgp_ref[...] = dgp
        dn = lax.dot_general(dgp, w_ref[...], _NT, preferred_element_type=F32)
        dx, dpre = _rms_bwd(h_ref[...], gpre_ref[...], dn)
        o_ref[...] = d + dx
        _acc_rows(dpost_ref, dpost, first)
        _acc_rows(dpre_ref, dpre, first)

    row = pl.BlockSpec((tm, D_MODEL), lambda i: (i, 0))
    vec = pl.BlockSpec((1, D_MODEL), lambda i: (0, 0))
    return pl.pallas_call(
        body, grid=(T // tm,),
        in_specs=[ANY] * n_dep + [row, row, row, vec, pl.BlockSpec(w2.shape, lambda i: (0, 0)), row, vec],
        out_specs=[row, row, row, vec, vec],
        out_shape=[jax.ShapeDtypeStruct((T, D_MODEL), BF)] * 2 + [jax.ShapeDtypeStruct((T, D_MODEL), F32)]
        + [jax.ShapeDtypeStruct((1, D_MODEL), F32)] * 2,
        name="ple_bwd", compiler_params=_params(1))(*deps, dh, e, gp, g_post, w2, h, g_pre)


def _ple_bwd(l, dh, sv, S, W, G, SG, deps=()):
    de, dgp, dh_in, SG["ple_post_g"], SG["ple_pre_g"] = _ple_bwd_rows(
        dh, sv["e"], sv["gp"], S["ple_post_g"], W["ple_w_gate"][l], sv["h"], S["ple_pre_g"], deps)
    G["ple_w_proj"] = _mm_tn("ple_dw_proj", sv["p"], de, G["ple_w_proj"], l, False)
    G["ple_w_gate"] = _mm_tn("ple_dw_gate", sv["n"], dgp, G["ple_w_gate"], l, True)
    return dh_in


def _layer_small(a, l):
    S = {}
    for name in SMALL:
        v = a[name][l]
        S[name] = v.reshape(1, -1) if v.ndim == 1 else v
    S["sgu_w_st"] = jnp.swapaxes(S["sgu_w_s"], 1, 2)
    S["sgu_bias"] = jnp.repeat(S["sgu_b_s"].T, LANES, axis=1)
    return S


def _layer_fwd(l, h, p_l, S, W, deps=(), hooks=None):
    hooks = hooks or {}

    def after(part, hv):
        return hooks[part](hv) if part in hooks else ()

    h, sv1 = _ffn_fwd(l, h, S, W, "ffn1", deps)
    h, sv2 = _mix_fwd(l, h, S, W, after("ffn1", h))
    h, sv3 = _ffn_fwd(l, h, S, W, "ffn2", after("mix", h))
    h, sv4 = _ple_fwd(l, h, p_l, S, W, after("ffn2", h))
    return h, (sv1, sv2, sv3, sv4)


def _layer_bwd(l, dh, sv, S, W, G, deps=(), hooks=None):
    hooks = hooks or {}

    def after(part, dv):
        return hooks[part](dv) if part in hooks else ()

    SG = {}
    dh = _ple_bwd(l, dh, sv[3], S, W, G, SG, deps)
    dh = _ffn_bwd(l, dh, sv[2], S, W, G, SG, "ffn2")
    dh = _mix_bwd(l, dh, sv[1], S, W, G, SG, after("ffn2", dh))
    dh = _ffn_bwd(l, dh, sv[0], S, W, G, SG, "ffn1", after("mix", dh))
    return dh, SG


HBM = pl.BlockSpec(memory_space=pltpu.HBM)
SEM = pl.BlockSpec(memory_space=pltpu.SEMAPHORE)
SIDE_EFFECT = pltpu.SideEffectType.DATAFLOW_SIDE_EFFECTING


def _place():
    x, y, c = lax.axis_index("x"), lax.axis_index("y"), lax.axis_index("c")
    chips = [(1 - x, y), (x, 1 - y), (1 - x, 1 - y)]
    return x, y, c, chips


def _remote(src, dst, send_sem, recv_sem, to):
    return pltpu.make_async_remote_copy(src_ref=src, dst_ref=dst, send_sem=send_sem, recv_sem=recv_sem,
                                        device_id=to, device_id_type=MESH)


def _split_start(name, plan, bufs, deps):
    count, fn = plan
    n, nd = len(bufs), len(deps)

    def body(*refs):
        send, recv = refs[nd + n], refs[nd + n + 1]
        x, y, c, chips = _place()
        for k, (src, dst, _, to) in enumerate(fn(refs[nd:nd + n], x, y, c, chips)):
            _remote(src, dst, send.at[k], recv.at[k], to).start()
        refs[-1][...] = jnp.zeros_like(refs[-1])

    res = pl.pallas_call(
        body, in_specs=[ANY] * nd + [HBM] * n,
        out_specs=[SEM, SEM] + [HBM] * n + [pl.BlockSpec(memory_space=pltpu.VMEM)],
        out_shape=[pltpu.SemaphoreType.DMA((count,)), pltpu.SemaphoreType.DMA((count,))]
        + [pltpu.HBM(b.shape, b.dtype) for b in bufs] + [jax.ShapeDtypeStruct((8, LANES), F32)],
        input_output_aliases={nd + i: 2 + i for i in range(n)}, name=name,
        compiler_params=pltpu.CompilerParams(has_side_effects=SIDE_EFFECT),
    )(*deps, *[pltpu.with_memory_space_constraint(b, pltpu.HBM) for b in bufs])
    return (res[0], res[1]), list(res[2:2 + n]), res[-1]


def _split_wait(name, plan, sems, bufs, after):
    _, fn = plan
    n = len(bufs)

    def body(*refs):
        send, recv = refs[n], refs[n + 1]
        x, y, c, chips = _place()
        for k, (src, _, land, to) in enumerate(fn(refs[:n], x, y, c, chips)):
            cp = _remote(src, land, send.at[k], recv.at[k], to)
            cp.wait_send()
            cp.wait_recv()

    res = pl.pallas_call(
        body, in_specs=[HBM] * n + [SEM, SEM] + [ANY] * len(after), out_specs=[HBM] * n,
        out_shape=[pltpu.HBM(b.shape, b.dtype) for b in bufs], input_output_aliases={i: i for i in range(n)},
        name=name, compiler_params=pltpu.CompilerParams(has_side_effects=SIDE_EFFECT))(*bufs, *sems, *after)
    return list(res)


def _gather_plans(n):
    def across(b, x, y, c, chips):
        me, out = 2 * x + y, []
        for a in range(n):
            rh = b[a].shape[1] // 2
            mine = b[a].at[me, pl.ds(c * rh, rh)]
            for cx, cy in chips:
                out.append((mine, mine, b[a].at[2 * cx + cy, pl.ds(c * rh, rh)], (cx, cy, c)))
        return out

    def to_sibling(b, x, y, c, chips):
        out = []
        for a in range(n):
            rh = b[a].shape[1] // 2
            for cx, cy in chips:
                piece = b[a].at[2 * cx + cy, pl.ds(c * rh, rh)]
                out.append((piece, piece, b[a].at[2 * cx + cy, pl.ds((1 - c) * rh, rh)], (x, y, 1 - c)))
        return out

    return (3 * n, across), (3 * n, to_sibling)


def _pair_plan(n):
    def fn(b, x, y, c, chips):
        out = []
        for a in range(n):
            rh = b[a].shape[1] // 2
            out.append((b[a].at[:, pl.ds((1 - c) * rh, rh)], b[n + a], b[n + a], (x, y, 1 - c)))
        return out

    return n, fn


def _cross_plan(n):
    def fn(b, x, y, c, chips):
        out = []
        for a in range(n):
            for j, (cx, cy) in enumerate(chips):
                out.append((b[a].at[2 * cx + cy], b[n + a].at[j], b[n + a].at[j], (cx, cy, c)))
        return out

    return 3 * n, fn


def _share_plan(n, l):
    def fn(b, x, y, c, chips):
        out = []
        for a in range(n):
            rh = b[a].shape[1] // 2
            mine = b[a].at[l, pl.ds(c * rh, rh)]
            out.append((mine, mine, b[a].at[l, pl.ds((1 - c) * rh, rh)], (x, y, 1 - c)))
        return out

    return n, fn


def _peers(x, y, c):
    return [(1 - x if m & 4 else x, 1 - y if m & 2 else y, 1 - c if m & 1 else c) for m in range(1, 8)]


def _small_plans():
    def scatter(b, x, y, c, chips):
        return [(b[0].at[4 * px + 2 * py + pc], b[1].at[m], b[1].at[m], (px, py, pc))
                for m, (px, py, pc) in enumerate(_peers(x, y, c))]

    def gather(b, x, y, c, chips):
        mine = b[0].at[4 * x + 2 * y + c]
        return [(mine, mine, b[0].at[4 * px + 2 * py + pc], (px, py, pc)) for px, py, pc in _peers(x, y, c)]

    return (7, scatter), (7, gather)


def _sum_small(v3, got, pos):
    rs = v3.shape[1]
    tm = _tile(rs)
    ins = [(v3, (None, tm, LANES), lambda i, p: (p[2], i, 0))]
    ins += [(got, (None, tm, LANES), lambda i, p, m=m: (m, i, 0)) for m in range(7)]
    return _tiled("sum_small", lambda *t: (((((((t[0] + t[1]) + t[2]) + t[3]) + t[4]) + t[5]) + t[6]) + t[7],),
                  (rs // tm,), pos, ins, [((8, rs, LANES), F32, (None, tm, LANES), lambda i, p: (p[2], i, 0))])[0]


ADD_ROWS = 256


def _multi_tiled(name, fn, pos, groups, in_place=False):
    steps = max(g[2] for g in groups)
    flat_in, in_specs, out_specs, out_shape, counts, dests = [], [], [], [], [], []
    for ins, rows, n_t, (shape, dtype, oidx, dest) in groups:
        for arr, idx in ins:
            flat_in.append(arr)
            in_specs.append(pl.BlockSpec((rows, arr.shape[1]),
                                         lambda i, p, idx=idx, n_t=n_t: (idx(jnp.minimum(i, n_t - 1), p), 0)))
        out_specs.append(pl.BlockSpec((rows, shape[1]),
                                      lambda i, p, oidx=oidx, n_t=n_t: (oidx(jnp.minimum(i, n_t - 1), p), 0)))
        out_shape.append(jax.ShapeDtypeStruct(shape, dtype))
        counts.append((len(ins), n_t))
        dests.append(dest)
    n_in = len(flat_in)
    extra = dests if in_place else []

    def body(_, *refs):
        outs = refs[n_in + len(extra):]
        k = 0
        for (n_a, n_t), o_ref in zip(counts, outs):
            tiles = refs[k:k + n_a]
            k += n_a

            @pl.when(pl.program_id(0) < n_t)
            def _(tiles=tiles, o_ref=o_ref):
                o_ref[...] = fn(*[t[...] for t in tiles]).astype(o_ref.dtype)

    spec = pltpu.PrefetchScalarGridSpec(num_scalar_prefetch=1, grid=(steps,),
                                        in_specs=in_specs + [ANY] * len(extra), out_specs=out_specs)
    return pl.pallas_call(body, grid_spec=spec, out_shape=out_shape,
                          input_output_aliases={1 + n_in + k: k for k in range(len(extra))}, name=name,
                          compiler_params=_params(1))(pos, *flat_in, *extra)


def _add_pair(grads, got, pos):
    groups = []
    for g, q in zip(grads, got):
        nb, R, C = g.shape
        rh = R // 2
        rows = _tile(rh, ADD_ROWS)
        nh = rh // rows
        groups.append(([(g.reshape(nb * R, C), lambda t, p, nh=nh: (t // nh) * 2 * nh + p[1] * nh + t % nh),
                        (q.reshape(nb * rh, C), lambda t, p: t)], rows, nb * nh,
                       ((nb * rh, C), BF, lambda t, p: t, None)))
    res = _multi_tiled("rs_add_pair", lambda u, w: u.astype(F32) + w.astype(F32), pos, groups)
    return [t.reshape(q.shape) for t, q in zip(res, got)]


def _add_chips(parts, slots, reduced, l, pos):
    def add(own, s0, s1, s2):
        return ((own.astype(F32) + s0.astype(F32)) + s1.astype(F32)) + s2.astype(F32)

    groups = []
    for t, s, red in zip(parts, slots, reduced):
        nb, rh, C = t.shape
        L = red.shape[0]
        rows = _tile(rh, ADD_ROWS)
        nh = rh // rows
        ins = [(t.reshape(nb * rh, C), lambda i, p, nh=nh: p[0] * nh + i)]
        ins += [(s.reshape(3 * rh, C), lambda i, p, j=j, nh=nh: j * nh + i) for j in range(3)]
        groups.append((ins, rows, nh, ((L * 2 * rh, C), F32, lambda i, p, nh=nh: l * 2 * nh + p[1] * nh + i,
                                 red.reshape(L * 2 * rh, C))))
    res = _multi_tiled("rs_add_chips", add, pos, groups, in_place=True)
    return [buf.reshape(red.shape) for buf, red in zip(res, reduced)]


def _adamw_math(w, g, m, v):
    m = ADAM_B1 * m + (1.0 - ADAM_B1) * g
    v = ADAM_B2 * v + (1.0 - ADAM_B2) * (g * g)
    m_hat = m / (1.0 - ADAM_B1 ** ADAM_STEP)
    v_hat = v / (1.0 - ADAM_B2 ** ADAM_STEP)
    return -ADAM_LR * (m_hat / (jnp.sqrt(v_hat) + ADAM_EPS) + ADAM_WD * w), m, v


def _adamw(w, g, m, v, lo=0, hi=None, into=None, deps=()):
    L, R, C = w.shape
    hi = L if hi is None else hi
    tr = _tile(R, max(16, ADAM_TILE_ELEMS // C))
    extra = (list(into) if into else []) + list(deps)
    n_alias = 4 if into else 0

    def body(w_ref, g_ref, m_ref, v_ref, *rest):
        go_ref, d_ref, mo_ref, vo_ref = rest[len(extra):]
        gv = g_ref[...]
        d, mn, vn = _adamw_math(w_ref[...], gv, m_ref[...], v_ref[...])
        go_ref[...] = gv
        d_ref[...] = d
        mo_ref[...] = mn
        vo_ref[...] = vn

    spec = pl.BlockSpec((None, tr, C), lambda l, i: (l + lo, i, 0))
    out = jax.ShapeDtypeStruct(w.shape, F32)
    return pl.pallas_call(body, grid=(hi - lo, R // tr), in_specs=[spec] * 4 + [ANY] * len(extra),
                          out_specs=[spec] * 4, out_shape=[out] * 4,
                          input_output_aliases={4 + k: k for k in range(n_alias)}, name="adamw",
                          compiler_params=_params(2))(w, g, m, v, *extra)


def _pack(parts):
    flat = jnp.concatenate([q.reshape(-1, LANES) for q in parts], axis=0)
    return jnp.pad(flat, ((0, -flat.shape[0] % ROW_TILE), (0, 0)))


def _unpack(flat, like):
    out, r = [], 0
    for q in like:
        n = q.size // LANES
        out.append(flat[r:r + n].reshape(q.shape))
        r += n
    return out


def _train_step(a):
    a = dict(a)
    L = a["ffn1_pre_g"].shape[0]
    x, y, c, _ = _place()
    chip = 2 * x + y
    pos = jnp.stack([chip, c, 2 * chip + c]).astype(jnp.int32)
    for name in TRANSPOSED:
        for pre in ("", "m_", "v_"):
            a[pre + name] = jnp.swapaxes(a[pre + name], 1, 2)
    big = [b[0] for b in BIG]
    gathered = big + ["conv_dw_k"]
    n_w, n_g = len(gathered), len(big)

    own = [None] * n_w
    W = {name: [None] * L for name in gathered}
    every = list(range(n_w))
    first, mixer, later = every[:3], every[3:8] + [n_g], every[8:n_g]
    rest = mixer + later

    def cast(i, deps):
        if i == n_g:
            taps = a["conv_dw_k"].reshape(L, CONV_TAPS, LANES)
            return _cast_layers("pad_conv_taps", taps, CONV_PAD, LANES, F32, pos, deps)
        name, _, _, _, rp, cp = BIG[i]
        return _cast_layers("cast_weight", a[name], rp, cp, BF, pos, deps)

    def gather_first(l, ids, tag, deps):
        return _split_start("gather_a%d%s" % (l, tag), _gather_plans(len(ids))[0], [own[i][l] for i in ids], deps)

    def gather_second(l, ids, tag, state, after):
        across, to_sibling = _gather_plans(len(ids))
        bufs = _split_wait("gather_a%d%s_done" % (l, tag), across, state[0], state[1], after)
        return _split_start("gather_b%d%s" % (l, tag), to_sibling, bufs, [])

    def gather_done(l, ids, tag, state, after):
        to_sibling = _gather_plans(len(ids))[1]
        bufs = _split_wait("gather_b%d%s_done" % (l, tag), to_sibling, state[0], state[1], after)
        for i, buf in zip(ids, bufs):
            W[gathered[i]][l] = buf

    for i in first:
        own[i] = cast(i, ())
    state = gather_first(0, first, "f", [])
    for i in rest:
        own[i] = cast(i, (state[2],))
    state = gather_second(0, first, "f", state, [own[i][0] for i in rest])
    gather_done(0, first, "f", state, [])

    parts = {"f": first, "m": mixer, "t": later}
    flying = {}

    def begin(l, part, deps):
        flying[l, part] = gather_first(l, parts[part], part, deps)
        return flying[l, part][2]

    def hand_on(l, part, after):
        flying[l, part] = gather_second(l, parts[part], part, flying[l, part], after)
        return flying[l, part][2]

    def arrive(l, part, after):
        gather_done(l, parts[part], part, flying.pop((l, part)), after)

    def hooks_of(l):
        nxt = l + 1 < L

        def after_ffn1(hv):
            tokens = []
            if l == 0:
                hand_on(0, "m", [hv])
            arrive(l, "m", [hv])
            if l == 0:
                tokens.append(begin(0, "t", [hv]))
            else:
                tokens.append(hand_on(l, "t", [hv]))
            if nxt:
                tokens.append(begin(l + 1, "f", tokens[-1:]))
            return tuple(tokens)

        def after_mix(hv):
            tokens = []
            if l == 0:
                hand_on(0, "t", [hv])
            arrive(l, "t", [hv])
            if nxt:
                tokens.append(hand_on(l + 1, "f", [hv]))
                tokens.append(begin(l + 1, "m", tokens[-1:]))
            return tuple(tokens)

        def after_ffn2(hv):
            tokens = []
            if nxt:
                arrive(l + 1, "f", [hv])
                tokens.append(hand_on(l + 1, "m", [hv]))
                tokens.append(begin(l + 1, "t", tokens[-1:]))
            return tuple(tokens)

        return {"ffn1": after_ffn1, "mix": after_mix, "ffn2": after_ffn2}

    small = [_layer_small(a, l) for l in range(L)]
    h, saved = a["x"][0], []
    deps = (begin(0, "m", []),)
    for l in range(L):
        h, sv = _layer_fwd(l, h, a["p"][l, 0], small[l], W, deps, hooks_of(l))
        saved.append(sv)
        deps = ()

    def loss_fn(yv, t):
        e = yv - t
        return e * (1.0 / D), jnp.sum(e * e, axis=0, keepdims=True)

    dh, lsum = _rowwise("loss", loss_fn, [(h, D, 0), (a["loss_target"][0], D, 0)], [], [(D, F32)], [(1, D)])
    loss = lax.psum(0.5 * jnp.sum(lsum) / D, ("x", "y", "c"))

    G = {name: [jax.ShapeDtypeStruct((N_CHIPS, rp, cp), BF)] * L for name, _, _, _, rp, cp in BIG}
    reduced = [lax.empty((L, rp, cp), F32) for _, _, _, _, rp, cp in BIG]
    small_grads = [None] * L
    whole = list(range(n_g))
    piece_a, piece_b, piece_c = whole[8:], whole[3:8], whole[:3]

    def pair_start(l, ids, tag, deps):
        grads = [G[big[i]][l] for i in ids]
        lands = [lax.empty((N_CHIPS, g.shape[1] // 2, g.shape[2]), BF) for g in grads]
        return _split_start("rs_pair%d%s" % (l, tag), _pair_plan(len(ids)), grads + lands, deps)

    def cross_start(l, ids, tag, state, after):
        n = len(ids)
        bufs = _split_wait("rs_pair%d%s_done" % (l, tag), _pair_plan(n), state[0], state[1], after)
        parts = _add_pair(bufs[:n], bufs[n:], pos)
        lands = [lax.empty((3,) + t.shape[1:], BF) for t in parts]
        return _split_start("rs_cross%d%s" % (l, tag), _cross_plan(n), parts + lands, [])

    def cross_finish(l, ids, tag, state, after, reduced):
        n = len(ids)
        bufs = _split_wait("rs_cross%d%s_done" % (l, tag), _cross_plan(n), state[0], state[1], after)
        reduced = list(reduced)
        for i, r in zip(ids, _add_chips(bufs[:n], bufs[n:], [reduced[i] for i in ids], l, pos)):
            reduced[i] = r
        return reduced

    def share_start(l, reduced):
        return _split_start("rs_share%d" % l, _share_plan(n_g, l), reduced, [])

    def share_done(l, state, after):
        return _split_wait("rs_share%d_done" % l, _share_plan(n_g, l), state[0], state[1], after)

    small_names = SMALL + ("conv_dw_k",)
    scatter, gather = _small_plans()
    totals = [None] * L

    def small_scatter(l, deps):
        packed = _pack([small_grads[l][name] for name in small_names])
        v3 = packed.reshape(8, packed.shape[0] // 8, LANES)
        return _split_start("small_scatter%d" % l, scatter, [v3, lax.empty((7,) + v3.shape[1:], F32)], deps)

    def small_gather(l, state, after):
        bufs = _split_wait("small_scatter%d_done" % l, scatter, state[0], state[1], after)
        return _split_start("small_gather%d" % l, gather, [_sum_small(bufs[0], bufs[1], pos)], [])

    def small_done(l, state, after):
        total = _split_wait("small_gather%d_done" % l, gather, state[0], state[1], after)[0]
        totals[l] = total.reshape(-1, LANES)

    st_pair = st_share = st_small = None
    for l in reversed(range(L)):
        deps = tuple(s[2] for s in (st_pair, st_share, st_small) if s is not None)
        box = {}

        def after_ffn2(dm, l=l, box=box, st_pair=st_pair, st_share=st_share, st_small=st_small):
            out = []
            if st_share is not None:
                box["reduced"] = share_done(l + 2, st_share, [dm])
            if st_small is not None:
                box["small"] = small_gather(l + 1, st_small, [dm])
                out.append(box["small"][2])
            if st_pair is not None:
                box["cross"] = cross_start(l + 1, whole, "", st_pair, [dm])
                out.append(box["cross"][2])
            if l == 0:
                box["pair_a"] = pair_start(0, piece_a, "a", [dm])
                out.append(box["pair_a"][2])
            return tuple(out)

        def after_mix(dm, box=box):
            box["cross_a"] = cross_start(0, piece_a, "a", box["pair_a"], [dm])
            box["pair_b"] = pair_start(0, piece_b, "b", [dm])
            return (box["cross_a"][2], box["pair_b"][2])

        hooks = {"ffn2": after_ffn2, "mix": after_mix} if l == 0 else {"ffn2": after_ffn2}
        dh, small_grads[l] = _layer_bwd(l, dh, saved[l], small[l], W, G, deps, hooks)
        if st_share is not None:
            reduced = box["reduced"]
        if "small" in box:
            small_done(l + 1, box["small"], [dh])
        st_share = None
        if "cross" in box:
            reduced = cross_finish(l + 1, whole, "", box["cross"], [dh], reduced)
            st_share = share_start(l + 1, reduced)
        st_small = small_scatter(l, [dh])
        st_pair = pair_start(l, whole, "", [st_small[2]]) if l else None
    grad_x = dh
    cross_b = cross_start(0, piece_b, "b", box["pair_b"], [st_small[2]])
    cross_c = cross_start(0, piece_c, "c", pair_start(0, piece_c, "c", [cross_b[2]]), [])
    if st_share is not None:
        reduced = share_done(1, st_share, [cross_c[2]])
    upper, token = {}, cross_c[2]
    for k, (name, red) in enumerate(zip(big, reduced)):
        if k == 2:
            st_small = small_gather(0, st_small, [token])
            token = st_small[2]
        if L > 1:
            upper[name] = _adamw(a[name], red, a["m_" + name], a["v_" + name], 1, L, deps=[token])
            token = upper[name][1]
    small_done(0, st_small, [token])
    per_layer = [_unpack(totals[l], [small_grads[l][name] for name in small_names]) for l in range(L)]
    summed = {name: jnp.stack([per_layer[l][k] for l in range(L)]) for k, name in enumerate(small_names)}
    done = [r[1] for r in upper.values()] + [summed[small_names[0]]]
    for ids, tag, state in ((piece_a, "a", box["cross_a"]), (piece_b, "b", cross_b), (piece_c, "c", cross_c)):
        reduced = cross_finish(0, ids, tag, state, done, reduced)
    reduced = share_done(0, share_start(0, reduced), [])
    big_grads = dict(zip(big, reduced))

    grads, deltas, new_m, new_v = {}, {}, {}, {}
    for name in big:
        res = _adamw(a[name], big_grads[name], a["m_" + name], a["v_" + name], 0, 1, upper.get(name))
        if name in TRANSPOSED:
            res = [jnp.swapaxes(r, 1, 2) for r in res]
        grads[name], deltas[name], new_m[name], new_v[name] = res
    taps = lax.dynamic_slice_in_dim(summed["conv_dw_k"], chip * LANES, LANES, axis=2)[:, :CONV_TAPS]
    grads["conv_dw_k"] = taps.reshape(a["conv_dw_k"].shape)
    for name in SMALL:
        grads[name] = summed[name].reshape(a[name].shape)
    shapes = [a[name] for name in small_names]
    res = _adamw(*[_pack([a[pre + name] if pre != "g" else grads[name] for name in small_names])[None]
                   for pre in ("", "g", "m_", "v_")])
    for dst, flat in zip((deltas, new_m, new_v), res[1:]):
        for name, val in zip(small_names, _unpack(flat[0], shapes)):
            dst[name] = val

    return (loss, grad_x[None], *[grads[n] for n in WEIGHTS], *[deltas[n] for n in WEIGHTS],
            *[new_m[n] for n in WEIGHTS], *[new_v[n] for n in WEIGHTS])


def kernel(x, p, ffn1_pre_g, ffn1_w_gate, ffn1_w_up, ffn1_w_down, ffn1_post_g, mix_pre_g, w_in, pool_w, pool_scale, w_pool_out, sgu_ln_g, sgu_ln_b, sgu_w_s, sgu_b_s, w_sgu_out, conv_dw_k, conv_dw_b, conv_ln_g, conv_ln_b, w_conv_out, w_out, mix_post_g, ffn2_pre_g, ffn2_w_gate, ffn2_w_up, ffn2_w_down, ffn2_post_g, ple_w_proj, ple_pre_g, ple_w_gate, ple_post_g, loss_target, m_ffn1_pre_g, m_ffn1_w_gate, m_ffn1_w_up, m_ffn1_w_down, m_ffn1_post_g, m_mix_pre_g, m_w_in, m_pool_w, m_pool_scale, m_w_pool_out, m_sgu_ln_g, m_sgu_ln_b, m_sgu_w_s, m_sgu_b_s, m_w_sgu_out, m_conv_dw_k, m_conv_dw_b, m_conv_ln_g, m_conv_ln_b, m_w_conv_out, m_w_out, m_mix_post_g, m_ffn2_pre_g, m_ffn2_w_gate, m_ffn2_w_up, m_ffn2_w_down, m_ffn2_post_g, m_ple_w_proj, m_ple_pre_g, m_ple_w_gate, m_ple_post_g, v_ffn1_pre_g, v_ffn1_w_gate, v_ffn1_w_up, v_ffn1_w_down, v_ffn1_post_g, v_mix_pre_g, v_w_in, v_pool_w, v_pool_scale, v_w_pool_out, v_sgu_ln_g, v_sgu_ln_b, v_sgu_w_s, v_sgu_b_s, v_w_sgu_out, v_conv_dw_k, v_conv_dw_b, v_conv_ln_g, v_conv_ln_b, v_w_conv_out, v_w_out, v_mix_post_g, v_ffn2_pre_g, v_ffn2_w_gate, v_ffn2_w_up, v_ffn2_w_down, v_ffn2_post_g, v_ple_w_proj, v_ple_pre_g, v_ple_w_gate, v_ple_post_g):
    return _train_step(dict(locals()))
```

```python
import math

import jax
import jax.numpy as jnp
from jax import lax
from jax.experimental import pallas as pl
from jax.experimental.pallas import tpu as pltpu

BF = jnp.bfloat16
F32 = jnp.float32
EPS = 1e-6
D_MODEL = 1024
LANES = 128
SUBLANES = 8
MXU_TILE = 256
N_CHIPS = 4
FFN_SHARD = 704
FFN_SHARD_PAD = 768
POOL_WINDOWS = (2, 4, 8, 16)
SGU_HEADS = 4
CHUNK = 128
CONV_TAPS = 31
CONV_PAD = 32
ROW_TILE = 512
EPI_ROWS = 256
VMEM_LIMIT_BYTES = 56 * 1024 * 1024
ADAM_TILE_ELEMS = 3 * 128 * 1024
ADAM_LR, ADAM_B1, ADAM_B2, ADAM_EPS, ADAM_WD, ADAM_STEP =0.001, 0.9, 0.999, 1e-08, 0.01, 10
MESH = pl.DeviceIdType.MESH
ANY = pl.BlockSpec(memory_space=pl.ANY)

BRANCH = 512
ZB_POOL, ZB_U, ZB_V, ZB_A, ZB_B, ZB_GATES = 0, 1, 2, 3, 4, 5
DZ_HALF = 2816

TRANSPOSED = ("ffn1_w_gate", "ffn1_w_up", "ffn2_w_gate", "ffn2_w_up")
BIG = (
    ("ffn1_w_gate", "row", FFN_SHARD, 1024, FFN_SHARD_PAD, 1024),
    ("ffn1_w_up", "row", FFN_SHARD, 1024, FFN_SHARD_PAD, 1024),
    ("ffn1_w_down", "row", FFN_SHARD, 1024, FFN_SHARD_PAD, 1024),
    ("w_in", "col", 1024, 1408, 1024, 1408),
    ("w_pool_out", "col", 512, 256, 512, 256),
    ("w_sgu_out", "col", 512, 256, 512, 256),
    ("w_conv_out", "col", 512, 256, 512, 256),
    ("w_out", "row", 256, 1024, 256, 1024),
    ("ffn2_w_gate", "row", FFN_SHARD, 1024, FFN_SHARD_PAD, 1024),
    ("ffn2_w_up", "row", FFN_SHARD, 1024, FFN_SHARD_PAD, 1024),
    ("ffn2_w_down", "row", FFN_SHARD, 1024, FFN_SHARD_PAD, 1024),
    ("ple_w_proj", "col", 256, 256, 256, 256),
    ("ple_w_gate", "row", 256, 1024, 256, 1024),
)
SMALL = ("ffn1_pre_g", "ffn1_post_g", "mix_pre_g", "pool_w", "pool_scale", "sgu_ln_g", "sgu_ln_b", "sgu_w_s",
         "sgu_b_s", "conv_dw_b", "conv_ln_g", "conv_ln_b", "mix_post_g", "ffn2_pre_g", "ffn2_post_g",
         "ple_pre_g", "ple_post_g")
WEIGHTS = ("ffn1_pre_g", "ffn1_w_gate", "ffn1_w_up", "ffn1_w_down", "ffn1_post_g", "mix_pre_g", "w_in", "pool_w",
           "pool_scale", "w_pool_out", "sgu_ln_g", "sgu_ln_b", "sgu_w_s", "sgu_b_s", "w_sgu_out", "conv_dw_k",
           "conv_dw_b", "conv_ln_g", "conv_ln_b", "w_conv_out", "w_out", "mix_post_g", "ffn2_pre_g", "ffn2_w_gate",
           "ffn2_w_up", "ffn2_w_down", "ffn2_post_g", "ple_w_proj", "ple_pre_g", "ple_w_gate", "ple_post_g")


def _params(n_grid):
    return pltpu.CompilerParams(dimension_semantics=("arbitrary",) * n_grid, vmem_limit_bytes=VMEM_LIMIT_BYTES)


def _tile(n, cap=ROW_TILE):
    for t in range(min(cap, n) - min(cap, n) % 16, 0, -16):
        if n % t == 0:
            return t
    return n


def _sigmoid(x):
    return 0.5 * jnp.tanh(0.5 * x) + 0.5


def _silu_and_grad(x):
    s = _sigmoid(x)
    return x * s, s * (1.0 + x * (1.0 - s))


def _gelu_and_grad(x):
    cdf = 0.5 * (1.0 + lax.erf(x * (1.0 / math.sqrt(2.0))))
    pdf = jnp.exp(-0.5 * x * x) * (1.0 / math.sqrt(2.0 * math.pi))
    return x * cdf, cdf + x * pdf


def _rms_fwd(x, g):
    return x * lax.rsqrt(jnp.mean(x * x, axis=-1, keepdims=True) + EPS) * g


def _rms_bwd(x, g, dy):
    r = lax.rsqrt(jnp.mean(x * x, axis=-1, keepdims=True) + EPS)
    xh = x * r
    dxh = dy * g
    dx = r * (dxh - xh * jnp.mean(dxh * xh, axis=-1, keepdims=True))
    return dx, jnp.sum(dy * xh, axis=0, keepdims=True)


def _ln_stats(x):
    xc = x - jnp.mean(x, axis=-1, keepdims=True)
    r = lax.rsqrt(jnp.mean(xc * xc, axis=-1, keepdims=True) + EPS)
    return xc * r, r


def _ln_bwd(xh, r, g, dy):
    dxh = dy * g
    dx = r * (dxh - jnp.mean(dxh, axis=-1, keepdims=True) - xh * jnp.mean(dxh * xh, axis=-1, keepdims=True))
    return dx, jnp.sum(dy * xh, axis=0, keepdims=True), jnp.sum(dy, axis=0, keepdims=True)


def _rowwise(name, fn, rows, consts, outs, accs=(), tm=ROW_TILE, deps=()):
    T = rows[0][0].shape[-2]
    tm = _tile(T, tm)
    n_in, n_o, n_dep = len(rows) + len(consts), len(outs), len(deps)

    def body(*refs):
        refs = refs[n_dep:]
        res = fn(*[r[...] for r in refs[:n_in]])
        for ref, val in zip(refs[n_in:n_in + n_o], res[:n_o]):
            ref[...] = val.astype(ref.dtype)
        acc_refs = refs[n_in + n_o:]
        if acc_refs:
            @pl.when(pl.program_id(0) == 0)
            def _():
                for ref, val in zip(acc_refs, res[n_o:]):
                    ref[...] = val

            @pl.when(pl.program_id(0) != 0)
            def _():
                for ref, val in zip(acc_refs, res[n_o:]):
                    ref[...] += val

    in_specs = [ANY] * n_dep
    for row in rows:
        w, cb = row[1], row[2]
        if len(row) == 4:
            in_specs.append(pl.BlockSpec((None, tm, w), lambda i, cb=cb, ld=row[3]: (ld, i, cb)))
        else:
            in_specs.append(pl.BlockSpec((tm, w), lambda i, cb=cb: (i, cb)))
    in_specs += [pl.BlockSpec(c.shape, lambda i: (0, 0)) for c in consts]
    out_specs = [pl.BlockSpec((tm, w), lambda i: (i, 0)) for w, _ in outs]
    out_specs += [pl.BlockSpec(s, lambda i: (0, 0)) for s in accs]
    out_shape = [jax.ShapeDtypeStruct((T, w), dt) for w, dt in outs]
    out_shape += [jax.ShapeDtypeStruct(s, F32) for s in accs]
    return pl.pallas_call(body, grid=(T // tm,), in_specs=in_specs, out_specs=out_specs, out_shape=out_shape,
                          name=name, compiler_params=_params(1))(*deps, *[r[0] for r in rows], *consts)


def _tiled(name, fn, grid, pos, ins, outs):
    n_in = len(ins)

    def body(_, *refs):
        res = fn(*[r[...] for r in refs[:n_in]])
        for ref, val in zip(refs[n_in:], res):
            ref[...] = val.astype(ref.dtype)

    spec = pltpu.PrefetchScalarGridSpec(
        num_scalar_prefetch=1, grid=grid, in_specs=[pl.BlockSpec(bs, im) for _, bs, im in ins],
        out_specs=[pl.BlockSpec(bs, im) for _, _, bs, im in outs])
    return pl.pallas_call(body, grid_spec=spec, out_shape=[jax.ShapeDtypeStruct(s, d) for s, d, _, _ in outs],
                          name=name, compiler_params=_params(len(grid)))(pos, *[a for a, _, _ in ins])


def _cast_layers(name, w, rp, cp, dtype, pos, deps=()):
    L, r, c = w.shape

    def body(_, w_ref, *rest):
        for k, o_ref in enumerate(rest[len(deps):]):
            @pl.when(pl.program_id(0) == k)
            def _(o_ref=o_ref):
                if (rp, cp) != (r, c):
                    o_ref[...] = jnp.zeros_like(o_ref)
                    o_ref[pl.ds(0, r), pl.ds(0, c)] = w_ref[...].astype(dtype)
                else:
                    o_ref[...] = w_ref[...].astype(dtype)

    spec = pltpu.PrefetchScalarGridSpec(
        num_scalar_prefetch=1, grid=(L,),
        in_specs=[pl.BlockSpec((None, r, c), lambda l, p: (l, 0, 0))] + [ANY] * len(deps),
        out_specs=[pl.BlockSpec((None, rp, cp), lambda l, p: (p[0], 0, 0))] * L)
    return pl.pallas_call(body, grid_spec=spec, out_shape=[jax.ShapeDtypeStruct((N_CHIPS, rp, cp), dtype)] * L,
                          name=name, compiler_params=_params(1))(pos, w, *deps)


_NN = (((1,), (0,)), ((), ()))
_NT = (((1,), (1,)), ((), ()))
_TN = (((0,), (0,)), ((), ()))


def _mm_tn(name, a, dy, buf, l, a_blocked, tk=ROW_TILE, first=0):
    T = a.shape[0]
    nb, R, C = buf[l].shape
    extra = [buf[l]] if first else []

    def body(a_ref, dy_ref, *rest):
        rest[-1][...] = lax.dot_general(a_ref[...].astype(BF), dy_ref[...].astype(BF), _TN,
                                        preferred_element_type=F32).astype(BF)

    if a_blocked:
        grid = (nb,)
        in_specs = [pl.BlockSpec((T, R), lambda b: (0, b)), pl.BlockSpec((T, C), lambda b: (0, 0))]
        out_specs = pl.BlockSpec((None, R, C), lambda b: (b, 0, 0))
    else:
        tk = min(tk, R)
        grid = (dy.shape[1] // C, R // tk)
        in_specs = [pl.BlockSpec((T, tk), lambda b, k: (0, k)), pl.BlockSpec((T, C), lambda b, k: (0, b))]
        out_specs = pl.BlockSpec((None, tk, C), lambda b, k: (b + first, k, 0))
    buf = list(buf)
    buf[l] = pl.pallas_call(body, grid=grid, in_specs=in_specs + [ANY] * len(extra), out_specs=out_specs,
                            out_shape=jax.ShapeDtypeStruct((nb, R, C), BF),
                            input_output_aliases={2: 0} if extra else {}, name=name,
                            compiler_params=_params(len(grid)))(a, dy, *extra)
    return buf


def _acc_rows(ref, val, first):
    @pl.when(first)
    def _():
        ref[...] = val

    @pl.when(jnp.logical_not(first))
    def _():
        ref[...] += val


def _norm_mm(name, h, g, ws, trans_w, act=False, deps=(), tm=2 * ROW_TILE):
    T = h.shape[0]
    nb, r, cc = ws[0].shape
    bo = r if trans_w else cc
    tm = _tile(T, tm)
    n_w, n_dep = len(ws), len(deps)

    def body(*refs):
        refs = refs[n_dep:]
        h_ref, g_ref, w_refs = refs[0], refs[1], refs[2:2 + n_w]
        n_ref, o_refs, n_s = refs[2 + n_w], refs[3 + n_w:3 + 2 * n_w], refs[-1]

        @pl.when(pl.program_id(1) == 0)
        def _():
            n = _rms_fwd(h_ref[...].astype(F32), g_ref[...]).astype(BF)
            n_s[...] = n
            n_ref[...] = n

        n = n_s[...]
        prods = []
        for w_ref, o_ref in zip(w_refs, o_refs):
            prods.append(lax.dot_general(n, w_ref[...], _NT if trans_w else _NN,
                                         preferred_element_type=F32).astype(BF))
            o_ref[...] = prods[-1]
        if act:
            refs[3 + 2 * n_w][...] = (_silu_and_grad(prods[0].astype(F32))[0] * prods[1].astype(F32)).astype(BF)

    wide = pl.BlockSpec((tm, bo), lambda i, b: (i, b))
    n_out = n_w + (1 if act else 0)
    return pl.pallas_call(
        body, grid=(T // tm, nb),
        in_specs=[ANY] * n_dep + [pl.BlockSpec((tm, D_MODEL), lambda i, b: (i, 0)),
                                  pl.BlockSpec(g.shape, lambda i, b: (0, 0))]
        + [pl.BlockSpec((None, r, cc), lambda i, b: (b, 0, 0))] * n_w,
        out_specs=[pl.BlockSpec((tm, D_MODEL), lambda i, b: (i, 0))] + [wide] * n_out,
        out_shape=[jax.ShapeDtypeStruct((T, D_MODEL), BF)] + [jax.ShapeDtypeStruct((T, nb * bo), BF)] * n_out,
        scratch_shapes=[pltpu.VMEM((tm, D_MODEL), BF)], name=name, compiler_params=_params(2))(*deps, h, g, *ws)


def _mm_res(name, x, w3, h, g, coef, tm=ROW_TILE):
    T, kx = x.shape
    w2 = w3.reshape(kx, D_MODEL)
    tm = _tile(T, tm)

    def body(x_ref, w_ref, h_ref, g_ref, f_ref, o_ref):
        f = jnp.dot(x_ref[...], w_ref[...], preferred_element_type=F32).astype(BF)
        f_ref[...] = f
        o_ref[...] = h_ref[...] + coef * _rms_fwd(f.astype(F32), g_ref[...])

    row = pl.BlockSpec((tm, D_MODEL), lambda i: (i, 0))
    return pl.pallas_call(
        body, grid=(T // tm,),
        in_specs=[pl.BlockSpec((tm, kx), lambda i: (i, 0)), pl.BlockSpec(w2.shape, lambda i: (0, 0)), row,
                  pl.BlockSpec(g.shape, lambda i: (0, 0))],
        out_specs=[row, row],
        out_shape=[jax.ShapeDtypeStruct((T, D_MODEL), BF), jax.ShapeDtypeStruct((T, D_MODEL), F32)],
        name=name, compiler_params=_params(1))(x, w2, h, g)


def _resbwd_mm(name, dh, f, g, coef, w3, trans_w, act=None, deps=(), tm=2 * ROW_TILE):
    T = dh.shape[0]
    nb, r, cc = w3.shape
    bo = r if trans_w else cc
    tm = _tile(T, tm)
    n_dep, n_act = len(deps), 3 if act else 0
    n_i = T // tm

    def body(*refs):
        refs = refs[n_dep:]
        dh_ref, f_ref, g_ref, w_ref = refs[:4]
        df_ref, dg_ref = refs[4 + n_act], refs[5 + n_act]
        df_s = refs[-2] if act else refs[-1]
        i, b = pl.program_id(0), pl.program_id(1)

        @pl.when(b == 0)
        def _():
            dg = jnp.zeros((1, D_MODEL), F32)
            for c in range(tm // EPI_ROWS):
                rows = slice(c * EPI_ROWS, (c + 1) * EPI_ROWS)
                dx, dg_c = _rms_bwd(f_ref[rows, :].astype(F32), g_ref[...], coef * dh_ref[rows, :])
                df_s[rows, :] = dx.astype(BF)
                df_ref[rows, :] = dx.astype(BF)
                dg = dg + dg_c
            _acc_rows(dg_ref, dg, i == 0)

        if act:
            for j in range(bo // MXU_TILE):
                cols = slice(j * MXU_TILE, (j + 1) * MXU_TILE)
                prod = lax.dot_general(df_s[...], w_ref[cols, :], _NT, preferred_element_type=F32)
                val, grad = _silu_and_grad(refs[4][:, cols])
                prod = prod.astype(BF)
                refs[6 + n_act][:, cols] = prod * refs[5][:, cols] * grad
                refs[7 + n_act][:, cols] = prod * val
            acc = refs[-1]
            part = lax.dot_general(refs[6][...], df_s[...], _TN, preferred_element_type=F32)

            @pl.when(i == 0)
            def _():
                acc[b] = part

            @pl.when(i != 0)
            def _():
                acc[b] += part

            @pl.when(i == n_i - 1)
            def _():
                refs[8 + n_act][...] = acc[b].astype(BF)
        else:
            refs[6][...] = lax.dot_general(df_s[...], w_ref[...], _NT if trans_w else _NN,
                                           preferred_element_type=F32).astype(BF)

    row = pl.BlockSpec((tm, D_MODEL), lambda i, b: (i, 0))
    wide = pl.BlockSpec((tm, bo), lambda i, b: (i, b))
    vec = pl.BlockSpec((1, D_MODEL), lambda i, b: (0, 0))
    out_specs = [row, vec] + [wide] * (2 if act else 1)
    out_shape = [jax.ShapeDtypeStruct((T, D_MODEL), BF), jax.ShapeDtypeStruct((1, D_MODEL), F32)]
    out_shape += [jax.ShapeDtypeStruct((T, nb * bo), BF)] * (2 if act else 1)
    scratch = [pltpu.VMEM((tm, D_MODEL), BF)]
    if act:
        out_specs.append(pl.BlockSpec((None, r, cc), lambda i, b: (jnp.where(i == n_i - 1, b, 0), 0, 0)))
        out_shape.append(jax.ShapeDtypeStruct((nb, r, cc), BF))
        scratch.append(pltpu.VMEM((nb, r, cc), F32))
    return pl.pallas_call(
        body, grid=(n_i, nb),
        in_specs=[ANY] * n_dep + [row, row, vec, pl.BlockSpec((None, r, cc), lambda i, b: (b, 0, 0))] + [wide] * n_act,
        out_specs=out_specs, out_shape=out_shape, scratch_shapes=scratch, name=name,
        compiler_params=_params(2))(*deps, dh, f, g, w3, *(act or ()))


def _dn_prenorm(name, xs, ws, trans_w, dh, h, g, tm=2 * ROW_TILE):
    T = dh.shape[0]
    chained = not isinstance(ws, (list, tuple))
    ws = [ws] if chained else list(ws)
    _, r, cc = ws[0].shape
    bw = cc if trans_w else r
    per_x = xs[0].shape[1] // bw
    nb = per_x * len(xs) if chained else per_x
    tm = _tile(T, tm)
    n_x, n_w = len(xs), len(ws)

    def body(*refs):
        x_refs, w_refs = refs[:n_x], refs[n_x:n_x + n_w]
        dh_ref, h_ref, g_ref, o_ref, dg_ref, acc = refs[n_x + n_w:]
        i, b = pl.program_id(0), pl.program_id(1)

        @pl.when(b == 0)
        def _():
            acc[...] = jnp.zeros_like(acc)

        def add(x_ref, w_ref):
            acc[...] += lax.dot_general(x_ref[...], w_ref[...], _NT if trans_w else _NN, preferred_element_type=F32)

        if chained:
            for k, x_ref in enumerate(x_refs):
                pl.when(b // per_x == k)(lambda x_ref=x_ref: add(x_ref, w_refs[0]))
        else:
            for x_ref, w_ref in zip(x_refs, w_refs):
                add(x_ref, w_ref)

        @pl.when(b == nb - 1)
        def _():
            dg = jnp.zeros((1, D_MODEL), F32)
            for c in range(tm // EPI_ROWS):
                rows = slice(c * EPI_ROWS, (c + 1) * EPI_ROWS)
                dx, dg_c = _rms_bwd(h_ref[rows, :], g_ref[...], acc[rows, :])
                o_ref[rows, :] = dh_ref[rows, :] + dx
                dg = dg + dg_c
            _acc_rows(dg_ref, dg, i == 0)

    row = pl.BlockSpec((tm, D_MODEL), lambda i, b: (i, 0))
    vec = pl.BlockSpec((1, D_MODEL), lambda i, b: (0, 0))
    if chained:
        x_specs = [pl.BlockSpec((tm, bw), lambda i, b, k=k: (i, jnp.clip(b - k * per_x, 0, per_x - 1)))
                   for k in range(n_x)]
    else:
        x_specs = [pl.BlockSpec((tm, bw), lambda i, b: (i, b))] * n_x
    return pl.pallas_call(
        body, grid=(T // tm, nb),
        in_specs=x_specs + [pl.BlockSpec((None, r, cc), lambda i, b: (b, 0, 0))] * n_w + [row, row, vec],
        out_specs=[row, vec],
        out_shape=[jax.ShapeDtypeStruct((T, D_MODEL), F32), jax.ShapeDtypeStruct((1, D_MODEL), F32)],
        scratch_shapes=[pltpu.VMEM((tm, D_MODEL), F32)], name=name,
        compiler_params=_params(2))(*xs, *ws, dh, h, g)


def _pool_apply(x, win, row):
    s, k = x, 1
    while k < win:
        s = s + jnp.where(row >= k, pltpu.roll(s, k, 0), 0.0)
        k *= 2
    return s / jnp.minimum(row + 1, win).astype(F32) - x


def _pool_apply_t(dp, win, row):
    T = dp.shape[0]
    s, k = dp / jnp.minimum(row + 1, win).astype(F32), 1
    while k < win:
        s = s + jnp.where(row < T - k, pltpu.roll(s, T - k, 0), 0.0)
        k *= 2
    return s - dp


def _pool_fwd(z, w, scale):
    T = z.shape[0]

    def body(z_ref, w_ref, s_ref, o_ref):
        row = lax.broadcasted_iota(jnp.int32, (T, LANES), 0)
        for gi, win in enumerate(POOL_WINDOWS):
            cols = pl.ds(gi * LANES, LANES)
            pooled = _pool_apply(z_ref[:, cols].astype(F32), win, row)
            y = jnp.dot(pooled.astype(BF), w_ref[gi].astype(BF), preferred_element_type=F32)
            o_ref[:, cols] = (y * s_ref[:, cols]).astype(o_ref.dtype)

    return pl.pallas_call(
        body, grid=(1,),
        in_specs=[pl.BlockSpec((T, BRANCH), lambda i: (0, ZB_POOL)), pl.BlockSpec(w.shape, lambda i: (0, 0, 0)),
                  pl.BlockSpec(scale.shape, lambda i: (0, 0))],
        out_specs=pl.BlockSpec((T, BRANCH), lambda i: (0, 0)), out_shape=jax.ShapeDtypeStruct((T, BRANCH), BF),
        name="pool_fwd", compiler_params=_params(1))(z, w, scale)


def _pool_bwd(dr, z, w, scale):
    T = z.shape[0]

    def body(dr_ref, z_ref, w_ref, s_ref, dz_ref, dw_ref, ds_ref):
        row = lax.broadcasted_iota(jnp.int32, (T, LANES), 0)
        for gi, win in enumerate(POOL_WINDOWS):
            cols = pl.ds(gi * LANES, LANES)
            pooled = _pool_apply(z_ref[:, cols].astype(F32), win, row).astype(BF)
            wg = w_ref[gi].astype(BF)
            y = jnp.dot(pooled, wg, preferred_element_type=F32)
            d = dr_ref[:, cols].astype(F32)
            ds_ref[:, cols] = jnp.sum(d * y, axis=0, keepdims=True)
            dy = (d * s_ref[:, cols]).astype(BF)
            dw_ref[gi] = lax.dot_general(pooled, dy, _TN, preferred_element_type=F32)
            dpooled = lax.dot_general(dy, wg, _NT, preferred_element_type=F32)
            dz_ref[:, cols] = _pool_apply_t(dpooled, win, row).astype(dz_ref.dtype)

    return pl.pallas_call(
        body, grid=(1,),
        in_specs=[pl.BlockSpec((T, BRANCH), lambda i: (0, 0)), pl.BlockSpec((T, BRANCH), lambda i: (0, ZB_POOL)),
                  pl.BlockSpec(w.shape, lambda i: (0, 0, 0)), pl.BlockSpec(scale.shape, lambda i: (0, 0))],
        out_specs=[pl.BlockSpec((T, BRANCH), lambda i: (0, 0)), pl.BlockSpec(w.shape, lambda i: (0, 0, 0)),
                   pl.BlockSpec(scale.shape, lambda i: (0, 0))],
        out_shape=[jax.ShapeDtypeStruct((T, BRANCH), BF), jax.ShapeDtypeStruct(w.shape, F32),
                   jax.ShapeDtypeStruct(scale.shape, F32)],
        name="pool_bwd", compiler_params=_params(1))(dr, z, w, scale)


def _tril(transposed=False):
    r = lax.broadcasted_iota(jnp.int32, (CHUNK, CHUNK), 0)
    c = lax.broadcasted_iota(jnp.int32, (CHUNK, CHUNK), 1)
    return c >= r if transposed else r >= c


def _sgu_fwd(z, ln_g, ln_b, w_s, bias):
    T = z.shape[0]
    tm = _tile(T)

    def body(zu_ref, zv_ref, g_ref, b_ref, w_ref, bias_ref, o_ref):
        gu, _ = _gelu_and_grad(zu_ref[...].astype(F32))
        gv, _ = _gelu_and_grad(zv_ref[...].astype(F32))
        xh, _ = _ln_stats(gv)
        v16 = (xh * g_ref[...] + b_ref[...]).astype(BF)
        tri = _tril()
        for h in range(SGU_HEADS):
            cols = slice(h * LANES, (h + 1) * LANES)
            wh = jnp.where(tri, w_ref[h], 0.0).astype(BF)
            for c in range(tm // CHUNK):
                rows = slice(c * CHUNK, (c + 1) * CHUNK)
                s = jnp.dot(wh, v16[rows, cols], preferred_element_type=F32) + bias_ref[:, cols]
                o_ref[rows, cols] = (gu[rows, cols] * s).astype(o_ref.dtype)

    small = [pl.BlockSpec(a.shape, lambda i, n=a.ndim: (0,) * n) for a in (ln_g, ln_b, w_s, bias)]
    return pl.pallas_call(
        body, grid=(T // tm,),
        in_specs=[pl.BlockSpec((tm, BRANCH), lambda i: (i, ZB_U)), pl.BlockSpec((tm, BRANCH), lambda i: (i, ZB_V))] + small,
        out_specs=pl.BlockSpec((tm, BRANCH), lambda i: (i, 0)), out_shape=jax.ShapeDtypeStruct((T, BRANCH), BF),
        name="sgu_fwd", compiler_params=_params(1))(z, z, ln_g, ln_b, w_s, bias)


def _sgu_bwd(dr, z, ln_g, ln_b, w_s, w_st, bias):
    T = z.shape[0]
    tm = _tile(T)
    n_steps = T // tm

    def body(dr_ref, zu_ref, zv_ref, g_ref, b_ref, w_ref, wt_ref, bias_ref,
             dzu_ref, dzv_ref, dg_ref, db_ref, dw_ref, dbias_ref, dgu_s, dv_s):
        i = pl.program_id(0)

        @pl.when(i == 0)
        def _():
            dg_ref[...] = jnp.zeros_like(dg_ref)
            db_ref[...] = jnp.zeros_like(db_ref)
            dw_ref[...] = jnp.zeros_like(dw_ref)
            dbias_ref[...] = jnp.zeros_like(dbias_ref)

        zu = zu_ref[...].astype(F32)
        zv = zv_ref[...].astype(F32)
        gu, gu_grad = _gelu_and_grad(zu)
        gv, gv_grad = _gelu_and_grad(zv)
        xh, r = _ln_stats(gv)
        v16 = (xh * g_ref[...] + b_ref[...]).astype(BF)
        dr = dr_ref[...].astype(F32)
        tri = _tril()
        for h in range(SGU_HEADS):
            cols = slice(h * LANES, (h + 1) * LANES)
            wh = jnp.where(tri, w_ref[h], 0.0).astype(BF)
            wht = jnp.where(_tril(transposed=True), wt_ref[h], 0.0).astype(BF)
            for c in range(tm // CHUNK):
                rows = slice(c * CHUNK, (c + 1) * CHUNK)
                v_blk = v16[rows, cols]
                s = jnp.dot(wh, v_blk, preferred_element_type=F32) + bias_ref[:, cols]
                ds = dr[rows, cols] * gu[rows, cols]
                dgu_s[rows, cols] = dr[rows, cols] * s
                ds16 = ds.astype(BF)
                dw_ref[h] += jnp.where(tri, lax.dot_general(ds16, v_blk, _NT, preferred_element_type=F32), 0.0)
                dv_s[rows, cols] = jnp.dot(wht, ds16, preferred_element_type=F32)
                dbias_ref[:, cols] += ds
        dzu_ref[...] = (dgu_s[...] * gu_grad).astype(dzu_ref.dtype)
        dgv, dg, db = _ln_bwd(xh, r, g_ref[...], dv_s[...])
        dzv_ref[...] = (dgv * gv_grad).astype(dzv_ref.dtype)
        dg_ref[...] += dg
        db_ref[...] += db

        @pl.when(i == n_steps - 1)
        def _():
            for h in range(SGU_HEADS):
                cols = slice(h * LANES, (h + 1) * LANES)
                tot = jnp.sum(dbias_ref[:, cols], axis=1, keepdims=True)
                dbias_ref[:, cols] = jnp.broadcast_to(tot, (CHUNK, LANES))

    small = (ln_g, ln_b, w_s, w_st, bias)
    small_specs = [pl.BlockSpec(a.shape, lambda i, n=a.ndim: (0,) * n) for a in small]
    return pl.pallas_call(
        body, grid=(n_steps,),
        in_specs=[pl.BlockSpec((tm, BRANCH), lambda i: (i, 0)), pl.BlockSpec((tm, BRANCH), lambda i: (i, ZB_U)),
                  pl.BlockSpec((tm, BRANCH), lambda i: (i, ZB_V))] + small_specs,
        out_specs=[pl.BlockSpec((tm, BRANCH), lambda i: (i, 0)), pl.BlockSpec((tm, BRANCH), lambda i: (i, 0)),
                   pl.BlockSpec((1, BRANCH), lambda i: (0, 0)), pl.BlockSpec((1, BRANCH), lambda i: (0, 0)),
                   pl.BlockSpec(w_s.shape, lambda i: (0, 0, 0)), pl.BlockSpec(bias.shape, lambda i: (0, 0))],
        out_shape=[jax.ShapeDtypeStruct((T, BRANCH), BF), jax.ShapeDtypeStruct((T, BRANCH), BF),
                   jax.ShapeDtypeStruct((1, BRANCH), F32), jax.ShapeDtypeStruct((1, BRANCH), F32),
                   jax.ShapeDtypeStruct(w_s.shape, F32), jax.ShapeDtypeStruct(bias.shape, F32)],
        scratch_shapes=[pltpu.VMEM((tm, BRANCH), F32), pltpu.VMEM((tm, BRANCH), F32)],
        name="sgu_bwd", compiler_params=_params(1))(dr, z, z, ln_g, ln_b, w_s, w_st, bias)


def _conv_fwd(z, convk, l, bias):
    T = z.shape[0]

    def body(za_ref, zb_ref, k_ref, b_ref, o_ref):
        xg = za_ref[...].astype(F32) * _sigmoid(zb_ref[...].astype(F32))
        xp = jnp.concatenate([jnp.zeros((CONV_PAD, LANES), F32), xg], axis=0)
        kw = k_ref[...]
        acc = jnp.broadcast_to(b_ref[...], (T, LANES))
        for s in range(SUBLANES):
            xs = xp if s == 0 else pltpu.roll(xp, s, 0)
            for q in range(CONV_PAD // SUBLANES):
                k = CONV_TAPS - 1 - (SUBLANES * q + s)
                if k >= 0:
                    lo = CONV_PAD - SUBLANES * q
                    acc = acc + kw[k:k + 1, :] * xs[lo:lo + T, :]
        o_ref[...] = acc.astype(o_ref.dtype)

    return pl.pallas_call(
        body, grid=(4,),
        in_specs=[pl.BlockSpec((T, LANES), lambda g: (0, 4 * ZB_A + g)),
                  pl.BlockSpec((T, LANES), lambda g: (0, 4 * ZB_B + g)),
                  pl.BlockSpec((None, CONV_PAD, LANES), lambda g: (g, 0, 0)),
                  pl.BlockSpec((1, LANES), lambda g: (0, g))],
        out_specs=pl.BlockSpec((T, LANES), lambda g: (0, g)), out_shape=jax.ShapeDtypeStruct((T, BRANCH), BF),
        name="conv_fwd", compiler_params=_params(1))(z, z, convk[l], bias)


def _conv_bwd(dy, z, convk, l):
    T = z.shape[0]

    def body(dy_ref, za_ref, zb_ref, k_ref, dza_ref, dzb_ref, dk_ref, db_ref):
        a = za_ref[...].astype(F32)
        sg = _sigmoid(zb_ref[...].astype(F32))
        d = dy_ref[...].astype(F32)
        kw = k_ref[...]
        xp = jnp.concatenate([jnp.zeros((CONV_PAD, LANES), F32), a * sg], axis=0)
        dp = jnp.concatenate([d, jnp.zeros((CONV_PAD, LANES), F32)], axis=0)
        dxg = jnp.zeros((T, LANES), F32)
        dk_ref[...] = jnp.zeros_like(dk_ref)
        for s in range(SUBLANES):
            xs = xp if s == 0 else pltpu.roll(xp, s, 0)
            ds = dp if s == 0 else pltpu.roll(dp, T + CONV_PAD - s, 0)
            for q in range(CONV_PAD // SUBLANES):
                k = CONV_TAPS - 1 - (SUBLANES * q + s)
                if k >= 0:
                    lo = CONV_PAD - SUBLANES * q
                    dk_ref[k:k + 1, :] = jnp.sum(d * xs[lo:lo + T, :], axis=0, keepdims=True)
                    dxg = dxg + kw[k:k + 1, :] * ds[SUBLANES * q:SUBLANES * q + T, :]
        db_ref[...] = jnp.sum(d, axis=0, keepdims=True)
        dza_ref[...] = (dxg * sg).astype(dza_ref.dtype)
        dzb_ref[...] = (dxg * a * sg * (1.0 - sg)).astype(dzb_ref.dtype)

    col = pl.BlockSpec((T, LANES), lambda g: (0, g))
    return pl.pallas_call(
        body, grid=(4,),
        in_specs=[col, pl.BlockSpec((T, LANES), lambda g: (0, 4 * ZB_A + g)),
                  pl.BlockSpec((T, LANES), lambda g: (0, 4 * ZB_B + g)),
                  pl.BlockSpec((None, CONV_PAD, LANES), lambda g: (g, 0, 0))],
        out_specs=[col, col, pl.BlockSpec((CONV_PAD, LANES), lambda g: (0, g)),
                   pl.BlockSpec((1, LANES), lambda g: (0, g))],
        out_shape=[jax.ShapeDtypeStruct((T, BRANCH), BF), jax.ShapeDtypeStruct((T, BRANCH), BF),
                   jax.ShapeDtypeStruct((CONV_PAD, BRANCH), F32), jax.ShapeDtypeStruct((1, BRANCH), F32)],
        name="conv_bwd", compiler_params=_params(1))(dy, z, z, convk[l])


D = D_MODEL


def _ffn_fwd(l, h, S, W, pre, deps=()):
    n, gp, u, a = _norm_mm("ffn_in", h, S[pre + "_pre_g"], [W[pre + "_w_gate"][l], W[pre + "_w_up"][l]], True,
                           act=True, deps=deps)
    f, out = _mm_res("ffn_out", a, W[pre + "_w_down"][l], h, S[pre + "_post_g"], 0.5)
    return out, dict(h=h, n=n, gp=gp, u=u, a=a, f=f)


def _ffn_bwd(l, dh, sv, S, W, G, SG, pre, deps=()):
    df, SG[pre + "_post_g"], dgp, du, dwd = _resbwd_mm(
        "ffn_bwd_act", dh, sv["f"], S[pre + "_post_g"], 0.5, W[pre + "_w_down"][l], True,
        act=(sv["gp"], sv["u"], sv["a"]), deps=deps, tm=ROW_TILE)
    G[pre + "_w_down"] = G[pre + "_w_down"][:l] + [dwd] + G[pre + "_w_down"][l + 1:]
    G[pre + "_w_gate"] = _mm_tn("ffn_dw_gate", dgp, sv["n"], G[pre + "_w_gate"], l, True)
    G[pre + "_w_up"] = _mm_tn("ffn_dw_up", du, sv["n"], G[pre + "_w_up"], l, True)
    dh_in, SG[pre + "_pre_g"] = _dn_prenorm("ffn_bwd_in", [dgp, du], [W[pre + "_w_gate"][l], W[pre + "_w_up"][l]],
                                            False, dh, sv["h"], S[pre + "_pre_g"])
    return dh_in


def _gates(zg):
    return [_sigmoid(jnp.concatenate([zg[2 * k].astype(F32), zg[2 * k + 1].astype(F32)], axis=1)) for k in range(3)]


def _merge_fwd(z, rs, ws, tm=ROW_TILE):
    T = z.shape[0]
    tm = _tile(T, tm)
    nb, kk, bw = ws[0].shape

    def body(*refs):
        r_refs, g_refs, w_refs, y_refs, m_ref = refs[:3], refs[3:9], refs[9:12], refs[12:15], refs[15]
        for r_ref, w_ref, y_ref in zip(r_refs, w_refs, y_refs):
            for b in range(nb):
                y_ref[:, b * bw:(b + 1) * bw] = jnp.dot(r_ref[...], w_ref[b],
                                                        preferred_element_type=F32).astype(y_ref.dtype)
        g = _gates([q[...] for q in g_refs])
        m_ref[...] = (g[0] * y_refs[0][...].astype(F32) + g[1] * y_refs[1][...].astype(F32)
                      + g[2] * y_refs[2][...].astype(F32)).astype(m_ref.dtype)

    row = pl.BlockSpec((tm, D_MODEL), lambda i: (i, 0))
    return pl.pallas_call(
        body, grid=(T // tm,),
        in_specs=[pl.BlockSpec((tm, kk), lambda i: (i, 0))] * 3
        + [pl.BlockSpec((tm, BRANCH), lambda i, j=j: (i, ZB_GATES + j)) for j in range(6)]
        + [pl.BlockSpec(ws[0].shape, lambda i: (0, 0, 0))] * 3,
        out_specs=[row] * 4, out_shape=[jax.ShapeDtypeStruct((T, D_MODEL), BF)] * 4,
        name="mix_merge", compiler_params=_params(1))(*rs, *[z] * 6, *ws)


def _merge_bwd(dmerged, z, ys, ws, tm=ROW_TILE):
    T = z.shape[0]
    tm = _tile(T, tm)
    nb, kk, bw = ws[0].shape

    def body(*refs):
        dm_ref, g_refs, y_refs, w_refs = refs[0], refs[1:7], refs[7:10], refs[10:13]
        dy_refs, lo_ref, hi_ref, dr_refs = refs[13:16], refs[16], refs[17], refs[18:21]
        cut = DZ_HALF - ZB_GATES * BRANCH
        dm = dm_ref[...].astype(F32)
        g = _gates([q[...] for q in g_refs])
        for k in range(3):
            dy_refs[k][...] = (dm * g[k]).astype(BF)
            dzg = (dm * y_refs[k][...].astype(F32) * g[k] * (1.0 - g[k])).astype(BF)
            if k == 0:
                lo_ref[...] = dzg[:, :cut]
                hi_ref[:, :D_MODEL - cut] = dzg[:, cut:]
            else:
                hi_ref[:, k * D_MODEL - cut:(k + 1) * D_MODEL - cut] = dzg
            dr = None
            for b in range(nb):
                p = lax.dot_general(dy_refs[k][:, b * bw:(b + 1) * bw], w_refs[k][b], _NT,
                                    preferred_element_type=F32)
                dr = p if dr is None else dr + p
            dr_refs[k][...] = dr.astype(BF)

    row = pl.BlockSpec((tm, D_MODEL), lambda i: (i, 0))
    return pl.pallas_call(
        body, grid=(T // tm,),
        in_specs=[row] + [pl.BlockSpec((tm, BRANCH), lambda i, j=j: (i, ZB_GATES + j)) for j in range(6)] + [row] * 3
        + [pl.BlockSpec(ws[0].shape, lambda i: (0, 0, 0))] * 3,
        out_specs=[row] * 3 + [pl.BlockSpec((tm, DZ_HALF - ZB_GATES * BRANCH), lambda i: (i, 0)),
                               pl.BlockSpec((tm, DZ_HALF), lambda i: (i, 0))]
        + [pl.BlockSpec((tm, kk), lambda i: (i, 0))] * 3,
        out_shape=[jax.ShapeDtypeStruct((T, D_MODEL), BF)] * 3
        + [jax.ShapeDtypeStruct((T, DZ_HALF - ZB_GATES * BRANCH), BF), jax.ShapeDtypeStruct((T, DZ_HALF), BF)]
        + [jax.ShapeDtypeStruct((T, kk), BF)] * 3,
        name="mix_merge_bwd", compiler_params=_params(1))(dmerged, *[z] * 6, *ys, *ws)


def _mix_fwd(l, h, S, W, deps=()):
    n, z = _norm_mm("mix_in", h, S["mix_pre_g"], [W["w_in"][l]], False, deps=deps)
    r_pool = _pool_fwd(z, S["pool_w"], S["pool_scale"])
    r_sgu = _sgu_fwd(z, S["sgu_ln_g"], S["sgu_ln_b"], S["sgu_w_s"], S["sgu_bias"])
    yc = _conv_fwd(z, W["conv_dw_k"], l, S["conv_dw_b"])

    def ln_silu(y, g, b):
        xh, _ = _ln_stats(y.astype(F32))
        return (_silu_and_grad(xh * g + b)[0],)

    r_conv = _rowwise("conv_ln", ln_silu, [(yc, BRANCH, 0)], [S["conv_ln_g"], S["conv_ln_b"]], [(BRANCH, BF)])[0]
    y_pool, y_sgu, y_conv, merged = _merge_fwd(z, (r_pool, r_sgu, r_conv),
                                               [W["w_%s_out" % br][l] for br in ("pool", "sgu", "conv")])
    o, out = _mm_res("mix_out", merged, W["w_out"][l], h, S["mix_post_g"], 1.0, tm=2 * ROW_TILE)
    return out, dict(h=h, n=n, z=z, r_pool=r_pool, r_sgu=r_sgu, yc=yc, r_conv=r_conv, y_pool=y_pool, y_sgu=y_sgu,
                     y_conv=y_conv, merged=merged, o=o)


def _branch_dw(rs, dys, shape):
    T, kk = rs[0].shape
    nb, _, bw = shape

    def body(*refs):
        for k in range(3):
            refs[6 + k][...] = lax.dot_general(refs[k][...], refs[3 + k][...], _TN,
                                               preferred_element_type=F32).astype(BF)

    return pl.pallas_call(
        body, grid=(nb,),
        in_specs=[pl.BlockSpec((T, kk), lambda b: (0, 0))] * 3 + [pl.BlockSpec((T, bw), lambda b: (0, b))] * 3,
        out_specs=[pl.BlockSpec((None, kk, bw), lambda b: (b, 0, 0))] * 3,
        out_shape=[jax.ShapeDtypeStruct((nb, kk, bw), BF)] * 3, name="branch_dw",
        compiler_params=_params(1))(*rs, *dys)


def _mix_bwd(l, dh, sv, S, W, G, SG, deps=()):
    z = sv["z"]
    do, SG["mix_post_g"], dmerged = _resbwd_mm("mix_bwd_out", dh, sv["o"], S["mix_post_g"], 1.0,
                                               W["w_out"][l].reshape(1, D, D), True, deps=deps)
    G["w_out"] = _mm_tn("mix_dw_out", sv["merged"], do, G["w_out"], l, True)

    branches = ("pool", "sgu", "conv")
    res = _merge_bwd(dmerged, z, [sv["y_" + br] for br in branches], [W["w_%s_out" % br][l] for br in branches])
    dz_gate_lo, dz_hi, dr = res[3], res[4], dict(zip(branches, res[5:]))
    for br, dw in zip(branches, _branch_dw([sv["r_" + br] for br in branches], res[:3], G["w_pool_out"][l].shape)):
        wn = "w_%s_out" % br
        G[wn] = G[wn][:l] + [dw] + G[wn][l + 1:]
    dz_pool, SG["pool_w"], SG["pool_scale"] = _pool_bwd(dr["pool"], z, S["pool_w"], S["pool_scale"])
    dzu, dzv, SG["sgu_ln_g"], SG["sgu_ln_b"], SG["sgu_w_s"], dbias = _sgu_bwd(
        dr["sgu"], z, S["sgu_ln_g"], S["sgu_ln_b"], S["sgu_w_s"], S["sgu_w_st"], S["sgu_bias"])
    SG["sgu_b_s"] = dbias[:, ::LANES].T

    def ln_silu_bwd(d, y, g, b):
        xh, r = _ln_stats(y.astype(F32))
        _, grad = _silu_and_grad(xh * g + b)
        return _ln_bwd(xh, r, g, d.astype(F32) * grad)

    dyc, SG["conv_ln_g"], SG["conv_ln_b"] = _rowwise(
        "conv_ln_bwd", ln_silu_bwd, [(dr["conv"], BRANCH, 0), (sv["yc"], BRANCH, 0)],
        [S["conv_ln_g"], S["conv_ln_b"]], [(BRANCH, BF)], [(1, BRANCH), (1, BRANCH)])
    dza, dzb, SG["conv_dw_k"], SG["conv_dw_b"] = _conv_bwd(dyc, z, W["conv_dw_k"], l)
    dz_lo = jnp.concatenate([dz_pool, dzu, dzv, dza, dzb, dz_gate_lo], axis=1)
    G["w_in"] = _mm_tn("mix_dw_in", sv["n"], dz_lo, G["w_in"], l, False)
    G["w_in"] = _mm_tn("mix_dw_in", sv["n"], dz_hi, G["w_in"], l, False, first=2)
    dh_in, SG["mix_pre_g"] = _dn_prenorm("mix_bwd_in", [dz_lo, dz_hi], W["w_in"][l], True, dh, sv["h"],
                                         S["mix_pre_g"])
    return dh_in


def _ple_out(h, p, gp, w3, g, tm=ROW_TILE):
    T, kp = p.shape
    nb, _, bw = w3.shape
    tm = _tile(T, tm)

    def body(h_ref, p_ref, gp_ref, w_ref, g_ref, e_ref, o_ref):
        p16 = p_ref[...].astype(BF)
        for b in range(nb):
            e_ref[:, b * bw:(b + 1) * bw] = jnp.dot(p16, w_ref[b], preferred_element_type=F32).astype(BF)
        q = _sigmoid(gp_ref[...].astype(F32)) * e_ref[...].astype(F32)
        o_ref[...] = h_ref[...] + _rms_fwd(q, g_ref[...])

    row = pl.BlockSpec((tm, D_MODEL), lambda i: (i, 0))
    return pl.pallas_call(
        body, grid=(T // tm,),
        in_specs=[row, pl.BlockSpec((tm, kp), lambda i: (i, 0)), row, pl.BlockSpec(w3.shape, lambda i: (0, 0, 0)),
                  pl.BlockSpec(g.shape, lambda i: (0, 0))],
        out_specs=[row, row],
        out_shape=[jax.ShapeDtypeStruct((T, D_MODEL), BF), jax.ShapeDtypeStruct((T, D_MODEL), F32)],
        name="ple_out", compiler_params=_params(1))(h, p, gp, w3, g)


def _ple_fwd(l, h, p_l, S, W, deps=()):
    n, gp = _norm_mm("ple_in", h, S["ple_pre_g"], [W["ple_w_gate"][l].reshape(1, D, D)], False, deps=deps)
    e, out = _ple_out(h, p_l, gp, W["ple_w_proj"][l], S["ple_post_g"])
    return out, dict(h=h, n=n, e=e, gp=gp, p=p_l)


def _ple_bwd_rows(dh, e, gp, g_post, w3, h, g_pre, deps=(), tm=ROW_TILE):
    T = dh.shape[0]
    w2 = w3.reshape(D_MODEL, D_MODEL)
    tm = _tile(T, tm)
    n_dep = len(deps)

    def body(*refs):
        dh_ref, e_ref, gp_ref, gpost_ref, w_ref, h_ref, gpre_ref, de_ref, dgp_ref, o_ref, dpost_ref, dpre_ref = \
            refs[n_dep:]
        first = pl.program_id(0) == 0
        d = dh_ref[...]
        sg = _sigmoid(gp_ref[...].astype(F32))
        ee = e_ref[...].astype(F32)
        dq, dpost = _rms_bwd(sg * ee, gpost_ref[...], d)
        de_ref[...] = (dq * sg).astype(BF)
        dgp = (dq * ee * sg * (1.0 - sg)).astype(BF)
        dgp_ref[...] = dgp
        dn = lax.dot_general(dgp, w_ref[...], _NT, preferred_element_type=F32)
        dx, dpre = _rms_bwd(h_ref[...], gpre_ref[...], dn)
        o_ref[...] = d + dx
        _acc_rows(dpost_ref, dpost, first)
        _acc_rows(dpre_ref, dpre, first)

    row = pl.BlockSpec((tm, D_MODEL), lambda i: (i, 0))
    vec = pl.BlockSpec((1, D_MODEL), lambda i: (0, 0))
    return pl.pallas_call(
        body, grid=(T // tm,),
        in_specs=[ANY] * n_dep + [row, row, row, vec, pl.BlockSpec(w2.shape, lambda i: (0, 0)), row, vec],
        out_specs=[row, row, row, vec, vec],
        out_shape=[jax.ShapeDtypeStruct((T, D_MODEL), BF)] * 2 + [jax.ShapeDtypeStruct((T, D_MODEL), F32)]
        + [jax.ShapeDtypeStruct((1, D_MODEL), F32)] * 2,
        name="ple_bwd", compiler_params=_params(1))(*deps, dh, e, gp, g_post, w2, h, g_pre)


def _ple_bwd(l, dh, sv, S, W, G, SG, deps=()):
    de, dgp, dh_in, SG["ple_post_g"], SG["ple_pre_g"] = _ple_bwd_rows(
        dh, sv["e"], sv["gp"], S["ple_post_g"], W["ple_w_gate"][l], sv["h"], S["ple_pre_g"], deps)
    G["ple_w_proj"] = _mm_tn("ple_dw_proj", sv["p"], de, G["ple_w_proj"], l, False)
    G["ple_w_gate"] = _mm_tn("ple_dw_gate", sv["n"], dgp, G["ple_w_gate"], l, True)
    return dh_in


def _layer_small(a, l):
    S = {}
    for name in SMALL:
        v = a[name][l]
        S[name] = v.reshape(1, -1) if v.ndim == 1 else v
    S["sgu_w_st"] = jnp.swapaxes(S["sgu_w_s"], 1, 2)
    S["sgu_bias"] = jnp.repeat(S["sgu_b_s"].T, LANES, axis=1)
    return S


def _layer_fwd(l, h, p_l, S, W, deps=(), hooks=None):
    hooks = hooks or {}

    def after(part, hv):
        return hooks[part](hv) if part in hooks else ()

    h, sv1 = _ffn_fwd(l, h, S, W, "ffn1", deps)
    h, sv2 = _mix_fwd(l, h, S, W, after("ffn1", h))
    h, sv3 = _ffn_fwd(l, h, S, W, "ffn2", after("mix", h))
    h, sv4 = _ple_fwd(l, h, p_l, S, W, after("ffn2", h))
    return h, (sv1, sv2, sv3, sv4)


def _layer_bwd(l, dh, sv, S, W, G, deps=(), hooks=None):
    hooks = hooks or {}

    def after(part, dv):
        return hooks[part](dv) if part in hooks else ()

    SG = {}
    dh = _ple_bwd(l, dh, sv[3], S, W, G, SG, deps)
    dh = _ffn_bwd(l, dh, sv[2], S, W, G, SG, "ffn2")
    dh = _mix_bwd(l, dh, sv[1], S, W, G, SG, after("ffn2", dh))
    dh = _ffn_bwd(l, dh, sv[0], S, W, G, SG, "ffn1", after("mix", dh))
    return dh, SG


HBM = pl.BlockSpec(memory_space=pltpu.HBM)
SEM = pl.BlockSpec(memory_space=pltpu.SEMAPHORE)
SIDE_EFFECT = pltpu.SideEffectType.DATAFLOW_SIDE_EFFECTING


def _place():
    x, y, c = lax.axis_index("x"), lax.axis_index("y"), lax.axis_index("c")
    chips = [(1 - x, y), (x, 1 - y), (1 - x, 1 - y)]
    return x, y, c, chips


def _remote(src, dst, send_sem, recv_sem, to):
    return pltpu.make_async_remote_copy(src_ref=src, dst_ref=dst, send_sem=send_sem, recv_sem=recv_sem,
                                        device_id=to, device_id_type=MESH)


def _split_start(name, plan, bufs, deps):
    count, fn = plan
    n, nd = len(bufs), len(deps)

    def body(*refs):
        send, recv = refs[nd + n], refs[nd + n + 1]
        x, y, c, chips = _place()
        for k, (src, dst, _, to) in enumerate(fn(refs[nd:nd + n], x, y, c, chips)):
            _remote(src, dst, send.at[k], recv.at[k], to).start()
        refs[-1][...] = jnp.zeros_like(refs[-1])

    res = pl.pallas_call(
        body, in_specs=[ANY] * nd + [HBM] * n,
        out_specs=[SEM, SEM] + [HBM] * n + [pl.BlockSpec(memory_space=pltpu.VMEM)],
        out_shape=[pltpu.SemaphoreType.DMA((count,)), pltpu.SemaphoreType.DMA((count,))]
        + [pltpu.HBM(b.shape, b.dtype) for b in bufs] + [jax.ShapeDtypeStruct((8, LANES), F32)],
        input_output_aliases={nd + i: 2 + i for i in range(n)}, name=name,
        compiler_params=pltpu.CompilerParams(has_side_effects=SIDE_EFFECT),
    )(*deps, *[pltpu.with_memory_space_constraint(b, pltpu.HBM) for b in bufs])
    return (res[0], res[1]), list(res[2:2 + n]), res[-1]


def _split_wait(name, plan, sems, bufs, after):
    _, fn = plan
    n = len(bufs)

    def body(*refs):
        send, recv = refs[n], refs[n + 1]
        x, y, c, chips = _place()
        for k, (src, _, land, to) in enumerate(fn(refs[:n], x, y, c, chips)):
            cp = _remote(src, land, send.at[k], recv.at[k], to)
            cp.wait_send()
            cp.wait_recv()

    res = pl.pallas_call(
        body, in_specs=[HBM] * n + [SEM, SEM] + [ANY] * len(after), out_specs=[HBM] * n,
        out_shape=[pltpu.HBM(b.shape, b.dtype) for b in bufs], input_output_aliases={i: i for i in range(n)},
        name=name, compiler_params=pltpu.CompilerParams(has_side_effects=SIDE_EFFECT))(*bufs, *sems, *after)
    return list(res)


def _gather_plans(n):
    def across(b, x, y, c, chips):
        me, out = 2 * x + y, []
        for a in range(n):
            rh = b[a].shape[1] // 2
            mine = b[a].at[me, pl.ds(c * rh, rh)]
            for cx, cy in chips:
                out.append((mine, mine, b[a].at[2 * cx + cy, pl.ds(c * rh, rh)], (cx, cy, c)))
        return out

    def to_sibling(b, x, y, c, chips):
        out = []
        for a in range(n):
            rh = b[a].shape[1] // 2
            for cx, cy in chips:
                piece = b[a].at[2 * cx + cy, pl.ds(c * rh, rh)]
                out.append((piece, piece, b[a].at[2 * cx + cy, pl.ds((1 - c) * rh, rh)], (x, y, 1 - c)))
        return out

    return (3 * n, across), (3 * n, to_sibling)


def _pair_plan(n):
    def fn(b, x, y, c, chips):
        out = []
        for a in range(n):
            rh = b[a].shape[1] // 2
            out.append((b[a].at[:, pl.ds((1 - c) * rh, rh)], b[n + a], b[n + a], (x, y, 1 - c)))
        return out

    return n, fn


def _cross_plan(n):
    def fn(b, x, y, c, chips):
        out = []
        for a in range(n):
            for j, (cx, cy) in enumerate(chips):
                out.append((b[a].at[2 * cx + cy], b[n + a].at[j], b[n + a].at[j], (cx, cy, c)))
        return out

    return 3 * n, fn


def _share_plan(n, l):
    def fn(b, x, y, c, chips):
        out = []
        for a in range(n):
            rh = b[a].shape[1] // 2
            mine = b[a].at[l, pl.ds(c * rh, rh)]
            out.append((mine, mine, b[a].at[l, pl.ds((1 - c) * rh, rh)], (x, y, 1 - c)))
        return out

    return n, fn


def _peers(x, y, c):
    return [(1 - x if m & 4 else x, 1 - y if m & 2 else y, 1 - c if m & 1 else c) for m in range(1, 8)]


def _small_plans():
    def scatter(b, x, y, c, chips):
        return [(b[0].at[4 * px + 2 * py + pc], b[1].at[m], b[1].at[m], (px, py, pc))
                for m, (px, py, pc) in enumerate(_peers(x, y, c))]

    def gather(b, x, y, c, chips):
        mine = b[0].at[4 * x + 2 * y + c]
        return [(mine, mine, b[0].at[4 * px + 2 * py + pc], (px, py, pc)) for px, py, pc in _peers(x, y, c)]

    return (7, scatter), (7, gather)


def _sum_small(v3, got, pos):
    rs = v3.shape[1]
    tm = _tile(rs)
    ins = [(v3, (None, tm, LANES), lambda i, p: (p[2], i, 0))]
    ins += [(got, (None, tm, LANES), lambda i, p, m=m: (m, i, 0)) for m in range(7)]
    return _tiled("sum_small", lambda *t: (((((((t[0] + t[1]) + t[2]) + t[3]) + t[4]) + t[5]) + t[6]) + t[7],),
                  (rs // tm,), pos, ins, [((8, rs, LANES), F32, (None, tm, LANES), lambda i, p: (p[2], i, 0))])[0]


ADD_ROWS = 256


def _multi_tiled(name, fn, pos, groups, in_place=False):
    steps = max(g[2] for g in groups)
    flat_in, in_specs, out_specs, out_shape, counts, dests = [], [], [], [], [], []
    for ins, rows, n_t, (shape, dtype, oidx, dest) in groups:
        for arr, idx in ins:
            flat_in.append(arr)
            in_specs.append(pl.BlockSpec((rows, arr.shape[1]),
                                         lambda i, p, idx=idx, n_t=n_t: (idx(jnp.minimum(i, n_t - 1), p), 0)))
        out_specs.append(pl.BlockSpec((rows, shape[1]),
                                      lambda i, p, oidx=oidx, n_t=n_t: (oidx(jnp.minimum(i, n_t - 1), p), 0)))
        out_shape.append(jax.ShapeDtypeStruct(shape, dtype))
        counts.append((len(ins), n_t))
        dests.append(dest)
    n_in = len(flat_in)
    extra = dests if in_place else []

    def body(_, *refs):
        outs = refs[n_in + len(extra):]
        k = 0
        for (n_a, n_t), o_ref in zip(counts, outs):
            tiles = refs[k:k + n_a]
            k += n_a

            @pl.when(pl.program_id(0) < n_t)
            def _(tiles=tiles, o_ref=o_ref):
                o_ref[...] = fn(*[t[...] for t in tiles]).astype(o_ref.dtype)

    spec = pltpu.PrefetchScalarGridSpec(num_scalar_prefetch=1, grid=(steps,),
                                        in_specs=in_specs + [ANY] * len(extra), out_specs=out_specs)
    return pl.pallas_call(body, grid_spec=spec, out_shape=out_shape,
                          input_output_aliases={1 + n_in + k: k for k in range(len(extra))}, name=name,
                          compiler_params=_params(1))(pos, *flat_in, *extra)


def _add_pair(grads, got, pos):
    groups = []
    for g, q in zip(grads, got):
        nb, R, C = g.shape
        rh = R // 2
        rows = _tile(rh, ADD_ROWS)
        nh = rh // rows
        groups.append(([(g.reshape(nb * R, C), lambda t, p, nh=nh: (t // nh) * 2 * nh + p[1] * nh + t % nh),
                        (q.reshape(nb * rh, C), lambda t, p: t)], rows, nb * nh,
                       ((nb * rh, C), BF, lambda t, p: t, None)))
    res = _multi_tiled("rs_add_pair", lambda u, w: u.astype(F32) + w.astype(F32), pos, groups)
    return [t.reshape(q.shape) for t, q in zip(res, got)]


def _add_chips(parts, slots, reduced, l, pos):
    def add(own, s0, s1, s2):
        return ((own.astype(F32) + s0.astype(F32)) + s1.astype(F32)) + s2.astype(F32)

    groups = []
    for t, s, red in zip(parts, slots, reduced):
        nb, rh, C = t.shape
        L = red.shape[0]
        rows = _tile(rh, ADD_ROWS)
        nh = rh // rows
        ins = [(t.reshape(nb * rh, C), lambda i, p, nh=nh: p[0] * nh + i)]
        ins += [(s.reshape(3 * rh, C), lambda i, p, j=j, nh=nh: j * nh + i) for j in range(3)]
        groups.append((ins, rows, nh, ((L * 2 * rh, C), F32, lambda i, p, nh=nh: l * 2 * nh + p[1] * nh + i,
                                 red.reshape(L * 2 * rh, C))))
    res = _multi_tiled("rs_add_chips", add, pos, groups, in_place=True)
    return [buf.reshape(red.shape) for buf, red in zip(res, reduced)]


def _adamw_math(w, g, m, v):
    m = ADAM_B1 * m + (1.0 - ADAM_B1) * g
    v = ADAM_B2 * v + (1.0 - ADAM_B2) * (g * g)
    m_hat = m / (1.0 - ADAM_B1 ** ADAM_STEP)
    v_hat = v / (1.0 - ADAM_B2 ** ADAM_STEP)
    return -ADAM_LR * (m_hat / (jnp.sqrt(v_hat) + ADAM_EPS) + ADAM_WD * w), m, v


def _adamw(w, g, m, v, lo=0, hi=None, into=None, deps=()):
    L, R, C = w.shape
    hi = L if hi is None else hi
    tr = _tile(R, max(16, ADAM_TILE_ELEMS // C))
    extra = (list(into) if into else []) + list(deps)
    n_alias = 4 if into else 0

    def body(w_ref, g_ref, m_ref, v_ref, *rest):
        go_ref, d_ref, mo_ref, vo_ref = rest[len(extra):]
        gv = g_ref[...]
        d, mn, vn = _adamw_math(w_ref[...], gv, m_ref[...], v_ref[...])
        go_ref[...] = gv
        d_ref[...] = d
        mo_ref[...] = mn
        vo_ref[...] = vn

    spec = pl.BlockSpec((None, tr, C), lambda l, i: (l + lo, i, 0))
    out = jax.ShapeDtypeStruct(w.shape, F32)
    return pl.pallas_call(body, grid=(hi - lo, R // tr), in_specs=[spec] * 4 + [ANY] * len(extra),
                          out_specs=[spec] * 4, out_shape=[out] * 4,
                          input_output_aliases={4 + k: k for k in range(n_alias)}, name="adamw",
                          compiler_params=_params(2))(w, g, m, v, *extra)


def _pack(parts):
    flat = jnp.concatenate([q.reshape(-1, LANES) for q in parts], axis=0)
    return jnp.pad(flat, ((0, -flat.shape[0] % ROW_TILE), (0, 0)))


def _unpack(flat, like):
    out, r = [], 0
    for q in like:
        n = q.size // LANES
        out.append(flat[r:r + n].reshape(q.shape))
        r += n
    return out


def _train_step(a):
    a = dict(a)
    L = a["ffn1_pre_g"].shape[0]
    x, y, c, _ = _place()
    chip = 2 * x + y
    pos = jnp.stack([chip, c, 2 * chip + c]).astype(jnp.int32)
    for name in TRANSPOSED:
        for pre in ("", "m_", "v_"):
            a[pre + name] = jnp.swapaxes(a[pre + name], 1, 2)
    big = [b[0] for b in BIG]
    gathered = big + ["conv_dw_k"]
    n_w, n_g = len(gathered), len(big)

    own = [None] * n_w
    W = {name: [None] * L for name in gathered}
    every = list(range(n_w))
    first, mixer, later = every[:3], every[3:8] + [n_g], every[8:n_g]
    rest = mixer + later

    def cast(i, deps):
        if i == n_g:
            taps = a["conv_dw_k"].reshape(L, CONV_TAPS, LANES)
            return _cast_layers("pad_conv_taps", taps, CONV_PAD, LANES, F32, pos, deps)
        name, _, _, _, rp, cp = BIG[i]
        return _cast_layers("cast_weight", a[name], rp, cp, BF, pos, deps)

    def gather_first(l, ids, tag, deps):
        return _split_start("gather_a%d%s" % (l, tag), _gather_plans(len(ids))[0], [own[i][l] for i in ids], deps)

    def gather_second(l, ids, tag, state, after):
        across, to_sibling = _gather_plans(len(ids))
        bufs = _split_wait("gather_a%d%s_done" % (l, tag), across, state[0], state[1], after)
        return _split_start("gather_b%d%s" % (l, tag), to_sibling, bufs, [])

    def gather_done(l, ids, tag, state, after):
        to_sibling = _gather_plans(len(ids))[1]
        bufs = _split_wait("gather_b%d%s_done" % (l, tag), to_sibling, state[0], state[1], after)
        for i, buf in zip(ids, bufs):
            W[gathered[i]][l] = buf

    for i in first:
        own[i] = cast(i, ())
    state = gather_first(0, first, "f", [])
    for i in rest:
        own[i] = cast(i, (state[2],))
    state = gather_second(0, first, "f", state, [own[i][0] for i in rest])
    gather_done(0, first, "f", state, [])

    parts = {"f": first, "m": mixer, "t": later}
    flying = {}

    def begin(l, part, deps):
        flying[l, part] = gather_first(l, parts[part], part, deps)
        return flying[l, part][2]

    def hand_on(l, part, after):
        flying[l, part] = gather_second(l, parts[part], part, flying[l, part], after)
        return flying[l, part][2]

    def arrive(l, part, after):
        gather_done(l, parts[part], part, flying.pop((l, part)), after)

    def hooks_of(l):
        nxt = l + 1 < L

        def after_ffn1(hv):
            tokens = []
            if l == 0:
                hand_on(0, "m", [hv])
            arrive(l, "m", [hv])
            if l == 0:
                tokens.append(begin(0, "t", [hv]))
            else:
                tokens.append(hand_on(l, "t", [hv]))
            if nxt:
                tokens.append(begin(l + 1, "f", tokens[-1:]))
            return tuple(tokens)

        def after_mix(hv):
            tokens = []
            if l == 0:
                hand_on(0, "t", [hv])
            arrive(l, "t", [hv])
            if nxt:
                tokens.append(hand_on(l + 1, "f", [hv]))
                tokens.append(begin(l + 1, "m", tokens[-1:]))
            return tuple(tokens)

        def after_ffn2(hv):
            tokens = []
            if nxt:
                arrive(l + 1, "f", [hv])
                tokens.append(hand_on(l + 1, "m", [hv]))
                tokens.append(begin(l + 1, "t", tokens[-1:]))
            return tuple(tokens)

        return {"ffn1": after_ffn1, "mix": after_mix, "ffn2": after_ffn2}

    small = [_layer_small(a, l) for l in range(L)]
    h, saved = a["x"][0], []
    deps = (begin(0, "m", []),)
    for l in range(L):
        h, sv = _layer_fwd(l, h, a["p"][l, 0], small[l], W, deps, hooks_of(l))
        saved.append(sv)
        deps = ()

    def loss_fn(yv, t):
        e = yv - t
        return e * (1.0 / D), jnp.sum(e * e, axis=0, keepdims=True)

    dh, lsum = _rowwise("loss", loss_fn, [(h, D, 0), (a["loss_target"][0], D, 0)], [], [(D, F32)], [(1, D)])
    loss = lax.psum(0.5 * jnp.sum(lsum) / D, ("x", "y", "c"))

    G = {name: [jax.ShapeDtypeStruct((N_CHIPS, rp, cp), BF)] * L for name, _, _, _, rp, cp in BIG}
    reduced = [lax.empty((L, rp, cp), F32) for _, _, _, _, rp, cp in BIG]
    small_grads = [None] * L
    whole = list(range(n_g))
    piece_a, piece_b, piece_c = whole[8:], whole[3:8], whole[:3]

    def pair_start(l, ids, tag, deps):
        grads = [G[big[i]][l] for i in ids]
        lands = [lax.empty((N_CHIPS, g.shape[1] // 2, g.shape[2]), BF) for g in grads]
        return _split_start("rs_pair%d%s" % (l, tag), _pair_plan(len(ids)), grads + lands, deps)

    def cross_start(l, ids, tag, state, after):
        n = len(ids)
        bufs = _split_wait("rs_pair%d%s_done" % (l, tag), _pair_plan(n), state[0], state[1], after)
        parts = _add_pair(bufs[:n], bufs[n:], pos)
        lands = [lax.empty((3,) + t.shape[1:], BF) for t in parts]
        return _split_start("rs_cross%d%s" % (l, tag), _cross_plan(n), parts + lands, [])

    def cross_finish(l, ids, tag, state, after, reduced):
        n = len(ids)
        bufs = _split_wait("rs_cross%d%s_done" % (l, tag), _cross_plan(n), state[0], state[1], after)
        reduced = list(reduced)
        for i, r in zip(ids, _add_chips(bufs[:n], bufs[n:], [reduced[i] for i in ids], l, pos)):
            reduced[i] = r
        return reduced

    def share_start(l, reduced):
        return _split_start("rs_share%d" % l, _share_plan(n_g, l), reduced, [])

    def share_done(l, state, after):
        return _split_wait("rs_share%d_done" % l, _share_plan(n_g, l), state[0], state[1], after)

    small_names = SMALL + ("conv_dw_k",)
    scatter, gather = _small_plans()
    totals = [None] * L

    def small_scatter(l, deps):
        packed = _pack([small_grads[l][name] for name in small_names])
        v3 = packed.reshape(8, packed.shape[0] // 8, LANES)
        return _split_start("small_scatter%d" % l, scatter, [v3, lax.empty((7,) + v3.shape[1:], F32)], deps)

    def small_gather(l, state, after):
        bufs = _split_wait("small_scatter%d_done" % l, scatter, state[0], state[1], after)
        return _split_start("small_gather%d" % l, gather, [_sum_small(bufs[0], bufs[1], pos)], [])

    def small_done(l, state, after):
        total = _split_wait("small_gather%d_done" % l, gather, state[0], state[1], after)[0]
        totals[l] = total.reshape(-1, LANES)

    st_pair = st_share = st_small = None
    for l in reversed(range(L)):
        deps = tuple(s[2] for s in (st_pair, st_share, st_small) if s is not None)
        box = {}

        def after_ffn2(dm, l=l, box=box, st_pair=st_pair, st_share=st_share, st_small=st_small):
            out = []
            if st_share is not None:
                box["reduced"] = share_done(l + 2, st_share, [dm])
            if st_small is not None:
                box["small"] = small_gather(l + 1, st_small, [dm])
                out.append(box["small"][2])
            if st_pair is not None:
                box["cross"] = cross_start(l + 1, whole, "", st_pair, [dm])
                out.append(box["cross"][2])
            if l == 0:
                box["pair_a"] = pair_start(0, piece_a, "a", [dm])
                out.append(box["pair_a"][2])
            return tuple(out)

        def after_mix(dm, box=box):
            box["cross_a"] = cross_start(0, piece_a, "a", box["pair_a"], [dm])
            box["pair_b"] = pair_start(0, piece_b, "b", [dm])
            return (box["cross_a"][2], box["pair_b"][2])

        hooks = {"ffn2": after_ffn2, "mix": after_mix} if l == 0 else {"ffn2": after_ffn2}
        dh, small_grads[l] = _layer_bwd(l, dh, saved[l], small[l], W, G, deps, hooks)
        if st_share is not None:
            reduced = box["reduced"]
        if "small" in box:
            small_done(l + 1, box["small"], [dh])
        st_share = None
        if "cross" in box:
            reduced = cross_finish(l + 1, whole, "", box["cross"], [dh], reduced)
            st_share = share_start(l + 1, reduced)
        st_small = small_scatter(l, [dh])
        st_pair = pair_start(l, whole, "", [st_small[2]]) if l else None
    grad_x = dh
    cross_b = cross_start(0, piece_b, "b", box["pair_b"], [st_small[2]])
    cross_c = cross_start(0, piece_c, "c", pair_start(0, piece_c, "c", [cross_b[2]]), [])
    if st_share is not None:
        reduced = share_done(1, st_share, [cross_c[2]])
    upper, token = {}, cross_c[2]
    for k, (name, red) in enumerate(zip(big, reduced)):
        if k == 2:
            st_small = small_gather(0, st_small, [token])
            token = st_small[2]
        if L > 1:
            upper[name] = _adamw(a[name], red, a["m_" + name], a["v_" + name], 1, L, deps=[token])
            token = upper[name][1]
    small_done(0, st_small, [token])
    per_layer = [_unpack(totals[l], [small_grads[l][name] for name in small_names]) for l in range(L)]
    summed = {name: jnp.stack([per_layer[l][k] for l in range(L)]) for k, name in enumerate(small_names)}
    done = [r[1] for r in upper.values()] + [summed[small_names[0]]]
    for ids, tag, state in ((piece_a, "a", box["cross_a"]), (piece_b, "b", cross_b), (piece_c, "c", cross_c)):
        reduced = cross_finish(0, ids, tag, state, done, reduced)
    reduced = share_done(0, share_start(0, reduced), [])
    big_grads = dict(zip(big, reduced))

    grads, deltas, new_m, new_v = {}, {}, {}, {}
    for name in big:
        res = _adamw(a[name], big_grads[name], a["m_" + name], a["v_" + name], 0, 1, upper.get(name))
        if name in TRANSPOSED:
            res = [jnp.swapaxes(r, 1, 2) for r in res]
        grads[name], deltas[name], new_m[name], new_v[name] = res
    taps = lax.dynamic_slice_in_dim(summed["conv_dw_k"], chip * LANES, LANES, axis=2)[:, :CONV_TAPS]
    grads["conv_dw_k"] = taps.reshape(a["conv_dw_k"].shape)
    for name in SMALL:
        grads[name] = summed[name].reshape(a[name].shape)
    shapes = [a[name] for name in small_names]
    res = _adamw(*[_pack([a[pre + name] if pre != "g" else grads[name] for name in small_names])[None]
                   for pre in ("", "g", "m_", "v_")])
    for dst, flat in zip((deltas, new_m, new_v), res[1:]):
        for name, val in zip(small_names, _unpack(flat[0], shapes)):
            dst[name] = val

    return (loss, grad_x[None], *[grads[n] for n in WEIGHTS], *[deltas[n] for n in WEIGHTS],
            *[new_m[n] for n in WEIGHTS], *[new_v[n] for n in WEIGHTS])


def kernel(x, p, ffn1_pre_g, ffn1_w_gate, ffn1_w_up, ffn1_w_down, ffn1_post_g, mix_pre_g, w_in, pool_w, pool_scale, w_pool_out, sgu_ln_g, sgu_ln_b, sgu_w_s, sgu_b_s, w_sgu_out, conv_dw_k, conv_dw_b, conv_ln_g, conv_ln_b, w_conv_out, w_out, mix_post_g, ffn2_pre_g, ffn2_w_gate, ffn2_w_up, ffn2_w_down, ffn2_post_g, ple_w_proj, ple_pre_g, ple_w_gate, ple_post_g, loss_target, m_ffn1_pre_g, m_ffn1_w_gate, m_ffn1_w_up, m_ffn1_w_down, m_ffn1_post_g, m_mix_pre_g, m_w_in, m_pool_w, m_pool_scale, m_w_pool_out, m_sgu_ln_g, m_sgu_ln_b, m_sgu_w_s, m_sgu_b_s, m_w_sgu_out, m_conv_dw_k, m_conv_dw_b, m_conv_ln_g, m_conv_ln_b, m_w_conv_out, m_w_out, m_mix_post_g, m_ffn2_pre_g, m_ffn2_w_gate, m_ffn2_w_up, m_ffn2_w_down, m_ffn2_post_g, m_ple_w_proj, m_ple_pre_g, m_ple_w_gate, m_ple_post_g, v_ffn1_pre_g, v_ffn1_w_gate, v_ffn1_w_up, v_ffn1_w_down, v_ffn1_post_g, v_mix_pre_g, v_w_in, v_pool_w, v_pool_scale, v_w_pool_out, v_sgu_ln_g, v_sgu_ln_b, v_sgu_w_s, v_sgu_b_s, v_w_sgu_out, v_conv_dw_k, v_conv_dw_b, v_conv_ln_g, v_conv_ln_b, v_w_conv_out, v_w_out, v_mix_post_g, v_ffn2_pre_g, v_ffn2_w_gate, v_ffn2_w_up, v_ffn2_w_down, v_ffn2_post_g, v_ple_w_proj, v_ple_pre_g, v_ple_w_gate, v_ple_post_g):
    return _train_step(dict(locals()))
```
